```python
import jax, jax.numpy as jnp
from jax import lax
import numpy as np

D_MODEL = 4096
BATCH = 8
SEQ = 4096
DEPTH = 2

D_MIX = D_MODEL
HEAD_DIM = 128
A_GROUPS = 8
A_CH = 128
A_WIDTH = A_GROUPS * A_CH
CHUNK = 128
B_HEADS = 12
B_WIDTH = B_HEADS * HEAD_DIM
C_HEADS = 12
C_NOPE = 128
C_ROPE = 64
C_VDIM = 128
C_WIDTH = C_HEADS * C_VDIM
Q_LORA = 768
KV_LORA = 512
ROPE_THETA = 10000.0
D_IN = 3 * A_WIDTH + 4 * B_WIDTH + Q_LORA + KV_LORA + C_ROPE + C_WIDTH
EPS = 1e-6
Q_BLOCK = 128

kernel_name = "hybrid_gmlp_stickbreak_mla_parallel_heads"


def rmsnorm(x, g):
    xf = x.astype(jnp.float32)
    xf = xf * lax.rsqrt(jnp.mean(xf * xf, axis=-1, keepdims=True) + EPS)
    return (xf * g.astype(jnp.float32)).astype(x.dtype)


def gated_norm(y, g, z):
    return rmsnorm(y, g) * jax.nn.silu(z)


def rope(x, cos, sin):
    half = x.shape[-1] // 2
    x1, x2 = x[..., :half], x[..., half:]
    out = jnp.concatenate([x1 * cos - x2 * sin, x2 * cos + x1 * sin], axis=-1)
    return out.astype(x.dtype)


def chunked_gmlp(u, v, g_v, w_s, b_s):
    bn, s, _ = u.shape
    u = jax.nn.gelu(u)
    v = rmsnorm(jax.nn.gelu(v).reshape(bn, s, A_GROUPS, A_CH), g_v)
    v = v.reshape(bn, s // CHUNK, CHUNK, A_GROUPS, A_CH)
    causal = jnp.tril(jnp.ones((CHUNK, CHUNK), dtype=bool))
    w = jnp.where(causal[None], w_s, 0.0).astype(v.dtype)
    sv = jnp.einsum('gts,bcsgd->bctgd', w, v) + b_s.T[None, None, :, :, None]
    return u * sv.reshape(bn, s, A_WIDTH)


def stick_breaking(q, k, v):
    bn, s, h, d = q.shape
    nb = s // Q_BLOCK
    scale = d ** -0.5
    kpos = jnp.arange(s)
    qb = q.reshape(bn, nb, Q_BLOCK, h, d).transpose(1, 0, 2, 3, 4)

    def block(args):
        qi, bi = args
        z = jnp.einsum('bthd,bshd->bhts', qi, k).astype(jnp.float32) * scale
        qpos = bi * Q_BLOCK + jnp.arange(Q_BLOCK)
        strict = kpos[None, :] < qpos[:, None]
        log_keep = jnp.where(strict, jax.nn.log_sigmoid(-z), 0.0)
        after = lax.cumsum(log_keep, axis=3, reverse=True) - log_keep
        a = jnp.where(strict, jnp.exp(jax.nn.log_sigmoid(z) + after), 0.0)
        return jnp.einsum('bhts,bshd->bthd', a.astype(v.dtype), v)

    out = lax.map(block, (qb, jnp.arange(nb)))
    return out.transpose(1, 0, 2, 3, 4).reshape(bn, s, h * d)


def mla(c_q, c_kv, k_rope, cos, sin, g_q, g_kv, w_uq, w_ukv):
    bn, s, _ = c_q.shape
    q = jnp.einsum('bsr,rn->bsn', rmsnorm(c_q, g_q), w_uq).reshape(bn, s, C_HEADS, C_NOPE + C_ROPE)
    q_nope = q[..., :C_NOPE]
    q_rope = rope(q[..., C_NOPE:], cos, sin)
    kv = jnp.einsum('bsr,rn->bsn', rmsnorm(c_kv, g_kv), w_ukv).reshape(bn, s, C_HEADS, C_NOPE + C_VDIM)
    k_nope, v = kv[..., :C_NOPE], kv[..., C_NOPE:]
    k_r = rope(k_rope, cos[:, :, 0], sin[:, :, 0])
    scale = (C_NOPE + C_ROPE) ** -0.5
    nb = s // Q_BLOCK
    kpos = jnp.arange(s)
    qn_b = q_nope.reshape(bn, nb, Q_BLOCK, C_HEADS, C_NOPE).transpose(1, 0, 2, 3, 4)
    qr_b = q_rope.reshape(bn, nb, Q_BLOCK, C_HEADS, C_ROPE).transpose(1, 0, 2, 3, 4)

    def block(args):
        qn, qr, bi = args
        z = (jnp.einsum('bthd,bshd->bhts', qn, k_nope)
             + jnp.einsum('bthr,bsr->bhts', qr, k_r)).astype(jnp.float32) * scale
        qpos = bi * Q_BLOCK + jnp.arange(Q_BLOCK)
        causal = kpos[None, :] <= qpos[:, None]
        p = jax.nn.softmax(jnp.where(causal, z, -jnp.inf), axis=-1)
        return jnp.einsum('bhts,bshd->bthd', p.astype(v.dtype), v)

    out = lax.map(block, (qn_b, qr_b, jnp.arange(nb)))
    return out.transpose(1, 0, 2, 3, 4).reshape(bn, s, C_WIDTH)


def hybrid_layer(x, cos, sin, g_pre, w_in, a_g_v, a_w_s, a_b_s,
                 c_g_q, c_g_kv, c_w_uq, c_w_ukv, g_out, w_out):
    bn, s, _ = x.shape
    h = rmsnorm(x, g_pre)
    proj = jnp.einsum('bsd,dn->bsn', h, w_in)
    sizes = [A_WIDTH, A_WIDTH, A_WIDTH, B_WIDTH, B_WIDTH, B_WIDTH, B_WIDTH,
             Q_LORA, KV_LORA, C_ROPE, C_WIDTH]
    offsets, acc = [], 0
    for sz in sizes[:-1]:
        acc += sz
        offsets.append(acc)
    u_a, v_a, z_a, q_b, k_b, v_b, z_b, cq, ckv, kr, z_c = jnp.split(proj, offsets, axis=-1)

    y_a = chunked_gmlp(u_a, v_a, a_g_v, a_w_s, a_b_s)
    y_b = stick_breaking(q_b.reshape(bn, s, B_HEADS, HEAD_DIM),
                         k_b.reshape(bn, s, B_HEADS, HEAD_DIM),
                         v_b.reshape(bn, s, B_HEADS, HEAD_DIM))
    y_c = mla(cq, ckv, kr, cos, sin, c_g_q, c_g_kv, c_w_uq, c_w_ukv)

    y = jnp.concatenate([
        gated_norm(y_a, g_out[:A_WIDTH], z_a),
        gated_norm(y_b, g_out[A_WIDTH:A_WIDTH + B_WIDTH], z_b),
        gated_norm(y_c, g_out[A_WIDTH + B_WIDTH:], z_c),
    ], axis=-1)
    return x + jnp.einsum('bsn,nd->bsd', y, w_out)


def _fwd_setup_inputs(seed: int = 0) -> dict:
    key = jax.random.key(seed)
    ks = jax.random.split(key, 16)
    f32 = jnp.float32

    def nrm(k, shape, scale):
        return jax.random.normal(k, shape, f32) * scale

    x = jax.random.normal(ks[0], (BATCH, SEQ, D_MODEL), f32)
    offset = jax.random.randint(ks[1], (BATCH, 1), 0, 1024, dtype=jnp.int32)
    positions = (offset + jnp.arange(SEQ, dtype=jnp.int32)[None, :]).astype(jnp.int32)
    return {
        "x": x,
        "positions": positions,
        "g_pre": 1.0 + nrm(ks[2], (DEPTH, D_MODEL), 0.02),
        "w_in": nrm(ks[3], (DEPTH, D_MODEL, D_IN), D_MODEL ** -0.5),
        "a_g_v": 1.0 + nrm(ks[4], (DEPTH, A_GROUPS, A_CH), 0.02),
        "a_w_s": nrm(ks[5], (DEPTH, A_GROUPS, CHUNK, CHUNK), CHUNK ** -0.5),
        "a_b_s": 1.0 + nrm(ks[6], (DEPTH, A_GROUPS, CHUNK), 0.02),
        "c_g_q": 1.0 + nrm(ks[7], (DEPTH, Q_LORA), 0.02),
        "c_g_kv": 1.0 + nrm(ks[8], (DEPTH, KV_LORA), 0.02),
        "c_w_uq": nrm(ks[9], (DEPTH, Q_LORA, C_HEADS * (C_NOPE + C_ROPE)), Q_LORA ** -0.5),
        "c_w_ukv": nrm(ks[10], (DEPTH, KV_LORA, C_HEADS * (C_NOPE + C_VDIM)), KV_LORA ** -0.5),
        "g_out": 1.0 + nrm(ks[11], (DEPTH, D_MIX), 0.02),
        "w_out": nrm(ks[12], (DEPTH, D_MIX, D_MODEL), D_MIX ** -0.5),
        "g_final": 1.0 + nrm(ks[13], (D_MODEL,), 0.02),
    }


def _fwd_reference(x, positions, g_pre, w_in, a_g_v, a_w_s, a_b_s, c_g_q, c_g_kv,
              c_w_uq, c_w_ukv, g_out, w_out, g_final):
    inv_freq = 1.0 / (ROPE_THETA ** (jnp.arange(0, C_ROPE, 2, dtype=jnp.float32) / C_ROPE))
    ang = positions.astype(jnp.float32)[..., None] * inv_freq
    cos = jnp.cos(ang)[:, :, None, :]
    sin = jnp.sin(ang)[:, :, None, :]
    h = x
    for l in range(DEPTH):
        h = hybrid_layer(h, cos, sin, g_pre[l], w_in[l], a_g_v[l], a_w_s[l], a_b_s[l],
                         c_g_q[l], c_g_kv[l], c_w_uq[l], c_w_ukv[l], g_out[l], w_out[l])
    return rmsnorm(h, g_final)


import jax as _jax
import jax.numpy as _jnp

TWIN_FORMAT = 'train_step'
FWD_PARAMS = ['x', 'positions', 'g_pre', 'w_in', 'a_g_v', 'a_w_s', 'a_b_s', 'c_g_q', 'c_g_kv', 'c_w_uq', 'c_w_ukv', 'g_out', 'w_out', 'g_final']
TWIN_WEIGHTS = ['g_pre', 'w_in', 'a_g_v', 'a_w_s', 'a_b_s', 'c_g_q', 'c_g_kv', 'c_w_uq', 'c_w_ukv', 'g_out', 'w_out', 'g_final']
TWIN_DIFF_INPUT = 'x'
TWIN_INPUTS = ['x', 'positions', 'g_pre', 'w_in', 'a_g_v', 'a_w_s', 'a_b_s', 'c_g_q', 'c_g_kv', 'c_w_uq', 'c_w_ukv', 'g_out', 'w_out', 'g_final', 'loss_target', 'm_g_pre', 'm_w_in', 'm_a_g_v', 'm_a_w_s', 'm_a_b_s', 'm_c_g_q', 'm_c_g_kv', 'm_c_w_uq', 'm_c_w_ukv', 'm_g_out', 'm_w_out', 'm_g_final', 'v_g_pre', 'v_w_in', 'v_a_g_v', 'v_a_w_s', 'v_a_b_s', 'v_c_g_q', 'v_c_g_kv', 'v_c_w_uq', 'v_c_w_ukv', 'v_g_out', 'v_w_out', 'v_g_final']
TWIN_OUTPUTS = ['loss', 'grad_x', 'grad_g_pre', 'grad_w_in', 'grad_a_g_v', 'grad_a_w_s', 'grad_a_b_s', 'grad_c_g_q', 'grad_c_g_kv', 'grad_c_w_uq', 'grad_c_w_ukv', 'grad_g_out', 'grad_w_out', 'grad_g_final', 'delta_g_pre', 'delta_w_in', 'delta_a_g_v', 'delta_a_w_s', 'delta_a_b_s', 'delta_c_g_q', 'delta_c_g_kv', 'delta_c_w_uq', 'delta_c_w_ukv', 'delta_g_out', 'delta_w_out', 'delta_g_final', 'new_m_g_pre', 'new_m_w_in', 'new_m_a_g_v', 'new_m_a_w_s', 'new_m_a_b_s', 'new_m_c_g_q', 'new_m_c_g_kv', 'new_m_c_w_uq', 'new_m_c_w_ukv', 'new_m_g_out', 'new_m_w_out', 'new_m_g_final', 'new_v_g_pre', 'new_v_w_in', 'new_v_a_g_v', 'new_v_a_w_s', 'new_v_a_b_s', 'new_v_c_g_q', 'new_v_c_g_kv', 'new_v_c_w_uq', 'new_v_c_w_ukv', 'new_v_g_out', 'new_v_w_out', 'new_v_g_final']
TWIN_LEAF_KINDS = {'loss': 'loss', 'grad_x': 'grad_x', 'grad_g_pre': 'grad_w', 'grad_w_in': 'grad_w', 'grad_a_g_v': 'grad_w', 'grad_a_w_s': 'grad_w', 'grad_a_b_s': 'grad_w', 'grad_c_g_q': 'grad_w', 'grad_c_g_kv': 'grad_w', 'grad_c_w_uq': 'grad_w', 'grad_c_w_ukv': 'grad_w', 'grad_g_out': 'grad_w', 'grad_w_out': 'grad_w', 'grad_g_final': 'grad_w', 'delta_g_pre': 'delta_w', 'delta_w_in': 'delta_w', 'delta_a_g_v': 'delta_w', 'delta_a_w_s': 'delta_w', 'delta_a_b_s': 'delta_w', 'delta_c_g_q': 'delta_w', 'delta_c_g_kv': 'delta_w', 'delta_c_w_uq': 'delta_w', 'delta_c_w_ukv': 'delta_w', 'delta_g_out': 'delta_w', 'delta_w_out': 'delta_w', 'delta_g_final': 'delta_w', 'new_m_g_pre': 'new_m', 'new_m_w_in': 'new_m', 'new_m_a_g_v': 'new_m', 'new_m_a_w_s': 'new_m', 'new_m_a_b_s': 'new_m', 'new_m_c_g_q': 'new_m', 'new_m_c_g_kv': 'new_m', 'new_m_c_w_uq': 'new_m', 'new_m_c_w_ukv': 'new_m', 'new_m_g_out': 'new_m', 'new_m_w_out': 'new_m', 'new_m_g_final': 'new_m', 'new_v_g_pre': 'new_v', 'new_v_w_in': 'new_v', 'new_v_a_g_v': 'new_v', 'new_v_a_w_s': 'new_v', 'new_v_a_b_s': 'new_v', 'new_v_c_g_q': 'new_v', 'new_v_c_g_kv': 'new_v', 'new_v_c_w_uq': 'new_v', 'new_v_c_w_ukv': 'new_v', 'new_v_g_out': 'new_v', 'new_v_w_out': 'new_v', 'new_v_g_final': 'new_v'}


def _forward(args):
    return _fwd_reference(*[args[k] for k in FWD_PARAMS])


def _output_shape():
    out = _jax.eval_shape(lambda: _forward(_fwd_setup_inputs(0)))
    return out.shape, out.dtype

N_MICROBATCH = 1
ADAM_LR = 0.001
ADAM_B1 = 0.9
ADAM_B2 = 0.999
ADAM_EPS = 1e-08
ADAM_WD = 0.01
ADAM_STEP = 10
PER_EXAMPLE_BATCH_AXIS = {'x': 0, 'positions': 0, 'loss_target': 0}
SHARED_INPUTS = []
_WEIGHT_DTYPES = {'g_pre': _jnp.float32, 'w_in': _jnp.float32, 'a_g_v': _jnp.float32, 'a_w_s': _jnp.float32, 'a_b_s': _jnp.float32, 'c_g_q': _jnp.float32, 'c_g_kv': _jnp.float32, 'c_w_uq': _jnp.float32, 'c_w_ukv': _jnp.float32, 'g_out': _jnp.float32, 'w_out': _jnp.float32, 'g_final': _jnp.float32}
MOMENT_SCALE = {'g_pre': 4.009806e-02, 'w_in': 2.343797e-02, 'a_g_v': 1.303840e-02, 'a_w_s': 1.320393e-02, 'a_b_s': 1.983975e-02, 'c_g_q': 2.984339e-02, 'c_g_kv': 5.953764e-02, 'c_w_uq': 1.777258e-02, 'c_w_ukv': 2.234779e-02, 'g_out': 2.398316e-02, 'w_out': 2.401337e-02, 'g_final': 7.982145e+00}


def _to_microbatches(a, axis):
    t = _jnp.moveaxis(a, axis, 0)
    t = t.reshape((N_MICROBATCH, t.shape[0] // N_MICROBATCH) + t.shape[1:])
    return _jnp.moveaxis(t, 1, axis + 1)


def setup_inputs(seed: int = 0) -> dict:
    inp = _fwd_setup_inputs(seed)
    key = _jax.random.fold_in(_jax.random.key(seed), 7919)
    shape, _ = _output_shape()
    out = dict(inp)
    out["loss_target"] = _jax.random.normal(_jax.random.fold_in(key, 0), shape, _jnp.float32)
    for i, name in enumerate(TWIN_WEIGHTS):
        w = inp[name].astype(_jnp.float32)
        if MOMENT_SCALE is None:
            s = _jnp.sqrt(_jnp.mean(_jnp.square(w)) + 1e-30)
        else:
            s = MOMENT_SCALE[name]
        km, kv = _jax.random.split(_jax.random.fold_in(key, i + 1))
        out[name] = w
        out["m_" + name] = s * _jax.random.normal(km, w.shape, _jnp.float32)
        out["v_" + name] = (s * s) * _jax.random.uniform(kv, w.shape, _jnp.float32, 0.5, 1.5)
    if N_MICROBATCH > 1:
        for name, axis in PER_EXAMPLE_BATCH_AXIS.items():
            out[name] = _to_microbatches(out[name], axis)
    return {'x': out['x'], 'positions': out['positions'], 'g_pre': out['g_pre'], 'w_in': out['w_in'], 'a_g_v': out['a_g_v'], 'a_w_s': out['a_w_s'], 'a_b_s': out['a_b_s'], 'c_g_q': out['c_g_q'], 'c_g_kv': out['c_g_kv'], 'c_w_uq': out['c_w_uq'], 'c_w_ukv': out['c_w_ukv'], 'g_out': out['g_out'], 'w_out': out['w_out'], 'g_final': out['g_final'], 'loss_target': out['loss_target'], 'm_g_pre': out['m_g_pre'], 'm_w_in': out['m_w_in'], 'm_a_g_v': out['m_a_g_v'], 'm_a_w_s': out['m_a_w_s'], 'm_a_b_s': out['m_a_b_s'], 'm_c_g_q': out['m_c_g_q'], 'm_c_g_kv': out['m_c_g_kv'], 'm_c_w_uq': out['m_c_w_uq'], 'm_c_w_ukv': out['m_c_w_ukv'], 'm_g_out': out['m_g_out'], 'm_w_out': out['m_w_out'], 'm_g_final': out['m_g_final'], 'v_g_pre': out['v_g_pre'], 'v_w_in': out['v_w_in'], 'v_a_g_v': out['v_a_g_v'], 'v_a_w_s': out['v_a_w_s'], 'v_a_b_s': out['v_a_b_s'], 'v_c_g_q': out['v_c_g_q'], 'v_c_g_kv': out['v_c_g_kv'], 'v_c_w_uq': out['v_c_w_uq'], 'v_c_w_ukv': out['v_c_w_ukv'], 'v_g_out': out['v_g_out'], 'v_w_out': out['v_w_out'], 'v_g_final': out['v_g_final']}


def _loss(weights, diff, rest, loss_target):
    with _jax.named_scope("forward"):
        args = {**rest, TWIN_DIFF_INPUT: diff, **{k: w.astype(_WEIGHT_DTYPES[k]) for k, w in weights.items()}}
        y = _forward(args)
    with _jax.named_scope("loss_head"):
        err = _jnp.square(y.astype(_jnp.float32) - loss_target)
        return 0.5 * _jnp.sum(_jnp.mean(err, axis=-1)) if err.ndim else 0.5 * err


def _adamw(w, g, m, v):
    m = ADAM_B1 * m + (1.0 - ADAM_B1) * g
    v = ADAM_B2 * v + (1.0 - ADAM_B2) * _jnp.square(g)
    m_hat = m / (1.0 - ADAM_B1 ** ADAM_STEP)
    v_hat = v / (1.0 - ADAM_B2 ** ADAM_STEP)
    delta = -ADAM_LR * (m_hat / (_jnp.sqrt(v_hat) + ADAM_EPS) + ADAM_WD * w)
    return delta, m, v


def reference(x, positions, g_pre, w_in, a_g_v, a_w_s, a_b_s, c_g_q, c_g_kv, c_w_uq, c_w_ukv, g_out, w_out, g_final, loss_target, m_g_pre, m_w_in, m_a_g_v, m_a_w_s, m_a_b_s, m_c_g_q, m_c_g_kv, m_c_w_uq, m_c_w_ukv, m_g_out, m_w_out, m_g_final, v_g_pre, v_w_in, v_a_g_v, v_a_w_s, v_a_b_s, v_c_g_q, v_c_g_kv, v_c_w_uq, v_c_w_ukv, v_g_out, v_w_out, v_g_final):
    given = dict(x=x, positions=positions, g_pre=g_pre, w_in=w_in, a_g_v=a_g_v, a_w_s=a_w_s, a_b_s=a_b_s, c_g_q=c_g_q, c_g_kv=c_g_kv, c_w_uq=c_w_uq, c_w_ukv=c_w_ukv, g_out=g_out, w_out=w_out, g_final=g_final, loss_target=loss_target, m_g_pre=m_g_pre, m_w_in=m_w_in, m_a_g_v=m_a_g_v, m_a_w_s=m_a_w_s, m_a_b_s=m_a_b_s, m_c_g_q=m_c_g_q, m_c_g_kv=m_c_g_kv, m_c_w_uq=m_c_w_uq, m_c_w_ukv=m_c_w_ukv, m_g_out=m_g_out, m_w_out=m_w_out, m_g_final=m_g_final, v_g_pre=v_g_pre, v_w_in=v_w_in, v_a_g_v=v_a_g_v, v_a_w_s=v_a_w_s, v_a_b_s=v_a_b_s, v_c_g_q=v_c_g_q, v_c_g_kv=v_c_g_kv, v_c_w_uq=v_c_w_uq, v_c_w_ukv=v_c_w_ukv, v_g_out=v_g_out, v_w_out=v_w_out, v_g_final=v_g_final)
    weights = {n: given[n] for n in TWIN_WEIGHTS}
    shared = {n: given[n] for n in SHARED_INPUTS}
    per_example = {n: given[n] for n in ['x', 'positions']}
    grad_fn = _jax.value_and_grad(_loss, argnums=(0, 1))

    def one_microbatch(ex, loss_target):
        ex = dict(ex)
        diff = ex.pop(TWIN_DIFF_INPUT)
        return grad_fn(weights, diff, {**shared, **ex}, loss_target)

    if N_MICROBATCH == 1:
        loss, (grad_w, grad_x) = one_microbatch(per_example, given["loss_target"])
    else:
        def body(carry, xs):
            loss_sum, grad_sum = carry
            l_k, (gw_k, gx_k) = one_microbatch(xs[0], xs[1])
            with _jax.named_scope("update"):
                return (loss_sum + l_k, _jax.tree.map(_jnp.add, grad_sum, gw_k)), gx_k

        init = (_jnp.zeros((), _jnp.float32), _jax.tree.map(_jnp.zeros_like, weights))
        (loss, grad_w), grad_x = _jax.lax.scan(body, init, (per_example, given["loss_target"]))
    with _jax.named_scope("update"):
        delta_w, new_m, new_v = {}, {}, {}
        for n in TWIN_WEIGHTS:
            delta_w[n], new_m[n], new_v[n] = _adamw(weights[n], grad_w[n], given["m_" + n], given["v_" + n])
    return (loss, grad_x, *[grad_w[n] for n in TWIN_WEIGHTS], *[delta_w[n] for n in TWIN_WEIGHTS],
            *[new_m[n] for n in TWIN_WEIGHTS], *[new_v[n] for n in TWIN_WEIGHTS])
```

```python
import functools

import numpy as np
import jax
import jax.numpy as jnp
from jax import lax
from jax.experimental import pallas as pl
from jax.experimental.pallas import tpu as pltpu

NDEV = 8
MESH_AXES = ("x", "y", "c")
LANE = 128
ROPE = 64
EPS = 1e-6
ROPE_THETA = 10000.0
ADAM_LR, ADAM_B1, ADAM_B2, ADAM_EPS, ADAM_WD, ADAM_STEP = 0.001, 0.9, 0.999, 1e-08, 0.01, 10
VMEM_LIMIT = 48 * 1024 * 1024
ADAM_TILE_BYTES = 512 * 1024
SMALL_ROWS = 256
F32, BF16 = jnp.float32, jnp.bfloat16
SMALL = ("g_pre", "a_g_v", "a_w_s", "a_b_s", "c_g_q", "c_g_kv", "g_out", "g_final")


def _tile(dim, cap, mult=LANE):
    if dim <= cap:
        return dim
    t = (cap // mult) * mult
    while t >= mult:
        if dim % t == 0:
            return t
        t -= mult
    return dim


def _dot_nt(a, b):
    return lax.dot_general(a, b, (((1,), (1,)), ((), ())), preferred_element_type=F32)


def _dot_tn(a, b):
    return lax.dot_general(a, b, (((0,), (0,)), ((), ())), preferred_element_type=F32)


def _dot(a, b):
    return jnp.dot(a, b, preferred_element_type=F32)


def _matmul(a, b, mode, out_dtype, name, add=None, tm=512, tn=1024, tk=512):
    if mode == "tn":
        (K, M), (K2, N) = a.shape, b.shape
    elif mode == "nt":
        (M, K), (N, K2) = a.shape, b.shape
    else:
        (M, K), (K2, N) = a.shape, b.shape
    assert K == K2, (a.shape, b.shape, mode)
    tm, tn, tk = _tile(M, tm), _tile(N, tn), _tile(K, tk)
    nk = K // tk
    a_spec = pl.BlockSpec((tk, tm), lambda i, j, k: (k, i)) if mode == "tn" else pl.BlockSpec((tm, tk), lambda i, j, k: (i, k))
    b_spec = pl.BlockSpec((tn, tk), lambda i, j, k: (j, k)) if mode == "nt" else pl.BlockSpec((tk, tn), lambda i, j, k: (k, j))
    dot = {"nn": _dot, "nt": _dot_nt, "tn": _dot_tn}[mode]
    has_add = add is not None

    def body(*refs):
        a_ref, b_ref = refs[0], refs[1]
        o_ref, acc = refs[-2], refs[-1]
        k = pl.program_id(2)

        @pl.when(k == 0)
        def _():
            acc[...] = jnp.zeros_like(acc)

        acc[...] += dot(a_ref[...].astype(BF16), b_ref[...].astype(BF16))

        @pl.when(k == nk - 1)
        def _():
            r = acc[...]
            if has_add:
                r = r + refs[2][...]
            o_ref[...] = r.astype(o_ref.dtype)

    in_specs = [a_spec, b_spec]
    args = [a, b]
    if has_add:
        in_specs.append(pl.BlockSpec((tm, tn), lambda i, j, k: (i, j)))
        args.append(add)
    return pl.pallas_call(
        body, name=name, grid=(M // tm, N // tn, nk),
        out_shape=jax.ShapeDtypeStruct((M, N), out_dtype),
        in_specs=in_specs, out_specs=pl.BlockSpec((tm, tn), lambda i, j, k: (i, j)),
        scratch_shapes=[pltpu.VMEM((tm, tn), F32)],
        compiler_params=pltpu.CompilerParams(dimension_semantics=("parallel", "parallel", "arbitrary"),
                                             vmem_limit_bytes=VMEM_LIMIT),
    )(*args)


def _row_specs(views, tile):
    return [pl.BlockSpec((tile, w), functools.partial(lambda i, cb: (i, cb), cb=cb)) for (_, w, cb) in views]


def _full_specs(arrs):
    return [pl.BlockSpec(p.shape, functools.partial(lambda i, nd: (0,) * nd, nd=p.ndim)) for p in arrs]


def _rowwise(fn, rows, aux, params, consts, outs, tile, name):
    S = rows[0][0].shape[0]
    nr, na, npar, nc = len(rows), len(aux), len(params), len(consts)

    def body(*refs):
        ins = [r[...].astype(F32) for r in refs[:nr + na]]
        small = [r[...] for r in refs[nr + na:nr + na + npar + nc]]
        res = fn(*ins, *small)
        for o_ref, r in zip(refs[nr + na + npar + nc:], res):
            o_ref[...] = r.astype(o_ref.dtype)

    return pl.pallas_call(
        body, name=name, grid=(S // tile,),
        out_shape=[jax.ShapeDtypeStruct((S, w), dt) for (w, dt) in outs],
        in_specs=_row_specs(rows + aux, tile) + _full_specs(params + consts),
        out_specs=[pl.BlockSpec((tile, w), lambda i: (i, 0)) for (w, _) in outs],
        compiler_params=pltpu.CompilerParams(dimension_semantics=("parallel",), vmem_limit_bytes=VMEM_LIMIT),
    )(*[v[0] for v in rows + aux], *params, *consts)


def _rowwise_vjp(fn, rows, aux, params, consts, cots, grad_dtypes, tile, name, primal=()):
    S = rows[0][0].shape[0]
    nr, na, npar, nc, nct, npr = len(rows), len(aux), len(params), len(consts), len(cots), len(primal)

    def body(*refs):
        n_in = nr + na + npar + nc + nct
        rv = [r[...].astype(F32) for r in refs[:nr]]
        av = [r[...].astype(F32) for r in refs[nr:nr + na]]
        pv = [r[...] for r in refs[nr + na:nr + na + npar]]
        cv = [r[...] for r in refs[nr + na + npar:nr + na + npar + nc]]
        ct = tuple(r[...].astype(F32) for r in refs[nr + na + npar + nc:n_in])
        res, vjp = jax.vjp(lambda *rp: tuple(fn(*rp[:nr], *av, *rp[nr:], *cv)), *rv, *pv)
        grads = vjp(ct)
        g_refs = refs[n_in:n_in + nr]
        p_refs = refs[n_in + nr:n_in + nr + npar]
        o_refs = refs[n_in + nr + npar:]
        for g_ref, g in zip(g_refs, grads[:nr]):
            g_ref[...] = g.astype(g_ref.dtype)

        @pl.when(pl.program_id(0) == 0)
        def _():
            for p_ref in p_refs:
                p_ref[...] = jnp.zeros_like(p_ref)

        for p_ref, g in zip(p_refs, grads[nr:]):
            p_ref[...] += g
        for o_ref, r in zip(o_refs, res[:npr]):
            o_ref[...] = r.astype(o_ref.dtype)

    out_shape = ([jax.ShapeDtypeStruct((S, w), dt) for (_, w, _), dt in zip(rows, grad_dtypes)]
                 + [jax.ShapeDtypeStruct(p.shape, F32) for p in params]
                 + [jax.ShapeDtypeStruct((S, w), dt) for (w, dt) in primal])
    out_specs = ([pl.BlockSpec((tile, w), lambda i: (i, 0)) for (_, w, _) in rows] + _full_specs(params)
                 + [pl.BlockSpec((tile, w), lambda i: (i, 0)) for (w, _) in primal])
    res = pl.pallas_call(
        body, name=name, grid=(S // tile,), out_shape=out_shape,
        in_specs=_row_specs(rows + aux, tile) + _full_specs(params + consts) + _row_specs(cots, tile),
        out_specs=out_specs,
        compiler_params=pltpu.CompilerParams(dimension_semantics=("arbitrary",), vmem_limit_bytes=VMEM_LIMIT),
    )(*[v[0] for v in rows + aux], *params, *consts, *[v[0] for v in cots])
    return res[:nr], res[nr:nr + npar], res[nr + npar:]


@jax.custom_vjp
def _mm(a, b):
    return _dot(a.astype(BF16), b.astype(BF16))


def _mm_fwd(a, b):
    return _mm(a, b), (a, b)


def _mm_bwd(res, ct):
    a, b = res
    ctb = ct.astype(BF16)
    return _dot_nt(ctb, b.astype(BF16)), _dot_tn(a.astype(BF16), ctb)


_mm.defvjp(_mm_fwd, _mm_bwd)


def _rms(x, g):
    return x * lax.rsqrt(jnp.mean(x * x, axis=-1, keepdims=True) + EPS) * g


def _f_pre(x, g):
    return (_rms(x, g),)


def _f_pre_res(x, g):
    return _rms(x, g), x


def _f_gate(y, z, g):
    return (_rms(y, g) * jax.nn.silu(z),)


def _f_gmlp(u, v, z, g_v, w_s, b_s, g_o):
    groups = w_s.shape[0]
    u, v = jax.nn.gelu(u), jax.nn.gelu(v)
    t_idx = lax.broadcasted_iota(jnp.int32, (LANE, LANE), 0)
    s_idx = lax.broadcasted_iota(jnp.int32, (LANE, LANE), 1)
    ys = []
    for g in range(groups):
        sl = slice(g * LANE, (g + 1) * LANE)
        vn = _rms(v[:, sl], g_v[:, sl])
        w = jnp.where(s_idx <= t_idx, w_s[g], 0.0)
        ys.append(u[:, sl] * (_mm(w, vn) + b_s[g]))
    return (_rms(jnp.concatenate(ys, axis=1), g_o) * jax.nn.silu(z),)


def _rope(x, cos2, sin2, rot):
    return x * cos2 + _mm(x, rot) * sin2


def _f_cpre(cq, ckv, kr, cos2, sin2, g_q, g_kv, rot):
    return _rms(cq, g_q), _rms(ckv, g_kv), _rope(kr, cos2, sin2, rot)


def _f_crope(q, kv, krr, cos2, sin2, rot):
    heads = q.shape[1] // (2 * LANE)
    qs, ks, vs = [], [], []
    for h in range(heads):
        lo, mid, hi = 2 * h * LANE, (2 * h + 1) * LANE, (2 * h + 2) * LANE
        qs += [q[:, lo:mid], _rope(q[:, mid:hi], cos2, sin2, rot)]
        ks += [kv[:, lo:mid], krr]
        vs += [kv[:, mid:hi]]
    return jnp.concatenate(qs, axis=1), jnp.concatenate(ks, axis=1), jnp.concatenate(vs, axis=1)


def _f_final(h, target, g):
    err = _rms(h, g) - target
    return (0.5 * jnp.mean(err * err, axis=-1, keepdims=True),)


def _rope_matrix():
    r = np.zeros((LANE, LANE), np.float32)
    half = ROPE // 2
    for i in range(half):
        r[i + half, i] = -1.0
        r[i, i + half] = 1.0
    return jnp.asarray(r)


def _head_spec(view, rows, n_rows_block):
    _, cb0, w = view
    if n_rows_block:
        return pl.BlockSpec((rows, w), functools.partial(lambda h, i, cb0: (i, cb0 + h), cb0=cb0))
    return pl.BlockSpec((rows, w), functools.partial(lambda h, i, cb0: (0, cb0 + h), cb0=cb0))


def _stat_spec(tq):
    return pl.BlockSpec((1, tq, 1), lambda h, i: (h, i, 0))


def _softplus(z):
    return jnp.maximum(z, 0.0) + jnp.log(1.0 + jnp.exp(-jnp.abs(z)))


def _cumsum_mm(x, m01):
    hi = x.astype(BF16)
    lo = (x - hi.astype(F32)).astype(BF16)
    return _dot(hi, m01) + _dot(lo, m01)


def _attn_call(body, name, heads, S, tq, ins, in_blocked, outs, out_blocked, scratch, stats_in=0, stats_out=0):
    in_specs = [_head_spec(v, tq if blk else S, blk) for v, blk in zip(ins[:len(ins) - stats_in], in_blocked)]
    in_specs += [_stat_spec(tq)] * stats_in
    out_specs = [_head_spec((None, 0, w), tq if blk else S, blk) for (w, _), blk in zip(outs, out_blocked)]
    out_specs += [_stat_spec(tq)] * stats_out
    out_shape = [jax.ShapeDtypeStruct((S, heads * w), dt) for (w, dt) in outs]
    out_shape += [jax.ShapeDtypeStruct((heads, S, 1), F32)] * stats_out
    args = [v[0] for v in ins[:len(ins) - stats_in]] + list(ins[len(ins) - stats_in:])
    return pl.pallas_call(
        body, name=name, grid=(heads, S // tq), out_shape=out_shape, in_specs=in_specs, out_specs=out_specs,
        scratch_shapes=scratch,
        compiler_params=pltpu.CompilerParams(dimension_semantics=("arbitrary", "arbitrary"), vmem_limit_bytes=VMEM_LIMIT),
    )(*args)


def _softmax_fwd(q, k, v, heads, scale, name, tq, bk):
    S, dv = q[0].shape[0], v[2]

    def body(q_ref, k_ref, v_ref, o_ref, lse_ref):
        qi = pl.program_id(1)
        qv = q_ref[...]
        row = qi * tq + lax.broadcasted_iota(jnp.int32, (tq, bk), 0)
        col0 = lax.broadcasted_iota(jnp.int32, (tq, bk), 1)

        def step(kb, carry):
            m, l, acc = carry
            sl = pl.ds(pl.multiple_of(kb * bk, bk), bk)
            s = _dot_nt(qv, k_ref[sl, :]) * scale
            s = jnp.where(kb * bk + col0 <= row, s, -1e30)
            m_new = jnp.maximum(m, jnp.max(s, axis=1, keepdims=True))
            p = jnp.exp(s - m_new)
            alpha = jnp.exp(m - m_new)
            l = alpha * l + jnp.sum(p, axis=1, keepdims=True)
            acc = alpha * acc + _dot(p.astype(BF16), v_ref[sl, :])
            return m_new, l, acc

        n_kb = (qi * tq + tq + bk - 1) // bk
        m, l, acc = lax.fori_loop(0, n_kb, step, (jnp.full((tq, 1), -1e30, F32), jnp.zeros((tq, 1), F32),
                                                  jnp.zeros((tq, dv), F32)))
        o_ref[...] = (acc / l).astype(o_ref.dtype)
        lse_ref[0] = m + jnp.log(l)

    return _attn_call(body, name, heads, S, tq, [q, k, v], [1, 0, 0], [(dv, BF16)], [1], [], stats_out=1)


def _softmax_bwd(q, k, v, o, do, lse, heads, scale, name, tq, bk):
    S, dq_w, dv = q[0].shape[0], q[2], v[2]
    nq = S // tq

    def body(q_ref, k_ref, v_ref, o_ref, do_ref, lse_ref, dq_ref, dk_ref, dv_ref, dk_acc, dv_acc):
        qi = pl.program_id(1)

        @pl.when(qi == 0)
        def _():
            dk_acc[...] = jnp.zeros_like(dk_acc)
            dv_acc[...] = jnp.zeros_like(dv_acc)

        qv, dov = q_ref[...], do_ref[...]
        delta = jnp.sum(dov.astype(F32) * o_ref[...].astype(F32), axis=1, keepdims=True)
        lse_v = lse_ref[0]
        row = qi * tq + lax.broadcasted_iota(jnp.int32, (tq, bk), 0)
        col0 = lax.broadcasted_iota(jnp.int32, (tq, bk), 1)

        def step(kb, dq):
            sl = pl.ds(pl.multiple_of(kb * bk, bk), bk)
            ks, vs = k_ref[sl, :], v_ref[sl, :]
            s = _dot_nt(qv, ks) * scale
            p = jnp.where(kb * bk + col0 <= row, jnp.exp(s - lse_v), 0.0)
            ds = (p * (_dot_nt(dov, vs) - delta) * scale).astype(BF16)
            dk_acc[sl, :] += _dot_tn(ds, qv)
            dv_acc[sl, :] += _dot_tn(p.astype(BF16), dov)
            return dq + _dot(ds, ks)

        n_kb = (qi * tq + tq + bk - 1) // bk
        dq_ref[...] = lax.fori_loop(0, n_kb, step, jnp.zeros((tq, dq_w), F32)).astype(dq_ref.dtype)

        @pl.when(qi == nq - 1)
        def _():
            dk_ref[...] = dk_acc[...].astype(dk_ref.dtype)
            dv_ref[...] = dv_acc[...].astype(dv_ref.dtype)

    return _attn_call(body, name, heads, S, tq, [q, k, v, o, do, lse], [1, 0, 0, 1, 1],
                      [(dq_w, BF16), (dq_w, BF16), (dv, BF16)], [1, 0, 0],
                      [pltpu.VMEM((S, dq_w), F32), pltpu.VMEM((S, dv), F32)], stats_in=1)


def _stick_fwd(q, k, v, heads, scale, name, tq, bk):
    S, dv = q[0].shape[0], v[2]

    def body(q_ref, k_ref, v_ref, o_ref, tot_ref):
        qi = pl.program_id(1)
        qv = q_ref[...]
        row = qi * tq + lax.broadcasted_iota(jnp.int32, (tq, bk), 0)
        col0 = lax.broadcasted_iota(jnp.int32, (tq, bk), 1)
        m_gt = (lax.broadcasted_iota(jnp.int32, (bk, bk), 0) > lax.broadcasted_iota(jnp.int32, (bk, bk), 1)).astype(BF16)
        n_kb = (qi * tq + tq + bk - 1) // bk

        def step(it, carry):
            c, acc = carry
            kb = n_kb - 1 - it
            sl = pl.ds(pl.multiple_of(kb * bk, bk), bk)
            z = _dot_nt(qv, k_ref[sl, :]) * scale
            mask = kb * bk + col0 < row
            sp = _softplus(z)
            lk = jnp.where(mask, -sp, 0.0)
            after = _cumsum_mm(lk, m_gt) + c
            a = jnp.where(mask, jnp.exp(z - sp + after), 0.0)
            acc = acc + _dot(a.astype(BF16), v_ref[sl, :])
            return c + jnp.sum(lk, axis=1, keepdims=True), acc

        c, acc = lax.fori_loop(0, n_kb, step, (jnp.zeros((tq, 1), F32), jnp.zeros((tq, dv), F32)))
        o_ref[...] = acc.astype(o_ref.dtype)
        tot_ref[0] = c

    return _attn_call(body, name, heads, S, tq, [q, k, v], [1, 0, 0], [(dv, BF16)], [1], [], stats_out=1)


def _stick_bwd(q, k, v, do, tot, heads, scale, name, tq, bk):
    S, dq_w, dv = q[0].shape[0], q[2], v[2]
    nq = S // tq

    def body(q_ref, k_ref, v_ref, do_ref, tot_ref, dq_ref, dk_ref, dv_ref, dk_acc, dv_acc):
        qi = pl.program_id(1)

        @pl.when(qi == 0)
        def _():
            dk_acc[...] = jnp.zeros_like(dk_acc)
            dv_acc[...] = jnp.zeros_like(dv_acc)

        qv, dov = q_ref[...], do_ref[...]
        tot_v = tot_ref[0]
        row = qi * tq + lax.broadcasted_iota(jnp.int32, (tq, bk), 0)
        col0 = lax.broadcasted_iota(jnp.int32, (tq, bk), 1)
        j_idx = lax.broadcasted_iota(jnp.int32, (bk, bk), 0)
        s_idx = lax.broadcasted_iota(jnp.int32, (bk, bk), 1)
        m_le, m_lt = (j_idx <= s_idx).astype(BF16), (j_idx < s_idx).astype(BF16)

        def step(kb, carry):
            pc, gc, dq = carry
            sl = pl.ds(pl.multiple_of(kb * bk, bk), bk)
            ks, vs = k_ref[sl, :], v_ref[sl, :]
            z = _dot_nt(qv, ks) * scale
            mask = kb * bk + col0 < row
            sp = _softplus(z)
            lk = jnp.where(mask, -sp, 0.0)
            after = tot_v - pc - _cumsum_mm(lk, m_le)
            log_beta = z - sp
            a = jnp.where(mask, jnp.exp(log_beta + after), 0.0)
            g = _dot_nt(dov, vs) * a
            cg = gc + _cumsum_mm(g, m_lt)
            dz = (jnp.where(mask, g * jnp.exp(-sp) - jnp.exp(log_beta) * cg, 0.0) * scale).astype(BF16)
            dk_acc[sl, :] += _dot_tn(dz, qv)
            dv_acc[sl, :] += _dot_tn(a.astype(BF16), dov)
            return (pc + jnp.sum(lk, axis=1, keepdims=True), gc + jnp.sum(g, axis=1, keepdims=True), dq + _dot(dz, ks))

        n_kb = (qi * tq + tq + bk - 1) // bk
        zero = jnp.zeros((tq, 1), F32)
        _, _, dq = lax.fori_loop(0, n_kb, step, (zero, zero, jnp.zeros((tq, dq_w), F32)))
        dq_ref[...] = dq.astype(dq_ref.dtype)

        @pl.when(qi == nq - 1)
        def _():
            dk_ref[...] = dk_acc[...].astype(dk_ref.dtype)
            dv_ref[...] = dv_acc[...].astype(dv_ref.dtype)

    return _attn_call(body, name, heads, S, tq, [q, k, v, do, tot], [1, 0, 0, 1],
                      [(dq_w, BF16), (dq_w, BF16), (dv, BF16)], [1, 0, 0],
                      [pltpu.VMEM((S, dq_w), F32), pltpu.VMEM((S, dv), F32)], stats_in=1)


def _exchange(groups, gather, name):
    flat = [(gi, li, a) for gi, grp in enumerate(groups) for li, a in enumerate(grp)]
    n = len(flat)

    def body(*refs):
        ins, outs = refs[:n], refs[n:n + len(groups)]
        send_sems, recv_sems, local_sems = refs[n + len(groups):]
        x, y, c = lax.axis_index("x"), lax.axis_index("y"), lax.axis_index("c")
        me = 4 * x + 2 * y + c
        pending = []
        for i, (gi, li, _) in enumerate(flat):
            own = pltpu.make_async_copy(ins[i] if gather else ins[i].at[me], outs[gi].at[li, me], local_sems.at[i])
            own.start()
            pending.append(own)
        waits = []
        for k in range(1, NDEV):
            px = 1 - x if k & 4 else x
            py = 1 - y if k & 2 else y
            pc = 1 - c if k & 1 else c
            peer = 4 * px + 2 * py + pc
            for i, (gi, li, _) in enumerate(flat):
                cp = pltpu.make_async_remote_copy(
                    src_ref=ins[i] if gather else ins[i].at[peer], dst_ref=outs[gi].at[li, me],
                    send_sem=send_sems.at[i, k - 1], recv_sem=recv_sems.at[i, k - 1],
                    device_id=(px, py, pc), device_id_type=pl.DeviceIdType.MESH)
                cp.start()
                landing = pltpu.make_async_remote_copy(
                    src_ref=ins[i] if gather else ins[i].at[peer], dst_ref=outs[gi].at[li, peer],
                    send_sem=send_sems.at[i, k - 1], recv_sem=recv_sems.at[i, k - 1],
                    device_id=(px, py, pc), device_id_type=pl.DeviceIdType.MESH)
                waits.append((cp, landing))
        for cp, landing in waits:
            cp.wait_send()
            landing.wait_recv()
        for own in pending:
            own.wait()

    anyspec = pl.BlockSpec(memory_space=pl.ANY)
    out_shape = [jax.ShapeDtypeStruct((len(grp), NDEV) + tuple(grp[0].shape[-2:]), grp[0].dtype) for grp in groups]
    return pl.pallas_call(
        body, name=name, out_shape=out_shape, in_specs=[anyspec] * n, out_specs=[anyspec] * len(groups),
        scratch_shapes=[pltpu.SemaphoreType.DMA((n, NDEV - 1)), pltpu.SemaphoreType.DMA((n, NDEV - 1)),
                        pltpu.SemaphoreType.DMA((n,))],
        compiler_params=pltpu.CompilerParams(has_side_effects=True),
    )(*[a for (_, _, a) in flat])


def _adamw(slots, w, m, v, name):
    L, _, R, C = slots.shape
    tc = _tile(C, 2048)
    tr = _tile(R, max(8, ADAM_TILE_BYTES // (slots.dtype.itemsize * tc)), mult=8)
    c1, c2 = 1.0 - ADAM_B1 ** ADAM_STEP, 1.0 - ADAM_B2 ** ADAM_STEP

    def body(s_ref, w_ref, m_ref, v_ref, g_out, d_out, m_out, v_out):
        g = s_ref[0, 0].astype(F32)
        for k in range(1, NDEV):
            g = g + s_ref[0, k].astype(F32)
        m_new = ADAM_B1 * m_ref[0] + (1.0 - ADAM_B1) * g
        v_new = ADAM_B2 * v_ref[0] + (1.0 - ADAM_B2) * (g * g)
        g_out[0] = g
        m_out[0] = m_new
        v_out[0] = v_new
        d_out[0] = -ADAM_LR * ((m_new / c1) / (jnp.sqrt(v_new / c2) + ADAM_EPS) + ADAM_WD * w_ref[0])

    spec = pl.BlockSpec((1, tr, tc), lambda l, i, j: (l, i, j))
    return pl.pallas_call(
        body, name=name, grid=(L, R // tr, C // tc), out_shape=[jax.ShapeDtypeStruct((L, R, C), F32)] * 4,
        in_specs=[pl.BlockSpec((1, NDEV, tr, tc), lambda l, i, j: (l, 0, i, j)), spec, spec, spec],
        out_specs=[spec] * 4,
        compiler_params=pltpu.CompilerParams(dimension_semantics=("parallel", "parallel", "parallel"),
                                             vmem_limit_bytes=VMEM_LIMIT),
    )(slots, w, m, v)


class _Cfg:
    def __init__(self, S, D, groups, q_lora, kv_lora, c_heads, d_mix):
        self.S, self.D, self.G, self.Q, self.KV, self.Hc, self.DMIX = S, D, groups, q_lora, kv_lora, c_heads, d_mix
        self.A, self.C = groups * LANE, c_heads * LANE
        self.B = d_mix - self.A - self.C
        self.Hb = self.B // LANE
        A, B, C = self.A, self.B, self.C
        assert B % LANE == 0 and B % C == 0 and (B + C) % A == 0
        self.ref_segs = [("ua", A), ("va", A), ("za", A), ("qb", B), ("kb", B), ("vb", B), ("zb", B),
                         ("cq", q_lora), ("ckv", kv_lora), ("kr", ROPE), ("zc", C)]
        self.off, off = {}, 0
        for nm, w in [("ua", A), ("va", A), ("za", A), ("qb", B), ("kb", B), ("vb", B), ("zb", B), ("zc", C),
                      ("cq", q_lora), ("kr", LANE), ("ckv", kv_lora)]:
            off = -(-off // w) * w
            self.off[nm] = off
            off += w
        self.NP = -(-off // 512) * 512
        self.width = {"kr": LANE, **{nm: w for nm, w in self.ref_segs if nm != "kr"}}

    def view(self, arr, nm):
        w = self.width[nm]
        return (arr, w, self.off[nm] // w)

    def heads_view(self, arr, nm):
        return (arr, self.off[nm] // LANE, LANE)


def _pad_w_in(cfg, w):
    pieces, start = {}, 0
    for nm, width in cfg.ref_segs:
        pieces[nm] = w[:, start:start + width]
        start += width
    cols, pos = [], 0
    for nm, off in sorted(cfg.off.items(), key=lambda kv: kv[1]):
        if off > pos:
            cols.append(jnp.zeros((w.shape[0], off - pos), w.dtype))
        cols.append(pieces[nm])
        pos = off + pieces[nm].shape[1]
    if cfg.NP > pos:
        cols.append(jnp.zeros((w.shape[0], cfg.NP - pos), w.dtype))
    return jnp.concatenate(cols, axis=1)


def _unpad_w_in(cfg, wp):
    return jnp.concatenate([wp[:, cfg.off[nm]:cfg.off[nm] + width] for nm, width in cfg.ref_segs], axis=1)


def _to_slots_cols(w):
    R = w.shape[0]
    return w.reshape(R, NDEV, -1).transpose(1, 0, 2)


def _from_slots_cols(s):
    return s.transpose(1, 0, 2).reshape(s.shape[1], -1)


def _perm_rows_out(cfg, w):
    return jnp.concatenate([w[cfg.A:], w[:cfg.A]], axis=0)


def _unperm_rows_out(cfg, w):
    return jnp.concatenate([w[cfg.B + cfg.C:], w[:cfg.B + cfg.C]], axis=0)


def _layer_params(cfg, l, g_pre, a_g_v, a_w_s, a_b_s, c_g_q, c_g_kv, g_out):
    A, B = cfg.A, cfg.B
    return dict(g_pre=g_pre[l][None], g_v=a_g_v[l].reshape(1, A), w_s=a_w_s[l], b_s=a_b_s[l][:, :, None],
                g_q=c_g_q[l][None], g_kv=c_g_kv[l][None],
                g_oa=g_out[l][None, :A], g_ob=g_out[l][None, A:A + B], g_oc=g_out[l][None, A + B:])


def _layer_fwd(cfg, l, x, W, p, cos2, sin2, rot):
    S, D, A, B, C = cfg.S, cfg.D, cfg.A, cfg.B, cfg.C
    tag = f"l{l}"
    (h,) = _rowwise(_f_pre, [(x, D, 0)], [], [p["g_pre"]], [], [(D, BF16)], 256, f"pre_{tag}")
    proj = _matmul(h, W["in"], "nn", BF16, f"mm_in_{tag}")
    a_rows = [cfg.view(proj, "ua"), cfg.view(proj, "va"), cfg.view(proj, "za")]
    a_par = [p["g_v"], p["w_s"], p["b_s"], p["g_oa"]]
    (ya,) = _rowwise(_f_gmlp, a_rows, [], a_par, [], [(A, BF16)], LANE, f"gmlp_{tag}")
    qb, kb, vb = cfg.heads_view(proj, "qb"), cfg.heads_view(proj, "kb"), cfg.heads_view(proj, "vb")
    yb, tot = _stick_fwd(qb, kb, vb, cfg.Hb, LANE ** -0.5, f"stick_fwd_{tag}", 128, 128)
    (ybg,) = _rowwise(_f_gate, [(yb, B, 0), cfg.view(proj, "zb")], [], [p["g_ob"]], [], [(B, BF16)], 256, f"gate_b_{tag}")
    c_rows = [cfg.view(proj, "cq"), cfg.view(proj, "ckv"), cfg.view(proj, "kr")]
    trig = [(cos2, LANE, 0), (sin2, LANE, 0)]
    cqn, ckvn, krr = _rowwise(_f_cpre, c_rows, trig, [p["g_q"], p["g_kv"]], [rot],
                              [(cfg.Q, BF16), (cfg.KV, BF16), (LANE, BF16)], 256, f"cpre_{tag}")
    q_raw = _matmul(cqn, W["uq"], "nn", BF16, f"mm_uq_{tag}")
    kv = _matmul(ckvn, W["ukv"], "nn", BF16, f"mm_ukv_{tag}")
    r_rows = [(q_raw, 2 * C, 0), (kv, 2 * C, 0), (krr, LANE, 0)]
    q_rot, k_full, v_c = _rowwise(_f_crope, r_rows, trig, [], [rot], [(2 * C, BF16), (2 * C, BF16), (C, BF16)], 128,
                                  f"crope_{tag}")
    qc, kc, vc = (q_rot, 0, 2 * LANE), (k_full, 0, 2 * LANE), (v_c, 0, LANE)
    yc, lse = _softmax_fwd(qc, kc, vc, cfg.Hc, (LANE + ROPE) ** -0.5, f"mla_fwd_{tag}", 256, 256)
    (ycg,) = _rowwise(_f_gate, [(yc, C, 0), cfg.view(proj, "zc")], [], [p["g_oc"]], [], [(C, BF16)], 256, f"gate_c_{tag}")
    y = jnp.concatenate([ybg, ycg, ya], axis=1)
    out = _matmul(y, W["out"], "nn", F32, f"mm_out_{tag}", add=x)
    saved = dict(x=x, h=h, proj=proj, yb=yb, tot=tot, cqn=cqn, ckvn=ckvn, krr=krr, q_raw=q_raw, kv=kv,
                 q_rot=q_rot, k_full=k_full, v_c=v_c, yc=yc, lse=lse, y=y)
    return out, saved


def _layer_bwd(cfg, l, dout, sv, W, p, cos2, sin2, rot):
    S, D, A, B, C = cfg.S, cfg.D, cfg.A, cfg.B, cfg.C
    tag = f"l{l}"
    proj = sv["proj"]
    dy = _matmul(dout, W["out"], "nt", BF16, f"mm_dy_{tag}")
    d_wout = _matmul(sv["y"], dout, "tn", BF16, f"mm_dwout_{tag}")
    (dyb, dzb), (dg_ob,), _ = _rowwise_vjp(_f_gate, [(sv["yb"], B, 0), cfg.view(proj, "zb")], [], [p["g_ob"]], [],
                                           [(dy, B, 0)], [BF16, BF16], 256, f"gate_b_bwd_{tag}")
    (dyc, dzc), (dg_oc,), _ = _rowwise_vjp(_f_gate, [(sv["yc"], C, 0), cfg.view(proj, "zc")], [], [p["g_oc"]], [],
                                           [(dy, C, B // C)], [BF16, BF16], 256, f"gate_c_bwd_{tag}")
    a_rows = [cfg.view(proj, "ua"), cfg.view(proj, "va"), cfg.view(proj, "za")]
    a_par = [p["g_v"], p["w_s"], p["b_s"], p["g_oa"]]
    (dua, dva, dza), (dg_v, dw_s, db_s, dg_oa), _ = _rowwise_vjp(
        _f_gmlp, a_rows, [], a_par, [], [(dy, A, (B + C) // A)], [BF16] * 3, LANE, f"gmlp_bwd_{tag}")
    qb, kb, vb = cfg.heads_view(proj, "qb"), cfg.heads_view(proj, "kb"), cfg.heads_view(proj, "vb")
    dqb, dkb, dvb = _stick_bwd(qb, kb, vb, (dyb, 0, LANE), sv["tot"], cfg.Hb, LANE ** -0.5, f"stick_bwd_{tag}", 128, 128)
    qc, kc, vc = (sv["q_rot"], 0, 2 * LANE), (sv["k_full"], 0, 2 * LANE), (sv["v_c"], 0, LANE)
    dq_rot, dk_full, dv_c = _softmax_bwd(qc, kc, vc, (sv["yc"], 0, LANE), (dyc, 0, LANE), sv["lse"], cfg.Hc,
                                         (LANE + ROPE) ** -0.5, f"mla_bwd_{tag}", 256, 256)
    trig = [(cos2, LANE, 0), (sin2, LANE, 0)]
    r_rows = [(sv["q_raw"], 2 * C, 0), (sv["kv"], 2 * C, 0), (sv["krr"], LANE, 0)]
    (dq_raw, dkv, dkrr), _, _ = _rowwise_vjp(_f_crope, r_rows, trig, [], [rot],
                                             [(dq_rot, 2 * C, 0), (dk_full, 2 * C, 0), (dv_c, C, 0)], [BF16] * 3, 128,
                                             f"crope_bwd_{tag}")
    dcqn = _matmul(dq_raw, W["uq"], "nt", BF16, f"mm_dcq_{tag}")
    d_wuq = _matmul(sv["cqn"], dq_raw, "tn", BF16, f"mm_dwuq_{tag}")
    dckvn = _matmul(dkv, W["ukv"], "nt", BF16, f"mm_dckv_{tag}")
    d_wukv = _matmul(sv["ckvn"], dkv, "tn", BF16, f"mm_dwukv_{tag}")
    c_rows = [cfg.view(proj, "cq"), cfg.view(proj, "ckv"), cfg.view(proj, "kr")]
    (dcq, dckv, dkr), (dg_q, dg_kv), _ = _rowwise_vjp(
        _f_cpre, c_rows, trig, [p["g_q"], p["g_kv"]], [rot],
        [(dcqn, cfg.Q, 0), (dckvn, cfg.KV, 0), (dkrr, LANE, 0)], [BF16] * 3, 256, f"cpre_bwd_{tag}")
    parts = dict(ua=dua, va=dva, za=dza, qb=dqb, kb=dkb, vb=dvb, zb=dzb, zc=dzc, cq=dcq, kr=dkr, ckv=dckv)
    cols, pos = [], 0
    for nm, off in sorted(cfg.off.items(), key=lambda kv_: kv_[1]):
        if off > pos:
            cols.append(jnp.zeros((S, off - pos), BF16))
        cols.append(parts[nm])
        pos = off + parts[nm].shape[1]
    if cfg.NP > pos:
        cols.append(jnp.zeros((S, cfg.NP - pos), BF16))
    dproj = jnp.concatenate(cols, axis=1)
    dh = _matmul(dproj, W["in"], "nt", BF16, f"mm_dh_{tag}")
    d_win = _matmul(sv["h"], dproj, "tn", BF16, f"mm_dwin_{tag}")
    (dx,), (dg_pre,), _ = _rowwise_vjp(_f_pre_res, [(sv["x"], D, 0)], [], [p["g_pre"]], [],
                                       [(dh, D, 0), (dout, D, 0)], [F32], 128, f"pre_bwd_{tag}")
    small = dict(g_pre=dg_pre[0], a_g_v=dg_v.reshape(cfg.G, LANE), a_w_s=dw_s, a_b_s=db_s[:, :, 0], c_g_q=dg_q[0],
                 c_g_kv=dg_kv[0], g_out=jnp.concatenate([dg_oa[0], dg_ob[0], dg_oc[0]]))
    big = dict(w_in=_to_slots_cols(_unpad_w_in(cfg, d_win)),
               c_w_uq=_to_slots_cols(d_wuq.reshape(cfg.Q, cfg.Hc, 2 * LANE)[:, :, :LANE + ROPE].reshape(cfg.Q, -1)),
               c_w_ukv=_to_slots_cols(d_wukv),
               w_out=_unperm_rows_out(cfg, d_wout).reshape(NDEV, cfg.DMIX // NDEV, D))
    return dx, small, big


def _pack_small(vals):
    packed = jnp.concatenate([vals[nm].reshape(-1, LANE) for nm in SMALL], axis=0)
    return jnp.pad(packed, ((0, -packed.shape[0] % SMALL_ROWS), (0, 0)))


def _unpack_small(packed, like):
    out, row = {}, 0
    for nm in SMALL:
        n = like[nm].size // LANE
        out[nm] = packed[row:row + n].reshape(like[nm].shape)
        row += n
    return out


def kernel(x, positions, g_pre, w_in, a_g_v, a_w_s, a_b_s, c_g_q, c_g_kv, c_w_uq, c_w_ukv, g_out, w_out, g_final, loss_target, m_g_pre, m_w_in, m_a_g_v, m_a_w_s, m_a_b_s, m_c_g_q, m_c_g_kv, m_c_w_uq, m_c_w_ukv, m_g_out, m_w_out, m_g_final, v_g_pre, v_w_in, v_a_g_v, v_a_w_s, v_a_b_s, v_c_g_q, v_c_g_kv, v_c_w_uq, v_c_w_ukv, v_g_out, v_w_out, v_g_final):
    depth, S, D = w_in.shape[0], x.shape[1], x.shape[2]
    cfg = _Cfg(S, D, a_g_v.shape[1], c_g_q.shape[1], c_g_kv.shape[1], c_w_ukv.shape[2] * NDEV // (2 * LANE), g_out.shape[1])
    weights = dict(g_pre=g_pre, w_in=w_in, a_g_v=a_g_v, a_w_s=a_w_s, a_b_s=a_b_s, c_g_q=c_g_q, c_g_kv=c_g_kv,
                   c_w_uq=c_w_uq, c_w_ukv=c_w_ukv, g_out=g_out, w_out=w_out, g_final=g_final)
    mom_m = dict(g_pre=m_g_pre, w_in=m_w_in, a_g_v=m_a_g_v, a_w_s=m_a_w_s, a_b_s=m_a_b_s, c_g_q=m_c_g_q, c_g_kv=m_c_g_kv,
                 c_w_uq=m_c_w_uq, c_w_ukv=m_c_w_ukv, g_out=m_g_out, w_out=m_w_out, g_final=m_g_final)
    mom_v = dict(g_pre=v_g_pre, w_in=v_w_in, a_g_v=v_a_g_v, a_w_s=v_a_w_s, a_b_s=v_a_b_s, c_g_q=v_c_g_q, c_g_kv=v_c_g_kv,
                 c_w_uq=v_c_w_uq, c_w_ukv=v_c_w_ukv, g_out=v_g_out, w_out=v_w_out, g_final=v_g_final)
    big_names = ("w_in", "c_w_uq", "c_w_ukv", "w_out")

    inv_freq = 1.0 / (ROPE_THETA ** (jnp.arange(0, ROPE, 2, dtype=F32) / ROPE))
    ang = positions[0].astype(F32)[:, None] * inv_freq
    zpad = jnp.zeros((S, LANE - ROPE), F32)
    cos2 = jnp.concatenate([jnp.cos(ang), jnp.cos(ang), zpad], axis=1)
    sin2 = jnp.concatenate([jnp.sin(ang), jnp.sin(ang), zpad], axis=1)
    rot = _rope_matrix()

    gathered = _exchange([[weights[nm][l].astype(BF16) for l in range(depth)] for nm in big_names], True, "gather_weights")
    Ws = []
    for l in range(depth):
        g_in, g_uq, g_ukv, g_wout = (g[l] for g in gathered)
        uq = _from_slots_cols(g_uq).reshape(cfg.Q, cfg.Hc, LANE + ROPE)
        uq = jnp.pad(uq, ((0, 0), (0, 0), (0, LANE - ROPE))).reshape(cfg.Q, 2 * cfg.C)
        Ws.append({"in": _pad_w_in(cfg, _from_slots_cols(g_in)), "uq": uq, "ukv": _from_slots_cols(g_ukv),
                   "out": _perm_rows_out(cfg, g_wout.reshape(cfg.DMIX, D))})
    params = [_layer_params(cfg, l, g_pre, a_g_v, a_w_s, a_b_s, c_g_q, c_g_kv, g_out) for l in range(depth)]

    hcur, saved = x[0], []
    for l in range(depth):
        hcur, sv = _layer_fwd(cfg, l, hcur, Ws[l], params[l], cos2, sin2, rot)
        saved.append(sv)
    (dh,), (dg_final,), (loss_rows,) = _rowwise_vjp(
        _f_final, [(hcur, D, 0)], [(loss_target[0], D, 0)], [g_final[None]], [], [(jnp.ones((S, 1), F32), 1, 0)],
        [F32], 128, "final", primal=[(1, F32)])
    loss = lax.psum(jnp.sum(loss_rows), MESH_AXES)
    small_g, big_g = [None] * depth, [None] * depth
    for l in reversed(range(depth)):
        dh, small_g[l], big_g[l] = _layer_bwd(cfg, l, dh, saved[l], Ws[l], params[l], cos2, sin2, rot)
    grad_x = dh[None]

    slots = _exchange([[big_g[l][nm] for l in range(depth)] for nm in big_names], False, "scatter_grads")
    small_grads = {nm: jnp.stack([small_g[l][nm] for l in range(depth)]) for nm in SMALL if nm != "g_final"}
    small_grads["g_final"] = dg_final[0]
    (small_slots,) = _exchange([[_pack_small(small_grads)]], True, "gather_small_grads")

    res = {}
    for nm, sl in zip(big_names, slots):
        res[nm] = _adamw(sl, weights[nm], mom_m[nm], mom_v[nm], f"adamw_{nm}")
    packed = _adamw(small_slots, _pack_small(weights)[None], _pack_small(mom_m)[None], _pack_small(mom_v)[None], "adamw_small")
    small_res = [_unpack_small(r[0], weights) for r in packed]
    order = ("g_pre", "w_in", "a_g_v", "a_w_s", "a_b_s", "c_g_q", "c_g_kv", "c_w_uq", "c_w_ukv", "g_out", "w_out", "g_final")
    outs = [loss, grad_x]
    for kind in range(4):
        outs += [small_res[kind][nm] if nm in SMALL else res[nm][kind] for nm in order]
    return tuple(outs)
```

```python
import functools

import numpy as np
import jax
import jax.numpy as jnp
from jax import lax
from jax.experimental import pallas as pl
from jax.experimental.pallas import tpu as pltpu

NDEV = 8
MESH_AXES = ("x", "y", "c")
LANE = 128
ROPE = 64
EPS = 1e-6
ROPE_THETA = 10000.0
ADAM_LR, ADAM_B1, ADAM_B2, ADAM_EPS, ADAM_WD, ADAM_STEP = 0.001, 0.9, 0.999, 1e-08, 0.01, 10
VMEM_LIMIT = 48 * 1024 * 1024
ADAM_TILE_BYTES = 512 * 1024
SMALL_ROWS = 256
F32, BF16 = jnp.float32, jnp.bfloat16
SMALL = ("g_pre", "a_g_v", "a_w_s", "a_b_s", "c_g_q", "c_g_kv", "g_out", "g_final")


def _tile(dim, cap, mult=LANE):
    if dim <= cap:
        return dim
    t = (cap // mult) * mult
    while t >= mult:
        if dim % t == 0:
            return t
        t -= mult
    return dim


def _dot_nt(a, b):
    return lax.dot_general(a, b, (((1,), (1,)), ((), ())), preferred_element_type=F32)


def _dot_tn(a, b):
    return lax.dot_general(a, b, (((0,), (0,)), ((), ())), preferred_element_type=F32)


def _dot(a, b):
    return jnp.dot(a, b, preferred_element_type=F32)


def _matmul(a, b, mode, out_dtype, name, add=None, tm=1024, tn=1024, tk=1024):
    if mode == "tn":
        (K, M), (K2, N) = a.shape, b.shape
    elif mode == "nt":
        (M, K), (N, K2) = a.shape, b.shape
    else:
        (M, K), (K2, N) = a.shape, b.shape
    assert K == K2, (a.shape, b.shape, mode)
    tm, tn, tk = _tile(M, tm), _tile(N, tn), _tile(K, tk)
    nk = K // tk
    a_spec = pl.BlockSpec((tk, tm), lambda i, j, k: (k, i)) if mode == "tn" else pl.BlockSpec((tm, tk), lambda i, j, k: (i, k))
    b_spec = pl.BlockSpec((tn, tk), lambda i, j, k: (j, k)) if mode == "nt" else pl.BlockSpec((tk, tn), lambda i, j, k: (k, j))
    dot = {"nn": _dot, "nt": _dot_nt, "tn": _dot_tn}[mode]
    has_add = add is not None

    def body(*refs):
        a_ref, b_ref = refs[0], refs[1]
        o_ref, acc = refs[-2], refs[-1]
        k = pl.program_id(2)

        @pl.when(k == 0)
        def _():
            acc[...] = jnp.zeros_like(acc)

        acc[...] += dot(a_ref[...].astype(BF16), b_ref[...].astype(BF16))

        @pl.when(k == nk - 1)
        def _():
            r = acc[...]
            if has_add:
                r = r + refs[2][...]
            o_ref[...] = r.astype(o_ref.dtype)

    in_specs = [a_spec, b_spec]
    args = [a, b]
    if has_add:
        in_specs.append(pl.BlockSpec((tm, tn), lambda i, j, k: (i, j)))
        args.append(add)
    return pl.pallas_call(
        body, name=name, grid=(M // tm, N // tn, nk),
        out_shape=jax.ShapeDtypeStruct((M, N), out_dtype),
        in_specs=in_specs, out_specs=pl.BlockSpec((tm, tn), lambda i, j, k: (i, j)),
        scratch_shapes=[pltpu.VMEM((tm, tn), F32)],
        compiler_params=pltpu.CompilerParams(dimension_semantics=("parallel", "parallel", "arbitrary"),
                                             vmem_limit_bytes=VMEM_LIMIT),
    )(*args)


def _row_specs(views, tile):
    return [pl.BlockSpec((tile, w), functools.partial(lambda i, cb: (i, cb), cb=cb)) for (_, w, cb) in views]


def _full_specs(arrs):
    return [pl.BlockSpec(p.shape, functools.partial(lambda i, nd: (0,) * nd, nd=p.ndim)) for p in arrs]


def _rowwise(fn, rows, aux, params, consts, outs, tile, name):
    S = rows[0][0].shape[0]
    nr, na, npar, nc = len(rows), len(aux), len(params), len(consts)

    def body(*refs):
        ins = [r[...].astype(F32) for r in refs[:nr + na]]
        small = [r[...] for r in refs[nr + na:nr + na + npar + nc]]
        res = fn(*ins, *small)
        for o_ref, r in zip(refs[nr + na + npar + nc:], res):
            o_ref[...] = r.astype(o_ref.dtype)

    return pl.pallas_call(
        body, name=name, grid=(S // tile,),
        out_shape=[jax.ShapeDtypeStruct((S, w), dt) for (w, dt) in outs],
        in_specs=_row_specs(rows + aux, tile) + _full_specs(params + consts),
        out_specs=[pl.BlockSpec((tile, w), lambda i: (i, 0)) for (w, _) in outs],
        compiler_params=pltpu.CompilerParams(dimension_semantics=("parallel",), vmem_limit_bytes=VMEM_LIMIT),
    )(*[v[0] for v in rows + aux], *params, *consts)


def _rowwise_vjp(fn, rows, aux, params, consts, cots, grad_dtypes, tile, name, primal=()):
    S = rows[0][0].shape[0]
    nr, na, npar, nc, nct, npr = len(rows), len(aux), len(params), len(consts), len(cots), len(primal)

    def body(*refs):
        n_in = nr + na + npar + nc + nct
        rv = [r[...].astype(F32) for r in refs[:nr]]
        av = [r[...].astype(F32) for r in refs[nr:nr + na]]
        pv = [r[...] for r in refs[nr + na:nr + na + npar]]
        cv = [r[...] for r in refs[nr + na + npar:nr + na + npar + nc]]
        ct = tuple(r[...].astype(F32) for r in refs[nr + na + npar + nc:n_in])
        res, vjp = jax.vjp(lambda *rp: tuple(fn(*rp[:nr], *av, *rp[nr:], *cv)), *rv, *pv)
        grads = vjp(ct)
        g_refs = refs[n_in:n_in + nr]
        p_refs = refs[n_in + nr:n_in + nr + npar]
        o_refs = refs[n_in + nr + npar:]
        for g_ref, g in zip(g_refs, grads[:nr]):
            g_ref[...] = g.astype(g_ref.dtype)

        @pl.when(pl.program_id(0) == 0)
        def _():
            for p_ref in p_refs:
                p_ref[...] = jnp.zeros_like(p_ref)

        for p_ref, g in zip(p_refs, grads[nr:]):
            p_ref[...] += g
        for o_ref, r in zip(o_refs, res[:npr]):
            o_ref[...] = r.astype(o_ref.dtype)

    out_shape = ([jax.ShapeDtypeStruct((S, w), dt) for (_, w, _), dt in zip(rows, grad_dtypes)]
                 + [jax.ShapeDtypeStruct(p.shape, F32) for p in params]
                 + [jax.ShapeDtypeStruct((S, w), dt) for (w, dt) in primal])
    out_specs = ([pl.BlockSpec((tile, w), lambda i: (i, 0)) for (_, w, _) in rows] + _full_specs(params)
                 + [pl.BlockSpec((tile, w), lambda i: (i, 0)) for (w, _) in primal])
    res = pl.pallas_call(
        body, name=name, grid=(S // tile,), out_shape=out_shape,
        in_specs=_row_specs(rows + aux, tile) + _full_specs(params + consts) + _row_specs(cots, tile),
        out_specs=out_specs,
        compiler_params=pltpu.CompilerParams(dimension_semantics=("arbitrary",), vmem_limit_bytes=VMEM_LIMIT),
    )(*[v[0] for v in rows + aux], *params, *consts, *[v[0] for v in cots])
    return res[:nr], res[nr:nr + npar], res[nr + npar:]


@jax.custom_vjp
def _mm(a, b):
    return _dot(a.astype(BF16), b.astype(BF16))


def _mm_fwd(a, b):
    return _mm(a, b), (a, b)


def _mm_bwd(res, ct):
    a, b = res
    ctb = ct.astype(BF16)
    return _dot_nt(ctb, b.astype(BF16)), _dot_tn(a.astype(BF16), ctb)


_mm.defvjp(_mm_fwd, _mm_bwd)


def _rms(x, g):
    return x * lax.rsqrt(jnp.mean(x * x, axis=-1, keepdims=True) + EPS) * g


def _f_pre(x, g):
    return (_rms(x, g),)


def _f_pre_res(x, g):
    return _rms(x, g), x


def _f_gate(y, z, g):
    return (_rms(y, g) * jax.nn.silu(z),)


def _f_gmlp(u, v, z, g_v, w_s, b_s, g_o):
    groups = w_s.shape[0]
    u, v = jax.nn.gelu(u), jax.nn.gelu(v)
    t_idx = lax.broadcasted_iota(jnp.int32, (LANE, LANE), 0)
    s_idx = lax.broadcasted_iota(jnp.int32, (LANE, LANE), 1)
    ys = []
    for g in range(groups):
        sl = slice(g * LANE, (g + 1) * LANE)
        vn = _rms(v[:, sl], g_v[:, sl])
        w = jnp.where(s_idx <= t_idx, w_s[g], 0.0)
        ys.append(u[:, sl] * (_mm(w, vn) + b_s[g]))
    return (_rms(jnp.concatenate(ys, axis=1), g_o) * jax.nn.silu(z),)


def _rope(x, cos2, sin2, rot):
    return x * cos2 + _mm(x, rot) * sin2


def _f_cpre(cq, ckv, kr, cos2, sin2, g_q, g_kv, rot):
    return _rms(cq, g_q), _rms(ckv, g_kv), _rope(kr, cos2, sin2, rot)


def _f_crope(q, kv, krr, cos2, sin2, rot):
    heads = q.shape[1] // (2 * LANE)
    qs, ks, vs = [], [], []
    for h in range(heads):
        lo, mid, hi = 2 * h * LANE, (2 * h + 1) * LANE, (2 * h + 2) * LANE
        qs += [q[:, lo:mid], _rope(q[:, mid:hi], cos2, sin2, rot)]
        ks += [kv[:, lo:mid], krr]
        vs += [kv[:, mid:hi]]
    return jnp.concatenate(qs, axis=1), jnp.concatenate(ks, axis=1), jnp.concatenate(vs, axis=1)


def _f_final(h, target, g):
    err = _rms(h, g) - target
    return (0.5 * jnp.mean(err * err, axis=-1, keepdims=True),)


def _rope_matrix():
    r = np.zeros((LANE, LANE), np.float32)
    half = ROPE // 2
    for i in range(half):
        r[i + half, i] = -1.0
        r[i, i + half] = 1.0
    return jnp.asarray(r)


def _head_spec(view, rows, n_rows_block):
    _, cb0, w = view
    if n_rows_block:
        return pl.BlockSpec((rows, w), functools.partial(lambda h, i, cb0: (i, cb0 + h), cb0=cb0))
    return pl.BlockSpec((rows, w), functools.partial(lambda h, i, cb0: (0, cb0 + h), cb0=cb0))


def _stat_spec(tq):
    return pl.BlockSpec((1, tq, 1), lambda h, i: (h, i, 0))


def _softplus(z):
    return jnp.maximum(z, 0.0) + jnp.log(1.0 + jnp.exp(-jnp.abs(z)))


def _cumsum_mm(x, m01):
    hi = x.astype(BF16)
    lo = (x - hi.astype(F32)).astype(BF16)
    return _dot(hi, m01) + _dot(lo, m01)


def _attn_call(body, name, heads, S, tq, ins, in_blocked, outs, out_blocked, scratch, stats_in=0, stats_out=0):
    in_specs = [_head_spec(v, tq if blk else S, blk) for v, blk in zip(ins[:len(ins) - stats_in], in_blocked)]
    in_specs += [_stat_spec(tq)] * stats_in
    out_specs = [_head_spec((None, 0, w), tq if blk else S, blk) for (w, _), blk in zip(outs, out_blocked)]
    out_specs += [_stat_spec(tq)] * stats_out
    out_shape = [jax.ShapeDtypeStruct((S, heads * w), dt) for (w, dt) in outs]
    out_shape += [jax.ShapeDtypeStruct((heads, S, 1), F32)] * stats_out
    args = [v[0] for v in ins[:len(ins) - stats_in]] + list(ins[len(ins) - stats_in:])
    return pl.pallas_call(
        body, name=name, grid=(heads, S // tq), out_shape=out_shape, in_specs=in_specs, out_specs=out_specs,
        scratch_shapes=scratch,
        compiler_params=pltpu.CompilerParams(dimension_semantics=("arbitrary", "arbitrary"), vmem_limit_bytes=VMEM_LIMIT),
    )(*args)


def _softmax_fwd(q, k, v, heads, scale, name, tq, bk):
    S, dv = q[0].shape[0], v[2]

    def body(q_ref, k_ref, v_ref, o_ref, lse_ref):
        qi = pl.program_id(1)
        qv = q_ref[...]
        row = qi * tq + lax.broadcasted_iota(jnp.int32, (tq, bk), 0)
        col0 = lax.broadcasted_iota(jnp.int32, (tq, bk), 1)

        def step(kb, carry):
            m, l, acc = carry
            sl = pl.ds(pl.multiple_of(kb * bk, bk), bk)
            s = _dot_nt(qv, k_ref[sl, :]) * scale
            s = jnp.where(kb * bk + col0 <= row, s, -1e30)
            m_new = jnp.maximum(m, jnp.max(s, axis=1, keepdims=True))
            p = jnp.exp(s - m_new)
            alpha = jnp.exp(m - m_new)
            l = alpha * l + jnp.sum(p, axis=1, keepdims=True)
            acc = alpha * acc + _dot(p.astype(BF16), v_ref[sl, :])
            return m_new, l, acc

        n_kb = (qi * tq + tq + bk - 1) // bk
        m, l, acc = lax.fori_loop(0, n_kb, step, (jnp.full((tq, 1), -1e30, F32), jnp.zeros((tq, 1), F32),
                                                  jnp.zeros((tq, dv), F32)))
        o_ref[...] = (acc / l).astype(o_ref.dtype)
        lse_ref[0] = m + jnp.log(l)

    return _attn_call(body, name, heads, S, tq, [q, k, v], [1, 0, 0], [(dv, BF16)], [1], [], stats_out=1)


def _softmax_bwd(q, k, v, o, do, lse, heads, scale, name, tq, bk):
    S, dq_w, dv = q[0].shape[0], q[2], v[2]
    nq = S // tq

    def body(q_ref, k_ref, v_ref, o_ref, do_ref, lse_ref, dq_ref, dk_ref, dv_ref, dk_acc, dv_acc):
        qi = pl.program_id(1)

        @pl.when(qi == 0)
        def _():
            dk_acc[...] = jnp.zeros_like(dk_acc)
            dv_acc[...] = jnp.zeros_like(dv_acc)

        qv, dov = q_ref[...], do_ref[...]
        delta = jnp.sum(dov.astype(F32) * o_ref[...].astype(F32), axis=1, keepdims=True)
        lse_v = lse_ref[0]
        row = qi * tq + lax.broadcasted_iota(jnp.int32, (tq, bk), 0)
        col0 = lax.broadcasted_iota(jnp.int32, (tq, bk), 1)

        def step(kb, dq):
            sl = pl.ds(pl.multiple_of(kb * bk, bk), bk)
            ks, vs = k_ref[sl, :], v_ref[sl, :]
            s = _dot_nt(qv, ks) * scale
            p = jnp.where(kb * bk + col0 <= row, jnp.exp(s - lse_v), 0.0)
            ds = (p * (_dot_nt(dov, vs) - delta) * scale).astype(BF16)
            dk_acc[sl, :] += _dot_tn(ds, qv)
            dv_acc[sl, :] += _dot_tn(p.astype(BF16), dov)
            return dq + _dot(ds, ks)

        n_kb = (qi * tq + tq + bk - 1) // bk
        dq_ref[...] = lax.fori_loop(0, n_kb, step, jnp.zeros((tq, dq_w), F32)).astype(dq_ref.dtype)

        @pl.when(qi == nq - 1)
        def _():
            dk_ref[...] = dk_acc[...].astype(dk_ref.dtype)
            dv_ref[...] = dv_acc[...].astype(dv_ref.dtype)

    return _attn_call(body, name, heads, S, tq, [q, k, v, o, do, lse], [1, 0, 0, 1, 1],
                      [(dq_w, BF16), (dq_w, BF16), (dv, BF16)], [1, 0, 0],
                      [pltpu.VMEM((S, dq_w), F32), pltpu.VMEM((S, dv), F32)], stats_in=1)


def _stick_fwd(q, k, v, heads, scale, name, tq, bk):
    S, dv = q[0].shape[0], v[2]

    def body(q_ref, k_ref, v_ref, o_ref, tot_ref):
        qi = pl.program_id(1)
        qv = q_ref[...]
        row = qi * tq + lax.broadcasted_iota(jnp.int32, (tq, bk), 0)
        col0 = lax.broadcasted_iota(jnp.int32, (tq, bk), 1)
        m_gt = (lax.broadcasted_iota(jnp.int32, (bk, bk), 0) > lax.broadcasted_iota(jnp.int32, (bk, bk), 1)).astype(BF16)
        n_kb = (qi * tq + tq + bk - 1) // bk

        def step(it, carry):
            c, acc = carry
            kb = n_kb - 1 - it
            sl = pl.ds(pl.multiple_of(kb * bk, bk), bk)
            z = _dot_nt(qv, k_ref[sl, :]) * scale
            mask = kb * bk + col0 < row
            sp = _softplus(z)
            lk = jnp.where(mask, -sp, 0.0)
            after = _cumsum_mm(lk, m_gt) + c
            a = jnp.where(mask, jnp.exp(z - sp + after), 0.0)
            acc = acc + _dot(a.astype(BF16), v_ref[sl, :])
            return c + jnp.sum(lk, axis=1, keepdims=True), acc

        c, acc = lax.fori_loop(0, n_kb, step, (jnp.zeros((tq, 1), F32), jnp.zeros((tq, dv), F32)))
        o_ref[...] = acc.astype(o_ref.dtype)
        tot_ref[0] = c

    return _attn_call(body, name, heads, S, tq, [q, k, v], [1, 0, 0], [(dv, BF16)], [1], [], stats_out=1)


def _stick_bwd(q, k, v, do, tot, heads, scale, name, tq, bk):
    S, dq_w, dv = q[0].shape[0], q[2], v[2]
    nq = S // tq

    def body(q_ref, k_ref, v_ref, do_ref, tot_ref, dq_ref, dk_ref, dv_ref, dk_acc, dv_acc):
        qi = pl.program_id(1)

        @pl.when(qi == 0)
        def _():
            dk_acc[...] = jnp.zeros_like(dk_acc)
            dv_acc[...] = jnp.zeros_like(dv_acc)

        qv, dov = q_ref[...], do_ref[...]
        tot_v = tot_ref[0]
        row = qi * tq + lax.broadcasted_iota(jnp.int32, (tq, bk), 0)
        col0 = lax.broadcasted_iota(jnp.int32, (tq, bk), 1)
        j_idx = lax.broadcasted_iota(jnp.int32, (bk, bk), 0)
        s_idx = lax.broadcasted_iota(jnp.int32, (bk, bk), 1)
        m_le, m_lt = (j_idx <= s_idx).astype(BF16), (j_idx < s_idx).astype(BF16)

        def step(kb, carry):
            pc, gc, dq = carry
            sl = pl.ds(pl.multiple_of(kb * bk, bk), bk)
            ks, vs = k_ref[sl, :], v_ref[sl, :]
            z = _dot_nt(qv, ks) * scale
            mask = kb * bk + col0 < row
            sp = _softplus(z)
            lk = jnp.where(mask, -sp, 0.0)
            after = tot_v - pc - _cumsum_mm(lk, m_le)
            log_beta = z - sp
            a = jnp.where(mask, jnp.exp(log_beta + after), 0.0)
            g = _dot_nt(dov, vs) * a
            cg = gc + _cumsum_mm(g, m_lt)
            dz = (jnp.where(mask, g * jnp.exp(-sp) - jnp.exp(log_beta) * cg, 0.0) * scale).astype(BF16)
            dk_acc[sl, :] += _dot_tn(dz, qv)
            dv_acc[sl, :] += _dot_tn(a.astype(BF16), dov)
            return (pc + jnp.sum(lk, axis=1, keepdims=True), gc + jnp.sum(g, axis=1, keepdims=True), dq + _dot(dz, ks))

        n_kb = (qi * tq + tq + bk - 1) // bk
        zero = jnp.zeros((tq, 1), F32)
        _, _, dq = lax.fori_loop(0, n_kb, step, (zero, zero, jnp.zeros((tq, dq_w), F32)))
        dq_ref[...] = dq.astype(dq_ref.dtype)

        @pl.when(qi == nq - 1)
        def _():
            dk_ref[...] = dk_acc[...].astype(dk_ref.dtype)
            dv_ref[...] = dv_acc[...].astype(dv_ref.dtype)

    return _attn_call(body, name, heads, S, tq, [q, k, v, do, tot], [1, 0, 0, 1],
                      [(dq_w, BF16), (dq_w, BF16), (dv, BF16)], [1, 0, 0],
                      [pltpu.VMEM((S, dq_w), F32), pltpu.VMEM((S, dv), F32)], stats_in=1)


def _exchange(groups, gather, name):
    flat = [(gi, li, a) for gi, grp in enumerate(groups) for li, a in enumerate(grp)]
    n = len(flat)

    def body(*refs):
        ins, outs = refs[:n], refs[n:n + len(groups)]
        send_sems, recv_sems, local_sems = refs[n + len(groups):]
        x, y, c = lax.axis_index("x"), lax.axis_index("y"), lax.axis_index("c")
        me = 4 * x + 2 * y + c
        pending = []
        for i, (gi, li, _) in enumerate(flat):
            own = pltpu.make_async_copy(ins[i] if gather else ins[i].at[me], outs[gi].at[li, me], local_sems.at[i])
            own.start()
            pending.append(own)
        waits = []
        for k in range(1, NDEV):
            px = 1 - x if k & 4 else x
            py = 1 - y if k & 2 else y
            pc = 1 - c if k & 1 else c
            peer = 4 * px + 2 * py + pc
            for i, (gi, li, _) in enumerate(flat):
                cp = pltpu.make_async_remote_copy(
                    src_ref=ins[i] if gather else ins[i].at[peer], dst_ref=outs[gi].at[li, me],
                    send_sem=send_sems.at[i, k - 1], recv_sem=recv_sems.at[i, k - 1],
                    device_id=(px, py, pc), device_id_type=pl.DeviceIdType.MESH)
                cp.start()
                landing = pltpu.make_async_remote_copy(
                    src_ref=ins[i] if gather else ins[i].at[peer], dst_ref=outs[gi].at[li, peer],
                    send_sem=send_sems.at[i, k - 1], recv_sem=recv_sems.at[i, k - 1],
                    device_id=(px, py, pc), device_id_type=pl.DeviceIdType.MESH)
                waits.append((cp, landing))
        for cp, landing in waits:
            cp.wait_send()
            landing.wait_recv()
        for own in pending:
            own.wait()

    anyspec = pl.BlockSpec(memory_space=pl.ANY)
    out_shape = [jax.ShapeDtypeStruct((len(grp), NDEV) + tuple(grp[0].shape[-2:]), grp[0].dtype) for grp in groups]
    return pl.pallas_call(
        body, name=name, out_shape=out_shape, in_specs=[anyspec] * n, out_specs=[anyspec] * len(groups),
        scratch_shapes=[pltpu.SemaphoreType.DMA((n, NDEV - 1)), pltpu.SemaphoreType.DMA((n, NDEV - 1)),
                        pltpu.SemaphoreType.DMA((n,))],
        compiler_params=pltpu.CompilerParams(has_side_effects=True),
    )(*[a for (_, _, a) in flat])


def _adamw(slots, w, m, v, name):
    L, _, R, C = slots.shape
    tc = _tile(C, 2048)
    tr = _tile(R, max(8, ADAM_TILE_BYTES // (slots.dtype.itemsize * tc)), mult=8)
    c1, c2 = 1.0 - ADAM_B1 ** ADAM_STEP, 1.0 - ADAM_B2 ** ADAM_STEP

    def body(s_ref, w_ref, m_ref, v_ref, g_out, d_out, m_out, v_out):
        g = s_ref[0, 0].astype(F32)
        for k in range(1, NDEV):
            g = g + s_ref[0, k].astype(F32)
        m_new = ADAM_B1 * m_ref[0] + (1.0 - ADAM_B1) * g
        v_new = ADAM_B2 * v_ref[0] + (1.0 - ADAM_B2) * (g * g)
        g_out[0] = g
        m_out[0] = m_new
        v_out[0] = v_new
        d_out[0] = -ADAM_LR * ((m_new / c1) / (jnp.sqrt(v_new / c2) + ADAM_EPS) + ADAM_WD * w_ref[0])

    spec = pl.BlockSpec((1, tr, tc), lambda l, i, j: (l, i, j))
    return pl.pallas_call(
        body, name=name, grid=(L, R // tr, C // tc), out_shape=[jax.ShapeDtypeStruct((L, R, C), F32)] * 4,
        in_specs=[pl.BlockSpec((1, NDEV, tr, tc), lambda l, i, j: (l, 0, i, j)), spec, spec, spec],
        out_specs=[spec] * 4,
        compiler_params=pltpu.CompilerParams(dimension_semantics=("parallel", "parallel", "parallel"),
                                             vmem_limit_bytes=VMEM_LIMIT),
    )(slots, w, m, v)


class _Cfg:
    def __init__(self, S, D, groups, q_lora, kv_lora, c_heads, d_mix):
        self.S, self.D, self.G, self.Q, self.KV, self.Hc, self.DMIX = S, D, groups, q_lora, kv_lora, c_heads, d_mix
        self.A, self.C = groups * LANE, c_heads * LANE
        self.B = d_mix - self.A - self.C
        self.Hb = self.B // LANE
        A, B, C = self.A, self.B, self.C
        assert B % LANE == 0 and B % C == 0 and (B + C) % A == 0
        self.ref_segs = [("ua", A), ("va", A), ("za", A), ("qb", B), ("kb", B), ("vb", B), ("zb", B),
                         ("cq", q_lora), ("ckv", kv_lora), ("kr", ROPE), ("zc", C)]
        self.off, off = {}, 0
        for nm, w in [("ua", A), ("va", A), ("za", A), ("qb", B), ("kb", B), ("vb", B), ("zb", B), ("zc", C),
                      ("cq", q_lora), ("kr", LANE), ("ckv", kv_lora)]:
            off = -(-off // w) * w
            self.off[nm] = off
            off += w
        self.NP = -(-off // 512) * 512
        self.width = {"kr": LANE, **{nm: w for nm, w in self.ref_segs if nm != "kr"}}

    def view(self, arr, nm):
        w = self.width[nm]
        return (arr, w, self.off[nm] // w)

    def heads_view(self, arr, nm):
        return (arr, self.off[nm] // LANE, LANE)


def _pad_w_in(cfg, w):
    pieces, start = {}, 0
    for nm, width in cfg.ref_segs:
        pieces[nm] = w[:, start:start + width]
        start += width
    cols, pos = [], 0
    for nm, off in sorted(cfg.off.items(), key=lambda kv: kv[1]):
        if off > pos:
            cols.append(jnp.zeros((w.shape[0], off - pos), w.dtype))
        cols.append(pieces[nm])
        pos = off + pieces[nm].shape[1]
    if cfg.NP > pos:
        cols.append(jnp.zeros((w.shape[0], cfg.NP - pos), w.dtype))
    return jnp.concatenate(cols, axis=1)


def _unpad_w_in(cfg, wp):
    return jnp.concatenate([wp[:, cfg.off[nm]:cfg.off[nm] + width] for nm, width in cfg.ref_segs], axis=1)


def _to_slots_cols(w):
    R = w.shape[0]
    return w.reshape(R, NDEV, -1).transpose(1, 0, 2)


def _from_slots_cols(s):
    return s.transpose(1, 0, 2).reshape(s.shape[1], -1)


def _perm_rows_out(cfg, w):
    return jnp.concatenate([w[cfg.A:], w[:cfg.A]], axis=0)


def _unperm_rows_out(cfg, w):
    return jnp.concatenate([w[cfg.B + cfg.C:], w[:cfg.B + cfg.C]], axis=0)


def _layer_params(cfg, l, g_pre, a_g_v, a_w_s, a_b_s, c_g_q, c_g_kv, g_out):
    A, B = cfg.A, cfg.B
    return dict(g_pre=g_pre[l][None], g_v=a_g_v[l].reshape(1, A), w_s=a_w_s[l], b_s=a_b_s[l][:, :, None],
                g_q=c_g_q[l][None], g_kv=c_g_kv[l][None],
                g_oa=g_out[l][None, :A], g_ob=g_out[l][None, A:A + B], g_oc=g_out[l][None, A + B:])


def _layer_fwd(cfg, l, x, W, p, cos2, sin2, rot):
    S, D, A, B, C = cfg.S, cfg.D, cfg.A, cfg.B, cfg.C
    tag = f"l{l}"
    (h,) = _rowwise(_f_pre, [(x, D, 0)], [], [p["g_pre"]], [], [(D, BF16)], 256, f"pre_{tag}")
    proj = _matmul(h, W["in"], "nn", BF16, f"mm_in_{tag}")
    a_rows = [cfg.view(proj, "ua"), cfg.view(proj, "va"), cfg.view(proj, "za")]
    a_par = [p["g_v"], p["w_s"], p["b_s"], p["g_oa"]]
    (ya,) = _rowwise(_f_gmlp, a_rows, [], a_par, [], [(A, BF16)], LANE, f"gmlp_{tag}")
    qb, kb, vb = cfg.heads_view(proj, "qb"), cfg.heads_view(proj, "kb"), cfg.heads_view(proj, "vb")
    yb, tot = _stick_fwd(qb, kb, vb, cfg.Hb, LANE ** -0.5, f"stick_fwd_{tag}", 256, 256)
    (ybg,) = _rowwise(_f_gate, [(yb, B, 0), cfg.view(proj, "zb")], [], [p["g_ob"]], [], [(B, BF16)], 256, f"gate_b_{tag}")
    c_rows = [cfg.view(proj, "cq"), cfg.view(proj, "ckv"), cfg.view(proj, "kr")]
    trig = [(cos2, LANE, 0), (sin2, LANE, 0)]
    cqn, ckvn, krr = _rowwise(_f_cpre, c_rows, trig, [p["g_q"], p["g_kv"]], [rot],
                              [(cfg.Q, BF16), (cfg.KV, BF16), (LANE, BF16)], 256, f"cpre_{tag}")
    q_raw = _matmul(cqn, W["uq"], "nn", BF16, f"mm_uq_{tag}")
    kv = _matmul(ckvn, W["ukv"], "nn", BF16, f"mm_ukv_{tag}")
    r_rows = [(q_raw, 2 * C, 0), (kv, 2 * C, 0), (krr, LANE, 0)]
    q_rot, k_full, v_c = _rowwise(_f_crope, r_rows, trig, [], [rot], [(2 * C, BF16), (2 * C, BF16), (C, BF16)], 128,
                                  f"crope_{tag}")
    qc, kc, vc = (q_rot, 0, 2 * LANE), (k_full, 0, 2 * LANE), (v_c, 0, LANE)
    yc, lse = _softmax_fwd(qc, kc, vc, cfg.Hc, (LANE + ROPE) ** -0.5, f"mla_fwd_{tag}", 256, 256)
    (ycg,) = _rowwise(_f_gate, [(yc, C, 0), cfg.view(proj, "zc")], [], [p["g_oc"]], [], [(C, BF16)], 256, f"gate_c_{tag}")
    y = jnp.concatenate([ybg, ycg, ya], axis=1)
    out = _matmul(y, W["out"], "nn", F32, f"mm_out_{tag}", add=x)
    saved = dict(x=x, h=h, proj=proj, yb=yb, tot=tot, cqn=cqn, ckvn=ckvn, krr=krr, q_raw=q_raw, kv=kv,
                 q_rot=q_rot, k_full=k_full, v_c=v_c, yc=yc, lse=lse, y=y)
    return out, saved


def _layer_bwd(cfg, l, dout, sv, W, p, cos2, sin2, rot):
    S, D, A, B, C = cfg.S, cfg.D, cfg.A, cfg.B, cfg.C
    tag = f"l{l}"
    proj = sv["proj"]
    dy = _matmul(dout, W["out"], "nt", BF16, f"mm_dy_{tag}")
    d_wout = _matmul(sv["y"], dout, "tn", BF16, f"mm_dwout_{tag}")
    (dyb, dzb), (dg_ob,), _ = _rowwise_vjp(_f_gate, [(sv["yb"], B, 0), cfg.view(proj, "zb")], [], [p["g_ob"]], [],
                                           [(dy, B, 0)], [BF16, BF16], 256, f"gate_b_bwd_{tag}")
    (dyc, dzc), (dg_oc,), _ = _rowwise_vjp(_f_gate, [(sv["yc"], C, 0), cfg.view(proj, "zc")], [], [p["g_oc"]], [],
                                           [(dy, C, B // C)], [BF16, BF16], 256, f"gate_c_bwd_{tag}")
    a_rows = [cfg.view(proj, "ua"), cfg.view(proj, "va"), cfg.view(proj, "za")]
    a_par = [p["g_v"], p["w_s"], p["b_s"], p["g_oa"]]
    (dua, dva, dza), (dg_v, dw_s, db_s, dg_oa), _ = _rowwise_vjp(
        _f_gmlp, a_rows, [], a_par, [], [(dy, A, (B + C) // A)], [BF16] * 3, LANE, f"gmlp_bwd_{tag}")
    qb, kb, vb = cfg.heads_view(proj, "qb"), cfg.heads_view(proj, "kb"), cfg.heads_view(proj, "vb")
    dqb, dkb, dvb = _stick_bwd(qb, kb, vb, (dyb, 0, LANE), sv["tot"], cfg.Hb, LANE ** -0.5, f"stick_bwd_{tag}", 256, 256)
    qc, kc, vc = (sv["q_rot"], 0, 2 * LANE), (sv["k_full"], 0, 2 * LANE), (sv["v_c"], 0, LANE)
    dq_rot, dk_full, dv_c = _softmax_bwd(qc, kc, vc, (sv["yc"], 0, LANE), (dyc, 0, LANE), sv["lse"], cfg.Hc,
                                         (LANE + ROPE) ** -0.5, f"mla_bwd_{tag}", 256, 256)
    trig = [(cos2, LANE, 0), (sin2, LANE, 0)]
    r_rows = [(sv["q_raw"], 2 * C, 0), (sv["kv"], 2 * C, 0), (sv["krr"], LANE, 0)]
    (dq_raw, dkv, dkrr), _, _ = _rowwise_vjp(_f_crope, r_rows, trig, [], [rot],
                                             [(dq_rot, 2 * C, 0), (dk_full, 2 * C, 0), (dv_c, C, 0)], [BF16] * 3, 128,
                                             f"crope_bwd_{tag}")
    dcqn = _matmul(dq_raw, W["uq"], "nt", BF16, f"mm_dcq_{tag}")
    d_wuq = _matmul(sv["cqn"], dq_raw, "tn", BF16, f"mm_dwuq_{tag}")
    dckvn = _matmul(dkv, W["ukv"], "nt", BF16, f"mm_dckv_{tag}")
    d_wukv = _matmul(sv["ckvn"], dkv, "tn", BF16, f"mm_dwukv_{tag}")
    c_rows = [cfg.view(proj, "cq"), cfg.view(proj, "ckv"), cfg.view(proj, "kr")]
    (dcq, dckv, dkr), (dg_q, dg_kv), _ = _rowwise_vjp(
        _f_cpre, c_rows, trig, [p["g_q"], p["g_kv"]], [rot],
        [(dcqn, cfg.Q, 0), (dckvn, cfg.KV, 0), (dkrr, LANE, 0)], [BF16] * 3, 256, f"cpre_bwd_{tag}")
    parts = dict(ua=dua, va=dva, za=dza, qb=dqb, kb=dkb, vb=dvb, zb=dzb, zc=dzc, cq=dcq, kr=dkr, ckv=dckv)
    cols, pos = [], 0
    for nm, off in sorted(cfg.off.items(), key=lambda kv_: kv_[1]):
        if off > pos:
            cols.append(jnp.zeros((S, off - pos), BF16))
        cols.append(parts[nm])
        pos = off + parts[nm].shape[1]
    if cfg.NP > pos:
        cols.append(jnp.zeros((S, cfg.NP - pos), BF16))
    dproj = jnp.concatenate(cols, axis=1)
    dh = _matmul(dproj, W["in"], "nt", BF16, f"mm_dh_{tag}")
    d_win = _matmul(sv["h"], dproj, "tn", BF16, f"mm_dwin_{tag}")
    (dx,), (dg_pre,), _ = _rowwise_vjp(_f_pre_res, [(sv["x"], D, 0)], [], [p["g_pre"]], [],
                                       [(dh, D, 0), (dout, D, 0)], [F32], 128, f"pre_bwd_{tag}")
    small = dict(g_pre=dg_pre[0], a_g_v=dg_v.reshape(cfg.G, LANE), a_w_s=dw_s, a_b_s=db_s[:, :, 0], c_g_q=dg_q[0],
                 c_g_kv=dg_kv[0], g_out=jnp.concatenate([dg_oa[0], dg_ob[0], dg_oc[0]]))
    big = dict(w_in=_to_slots_cols(_unpad_w_in(cfg, d_win)),
               c_w_uq=_to_slots_cols(d_wuq.reshape(cfg.Q, cfg.Hc, 2 * LANE)[:, :, :LANE + ROPE].reshape(cfg.Q, -1)),
               c_w_ukv=_to_slots_cols(d_wukv),
               w_out=_unperm_rows_out(cfg, d_wout).reshape(NDEV, cfg.DMIX // NDEV, D))
    return dx, small, big


def _pack_small(vals):
    packed = jnp.concatenate([vals[nm].reshape(-1, LANE) for nm in SMALL], axis=0)
    return jnp.pad(packed, ((0, -packed.shape[0] % SMALL_ROWS), (0, 0)))


def _unpack_small(packed, like):
    out, row = {}, 0
    for nm in SMALL:
        n = like[nm].size // LANE
        out[nm] = packed[row:row + n].reshape(like[nm].shape)
        row += n
    return out


def kernel(x, positions, g_pre, w_in, a_g_v, a_w_s, a_b_s, c_g_q, c_g_kv, c_w_uq, c_w_ukv, g_out, w_out, g_final, loss_target, m_g_pre, m_w_in, m_a_g_v, m_a_w_s, m_a_b_s, m_c_g_q, m_c_g_kv, m_c_w_uq, m_c_w_ukv, m_g_out, m_w_out, m_g_final, v_g_pre, v_w_in, v_a_g_v, v_a_w_s, v_a_b_s, v_c_g_q, v_c_g_kv, v_c_w_uq, v_c_w_ukv, v_g_out, v_w_out, v_g_final):
    depth, S, D = w_in.shape[0], x.shape[1], x.shape[2]
    cfg = _Cfg(S, D, a_g_v.shape[1], c_g_q.shape[1], c_g_kv.shape[1], c_w_ukv.shape[2] * NDEV // (2 * LANE), g_out.shape[1])
    weights = dict(g_pre=g_pre, w_in=w_in, a_g_v=a_g_v, a_w_s=a_w_s, a_b_s=a_b_s, c_g_q=c_g_q, c_g_kv=c_g_kv,
                   c_w_uq=c_w_uq, c_w_ukv=c_w_ukv, g_out=g_out, w_out=w_out, g_final=g_final)
    mom_m = dict(g_pre=m_g_pre, w_in=m_w_in, a_g_v=m_a_g_v, a_w_s=m_a_w_s, a_b_s=m_a_b_s, c_g_q=m_c_g_q, c_g_kv=m_c_g_kv,
                 c_w_uq=m_c_w_uq, c_w_ukv=m_c_w_ukv, g_out=m_g_out, w_out=m_w_out, g_final=m_g_final)
    mom_v = dict(g_pre=v_g_pre, w_in=v_w_in, a_g_v=v_a_g_v, a_w_s=v_a_w_s, a_b_s=v_a_b_s, c_g_q=v_c_g_q, c_g_kv=v_c_g_kv,
                 c_w_uq=v_c_w_uq, c_w_ukv=v_c_w_ukv, g_out=v_g_out, w_out=v_w_out, g_final=v_g_final)
    big_names = ("w_in", "c_w_uq", "c_w_ukv", "w_out")

    inv_freq = 1.0 / (ROPE_THETA ** (jnp.arange(0, ROPE, 2, dtype=F32) / ROPE))
    ang = positions[0].astype(F32)[:, None] * inv_freq
    zpad = jnp.zeros((S, LANE - ROPE), F32)
    cos2 = jnp.concatenate([jnp.cos(ang), jnp.cos(ang), zpad], axis=1)
    sin2 = jnp.concatenate([jnp.sin(ang), jnp.sin(ang), zpad], axis=1)
    rot = _rope_matrix()

    gathered = _exchange([[weights[nm][l].astype(BF16) for l in range(depth)] for nm in big_names], True, "gather_weights")
    Ws = []
    for l in range(depth):
        g_in, g_uq, g_ukv, g_wout = (g[l] for g in gathered)
        uq = _from_slots_cols(g_uq).reshape(cfg.Q, cfg.Hc, LANE + ROPE)
        uq = jnp.pad(uq, ((0, 0), (0, 0), (0, LANE - ROPE))).reshape(cfg.Q, 2 * cfg.C)
        Ws.append({"in": _pad_w_in(cfg, _from_slots_cols(g_in)), "uq": uq, "ukv": _from_slots_cols(g_ukv),
                   "out": _perm_rows_out(cfg, g_wout.reshape(cfg.DMIX, D))})
    params = [_layer_params(cfg, l, g_pre, a_g_v, a_w_s, a_b_s, c_g_q, c_g_kv, g_out) for l in range(depth)]

    hcur, saved = x[0], []
    for l in range(depth):
        hcur, sv = _layer_fwd(cfg, l, hcur, Ws[l], params[l], cos2, sin2, rot)
        saved.append(sv)
    (dh,), (dg_final,), (loss_rows,) = _rowwise_vjp(
        _f_final, [(hcur, D, 0)], [(loss_target[0], D, 0)], [g_final[None]], [], [(jnp.ones((S, 1), F32), 1, 0)],
        [F32], 128, "final", primal=[(1, F32)])
    loss = lax.psum(jnp.sum(loss_rows), MESH_AXES)
    small_g, big_g = [None] * depth, [None] * depth
    for l in reversed(range(depth)):
        dh, small_g[l], big_g[l] = _layer_bwd(cfg, l, dh, saved[l], Ws[l], params[l], cos2, sin2, rot)
    grad_x = dh[None]

    slots = _exchange([[big_g[l][nm] for l in range(depth)] for nm in big_names], False, "scatter_grads")
    small_grads = {nm: jnp.stack([small_g[l][nm] for l in range(depth)]) for nm in SMALL if nm != "g_final"}
    small_grads["g_final"] = dg_final[0]
    (small_slots,) = _exchange([[_pack_small(small_grads)]], True, "gather_small_grads")

    res = {}
    for nm, sl in zip(big_names, slots):
        res[nm] = _adamw(sl, weights[nm], mom_m[nm], mom_v[nm], f"adamw_{nm}")
    packed = _adamw(small_slots, _pack_small(weights)[None], _pack_small(mom_m)[None], _pack_small(mom_v)[None], "adamw_small")
    small_res = [_unpack_small(r[0], weights) for r in packed]
    order = ("g_pre", "w_in", "a_g_v", "a_w_s", "a_b_s", "c_g_q", "c_g_kv", "c_w_uq", "c_w_ukv", "g_out", "w_out", "g_final")
    outs = [loss, grad_x]
    for kind in range(4):
        outs += [small_res[kind][nm] if nm in SMALL else res[nm][kind] for nm in order]
    return tuple(outs)
```

```python
import functools

import numpy as np
import jax
import jax.numpy as jnp
from jax import lax
from jax.experimental import pallas as pl
from jax.experimental.pallas import tpu as pltpu

NDEV = 8
MESH_AXES = ("x", "y", "c")
LANE = 128
ROPE = 64
EPS = 1e-6
ROPE_THETA = 10000.0
ADAM_LR, ADAM_B1, ADAM_B2, ADAM_EPS, ADAM_WD, ADAM_STEP = 0.001, 0.9, 0.999, 1e-08, 0.01, 10
VMEM_LIMIT = 48 * 1024 * 1024
ADAM_TILE_BYTES = 512 * 1024
SMALL_ROWS = 256
F32, BF16 = jnp.float32, jnp.bfloat16
SMALL = ("g_pre", "a_g_v", "a_w_s", "a_b_s", "c_g_q", "c_g_kv", "g_out", "g_final")


def _tile(dim, cap, mult=LANE):
    if dim <= cap:
        return dim
    t = (cap // mult) * mult
    while t >= mult:
        if dim % t == 0:
            return t
        t -= mult
    return dim


def _dot_nt(a, b):
    return lax.dot_general(a, b, (((1,), (1,)), ((), ())), preferred_element_type=F32)


def _dot_tn(a, b):
    return lax.dot_general(a, b, (((0,), (0,)), ((), ())), preferred_element_type=F32)


def _dot(a, b):
    return jnp.dot(a, b, preferred_element_type=F32)


class _Exchange:
    def __init__(self, groups, gather):
        self.groups, self.gather = groups, gather
        self.flat = [(gi, li, a) for gi, grp in enumerate(groups) for li, a in enumerate(grp)]
        self.n = len(self.flat)
        self.args = [a for (_, _, a) in self.flat]
        self.out_shape = [jax.ShapeDtypeStruct((len(grp), NDEV) + tuple(grp[0].shape[-2:]), grp[0].dtype) for grp in groups]
        self.scratch = [pltpu.SemaphoreType.DMA((self.n, NDEV - 1)), pltpu.SemaphoreType.DMA((self.n, NDEV - 1)),
                        pltpu.SemaphoreType.DMA((self.n,))]

    def _copies(self, ins, outs, send_sems, recv_sems, local_sems, landings):
        x, y, c = lax.axis_index("x"), lax.axis_index("y"), lax.axis_index("c")
        me = 4 * x + 2 * y + c
        owns = [pltpu.make_async_copy(ins[i] if self.gather else ins[i].at[me], outs[gi].at[li, me], local_sems.at[i])
                for i, (gi, li, _) in enumerate(self.flat)]
        pairs = []
        for k in range(1, NDEV):
            px = 1 - x if k & 4 else x
            py = 1 - y if k & 2 else y
            pc = 1 - c if k & 1 else c
            peer = 4 * px + 2 * py + pc
            for i, (gi, li, _) in enumerate(self.flat):
                src = ins[i] if self.gather else ins[i].at[peer]
                sems = dict(send_sem=send_sems.at[i, k - 1], recv_sem=recv_sems.at[i, k - 1],
                            device_id=(px, py, pc), device_id_type=pl.DeviceIdType.MESH)
                out = pltpu.make_async_remote_copy(src_ref=src, dst_ref=outs[gi].at[li, me], **sems)
                landing = pltpu.make_async_remote_copy(src_ref=src, dst_ref=outs[gi].at[li, peer], **sems) if landings else None
                pairs.append((out, landing))
        return owns, pairs

    def start(self, ins, outs, sems):
        owns, pairs = self._copies(ins, outs, *sems, landings=False)
        for own in owns:
            own.start()
        for out, _ in pairs:
            out.start()

    def wait(self, ins, outs, sems):
        owns, pairs = self._copies(ins, outs, *sems, landings=True)
        for out, landing in pairs:
            out.wait_send()
            landing.wait_recv()
        for own in owns:
            own.wait()


def _call(body, name, grid, in_specs, out_specs, out_shape, scratch, semantics, args, carry=None):
    n_in, n_out, n_scr = len(in_specs), len(out_specs), len(scratch)
    if carry is None:
        run = body
    else:
        semantics = ("arbitrary",) * len(grid)
        anyspec = pl.BlockSpec(memory_space=pl.ANY)
        in_specs = list(in_specs) + [anyspec] * carry.n
        out_specs = list(out_specs) + [anyspec] * len(carry.groups)
        out_shape = list(out_shape) + carry.out_shape
        scratch = list(scratch) + carry.scratch
        args = list(args) + carry.args

        def run(*refs):
            c_in, x_in = refs[:n_in], refs[n_in:n_in + carry.n]
            rest = refs[n_in + carry.n:]
            c_out, x_out = rest[:n_out], rest[n_out:n_out + len(carry.groups)]
            c_scr, sems = rest[n_out + len(carry.groups):len(rest) - 3], rest[len(rest) - 3:]
            first, last = True, True
            for d, extent in enumerate(grid):
                first = jnp.logical_and(first, pl.program_id(d) == 0)
                last = jnp.logical_and(last, pl.program_id(d) == extent - 1)

            @pl.when(first)
            def _():
                carry.start(x_in, x_out, sems)

            body(*c_in, *c_out, *c_scr)

            @pl.when(last)
            def _():
                carry.wait(x_in, x_out, sems)

    res = pl.pallas_call(
        run, name=name, grid=grid, out_shape=list(out_shape), in_specs=list(in_specs), out_specs=list(out_specs),
        scratch_shapes=list(scratch),
        compiler_params=pltpu.CompilerParams(dimension_semantics=semantics, vmem_limit_bytes=VMEM_LIMIT,
                                             has_side_effects=carry is not None),
    )(*args)
    return list(res[:n_out]), list(res[n_out:])


def _exchange(groups, gather, name):
    ex = _Exchange(groups, gather)

    def body(*refs):
        ins, outs, sems = refs[:ex.n], refs[ex.n:ex.n + len(groups)], refs[ex.n + len(groups):]
        ex.start(ins, outs, sems)
        ex.wait(ins, outs, sems)

    anyspec = pl.BlockSpec(memory_space=pl.ANY)
    return pl.pallas_call(
        body, name=name, out_shape=ex.out_shape, in_specs=[anyspec] * ex.n, out_specs=[anyspec] * len(groups),
        scratch_shapes=ex.scratch, compiler_params=pltpu.CompilerParams(has_side_effects=True),
    )(*ex.args)


def _matmul(a, b, mode, out_dtype, name, add=None, tm=1024, tn=1024, tk=1024, carry=None):
    if mode == "tn":
        (K, M), (K2, N) = a.shape, b.shape
    elif mode == "nt":
        (M, K), (N, K2) = a.shape, b.shape
    else:
        (M, K), (K2, N) = a.shape, b.shape
    assert K == K2, (a.shape, b.shape, mode)
    tm, tn, tk = _tile(M, tm), _tile(N, tn), _tile(K, tk)
    nk = K // tk
    a_spec = pl.BlockSpec((tk, tm), lambda i, j, k: (k, i)) if mode == "tn" else pl.BlockSpec((tm, tk), lambda i, j, k: (i, k))
    b_spec = pl.BlockSpec((tn, tk), lambda i, j, k: (j, k)) if mode == "nt" else pl.BlockSpec((tk, tn), lambda i, j, k: (k, j))
    dot = {"nn": _dot, "nt": _dot_nt, "tn": _dot_tn}[mode]
    has_add = add is not None

    def body(*refs):
        a_ref, b_ref = refs[0], refs[1]
        o_ref, acc = refs[-2], refs[-1]
        k = pl.program_id(2)

        @pl.when(k == 0)
        def _():
            acc[...] = jnp.zeros_like(acc)

        acc[...] += dot(a_ref[...].astype(BF16), b_ref[...].astype(BF16))

        @pl.when(k == nk - 1)
        def _():
            r = acc[...]
            if has_add:
                r = r + refs[2][...]
            o_ref[...] = r.astype(o_ref.dtype)

    in_specs = [a_spec, b_spec]
    args = [a, b]
    if has_add:
        in_specs.append(pl.BlockSpec((tm, tn), lambda i, j, k: (i, j)))
        args.append(add)
    (out,), moved = _call(body, name, (M // tm, N // tn, nk), in_specs, [pl.BlockSpec((tm, tn), lambda i, j, k: (i, j))],
                          [jax.ShapeDtypeStruct((M, N), out_dtype)], [pltpu.VMEM((tm, tn), F32)],
                          ("parallel", "parallel", "arbitrary"), args, carry)
    return out if carry is None else (out, moved)


def _row_specs(views, tile):
    return [pl.BlockSpec((tile, w), functools.partial(lambda i, cb: (i, cb), cb=cb)) for (_, w, cb) in views]


def _full_specs(arrs):
    return [pl.BlockSpec(p.shape, functools.partial(lambda i, nd: (0,) * nd, nd=p.ndim)) for p in arrs]


def _rowwise(fn, rows, aux, params, consts, outs, tile, name):
    S = rows[0][0].shape[0]
    nr, na, npar, nc = len(rows), len(aux), len(params), len(consts)

    def body(*refs):
        ins = [r[...].astype(F32) for r in refs[:nr + na]]
        small = [r[...] for r in refs[nr + na:nr + na + npar + nc]]
        res = fn(*ins, *small)
        for o_ref, r in zip(refs[nr + na + npar + nc:], res):
            o_ref[...] = r.astype(o_ref.dtype)

    return pl.pallas_call(
        body, name=name, grid=(S // tile,),
        out_shape=[jax.ShapeDtypeStruct((S, w), dt) for (w, dt) in outs],
        in_specs=_row_specs(rows + aux, tile) + _full_specs(params + consts),
        out_specs=[pl.BlockSpec((tile, w), lambda i: (i, 0)) for (w, _) in outs],
        compiler_params=pltpu.CompilerParams(dimension_semantics=("parallel",), vmem_limit_bytes=VMEM_LIMIT),
    )(*[v[0] for v in rows + aux], *params, *consts)


def _rowwise_vjp(fn, rows, aux, params, consts, cots, grad_dtypes, tile, name, primal=()):
    S = rows[0][0].shape[0]
    nr, na, npar, nc, nct, npr = len(rows), len(aux), len(params), len(consts), len(cots), len(primal)

    def body(*refs):
        n_in = nr + na + npar + nc + nct
        rv = [r[...].astype(F32) for r in refs[:nr]]
        av = [r[...].astype(F32) for r in refs[nr:nr + na]]
        pv = [r[...] for r in refs[nr + na:nr + na + npar]]
        cv = [r[...] for r in refs[nr + na + npar:nr + na + npar + nc]]
        ct = tuple(r[...].astype(F32) for r in refs[nr + na + npar + nc:n_in])
        res, vjp = jax.vjp(lambda *rp: tuple(fn(*rp[:nr], *av, *rp[nr:], *cv)), *rv, *pv)
        grads = vjp(ct)
        g_refs = refs[n_in:n_in + nr]
        p_refs = refs[n_in + nr:n_in + nr + npar]
        o_refs = refs[n_in + nr + npar:]
        for g_ref, g in zip(g_refs, grads[:nr]):
            g_ref[...] = g.astype(g_ref.dtype)

        @pl.when(pl.program_id(0) == 0)
        def _():
            for p_ref in p_refs:
                p_ref[...] = jnp.zeros_like(p_ref)

        for p_ref, g in zip(p_refs, grads[nr:]):
            p_ref[...] += g
        for o_ref, r in zip(o_refs, res[:npr]):
            o_ref[...] = r.astype(o_ref.dtype)

    out_shape = ([jax.ShapeDtypeStruct((S, w), dt) for (_, w, _), dt in zip(rows, grad_dtypes)]
                 + [jax.ShapeDtypeStruct(p.shape, F32) for p in params]
                 + [jax.ShapeDtypeStruct((S, w), dt) for (w, dt) in primal])
    out_specs = ([pl.BlockSpec((tile, w), lambda i: (i, 0)) for (_, w, _) in rows] + _full_specs(params)
                 + [pl.BlockSpec((tile, w), lambda i: (i, 0)) for (w, _) in primal])
    res = pl.pallas_call(
        body, name=name, grid=(S // tile,), out_shape=out_shape,
        in_specs=_row_specs(rows + aux, tile) + _full_specs(params + consts) + _row_specs(cots, tile),
        out_specs=out_specs,
        compiler_params=pltpu.CompilerParams(dimension_semantics=("arbitrary",), vmem_limit_bytes=VMEM_LIMIT),
    )(*[v[0] for v in rows + aux], *params, *consts, *[v[0] for v in cots])
    return res[:nr], res[nr:nr + npar], res[nr + npar:]


@jax.custom_vjp
def _mm(a, b):
    return _dot(a.astype(BF16), b.astype(BF16))


def _mm_fwd(a, b):
    return _mm(a, b), (a, b)


def _mm_bwd(res, ct):
    a, b = res
    ctb = ct.astype(BF16)
    return _dot_nt(ctb, b.astype(BF16)), _dot_tn(a.astype(BF16), ctb)


_mm.defvjp(_mm_fwd, _mm_bwd)


def _rms(x, g):
    return x * lax.rsqrt(jnp.mean(x * x, axis=-1, keepdims=True) + EPS) * g


def _f_pre(x, g):
    return (_rms(x, g),)


def _f_pre_res(x, g):
    return _rms(x, g), x


def _f_gate(y, z, g):
    return (_rms(y, g) * jax.nn.silu(z),)


def _f_gmlp(u, v, z, g_v, w_s, b_s, g_o):
    groups = w_s.shape[0]
    u, v = jax.nn.gelu(u), jax.nn.gelu(v)
    t_idx = lax.broadcasted_iota(jnp.int32, (LANE, LANE), 0)
    s_idx = lax.broadcasted_iota(jnp.int32, (LANE, LANE), 1)
    ys = []
    for g in range(groups):
        sl = slice(g * LANE, (g + 1) * LANE)
        vn = _rms(v[:, sl], g_v[:, sl])
        w = jnp.where(s_idx <= t_idx, w_s[g], 0.0)
        ys.append(u[:, sl] * (_mm(w, vn) + b_s[g]))
    return (_rms(jnp.concatenate(ys, axis=1), g_o) * jax.nn.silu(z),)


def _rope(x, cos2, sin2, rot):
    return x * cos2 + _mm(x, rot) * sin2


def _f_cpre(cq, ckv, kr, cos2, sin2, g_q, g_kv, rot):
    return _rms(cq, g_q), _rms(ckv, g_kv), _rope(kr, cos2, sin2, rot)


def _f_crope(q, kv, krr, cos2, sin2, rot):
    heads = q.shape[1] // (2 * LANE)
    qs, ks, vs = [], [], []
    for h in range(heads):
        lo, mid, hi = 2 * h * LANE, (2 * h + 1) * LANE, (2 * h + 2) * LANE
        qs += [q[:, lo:mid], _rope(q[:, mid:hi], cos2, sin2, rot)]
        ks += [kv[:, lo:mid], krr]
        vs += [kv[:, mid:hi]]
    return jnp.concatenate(qs, axis=1), jnp.concatenate(ks, axis=1), jnp.concatenate(vs, axis=1)


def _f_final(h, target, g):
    err = _rms(h, g) - target
    return (0.5 * jnp.mean(err * err, axis=-1, keepdims=True),)


def _rope_matrix():
    r = np.zeros((LANE, LANE), np.float32)
    half = ROPE // 2
    for i in range(half):
        r[i + half, i] = -1.0
        r[i, i + half] = 1.0
    return jnp.asarray(r)


def _head_spec(view, rows, n_rows_block):
    _, cb0, w = view
    if n_rows_block:
        return pl.BlockSpec((rows, w), functools.partial(lambda h, i, cb0: (i, cb0 + h), cb0=cb0))
    return pl.BlockSpec((rows, w), functools.partial(lambda h, i, cb0: (0, cb0 + h), cb0=cb0))


def _stat_spec(tq):
    return pl.BlockSpec((1, tq, 1), lambda h, i: (h, i, 0))


def _softplus(z):
    return jnp.maximum(z, 0.0) + jnp.log(1.0 + jnp.exp(-jnp.abs(z)))


def _cumsum_mm(x, m01):
    hi = x.astype(BF16)
    lo = (x - hi.astype(F32)).astype(BF16)
    return _dot(hi, m01) + _dot(lo, m01)


def _attn_call(body, name, heads, S, tq, ins, in_blocked, outs, out_blocked, scratch, stats_in=0, stats_out=0, carry=None):
    in_specs = [_head_spec(v, tq if blk else S, blk) for v, blk in zip(ins[:len(ins) - stats_in], in_blocked)]
    in_specs += [_stat_spec(tq)] * stats_in
    out_specs = [_head_spec((None, 0, w), tq if blk else S, blk) for (w, _), blk in zip(outs, out_blocked)]
    out_specs += [_stat_spec(tq)] * stats_out
    out_shape = [jax.ShapeDtypeStruct((S, heads * w), dt) for (w, dt) in outs]
    out_shape += [jax.ShapeDtypeStruct((heads, S, 1), F32)] * stats_out
    args = [v[0] for v in ins[:len(ins) - stats_in]] + list(ins[len(ins) - stats_in:])
    res, moved = _call(body, name, (heads, S // tq), in_specs, out_specs, out_shape, scratch, ("arbitrary", "arbitrary"),
                       args, carry)
    return res if carry is None else res + [moved]


def _softmax_fwd(q, k, v, heads, scale, name, tq, bk, carry=None):
    S, dv = q[0].shape[0], v[2]

    def body(q_ref, k_ref, v_ref, o_ref, lse_ref):
        qi = pl.program_id(1)
        qv = q_ref[...]
        row = qi * tq + lax.broadcasted_iota(jnp.int32, (tq, bk), 0)
        col0 = lax.broadcasted_iota(jnp.int32, (tq, bk), 1)

        def step(kb, carry):
            m, l, acc = carry
            sl = pl.ds(pl.multiple_of(kb * bk, bk), bk)
            s = _dot_nt(qv, k_ref[sl, :]) * scale
            s = jnp.where(kb * bk + col0 <= row, s, -1e30)
            m_new = jnp.maximum(m, jnp.max(s, axis=1, keepdims=True))
            p = jnp.exp(s - m_new)
            alpha = jnp.exp(m - m_new)
            l = alpha * l + jnp.sum(p, axis=1, keepdims=True)
            acc = alpha * acc + _dot(p.astype(BF16), v_ref[sl, :])
            return m_new, l, acc

        n_kb = (qi * tq + tq + bk - 1) // bk
        m, l, acc = lax.fori_loop(0, n_kb, step, (jnp.full((tq, 1), -1e30, F32), jnp.zeros((tq, 1), F32),
                                                  jnp.zeros((tq, dv), F32)))
        o_ref[...] = (acc / l).astype(o_ref.dtype)
        lse_ref[0] = m + jnp.log(l)

    return _attn_call(body, name, heads, S, tq, [q, k, v], [1, 0, 0], [(dv, BF16)], [1], [], stats_out=1, carry=carry)


def _softmax_bwd(q, k, v, o, do, lse, heads, scale, name, tq, bk, carry=None):
    S, dq_w, dv = q[0].shape[0], q[2], v[2]
    nq = S // tq

    def body(q_ref, k_ref, v_ref, o_ref, do_ref, lse_ref, dq_ref, dk_ref, dv_ref, dk_acc, dv_acc):
        qi = pl.program_id(1)

        @pl.when(qi == 0)
        def _():
            dk_acc[...] = jnp.zeros_like(dk_acc)
            dv_acc[...] = jnp.zeros_like(dv_acc)

        qv, dov = q_ref[...], do_ref[...]
        delta = jnp.sum(dov.astype(F32) * o_ref[...].astype(F32), axis=1, keepdims=True)
        lse_v = lse_ref[0]
        row = qi * tq + lax.broadcasted_iota(jnp.int32, (tq, bk), 0)
        col0 = lax.broadcasted_iota(jnp.int32, (tq, bk), 1)

        def step(kb, dq):
            sl = pl.ds(pl.multiple_of(kb * bk, bk), bk)
            ks, vs = k_ref[sl, :], v_ref[sl, :]
            s = _dot_nt(qv, ks) * scale
            p = jnp.where(kb * bk + col0 <= row, jnp.exp(s - lse_v), 0.0)
            ds = (p * (_dot_nt(dov, vs) - delta) * scale).astype(BF16)
            dk_acc[sl, :] += _dot_tn(ds, qv)
            dv_acc[sl, :] += _dot_tn(p.astype(BF16), dov)
            return dq + _dot(ds, ks)

        n_kb = (qi * tq + tq + bk - 1) // bk
        dq_ref[...] = lax.fori_loop(0, n_kb, step, jnp.zeros((tq, dq_w), F32)).astype(dq_ref.dtype)

        @pl.when(qi == nq - 1)
        def _():
            dk_ref[...] = dk_acc[...].astype(dk_ref.dtype)
            dv_ref[...] = dv_acc[...].astype(dv_ref.dtype)

    return _attn_call(body, name, heads, S, tq, [q, k, v, o, do, lse], [1, 0, 0, 1, 1],
                      [(dq_w, BF16), (dq_w, BF16), (dv, BF16)], [1, 0, 0],
                      [pltpu.VMEM((S, dq_w), F32), pltpu.VMEM((S, dv), F32)], stats_in=1, carry=carry)


def _stick_fwd(q, k, v, heads, scale, name, tq, bk, carry=None):
    S, dv = q[0].shape[0], v[2]

    def body(q_ref, k_ref, v_ref, o_ref, tot_ref):
        qi = pl.program_id(1)
        qv = q_ref[...]
        row = qi * tq + lax.broadcasted_iota(jnp.int32, (tq, bk), 0)
        col0 = lax.broadcasted_iota(jnp.int32, (tq, bk), 1)
        m_gt = (lax.broadcasted_iota(jnp.int32, (bk, bk), 0) > lax.broadcasted_iota(jnp.int32, (bk, bk), 1)).astype(BF16)
        n_kb = (qi * tq + tq + bk - 1) // bk

        def step(it, carry):
            c, acc = carry
            kb = n_kb - 1 - it
            sl = pl.ds(pl.multiple_of(kb * bk, bk), bk)
            z = _dot_nt(qv, k_ref[sl, :]) * scale
            mask = kb * bk + col0 < row
            sp = _softplus(z)
            lk = jnp.where(mask, -sp, 0.0)
            after = _cumsum_mm(lk, m_gt) + c
            a = jnp.where(mask, jnp.exp(z - sp + after), 0.0)
            acc = acc + _dot(a.astype(BF16), v_ref[sl, :])
            return c + jnp.sum(lk, axis=1, keepdims=True), acc

        c, acc = lax.fori_loop(0, n_kb, step, (jnp.zeros((tq, 1), F32), jnp.zeros((tq, dv), F32)))
        o_ref[...] = acc.astype(o_ref.dtype)
        tot_ref[0] = c

    return _attn_call(body, name, heads, S, tq, [q, k, v], [1, 0, 0], [(dv, BF16)], [1], [], stats_out=1, carry=carry)


def _stick_bwd(q, k, v, do, tot, heads, scale, name, tq, bk, carry=None):
    S, dq_w, dv = q[0].shape[0], q[2], v[2]
    nq = S // tq

    def body(q_ref, k_ref, v_ref, do_ref, tot_ref, dq_ref, dk_ref, dv_ref, dk_acc, dv_acc):
        qi = pl.program_id(1)

        @pl.when(qi == 0)
        def _():
            dk_acc[...] = jnp.zeros_like(dk_acc)
            dv_acc[...] = jnp.zeros_like(dv_acc)

        qv, dov = q_ref[...], do_ref[...]
        tot_v = tot_ref[0]
        row = qi * tq + lax.broadcasted_iota(jnp.int32, (tq, bk), 0)
        col0 = lax.broadcasted_iota(jnp.int32, (tq, bk), 1)
        j_idx = lax.broadcasted_iota(jnp.int32, (bk, bk), 0)
        s_idx = lax.broadcasted_iota(jnp.int32, (bk, bk), 1)
        m_le, m_lt = (j_idx <= s_idx).astype(BF16), (j_idx < s_idx).astype(BF16)

        def step(kb, carry):
            pc, gc, dq = carry
            sl = pl.ds(pl.multiple_of(kb * bk, bk), bk)
            ks, vs = k_ref[sl, :], v_ref[sl, :]
            z = _dot_nt(qv, ks) * scale
            mask = kb * bk + col0 < row
            sp = _softplus(z)
            lk = jnp.where(mask, -sp, 0.0)
            after = tot_v - pc - _cumsum_mm(lk, m_le)
            log_beta = z - sp
            a = jnp.where(mask, jnp.exp(log_beta + after), 0.0)
            g = _dot_nt(dov, vs) * a
            cg = gc + _cumsum_mm(g, m_lt)
            dz = (jnp.where(mask, g * jnp.exp(-sp) - jnp.exp(log_beta) * cg, 0.0) * scale).astype(BF16)
            dk_acc[sl, :] += _dot_tn(dz, qv)
            dv_acc[sl, :] += _dot_tn(a.astype(BF16), dov)
            return (pc + jnp.sum(lk, axis=1, keepdims=True), gc + jnp.sum(g, axis=1, keepdims=True), dq + _dot(dz, ks))

        n_kb = (qi * tq + tq + bk - 1) // bk
        zero = jnp.zeros((tq, 1), F32)
        _, _, dq = lax.fori_loop(0, n_kb, step, (zero, zero, jnp.zeros((tq, dq_w), F32)))
        dq_ref[...] = dq.astype(dq_ref.dtype)

        @pl.when(qi == nq - 1)
        def _():
            dk_ref[...] = dk_acc[...].astype(dk_ref.dtype)
            dv_ref[...] = dv_acc[...].astype(dv_ref.dtype)

    return _attn_call(body, name, heads, S, tq, [q, k, v, do, tot], [1, 0, 0, 1],
                      [(dq_w, BF16), (dq_w, BF16), (dv, BF16)], [1, 0, 0],
                      [pltpu.VMEM((S, dq_w), F32), pltpu.VMEM((S, dv), F32)], stats_in=1, carry=carry)


def _adamw(slots, w, m, v, layer, prev, name):
    _, R, C = slots.shape
    L = w.shape[0]
    tc = _tile(C, 2048)
    tr = _tile(R, max(8, ADAM_TILE_BYTES // (slots.dtype.itemsize * tc)), mult=8)
    c1, c2 = 1.0 - ADAM_B1 ** ADAM_STEP, 1.0 - ADAM_B2 ** ADAM_STEP
    n_prev = 0 if prev is None else 4

    def body(s_ref, w_ref, m_ref, v_ref, *rest):
        g_out, d_out, m_out, v_out = rest[n_prev:]
        g = s_ref[0].astype(F32)
        for k in range(1, NDEV):
            g = g + s_ref[k].astype(F32)
        m_new = ADAM_B1 * m_ref[0] + (1.0 - ADAM_B1) * g
        v_new = ADAM_B2 * v_ref[0] + (1.0 - ADAM_B2) * (g * g)
        g_out[0] = g
        m_out[0] = m_new
        v_out[0] = v_new
        d_out[0] = -ADAM_LR * ((m_new / c1) / (jnp.sqrt(v_new / c2) + ADAM_EPS) + ADAM_WD * w_ref[0])

    spec = pl.BlockSpec((1, tr, tc), lambda i, j: (layer, i, j))
    in_specs = [pl.BlockSpec((NDEV, tr, tc), lambda i, j: (0, i, j)), spec, spec, spec]
    in_specs += [pl.BlockSpec(memory_space=pl.ANY)] * n_prev
    return pl.pallas_call(
        body, name=name, grid=(R // tr, C // tc), out_shape=[jax.ShapeDtypeStruct((L, R, C), F32)] * 4,
        in_specs=in_specs, out_specs=[spec] * 4, input_output_aliases={4 + i: i for i in range(n_prev)},
        compiler_params=pltpu.CompilerParams(dimension_semantics=("parallel", "parallel"), vmem_limit_bytes=VMEM_LIMIT),
    )(slots, w, m, v, *(prev or []))


class _Cfg:
    def __init__(self, S, D, groups, q_lora, kv_lora, c_heads, d_mix):
        self.S, self.D, self.G, self.Q, self.KV, self.Hc, self.DMIX = S, D, groups, q_lora, kv_lora, c_heads, d_mix
        self.A, self.C = groups * LANE, c_heads * LANE
        self.B = d_mix - self.A - self.C
        self.Hb = self.B // LANE
        A, B, C = self.A, self.B, self.C
        assert B % LANE == 0 and B % C == 0 and (B + C) % A == 0
        self.ref_segs = [("ua", A), ("va", A), ("za", A), ("qb", B), ("kb", B), ("vb", B), ("zb", B),
                         ("cq", q_lora), ("ckv", kv_lora), ("kr", ROPE), ("zc", C)]
        self.off, off = {}, 0
        for nm, w in [("ua", A), ("va", A), ("za", A), ("qb", B), ("kb", B), ("vb", B), ("zb", B), ("zc", C),
                      ("cq", q_lora), ("kr", LANE), ("ckv", kv_lora)]:
            off = -(-off // w) * w
            self.off[nm] = off
            off += w
        self.NP = -(-off // 512) * 512
        self.width = {"kr": LANE, **{nm: w for nm, w in self.ref_segs if nm != "kr"}}

    def view(self, arr, nm):
        w = self.width[nm]
        return (arr, w, self.off[nm] // w)

    def heads_view(self, arr, nm):
        return (arr, self.off[nm] // LANE, LANE)


def _pad_w_in(cfg, w):
    pieces, start = {}, 0
    for nm, width in cfg.ref_segs:
        pieces[nm] = w[:, start:start + width]
        start += width
    cols, pos = [], 0
    for nm, off in sorted(cfg.off.items(), key=lambda kv: kv[1]):
        if off > pos:
            cols.append(jnp.zeros((w.shape[0], off - pos), w.dtype))
        cols.append(pieces[nm])
        pos = off + pieces[nm].shape[1]
    if cfg.NP > pos:
        cols.append(jnp.zeros((w.shape[0], cfg.NP - pos), w.dtype))
    return jnp.concatenate(cols, axis=1)


def _unpad_w_in(cfg, wp):
    return jnp.concatenate([wp[:, cfg.off[nm]:cfg.off[nm] + width] for nm, width in cfg.ref_segs], axis=1)


def _to_slots_cols(w):
    R = w.shape[0]
    return w.reshape(R, NDEV, -1).transpose(1, 0, 2)


def _from_slots_cols(s):
    return s.transpose(1, 0, 2).reshape(s.shape[1], -1)


def _perm_rows_out(cfg, w):
    return jnp.concatenate([w[cfg.A:], w[:cfg.A]], axis=0)


def _unperm_rows_out(cfg, w):
    return jnp.concatenate([w[cfg.B + cfg.C:], w[:cfg.B + cfg.C]], axis=0)


def _layer_params(cfg, l, g_pre, a_g_v, a_w_s, a_b_s, c_g_q, c_g_kv, g_out):
    A, B = cfg.A, cfg.B
    return dict(g_pre=g_pre[l][None], g_v=a_g_v[l].reshape(1, A), w_s=a_w_s[l], b_s=a_b_s[l][:, :, None],
                g_q=c_g_q[l][None], g_kv=c_g_kv[l][None],
                g_oa=g_out[l][None, :A], g_ob=g_out[l][None, A:A + B], g_oc=g_out[l][None, A + B:])


def _layer_fwd(cfg, l, x, W, p, cos2, sin2, rot, carry_stick=None, carry_mla=None):
    S, D, A, B, C = cfg.S, cfg.D, cfg.A, cfg.B, cfg.C
    tag = f"l{l}"
    (h,) = _rowwise(_f_pre, [(x, D, 0)], [], [p["g_pre"]], [], [(D, BF16)], 256, f"pre_{tag}")
    proj = _matmul(h, W["in"], "nn", BF16, f"mm_in_{tag}")
    a_rows = [cfg.view(proj, "ua"), cfg.view(proj, "va"), cfg.view(proj, "za")]
    a_par = [p["g_v"], p["w_s"], p["b_s"], p["g_oa"]]
    (ya,) = _rowwise(_f_gmlp, a_rows, [], a_par, [], [(A, BF16)], LANE, f"gmlp_{tag}")
    qb, kb, vb = cfg.heads_view(proj, "qb"), cfg.heads_view(proj, "kb"), cfg.heads_view(proj, "vb")
    yb, tot, *moved_stick = _stick_fwd(qb, kb, vb, cfg.Hb, LANE ** -0.5, f"stick_fwd_{tag}", 256, 256, carry=carry_stick)
    (ybg,) = _rowwise(_f_gate, [(yb, B, 0), cfg.view(proj, "zb")], [], [p["g_ob"]], [], [(B, BF16)], 256, f"gate_b_{tag}")
    c_rows = [cfg.view(proj, "cq"), cfg.view(proj, "ckv"), cfg.view(proj, "kr")]
    trig = [(cos2, LANE, 0), (sin2, LANE, 0)]
    cqn, ckvn, krr = _rowwise(_f_cpre, c_rows, trig, [p["g_q"], p["g_kv"]], [rot],
                              [(cfg.Q, BF16), (cfg.KV, BF16), (LANE, BF16)], 256, f"cpre_{tag}")
    q_raw = _matmul(cqn, W["uq"], "nn", BF16, f"mm_uq_{tag}")
    kv = _matmul(ckvn, W["ukv"], "nn", BF16, f"mm_ukv_{tag}")
    r_rows = [(q_raw, 2 * C, 0), (kv, 2 * C, 0), (krr, LANE, 0)]
    q_rot, k_full, v_c = _rowwise(_f_crope, r_rows, trig, [], [rot], [(2 * C, BF16), (2 * C, BF16), (C, BF16)], 128,
                                  f"crope_{tag}")
    qc, kc, vc = (q_rot, 0, 2 * LANE), (k_full, 0, 2 * LANE), (v_c, 0, LANE)
    yc, lse, *moved_mla = _softmax_fwd(qc, kc, vc, cfg.Hc, (LANE + ROPE) ** -0.5, f"mla_fwd_{tag}", 256, 256,
                                       carry=carry_mla)
    (ycg,) = _rowwise(_f_gate, [(yc, C, 0), cfg.view(proj, "zc")], [], [p["g_oc"]], [], [(C, BF16)], 256, f"gate_c_{tag}")
    y = jnp.concatenate([ybg, ycg, ya], axis=1)
    out = _matmul(y, W["out"], "nn", F32, f"mm_out_{tag}", add=x)
    saved = dict(x=x, h=h, proj=proj, yb=yb, tot=tot, cqn=cqn, ckvn=ckvn, krr=krr, q_raw=q_raw, kv=kv,
                 q_rot=q_rot, k_full=k_full, v_c=v_c, yc=yc, lse=lse, y=y)
    return out, saved, (moved_stick[0] if moved_stick else []), (moved_mla[0] if moved_mla else [])


def _layer_bwd(cfg, l, dout, sv, W, p, cos2, sin2, rot, ext_stick, ext_mla, last):
    S, D, A, B, C = cfg.S, cfg.D, cfg.A, cfg.B, cfg.C
    tag = f"l{l}"
    proj = sv["proj"]
    dy = _matmul(dout, W["out"], "nt", BF16, f"mm_dy_{tag}")
    d_wout = _matmul(sv["y"], dout, "tn", BF16, f"mm_dwout_{tag}")
    wout_slots = _unperm_rows_out(cfg, d_wout).reshape(NDEV, cfg.DMIX // NDEV, D)
    (dyb, dzb), (dg_ob,), _ = _rowwise_vjp(_f_gate, [(sv["yb"], B, 0), cfg.view(proj, "zb")], [], [p["g_ob"]], [],
                                           [(dy, B, 0)], [BF16, BF16], 256, f"gate_b_bwd_{tag}")
    (dyc, dzc), (dg_oc,), _ = _rowwise_vjp(_f_gate, [(sv["yc"], C, 0), cfg.view(proj, "zc")], [], [p["g_oc"]], [],
                                           [(dy, C, B // C)], [BF16, BF16], 256, f"gate_c_bwd_{tag}")
    a_rows = [cfg.view(proj, "ua"), cfg.view(proj, "va"), cfg.view(proj, "za")]
    a_par = [p["g_v"], p["w_s"], p["b_s"], p["g_oa"]]
    (dua, dva, dza), (dg_v, dw_s, db_s, dg_oa), _ = _rowwise_vjp(
        _f_gmlp, a_rows, [], a_par, [], [(dy, A, (B + C) // A)], [BF16] * 3, LANE, f"gmlp_bwd_{tag}")
    qb, kb, vb = cfg.heads_view(proj, "qb"), cfg.heads_view(proj, "kb"), cfg.heads_view(proj, "vb")
    dqb, dkb, dvb, moved_stick = _stick_bwd(qb, kb, vb, (dyb, 0, LANE), sv["tot"], cfg.Hb, LANE ** -0.5,
                                            f"stick_bwd_{tag}", 256, 256,
                                            carry=_Exchange([[a] for a in ext_stick + [wout_slots]], False))
    got = dict(w_out=moved_stick[-1][0])
    ext_got = [mv[0] for mv in moved_stick[:-1]]
    qc, kc, vc = (sv["q_rot"], 0, 2 * LANE), (sv["k_full"], 0, 2 * LANE), (sv["v_c"], 0, LANE)
    dq_rot, dk_full, dv_c, *moved_mla = _softmax_bwd(qc, kc, vc, (sv["yc"], 0, LANE), (dyc, 0, LANE), sv["lse"], cfg.Hc,
                                                     (LANE + ROPE) ** -0.5, f"mla_bwd_{tag}", 256, 256,
                                                     carry=_Exchange([[a] for a in ext_mla], False) if ext_mla else None)
    ext_got += [mv[0] for mv in (moved_mla[0] if moved_mla else [])]
    trig = [(cos2, LANE, 0), (sin2, LANE, 0)]
    r_rows = [(sv["q_raw"], 2 * C, 0), (sv["kv"], 2 * C, 0), (sv["krr"], LANE, 0)]
    (dq_raw, dkv, dkrr), _, _ = _rowwise_vjp(_f_crope, r_rows, trig, [], [rot],
                                             [(dq_rot, 2 * C, 0), (dk_full, 2 * C, 0), (dv_c, C, 0)], [BF16] * 3, 128,
                                             f"crope_bwd_{tag}")
    dcqn = _matmul(dq_raw, W["uq"], "nt", BF16, f"mm_dcq_{tag}")
    d_wuq = _matmul(sv["cqn"], dq_raw, "tn", BF16, f"mm_dwuq_{tag}")
    dckvn = _matmul(dkv, W["ukv"], "nt", BF16, f"mm_dckv_{tag}")
    d_wukv = _matmul(sv["ckvn"], dkv, "tn", BF16, f"mm_dwukv_{tag}")
    c_rows = [cfg.view(proj, "cq"), cfg.view(proj, "ckv"), cfg.view(proj, "kr")]
    (dcq, dckv, dkr), (dg_q, dg_kv), _ = _rowwise_vjp(
        _f_cpre, c_rows, trig, [p["g_q"], p["g_kv"]], [rot],
        [(dcqn, cfg.Q, 0), (dckvn, cfg.KV, 0), (dkrr, LANE, 0)], [BF16] * 3, 256, f"cpre_bwd_{tag}")
    parts = dict(ua=dua, va=dva, za=dza, qb=dqb, kb=dkb, vb=dvb, zb=dzb, zc=dzc, cq=dcq, kr=dkr, ckv=dckv)
    cols, pos = [], 0
    for nm, off in sorted(cfg.off.items(), key=lambda kv_: kv_[1]):
        if off > pos:
            cols.append(jnp.zeros((S, off - pos), BF16))
        cols.append(parts[nm])
        pos = off + parts[nm].shape[1]
    if cfg.NP > pos:
        cols.append(jnp.zeros((S, cfg.NP - pos), BF16))
    dproj = jnp.concatenate(cols, axis=1)
    d_win = _matmul(sv["h"], dproj, "tn", BF16, f"mm_dwin_{tag}")
    to_send = dict(w_in=_to_slots_cols(_unpad_w_in(cfg, d_win)),
                   c_w_uq=_to_slots_cols(d_wuq.reshape(cfg.Q, cfg.Hc, 2 * LANE)[:, :, :LANE + ROPE].reshape(cfg.Q, -1)),
                   c_w_ukv=_to_slots_cols(d_wukv))
    if last:
        dh, moved = _matmul(dproj, W["in"], "nt", BF16, f"mm_dh_{tag}",
                            carry=_Exchange([[to_send[nm]] for nm in ("w_in", "c_w_uq", "c_w_ukv")], False))
        got.update(w_in=moved[0][0], c_w_uq=moved[1][0], c_w_ukv=moved[2][0])
        to_send = {}
    else:
        dh = _matmul(dproj, W["in"], "nt", BF16, f"mm_dh_{tag}")
    (dx,), (dg_pre,), _ = _rowwise_vjp(_f_pre_res, [(sv["x"], D, 0)], [], [p["g_pre"]], [],
                                       [(dh, D, 0), (dout, D, 0)], [F32], 128, f"pre_bwd_{tag}")
    small = dict(g_pre=dg_pre[0], a_g_v=dg_v.reshape(cfg.G, LANE), a_w_s=dw_s, a_b_s=db_s[:, :, 0], c_g_q=dg_q[0],
                 c_g_kv=dg_kv[0], g_out=jnp.concatenate([dg_oa[0], dg_ob[0], dg_oc[0]]))
    return dx, small, got, to_send, ext_got


def _pack_small(vals):
    pieces = []
    for nm in SMALL:
        piece = vals[nm].reshape(-1, LANE)
        pieces.append(jnp.pad(piece, ((0, -piece.shape[0] % 8), (0, 0))))
    packed = jnp.concatenate(pieces, axis=0)
    return jnp.pad(packed, ((0, -packed.shape[0] % SMALL_ROWS), (0, 0)))


def _unpack_small(packed, like):
    out, row = {}, 0
    for nm in SMALL:
        n = like[nm].size // LANE
        out[nm] = packed[row:row + n].reshape(like[nm].shape)
        row += n + (-n % 8)
    return out


def kernel(x, positions, g_pre, w_in, a_g_v, a_w_s, a_b_s, c_g_q, c_g_kv, c_w_uq, c_w_ukv, g_out, w_out, g_final, loss_target, m_g_pre, m_w_in, m_a_g_v, m_a_w_s, m_a_b_s, m_c_g_q, m_c_g_kv, m_c_w_uq, m_c_w_ukv, m_g_out, m_w_out, m_g_final, v_g_pre, v_w_in, v_a_g_v, v_a_w_s, v_a_b_s, v_c_g_q, v_c_g_kv, v_c_w_uq, v_c_w_ukv, v_g_out, v_w_out, v_g_final):
    depth, S, D = w_in.shape[0], x.shape[1], x.shape[2]
    cfg = _Cfg(S, D, a_g_v.shape[1], c_g_q.shape[1], c_g_kv.shape[1], c_w_ukv.shape[2] * NDEV // (2 * LANE), g_out.shape[1])
    weights = dict(g_pre=g_pre, w_in=w_in, a_g_v=a_g_v, a_w_s=a_w_s, a_b_s=a_b_s, c_g_q=c_g_q, c_g_kv=c_g_kv,
                   c_w_uq=c_w_uq, c_w_ukv=c_w_ukv, g_out=g_out, w_out=w_out, g_final=g_final)
    mom_m = dict(g_pre=m_g_pre, w_in=m_w_in, a_g_v=m_a_g_v, a_w_s=m_a_w_s, a_b_s=m_a_b_s, c_g_q=m_c_g_q, c_g_kv=m_c_g_kv,
                 c_w_uq=m_c_w_uq, c_w_ukv=m_c_w_ukv, g_out=m_g_out, w_out=m_w_out, g_final=m_g_final)
    mom_v = dict(g_pre=v_g_pre, w_in=v_w_in, a_g_v=v_a_g_v, a_w_s=v_a_w_s, a_b_s=v_a_b_s, c_g_q=v_c_g_q, c_g_kv=v_c_g_kv,
                 c_w_uq=v_c_w_uq, c_w_ukv=v_c_w_ukv, g_out=v_g_out, w_out=v_w_out, g_final=v_g_final)
    big_names = ("w_in", "c_w_uq", "c_w_ukv", "w_out")

    inv_freq = 1.0 / (ROPE_THETA ** (jnp.arange(0, ROPE, 2, dtype=F32) / ROPE))
    ang = positions[0].astype(F32)[:, None] * inv_freq
    zpad = jnp.zeros((S, LANE - ROPE), F32)
    cos2 = jnp.concatenate([jnp.cos(ang), jnp.cos(ang), zpad], axis=1)
    sin2 = jnp.concatenate([jnp.sin(ang), jnp.sin(ang), zpad], axis=1)
    rot = _rope_matrix()

    def shards(l, names):
        return [[weights[nm][l].astype(BF16)] for nm in names]

    def assemble(g_in, g_uq, g_ukv, g_wout):
        uq = _from_slots_cols(g_uq).reshape(cfg.Q, cfg.Hc, LANE + ROPE)
        uq = jnp.pad(uq, ((0, 0), (0, 0), (0, LANE - ROPE))).reshape(cfg.Q, 2 * cfg.C)
        return {"in": _pad_w_in(cfg, _from_slots_cols(g_in)), "uq": uq, "ukv": _from_slots_cols(g_ukv),
                "out": _perm_rows_out(cfg, g_wout.reshape(cfg.DMIX, D))}

    params = [_layer_params(cfg, l, g_pre, a_g_v, a_w_s, a_b_s, c_g_q, c_g_kv, g_out) for l in range(depth)]

    gathered = [g[0] for g in _exchange(shards(0, big_names), True, "gather_weights_l0")]
    hcur, saved, Ws = x[0], [], []
    for l in range(depth):
        Ws.append(assemble(*gathered))
        nxt = l + 1 < depth
        hcur, sv, got_in, got_rest = _layer_fwd(
            cfg, l, hcur, Ws[l], params[l], cos2, sin2, rot,
            carry_stick=_Exchange(shards(l + 1, big_names[:1]), True) if nxt else None,
            carry_mla=_Exchange(shards(l + 1, big_names[1:]), True) if nxt else None)
        saved.append(sv)
        gathered = [g[0] for g in got_in + got_rest]
    (dh,), (dg_final,), (loss_rows,) = _rowwise_vjp(
        _f_final, [(hcur, D, 0)], [(loss_target[0], D, 0)], [g_final[None]], [], [(jnp.ones((S, 1), F32), 1, 0)],
        [F32], 128, "final", primal=[(1, F32)])
    loss = lax.psum(jnp.sum(loss_rows), MESH_AXES)

    small_g, slots, pending = [None] * depth, [None] * depth, {}
    for l in reversed(range(depth)):
        ext_stick = [pending["w_in"]] if pending else []
        ext_mla = [pending["c_w_uq"], pending["c_w_ukv"]] if pending else []
        dh, small_g[l], slots[l], pending, ext_got = _layer_bwd(cfg, l, dh, saved[l], Ws[l], params[l], cos2, sin2, rot,
                                                                ext_stick, ext_mla, l == 0)
        if ext_got:
            slots[l + 1].update(w_in=ext_got[0], c_w_uq=ext_got[1], c_w_ukv=ext_got[2])
    grad_x = dh[None]
    small_grads = {nm: jnp.stack([small_g[l][nm] for l in range(depth)]) for nm in SMALL if nm != "g_final"}
    small_grads["g_final"] = dg_final[0]
    (small_slots,) = _exchange([[_pack_small(small_grads)]], True, "gather_small_grads")

    res = {}
    for nm in big_names:
        res[nm] = None
        for l in range(depth):
            res[nm] = _adamw(slots[l][nm], weights[nm], mom_m[nm], mom_v[nm], l, res[nm], f"adamw_{nm}_l{l}")
    packed = _adamw(small_slots[0], _pack_small(weights)[None], _pack_small(mom_m)[None], _pack_small(mom_v)[None], 0, None,
                    "adamw_small")
    small_res = [_unpack_small(r[0], weights) for r in packed]
    order = ("g_pre", "w_in", "a_g_v", "a_w_s", "a_b_s", "c_g_q", "c_g_kv", "c_w_uq", "c_w_ukv", "g_out", "w_out", "g_final")
    outs = [loss, grad_x]
    for kind in range(4):
        outs += [small_res[kind][nm] if nm in SMALL else res[nm][kind] for nm in order]
    return tuple(outs)
```

```python
import functools

import numpy as np
import jax
import jax.numpy as jnp
from jax import lax
from jax.experimental import pallas as pl
from jax.experimental.pallas import tpu as pltpu

NDEV = 8
MESH_AXES = ("x", "y", "c")
LANE = 128
ROPE = 64
EPS = 1e-6
ROPE_THETA = 10000.0
ADAM_LR, ADAM_B1, ADAM_B2, ADAM_EPS, ADAM_WD, ADAM_STEP = 0.001, 0.9, 0.999, 1e-08, 0.01, 10
VMEM_LIMIT = 48 * 1024 * 1024
ADAM_TILE_BYTES = 512 * 1024
SMALL_ROWS = 256
ATTN_TILES = {"stick_fwd": [(1024, 256)], "stick_bwd": [(1024, 256)], "mla_fwd": [(512, 1024)], "mla_bwd": [(512, 1024)]}
F32, BF16 = jnp.float32, jnp.bfloat16
SMALL = ("g_pre", "a_g_v", "a_w_s", "a_b_s", "c_g_q", "c_g_kv", "g_out", "g_final")


def _tile(dim, cap, mult=LANE):
    if dim <= cap:
        return dim
    t = (cap // mult) * mult
    while t >= mult:
        if dim % t == 0:
            return t
        t -= mult
    return dim


def _dot_nt(a, b):
    return lax.dot_general(a, b, (((1,), (1,)), ((), ())), preferred_element_type=F32)


def _dot_tn(a, b):
    return lax.dot_general(a, b, (((0,), (0,)), ((), ())), preferred_element_type=F32)


def _dot(a, b):
    return jnp.dot(a, b, preferred_element_type=F32)


class _Exchange:
    def __init__(self, groups, gather):
        self.groups, self.gather = groups, gather
        self.flat = [(gi, li, a) for gi, grp in enumerate(groups) for li, a in enumerate(grp)]
        self.n = len(self.flat)
        self.args = [a for (_, _, a) in self.flat]
        self.out_shape = [jax.ShapeDtypeStruct((len(grp), NDEV) + tuple(grp[0].shape[-2:]), grp[0].dtype) for grp in groups]
        self.scratch = [pltpu.SemaphoreType.DMA((self.n, NDEV - 1)), pltpu.SemaphoreType.DMA((self.n, NDEV - 1)),
                        pltpu.SemaphoreType.DMA((self.n,))]

    def _copies(self, ins, outs, send_sems, recv_sems, local_sems, landings):
        x, y, c = lax.axis_index("x"), lax.axis_index("y"), lax.axis_index("c")
        me = 4 * x + 2 * y + c
        owns = [pltpu.make_async_copy(ins[i] if self.gather else ins[i].at[me], outs[gi].at[li, me], local_sems.at[i])
                for i, (gi, li, _) in enumerate(self.flat)]
        pairs = []
        for k in range(1, NDEV):
            px = 1 - x if k & 4 else x
            py = 1 - y if k & 2 else y
            pc = 1 - c if k & 1 else c
            peer = 4 * px + 2 * py + pc
            for i, (gi, li, _) in enumerate(self.flat):
                src = ins[i] if self.gather else ins[i].at[peer]
                sems = dict(send_sem=send_sems.at[i, k - 1], recv_sem=recv_sems.at[i, k - 1],
                            device_id=(px, py, pc), device_id_type=pl.DeviceIdType.MESH)
                out = pltpu.make_async_remote_copy(src_ref=src, dst_ref=outs[gi].at[li, me], **sems)
                landing = pltpu.make_async_remote_copy(src_ref=src, dst_ref=outs[gi].at[li, peer], **sems) if landings else None
                pairs.append((out, landing))
        return owns, pairs

    def start(self, ins, outs, sems):
        owns, pairs = self._copies(ins, outs, *sems, landings=False)
        for own in owns:
            own.start()
        for out, _ in pairs:
            out.start()

    def wait(self, ins, outs, sems):
        owns, pairs = self._copies(ins, outs, *sems, landings=True)
        for out, landing in pairs:
            out.wait_send()
            landing.wait_recv()
        for own in owns:
            own.wait()


def _call(body, name, grid, in_specs, out_specs, out_shape, scratch, semantics, args, carry=None):
    n_in, n_out, n_scr = len(in_specs), len(out_specs), len(scratch)
    if carry is None:
        run = body
    else:
        semantics = ("arbitrary",) * len(grid)
        anyspec = pl.BlockSpec(memory_space=pl.ANY)
        in_specs = list(in_specs) + [anyspec] * carry.n
        out_specs = list(out_specs) + [anyspec] * len(carry.groups)
        out_shape = list(out_shape) + carry.out_shape
        scratch = list(scratch) + carry.scratch
        args = list(args) + carry.args

        def run(*refs):
            c_in, x_in = refs[:n_in], refs[n_in:n_in + carry.n]
            rest = refs[n_in + carry.n:]
            c_out, x_out = rest[:n_out], rest[n_out:n_out + len(carry.groups)]
            c_scr, sems = rest[n_out + len(carry.groups):len(rest) - 3], rest[len(rest) - 3:]
            first, last = True, True
            for d, extent in enumerate(grid):
                first = jnp.logical_and(first, pl.program_id(d) == 0)
                last = jnp.logical_and(last, pl.program_id(d) == extent - 1)

            @pl.when(first)
            def _():
                carry.start(x_in, x_out, sems)

            body(*c_in, *c_out, *c_scr)

            @pl.when(last)
            def _():
                carry.wait(x_in, x_out, sems)

    res = pl.pallas_call(
        run, name=name, grid=grid, out_shape=list(out_shape), in_specs=list(in_specs), out_specs=list(out_specs),
        scratch_shapes=list(scratch),
        compiler_params=pltpu.CompilerParams(dimension_semantics=semantics, vmem_limit_bytes=VMEM_LIMIT,
                                             has_side_effects=carry is not None),
    )(*args)
    return list(res[:n_out]), list(res[n_out:])


def _exchange(groups, gather, name):
    ex = _Exchange(groups, gather)

    def body(*refs):
        ins, outs, sems = refs[:ex.n], refs[ex.n:ex.n + len(groups)], refs[ex.n + len(groups):]
        ex.start(ins, outs, sems)
        ex.wait(ins, outs, sems)

    anyspec = pl.BlockSpec(memory_space=pl.ANY)
    return pl.pallas_call(
        body, name=name, out_shape=ex.out_shape, in_specs=[anyspec] * ex.n, out_specs=[anyspec] * len(groups),
        scratch_shapes=ex.scratch, compiler_params=pltpu.CompilerParams(has_side_effects=True),
    )(*ex.args)


def _matmul(a, b, mode, out_dtype, name, add=None, tm=1024, tn=1024, tk=1024, carry=None):
    if mode == "tn":
        (K, M), (K2, N) = a.shape, b.shape
    elif mode == "nt":
        (M, K), (N, K2) = a.shape, b.shape
    else:
        (M, K), (K2, N) = a.shape, b.shape
    assert K == K2, (a.shape, b.shape, mode)
    tm, tn, tk = _tile(M, tm), _tile(N, tn), _tile(K, tk)
    nk = K // tk
    a_spec = pl.BlockSpec((tk, tm), lambda i, j, k: (k, i)) if mode == "tn" else pl.BlockSpec((tm, tk), lambda i, j, k: (i, k))
    b_spec = pl.BlockSpec((tn, tk), lambda i, j, k: (j, k)) if mode == "nt" else pl.BlockSpec((tk, tn), lambda i, j, k: (k, j))
    dot = {"nn": _dot, "nt": _dot_nt, "tn": _dot_tn}[mode]
    has_add = add is not None

    def body(*refs):
        a_ref, b_ref = refs[0], refs[1]
        o_ref, acc = refs[-2], refs[-1]
        k = pl.program_id(2)

        @pl.when(k == 0)
        def _():
            acc[...] = jnp.zeros_like(acc)

        acc[...] += dot(a_ref[...].astype(BF16), b_ref[...].astype(BF16))

        @pl.when(k == nk - 1)
        def _():
            r = acc[...]
            if has_add:
                r = r + refs[2][...]
            o_ref[...] = r.astype(o_ref.dtype)

    in_specs = [a_spec, b_spec]
    args = [a, b]
    if has_add:
        in_specs.append(pl.BlockSpec((tm, tn), lambda i, j, k: (i, j)))
        args.append(add)
    (out,), moved = _call(body, name, (M // tm, N // tn, nk), in_specs, [pl.BlockSpec((tm, tn), lambda i, j, k: (i, j))],
                          [jax.ShapeDtypeStruct((M, N), out_dtype)], [pltpu.VMEM((tm, tn), F32)],
                          ("parallel", "parallel", "arbitrary"), args, carry)
    return out if carry is None else (out, moved)


def _row_specs(views, tile):
    return [pl.BlockSpec((tile, w), functools.partial(lambda i, cb: (i, cb), cb=cb)) for (_, w, cb) in views]


def _full_specs(arrs):
    return [pl.BlockSpec(p.shape, functools.partial(lambda i, nd: (0,) * nd, nd=p.ndim)) for p in arrs]


def _rowwise(fn, rows, aux, params, consts, outs, tile, name):
    S = rows[0][0].shape[0]
    nr, na, npar, nc = len(rows), len(aux), len(params), len(consts)

    def body(*refs):
        ins = [r[...].astype(F32) for r in refs[:nr + na]]
        small = [r[...] for r in refs[nr + na:nr + na + npar + nc]]
        res = fn(*ins, *small)
        for o_ref, r in zip(refs[nr + na + npar + nc:], res):
            o_ref[...] = r.astype(o_ref.dtype)

    return pl.pallas_call(
        body, name=name, grid=(S // tile,),
        out_shape=[jax.ShapeDtypeStruct((S, w), dt) for (w, dt) in outs],
        in_specs=_row_specs(rows + aux, tile) + _full_specs(params + consts),
        out_specs=[pl.BlockSpec((tile, w), lambda i: (i, 0)) for (w, _) in outs],
        compiler_params=pltpu.CompilerParams(dimension_semantics=("parallel",), vmem_limit_bytes=VMEM_LIMIT),
    )(*[v[0] for v in rows + aux], *params, *consts)


def _rowwise_vjp(fn, rows, aux, params, consts, cots, grad_dtypes, tile, name, primal=()):
    S = rows[0][0].shape[0]
    nr, na, npar, nc, nct, npr = len(rows), len(aux), len(params), len(consts), len(cots), len(primal)

    def body(*refs):
        n_in = nr + na + npar + nc + nct
        rv = [r[...].astype(F32) for r in refs[:nr]]
        av = [r[...].astype(F32) for r in refs[nr:nr + na]]
        pv = [r[...] for r in refs[nr + na:nr + na + npar]]
        cv = [r[...] for r in refs[nr + na + npar:nr + na + npar + nc]]
        ct = tuple(r[...].astype(F32) for r in refs[nr + na + npar + nc:n_in])
        res, vjp = jax.vjp(lambda *rp: tuple(fn(*rp[:nr], *av, *rp[nr:], *cv)), *rv, *pv)
        grads = vjp(ct)
        g_refs = refs[n_in:n_in + nr]
        p_refs = refs[n_in + nr:n_in + nr + npar]
        o_refs = refs[n_in + nr + npar:]
        for g_ref, g in zip(g_refs, grads[:nr]):
            g_ref[...] = g.astype(g_ref.dtype)

        @pl.when(pl.program_id(0) == 0)
        def _():
            for p_ref in p_refs:
                p_ref[...] = jnp.zeros_like(p_ref)

        for p_ref, g in zip(p_refs, grads[nr:]):
            p_ref[...] += g
        for o_ref, r in zip(o_refs, res[:npr]):
            o_ref[...] = r.astype(o_ref.dtype)

    out_shape = ([jax.ShapeDtypeStruct((S, w), dt) for (_, w, _), dt in zip(rows, grad_dtypes)]
                 + [jax.ShapeDtypeStruct(p.shape, F32) for p in params]
                 + [jax.ShapeDtypeStruct((S, w), dt) for (w, dt) in primal])
    out_specs = ([pl.BlockSpec((tile, w), lambda i: (i, 0)) for (_, w, _) in rows] + _full_specs(params)
                 + [pl.BlockSpec((tile, w), lambda i: (i, 0)) for (w, _) in primal])
    res = pl.pallas_call(
        body, name=name, grid=(S // tile,), out_shape=out_shape,
        in_specs=_row_specs(rows + aux, tile) + _full_specs(params + consts) + _row_specs(cots, tile),
        out_specs=out_specs,
        compiler_params=pltpu.CompilerParams(dimension_semantics=("arbitrary",), vmem_limit_bytes=VMEM_LIMIT),
    )(*[v[0] for v in rows + aux], *params, *consts, *[v[0] for v in cots])
    return res[:nr], res[nr:nr + npar], res[nr + npar:]


@jax.custom_vjp
def _mm(a, b):
    return _dot(a.astype(BF16), b.astype(BF16))


def _mm_fwd(a, b):
    return _mm(a, b), (a, b)


def _mm_bwd(res, ct):
    a, b = res
    ctb = ct.astype(BF16)
    return _dot_nt(ctb, b.astype(BF16)), _dot_tn(a.astype(BF16), ctb)


_mm.defvjp(_mm_fwd, _mm_bwd)


def _rms(x, g):
    return x * lax.rsqrt(jnp.mean(x * x, axis=-1, keepdims=True) + EPS) * g


def _f_pre(x, g):
    return (_rms(x, g),)


def _f_pre_res(x, g):
    return _rms(x, g), x


def _f_gate(y, z, g):
    return (_rms(y, g) * jax.nn.silu(z),)


def _f_gmlp(u, v, z, g_v, w_s, b_s, g_o):
    groups = w_s.shape[0]
    u, v = jax.nn.gelu(u), jax.nn.gelu(v)
    t_idx = lax.broadcasted_iota(jnp.int32, (LANE, LANE), 0)
    s_idx = lax.broadcasted_iota(jnp.int32, (LANE, LANE), 1)
    ys = []
    for g in range(groups):
        sl = slice(g * LANE, (g + 1) * LANE)
        vn = _rms(v[:, sl], g_v[:, sl])
        w = jnp.where(s_idx <= t_idx, w_s[g], 0.0)
        ys.append(u[:, sl] * (_mm(w, vn) + b_s[g]))
    return (_rms(jnp.concatenate(ys, axis=1), g_o) * jax.nn.silu(z),)


def _rope(x, cos2, sin2, rot):
    return x * cos2 + _mm(x, rot) * sin2


def _f_cpre(cq, ckv, kr, cos2, sin2, g_q, g_kv, rot):
    return _rms(cq, g_q), _rms(ckv, g_kv), _rope(kr, cos2, sin2, rot)


def _f_crope(q, kv, krr, cos2, sin2, rot):
    heads = q.shape[1] // (2 * LANE)
    qs, ks, vs = [], [], []
    for h in range(heads):
        lo, mid, hi = 2 * h * LANE, (2 * h + 1) * LANE, (2 * h + 2) * LANE
        qs += [q[:, lo:mid], _rope(q[:, mid:hi], cos2, sin2, rot)]
        ks += [kv[:, lo:mid], krr]
        vs += [kv[:, mid:hi]]
    return jnp.concatenate(qs, axis=1), jnp.concatenate(ks, axis=1), jnp.concatenate(vs, axis=1)


def _f_final(h, target, g):
    err = _rms(h, g) - target
    return (0.5 * jnp.mean(err * err, axis=-1, keepdims=True),)


def _rope_matrix():
    r = np.zeros((LANE, LANE), np.float32)
    half = ROPE // 2
    for i in range(half):
        r[i + half, i] = -1.0
        r[i, i + half] = 1.0
    return jnp.asarray(r)


def _head_spec(view, rows, n_rows_block):
    _, cb0, w = view
    if n_rows_block:
        return pl.BlockSpec((rows, w), functools.partial(lambda h, i, cb0: (i, cb0 + h), cb0=cb0))
    return pl.BlockSpec((rows, w), functools.partial(lambda h, i, cb0: (0, cb0 + h), cb0=cb0))


def _stat_spec(tq):
    return pl.BlockSpec((1, tq, 1), lambda h, i: (h, i, 0))


def _softplus(z):
    return jnp.maximum(z, 0.0) + jnp.log(1.0 + jnp.exp(-jnp.abs(z)))


def _cumsum_mm(x, m01):
    hi = x.astype(BF16)
    lo = (x - hi.astype(F32)).astype(BF16)
    return _dot(hi, m01) + _dot(lo, m01)


def _attn_call(body, name, heads, S, tq, ins, in_blocked, outs, out_blocked, scratch, stats_in=0, stats_out=0, carry=None):
    in_specs = [_head_spec(v, tq if blk else S, blk) for v, blk in zip(ins[:len(ins) - stats_in], in_blocked)]
    in_specs += [_stat_spec(tq)] * stats_in
    out_specs = [_head_spec((None, 0, w), tq if blk else S, blk) for (w, _), blk in zip(outs, out_blocked)]
    out_specs += [_stat_spec(tq)] * stats_out
    out_shape = [jax.ShapeDtypeStruct((S, heads * w), dt) for (w, dt) in outs]
    out_shape += [jax.ShapeDtypeStruct((heads, S, 1), F32)] * stats_out
    args = [v[0] for v in ins[:len(ins) - stats_in]] + list(ins[len(ins) - stats_in:])
    res, moved = _call(body, name, (heads, S // tq), in_specs, out_specs, out_shape, scratch, ("arbitrary", "arbitrary"),
                       args, carry)
    return res if carry is None else res + [moved]


def _softmax_fwd(q, k, v, heads, scale, name, tq, bk, carry=None):
    S, dv = q[0].shape[0], v[2]

    def body(q_ref, k_ref, v_ref, o_ref, lse_ref):
        qi = pl.program_id(1)
        qv = q_ref[...]
        row = qi * tq + lax.broadcasted_iota(jnp.int32, (tq, bk), 0)
        col0 = lax.broadcasted_iota(jnp.int32, (tq, bk), 1)

        def step(kb, carry):
            m, l, acc = carry
            sl = pl.ds(pl.multiple_of(kb * bk, bk), bk)
            s = _dot_nt(qv, k_ref[sl, :]) * scale
            s = jnp.where(kb * bk + col0 <= row, s, -1e30)
            m_new = jnp.maximum(m, jnp.max(s, axis=1, keepdims=True))
            p = jnp.exp(s - m_new)
            alpha = jnp.exp(m - m_new)
            l = alpha * l + jnp.sum(p, axis=1, keepdims=True)
            acc = alpha * acc + _dot(p.astype(BF16), v_ref[sl, :])
            return m_new, l, acc

        n_kb = (qi * tq + tq + bk - 1) // bk
        m, l, acc = lax.fori_loop(0, n_kb, step, (jnp.full((tq, 1), -1e30, F32), jnp.zeros((tq, 1), F32),
                                                  jnp.zeros((tq, dv), F32)))
        o_ref[...] = (acc / l).astype(o_ref.dtype)
        lse_ref[0] = m + jnp.log(l)

    return _attn_call(body, name, heads, S, tq, [q, k, v], [1, 0, 0], [(dv, BF16)], [1], [], stats_out=1, carry=carry)


def _softmax_bwd(q, k, v, o, do, lse, heads, scale, name, tq, bk, carry=None):
    S, dq_w, dv = q[0].shape[0], q[2], v[2]
    nq = S // tq

    def body(q_ref, k_ref, v_ref, o_ref, do_ref, lse_ref, dq_ref, dk_ref, dv_ref, dk_acc, dv_acc):
        qi = pl.program_id(1)

        @pl.when(qi == 0)
        def _():
            dk_acc[...] = jnp.zeros_like(dk_acc)
            dv_acc[...] = jnp.zeros_like(dv_acc)

        qv, dov = q_ref[...], do_ref[...]
        delta = jnp.sum(dov.astype(F32) * o_ref[...].astype(F32), axis=1, keepdims=True)
        lse_v = lse_ref[0]
        row = qi * tq + lax.broadcasted_iota(jnp.int32, (tq, bk), 0)
        col0 = lax.broadcasted_iota(jnp.int32, (tq, bk), 1)

        def step(kb, dq):
            sl = pl.ds(pl.multiple_of(kb * bk, bk), bk)
            ks, vs = k_ref[sl, :], v_ref[sl, :]
            s = _dot_nt(qv, ks) * scale
            p = jnp.where(kb * bk + col0 <= row, jnp.exp(s - lse_v), 0.0)
            ds = (p * (_dot_nt(dov, vs) - delta) * scale).astype(BF16)
            dk_acc[sl, :] += _dot_tn(ds, qv)
            dv_acc[sl, :] += _dot_tn(p.astype(BF16), dov)
            return dq + _dot(ds, ks)

        n_kb = (qi * tq + tq + bk - 1) // bk
        dq_ref[...] = lax.fori_loop(0, n_kb, step, jnp.zeros((tq, dq_w), F32)).astype(dq_ref.dtype)

        @pl.when(qi == nq - 1)
        def _():
            dk_ref[...] = dk_acc[...].astype(dk_ref.dtype)
            dv_ref[...] = dv_acc[...].astype(dv_ref.dtype)

    return _attn_call(body, name, heads, S, tq, [q, k, v, o, do, lse], [1, 0, 0, 1, 1],
                      [(dq_w, BF16), (dq_w, BF16), (dv, BF16)], [1, 0, 0],
                      [pltpu.VMEM((S, dq_w), F32), pltpu.VMEM((S, dv), F32)], stats_in=1, carry=carry)


def _stick_fwd(q, k, v, heads, scale, name, tq, bk, carry=None):
    S, dv = q[0].shape[0], v[2]

    def body(q_ref, k_ref, v_ref, o_ref, tot_ref):
        qi = pl.program_id(1)
        qv = q_ref[...]
        row = qi * tq + lax.broadcasted_iota(jnp.int32, (tq, bk), 0)
        col0 = lax.broadcasted_iota(jnp.int32, (tq, bk), 1)
        m_gt = (lax.broadcasted_iota(jnp.int32, (bk, bk), 0) > lax.broadcasted_iota(jnp.int32, (bk, bk), 1)).astype(BF16)
        n_kb = (qi * tq + tq + bk - 1) // bk

        def step(it, carry):
            c, acc = carry
            kb = n_kb - 1 - it
            sl = pl.ds(pl.multiple_of(kb * bk, bk), bk)
            z = _dot_nt(qv, k_ref[sl, :]) * scale
            mask = kb * bk + col0 < row
            sp = _softplus(z)
            lk = jnp.where(mask, -sp, 0.0)
            after = _cumsum_mm(lk, m_gt) + c
            a = jnp.where(mask, jnp.exp(z - sp + after), 0.0)
            acc = acc + _dot(a.astype(BF16), v_ref[sl, :])
            return c + jnp.sum(lk, axis=1, keepdims=True), acc

        c, acc = lax.fori_loop(0, n_kb, step, (jnp.zeros((tq, 1), F32), jnp.zeros((tq, dv), F32)))
        o_ref[...] = acc.astype(o_ref.dtype)
        tot_ref[0] = c

    return _attn_call(body, name, heads, S, tq, [q, k, v], [1, 0, 0], [(dv, BF16)], [1], [], stats_out=1, carry=carry)


def _stick_bwd(q, k, v, do, tot, heads, scale, name, tq, bk, carry=None):
    S, dq_w, dv = q[0].shape[0], q[2], v[2]
    nq = S // tq

    def body(q_ref, k_ref, v_ref, do_ref, tot_ref, dq_ref, dk_ref, dv_ref, dk_acc, dv_acc):
        qi = pl.program_id(1)

        @pl.when(qi == 0)
        def _():
            dk_acc[...] = jnp.zeros_like(dk_acc)
            dv_acc[...] = jnp.zeros_like(dv_acc)

        qv, dov = q_ref[...], do_ref[...]
        tot_v = tot_ref[0]
        row = qi * tq + lax.broadcasted_iota(jnp.int32, (tq, bk), 0)
        col0 = lax.broadcasted_iota(jnp.int32, (tq, bk), 1)
        j_idx = lax.broadcasted_iota(jnp.int32, (bk, bk), 0)
        s_idx = lax.broadcasted_iota(jnp.int32, (bk, bk), 1)
        m_le, m_lt = (j_idx <= s_idx).astype(BF16), (j_idx < s_idx).astype(BF16)

        def step(kb, carry):
            pc, gc, dq = carry
            sl = pl.ds(pl.multiple_of(kb * bk, bk), bk)
            ks, vs = k_ref[sl, :], v_ref[sl, :]
            z = _dot_nt(qv, ks) * scale
            mask = kb * bk + col0 < row
            sp = _softplus(z)
            lk = jnp.where(mask, -sp, 0.0)
            after = tot_v - pc - _cumsum_mm(lk, m_le)
            log_beta = z - sp
            a = jnp.where(mask, jnp.exp(log_beta + after), 0.0)
            g = _dot_nt(dov, vs) * a
            cg = gc + _cumsum_mm(g, m_lt)
            dz = (jnp.where(mask, g * jnp.exp(-sp) - jnp.exp(log_beta) * cg, 0.0) * scale).astype(BF16)
            dk_acc[sl, :] += _dot_tn(dz, qv)
            dv_acc[sl, :] += _dot_tn(a.astype(BF16), dov)
            return (pc + jnp.sum(lk, axis=1, keepdims=True), gc + jnp.sum(g, axis=1, keepdims=True), dq + _dot(dz, ks))

        n_kb = (qi * tq + tq + bk - 1) // bk
        zero = jnp.zeros((tq, 1), F32)
        _, _, dq = lax.fori_loop(0, n_kb, step, (zero, zero, jnp.zeros((tq, dq_w), F32)))
        dq_ref[...] = dq.astype(dq_ref.dtype)

        @pl.when(qi == nq - 1)
        def _():
            dk_ref[...] = dk_acc[...].astype(dk_ref.dtype)
            dv_ref[...] = dv_acc[...].astype(dv_ref.dtype)

    return _attn_call(body, name, heads, S, tq, [q, k, v, do, tot], [1, 0, 0, 1],
                      [(dq_w, BF16), (dq_w, BF16), (dv, BF16)], [1, 0, 0],
                      [pltpu.VMEM((S, dq_w), F32), pltpu.VMEM((S, dv), F32)], stats_in=1, carry=carry)


def _adamw(slots, w, m, v, layer, prev, name):
    _, R, C = slots.shape
    L = w.shape[0]
    tc = _tile(C, 2048)
    tr = _tile(R, max(8, ADAM_TILE_BYTES // (slots.dtype.itemsize * tc)), mult=8)
    c1, c2 = 1.0 - ADAM_B1 ** ADAM_STEP, 1.0 - ADAM_B2 ** ADAM_STEP
    n_prev = 0 if prev is None else 4

    def body(s_ref, w_ref, m_ref, v_ref, *rest):
        g_out, d_out, m_out, v_out = rest[n_prev:]
        g = s_ref[0].astype(F32)
        for k in range(1, NDEV):
            g = g + s_ref[k].astype(F32)
        m_new = ADAM_B1 * m_ref[0] + (1.0 - ADAM_B1) * g
        v_new = ADAM_B2 * v_ref[0] + (1.0 - ADAM_B2) * (g * g)
        g_out[0] = g
        m_out[0] = m_new
        v_out[0] = v_new
        d_out[0] = -ADAM_LR * ((m_new / c1) / (jnp.sqrt(v_new / c2) + ADAM_EPS) + ADAM_WD * w_ref[0])

    spec = pl.BlockSpec((1, tr, tc), lambda i, j: (layer, i, j))
    in_specs = [pl.BlockSpec((NDEV, tr, tc), lambda i, j: (0, i, j)), spec, spec, spec]
    in_specs += [pl.BlockSpec(memory_space=pl.ANY)] * n_prev
    return pl.pallas_call(
        body, name=name, grid=(R // tr, C // tc), out_shape=[jax.ShapeDtypeStruct((L, R, C), F32)] * 4,
        in_specs=in_specs, out_specs=[spec] * 4, input_output_aliases={4 + i: i for i in range(n_prev)},
        compiler_params=pltpu.CompilerParams(dimension_semantics=("parallel", "parallel"), vmem_limit_bytes=VMEM_LIMIT),
    )(slots, w, m, v, *(prev or []))


class _Cfg:
    def __init__(self, S, D, groups, q_lora, kv_lora, c_heads, d_mix):
        self.S, self.D, self.G, self.Q, self.KV, self.Hc, self.DMIX = S, D, groups, q_lora, kv_lora, c_heads, d_mix
        self.A, self.C = groups * LANE, c_heads * LANE
        self.B = d_mix - self.A - self.C
        self.Hb = self.B // LANE
        A, B, C = self.A, self.B, self.C
        assert B % LANE == 0 and B % C == 0 and (B + C) % A == 0
        self.ref_segs = [("ua", A), ("va", A), ("za", A), ("qb", B), ("kb", B), ("vb", B), ("zb", B),
                         ("cq", q_lora), ("ckv", kv_lora), ("kr", ROPE), ("zc", C)]
        self.off, off = {}, 0
        for nm, w in [("ua", A), ("va", A), ("za", A), ("qb", B), ("kb", B), ("vb", B), ("zb", B), ("zc", C),
                      ("cq", q_lora), ("kr", LANE), ("ckv", kv_lora)]:
            off = -(-off // w) * w
            self.off[nm] = off
            off += w
        self.NP = -(-off // 512) * 512
        self.width = {"kr": LANE, **{nm: w for nm, w in self.ref_segs if nm != "kr"}}

    def tiles(self, kind, layer):
        tq, bk = ATTN_TILES[kind][layer % len(ATTN_TILES[kind])]
        return min(tq, self.S), min(bk, self.S)

    def view(self, arr, nm):
        w = self.width[nm]
        return (arr, w, self.off[nm] // w)

    def heads_view(self, arr, nm):
        return (arr, self.off[nm] // LANE, LANE)


def _pad_w_in(cfg, w):
    pieces, start = {}, 0
    for nm, width in cfg.ref_segs:
        pieces[nm] = w[:, start:start + width]
        start += width
    cols, pos = [], 0
    for nm, off in sorted(cfg.off.items(), key=lambda kv: kv[1]):
        if off > pos:
            cols.append(jnp.zeros((w.shape[0], off - pos), w.dtype))
        cols.append(pieces[nm])
        pos = off + pieces[nm].shape[1]
    if cfg.NP > pos:
        cols.append(jnp.zeros((w.shape[0], cfg.NP - pos), w.dtype))
    return jnp.concatenate(cols, axis=1)


def _unpad_w_in(cfg, wp):
    return jnp.concatenate([wp[:, cfg.off[nm]:cfg.off[nm] + width] for nm, width in cfg.ref_segs], axis=1)


def _to_slots_cols(w):
    R = w.shape[0]
    return w.reshape(R, NDEV, -1).transpose(1, 0, 2)


def _from_slots_cols(s):
    return s.transpose(1, 0, 2).reshape(s.shape[1], -1)


def _perm_rows_out(cfg, w):
    return jnp.concatenate([w[cfg.A:], w[:cfg.A]], axis=0)


def _unperm_rows_out(cfg, w):
    return jnp.concatenate([w[cfg.B + cfg.C:], w[:cfg.B + cfg.C]], axis=0)


def _layer_params(cfg, l, g_pre, a_g_v, a_w_s, a_b_s, c_g_q, c_g_kv, g_out):
    A, B = cfg.A, cfg.B
    return dict(g_pre=g_pre[l][None], g_v=a_g_v[l].reshape(1, A), w_s=a_w_s[l], b_s=a_b_s[l][:, :, None],
                g_q=c_g_q[l][None], g_kv=c_g_kv[l][None],
                g_oa=g_out[l][None, :A], g_ob=g_out[l][None, A:A + B], g_oc=g_out[l][None, A + B:])


def _layer_fwd(cfg, l, x, W, p, cos2, sin2, rot, carry_stick=None, carry_mla=None):
    S, D, A, B, C = cfg.S, cfg.D, cfg.A, cfg.B, cfg.C
    tag = f"l{l}"
    (h,) = _rowwise(_f_pre, [(x, D, 0)], [], [p["g_pre"]], [], [(D, BF16)], 256, f"pre_{tag}")
    proj = _matmul(h, W["in"], "nn", BF16, f"mm_in_{tag}")
    a_rows = [cfg.view(proj, "ua"), cfg.view(proj, "va"), cfg.view(proj, "za")]
    a_par = [p["g_v"], p["w_s"], p["b_s"], p["g_oa"]]
    (ya,) = _rowwise(_f_gmlp, a_rows, [], a_par, [], [(A, BF16)], LANE, f"gmlp_{tag}")
    qb, kb, vb = cfg.heads_view(proj, "qb"), cfg.heads_view(proj, "kb"), cfg.heads_view(proj, "vb")
    yb, tot, *moved_stick = _stick_fwd(qb, kb, vb, cfg.Hb, LANE ** -0.5, f"stick_fwd_{tag}", *cfg.tiles("stick_fwd", l),
                                       carry=carry_stick)
    (ybg,) = _rowwise(_f_gate, [(yb, B, 0), cfg.view(proj, "zb")], [], [p["g_ob"]], [], [(B, BF16)], 256, f"gate_b_{tag}")
    c_rows = [cfg.view(proj, "cq"), cfg.view(proj, "ckv"), cfg.view(proj, "kr")]
    trig = [(cos2, LANE, 0), (sin2, LANE, 0)]
    cqn, ckvn, krr = _rowwise(_f_cpre, c_rows, trig, [p["g_q"], p["g_kv"]], [rot],
                              [(cfg.Q, BF16), (cfg.KV, BF16), (LANE, BF16)], 256, f"cpre_{tag}")
    q_raw = _matmul(cqn, W["uq"], "nn", BF16, f"mm_uq_{tag}")
    kv = _matmul(ckvn, W["ukv"], "nn", BF16, f"mm_ukv_{tag}")
    r_rows = [(q_raw, 2 * C, 0), (kv, 2 * C, 0), (krr, LANE, 0)]
    q_rot, k_full, v_c = _rowwise(_f_crope, r_rows, trig, [], [rot], [(2 * C, BF16), (2 * C, BF16), (C, BF16)], 128,
                                  f"crope_{tag}")
    qc, kc, vc = (q_rot, 0, 2 * LANE), (k_full, 0, 2 * LANE), (v_c, 0, LANE)
    yc, lse, *moved_mla = _softmax_fwd(qc, kc, vc, cfg.Hc, (LANE + ROPE) ** -0.5, f"mla_fwd_{tag}", *cfg.tiles("mla_fwd", l),
                                       carry=carry_mla)
    (ycg,) = _rowwise(_f_gate, [(yc, C, 0), cfg.view(proj, "zc")], [], [p["g_oc"]], [], [(C, BF16)], 256, f"gate_c_{tag}")
    y = jnp.concatenate([ybg, ycg, ya], axis=1)
    out = _matmul(y, W["out"], "nn", F32, f"mm_out_{tag}", add=x)
    saved = dict(x=x, h=h, proj=proj, yb=yb, tot=tot, cqn=cqn, ckvn=ckvn, krr=krr, q_raw=q_raw, kv=kv,
                 q_rot=q_rot, k_full=k_full, v_c=v_c, yc=yc, lse=lse, y=y)
    return out, saved, (moved_stick[0] if moved_stick else []), (moved_mla[0] if moved_mla else [])


def _layer_bwd(cfg, l, dout, sv, W, p, cos2, sin2, rot, ext_stick, ext_mla, last):
    S, D, A, B, C = cfg.S, cfg.D, cfg.A, cfg.B, cfg.C
    tag = f"l{l}"
    proj = sv["proj"]
    dy = _matmul(dout, W["out"], "nt", BF16, f"mm_dy_{tag}")
    d_wout = _matmul(sv["y"], dout, "tn", BF16, f"mm_dwout_{tag}")
    wout_slots = _unperm_rows_out(cfg, d_wout).reshape(NDEV, cfg.DMIX // NDEV, D)
    (dyb, dzb), (dg_ob,), _ = _rowwise_vjp(_f_gate, [(sv["yb"], B, 0), cfg.view(proj, "zb")], [], [p["g_ob"]], [],
                                           [(dy, B, 0)], [BF16, BF16], 256, f"gate_b_bwd_{tag}")
    (dyc, dzc), (dg_oc,), _ = _rowwise_vjp(_f_gate, [(sv["yc"], C, 0), cfg.view(proj, "zc")], [], [p["g_oc"]], [],
                                           [(dy, C, B // C)], [BF16, BF16], 256, f"gate_c_bwd_{tag}")
    a_rows = [cfg.view(proj, "ua"), cfg.view(proj, "va"), cfg.view(proj, "za")]
    a_par = [p["g_v"], p["w_s"], p["b_s"], p["g_oa"]]
    (dua, dva, dza), (dg_v, dw_s, db_s, dg_oa), _ = _rowwise_vjp(
        _f_gmlp, a_rows, [], a_par, [], [(dy, A, (B + C) // A)], [BF16] * 3, LANE, f"gmlp_bwd_{tag}")
    qb, kb, vb = cfg.heads_view(proj, "qb"), cfg.heads_view(proj, "kb"), cfg.heads_view(proj, "vb")
    dqb, dkb, dvb, moved_stick = _stick_bwd(qb, kb, vb, (dyb, 0, LANE), sv["tot"], cfg.Hb, LANE ** -0.5,
                                            f"stick_bwd_{tag}", *cfg.tiles("stick_bwd", l),
                                            carry=_Exchange([[a] for a in ext_stick + [wout_slots]], False))
    got = dict(w_out=moved_stick[-1][0])
    ext_got = [mv[0] for mv in moved_stick[:-1]]
    qc, kc, vc = (sv["q_rot"], 0, 2 * LANE), (sv["k_full"], 0, 2 * LANE), (sv["v_c"], 0, LANE)
    dq_rot, dk_full, dv_c, *moved_mla = _softmax_bwd(qc, kc, vc, (sv["yc"], 0, LANE), (dyc, 0, LANE), sv["lse"], cfg.Hc,
                                                     (LANE + ROPE) ** -0.5, f"mla_bwd_{tag}", *cfg.tiles("mla_bwd", l),
                                                     carry=_Exchange([[a] for a in ext_mla], False) if ext_mla else None)
    ext_got += [mv[0] for mv in (moved_mla[0] if moved_mla else [])]
    trig = [(cos2, LANE, 0), (sin2, LANE, 0)]
    r_rows = [(sv["q_raw"], 2 * C, 0), (sv["kv"], 2 * C, 0), (sv["krr"], LANE, 0)]
    (dq_raw, dkv, dkrr), _, _ = _rowwise_vjp(_f_crope, r_rows, trig, [], [rot],
                                             [(dq_rot, 2 * C, 0), (dk_full, 2 * C, 0), (dv_c, C, 0)], [BF16] * 3, 128,
                                             f"crope_bwd_{tag}")
    dcqn = _matmul(dq_raw, W["uq"], "nt", BF16, f"mm_dcq_{tag}")
    d_wuq = _matmul(sv["cqn"], dq_raw, "tn", BF16, f"mm_dwuq_{tag}")
    dckvn = _matmul(dkv, W["ukv"], "nt", BF16, f"mm_dckv_{tag}")
    d_wukv = _matmul(sv["ckvn"], dkv, "tn", BF16, f"mm_dwukv_{tag}")
    c_rows = [cfg.view(proj, "cq"), cfg.view(proj, "ckv"), cfg.view(proj, "kr")]
    (dcq, dckv, dkr), (dg_q, dg_kv), _ = _rowwise_vjp(
        _f_cpre, c_rows, trig, [p["g_q"], p["g_kv"]], [rot],
        [(dcqn, cfg.Q, 0), (dckvn, cfg.KV, 0), (dkrr, LANE, 0)], [BF16] * 3, 256, f"cpre_bwd_{tag}")
    parts = dict(ua=dua, va=dva, za=dza, qb=dqb, kb=dkb, vb=dvb, zb=dzb, zc=dzc, cq=dcq, kr=dkr, ckv=dckv)
    cols, pos = [], 0
    for nm, off in sorted(cfg.off.items(), key=lambda kv_: kv_[1]):
        if off > pos:
            cols.append(jnp.zeros((S, off - pos), BF16))
        cols.append(parts[nm])
        pos = off + parts[nm].shape[1]
    if cfg.NP > pos:
        cols.append(jnp.zeros((S, cfg.NP - pos), BF16))
    dproj = jnp.concatenate(cols, axis=1)
    d_win = _matmul(sv["h"], dproj, "tn", BF16, f"mm_dwin_{tag}")
    to_send = dict(w_in=_to_slots_cols(_unpad_w_in(cfg, d_win)),
                   c_w_uq=_to_slots_cols(d_wuq.reshape(cfg.Q, cfg.Hc, 2 * LANE)[:, :, :LANE + ROPE].reshape(cfg.Q, -1)),
                   c_w_ukv=_to_slots_cols(d_wukv))
    if last:
        dh, moved = _matmul(dproj, W["in"], "nt", BF16, f"mm_dh_{tag}",
                            carry=_Exchange([[to_send[nm]] for nm in ("w_in", "c_w_uq", "c_w_ukv")], False))
        got.update(w_in=moved[0][0], c_w_uq=moved[1][0], c_w_ukv=moved[2][0])
        to_send = {}
    else:
        dh = _matmul(dproj, W["in"], "nt", BF16, f"mm_dh_{tag}")
    (dx,), (dg_pre,), _ = _rowwise_vjp(_f_pre_res, [(sv["x"], D, 0)], [], [p["g_pre"]], [],
                                       [(dh, D, 0), (dout, D, 0)], [F32], 128, f"pre_bwd_{tag}")
    small = dict(g_pre=dg_pre[0], a_g_v=dg_v.reshape(cfg.G, LANE), a_w_s=dw_s, a_b_s=db_s[:, :, 0], c_g_q=dg_q[0],
                 c_g_kv=dg_kv[0], g_out=jnp.concatenate([dg_oa[0], dg_ob[0], dg_oc[0]]))
    return dx, small, got, to_send, ext_got


def _pack_small(vals):
    pieces = []
    for nm in SMALL:
        piece = vals[nm].reshape(-1, LANE)
        pieces.append(jnp.pad(piece, ((0, -piece.shape[0] % 8), (0, 0))))
    packed = jnp.concatenate(pieces, axis=0)
    return jnp.pad(packed, ((0, -packed.shape[0] % SMALL_ROWS), (0, 0)))


def _unpack_small(packed, like):
    out, row = {}, 0
    for nm in SMALL:
        n = like[nm].size // LANE
        out[nm] = packed[row:row + n].reshape(like[nm].shape)
        row += n + (-n % 8)
    return out


def kernel(x, positions, g_pre, w_in, a_g_v, a_w_s, a_b_s, c_g_q, c_g_kv, c_w_uq, c_w_ukv, g_out, w_out, g_final, loss_target, m_g_pre, m_w_in, m_a_g_v, m_a_w_s, m_a_b_s, m_c_g_q, m_c_g_kv, m_c_w_uq, m_c_w_ukv, m_g_out, m_w_out, m_g_final, v_g_pre, v_w_in, v_a_g_v, v_a_w_s, v_a_b_s, v_c_g_q, v_c_g_kv, v_c_w_uq, v_c_w_ukv, v_g_out, v_w_out, v_g_final):
    depth, S, D = w_in.shape[0], x.shape[1], x.shape[2]
    cfg = _Cfg(S, D, a_g_v.shape[1], c_g_q.shape[1], c_g_kv.shape[1], c_w_ukv.shape[2] * NDEV // (2 * LANE), g_out.shape[1])
    weights = dict(g_pre=g_pre, w_in=w_in, a_g_v=a_g_v, a_w_s=a_w_s, a_b_s=a_b_s, c_g_q=c_g_q, c_g_kv=c_g_kv,
                   c_w_uq=c_w_uq, c_w_ukv=c_w_ukv, g_out=g_out, w_out=w_out, g_final=g_final)
    mom_m = dict(g_pre=m_g_pre, w_in=m_w_in, a_g_v=m_a_g_v, a_w_s=m_a_w_s, a_b_s=m_a_b_s, c_g_q=m_c_g_q, c_g_kv=m_c_g_kv,
                 c_w_uq=m_c_w_uq, c_w_ukv=m_c_w_ukv, g_out=m_g_out, w_out=m_w_out, g_final=m_g_final)
    mom_v = dict(g_pre=v_g_pre, w_in=v_w_in, a_g_v=v_a_g_v, a_w_s=v_a_w_s, a_b_s=v_a_b_s, c_g_q=v_c_g_q, c_g_kv=v_c_g_kv,
                 c_w_uq=v_c_w_uq, c_w_ukv=v_c_w_ukv, g_out=v_g_out, w_out=v_w_out, g_final=v_g_final)
    big_names = ("w_in", "c_w_uq", "c_w_ukv", "w_out")

    inv_freq = 1.0 / (ROPE_THETA ** (jnp.arange(0, ROPE, 2, dtype=F32) / ROPE))
    ang = positions[0].astype(F32)[:, None] * inv_freq
    zpad = jnp.zeros((S, LANE - ROPE), F32)
    cos2 = jnp.concatenate([jnp.cos(ang), jnp.cos(ang), zpad], axis=1)
    sin2 = jnp.concatenate([jnp.sin(ang), jnp.sin(ang), zpad], axis=1)
    rot = _rope_matrix()

    def shards(l, names):
        return [[weights[nm][l].astype(BF16)] for nm in names]

    def assemble(g_in, g_uq, g_ukv, g_wout):
        uq = _from_slots_cols(g_uq).reshape(cfg.Q, cfg.Hc, LANE + ROPE)
        uq = jnp.pad(uq, ((0, 0), (0, 0), (0, LANE - ROPE))).reshape(cfg.Q, 2 * cfg.C)
        return {"in": _pad_w_in(cfg, _from_slots_cols(g_in)), "uq": uq, "ukv": _from_slots_cols(g_ukv),
                "out": _perm_rows_out(cfg, g_wout.reshape(cfg.DMIX, D))}

    params = [_layer_params(cfg, l, g_pre, a_g_v, a_w_s, a_b_s, c_g_q, c_g_kv, g_out) for l in range(depth)]

    gathered = [g[0] for g in _exchange(shards(0, big_names), True, "gather_weights_l0")]
    hcur, saved, Ws = x[0], [], []
    for l in range(depth):
        Ws.append(assemble(*gathered))
        nxt = l + 1 < depth
        hcur, sv, got_in, got_rest = _layer_fwd(
            cfg, l, hcur, Ws[l], params[l], cos2, sin2, rot,
            carry_stick=_Exchange(shards(l + 1, big_names[:1]), True) if nxt else None,
            carry_mla=_Exchange(shards(l + 1, big_names[1:]), True) if nxt else None)
        saved.append(sv)
        gathered = [g[0] for g in got_in + got_rest]
    (dh,), (dg_final,), (loss_rows,) = _rowwise_vjp(
        _f_final, [(hcur, D, 0)], [(loss_target[0], D, 0)], [g_final[None]], [], [(jnp.ones((S, 1), F32), 1, 0)],
        [F32], 128, "final", primal=[(1, F32)])
    loss = lax.psum(jnp.sum(loss_rows), MESH_AXES)

    small_g, slots, pending = [None] * depth, [None] * depth, {}
    for l in reversed(range(depth)):
        ext_stick = [pending["w_in"]] if pending else []
        ext_mla = [pending["c_w_uq"], pending["c_w_ukv"]] if pending else []
        dh, small_g[l], slots[l], pending, ext_got = _layer_bwd(cfg, l, dh, saved[l], Ws[l], params[l], cos2, sin2, rot,
                                                                ext_stick, ext_mla, l == 0)
        if ext_got:
            slots[l + 1].update(w_in=ext_got[0], c_w_uq=ext_got[1], c_w_ukv=ext_got[2])
    grad_x = dh[None]
    small_grads = {nm: jnp.stack([small_g[l][nm] for l in range(depth)]) for nm in SMALL if nm != "g_final"}
    small_grads["g_final"] = dg_final[0]
    (small_slots,) = _exchange([[_pack_small(small_grads)]], True, "gather_small_grads")

    res = {}
    for nm in big_names:
        res[nm] = None
        for l in range(depth):
            res[nm] = _adamw(slots[l][nm], weights[nm], mom_m[nm], mom_v[nm], l, res[nm], f"adamw_{nm}_l{l}")
    packed = _adamw(small_slots[0], _pack_small(weights)[None], _pack_small(mom_m)[None], _pack_small(mom_v)[None], 0, None,
                    "adamw_small")
    small_res = [_unpack_small(r[0], weights) for r in packed]
    order = ("g_pre", "w_in", "a_g_v", "a_w_s", "a_b_s", "c_g_q", "c_g_kv", "c_w_uq", "c_w_ukv", "g_out", "w_out", "g_final")
    outs = [loss, grad_x]
    for kind in range(4):
        outs += [small_res[kind][nm] if nm in SMALL else res[nm][kind] for nm in order]
    return tuple(outs)
```

```python
import functools

import numpy as np
import jax
import jax.numpy as jnp
from jax import lax
from jax.experimental import pallas as pl
from jax.experimental.pallas import tpu as pltpu

NDEV = 8
MESH_AXES = ("x", "y", "c")
LANE = 128
ROPE = 64
EPS = 1e-6
ROPE_THETA = 10000.0
ADAM_LR, ADAM_B1, ADAM_B2, ADAM_EPS, ADAM_WD, ADAM_STEP = 0.001, 0.9, 0.999, 1e-08, 0.01, 10
VMEM_LIMIT = 48 * 1024 * 1024
ADAM_TILE_BYTES = 768 * 1024
SMALL_ROWS = 256
ATTN_TILES = {"stick_fwd": [(1024, 256)], "stick_bwd": [(1024, 256)], "mla_fwd": [(512, 1024)], "mla_bwd": [(512, 1024)]}
F32, BF16 = jnp.float32, jnp.bfloat16
SMALL = ("g_pre", "a_g_v", "a_w_s", "a_b_s", "c_g_q", "c_g_kv", "g_out", "g_final")
TRANSPOSED = ("w_in", "c_w_uq")


def _tile(dim, cap, mult=LANE):
    if dim <= cap:
        return dim
    t = (cap // mult) * mult
    while t >= mult:
        if dim % t == 0:
            return t
        t -= mult
    return dim


def _dot_nt(a, b):
    return lax.dot_general(a, b, (((1,), (1,)), ((), ())), preferred_element_type=F32)


def _dot_tn(a, b):
    return lax.dot_general(a, b, (((0,), (0,)), ((), ())), preferred_element_type=F32)


def _dot(a, b):
    return jnp.dot(a, b, preferred_element_type=F32)


class _Exchange:
    def __init__(self, groups, gather):
        self.groups, self.gather = groups, gather
        self.flat = [(gi, li, a) for gi, grp in enumerate(groups) for li, a in enumerate(grp)]
        self.n = len(self.flat)
        self.args = [a for (_, _, a) in self.flat]
        self.out_shape = [jax.ShapeDtypeStruct((len(grp), NDEV) + tuple(grp[0].shape[-2:]), grp[0].dtype) for grp in groups]
        self.scratch = [pltpu.SemaphoreType.DMA((self.n, NDEV - 1)), pltpu.SemaphoreType.DMA((self.n, NDEV - 1)),
                        pltpu.SemaphoreType.DMA((self.n,))]

    def _copies(self, ins, outs, send_sems, recv_sems, local_sems, landings):
        x, y, c = lax.axis_index("x"), lax.axis_index("y"), lax.axis_index("c")
        me = 4 * x + 2 * y + c
        owns = [pltpu.make_async_copy(ins[i] if self.gather else ins[i].at[me], outs[gi].at[li, me], local_sems.at[i])
                for i, (gi, li, _) in enumerate(self.flat)]
        pairs = []
        for k in range(1, NDEV):
            px = 1 - x if k & 4 else x
            py = 1 - y if k & 2 else y
            pc = 1 - c if k & 1 else c
            peer = 4 * px + 2 * py + pc
            for i, (gi, li, _) in enumerate(self.flat):
                src = ins[i] if self.gather else ins[i].at[peer]
                sems = dict(send_sem=send_sems.at[i, k - 1], recv_sem=recv_sems.at[i, k - 1],
                            device_id=(px, py, pc), device_id_type=pl.DeviceIdType.MESH)
                out = pltpu.make_async_remote_copy(src_ref=src, dst_ref=outs[gi].at[li, me], **sems)
                landing = pltpu.make_async_remote_copy(src_ref=src, dst_ref=outs[gi].at[li, peer], **sems) if landings else None
                pairs.append((out, landing))
        return owns, pairs

    def start(self, ins, outs, sems):
        owns, pairs = self._copies(ins, outs, *sems, landings=False)
        for own in owns:
            own.start()
        for out, _ in pairs:
            out.start()

    def wait(self, ins, outs, sems):
        owns, pairs = self._copies(ins, outs, *sems, landings=True)
        for out, landing in pairs:
            out.wait_send()
            landing.wait_recv()
        for own in owns:
            own.wait()


def _call(body, name, grid, in_specs, out_specs, out_shape, scratch, semantics, args, carry=None):
    n_in, n_out, n_scr = len(in_specs), len(out_specs), len(scratch)
    if carry is None:
        run = body
    else:
        semantics = ("arbitrary",) * len(grid)
        anyspec = pl.BlockSpec(memory_space=pl.ANY)
        in_specs = list(in_specs) + [anyspec] * carry.n
        out_specs = list(out_specs) + [anyspec] * len(carry.groups)
        out_shape = list(out_shape) + carry.out_shape
        scratch = list(scratch) + carry.scratch
        args = list(args) + carry.args

        def run(*refs):
            c_in, x_in = refs[:n_in], refs[n_in:n_in + carry.n]
            rest = refs[n_in + carry.n:]
            c_out, x_out = rest[:n_out], rest[n_out:n_out + len(carry.groups)]
            c_scr, sems = rest[n_out + len(carry.groups):len(rest) - 3], rest[len(rest) - 3:]
            first, last = True, True
            for d, extent in enumerate(grid):
                first = jnp.logical_and(first, pl.program_id(d) == 0)
                last = jnp.logical_and(last, pl.program_id(d) == extent - 1)

            @pl.when(first)
            def _():
                carry.start(x_in, x_out, sems)

            body(*c_in, *c_out, *c_scr)

            @pl.when(last)
            def _():
                carry.wait(x_in, x_out, sems)

    res = pl.pallas_call(
        run, name=name, grid=grid, out_shape=list(out_shape), in_specs=list(in_specs), out_specs=list(out_specs),
        scratch_shapes=list(scratch),
        compiler_params=pltpu.CompilerParams(dimension_semantics=semantics, vmem_limit_bytes=VMEM_LIMIT,
                                             has_side_effects=carry is not None),
    )(*args)
    return list(res[:n_out]), list(res[n_out:])


def _exchange(groups, gather, name):
    ex = _Exchange(groups, gather)

    def body(*refs):
        ins, outs, sems = refs[:ex.n], refs[ex.n:ex.n + len(groups)], refs[ex.n + len(groups):]
        ex.start(ins, outs, sems)
        ex.wait(ins, outs, sems)

    anyspec = pl.BlockSpec(memory_space=pl.ANY)
    return pl.pallas_call(
        body, name=name, out_shape=ex.out_shape, in_specs=[anyspec] * ex.n, out_specs=[anyspec] * len(groups),
        scratch_shapes=ex.scratch, compiler_params=pltpu.CompilerParams(has_side_effects=True),
    )(*ex.args)


def _matmul(a, b, mode, out_dtype, name, add=None, tm=1024, tn=1024, tk=1024, carry=None):
    if mode == "tn":
        (K, M), (K2, N) = a.shape, b.shape
    elif mode == "nt":
        (M, K), (N, K2) = a.shape, b.shape
    else:
        (M, K), (K2, N) = a.shape, b.shape
    assert K == K2, (a.shape, b.shape, mode)
    tm, tn, tk = _tile(M, tm), _tile(N, tn), _tile(K, tk)
    nk = K // tk
    a_spec = pl.BlockSpec((tk, tm), lambda i, j, k: (k, i)) if mode == "tn" else pl.BlockSpec((tm, tk), lambda i, j, k: (i, k))
    b_spec = pl.BlockSpec((tn, tk), lambda i, j, k: (j, k)) if mode == "nt" else pl.BlockSpec((tk, tn), lambda i, j, k: (k, j))
    dot = {"nn": _dot, "nt": _dot_nt, "tn": _dot_tn}[mode]
    has_add = add is not None

    def body(*refs):
        a_ref, b_ref = refs[0], refs[1]
        o_ref, acc = refs[-2], refs[-1]
        k = pl.program_id(2)

        @pl.when(k == 0)
        def _():
            acc[...] = jnp.zeros_like(acc)

        acc[...] += dot(a_ref[...].astype(BF16), b_ref[...].astype(BF16))

        @pl.when(k == nk - 1)
        def _():
            r = acc[...]
            if has_add:
                r = r + refs[2][...]
            o_ref[...] = r.astype(o_ref.dtype)

    in_specs = [a_spec, b_spec]
    args = [a, b]
    if has_add:
        in_specs.append(pl.BlockSpec((tm, tn), lambda i, j, k: (i, j)))
        args.append(add)
    (out,), moved = _call(body, name, (M // tm, N // tn, nk), in_specs, [pl.BlockSpec((tm, tn), lambda i, j, k: (i, j))],
                          [jax.ShapeDtypeStruct((M, N), out_dtype)], [pltpu.VMEM((tm, tn), F32)],
                          ("parallel", "parallel", "arbitrary"), args, carry)
    return out if carry is None else (out, moved)


def _row_specs(views, tile):
    return [pl.BlockSpec((tile, w), functools.partial(lambda i, cb: (i, cb), cb=cb)) for (_, w, cb) in views]


def _full_specs(arrs):
    return [pl.BlockSpec(p.shape, functools.partial(lambda i, nd: (0,) * nd, nd=p.ndim)) for p in arrs]


def _rowwise(fn, rows, aux, params, consts, outs, tile, name):
    S = rows[0][0].shape[0]
    nr, na, npar, nc = len(rows), len(aux), len(params), len(consts)

    def body(*refs):
        ins = [r[...].astype(F32) for r in refs[:nr + na]]
        small = [r[...] for r in refs[nr + na:nr + na + npar + nc]]
        res = fn(*ins, *small)
        for o_ref, r in zip(refs[nr + na + npar + nc:], res):
            o_ref[...] = r.astype(o_ref.dtype)

    return pl.pallas_call(
        body, name=name, grid=(S // tile,),
        out_shape=[jax.ShapeDtypeStruct((S, w), dt) for (w, dt) in outs],
        in_specs=_row_specs(rows + aux, tile) + _full_specs(params + consts),
        out_specs=[pl.BlockSpec((tile, w), lambda i: (i, 0)) for (w, _) in outs],
        compiler_params=pltpu.CompilerParams(dimension_semantics=("parallel",), vmem_limit_bytes=VMEM_LIMIT),
    )(*[v[0] for v in rows + aux], *params, *consts)


def _rowwise_vjp(fn, rows, aux, params, consts, cots, grad_dtypes, tile, name, primal=()):
    S = rows[0][0].shape[0]
    nr, na, npar, nc, nct, npr = len(rows), len(aux), len(params), len(consts), len(cots), len(primal)

    def body(*refs):
        n_in = nr + na + npar + nc + nct
        rv = [r[...].astype(F32) for r in refs[:nr]]
        av = [r[...].astype(F32) for r in refs[nr:nr + na]]
        pv = [r[...] for r in refs[nr + na:nr + na + npar]]
        cv = [r[...] for r in refs[nr + na + npar:nr + na + npar + nc]]
        ct = tuple(r[...].astype(F32) for r in refs[nr + na + npar + nc:n_in])
        res, vjp = jax.vjp(lambda *rp: tuple(fn(*rp[:nr], *av, *rp[nr:], *cv)), *rv, *pv)
        grads = vjp(ct)
        g_refs = refs[n_in:n_in + nr]
        p_refs = refs[n_in + nr:n_in + nr + npar]
        o_refs = refs[n_in + nr + npar:]
        for g_ref, g in zip(g_refs, grads[:nr]):
            g_ref[...] = g.astype(g_ref.dtype)

        @pl.when(pl.program_id(0) == 0)
        def _():
            for p_ref in p_refs:
                p_ref[...] = jnp.zeros_like(p_ref)

        for p_ref, g in zip(p_refs, grads[nr:]):
            p_ref[...] += g
        for o_ref, r in zip(o_refs, res[:npr]):
            o_ref[...] = r.astype(o_ref.dtype)

    out_shape = ([jax.ShapeDtypeStruct((S, w), dt) for (_, w, _), dt in zip(rows, grad_dtypes)]
                 + [jax.ShapeDtypeStruct(p.shape, F32) for p in params]
                 + [jax.ShapeDtypeStruct((S, w), dt) for (w, dt) in primal])
    out_specs = ([pl.BlockSpec((tile, w), lambda i: (i, 0)) for (_, w, _) in rows] + _full_specs(params)
                 + [pl.BlockSpec((tile, w), lambda i: (i, 0)) for (w, _) in primal])
    res = pl.pallas_call(
        body, name=name, grid=(S // tile,), out_shape=out_shape,
        in_specs=_row_specs(rows + aux, tile) + _full_specs(params + consts) + _row_specs(cots, tile),
        out_specs=out_specs,
        compiler_params=pltpu.CompilerParams(dimension_semantics=("arbitrary",), vmem_limit_bytes=VMEM_LIMIT),
    )(*[v[0] for v in rows + aux], *params, *consts, *[v[0] for v in cots])
    return res[:nr], res[nr:nr + npar], res[nr + npar:]


@jax.custom_vjp
def _mm(a, b):
    return _dot(a.astype(BF16), b.astype(BF16))


def _mm_fwd(a, b):
    return _mm(a, b), (a, b)


def _mm_bwd(res, ct):
    a, b = res
    ctb = ct.astype(BF16)
    return _dot_nt(ctb, b.astype(BF16)), _dot_tn(a.astype(BF16), ctb)


_mm.defvjp(_mm_fwd, _mm_bwd)


def _rms(x, g):
    return x * lax.rsqrt(jnp.mean(x * x, axis=-1, keepdims=True) + EPS) * g


def _f_pre(x, g):
    return (_rms(x, g),)


def _f_pre_res(x, g):
    return _rms(x, g), x


def _f_gate(y, z, g):
    return (_rms(y, g) * jax.nn.silu(z),)


def _f_gmlp(u, v, z, g_v, w_s, b_s, g_o):
    groups = w_s.shape[0]
    u, v = jax.nn.gelu(u), jax.nn.gelu(v)
    t_idx = lax.broadcasted_iota(jnp.int32, (LANE, LANE), 0)
    s_idx = lax.broadcasted_iota(jnp.int32, (LANE, LANE), 1)
    ys = []
    for g in range(groups):
        sl = slice(g * LANE, (g + 1) * LANE)
        vn = _rms(v[:, sl], g_v[:, sl])
        w = jnp.where(s_idx <= t_idx, w_s[g], 0.0)
        ys.append(u[:, sl] * (_mm(w, vn) + b_s[g]))
    return (_rms(jnp.concatenate(ys, axis=1), g_o) * jax.nn.silu(z),)


def _rope(x, cos2, sin2, rot):
    return x * cos2 + _mm(x, rot) * sin2


def _f_cpre(cq, ckv, kr, cos2, sin2, g_q, g_kv, rot):
    return _rms(cq, g_q), _rms(ckv, g_kv), _rope(kr, cos2, sin2, rot)


def _f_crope(q, kv, krr, cos2, sin2, rot):
    heads = q.shape[1] // (2 * LANE)
    qs, ks, vs = [], [], []
    for h in range(heads):
        lo, mid, hi = 2 * h * LANE, (2 * h + 1) * LANE, (2 * h + 2) * LANE
        qs += [q[:, lo:mid], _rope(q[:, mid:hi], cos2, sin2, rot)]
        ks += [kv[:, lo:mid], krr]
        vs += [kv[:, mid:hi]]
    return jnp.concatenate(qs, axis=1), jnp.concatenate(ks, axis=1), jnp.concatenate(vs, axis=1)


def _f_final(h, target, g):
    err = _rms(h, g) - target
    return (0.5 * jnp.mean(err * err, axis=-1, keepdims=True),)


def _rope_matrix():
    r = np.zeros((LANE, LANE), np.float32)
    half = ROPE // 2
    for i in range(half):
        r[i + half, i] = -1.0
        r[i, i + half] = 1.0
    return jnp.asarray(r)


def _head_spec(view, rows, n_rows_block):
    _, cb0, w = view
    if n_rows_block:
        return pl.BlockSpec((rows, w), functools.partial(lambda h, i, cb0: (i, cb0 + h), cb0=cb0))
    return pl.BlockSpec((rows, w), functools.partial(lambda h, i, cb0: (0, cb0 + h), cb0=cb0))


def _stat_spec(tq):
    return pl.BlockSpec((1, tq, 1), lambda h, i: (h, i, 0))


def _softplus(z):
    return jnp.maximum(z, 0.0) + jnp.log(1.0 + jnp.exp(-jnp.abs(z)))


def _cumsum_mm(x, m01):
    hi = x.astype(BF16)
    lo = (x - hi.astype(F32)).astype(BF16)
    return _dot(hi, m01) + _dot(lo, m01)


def _attn_call(body, name, heads, S, tq, ins, in_blocked, outs, out_blocked, scratch, stats_in=0, stats_out=0, carry=None):
    in_specs = [_head_spec(v, tq if blk else S, blk) for v, blk in zip(ins[:len(ins) - stats_in], in_blocked)]
    in_specs += [_stat_spec(tq)] * stats_in
    out_specs = [_head_spec((None, 0, w), tq if blk else S, blk) for (w, _), blk in zip(outs, out_blocked)]
    out_specs += [_stat_spec(tq)] * stats_out
    out_shape = [jax.ShapeDtypeStruct((S, heads * w), dt) for (w, dt) in outs]
    out_shape += [jax.ShapeDtypeStruct((heads, S, 1), F32)] * stats_out
    args = [v[0] for v in ins[:len(ins) - stats_in]] + list(ins[len(ins) - stats_in:])
    res, moved = _call(body, name, (heads, S // tq), in_specs, out_specs, out_shape, scratch, ("arbitrary", "arbitrary"),
                       args, carry)
    return res if carry is None else res + [moved]


def _softmax_fwd(q, k, v, heads, scale, name, tq, bk, carry=None):
    S, dv = q[0].shape[0], v[2]

    def body(q_ref, k_ref, v_ref, o_ref, lse_ref):
        qi = pl.program_id(1)
        qv = q_ref[...]
        row = qi * tq + lax.broadcasted_iota(jnp.int32, (tq, bk), 0)
        col0 = lax.broadcasted_iota(jnp.int32, (tq, bk), 1)

        def step(kb, carry):
            m, l, acc = carry
            sl = pl.ds(pl.multiple_of(kb * bk, bk), bk)
            s = _dot_nt(qv, k_ref[sl, :]) * scale
            s = jnp.where(kb * bk + col0 <= row, s, -1e30)
            m_new = jnp.maximum(m, jnp.max(s, axis=1, keepdims=True))
            p = jnp.exp(s - m_new)
            alpha = jnp.exp(m - m_new)
            l = alpha * l + jnp.sum(p, axis=1, keepdims=True)
            acc = alpha * acc + _dot(p.astype(BF16), v_ref[sl, :])
            return m_new, l, acc

        n_kb = (qi * tq + tq + bk - 1) // bk
        m, l, acc = lax.fori_loop(0, n_kb, step, (jnp.full((tq, 1), -1e30, F32), jnp.zeros((tq, 1), F32),
                                                  jnp.zeros((tq, dv), F32)))
        o_ref[...] = (acc / l).astype(o_ref.dtype)
        lse_ref[0] = m + jnp.log(l)

    return _attn_call(body, name, heads, S, tq, [q, k, v], [1, 0, 0], [(dv, BF16)], [1], [], stats_out=1, carry=carry)


def _softmax_bwd(q, k, v, o, do, lse, heads, scale, name, tq, bk, carry=None):
    S, dq_w, dv = q[0].shape[0], q[2], v[2]
    nq = S // tq

    def body(q_ref, k_ref, v_ref, o_ref, do_ref, lse_ref, dq_ref, dk_ref, dv_ref, dk_acc, dv_acc):
        qi = pl.program_id(1)

        @pl.when(qi == 0)
        def _():
            dk_acc[...] = jnp.zeros_like(dk_acc)
            dv_acc[...] = jnp.zeros_like(dv_acc)

        qv, dov = q_ref[...], do_ref[...]
        delta = jnp.sum(dov.astype(F32) * o_ref[...].astype(F32), axis=1, keepdims=True)
        lse_v = lse_ref[0]
        row = qi * tq + lax.broadcasted_iota(jnp.int32, (tq, bk), 0)
        col0 = lax.broadcasted_iota(jnp.int32, (tq, bk), 1)

        def step(kb, dq):
            sl = pl.ds(pl.multiple_of(kb * bk, bk), bk)
            ks, vs = k_ref[sl, :], v_ref[sl, :]
            s = _dot_nt(qv, ks) * scale
            p = jnp.where(kb * bk + col0 <= row, jnp.exp(s - lse_v), 0.0)
            ds = (p * (_dot_nt(dov, vs) - delta) * scale).astype(BF16)
            dk_acc[sl, :] += _dot_tn(ds, qv)
            dv_acc[sl, :] += _dot_tn(p.astype(BF16), dov)
            return dq + _dot(ds, ks)

        n_kb = (qi * tq + tq + bk - 1) // bk
        dq_ref[...] = lax.fori_loop(0, n_kb, step, jnp.zeros((tq, dq_w), F32)).astype(dq_ref.dtype)

        @pl.when(qi == nq - 1)
        def _():
            dk_ref[...] = dk_acc[...].astype(dk_ref.dtype)
            dv_ref[...] = dv_acc[...].astype(dv_ref.dtype)

    return _attn_call(body, name, heads, S, tq, [q, k, v, o, do, lse], [1, 0, 0, 1, 1],
                      [(dq_w, BF16), (dq_w, BF16), (dv, BF16)], [1, 0, 0],
                      [pltpu.VMEM((S, dq_w), F32), pltpu.VMEM((S, dv), F32)], stats_in=1, carry=carry)


def _stick_fwd(q, k, v, heads, scale, name, tq, bk, carry=None):
    S, dv = q[0].shape[0], v[2]

    def body(q_ref, k_ref, v_ref, o_ref, tot_ref):
        qi = pl.program_id(1)
        qv = q_ref[...]
        row = qi * tq + lax.broadcasted_iota(jnp.int32, (tq, bk), 0)
        col0 = lax.broadcasted_iota(jnp.int32, (tq, bk), 1)
        m_gt = (lax.broadcasted_iota(jnp.int32, (bk, bk), 0) > lax.broadcasted_iota(jnp.int32, (bk, bk), 1)).astype(BF16)
        n_kb = (qi * tq + tq + bk - 1) // bk

        def step(it, carry):
            c, acc = carry
            kb = n_kb - 1 - it
            sl = pl.ds(pl.multiple_of(kb * bk, bk), bk)
            z = _dot_nt(qv, k_ref[sl, :]) * scale
            mask = kb * bk + col0 < row
            sp = _softplus(z)
            lk = jnp.where(mask, -sp, 0.0)
            after = _cumsum_mm(lk, m_gt) + c
            a = jnp.where(mask, jnp.exp(z - sp + after), 0.0)
            acc = acc + _dot(a.astype(BF16), v_ref[sl, :])
            return c + jnp.sum(lk, axis=1, keepdims=True), acc

        c, acc = lax.fori_loop(0, n_kb, step, (jnp.zeros((tq, 1), F32), jnp.zeros((tq, dv), F32)))
        o_ref[...] = acc.astype(o_ref.dtype)
        tot_ref[0] = c

    return _attn_call(body, name, heads, S, tq, [q, k, v], [1, 0, 0], [(dv, BF16)], [1], [], stats_out=1, carry=carry)


def _stick_bwd(q, k, v, do, tot, heads, scale, name, tq, bk, carry=None):
    S, dq_w, dv = q[0].shape[0], q[2], v[2]
    nq = S // tq

    def body(q_ref, k_ref, v_ref, do_ref, tot_ref, dq_ref, dk_ref, dv_ref, dk_acc, dv_acc):
        qi = pl.program_id(1)

        @pl.when(qi == 0)
        def _():
            dk_acc[...] = jnp.zeros_like(dk_acc)
            dv_acc[...] = jnp.zeros_like(dv_acc)

        qv, dov = q_ref[...], do_ref[...]
        tot_v = tot_ref[0]
        row = qi * tq + lax.broadcasted_iota(jnp.int32, (tq, bk), 0)
        col0 = lax.broadcasted_iota(jnp.int32, (tq, bk), 1)
        j_idx = lax.broadcasted_iota(jnp.int32, (bk, bk), 0)
        s_idx = lax.broadcasted_iota(jnp.int32, (bk, bk), 1)
        m_le, m_lt = (j_idx <= s_idx).astype(BF16), (j_idx < s_idx).astype(BF16)

        def step(kb, carry):
            pc, gc, dq = carry
            sl = pl.ds(pl.multiple_of(kb * bk, bk), bk)
            ks, vs = k_ref[sl, :], v_ref[sl, :]
            z = _dot_nt(qv, ks) * scale
            mask = kb * bk + col0 < row
            sp = _softplus(z)
            lk = jnp.where(mask, -sp, 0.0)
            after = tot_v - pc - _cumsum_mm(lk, m_le)
            log_beta = z - sp
            a = jnp.where(mask, jnp.exp(log_beta + after), 0.0)
            g = _dot_nt(dov, vs) * a
            cg = gc + _cumsum_mm(g, m_lt)
            dz = (jnp.where(mask, g * jnp.exp(-sp) - jnp.exp(log_beta) * cg, 0.0) * scale).astype(BF16)
            dk_acc[sl, :] += _dot_tn(dz, qv)
            dv_acc[sl, :] += _dot_tn(a.astype(BF16), dov)
            return (pc + jnp.sum(lk, axis=1, keepdims=True), gc + jnp.sum(g, axis=1, keepdims=True), dq + _dot(dz, ks))

        n_kb = (qi * tq + tq + bk - 1) // bk
        zero = jnp.zeros((tq, 1), F32)
        _, _, dq = lax.fori_loop(0, n_kb, step, (zero, zero, jnp.zeros((tq, dq_w), F32)))
        dq_ref[...] = dq.astype(dq_ref.dtype)

        @pl.when(qi == nq - 1)
        def _():
            dk_ref[...] = dk_acc[...].astype(dk_ref.dtype)
            dv_ref[...] = dv_acc[...].astype(dv_ref.dtype)

    return _attn_call(body, name, heads, S, tq, [q, k, v, do, tot], [1, 0, 0, 1],
                      [(dq_w, BF16), (dq_w, BF16), (dv, BF16)], [1, 0, 0],
                      [pltpu.VMEM((S, dq_w), F32), pltpu.VMEM((S, dv), F32)], stats_in=1, carry=carry)


def _adamw(slots, w, m, v, layer, prev, name):
    _, R, C = slots.shape
    L = w.shape[0]
    item = slots.dtype.itemsize
    tc = _tile(C, 2048)
    tr = _tile(R, max(16, ADAM_TILE_BYTES // (item * tc)), mult=16)
    if tr == R and R * tc * item > ADAM_TILE_BYTES:
        tc = _tile(C, max(LANE, ADAM_TILE_BYTES // (item * R)))
    c1, c2 = 1.0 - ADAM_B1 ** ADAM_STEP, 1.0 - ADAM_B2 ** ADAM_STEP
    n_prev = 0 if prev is None else 4

    def body(s_ref, w_ref, m_ref, v_ref, *rest):
        g_out, d_out, m_out, v_out = rest[n_prev:]
        g = s_ref[0].astype(F32)
        for k in range(1, NDEV):
            g = g + s_ref[k].astype(F32)
        m_new = ADAM_B1 * m_ref[0] + (1.0 - ADAM_B1) * g
        v_new = ADAM_B2 * v_ref[0] + (1.0 - ADAM_B2) * (g * g)
        g_out[0] = g
        m_out[0] = m_new
        v_out[0] = v_new
        d_out[0] = -ADAM_LR * ((m_new / c1) / (jnp.sqrt(v_new / c2) + ADAM_EPS) + ADAM_WD * w_ref[0])

    spec = pl.BlockSpec((1, tr, tc), lambda i, j: (layer, i, j))
    in_specs = [pl.BlockSpec((NDEV, tr, tc), lambda i, j: (0, i, j)), spec, spec, spec]
    in_specs += [pl.BlockSpec(memory_space=pl.ANY)] * n_prev
    return pl.pallas_call(
        body, name=name, grid=(R // tr, C // tc), out_shape=[jax.ShapeDtypeStruct((L, R, C), F32)] * 4,
        in_specs=in_specs, out_specs=[spec] * 4, input_output_aliases={4 + i: i for i in range(n_prev)},
        compiler_params=pltpu.CompilerParams(dimension_semantics=("parallel", "parallel"), vmem_limit_bytes=VMEM_LIMIT),
    )(slots, w, m, v, *(prev or []))


class _Cfg:
    def __init__(self, S, D, groups, q_lora, kv_lora, c_heads, d_mix):
        self.S, self.D, self.G, self.Q, self.KV, self.Hc, self.DMIX = S, D, groups, q_lora, kv_lora, c_heads, d_mix
        self.A, self.C = groups * LANE, c_heads * LANE
        self.B = d_mix - self.A - self.C
        self.Hb = self.B // LANE
        A, B, C = self.A, self.B, self.C
        assert B % LANE == 0 and B % C == 0 and (B + C) % A == 0
        self.ref_segs = [("ua", A), ("va", A), ("za", A), ("qb", B), ("kb", B), ("vb", B), ("zb", B),
                         ("cq", q_lora), ("ckv", kv_lora), ("kr", ROPE), ("zc", C)]
        self.off, off = {}, 0
        for nm, w in [("ua", A), ("va", A), ("za", A), ("qb", B), ("kb", B), ("vb", B), ("zb", B), ("zc", C),
                      ("cq", q_lora), ("kr", LANE), ("ckv", kv_lora)]:
            off = -(-off // w) * w
            self.off[nm] = off
            off += w
        self.NP = -(-off // 512) * 512
        self.width = {"kr": LANE, **{nm: w for nm, w in self.ref_segs if nm != "kr"}}

    def tiles(self, kind, layer):
        tq, bk = ATTN_TILES[kind][layer % len(ATTN_TILES[kind])]
        return min(tq, self.S), min(bk, self.S)

    def view(self, arr, nm):
        w = self.width[nm]
        return (arr, w, self.off[nm] // w)

    def heads_view(self, arr, nm):
        return (arr, self.off[nm] // LANE, LANE)


def _pad_w_in(cfg, wt):
    pieces, start = {}, 0
    for nm, width in cfg.ref_segs:
        pieces[nm] = wt[start:start + width]
        start += width
    rows, pos = [], 0
    for nm, off in sorted(cfg.off.items(), key=lambda kv: kv[1]):
        if off > pos:
            rows.append(jnp.zeros((off - pos, wt.shape[1]), wt.dtype))
        rows.append(pieces[nm])
        pos = off + pieces[nm].shape[0]
    if cfg.NP > pos:
        rows.append(jnp.zeros((cfg.NP - pos, wt.shape[1]), wt.dtype))
    return jnp.concatenate(rows, axis=0)


def _unpad_w_in(cfg, wpt):
    return jnp.concatenate([wpt[cfg.off[nm]:cfg.off[nm] + width] for nm, width in cfg.ref_segs], axis=0)


def _to_slots_cols(w):
    R = w.shape[0]
    return w.reshape(R, NDEV, -1).transpose(1, 0, 2)


def _from_slots_cols(s):
    return s.transpose(1, 0, 2).reshape(s.shape[1], -1)


def _perm_rows_out(cfg, w):
    return jnp.concatenate([w[cfg.A:], w[:cfg.A]], axis=0)


def _unperm_rows_out(cfg, w):
    return jnp.concatenate([w[cfg.B + cfg.C:], w[:cfg.B + cfg.C]], axis=0)


def _layer_params(cfg, l, g_pre, a_g_v, a_w_s, a_b_s, c_g_q, c_g_kv, g_out):
    A, B = cfg.A, cfg.B
    return dict(g_pre=g_pre[l][None], g_v=a_g_v[l].reshape(1, A), w_s=a_w_s[l], b_s=a_b_s[l][:, :, None],
                g_q=c_g_q[l][None], g_kv=c_g_kv[l][None],
                g_oa=g_out[l][None, :A], g_ob=g_out[l][None, A:A + B], g_oc=g_out[l][None, A + B:])


def _layer_fwd(cfg, l, x, W, p, cos2, sin2, rot, carry_stick=None, carry_mla=None):
    S, D, A, B, C = cfg.S, cfg.D, cfg.A, cfg.B, cfg.C
    tag = f"l{l}"
    (h,) = _rowwise(_f_pre, [(x, D, 0)], [], [p["g_pre"]], [], [(D, BF16)], 256, f"pre_{tag}")
    proj = _matmul(h, W["in"], "nt", BF16, f"mm_in_{tag}")
    a_rows = [cfg.view(proj, "ua"), cfg.view(proj, "va"), cfg.view(proj, "za")]
    a_par = [p["g_v"], p["w_s"], p["b_s"], p["g_oa"]]
    (ya,) = _rowwise(_f_gmlp, a_rows, [], a_par, [], [(A, BF16)], LANE, f"gmlp_{tag}")
    qb, kb, vb = cfg.heads_view(proj, "qb"), cfg.heads_view(proj, "kb"), cfg.heads_view(proj, "vb")
    yb, tot, *moved_stick = _stick_fwd(qb, kb, vb, cfg.Hb, LANE ** -0.5, f"stick_fwd_{tag}", *cfg.tiles("stick_fwd", l),
                                       carry=carry_stick)
    (ybg,) = _rowwise(_f_gate, [(yb, B, 0), cfg.view(proj, "zb")], [], [p["g_ob"]], [], [(B, BF16)], 256, f"gate_b_{tag}")
    c_rows = [cfg.view(proj, "cq"), cfg.view(proj, "ckv"), cfg.view(proj, "kr")]
    trig = [(cos2, LANE, 0), (sin2, LANE, 0)]
    cqn, ckvn, krr = _rowwise(_f_cpre, c_rows, trig, [p["g_q"], p["g_kv"]], [rot],
                              [(cfg.Q, BF16), (cfg.KV, BF16), (LANE, BF16)], 256, f"cpre_{tag}")
    q_raw = _matmul(cqn, W["uq"], "nt", BF16, f"mm_uq_{tag}")
    kv = _matmul(ckvn, W["ukv"], "nn", BF16, f"mm_ukv_{tag}")
    r_rows = [(q_raw, 2 * C, 0), (kv, 2 * C, 0), (krr, LANE, 0)]
    q_rot, k_full, v_c = _rowwise(_f_crope, r_rows, trig, [], [rot], [(2 * C, BF16), (2 * C, BF16), (C, BF16)], 128,
                                  f"crope_{tag}")
    qc, kc, vc = (q_rot, 0, 2 * LANE), (k_full, 0, 2 * LANE), (v_c, 0, LANE)
    yc, lse, *moved_mla = _softmax_fwd(qc, kc, vc, cfg.Hc, (LANE + ROPE) ** -0.5, f"mla_fwd_{tag}", *cfg.tiles("mla_fwd", l),
                                       carry=carry_mla)
    (ycg,) = _rowwise(_f_gate, [(yc, C, 0), cfg.view(proj, "zc")], [], [p["g_oc"]], [], [(C, BF16)], 256, f"gate_c_{tag}")
    y = jnp.concatenate([ybg, ycg, ya], axis=1)
    out = _matmul(y, W["out"], "nn", F32, f"mm_out_{tag}", add=x)
    saved = dict(x=x, h=h, proj=proj, yb=yb, tot=tot, cqn=cqn, ckvn=ckvn, krr=krr, q_raw=q_raw, kv=kv,
                 q_rot=q_rot, k_full=k_full, v_c=v_c, yc=yc, lse=lse, y=y)
    return out, saved, (moved_stick[0] if moved_stick else []), (moved_mla[0] if moved_mla else [])


def _layer_bwd(cfg, l, dout, sv, W, p, cos2, sin2, rot, ext_stick, ext_mla, last):
    S, D, A, B, C = cfg.S, cfg.D, cfg.A, cfg.B, cfg.C
    tag = f"l{l}"
    proj = sv["proj"]
    dy = _matmul(dout, W["out"], "nt", BF16, f"mm_dy_{tag}")
    d_wout = _matmul(sv["y"], dout, "tn", BF16, f"mm_dwout_{tag}")
    wout_slots = _unperm_rows_out(cfg, d_wout).reshape(NDEV, cfg.DMIX // NDEV, D)
    (dyb, dzb), (dg_ob,), _ = _rowwise_vjp(_f_gate, [(sv["yb"], B, 0), cfg.view(proj, "zb")], [], [p["g_ob"]], [],
                                           [(dy, B, 0)], [BF16, BF16], 256, f"gate_b_bwd_{tag}")
    (dyc, dzc), (dg_oc,), _ = _rowwise_vjp(_f_gate, [(sv["yc"], C, 0), cfg.view(proj, "zc")], [], [p["g_oc"]], [],
                                           [(dy, C, B // C)], [BF16, BF16], 256, f"gate_c_bwd_{tag}")
    a_rows = [cfg.view(proj, "ua"), cfg.view(proj, "va"), cfg.view(proj, "za")]
    a_par = [p["g_v"], p["w_s"], p["b_s"], p["g_oa"]]
    (dua, dva, dza), (dg_v, dw_s, db_s, dg_oa), _ = _rowwise_vjp(
        _f_gmlp, a_rows, [], a_par, [], [(dy, A, (B + C) // A)], [BF16] * 3, LANE, f"gmlp_bwd_{tag}")
    qb, kb, vb = cfg.heads_view(proj, "qb"), cfg.heads_view(proj, "kb"), cfg.heads_view(proj, "vb")
    dqb, dkb, dvb, moved_stick = _stick_bwd(qb, kb, vb, (dyb, 0, LANE), sv["tot"], cfg.Hb, LANE ** -0.5,
                                            f"stick_bwd_{tag}", *cfg.tiles("stick_bwd", l),
                                            carry=_Exchange([[a] for a in ext_stick + [wout_slots]], False))
    got = dict(w_out=moved_stick[-1][0])
    ext_got = [mv[0] for mv in moved_stick[:-1]]
    qc, kc, vc = (sv["q_rot"], 0, 2 * LANE), (sv["k_full"], 0, 2 * LANE), (sv["v_c"], 0, LANE)
    dq_rot, dk_full, dv_c, *moved_mla = _softmax_bwd(qc, kc, vc, (sv["yc"], 0, LANE), (dyc, 0, LANE), sv["lse"], cfg.Hc,
                                                     (LANE + ROPE) ** -0.5, f"mla_bwd_{tag}", *cfg.tiles("mla_bwd", l),
                                                     carry=_Exchange([[a] for a in ext_mla], False) if ext_mla else None)
    ext_got += [mv[0] for mv in (moved_mla[0] if moved_mla else [])]
    trig = [(cos2, LANE, 0), (sin2, LANE, 0)]
    r_rows = [(sv["q_raw"], 2 * C, 0), (sv["kv"], 2 * C, 0), (sv["krr"], LANE, 0)]
    (dq_raw, dkv, dkrr), _, _ = _rowwise_vjp(_f_crope, r_rows, trig, [], [rot],
                                             [(dq_rot, 2 * C, 0), (dk_full, 2 * C, 0), (dv_c, C, 0)], [BF16] * 3, 128,
                                             f"crope_bwd_{tag}")
    dcqn = _matmul(dq_raw, W["uq"], "nn", BF16, f"mm_dcq_{tag}")
    d_wuq = _matmul(dq_raw, sv["cqn"], "tn", BF16, f"mm_dwuq_{tag}")
    dckvn = _matmul(dkv, W["ukv"], "nt", BF16, f"mm_dckv_{tag}")
    d_wukv = _matmul(sv["ckvn"], dkv, "tn", BF16, f"mm_dwukv_{tag}")
    c_rows = [cfg.view(proj, "cq"), cfg.view(proj, "ckv"), cfg.view(proj, "kr")]
    (dcq, dckv, dkr), (dg_q, dg_kv), _ = _rowwise_vjp(
        _f_cpre, c_rows, trig, [p["g_q"], p["g_kv"]], [rot],
        [(dcqn, cfg.Q, 0), (dckvn, cfg.KV, 0), (dkrr, LANE, 0)], [BF16] * 3, 256, f"cpre_bwd_{tag}")
    parts = dict(ua=dua, va=dva, za=dza, qb=dqb, kb=dkb, vb=dvb, zb=dzb, zc=dzc, cq=dcq, kr=dkr, ckv=dckv)
    cols, pos = [], 0
    for nm, off in sorted(cfg.off.items(), key=lambda kv_: kv_[1]):
        if off > pos:
            cols.append(jnp.zeros((S, off - pos), BF16))
        cols.append(parts[nm])
        pos = off + parts[nm].shape[1]
    if cfg.NP > pos:
        cols.append(jnp.zeros((S, cfg.NP - pos), BF16))
    dproj = jnp.concatenate(cols, axis=1)
    d_win = _matmul(dproj, sv["h"], "tn", BF16, f"mm_dwin_{tag}")
    to_send = dict(w_in=_unpad_w_in(cfg, d_win).reshape(NDEV, -1, D),
                   c_w_uq=d_wuq.reshape(cfg.Hc, 2 * LANE, cfg.Q)[:, :LANE + ROPE].reshape(NDEV, -1, cfg.Q),
                   c_w_ukv=_to_slots_cols(d_wukv))
    if last:
        dh, moved = _matmul(dproj, W["in"], "nn", BF16, f"mm_dh_{tag}",
                            carry=_Exchange([[to_send[nm]] for nm in ("w_in", "c_w_uq", "c_w_ukv")], False))
        got.update(w_in=moved[0][0], c_w_uq=moved[1][0], c_w_ukv=moved[2][0])
        to_send = {}
    else:
        dh = _matmul(dproj, W["in"], "nn", BF16, f"mm_dh_{tag}")
    (dx,), (dg_pre,), _ = _rowwise_vjp(_f_pre_res, [(sv["x"], D, 0)], [], [p["g_pre"]], [],
                                       [(dh, D, 0), (dout, D, 0)], [F32], 128, f"pre_bwd_{tag}")
    small = dict(g_pre=dg_pre[0], a_g_v=dg_v.reshape(cfg.G, LANE), a_w_s=dw_s, a_b_s=db_s[:, :, 0], c_g_q=dg_q[0],
                 c_g_kv=dg_kv[0], g_out=jnp.concatenate([dg_oa[0], dg_ob[0], dg_oc[0]]))
    return dx, small, got, to_send, ext_got


def _pack_small(vals):
    pieces = []
    for nm in SMALL:
        piece = vals[nm].reshape(-1, LANE)
        pieces.append(jnp.pad(piece, ((0, -piece.shape[0] % 8), (0, 0))))
    packed = jnp.concatenate(pieces, axis=0)
    return jnp.pad(packed, ((0, -packed.shape[0] % SMALL_ROWS), (0, 0)))


def _unpack_small(packed, like):
    out, row = {}, 0
    for nm in SMALL:
        n = like[nm].size // LANE
        out[nm] = packed[row:row + n].reshape(like[nm].shape)
        row += n + (-n % 8)
    return out


def kernel(x, positions, g_pre, w_in, a_g_v, a_w_s, a_b_s, c_g_q, c_g_kv, c_w_uq, c_w_ukv, g_out, w_out, g_final, loss_target, m_g_pre, m_w_in, m_a_g_v, m_a_w_s, m_a_b_s, m_c_g_q, m_c_g_kv, m_c_w_uq, m_c_w_ukv, m_g_out, m_w_out, m_g_final, v_g_pre, v_w_in, v_a_g_v, v_a_w_s, v_a_b_s, v_c_g_q, v_c_g_kv, v_c_w_uq, v_c_w_ukv, v_g_out, v_w_out, v_g_final):
    depth, S, D = w_in.shape[0], x.shape[1], x.shape[2]
    cfg = _Cfg(S, D, a_g_v.shape[1], c_g_q.shape[1], c_g_kv.shape[1], c_w_ukv.shape[2] * NDEV // (2 * LANE), g_out.shape[1])
    weights = dict(g_pre=g_pre, w_in=w_in, a_g_v=a_g_v, a_w_s=a_w_s, a_b_s=a_b_s, c_g_q=c_g_q, c_g_kv=c_g_kv,
                   c_w_uq=c_w_uq, c_w_ukv=c_w_ukv, g_out=g_out, w_out=w_out, g_final=g_final)
    mom_m = dict(g_pre=m_g_pre, w_in=m_w_in, a_g_v=m_a_g_v, a_w_s=m_a_w_s, a_b_s=m_a_b_s, c_g_q=m_c_g_q, c_g_kv=m_c_g_kv,
                 c_w_uq=m_c_w_uq, c_w_ukv=m_c_w_ukv, g_out=m_g_out, w_out=m_w_out, g_final=m_g_final)
    mom_v = dict(g_pre=v_g_pre, w_in=v_w_in, a_g_v=v_a_g_v, a_w_s=v_a_w_s, a_b_s=v_a_b_s, c_g_q=v_c_g_q, c_g_kv=v_c_g_kv,
                 c_w_uq=v_c_w_uq, c_w_ukv=v_c_w_ukv, g_out=v_g_out, w_out=v_w_out, g_final=v_g_final)
    big_names = ("w_in", "c_w_uq", "c_w_ukv", "w_out")

    inv_freq = 1.0 / (ROPE_THETA ** (jnp.arange(0, ROPE, 2, dtype=F32) / ROPE))
    ang = positions[0].astype(F32)[:, None] * inv_freq
    zpad = jnp.zeros((S, LANE - ROPE), F32)
    cos2 = jnp.concatenate([jnp.cos(ang), jnp.cos(ang), zpad], axis=1)
    sin2 = jnp.concatenate([jnp.sin(ang), jnp.sin(ang), zpad], axis=1)
    rot = _rope_matrix()

    for tree in (weights, mom_m, mom_v):
        for nm in TRANSPOSED:
            tree[nm] = jnp.swapaxes(tree[nm], 1, 2)

    def shards(l, names):
        return [[weights[nm][l].astype(BF16)] for nm in names]

    def assemble(g_in, g_uq, g_ukv, g_wout):
        uq = jnp.pad(g_uq.reshape(cfg.Hc, LANE + ROPE, cfg.Q), ((0, 0), (0, LANE - ROPE), (0, 0))).reshape(2 * cfg.C, cfg.Q)
        return {"in": _pad_w_in(cfg, g_in.reshape(-1, D)), "uq": uq, "ukv": _from_slots_cols(g_ukv),
                "out": _perm_rows_out(cfg, g_wout.reshape(cfg.DMIX, D))}

    params = [_layer_params(cfg, l, g_pre, a_g_v, a_w_s, a_b_s, c_g_q, c_g_kv, g_out) for l in range(depth)]

    gathered = [g[0] for g in _exchange(shards(0, big_names), True, "gather_weights_l0")]
    hcur, saved, Ws = x[0], [], []
    for l in range(depth):
        Ws.append(assemble(*gathered))
        nxt = l + 1 < depth
        hcur, sv, got_in, got_rest = _layer_fwd(
            cfg, l, hcur, Ws[l], params[l], cos2, sin2, rot,
            carry_stick=_Exchange(shards(l + 1, big_names[:1]), True) if nxt else None,
            carry_mla=_Exchange(shards(l + 1, big_names[1:]), True) if nxt else None)
        saved.append(sv)
        gathered = [g[0] for g in got_in + got_rest]
    (dh,), (dg_final,), (loss_rows,) = _rowwise_vjp(
        _f_final, [(hcur, D, 0)], [(loss_target[0], D, 0)], [g_final[None]], [], [(jnp.ones((S, 1), F32), 1, 0)],
        [F32], 128, "final", primal=[(1, F32)])
    loss = lax.psum(jnp.sum(loss_rows), MESH_AXES)

    small_g, slots, pending = [None] * depth, [None] * depth, {}
    for l in reversed(range(depth)):
        ext_stick = [pending["w_in"]] if pending else []
        ext_mla = [pending["c_w_uq"], pending["c_w_ukv"]] if pending else []
        dh, small_g[l], slots[l], pending, ext_got = _layer_bwd(cfg, l, dh, saved[l], Ws[l], params[l], cos2, sin2, rot,
                                                                ext_stick, ext_mla, l == 0)
        if ext_got:
            slots[l + 1].update(w_in=ext_got[0], c_w_uq=ext_got[1], c_w_ukv=ext_got[2])
    grad_x = dh[None]
    small_grads = {nm: jnp.stack([small_g[l][nm] for l in range(depth)]) for nm in SMALL if nm != "g_final"}
    small_grads["g_final"] = dg_final[0]
    (small_slots,) = _exchange([[_pack_small(small_grads)]], True, "gather_small_grads")

    res = {}
    for nm in big_names:
        res[nm] = None
        for l in range(depth):
            res[nm] = _adamw(slots[l][nm], weights[nm], mom_m[nm], mom_v[nm], l, res[nm], f"adamw_{nm}_l{l}")
        if nm in TRANSPOSED:
            res[nm] = [jnp.swapaxes(r, 1, 2) for r in res[nm]]
    packed = _adamw(small_slots[0], _pack_small(weights)[None], _pack_small(mom_m)[None], _pack_small(mom_v)[None], 0, None,
                    "adamw_small")
    small_res = [_unpack_small(r[0], weights) for r in packed]
    order = ("g_pre", "w_in", "a_g_v", "a_w_s", "a_b_s", "c_g_q", "c_g_kv", "c_w_uq", "c_w_ukv", "g_out", "w_out", "g_final")
    outs = [loss, grad_x]
    for kind in range(4):
        outs += [small_res[kind][nm] if nm in SMALL else res[nm][kind] for nm in order]
    return tuple(outs)
```

```python
import functools

import numpy as np
import jax
import jax.numpy as jnp
from jax import lax
from jax.experimental import pallas as pl
from jax.experimental.pallas import tpu as pltpu

NDEV = 8
MESH_AXES = ("x", "y", "c")
LANE = 128
ROPE = 64
EPS = 1e-6
ROPE_THETA = 10000.0
ADAM_LR, ADAM_B1, ADAM_B2, ADAM_EPS, ADAM_WD, ADAM_STEP = 0.001, 0.9, 0.999, 1e-08, 0.01, 10
VMEM_LIMIT = 48 * 1024 * 1024
ADAM_TILE_BYTES = 768 * 1024
CARRY_MID_PERCENT = 80
SMALL_ROWS = 256
ATTN_TILES = {"stick_fwd": [(1024, 256)], "stick_bwd": [(1024, 256)], "mla_fwd": [(512, 1024)], "mla_bwd": [(512, 1024)]}
F32, BF16 = jnp.float32, jnp.bfloat16
SMALL = ("g_pre", "a_g_v", "a_w_s", "a_b_s", "c_g_q", "c_g_kv", "g_out", "g_final")
TRANSPOSED = ("w_in", "c_w_uq")


def _tile(dim, cap, mult=LANE):
    if dim <= cap:
        return dim
    t = (cap // mult) * mult
    while t >= mult:
        if dim % t == 0:
            return t
        t -= mult
    return dim


def _dot_nt(a, b):
    return lax.dot_general(a, b, (((1,), (1,)), ((), ())), preferred_element_type=F32)


def _dot_tn(a, b):
    return lax.dot_general(a, b, (((0,), (0,)), ((), ())), preferred_element_type=F32)


def _dot(a, b):
    return jnp.dot(a, b, preferred_element_type=F32)


class _Exchange:
    def __init__(self, groups, gather):
        self.groups, self.gather = groups, gather
        self.flat = [(gi, li, a) for gi, grp in enumerate(groups) for li, a in enumerate(grp)]
        self.n = len(self.flat)
        self.args = [a for (_, _, a) in self.flat]
        self.out_shape = [jax.ShapeDtypeStruct((len(grp), NDEV) + tuple(grp[0].shape[-2:]), grp[0].dtype) for grp in groups]
        self.scratch = [pltpu.SemaphoreType.DMA((self.n, NDEV - 1)), pltpu.SemaphoreType.DMA((self.n, NDEV - 1)),
                        pltpu.SemaphoreType.DMA((self.n,))]

    def _copies(self, ins, outs, send_sems, recv_sems, local_sems, landings):
        x, y, c = lax.axis_index("x"), lax.axis_index("y"), lax.axis_index("c")
        me = 4 * x + 2 * y + c
        owns = [pltpu.make_async_copy(ins[i] if self.gather else ins[i].at[me], outs[gi].at[li, me], local_sems.at[i])
                for i, (gi, li, _) in enumerate(self.flat)]
        pairs = []
        for k in range(1, NDEV):
            px = 1 - x if k & 4 else x
            py = 1 - y if k & 2 else y
            pc = 1 - c if k & 1 else c
            peer = 4 * px + 2 * py + pc
            for i, (gi, li, _) in enumerate(self.flat):
                src = ins[i] if self.gather else ins[i].at[peer]
                sems = dict(send_sem=send_sems.at[i, k - 1], recv_sem=recv_sems.at[i, k - 1],
                            device_id=(px, py, pc), device_id_type=pl.DeviceIdType.MESH)
                out = pltpu.make_async_remote_copy(src_ref=src, dst_ref=outs[gi].at[li, me], **sems)
                landing = pltpu.make_async_remote_copy(src_ref=src, dst_ref=outs[gi].at[li, peer], **sems) if landings else None
                pairs.append((out, landing))
        return owns, pairs

    def start(self, ins, outs, sems):
        owns, pairs = self._copies(ins, outs, *sems, landings=False)
        for own in owns:
            own.start()
        for out, _ in pairs:
            out.start()

    def mid(self, ins, outs, sems):
        pass

    def wait(self, ins, outs, sems):
        owns, pairs = self._copies(ins, outs, *sems, landings=True)
        for out, landing in pairs:
            out.wait_send()
            landing.wait_recv()
        for own in owns:
            own.wait()


class _GatherTwoLevel(_Exchange):
    def __init__(self, groups):
        super().__init__(groups, True)

    def _copy(self, i, k, ins, outs, send_sems, recv_sems, landing):
        gi, li, _ = self.flat[i]
        x, y, c = lax.axis_index("x"), lax.axis_index("y"), lax.axis_index("c")
        chips = [(x, y), (1 - x, y), (x, 1 - y), (1 - x, 1 - y)]

        def slot(chip, core):
            return outs[gi].at[li, 4 * chip[0] + 2 * chip[1] + core]

        if k == 0:
            to, src, dst, lands = (x, y, 1 - c), ins[i], slot(chips[0], c), slot(chips[0], 1 - c)
        elif k <= 3:
            to, src, dst, lands = (*chips[k], c), ins[i], slot(chips[0], c), slot(chips[k], c)
        else:
            to, src, dst, lands = (x, y, 1 - c), slot(chips[k - 3], c), slot(chips[k - 3], c), slot(chips[k - 3], 1 - c)
        return pltpu.make_async_remote_copy(src_ref=src, dst_ref=lands if landing else dst, send_sem=send_sems.at[i, k],
                                            recv_sem=recv_sems.at[i, k], device_id=to, device_id_type=pl.DeviceIdType.MESH)

    def _own(self, i, ins, outs, local_sems):
        gi, li, _ = self.flat[i]
        me = 4 * lax.axis_index("x") + 2 * lax.axis_index("y") + lax.axis_index("c")
        return pltpu.make_async_copy(ins[i], outs[gi].at[li, me], local_sems.at[i])

    def start(self, ins, outs, sems):
        send_sems, recv_sems, local_sems = sems
        for i in range(self.n):
            self._own(i, ins, outs, local_sems).start()
        for k in range(4):
            for i in range(self.n):
                self._copy(i, k, ins, outs, send_sems, recv_sems, False).start()

    def mid(self, ins, outs, sems):
        send_sems, recv_sems, _ = sems
        for k in range(1, 4):
            for i in range(self.n):
                self._copy(i, k, ins, outs, send_sems, recv_sems, True).wait_recv()
                self._copy(i, k + 3, ins, outs, send_sems, recv_sems, False).start()

    def wait(self, ins, outs, sems):
        send_sems, recv_sems, local_sems = sems
        for k in (0, 4, 5, 6):
            for i in range(self.n):
                self._copy(i, k, ins, outs, send_sems, recv_sems, True).wait_recv()
        for k in range(NDEV - 1):
            for i in range(self.n):
                self._copy(i, k, ins, outs, send_sems, recv_sems, False).wait_send()
        for i in range(self.n):
            self._own(i, ins, outs, local_sems).wait()


def _call(body, name, grid, in_specs, out_specs, out_shape, scratch, semantics, args, carry=None):
    n_in, n_out, n_scr = len(in_specs), len(out_specs), len(scratch)
    if carry is None:
        run = body
    else:
        semantics = ("arbitrary",) * len(grid)
        anyspec = pl.BlockSpec(memory_space=pl.ANY)
        in_specs = list(in_specs) + [anyspec] * carry.n
        out_specs = list(out_specs) + [anyspec] * len(carry.groups)
        out_shape = list(out_shape) + carry.out_shape
        scratch = list(scratch) + carry.scratch
        args = list(args) + carry.args

        def run(*refs):
            c_in, x_in = refs[:n_in], refs[n_in:n_in + carry.n]
            rest = refs[n_in + carry.n:]
            c_out, x_out = rest[:n_out], rest[n_out:n_out + len(carry.groups)]
            c_scr, sems = rest[n_out + len(carry.groups):len(rest) - 3], rest[len(rest) - 3:]
            step, total = 0, 1
            for d, extent in enumerate(grid):
                step = step * extent + pl.program_id(d)
                total *= extent

            @pl.when(step == 0)
            def _():
                carry.start(x_in, x_out, sems)

            body(*c_in, *c_out, *c_scr)

            @pl.when(step == (total * CARRY_MID_PERCENT) // 100)
            def _():
                carry.mid(x_in, x_out, sems)

            @pl.when(step == total - 1)
            def _():
                carry.wait(x_in, x_out, sems)

    res = pl.pallas_call(
        run, name=name, grid=grid, out_shape=list(out_shape), in_specs=list(in_specs), out_specs=list(out_specs),
        scratch_shapes=list(scratch),
        compiler_params=pltpu.CompilerParams(dimension_semantics=semantics, vmem_limit_bytes=VMEM_LIMIT,
                                             has_side_effects=carry is not None),
    )(*args)
    return list(res[:n_out]), list(res[n_out:])


def _exchange(ex, name):
    groups = ex.groups

    def body(*refs):
        ins, outs, sems = refs[:ex.n], refs[ex.n:ex.n + len(groups)], refs[ex.n + len(groups):]
        ex.start(ins, outs, sems)
        ex.mid(ins, outs, sems)
        ex.wait(ins, outs, sems)

    anyspec = pl.BlockSpec(memory_space=pl.ANY)
    return pl.pallas_call(
        body, name=name, out_shape=ex.out_shape, in_specs=[anyspec] * ex.n, out_specs=[anyspec] * len(groups),
        scratch_shapes=ex.scratch, compiler_params=pltpu.CompilerParams(has_side_effects=True),
    )(*ex.args)


def _matmul(a, b, mode, out_dtype, name, add=None, tm=1024, tn=1024, tk=1024, carry=None):
    if mode == "tn":
        (K, M), (K2, N) = a.shape, b.shape
    elif mode == "nt":
        (M, K), (N, K2) = a.shape, b.shape
    else:
        (M, K), (K2, N) = a.shape, b.shape
    assert K == K2, (a.shape, b.shape, mode)
    tm, tn, tk = _tile(M, tm), _tile(N, tn), _tile(K, tk)
    nk = K // tk
    a_spec = pl.BlockSpec((tk, tm), lambda i, j, k: (k, i)) if mode == "tn" else pl.BlockSpec((tm, tk), lambda i, j, k: (i, k))
    b_spec = pl.BlockSpec((tn, tk), lambda i, j, k: (j, k)) if mode == "nt" else pl.BlockSpec((tk, tn), lambda i, j, k: (k, j))
    dot = {"nn": _dot, "nt": _dot_nt, "tn": _dot_tn}[mode]
    has_add = add is not None

    def body(*refs):
        a_ref, b_ref = refs[0], refs[1]
        o_ref, acc = refs[-2], refs[-1]
        k = pl.program_id(2)

        @pl.when(k == 0)
        def _():
            acc[...] = jnp.zeros_like(acc)

        acc[...] += dot(a_ref[...].astype(BF16), b_ref[...].astype(BF16))

        @pl.when(k == nk - 1)
        def _():
            r = acc[...]
            if has_add:
                r = r + refs[2][...]
            o_ref[...] = r.astype(o_ref.dtype)

    in_specs = [a_spec, b_spec]
    args = [a, b]
    if has_add:
        in_specs.append(pl.BlockSpec((tm, tn), lambda i, j, k: (i, j)))
        args.append(add)
    (out,), moved = _call(body, name, (M // tm, N // tn, nk), in_specs, [pl.BlockSpec((tm, tn), lambda i, j, k: (i, j))],
                          [jax.ShapeDtypeStruct((M, N), out_dtype)], [pltpu.VMEM((tm, tn), F32)],
                          ("parallel", "parallel", "arbitrary"), args, carry)
    return out if carry is None else (out, moved)


def _row_specs(views, tile):
    return [pl.BlockSpec((tile, w), functools.partial(lambda i, cb: (i, cb), cb=cb)) for (_, w, cb) in views]


def _full_specs(arrs):
    return [pl.BlockSpec(p.shape, functools.partial(lambda i, nd: (0,) * nd, nd=p.ndim)) for p in arrs]


def _rowwise(fn, rows, aux, params, consts, outs, tile, name):
    S = rows[0][0].shape[0]
    nr, na, npar, nc = len(rows), len(aux), len(params), len(consts)

    def body(*refs):
        ins = [r[...].astype(F32) for r in refs[:nr + na]]
        small = [r[...] for r in refs[nr + na:nr + na + npar + nc]]
        res = fn(*ins, *small)
        for o_ref, r in zip(refs[nr + na + npar + nc:], res):
            o_ref[...] = r.astype(o_ref.dtype)

    return pl.pallas_call(
        body, name=name, grid=(S // tile,),
        out_shape=[jax.ShapeDtypeStruct((S, w), dt) for (w, dt) in outs],
        in_specs=_row_specs(rows + aux, tile) + _full_specs(params + consts),
        out_specs=[pl.BlockSpec((tile, w), lambda i: (i, 0)) for (w, _) in outs],
        compiler_params=pltpu.CompilerParams(dimension_semantics=("parallel",), vmem_limit_bytes=VMEM_LIMIT),
    )(*[v[0] for v in rows + aux], *params, *consts)


def _rowwise_vjp(fn, rows, aux, params, consts, cots, grad_dtypes, tile, name, primal=()):
    S = rows[0][0].shape[0]
    nr, na, npar, nc, nct, npr = len(rows), len(aux), len(params), len(consts), len(cots), len(primal)

    def body(*refs):
        n_in = nr + na + npar + nc + nct
        rv = [r[...].astype(F32) for r in refs[:nr]]
        av = [r[...].astype(F32) for r in refs[nr:nr + na]]
        pv = [r[...] for r in refs[nr + na:nr + na + npar]]
        cv = [r[...] for r in refs[nr + na + npar:nr + na + npar + nc]]
        ct = tuple(r[...].astype(F32) for r in refs[nr + na + npar + nc:n_in])
        res, vjp = jax.vjp(lambda *rp: tuple(fn(*rp[:nr], *av, *rp[nr:], *cv)), *rv, *pv)
        grads = vjp(ct)
        g_refs = refs[n_in:n_in + nr]
        p_refs = refs[n_in + nr:n_in + nr + npar]
        o_refs = refs[n_in + nr + npar:]
        for g_ref, g in zip(g_refs, grads[:nr]):
            g_ref[...] = g.astype(g_ref.dtype)

        @pl.when(pl.program_id(0) == 0)
        def _():
            for p_ref in p_refs:
                p_ref[...] = jnp.zeros_like(p_ref)

        for p_ref, g in zip(p_refs, grads[nr:]):
            p_ref[...] += g
        for o_ref, r in zip(o_refs, res[:npr]):
            o_ref[...] = r.astype(o_ref.dtype)

    out_shape = ([jax.ShapeDtypeStruct((S, w), dt) for (_, w, _), dt in zip(rows, grad_dtypes)]
                 + [jax.ShapeDtypeStruct(p.shape, F32) for p in params]
                 + [jax.ShapeDtypeStruct((S, w), dt) for (w, dt) in primal])
    out_specs = ([pl.BlockSpec((tile, w), lambda i: (i, 0)) for (_, w, _) in rows] + _full_specs(params)
                 + [pl.BlockSpec((tile, w), lambda i: (i, 0)) for (w, _) in primal])
    res = pl.pallas_call(
        body, name=name, grid=(S // tile,), out_shape=out_shape,
        in_specs=_row_specs(rows + aux, tile) + _full_specs(params + consts) + _row_specs(cots, tile),
        out_specs=out_specs,
        compiler_params=pltpu.CompilerParams(dimension_semantics=("arbitrary",), vmem_limit_bytes=VMEM_LIMIT),
    )(*[v[0] for v in rows + aux], *params, *consts, *[v[0] for v in cots])
    return res[:nr], res[nr:nr + npar], res[nr + npar:]


@jax.custom_vjp
def _mm(a, b):
    return _dot(a.astype(BF16), b.astype(BF16))


def _mm_fwd(a, b):
    return _mm(a, b), (a, b)


def _mm_bwd(res, ct):
    a, b = res
    ctb = ct.astype(BF16)
    return _dot_nt(ctb, b.astype(BF16)), _dot_tn(a.astype(BF16), ctb)


_mm.defvjp(_mm_fwd, _mm_bwd)


def _rms(x, g):
    return x * lax.rsqrt(jnp.mean(x * x, axis=-1, keepdims=True) + EPS) * g


def _f_pre(x, g):
    return (_rms(x, g),)


def _f_pre_res(x, g):
    return _rms(x, g), x


def _f_gate(y, z, g):
    return (_rms(y, g) * jax.nn.silu(z),)


def _f_gmlp(u, v, z, g_v, w_s, b_s, g_o):
    groups = w_s.shape[0]
    u, v = jax.nn.gelu(u), jax.nn.gelu(v)
    t_idx = lax.broadcasted_iota(jnp.int32, (LANE, LANE), 0)
    s_idx = lax.broadcasted_iota(jnp.int32, (LANE, LANE), 1)
    ys = []
    for g in range(groups):
        sl = slice(g * LANE, (g + 1) * LANE)
        vn = _rms(v[:, sl], g_v[:, sl])
        w = jnp.where(s_idx <= t_idx, w_s[g], 0.0)
        ys.append(u[:, sl] * (_mm(w, vn) + b_s[g]))
    return (_rms(jnp.concatenate(ys, axis=1), g_o) * jax.nn.silu(z),)


def _rope(x, cos2, sin2, rot):
    return x * cos2 + _mm(x, rot) * sin2


def _f_cpre(cq, ckv, kr, cos2, sin2, g_q, g_kv, rot):
    return _rms(cq, g_q), _rms(ckv, g_kv), _rope(kr, cos2, sin2, rot)


def _f_crope(q, kv, krr, cos2, sin2, rot):
    heads = q.shape[1] // (2 * LANE)
    qs, ks, vs = [], [], []
    for h in range(heads):
        lo, mid, hi = 2 * h * LANE, (2 * h + 1) * LANE, (2 * h + 2) * LANE
        qs += [q[:, lo:mid], _rope(q[:, mid:hi], cos2, sin2, rot)]
        ks += [kv[:, lo:mid], krr]
        vs += [kv[:, mid:hi]]
    return jnp.concatenate(qs, axis=1), jnp.concatenate(ks, axis=1), jnp.concatenate(vs, axis=1)


def _f_final(h, target, g):
    err = _rms(h, g) - target
    return (0.5 * jnp.mean(err * err, axis=-1, keepdims=True),)


def _rope_matrix():
    r = np.zeros((LANE, LANE), np.float32)
    half = ROPE // 2
    for i in range(half):
        r[i + half, i] = -1.0
        r[i, i + half] = 1.0
    return jnp.asarray(r)


def _head_spec(view, rows, n_rows_block):
    _, cb0, w = view
    if n_rows_block:
        return pl.BlockSpec((rows, w), functools.partial(lambda h, i, cb0: (i, cb0 + h), cb0=cb0))
    return pl.BlockSpec((rows, w), functools.partial(lambda h, i, cb0: (0, cb0 + h), cb0=cb0))


def _stat_spec(tq):
    return pl.BlockSpec((1, tq, 1), lambda h, i: (h, i, 0))


def _softplus(z):
    return jnp.maximum(z, 0.0) + jnp.log(1.0 + jnp.exp(-jnp.abs(z)))


def _cumsum_mm(x, m01):
    hi = x.astype(BF16)
    lo = (x - hi.astype(F32)).astype(BF16)
    return _dot(hi, m01) + _dot(lo, m01)


def _attn_call(body, name, heads, S, tq, ins, in_blocked, outs, out_blocked, scratch, stats_in=0, stats_out=0, carry=None):
    in_specs = [_head_spec(v, tq if blk else S, blk) for v, blk in zip(ins[:len(ins) - stats_in], in_blocked)]
    in_specs += [_stat_spec(tq)] * stats_in
    out_specs = [_head_spec((None, 0, w), tq if blk else S, blk) for (w, _), blk in zip(outs, out_blocked)]
    out_specs += [_stat_spec(tq)] * stats_out
    out_shape = [jax.ShapeDtypeStruct((S, heads * w), dt) for (w, dt) in outs]
    out_shape += [jax.ShapeDtypeStruct((heads, S, 1), F32)] * stats_out
    args = [v[0] for v in ins[:len(ins) - stats_in]] + list(ins[len(ins) - stats_in:])
    res, moved = _call(body, name, (heads, S // tq), in_specs, out_specs, out_shape, scratch, ("arbitrary", "arbitrary"),
                       args, carry)
    return res if carry is None else res + [moved]


def _softmax_fwd(q, k, v, heads, scale, name, tq, bk, carry=None):
    S, dv = q[0].shape[0], v[2]

    def body(q_ref, k_ref, v_ref, o_ref, lse_ref):
        qi = pl.program_id(1)
        qv = q_ref[...]
        row = qi * tq + lax.broadcasted_iota(jnp.int32, (tq, bk), 0)
        col0 = lax.broadcasted_iota(jnp.int32, (tq, bk), 1)

        def step(kb, carry):
            m, l, acc = carry
            sl = pl.ds(pl.multiple_of(kb * bk, bk), bk)
            s = _dot_nt(qv, k_ref[sl, :]) * scale
            s = jnp.where(kb * bk + col0 <= row, s, -1e30)
            m_new = jnp.maximum(m, jnp.max(s, axis=1, keepdims=True))
            p = jnp.exp(s - m_new)
            alpha = jnp.exp(m - m_new)
            l = alpha * l + jnp.sum(p, axis=1, keepdims=True)
            acc = alpha * acc + _dot(p.astype(BF16), v_ref[sl, :])
            return m_new, l, acc

        n_kb = (qi * tq + tq + bk - 1) // bk
        m, l, acc = lax.fori_loop(0, n_kb, step, (jnp.full((tq, 1), -1e30, F32), jnp.zeros((tq, 1), F32),
                                                  jnp.zeros((tq, dv), F32)))
        o_ref[...] = (acc / l).astype(o_ref.dtype)
        lse_ref[0] = m + jnp.log(l)

    return _attn_call(body, name, heads, S, tq, [q, k, v], [1, 0, 0], [(dv, BF16)], [1], [], stats_out=1, carry=carry)


def _softmax_bwd(q, k, v, o, do, lse, heads, scale, name, tq, bk, carry=None):
    S, dq_w, dv = q[0].shape[0], q[2], v[2]
    nq = S // tq

    def body(q_ref, k_ref, v_ref, o_ref, do_ref, lse_ref, dq_ref, dk_ref, dv_ref, dk_acc, dv_acc):
        qi = pl.program_id(1)

        @pl.when(qi == 0)
        def _():
            dk_acc[...] = jnp.zeros_like(dk_acc)
            dv_acc[...] = jnp.zeros_like(dv_acc)

        qv, dov = q_ref[...], do_ref[...]
        delta = jnp.sum(dov.astype(F32) * o_ref[...].astype(F32), axis=1, keepdims=True)
        lse_v = lse_ref[0]
        row = qi * tq + lax.broadcasted_iota(jnp.int32, (tq, bk), 0)
        col0 = lax.broadcasted_iota(jnp.int32, (tq, bk), 1)

        def step(kb, dq):
            sl = pl.ds(pl.multiple_of(kb * bk, bk), bk)
            ks, vs = k_ref[sl, :], v_ref[sl, :]
            s = _dot_nt(qv, ks) * scale
            p = jnp.where(kb * bk + col0 <= row, jnp.exp(s - lse_v), 0.0)
            ds = (p * (_dot_nt(dov, vs) - delta) * scale).astype(BF16)
            dk_acc[sl, :] += _dot_tn(ds, qv)
            dv_acc[sl, :] += _dot_tn(p.astype(BF16), dov)
            return dq + _dot(ds, ks)

        n_kb = (qi * tq + tq + bk - 1) // bk
        dq_ref[...] = lax.fori_loop(0, n_kb, step, jnp.zeros((tq, dq_w), F32)).astype(dq_ref.dtype)

        @pl.when(qi == nq - 1)
        def _():
            dk_ref[...] = dk_acc[...].astype(dk_ref.dtype)
            dv_ref[...] = dv_acc[...].astype(dv_ref.dtype)

    return _attn_call(body, name, heads, S, tq, [q, k, v, o, do, lse], [1, 0, 0, 1, 1],
                      [(dq_w, BF16), (dq_w, BF16), (dv, BF16)], [1, 0, 0],
                      [pltpu.VMEM((S, dq_w), F32), pltpu.VMEM((S, dv), F32)], stats_in=1, carry=carry)


def _stick_fwd(q, k, v, heads, scale, name, tq, bk, carry=None):
    S, dv = q[0].shape[0], v[2]

    def body(q_ref, k_ref, v_ref, o_ref, tot_ref):
        qi = pl.program_id(1)
        qv = q_ref[...]
        row = qi * tq + lax.broadcasted_iota(jnp.int32, (tq, bk), 0)
        col0 = lax.broadcasted_iota(jnp.int32, (tq, bk), 1)
        m_gt = (lax.broadcasted_iota(jnp.int32, (bk, bk), 0) > lax.broadcasted_iota(jnp.int32, (bk, bk), 1)).astype(BF16)
        n_kb = (qi * tq + tq + bk - 1) // bk

        def step(it, carry):
            c, acc = carry
            kb = n_kb - 1 - it
            sl = pl.ds(pl.multiple_of(kb * bk, bk), bk)
            z = _dot_nt(qv, k_ref[sl, :]) * scale
            mask = kb * bk + col0 < row
            sp = _softplus(z)
            lk = jnp.where(mask, -sp, 0.0)
            after = _cumsum_mm(lk, m_gt) + c
            a = jnp.where(mask, jnp.exp(z - sp + after), 0.0)
            acc = acc + _dot(a.astype(BF16), v_ref[sl, :])
            return c + jnp.sum(lk, axis=1, keepdims=True), acc

        c, acc = lax.fori_loop(0, n_kb, step, (jnp.zeros((tq, 1), F32), jnp.zeros((tq, dv), F32)))
        o_ref[...] = acc.astype(o_ref.dtype)
        tot_ref[0] = c

    return _attn_call(body, name, heads, S, tq, [q, k, v], [1, 0, 0], [(dv, BF16)], [1], [], stats_out=1, carry=carry)


def _stick_bwd(q, k, v, do, tot, heads, scale, name, tq, bk, carry=None):
    S, dq_w, dv = q[0].shape[0], q[2], v[2]
    nq = S // tq

    def body(q_ref, k_ref, v_ref, do_ref, tot_ref, dq_ref, dk_ref, dv_ref, dk_acc, dv_acc):
        qi = pl.program_id(1)

        @pl.when(qi == 0)
        def _():
            dk_acc[...] = jnp.zeros_like(dk_acc)
            dv_acc[...] = jnp.zeros_like(dv_acc)

        qv, dov = q_ref[...], do_ref[...]
        tot_v = tot_ref[0]
        row = qi * tq + lax.broadcasted_iota(jnp.int32, (tq, bk), 0)
        col0 = lax.broadcasted_iota(jnp.int32, (tq, bk), 1)
        j_idx = lax.broadcasted_iota(jnp.int32, (bk, bk), 0)
        s_idx = lax.broadcasted_iota(jnp.int32, (bk, bk), 1)
        m_le, m_lt = (j_idx <= s_idx).astype(BF16), (j_idx < s_idx).astype(BF16)

        def step(kb, carry):
            pc, gc, dq = carry
            sl = pl.ds(pl.multiple_of(kb * bk, bk), bk)
            ks, vs = k_ref[sl, :], v_ref[sl, :]
            z = _dot_nt(qv, ks) * scale
            mask = kb * bk + col0 < row
            sp = _softplus(z)
            lk = jnp.where(mask, -sp, 0.0)
            after = tot_v - pc - _cumsum_mm(lk, m_le)
            log_beta = z - sp
            a = jnp.where(mask, jnp.exp(log_beta + after), 0.0)
            g = _dot_nt(dov, vs) * a
            cg = gc + _cumsum_mm(g, m_lt)
            dz = (jnp.where(mask, g * jnp.exp(-sp) - jnp.exp(log_beta) * cg, 0.0) * scale).astype(BF16)
            dk_acc[sl, :] += _dot_tn(dz, qv)
            dv_acc[sl, :] += _dot_tn(a.astype(BF16), dov)
            return (pc + jnp.sum(lk, axis=1, keepdims=True), gc + jnp.sum(g, axis=1, keepdims=True), dq + _dot(dz, ks))

        n_kb = (qi * tq + tq + bk - 1) // bk
        zero = jnp.zeros((tq, 1), F32)
        _, _, dq = lax.fori_loop(0, n_kb, step, (zero, zero, jnp.zeros((tq, dq_w), F32)))
        dq_ref[...] = dq.astype(dq_ref.dtype)

        @pl.when(qi == nq - 1)
        def _():
            dk_ref[...] = dk_acc[...].astype(dk_ref.dtype)
            dv_ref[...] = dv_acc[...].astype(dv_ref.dtype)

    return _attn_call(body, name, heads, S, tq, [q, k, v, do, tot], [1, 0, 0, 1],
                      [(dq_w, BF16), (dq_w, BF16), (dv, BF16)], [1, 0, 0],
                      [pltpu.VMEM((S, dq_w), F32), pltpu.VMEM((S, dv), F32)], stats_in=1, carry=carry)


def _adamw(slots, w, m, v, layer, prev, name):
    _, R, C = slots.shape
    L = w.shape[0]
    item = slots.dtype.itemsize
    tc = _tile(C, 2048)
    tr = _tile(R, max(16, ADAM_TILE_BYTES // (item * tc)), mult=16)
    if tr == R and R * tc * item > ADAM_TILE_BYTES:
        tc = _tile(C, max(LANE, ADAM_TILE_BYTES // (item * R)))
    c1, c2 = 1.0 - ADAM_B1 ** ADAM_STEP, 1.0 - ADAM_B2 ** ADAM_STEP
    n_prev = 0 if prev is None else 4

    def body(s_ref, w_ref, m_ref, v_ref, *rest):
        g_out, d_out, m_out, v_out = rest[n_prev:]
        g = s_ref[0].astype(F32)
        for k in range(1, NDEV):
            g = g + s_ref[k].astype(F32)
        m_new = ADAM_B1 * m_ref[0] + (1.0 - ADAM_B1) * g
        v_new = ADAM_B2 * v_ref[0] + (1.0 - ADAM_B2) * (g * g)
        g_out[0] = g
        m_out[0] = m_new
        v_out[0] = v_new
        d_out[0] = -ADAM_LR * ((m_new / c1) / (jnp.sqrt(v_new / c2) + ADAM_EPS) + ADAM_WD * w_ref[0])

    spec = pl.BlockSpec((1, tr, tc), lambda i, j: (layer, i, j))
    in_specs = [pl.BlockSpec((NDEV, tr, tc), lambda i, j: (0, i, j)), spec, spec, spec]
    in_specs += [pl.BlockSpec(memory_space=pl.ANY)] * n_prev
    return pl.pallas_call(
        body, name=name, grid=(R // tr, C // tc), out_shape=[jax.ShapeDtypeStruct((L, R, C), F32)] * 4,
        in_specs=in_specs, out_specs=[spec] * 4, input_output_aliases={4 + i: i for i in range(n_prev)},
        compiler_params=pltpu.CompilerParams(dimension_semantics=("parallel", "parallel"), vmem_limit_bytes=VMEM_LIMIT),
    )(slots, w, m, v, *(prev or []))


class _Cfg:
    def __init__(self, S, D, groups, q_lora, kv_lora, c_heads, d_mix):
        self.S, self.D, self.G, self.Q, self.KV, self.Hc, self.DMIX = S, D, groups, q_lora, kv_lora, c_heads, d_mix
        self.A, self.C = groups * LANE, c_heads * LANE
        self.B = d_mix - self.A - self.C
        self.Hb = self.B // LANE
        A, B, C = self.A, self.B, self.C
        assert B % LANE == 0 and B % C == 0 and (B + C) % A == 0
        self.ref_segs = [("ua", A), ("va", A), ("za", A), ("qb", B), ("kb", B), ("vb", B), ("zb", B),
                         ("cq", q_lora), ("ckv", kv_lora), ("kr", ROPE), ("zc", C)]
        self.off, off = {}, 0
        for nm, w in [("ua", A), ("va", A), ("za", A), ("qb", B), ("kb", B), ("vb", B), ("zb", B), ("zc", C),
                      ("cq", q_lora), ("kr", LANE), ("ckv", kv_lora)]:
            off = -(-off // w) * w
            self.off[nm] = off
            off += w
        self.NP = -(-off // 512) * 512
        self.width = {"kr": LANE, **{nm: w for nm, w in self.ref_segs if nm != "kr"}}

    def tiles(self, kind, layer):
        tq, bk = ATTN_TILES[kind][layer % len(ATTN_TILES[kind])]
        return min(tq, self.S), min(bk, self.S)

    def view(self, arr, nm):
        w = self.width[nm]
        return (arr, w, self.off[nm] // w)

    def heads_view(self, arr, nm):
        return (arr, self.off[nm] // LANE, LANE)


def _pad_w_in(cfg, wt):
    pieces, start = {}, 0
    for nm, width in cfg.ref_segs:
        pieces[nm] = wt[start:start + width]
        start += width
    rows, pos = [], 0
    for nm, off in sorted(cfg.off.items(), key=lambda kv: kv[1]):
        if off > pos:
            rows.append(jnp.zeros((off - pos, wt.shape[1]), wt.dtype))
        rows.append(pieces[nm])
        pos = off + pieces[nm].shape[0]
    if cfg.NP > pos:
        rows.append(jnp.zeros((cfg.NP - pos, wt.shape[1]), wt.dtype))
    return jnp.concatenate(rows, axis=0)


def _unpad_w_in(cfg, wpt):
    return jnp.concatenate([wpt[cfg.off[nm]:cfg.off[nm] + width] for nm, width in cfg.ref_segs], axis=0)


def _to_slots_cols(w):
    R = w.shape[0]
    return w.reshape(R, NDEV, -1).transpose(1, 0, 2)


def _from_slots_cols(s):
    return s.transpose(1, 0, 2).reshape(s.shape[1], -1)


def _perm_rows_out(cfg, w):
    return jnp.concatenate([w[cfg.A:], w[:cfg.A]], axis=0)


def _unperm_rows_out(cfg, w):
    return jnp.concatenate([w[cfg.B + cfg.C:], w[:cfg.B + cfg.C]], axis=0)


def _layer_params(cfg, l, g_pre, a_g_v, a_w_s, a_b_s, c_g_q, c_g_kv, g_out):
    A, B = cfg.A, cfg.B
    return dict(g_pre=g_pre[l][None], g_v=a_g_v[l].reshape(1, A), w_s=a_w_s[l], b_s=a_b_s[l][:, :, None],
                g_q=c_g_q[l][None], g_kv=c_g_kv[l][None],
                g_oa=g_out[l][None, :A], g_ob=g_out[l][None, A:A + B], g_oc=g_out[l][None, A + B:])


def _layer_fwd(cfg, l, x, W, p, cos2, sin2, rot, carry_stick=None, carry_mla=None):
    S, D, A, B, C = cfg.S, cfg.D, cfg.A, cfg.B, cfg.C
    tag = f"l{l}"
    (h,) = _rowwise(_f_pre, [(x, D, 0)], [], [p["g_pre"]], [], [(D, BF16)], 256, f"pre_{tag}")
    proj = _matmul(h, W["in"], "nt", BF16, f"mm_in_{tag}")
    a_rows = [cfg.view(proj, "ua"), cfg.view(proj, "va"), cfg.view(proj, "za")]
    a_par = [p["g_v"], p["w_s"], p["b_s"], p["g_oa"]]
    (ya,) = _rowwise(_f_gmlp, a_rows, [], a_par, [], [(A, BF16)], LANE, f"gmlp_{tag}")
    qb, kb, vb = cfg.heads_view(proj, "qb"), cfg.heads_view(proj, "kb"), cfg.heads_view(proj, "vb")
    yb, tot, *moved_stick = _stick_fwd(qb, kb, vb, cfg.Hb, LANE ** -0.5, f"stick_fwd_{tag}", *cfg.tiles("stick_fwd", l),
                                       carry=carry_stick)
    (ybg,) = _rowwise(_f_gate, [(yb, B, 0), cfg.view(proj, "zb")], [], [p["g_ob"]], [], [(B, BF16)], 256, f"gate_b_{tag}")
    c_rows = [cfg.view(proj, "cq"), cfg.view(proj, "ckv"), cfg.view(proj, "kr")]
    trig = [(cos2, LANE, 0), (sin2, LANE, 0)]
    cqn, ckvn, krr = _rowwise(_f_cpre, c_rows, trig, [p["g_q"], p["g_kv"]], [rot],
                              [(cfg.Q, BF16), (cfg.KV, BF16), (LANE, BF16)], 256, f"cpre_{tag}")
    q_raw = _matmul(cqn, W["uq"], "nt", BF16, f"mm_uq_{tag}")
    kv = _matmul(ckvn, W["ukv"], "nn", BF16, f"mm_ukv_{tag}")
    r_rows = [(q_raw, 2 * C, 0), (kv, 2 * C, 0), (krr, LANE, 0)]
    q_rot, k_full, v_c = _rowwise(_f_crope, r_rows, trig, [], [rot], [(2 * C, BF16), (2 * C, BF16), (C, BF16)], 128,
                                  f"crope_{tag}")
    qc, kc, vc = (q_rot, 0, 2 * LANE), (k_full, 0, 2 * LANE), (v_c, 0, LANE)
    yc, lse, *moved_mla = _softmax_fwd(qc, kc, vc, cfg.Hc, (LANE + ROPE) ** -0.5, f"mla_fwd_{tag}", *cfg.tiles("mla_fwd", l),
                                       carry=carry_mla)
    (ycg,) = _rowwise(_f_gate, [(yc, C, 0), cfg.view(proj, "zc")], [], [p["g_oc"]], [], [(C, BF16)], 256, f"gate_c_{tag}")
    y = jnp.concatenate([ybg, ycg, ya], axis=1)
    out = _matmul(y, W["out"], "nn", F32, f"mm_out_{tag}", add=x)
    saved = dict(x=x, h=h, proj=proj, yb=yb, tot=tot, cqn=cqn, ckvn=ckvn, krr=krr, q_raw=q_raw, kv=kv,
                 q_rot=q_rot, k_full=k_full, v_c=v_c, yc=yc, lse=lse, y=y)
    return out, saved, (moved_stick[0] if moved_stick else []), (moved_mla[0] if moved_mla else [])


def _layer_bwd(cfg, l, dout, sv, W, p, cos2, sin2, rot, ext_stick, ext_mla, last):
    S, D, A, B, C = cfg.S, cfg.D, cfg.A, cfg.B, cfg.C
    tag = f"l{l}"
    proj = sv["proj"]
    dy = _matmul(dout, W["out"], "nt", BF16, f"mm_dy_{tag}")
    d_wout = _matmul(sv["y"], dout, "tn", BF16, f"mm_dwout_{tag}")
    wout_slots = _unperm_rows_out(cfg, d_wout).reshape(NDEV, cfg.DMIX // NDEV, D)
    (dyb, dzb), (dg_ob,), _ = _rowwise_vjp(_f_gate, [(sv["yb"], B, 0), cfg.view(proj, "zb")], [], [p["g_ob"]], [],
                                           [(dy, B, 0)], [BF16, BF16], 256, f"gate_b_bwd_{tag}")
    (dyc, dzc), (dg_oc,), _ = _rowwise_vjp(_f_gate, [(sv["yc"], C, 0), cfg.view(proj, "zc")], [], [p["g_oc"]], [],
                                           [(dy, C, B // C)], [BF16, BF16], 256, f"gate_c_bwd_{tag}")
    a_rows = [cfg.view(proj, "ua"), cfg.view(proj, "va"), cfg.view(proj, "za")]
    a_par = [p["g_v"], p["w_s"], p["b_s"], p["g_oa"]]
    (dua, dva, dza), (dg_v, dw_s, db_s, dg_oa), _ = _rowwise_vjp(
        _f_gmlp, a_rows, [], a_par, [], [(dy, A, (B + C) // A)], [BF16] * 3, LANE, f"gmlp_bwd_{tag}")
    qb, kb, vb = cfg.heads_view(proj, "qb"), cfg.heads_view(proj, "kb"), cfg.heads_view(proj, "vb")
    dqb, dkb, dvb, moved_stick = _stick_bwd(qb, kb, vb, (dyb, 0, LANE), sv["tot"], cfg.Hb, LANE ** -0.5,
                                            f"stick_bwd_{tag}", *cfg.tiles("stick_bwd", l),
                                            carry=_Exchange([[a] for a in ext_stick + [wout_slots]], False))
    got = dict(w_out=moved_stick[-1][0])
    ext_got = [mv[0] for mv in moved_stick[:-1]]
    qc, kc, vc = (sv["q_rot"], 0, 2 * LANE), (sv["k_full"], 0, 2 * LANE), (sv["v_c"], 0, LANE)
    dq_rot, dk_full, dv_c, *moved_mla = _softmax_bwd(qc, kc, vc, (sv["yc"], 0, LANE), (dyc, 0, LANE), sv["lse"], cfg.Hc,
                                                     (LANE + ROPE) ** -0.5, f"mla_bwd_{tag}", *cfg.tiles("mla_bwd", l),
                                                     carry=_Exchange([[a] for a in ext_mla], False) if ext_mla else None)
    ext_got += [mv[0] for mv in (moved_mla[0] if moved_mla else [])]
    trig = [(cos2, LANE, 0), (sin2, LANE, 0)]
    r_rows = [(sv["q_raw"], 2 * C, 0), (sv["kv"], 2 * C, 0), (sv["krr"], LANE, 0)]
    (dq_raw, dkv, dkrr), _, _ = _rowwise_vjp(_f_crope, r_rows, trig, [], [rot],
                                             [(dq_rot, 2 * C, 0), (dk_full, 2 * C, 0), (dv_c, C, 0)], [BF16] * 3, 128,
                                             f"crope_bwd_{tag}")
    dcqn = _matmul(dq_raw, W["uq"], "nn", BF16, f"mm_dcq_{tag}")
    d_wuq = _matmul(dq_raw, sv["cqn"], "tn", BF16, f"mm_dwuq_{tag}")
    dckvn = _matmul(dkv, W["ukv"], "nt", BF16, f"mm_dckv_{tag}")
    d_wukv = _matmul(sv["ckvn"], dkv, "tn", BF16, f"mm_dwukv_{tag}")
    c_rows = [cfg.view(proj, "cq"), cfg.view(proj, "ckv"), cfg.view(proj, "kr")]
    (dcq, dckv, dkr), (dg_q, dg_kv), _ = _rowwise_vjp(
        _f_cpre, c_rows, trig, [p["g_q"], p["g_kv"]], [rot],
        [(dcqn, cfg.Q, 0), (dckvn, cfg.KV, 0), (dkrr, LANE, 0)], [BF16] * 3, 256, f"cpre_bwd_{tag}")
    parts = dict(ua=dua, va=dva, za=dza, qb=dqb, kb=dkb, vb=dvb, zb=dzb, zc=dzc, cq=dcq, kr=dkr, ckv=dckv)
    cols, pos = [], 0
    for nm, off in sorted(cfg.off.items(), key=lambda kv_: kv_[1]):
        if off > pos:
            cols.append(jnp.zeros((S, off - pos), BF16))
        cols.append(parts[nm])
        pos = off + parts[nm].shape[1]
    if cfg.NP > pos:
        cols.append(jnp.zeros((S, cfg.NP - pos), BF16))
    dproj = jnp.concatenate(cols, axis=1)
    d_win = _matmul(dproj, sv["h"], "tn", BF16, f"mm_dwin_{tag}")
    to_send = dict(w_in=_unpad_w_in(cfg, d_win).reshape(NDEV, -1, D),
                   c_w_uq=d_wuq.reshape(cfg.Hc, 2 * LANE, cfg.Q)[:, :LANE + ROPE].reshape(NDEV, -1, cfg.Q),
                   c_w_ukv=_to_slots_cols(d_wukv))
    if last:
        dh, moved = _matmul(dproj, W["in"], "nn", BF16, f"mm_dh_{tag}",
                            carry=_Exchange([[to_send[nm]] for nm in ("w_in", "c_w_uq", "c_w_ukv")], False))
        got.update(w_in=moved[0][0], c_w_uq=moved[1][0], c_w_ukv=moved[2][0])
        to_send = {}
    else:
        dh = _matmul(dproj, W["in"], "nn", BF16, f"mm_dh_{tag}")
    (dx,), (dg_pre,), _ = _rowwise_vjp(_f_pre_res, [(sv["x"], D, 0)], [], [p["g_pre"]], [],
                                       [(dh, D, 0), (dout, D, 0)], [F32], 128, f"pre_bwd_{tag}")
    small = dict(g_pre=dg_pre[0], a_g_v=dg_v.reshape(cfg.G, LANE), a_w_s=dw_s, a_b_s=db_s[:, :, 0], c_g_q=dg_q[0],
                 c_g_kv=dg_kv[0], g_out=jnp.concatenate([dg_oa[0], dg_ob[0], dg_oc[0]]))
    return dx, small, got, to_send, ext_got


def _pack_small(vals):
    pieces = []
    for nm in SMALL:
        piece = vals[nm].reshape(-1, LANE)
        pieces.append(jnp.pad(piece, ((0, -piece.shape[0] % 8), (0, 0))))
    packed = jnp.concatenate(pieces, axis=0)
    return jnp.pad(packed, ((0, -packed.shape[0] % SMALL_ROWS), (0, 0)))


def _unpack_small(packed, like):
    out, row = {}, 0
    for nm in SMALL:
        n = like[nm].size // LANE
        out[nm] = packed[row:row + n].reshape(like[nm].shape)
        row += n + (-n % 8)
    return out


def kernel(x, positions, g_pre, w_in, a_g_v, a_w_s, a_b_s, c_g_q, c_g_kv, c_w_uq, c_w_ukv, g_out, w_out, g_final, loss_target, m_g_pre, m_w_in, m_a_g_v, m_a_w_s, m_a_b_s, m_c_g_q, m_c_g_kv, m_c_w_uq, m_c_w_ukv, m_g_out, m_w_out, m_g_final, v_g_pre, v_w_in, v_a_g_v, v_a_w_s, v_a_b_s, v_c_g_q, v_c_g_kv, v_c_w_uq, v_c_w_ukv, v_g_out, v_w_out, v_g_final):
    depth, S, D = w_in.shape[0], x.shape[1], x.shape[2]
    cfg = _Cfg(S, D, a_g_v.shape[1], c_g_q.shape[1], c_g_kv.shape[1], c_w_ukv.shape[2] * NDEV // (2 * LANE), g_out.shape[1])
    weights = dict(g_pre=g_pre, w_in=w_in, a_g_v=a_g_v, a_w_s=a_w_s, a_b_s=a_b_s, c_g_q=c_g_q, c_g_kv=c_g_kv,
                   c_w_uq=c_w_uq, c_w_ukv=c_w_ukv, g_out=g_out, w_out=w_out, g_final=g_final)
    mom_m = dict(g_pre=m_g_pre, w_in=m_w_in, a_g_v=m_a_g_v, a_w_s=m_a_w_s, a_b_s=m_a_b_s, c_g_q=m_c_g_q, c_g_kv=m_c_g_kv,
                 c_w_uq=m_c_w_uq, c_w_ukv=m_c_w_ukv, g_out=m_g_out, w_out=m_w_out, g_final=m_g_final)
    mom_v = dict(g_pre=v_g_pre, w_in=v_w_in, a_g_v=v_a_g_v, a_w_s=v_a_w_s, a_b_s=v_a_b_s, c_g_q=v_c_g_q, c_g_kv=v_c_g_kv,
                 c_w_uq=v_c_w_uq, c_w_ukv=v_c_w_ukv, g_out=v_g_out, w_out=v_w_out, g_final=v_g_final)
    big_names = ("w_in", "c_w_uq", "c_w_ukv", "w_out")

    inv_freq = 1.0 / (ROPE_THETA ** (jnp.arange(0, ROPE, 2, dtype=F32) / ROPE))
    ang = positions[0].astype(F32)[:, None] * inv_freq
    zpad = jnp.zeros((S, LANE - ROPE), F32)
    cos2 = jnp.concatenate([jnp.cos(ang), jnp.cos(ang), zpad], axis=1)
    sin2 = jnp.concatenate([jnp.sin(ang), jnp.sin(ang), zpad], axis=1)
    rot = _rope_matrix()

    for tree in (weights, mom_m, mom_v):
        for nm in TRANSPOSED:
            tree[nm] = jnp.swapaxes(tree[nm], 1, 2)

    def shards(l, names):
        return [[weights[nm][l].astype(BF16)] for nm in names]

    def assemble(g_in, g_uq, g_ukv, g_wout):
        uq = jnp.pad(g_uq.reshape(cfg.Hc, LANE + ROPE, cfg.Q), ((0, 0), (0, LANE - ROPE), (0, 0))).reshape(2 * cfg.C, cfg.Q)
        return {"in": _pad_w_in(cfg, g_in.reshape(-1, D)), "uq": uq, "ukv": _from_slots_cols(g_ukv),
                "out": _perm_rows_out(cfg, g_wout.reshape(cfg.DMIX, D))}

    params = [_layer_params(cfg, l, g_pre, a_g_v, a_w_s, a_b_s, c_g_q, c_g_kv, g_out) for l in range(depth)]

    gathered = [g[0] for g in _exchange(_GatherTwoLevel(shards(0, big_names)), "gather_weights_l0")]
    hcur, saved, Ws = x[0], [], []
    for l in range(depth):
        Ws.append(assemble(*gathered))
        nxt = l + 1 < depth
        hcur, sv, got_in, got_rest = _layer_fwd(
            cfg, l, hcur, Ws[l], params[l], cos2, sin2, rot,
            carry_stick=_GatherTwoLevel(shards(l + 1, big_names[:1])) if nxt else None,
            carry_mla=_GatherTwoLevel(shards(l + 1, big_names[1:])) if nxt else None)
        saved.append(sv)
        gathered = [g[0] for g in got_in + got_rest]
    (dh,), (dg_final,), (loss_rows,) = _rowwise_vjp(
        _f_final, [(hcur, D, 0)], [(loss_target[0], D, 0)], [g_final[None]], [], [(jnp.ones((S, 1), F32), 1, 0)],
        [F32], 128, "final", primal=[(1, F32)])
    loss = lax.psum(jnp.sum(loss_rows), MESH_AXES)

    small_g, slots, pending = [None] * depth, [None] * depth, {}
    for l in reversed(range(depth)):
        ext_stick = [pending["w_in"]] if pending else []
        ext_mla = [pending["c_w_uq"], pending["c_w_ukv"]] if pending else []
        dh, small_g[l], slots[l], pending, ext_got = _layer_bwd(cfg, l, dh, saved[l], Ws[l], params[l], cos2, sin2, rot,
                                                                ext_stick, ext_mla, l == 0)
        if ext_got:
            slots[l + 1].update(w_in=ext_got[0], c_w_uq=ext_got[1], c_w_ukv=ext_got[2])
    grad_x = dh[None]
    small_grads = {nm: jnp.stack([small_g[l][nm] for l in range(depth)]) for nm in SMALL if nm != "g_final"}
    small_grads["g_final"] = dg_final[0]
    (small_slots,) = _exchange(_Exchange([[_pack_small(small_grads)]], True), "gather_small_grads")

    res = {}
    for nm in big_names:
        res[nm] = None
        for l in range(depth):
            res[nm] = _adamw(slots[l][nm], weights[nm], mom_m[nm], mom_v[nm], l, res[nm], f"adamw_{nm}_l{l}")
        if nm in TRANSPOSED:
            res[nm] = [jnp.swapaxes(r, 1, 2) for r in res[nm]]
    packed = _adamw(small_slots[0], _pack_small(weights)[None], _pack_small(mom_m)[None], _pack_small(mom_v)[None], 0, None,
                    "adamw_small")
    small_res = [_unpack_small(r[0], weights) for r in packed]
    order = ("g_pre", "w_in", "a_g_v", "a_w_s", "a_b_s", "c_g_q", "c_g_kv", "c_w_uq", "c_w_ukv", "g_out", "w_out", "g_final")
    outs = [loss, grad_x]
    for kind in range(4):
        outs += [small_res[kind][nm] if nm in SMALL else res[nm][kind] for nm in order]
    return tuple(outs)
```

```python
import functools

import numpy as np
import jax
import jax.numpy as jnp
from jax import lax
from jax.experimental import pallas as pl
from jax.experimental.pallas import tpu as pltpu

NDEV = 8
MESH_AXES = ("x", "y", "c")
LANE = 128
ROPE = 64
EPS = 1e-6
ROPE_THETA = 10000.0
ADAM_LR, ADAM_B1, ADAM_B2, ADAM_EPS, ADAM_WD, ADAM_STEP = 0.001, 0.9, 0.999, 1e-08, 0.01, 10
VMEM_LIMIT = 48 * 1024 * 1024
ADAM_TILE_BYTES = 768 * 1024
CARRY_MID_PERCENT = 80
SMALL_ROWS = 256
ATTN_TILES = {"stick_fwd": [(1024, 256)], "stick_bwd": [(1024, 256)], "mla_fwd": [(512, 1024)], "mla_bwd": [(512, 1024)]}
F32, BF16 = jnp.float32, jnp.bfloat16
SMALL = ("g_pre", "a_g_v", "a_w_s", "a_b_s", "c_g_q", "c_g_kv", "g_out", "g_final")
TRANSPOSED = ("w_in", "c_w_uq")


def _tile(dim, cap, mult=LANE):
    if dim <= cap:
        return dim
    t = (cap // mult) * mult
    while t >= mult:
        if dim % t == 0:
            return t
        t -= mult
    return dim


def _dot_nt(a, b):
    return lax.dot_general(a, b, (((1,), (1,)), ((), ())), preferred_element_type=F32)


def _dot_tn(a, b):
    return lax.dot_general(a, b, (((0,), (0,)), ((), ())), preferred_element_type=F32)


def _dot(a, b):
    return jnp.dot(a, b, preferred_element_type=F32)


class _Exchange:
    def __init__(self, groups, gather):
        self.groups, self.gather = groups, gather
        self.flat = [(gi, li, a) for gi, grp in enumerate(groups) for li, a in enumerate(grp)]
        self.n = len(self.flat)
        self.args = [a for (_, _, a) in self.flat]
        self.out_shape = [jax.ShapeDtypeStruct((len(grp), NDEV) + tuple(grp[0].shape[-2:]), grp[0].dtype) for grp in groups]
        self.scratch = [pltpu.SemaphoreType.DMA((self.n, NDEV - 1)), pltpu.SemaphoreType.DMA((self.n, NDEV - 1)),
                        pltpu.SemaphoreType.DMA((self.n,))]

    def _copies(self, ins, outs, send_sems, recv_sems, local_sems, landings):
        x, y, c = lax.axis_index("x"), lax.axis_index("y"), lax.axis_index("c")
        me = 4 * x + 2 * y + c
        owns = [pltpu.make_async_copy(ins[i] if self.gather else ins[i].at[me], outs[gi].at[li, me], local_sems.at[i])
                for i, (gi, li, _) in enumerate(self.flat)]
        pairs = []
        for k in range(1, NDEV):
            px = 1 - x if k & 4 else x
            py = 1 - y if k & 2 else y
            pc = 1 - c if k & 1 else c
            peer = 4 * px + 2 * py + pc
            for i, (gi, li, _) in enumerate(self.flat):
                src = ins[i] if self.gather else ins[i].at[peer]
                sems = dict(send_sem=send_sems.at[i, k - 1], recv_sem=recv_sems.at[i, k - 1],
                            device_id=(px, py, pc), device_id_type=pl.DeviceIdType.MESH)
                out = pltpu.make_async_remote_copy(src_ref=src, dst_ref=outs[gi].at[li, me], **sems)
                landing = pltpu.make_async_remote_copy(src_ref=src, dst_ref=outs[gi].at[li, peer], **sems) if landings else None
                pairs.append((out, landing))
        return owns, pairs

    def start(self, ins, outs, sems):
        owns, pairs = self._copies(ins, outs, *sems, landings=False)
        for own in owns:
            own.start()
        for out, _ in pairs:
            out.start()

    def mid(self, ins, outs, sems):
        pass

    def wait(self, ins, outs, sems):
        owns, pairs = self._copies(ins, outs, *sems, landings=True)
        for out, landing in pairs:
            out.wait_send()
            landing.wait_recv()
        for own in owns:
            own.wait()


class _GatherTwoLevel(_Exchange):
    def __init__(self, groups):
        super().__init__(groups, True)

    def _copy(self, i, k, ins, outs, send_sems, recv_sems, landing):
        gi, li, _ = self.flat[i]
        x, y, c = lax.axis_index("x"), lax.axis_index("y"), lax.axis_index("c")
        chips = [(x, y), (1 - x, y), (x, 1 - y), (1 - x, 1 - y)]

        def slot(chip, core):
            return outs[gi].at[li, 4 * chip[0] + 2 * chip[1] + core]

        if k == 0:
            to, src, dst, lands = (x, y, 1 - c), ins[i], slot(chips[0], c), slot(chips[0], 1 - c)
        elif k <= 3:
            to, src, dst, lands = (*chips[k], c), ins[i], slot(chips[0], c), slot(chips[k], c)
        else:
            to, src, dst, lands = (x, y, 1 - c), slot(chips[k - 3], c), slot(chips[k - 3], c), slot(chips[k - 3], 1 - c)
        return pltpu.make_async_remote_copy(src_ref=src, dst_ref=lands if landing else dst, send_sem=send_sems.at[i, k],
                                            recv_sem=recv_sems.at[i, k], device_id=to, device_id_type=pl.DeviceIdType.MESH)

    def _own(self, i, ins, outs, local_sems):
        gi, li, _ = self.flat[i]
        me = 4 * lax.axis_index("x") + 2 * lax.axis_index("y") + lax.axis_index("c")
        return pltpu.make_async_copy(ins[i], outs[gi].at[li, me], local_sems.at[i])

    def start(self, ins, outs, sems):
        send_sems, recv_sems, local_sems = sems
        for i in range(self.n):
            self._own(i, ins, outs, local_sems).start()
        for k in range(4):
            for i in range(self.n):
                self._copy(i, k, ins, outs, send_sems, recv_sems, False).start()

    def mid(self, ins, outs, sems):
        send_sems, recv_sems, _ = sems
        for k in range(1, 4):
            for i in range(self.n):
                self._copy(i, k, ins, outs, send_sems, recv_sems, True).wait_recv()
                self._copy(i, k + 3, ins, outs, send_sems, recv_sems, False).start()

    def wait(self, ins, outs, sems):
        send_sems, recv_sems, local_sems = sems
        for k in (0, 4, 5, 6):
            for i in range(self.n):
                self._copy(i, k, ins, outs, send_sems, recv_sems, True).wait_recv()
        for k in range(NDEV - 1):
            for i in range(self.n):
                self._copy(i, k, ins, outs, send_sems, recv_sems, False).wait_send()
        for i in range(self.n):
            self._own(i, ins, outs, local_sems).wait()


def _call(body, name, grid, in_specs, out_specs, out_shape, scratch, semantics, args, carry=None):
    n_in, n_out, n_scr = len(in_specs), len(out_specs), len(scratch)
    if carry is None:
        run = body
    else:
        semantics = ("arbitrary",) * len(grid)
        anyspec = pl.BlockSpec(memory_space=pl.ANY)
        in_specs = list(in_specs) + [anyspec] * carry.n
        out_specs = list(out_specs) + [anyspec] * len(carry.groups)
        out_shape = list(out_shape) + carry.out_shape
        scratch = list(scratch) + carry.scratch
        args = list(args) + carry.args

        def run(*refs):
            c_in, x_in = refs[:n_in], refs[n_in:n_in + carry.n]
            rest = refs[n_in + carry.n:]
            c_out, x_out = rest[:n_out], rest[n_out:n_out + len(carry.groups)]
            c_scr, sems = rest[n_out + len(carry.groups):len(rest) - 3], rest[len(rest) - 3:]
            step, total = 0, 1
            for d, extent in enumerate(grid):
                step = step * extent + pl.program_id(d)
                total *= extent

            @pl.when(step == 0)
            def _():
                carry.start(x_in, x_out, sems)

            body(*c_in, *c_out, *c_scr)

            @pl.when(step == (total * CARRY_MID_PERCENT) // 100)
            def _():
                carry.mid(x_in, x_out, sems)

            @pl.when(step == total - 1)
            def _():
                carry.wait(x_in, x_out, sems)

    res = pl.pallas_call(
        run, name=name, grid=grid, out_shape=list(out_shape), in_specs=list(in_specs), out_specs=list(out_specs),
        scratch_shapes=list(scratch),
        compiler_params=pltpu.CompilerParams(dimension_semantics=semantics, vmem_limit_bytes=VMEM_LIMIT,
                                             has_side_effects=carry is not None),
    )(*args)
    return list(res[:n_out]), list(res[n_out:])


def _exchange(ex, name):
    groups = ex.groups

    def body(*refs):
        ins, outs, sems = refs[:ex.n], refs[ex.n:ex.n + len(groups)], refs[ex.n + len(groups):]
        ex.start(ins, outs, sems)
        ex.mid(ins, outs, sems)
        ex.wait(ins, outs, sems)

    anyspec = pl.BlockSpec(memory_space=pl.ANY)
    return pl.pallas_call(
        body, name=name, out_shape=ex.out_shape, in_specs=[anyspec] * ex.n, out_specs=[anyspec] * len(groups),
        scratch_shapes=ex.scratch, compiler_params=pltpu.CompilerParams(has_side_effects=True),
    )(*ex.args)


def _matmul(a, b, mode, out_dtype, name, add=None, tm=1024, tn=1024, tk=1024, carry=None):
    if mode == "tn":
        (K, M), (K2, N) = a.shape, b.shape
    elif mode == "nt":
        (M, K), (N, K2) = a.shape, b.shape
    else:
        (M, K), (K2, N) = a.shape, b.shape
    assert K == K2, (a.shape, b.shape, mode)
    tm, tn, tk = _tile(M, tm), _tile(N, tn), _tile(K, tk)
    nk = K // tk
    a_spec = pl.BlockSpec((tk, tm), lambda i, j, k: (k, i)) if mode == "tn" else pl.BlockSpec((tm, tk), lambda i, j, k: (i, k))
    b_spec = pl.BlockSpec((tn, tk), lambda i, j, k: (j, k)) if mode == "nt" else pl.BlockSpec((tk, tn), lambda i, j, k: (k, j))
    dot = {"nn": _dot, "nt": _dot_nt, "tn": _dot_tn}[mode]
    has_add = add is not None

    def body(*refs):
        a_ref, b_ref = refs[0], refs[1]
        o_ref, acc = refs[-2], refs[-1]
        k = pl.program_id(2)

        @pl.when(k == 0)
        def _():
            acc[...] = jnp.zeros_like(acc)

        acc[...] += dot(a_ref[...].astype(BF16), b_ref[...].astype(BF16))

        @pl.when(k == nk - 1)
        def _():
            r = acc[...]
            if has_add:
                r = r + refs[2][...]
            o_ref[...] = r.astype(o_ref.dtype)

    in_specs = [a_spec, b_spec]
    args = [a, b]
    if has_add:
        in_specs.append(pl.BlockSpec((tm, tn), lambda i, j, k: (i, j)))
        args.append(add)
    (out,), moved = _call(body, name, (M // tm, N // tn, nk), in_specs, [pl.BlockSpec((tm, tn), lambda i, j, k: (i, j))],
                          [jax.ShapeDtypeStruct((M, N), out_dtype)], [pltpu.VMEM((tm, tn), F32)],
                          ("parallel", "parallel", "arbitrary"), args, carry)
    return out if carry is None else (out, moved)


def _row_specs(views, tile):
    return [pl.BlockSpec((tile, w), functools.partial(lambda i, cb: (i, cb), cb=cb)) for (_, w, cb) in views]


def _full_specs(arrs):
    return [pl.BlockSpec(p.shape, functools.partial(lambda i, nd: (0,) * nd, nd=p.ndim)) for p in arrs]


def _rowwise(fn, rows, aux, params, consts, outs, tile, name):
    S = rows[0][0].shape[0]
    nr, na, npar, nc = len(rows), len(aux), len(params), len(consts)

    def body(*refs):
        ins = [r[...].astype(F32) for r in refs[:nr + na]]
        small = [r[...] for r in refs[nr + na:nr + na + npar + nc]]
        res = fn(*ins, *small)
        for o_ref, r in zip(refs[nr + na + npar + nc:], res):
            o_ref[...] = r.astype(o_ref.dtype)

    return pl.pallas_call(
        body, name=name, grid=(S // tile,),
        out_shape=[jax.ShapeDtypeStruct((S, w), dt) for (w, dt) in outs],
        in_specs=_row_specs(rows + aux, tile) + _full_specs(params + consts),
        out_specs=[pl.BlockSpec((tile, w), lambda i: (i, 0)) for (w, _) in outs],
        compiler_params=pltpu.CompilerParams(dimension_semantics=("parallel",), vmem_limit_bytes=VMEM_LIMIT),
    )(*[v[0] for v in rows + aux], *params, *consts)


def _rowwise_vjp(fn, rows, aux, params, consts, cots, grad_dtypes, tile, name, primal=()):
    S = rows[0][0].shape[0]
    nr, na, npar, nc, nct, npr = len(rows), len(aux), len(params), len(consts), len(cots), len(primal)

    def body(*refs):
        n_in = nr + na + npar + nc + nct
        rv = [r[...].astype(F32) for r in refs[:nr]]
        av = [r[...].astype(F32) for r in refs[nr:nr + na]]
        pv = [r[...] for r in refs[nr + na:nr + na + npar]]
        cv = [r[...] for r in refs[nr + na + npar:nr + na + npar + nc]]
        ct = tuple(r[...].astype(F32) for r in refs[nr + na + npar + nc:n_in])
        res, vjp = jax.vjp(lambda *rp: tuple(fn(*rp[:nr], *av, *rp[nr:], *cv)), *rv, *pv)
        grads = vjp(ct)
        g_refs = refs[n_in:n_in + nr]
        p_refs = refs[n_in + nr:n_in + nr + npar]
        o_refs = refs[n_in + nr + npar:]
        for g_ref, g in zip(g_refs, grads[:nr]):
            g_ref[...] = g.astype(g_ref.dtype)

        @pl.when(pl.program_id(0) == 0)
        def _():
            for p_ref in p_refs:
                p_ref[...] = jnp.zeros_like(p_ref)

        for p_ref, g in zip(p_refs, grads[nr:]):
            p_ref[...] += g
        for o_ref, r in zip(o_refs, res[:npr]):
            o_ref[...] = r.astype(o_ref.dtype)

    out_shape = ([jax.ShapeDtypeStruct((S, w), dt) for (_, w, _), dt in zip(rows, grad_dtypes)]
                 + [jax.ShapeDtypeStruct(p.shape, F32) for p in params]
                 + [jax.ShapeDtypeStruct((S, w), dt) for (w, dt) in primal])
    out_specs = ([pl.BlockSpec((tile, w), lambda i: (i, 0)) for (_, w, _) in rows] + _full_specs(params)
                 + [pl.BlockSpec((tile, w), lambda i: (i, 0)) for (w, _) in primal])
    res = pl.pallas_call(
        body, name=name, grid=(S // tile,), out_shape=out_shape,
        in_specs=_row_specs(rows + aux, tile) + _full_specs(params + consts) + _row_specs(cots, tile),
        out_specs=out_specs,
        compiler_params=pltpu.CompilerParams(dimension_semantics=("arbitrary",), vmem_limit_bytes=VMEM_LIMIT),
    )(*[v[0] for v in rows + aux], *params, *consts, *[v[0] for v in cots])
    return res[:nr], res[nr:nr + npar], res[nr + npar:]


@jax.custom_vjp
def _mm(a, b):
    return _dot(a.astype(BF16), b.astype(BF16))


def _mm_fwd(a, b):
    return _mm(a, b), (a, b)


def _mm_bwd(res, ct):
    a, b = res
    ctb = ct.astype(BF16)
    return _dot_nt(ctb, b.astype(BF16)), _dot_tn(a.astype(BF16), ctb)


_mm.defvjp(_mm_fwd, _mm_bwd)


def _rms(x, g):
    return x * lax.rsqrt(jnp.mean(x * x, axis=-1, keepdims=True) + EPS) * g


def _f_pre(x, g):
    return (_rms(x, g),)


def _f_pre_res(x, g):
    return _rms(x, g), x


def _f_gate(y, z, g):
    return (_rms(y, g) * jax.nn.silu(z),)


def _f_gmlp(u, v, z, g_v, w_s, b_s, g_o):
    groups = w_s.shape[0]
    u, v = jax.nn.gelu(u), jax.nn.gelu(v)
    t_idx = lax.broadcasted_iota(jnp.int32, (LANE, LANE), 0)
    s_idx = lax.broadcasted_iota(jnp.int32, (LANE, LANE), 1)
    ys = []
    for g in range(groups):
        sl = slice(g * LANE, (g + 1) * LANE)
        vn = _rms(v[:, sl], g_v[:, sl])
        w = jnp.where(s_idx <= t_idx, w_s[g], 0.0)
        ys.append(u[:, sl] * (_mm(w, vn) + b_s[g]))
    return (_rms(jnp.concatenate(ys, axis=1), g_o) * jax.nn.silu(z),)


def _rope(x, cos2, sin2, rot):
    return x * cos2 + _mm(x, rot) * sin2


def _f_cpre(cq, ckv, kr, cos2, sin2, g_q, g_kv, rot):
    return _rms(cq, g_q), _rms(ckv, g_kv), _rope(kr, cos2, sin2, rot)


def _f_crope(q, kv, krr, cos2, sin2, rot):
    heads = q.shape[1] // (2 * LANE)
    qs, ks, vs = [], [], []
    for h in range(heads):
        lo, mid, hi = 2 * h * LANE, (2 * h + 1) * LANE, (2 * h + 2) * LANE
        qs += [q[:, lo:mid], _rope(q[:, mid:hi], cos2, sin2, rot)]
        ks += [kv[:, lo:mid], krr]
        vs += [kv[:, mid:hi]]
    return jnp.concatenate(qs, axis=1), jnp.concatenate(ks, axis=1), jnp.concatenate(vs, axis=1)


def _f_final(h, target, g):
    err = _rms(h, g) - target
    return (0.5 * jnp.mean(err * err, axis=-1, keepdims=True),)


def _rope_matrix():
    r = np.zeros((LANE, LANE), np.float32)
    half = ROPE // 2
    for i in range(half):
        r[i + half, i] = -1.0
        r[i, i + half] = 1.0
    return jnp.asarray(r)


def _head_spec(view, rows, n_rows_block):
    _, cb0, w = view
    if n_rows_block:
        return pl.BlockSpec((rows, w), functools.partial(lambda h, i, cb0: (i, cb0 + h), cb0=cb0))
    return pl.BlockSpec((rows, w), functools.partial(lambda h, i, cb0: (0, cb0 + h), cb0=cb0))


def _stat_spec(tq):
    return pl.BlockSpec((1, tq, 1), lambda h, i: (h, i, 0))


def _softplus(z):
    return jnp.maximum(z, 0.0) + jnp.log(1.0 + jnp.exp(-jnp.abs(z)))


def _cumsum_mm(x, m01):
    hi = x.astype(BF16)
    lo = (x - hi.astype(F32)).astype(BF16)
    return _dot(hi, m01) + _dot(lo, m01)


def _attn_call(body, name, heads, S, tq, ins, in_blocked, outs, out_blocked, scratch, stats_in=0, stats_out=0, carry=None):
    in_specs = [_head_spec(v, tq if blk else S, blk) for v, blk in zip(ins[:len(ins) - stats_in], in_blocked)]
    in_specs += [_stat_spec(tq)] * stats_in
    out_specs = [_head_spec((None, 0, w), tq if blk else S, blk) for (w, _), blk in zip(outs, out_blocked)]
    out_specs += [_stat_spec(tq)] * stats_out
    out_shape = [jax.ShapeDtypeStruct((S, heads * w), dt) for (w, dt) in outs]
    out_shape += [jax.ShapeDtypeStruct((heads, S, 1), F32)] * stats_out
    args = [v[0] for v in ins[:len(ins) - stats_in]] + list(ins[len(ins) - stats_in:])
    res, moved = _call(body, name, (heads, S // tq), in_specs, out_specs, out_shape, scratch, ("arbitrary", "arbitrary"),
                       args, carry)
    return res if carry is None else res + [moved]


def _softmax_fwd(q, k, v, heads, scale, name, tq, bk, carry=None):
    S, dv = q[0].shape[0], v[2]

    def body(q_ref, k_ref, v_ref, o_ref, lse_ref):
        qi = pl.program_id(1)
        qv = q_ref[...]
        row = qi * tq + lax.broadcasted_iota(jnp.int32, (tq, bk), 0)
        col0 = lax.broadcasted_iota(jnp.int32, (tq, bk), 1)

        def step(kb, carry):
            m, l, acc = carry
            sl = pl.ds(pl.multiple_of(kb * bk, bk), bk)
            s = _dot_nt(qv, k_ref[sl, :]) * scale
            s = jnp.where(kb * bk + col0 <= row, s, -1e30)
            m_new = jnp.maximum(m, jnp.max(s, axis=1, keepdims=True))
            p = jnp.exp(s - m_new)
            alpha = jnp.exp(m - m_new)
            l = alpha * l + jnp.sum(p, axis=1, keepdims=True)
            acc = alpha * acc + _dot(p.astype(BF16), v_ref[sl, :])
            return m_new, l, acc

        n_kb = (qi * tq + tq + bk - 1) // bk
        m, l, acc = lax.fori_loop(0, n_kb, step, (jnp.full((tq, 1), -1e30, F32), jnp.zeros((tq, 1), F32),
                                                  jnp.zeros((tq, dv), F32)))
        o_ref[...] = (acc / l).astype(o_ref.dtype)
        lse_ref[0] = m + jnp.log(l)

    return _attn_call(body, name, heads, S, tq, [q, k, v], [1, 0, 0], [(dv, BF16)], [1], [], stats_out=1, carry=carry)


def _softmax_bwd(q, k, v, o, do, lse, heads, scale, name, tq, bk, carry=None):
    S, dq_w, dv = q[0].shape[0], q[2], v[2]
    nq = S // tq

    def body(q_ref, k_ref, v_ref, o_ref, do_ref, lse_ref, dq_ref, dk_ref, dv_ref, dk_acc, dv_acc):
        qi = pl.program_id(1)

        @pl.when(qi == 0)
        def _():
            dk_acc[...] = jnp.zeros_like(dk_acc)
            dv_acc[...] = jnp.zeros_like(dv_acc)

        qv, dov = q_ref[...], do_ref[...]
        delta = jnp.sum(dov.astype(F32) * o_ref[...].astype(F32), axis=1, keepdims=True)
        lse_v = lse_ref[0]
        row = qi * tq + lax.broadcasted_iota(jnp.int32, (tq, bk), 0)
        col0 = lax.broadcasted_iota(jnp.int32, (tq, bk), 1)

        def step(kb, dq):
            sl = pl.ds(pl.multiple_of(kb * bk, bk), bk)
            ks, vs = k_ref[sl, :], v_ref[sl, :]
            s = _dot_nt(qv, ks) * scale
            p = jnp.where(kb * bk + col0 <= row, jnp.exp(s - lse_v), 0.0)
            ds = (p * (_dot_nt(dov, vs) - delta) * scale).astype(BF16)
            dk_acc[sl, :] += _dot_tn(ds, qv)
            dv_acc[sl, :] += _dot_tn(p.astype(BF16), dov)
            return dq + _dot(ds, ks)

        n_kb = (qi * tq + tq + bk - 1) // bk
        dq_ref[...] = lax.fori_loop(0, n_kb, step, jnp.zeros((tq, dq_w), F32)).astype(dq_ref.dtype)

        @pl.when(qi == nq - 1)
        def _():
            dk_ref[...] = dk_acc[...].astype(dk_ref.dtype)
            dv_ref[...] = dv_acc[...].astype(dv_ref.dtype)

    return _attn_call(body, name, heads, S, tq, [q, k, v, o, do, lse], [1, 0, 0, 1, 1],
                      [(dq_w, BF16), (dq_w, BF16), (dv, BF16)], [1, 0, 0],
                      [pltpu.VMEM((S, dq_w), F32), pltpu.VMEM((S, dv), F32)], stats_in=1, carry=carry)


def _stick_fwd(q, k, v, heads, scale, name, tq, bk, carry=None):
    S, dv = q[0].shape[0], v[2]

    assert tq % bk == 0
    n_sub = tq // bk

    def body(q_ref, k_ref, v_ref, o_ref, tot_ref, c_scr, acc_scr):
        qi = pl.program_id(1)
        m_gt = (lax.broadcasted_iota(jnp.int32, (bk, bk), 0) > lax.broadcasted_iota(jnp.int32, (bk, bk), 1)).astype(BF16)

        def block(r0, sl, masked):
            rows = tq - r0
            z = _dot_nt(q_ref[r0:, :], k_ref[sl, :]) * scale
            sp = _softplus(z)
            lk = -sp
            if masked:
                mask = lax.broadcasted_iota(jnp.int32, (rows, bk), 1) < lax.broadcasted_iota(jnp.int32, (rows, bk), 0)
                lk = jnp.where(mask, lk, 0.0)
            after = _cumsum_mm(lk, m_gt) + c_scr[r0:, :]
            a = jnp.exp(z - sp + after)
            if masked:
                a = jnp.where(mask, a, 0.0)
            acc_scr[r0:, :] += _dot(a.astype(BF16), v_ref[sl, :])
            c_scr[r0:, :] += jnp.sum(lk, axis=1, keepdims=True)

        c_scr[...] = jnp.zeros_like(c_scr)
        acc_scr[...] = jnp.zeros_like(acc_scr)
        for j in reversed(range(n_sub)):
            block(j * bk, pl.ds(pl.multiple_of(qi * tq + j * bk, bk), bk), True)

        def step(it, _):
            block(0, pl.ds(pl.multiple_of((qi * n_sub - 1 - it) * bk, bk), bk), False)
            return 0

        lax.fori_loop(0, qi * n_sub, step, 0)
        o_ref[...] = acc_scr[...].astype(o_ref.dtype)
        tot_ref[0] = c_scr[...]

    return _attn_call(body, name, heads, S, tq, [q, k, v], [1, 0, 0], [(dv, BF16)], [1],
                      [pltpu.VMEM((tq, 1), F32), pltpu.VMEM((tq, dv), F32)], stats_out=1, carry=carry)


def _stick_bwd(q, k, v, do, tot, heads, scale, name, tq, bk, carry=None):
    S, dq_w, dv = q[0].shape[0], q[2], v[2]
    nq = S // tq

    assert tq % bk == 0
    n_sub = tq // bk

    def body(q_ref, k_ref, v_ref, do_ref, tot_ref, dq_ref, dk_ref, dv_ref, dk_acc, dv_acc, pc_scr, gc_scr, dq_scr):
        qi = pl.program_id(1)

        @pl.when(qi == 0)
        def _():
            dk_acc[...] = jnp.zeros_like(dk_acc)
            dv_acc[...] = jnp.zeros_like(dv_acc)

        j_idx = lax.broadcasted_iota(jnp.int32, (bk, bk), 0)
        s_idx = lax.broadcasted_iota(jnp.int32, (bk, bk), 1)
        m_le, m_lt = (j_idx <= s_idx).astype(BF16), (j_idx < s_idx).astype(BF16)

        def block(r0, sl, masked):
            rows = tq - r0
            qv, dov = q_ref[r0:, :], do_ref[r0:, :]
            ks, vs = k_ref[sl, :], v_ref[sl, :]
            z = _dot_nt(qv, ks) * scale
            sp = _softplus(z)
            lk = -sp
            if masked:
                mask = lax.broadcasted_iota(jnp.int32, (rows, bk), 1) < lax.broadcasted_iota(jnp.int32, (rows, bk), 0)
                lk = jnp.where(mask, lk, 0.0)
            after = tot_ref[0, r0:, :] - pc_scr[r0:, :] - _cumsum_mm(lk, m_le)
            log_beta = z - sp
            a = jnp.exp(log_beta + after)
            if masked:
                a = jnp.where(mask, a, 0.0)
            g = _dot_nt(dov, vs) * a
            cg = gc_scr[r0:, :] + _cumsum_mm(g, m_lt)
            dz = g * jnp.exp(-sp) - jnp.exp(log_beta) * cg
            if masked:
                dz = jnp.where(mask, dz, 0.0)
            dz = (dz * scale).astype(BF16)
            dk_acc[sl, :] += _dot_tn(dz, qv)
            dv_acc[sl, :] += _dot_tn(a.astype(BF16), dov)
            dq_scr[r0:, :] += _dot(dz, ks)
            pc_scr[r0:, :] += jnp.sum(lk, axis=1, keepdims=True)
            gc_scr[r0:, :] += jnp.sum(g, axis=1, keepdims=True)

        pc_scr[...] = jnp.zeros_like(pc_scr)
        gc_scr[...] = jnp.zeros_like(gc_scr)
        dq_scr[...] = jnp.zeros_like(dq_scr)

        def step(kb, _):
            block(0, pl.ds(pl.multiple_of(kb * bk, bk), bk), False)
            return 0

        lax.fori_loop(0, qi * n_sub, step, 0)
        for j in range(n_sub):
            block(j * bk, pl.ds(pl.multiple_of(qi * tq + j * bk, bk), bk), True)
        dq_ref[...] = dq_scr[...].astype(dq_ref.dtype)

        @pl.when(qi == nq - 1)
        def _():
            dk_ref[...] = dk_acc[...].astype(dk_ref.dtype)
            dv_ref[...] = dv_acc[...].astype(dv_ref.dtype)

    return _attn_call(body, name, heads, S, tq, [q, k, v, do, tot], [1, 0, 0, 1],
                      [(dq_w, BF16), (dq_w, BF16), (dv, BF16)], [1, 0, 0],
                      [pltpu.VMEM((S, dq_w), F32), pltpu.VMEM((S, dv), F32), pltpu.VMEM((tq, 1), F32),
                       pltpu.VMEM((tq, 1), F32), pltpu.VMEM((tq, dq_w), F32)], stats_in=1, carry=carry)


def _adamw(slots, w, m, v, layer, prev, name):
    _, R, C = slots.shape
    L = w.shape[0]
    item = slots.dtype.itemsize
    tc = _tile(C, 2048)
    tr = _tile(R, max(16, ADAM_TILE_BYTES // (item * tc)), mult=16)
    if tr == R and R * tc * item > ADAM_TILE_BYTES:
        tc = _tile(C, max(LANE, ADAM_TILE_BYTES // (item * R)))
    c1, c2 = 1.0 - ADAM_B1 ** ADAM_STEP, 1.0 - ADAM_B2 ** ADAM_STEP
    n_prev = 0 if prev is None else 4

    def body(s_ref, w_ref, m_ref, v_ref, *rest):
        g_out, d_out, m_out, v_out = rest[n_prev:]
        g = s_ref[0].astype(F32)
        for k in range(1, NDEV):
            g = g + s_ref[k].astype(F32)
        m_new = ADAM_B1 * m_ref[0] + (1.0 - ADAM_B1) * g
        v_new = ADAM_B2 * v_ref[0] + (1.0 - ADAM_B2) * (g * g)
        g_out[0] = g
        m_out[0] = m_new
        v_out[0] = v_new
        d_out[0] = -ADAM_LR * ((m_new / c1) / (jnp.sqrt(v_new / c2) + ADAM_EPS) + ADAM_WD * w_ref[0])

    spec = pl.BlockSpec((1, tr, tc), lambda i, j: (layer, i, j))
    in_specs = [pl.BlockSpec((NDEV, tr, tc), lambda i, j: (0, i, j)), spec, spec, spec]
    in_specs += [pl.BlockSpec(memory_space=pl.ANY)] * n_prev
    return pl.pallas_call(
        body, name=name, grid=(R // tr, C // tc), out_shape=[jax.ShapeDtypeStruct((L, R, C), F32)] * 4,
        in_specs=in_specs, out_specs=[spec] * 4, input_output_aliases={4 + i: i for i in range(n_prev)},
        compiler_params=pltpu.CompilerParams(dimension_semantics=("parallel", "parallel"), vmem_limit_bytes=VMEM_LIMIT),
    )(slots, w, m, v, *(prev or []))


class _Cfg:
    def __init__(self, S, D, groups, q_lora, kv_lora, c_heads, d_mix):
        self.S, self.D, self.G, self.Q, self.KV, self.Hc, self.DMIX = S, D, groups, q_lora, kv_lora, c_heads, d_mix
        self.A, self.C = groups * LANE, c_heads * LANE
        self.B = d_mix - self.A - self.C
        self.Hb = self.B // LANE
        A, B, C = self.A, self.B, self.C
        assert B % LANE == 0 and B % C == 0 and (B + C) % A == 0
        self.ref_segs = [("ua", A), ("va", A), ("za", A), ("qb", B), ("kb", B), ("vb", B), ("zb", B),
                         ("cq", q_lora), ("ckv", kv_lora), ("kr", ROPE), ("zc", C)]
        self.off, off = {}, 0
        for nm, w in [("ua", A), ("va", A), ("za", A), ("qb", B), ("kb", B), ("vb", B), ("zb", B), ("zc", C),
                      ("cq", q_lora), ("kr", LANE), ("ckv", kv_lora)]:
            off = -(-off // w) * w
            self.off[nm] = off
            off += w
        self.NP = -(-off // 512) * 512
        self.width = {"kr": LANE, **{nm: w for nm, w in self.ref_segs if nm != "kr"}}

    def tiles(self, kind, layer):
        tq, bk = ATTN_TILES[kind][layer % len(ATTN_TILES[kind])]
        return min(tq, self.S), min(bk, self.S)

    def view(self, arr, nm):
        w = self.width[nm]
        return (arr, w, self.off[nm] // w)

    def heads_view(self, arr, nm):
        return (arr, self.off[nm] // LANE, LANE)


def _pad_w_in(cfg, wt):
    pieces, start = {}, 0
    for nm, width in cfg.ref_segs:
        pieces[nm] = wt[start:start + width]
        start += width
    rows, pos = [], 0
    for nm, off in sorted(cfg.off.items(), key=lambda kv: kv[1]):
        if off > pos:
            rows.append(jnp.zeros((off - pos, wt.shape[1]), wt.dtype))
        rows.append(pieces[nm])
        pos = off + pieces[nm].shape[0]
    if cfg.NP > pos:
        rows.append(jnp.zeros((cfg.NP - pos, wt.shape[1]), wt.dtype))
    return jnp.concatenate(rows, axis=0)


def _unpad_w_in(cfg, wpt):
    return jnp.concatenate([wpt[cfg.off[nm]:cfg.off[nm] + width] for nm, width in cfg.ref_segs], axis=0)


def _to_slots_cols(w):
    R = w.shape[0]
    return w.reshape(R, NDEV, -1).transpose(1, 0, 2)


def _from_slots_cols(s):
    return s.transpose(1, 0, 2).reshape(s.shape[1], -1)


def _perm_rows_out(cfg, w):
    return jnp.concatenate([w[cfg.A:], w[:cfg.A]], axis=0)


def _unperm_rows_out(cfg, w):
    return jnp.concatenate([w[cfg.B + cfg.C:], w[:cfg.B + cfg.C]], axis=0)


def _layer_params(cfg, l, g_pre, a_g_v, a_w_s, a_b_s, c_g_q, c_g_kv, g_out):
    A, B = cfg.A, cfg.B
    return dict(g_pre=g_pre[l][None], g_v=a_g_v[l].reshape(1, A), w_s=a_w_s[l], b_s=a_b_s[l][:, :, None],
                g_q=c_g_q[l][None], g_kv=c_g_kv[l][None],
                g_oa=g_out[l][None, :A], g_ob=g_out[l][None, A:A + B], g_oc=g_out[l][None, A + B:])


def _layer_fwd(cfg, l, x, W, p, cos2, sin2, rot, carry_stick=None, carry_mla=None):
    S, D, A, B, C = cfg.S, cfg.D, cfg.A, cfg.B, cfg.C
    tag = f"l{l}"
    (h,) = _rowwise(_f_pre, [(x, D, 0)], [], [p["g_pre"]], [], [(D, BF16)], 256, f"pre_{tag}")
    proj = _matmul(h, W["in"], "nt", BF16, f"mm_in_{tag}")
    a_rows = [cfg.view(proj, "ua"), cfg.view(proj, "va"), cfg.view(proj, "za")]
    a_par = [p["g_v"], p["w_s"], p["b_s"], p["g_oa"]]
    (ya,) = _rowwise(_f_gmlp, a_rows, [], a_par, [], [(A, BF16)], LANE, f"gmlp_{tag}")
    qb, kb, vb = cfg.heads_view(proj, "qb"), cfg.heads_view(proj, "kb"), cfg.heads_view(proj, "vb")
    yb, tot, *moved_stick = _stick_fwd(qb, kb, vb, cfg.Hb, LANE ** -0.5, f"stick_fwd_{tag}", *cfg.tiles("stick_fwd", l),
                                       carry=carry_stick)
    (ybg,) = _rowwise(_f_gate, [(yb, B, 0), cfg.view(proj, "zb")], [], [p["g_ob"]], [], [(B, BF16)], 256, f"gate_b_{tag}")
    c_rows = [cfg.view(proj, "cq"), cfg.view(proj, "ckv"), cfg.view(proj, "kr")]
    trig = [(cos2, LANE, 0), (sin2, LANE, 0)]
    cqn, ckvn, krr = _rowwise(_f_cpre, c_rows, trig, [p["g_q"], p["g_kv"]], [rot],
                              [(cfg.Q, BF16), (cfg.KV, BF16), (LANE, BF16)], 256, f"cpre_{tag}")
    q_raw = _matmul(cqn, W["uq"], "nt", BF16, f"mm_uq_{tag}")
    kv = _matmul(ckvn, W["ukv"], "nn", BF16, f"mm_ukv_{tag}")
    r_rows = [(q_raw, 2 * C, 0), (kv, 2 * C, 0), (krr, LANE, 0)]
    q_rot, k_full, v_c = _rowwise(_f_crope, r_rows, trig, [], [rot], [(2 * C, BF16), (2 * C, BF16), (C, BF16)], 128,
                                  f"crope_{tag}")
    qc, kc, vc = (q_rot, 0, 2 * LANE), (k_full, 0, 2 * LANE), (v_c, 0, LANE)
    yc, lse, *moved_mla = _softmax_fwd(qc, kc, vc, cfg.Hc, (LANE + ROPE) ** -0.5, f"mla_fwd_{tag}", *cfg.tiles("mla_fwd", l),
                                       carry=carry_mla)
    (ycg,) = _rowwise(_f_gate, [(yc, C, 0), cfg.view(proj, "zc")], [], [p["g_oc"]], [], [(C, BF16)], 256, f"gate_c_{tag}")
    y = jnp.concatenate([ybg, ycg, ya], axis=1)
    out = _matmul(y, W["out"], "nn", F32, f"mm_out_{tag}", add=x)
    saved = dict(x=x, h=h, proj=proj, yb=yb, tot=tot, cqn=cqn, ckvn=ckvn, krr=krr, q_raw=q_raw, kv=kv,
                 q_rot=q_rot, k_full=k_full, v_c=v_c, yc=yc, lse=lse, y=y)
    return out, saved, (moved_stick[0] if moved_stick else []), (moved_mla[0] if moved_mla else [])


def _layer_bwd(cfg, l, dout, sv, W, p, cos2, sin2, rot, ext_stick, ext_mla, last):
    S, D, A, B, C = cfg.S, cfg.D, cfg.A, cfg.B, cfg.C
    tag = f"l{l}"
    proj = sv["proj"]
    dy = _matmul(dout, W["out"], "nt", BF16, f"mm_dy_{tag}")
    d_wout = _matmul(sv["y"], dout, "tn", BF16, f"mm_dwout_{tag}")
    wout_slots = _unperm_rows_out(cfg, d_wout).reshape(NDEV, cfg.DMIX // NDEV, D)
    (dyb, dzb), (dg_ob,), _ = _rowwise_vjp(_f_gate, [(sv["yb"], B, 0), cfg.view(proj, "zb")], [], [p["g_ob"]], [],
                                           [(dy, B, 0)], [BF16, BF16], 256, f"gate_b_bwd_{tag}")
    (dyc, dzc), (dg_oc,), _ = _rowwise_vjp(_f_gate, [(sv["yc"], C, 0), cfg.view(proj, "zc")], [], [p["g_oc"]], [],
                                           [(dy, C, B // C)], [BF16, BF16], 256, f"gate_c_bwd_{tag}")
    a_rows = [cfg.view(proj, "ua"), cfg.view(proj, "va"), cfg.view(proj, "za")]
    a_par = [p["g_v"], p["w_s"], p["b_s"], p["g_oa"]]
    (dua, dva, dza), (dg_v, dw_s, db_s, dg_oa), _ = _rowwise_vjp(
        _f_gmlp, a_rows, [], a_par, [], [(dy, A, (B + C) // A)], [BF16] * 3, LANE, f"gmlp_bwd_{tag}")
    qb, kb, vb = cfg.heads_view(proj, "qb"), cfg.heads_view(proj, "kb"), cfg.heads_view(proj, "vb")
    dqb, dkb, dvb, moved_stick = _stick_bwd(qb, kb, vb, (dyb, 0, LANE), sv["tot"], cfg.Hb, LANE ** -0.5,
                                            f"stick_bwd_{tag}", *cfg.tiles("stick_bwd", l),
                                            carry=_Exchange([[a] for a in ext_stick + [wout_slots]], False))
    got = dict(w_out=moved_stick[-1][0])
    ext_got = [mv[0] for mv in moved_stick[:-1]]
    qc, kc, vc = (sv["q_rot"], 0, 2 * LANE), (sv["k_full"], 0, 2 * LANE), (sv["v_c"], 0, LANE)
    dq_rot, dk_full, dv_c, *moved_mla = _softmax_bwd(qc, kc, vc, (sv["yc"], 0, LANE), (dyc, 0, LANE), sv["lse"], cfg.Hc,
                                                     (LANE + ROPE) ** -0.5, f"mla_bwd_{tag}", *cfg.tiles("mla_bwd", l),
                                                     carry=_Exchange([[a] for a in ext_mla], False) if ext_mla else None)
    ext_got += [mv[0] for mv in (moved_mla[0] if moved_mla else [])]
    trig = [(cos2, LANE, 0), (sin2, LANE, 0)]
    r_rows = [(sv["q_raw"], 2 * C, 0), (sv["kv"], 2 * C, 0), (sv["krr"], LANE, 0)]
    (dq_raw, dkv, dkrr), _, _ = _rowwise_vjp(_f_crope, r_rows, trig, [], [rot],
                                             [(dq_rot, 2 * C, 0), (dk_full, 2 * C, 0), (dv_c, C, 0)], [BF16] * 3, 128,
                                             f"crope_bwd_{tag}")
    dcqn = _matmul(dq_raw, W["uq"], "nn", BF16, f"mm_dcq_{tag}")
    d_wuq = _matmul(dq_raw, sv["cqn"], "tn", BF16, f"mm_dwuq_{tag}")
    dckvn = _matmul(dkv, W["ukv"], "nt", BF16, f"mm_dckv_{tag}")
    d_wukv = _matmul(sv["ckvn"], dkv, "tn", BF16, f"mm_dwukv_{tag}")
    c_rows = [cfg.view(proj, "cq"), cfg.view(proj, "ckv"), cfg.view(proj, "kr")]
    (dcq, dckv, dkr), (dg_q, dg_kv), _ = _rowwise_vjp(
        _f_cpre, c_rows, trig, [p["g_q"], p["g_kv"]], [rot],
        [(dcqn, cfg.Q, 0), (dckvn, cfg.KV, 0), (dkrr, LANE, 0)], [BF16] * 3, 256, f"cpre_bwd_{tag}")
    parts = dict(ua=dua, va=dva, za=dza, qb=dqb, kb=dkb, vb=dvb, zb=dzb, zc=dzc, cq=dcq, kr=dkr, ckv=dckv)
    cols, pos = [], 0
    for nm, off in sorted(cfg.off.items(), key=lambda kv_: kv_[1]):
        if off > pos:
            cols.append(jnp.zeros((S, off - pos), BF16))
        cols.append(parts[nm])
        pos = off + parts[nm].shape[1]
    if cfg.NP > pos:
        cols.append(jnp.zeros((S, cfg.NP - pos), BF16))
    dproj = jnp.concatenate(cols, axis=1)
    d_win = _matmul(dproj, sv["h"], "tn", BF16, f"mm_dwin_{tag}")
    to_send = dict(w_in=_unpad_w_in(cfg, d_win).reshape(NDEV, -1, D),
                   c_w_uq=d_wuq.reshape(cfg.Hc, 2 * LANE, cfg.Q)[:, :LANE + ROPE].reshape(NDEV, -1, cfg.Q),
                   c_w_ukv=_to_slots_cols(d_wukv))
    if last:
        dh, moved = _matmul(dproj, W["in"], "nn", BF16, f"mm_dh_{tag}",
                            carry=_Exchange([[to_send[nm]] for nm in ("w_in", "c_w_uq", "c_w_ukv")], False))
        got.update(w_in=moved[0][0], c_w_uq=moved[1][0], c_w_ukv=moved[2][0])
        to_send = {}
    else:
        dh = _matmul(dproj, W["in"], "nn", BF16, f"mm_dh_{tag}")
    (dx,), (dg_pre,), _ = _rowwise_vjp(_f_pre_res, [(sv["x"], D, 0)], [], [p["g_pre"]], [],
                                       [(dh, D, 0), (dout, D, 0)], [F32], 128, f"pre_bwd_{tag}")
    small = dict(g_pre=dg_pre[0], a_g_v=dg_v.reshape(cfg.G, LANE), a_w_s=dw_s, a_b_s=db_s[:, :, 0], c_g_q=dg_q[0],
                 c_g_kv=dg_kv[0], g_out=jnp.concatenate([dg_oa[0], dg_ob[0], dg_oc[0]]))
    return dx, small, got, to_send, ext_got


def _pack_small(vals):
    pieces = []
    for nm in SMALL:
        piece = vals[nm].reshape(-1, LANE)
        pieces.append(jnp.pad(piece, ((0, -piece.shape[0] % 8), (0, 0))))
    packed = jnp.concatenate(pieces, axis=0)
    return jnp.pad(packed, ((0, -packed.shape[0] % SMALL_ROWS), (0, 0)))


def _unpack_small(packed, like):
    out, row = {}, 0
    for nm in SMALL:
        n = like[nm].size // LANE
        out[nm] = packed[row:row + n].reshape(like[nm].shape)
        row += n + (-n % 8)
    return out


def kernel(x, positions, g_pre, w_in, a_g_v, a_w_s, a_b_s, c_g_q, c_g_kv, c_w_uq, c_w_ukv, g_out, w_out, g_final, loss_target, m_g_pre, m_w_in, m_a_g_v, m_a_w_s, m_a_b_s, m_c_g_q, m_c_g_kv, m_c_w_uq, m_c_w_ukv, m_g_out, m_w_out, m_g_final, v_g_pre, v_w_in, v_a_g_v, v_a_w_s, v_a_b_s, v_c_g_q, v_c_g_kv, v_c_w_uq, v_c_w_ukv, v_g_out, v_w_out, v_g_final):
    depth, S, D = w_in.shape[0], x.shape[1], x.shape[2]
    cfg = _Cfg(S, D, a_g_v.shape[1], c_g_q.shape[1], c_g_kv.shape[1], c_w_ukv.shape[2] * NDEV // (2 * LANE), g_out.shape[1])
    weights = dict(g_pre=g_pre, w_in=w_in, a_g_v=a_g_v, a_w_s=a_w_s, a_b_s=a_b_s, c_g_q=c_g_q, c_g_kv=c_g_kv,
                   c_w_uq=c_w_uq, c_w_ukv=c_w_ukv, g_out=g_out, w_out=w_out, g_final=g_final)
    mom_m = dict(g_pre=m_g_pre, w_in=m_w_in, a_g_v=m_a_g_v, a_w_s=m_a_w_s, a_b_s=m_a_b_s, c_g_q=m_c_g_q, c_g_kv=m_c_g_kv,
                 c_w_uq=m_c_w_uq, c_w_ukv=m_c_w_ukv, g_out=m_g_out, w_out=m_w_out, g_final=m_g_final)
    mom_v = dict(g_pre=v_g_pre, w_in=v_w_in, a_g_v=v_a_g_v, a_w_s=v_a_w_s, a_b_s=v_a_b_s, c_g_q=v_c_g_q, c_g_kv=v_c_g_kv,
                 c_w_uq=v_c_w_uq, c_w_ukv=v_c_w_ukv, g_out=v_g_out, w_out=v_w_out, g_final=v_g_final)
    big_names = ("w_in", "c_w_uq", "c_w_ukv", "w_out")

    inv_freq = 1.0 / (ROPE_THETA ** (jnp.arange(0, ROPE, 2, dtype=F32) / ROPE))
    ang = positions[0].astype(F32)[:, None] * inv_freq
    zpad = jnp.zeros((S, LANE - ROPE), F32)
    cos2 = jnp.concatenate([jnp.cos(ang), jnp.cos(ang), zpad], axis=1)
    sin2 = jnp.concatenate([jnp.sin(ang), jnp.sin(ang), zpad], axis=1)
    rot = _rope_matrix()

    for tree in (weights, mom_m, mom_v):
        for nm in TRANSPOSED:
            tree[nm] = jnp.swapaxes(tree[nm], 1, 2)

    def shards(l, names):
        return [[weights[nm][l].astype(BF16)] for nm in names]

    def assemble(g_in, g_uq, g_ukv, g_wout):
        uq = jnp.pad(g_uq.reshape(cfg.Hc, LANE + ROPE, cfg.Q), ((0, 0), (0, LANE - ROPE), (0, 0))).reshape(2 * cfg.C, cfg.Q)
        return {"in": _pad_w_in(cfg, g_in.reshape(-1, D)), "uq": uq, "ukv": _from_slots_cols(g_ukv),
                "out": _perm_rows_out(cfg, g_wout.reshape(cfg.DMIX, D))}

    params = [_layer_params(cfg, l, g_pre, a_g_v, a_w_s, a_b_s, c_g_q, c_g_kv, g_out) for l in range(depth)]

    gathered = [g[0] for g in _exchange(_GatherTwoLevel(shards(0, big_names)), "gather_weights_l0")]
    hcur, saved, Ws = x[0], [], []
    for l in range(depth):
        Ws.append(assemble(*gathered))
        nxt = l + 1 < depth
        hcur, sv, got_in, got_rest = _layer_fwd(
            cfg, l, hcur, Ws[l], params[l], cos2, sin2, rot,
            carry_stick=_GatherTwoLevel(shards(l + 1, big_names[:1])) if nxt else None,
            carry_mla=_GatherTwoLevel(shards(l + 1, big_names[1:])) if nxt else None)
        saved.append(sv)
        gathered = [g[0] for g in got_in + got_rest]
    (dh,), (dg_final,), (loss_rows,) = _rowwise_vjp(
        _f_final, [(hcur, D, 0)], [(loss_target[0], D, 0)], [g_final[None]], [], [(jnp.ones((S, 1), F32), 1, 0)],
        [F32], 128, "final", primal=[(1, F32)])
    loss = lax.psum(jnp.sum(loss_rows), MESH_AXES)

    small_g, slots, pending = [None] * depth, [None] * depth, {}
    for l in reversed(range(depth)):
        ext_stick = [pending["w_in"]] if pending else []
        ext_mla = [pending["c_w_uq"], pending["c_w_ukv"]] if pending else []
        dh, small_g[l], slots[l], pending, ext_got = _layer_bwd(cfg, l, dh, saved[l], Ws[l], params[l], cos2, sin2, rot,
                                                                ext_stick, ext_mla, l == 0)
        if ext_got:
            slots[l + 1].update(w_in=ext_got[0], c_w_uq=ext_got[1], c_w_ukv=ext_got[2])
    grad_x = dh[None]
    small_grads = {nm: jnp.stack([small_g[l][nm] for l in range(depth)]) for nm in SMALL if nm != "g_final"}
    small_grads["g_final"] = dg_final[0]
    (small_slots,) = _exchange(_Exchange([[_pack_small(small_grads)]], True), "gather_small_grads")

    res = {}
    for nm in big_names:
        res[nm] = None
        for l in range(depth):
            res[nm] = _adamw(slots[l][nm], weights[nm], mom_m[nm], mom_v[nm], l, res[nm], f"adamw_{nm}_l{l}")
        if nm in TRANSPOSED:
            res[nm] = [jnp.swapaxes(r, 1, 2) for r in res[nm]]
    packed = _adamw(small_slots[0], _pack_small(weights)[None], _pack_small(mom_m)[None], _pack_small(mom_v)[None], 0, None,
                    "adamw_small")
    small_res = [_unpack_small(r[0], weights) for r in packed]
    order = ("g_pre", "w_in", "a_g_v", "a_w_s", "a_b_s", "c_g_q", "c_g_kv", "c_w_uq", "c_w_ukv", "g_out", "w_out", "g_final")
    outs = [loss, grad_x]
    for kind in range(4):
        outs += [small_res[kind][nm] if nm in SMALL else res[nm][kind] for nm in order]
    return tuple(outs)
```

```python
import functools

import numpy as np
import jax
import jax.numpy as jnp
from jax import lax
from jax.experimental import pallas as pl
from jax.experimental.pallas import tpu as pltpu

NDEV = 8
MESH_AXES = ("x", "y", "c")
LANE = 128
ROPE = 64
EPS = 1e-6
ROPE_THETA = 10000.0
ADAM_LR, ADAM_B1, ADAM_B2, ADAM_EPS, ADAM_WD, ADAM_STEP = 0.001, 0.9, 0.999, 1e-08, 0.01, 10
VMEM_LIMIT = 48 * 1024 * 1024
ADAM_TILE_BYTES = 768 * 1024
CARRY_MID_PERCENT = 80
SMALL_ROWS = 256
ATTN_TILES = {"stick_fwd": [(1024, 256)], "stick_bwd": [(1024, 256)], "mla_fwd": [(512, 1024)], "mla_bwd": [(1024, 512)]}
DIAG_BLOCK = 256
F32, BF16 = jnp.float32, jnp.bfloat16
SMALL = ("g_pre", "a_g_v", "a_w_s", "a_b_s", "c_g_q", "c_g_kv", "g_out", "g_final")
TRANSPOSED = ("w_in", "c_w_uq")


def _tile(dim, cap, mult=LANE):
    if dim <= cap:
        return dim
    t = (cap // mult) * mult
    while t >= mult:
        if dim % t == 0:
            return t
        t -= mult
    return dim


def _dot_nt(a, b):
    return lax.dot_general(a, b, (((1,), (1,)), ((), ())), preferred_element_type=F32)


def _dot_tn(a, b):
    return lax.dot_general(a, b, (((0,), (0,)), ((), ())), preferred_element_type=F32)


def _dot(a, b):
    return jnp.dot(a, b, preferred_element_type=F32)


class _Exchange:
    def __init__(self, groups, gather):
        self.groups, self.gather = groups, gather
        self.flat = [(gi, li, a) for gi, grp in enumerate(groups) for li, a in enumerate(grp)]
        self.n = len(self.flat)
        self.args = [a for (_, _, a) in self.flat]
        self.out_shape = [jax.ShapeDtypeStruct((len(grp), NDEV) + tuple(grp[0].shape[-2:]), grp[0].dtype) for grp in groups]
        self.scratch = [pltpu.SemaphoreType.DMA((self.n, NDEV - 1)), pltpu.SemaphoreType.DMA((self.n, NDEV - 1)),
                        pltpu.SemaphoreType.DMA((self.n,))]

    def _copies(self, ins, outs, send_sems, recv_sems, local_sems, landings):
        x, y, c = lax.axis_index("x"), lax.axis_index("y"), lax.axis_index("c")
        me = 4 * x + 2 * y + c
        owns = [pltpu.make_async_copy(ins[i] if self.gather else ins[i].at[me], outs[gi].at[li, me], local_sems.at[i])
                for i, (gi, li, _) in enumerate(self.flat)]
        pairs = []
        for k in range(1, NDEV):
            px = 1 - x if k & 4 else x
            py = 1 - y if k & 2 else y
            pc = 1 - c if k & 1 else c
            peer = 4 * px + 2 * py + pc
            for i, (gi, li, _) in enumerate(self.flat):
                src = ins[i] if self.gather else ins[i].at[peer]
                sems = dict(send_sem=send_sems.at[i, k - 1], recv_sem=recv_sems.at[i, k - 1],
                            device_id=(px, py, pc), device_id_type=pl.DeviceIdType.MESH)
                out = pltpu.make_async_remote_copy(src_ref=src, dst_ref=outs[gi].at[li, me], **sems)
                landing = pltpu.make_async_remote_copy(src_ref=src, dst_ref=outs[gi].at[li, peer], **sems) if landings else None
                pairs.append((out, landing))
        return owns, pairs

    def start(self, ins, outs, sems):
        owns, pairs = self._copies(ins, outs, *sems, landings=False)
        for own in owns:
            own.start()
        for out, _ in pairs:
            out.start()

    def mid(self, ins, outs, sems):
        pass

    def wait(self, ins, outs, sems):
        owns, pairs = self._copies(ins, outs, *sems, landings=True)
        for out, landing in pairs:
            out.wait_send()
            landing.wait_recv()
        for own in owns:
            own.wait()


class _GatherTwoLevel(_Exchange):
    def __init__(self, groups):
        super().__init__(groups, True)

    def _copy(self, i, k, ins, outs, send_sems, recv_sems, landing):
        gi, li, _ = self.flat[i]
        x, y, c = lax.axis_index("x"), lax.axis_index("y"), lax.axis_index("c")
        chips = [(x, y), (1 - x, y), (x, 1 - y), (1 - x, 1 - y)]

        def slot(chip, core):
            return outs[gi].at[li, 4 * chip[0] + 2 * chip[1] + core]

        if k == 0:
            to, src, dst, lands = (x, y, 1 - c), ins[i], slot(chips[0], c), slot(chips[0], 1 - c)
        elif k <= 3:
            to, src, dst, lands = (*chips[k], c), ins[i], slot(chips[0], c), slot(chips[k], c)
        else:
            to, src, dst, lands = (x, y, 1 - c), slot(chips[k - 3], c), slot(chips[k - 3], c), slot(chips[k - 3], 1 - c)
        return pltpu.make_async_remote_copy(src_ref=src, dst_ref=lands if landing else dst, send_sem=send_sems.at[i, k],
                                            recv_sem=recv_sems.at[i, k], device_id=to, device_id_type=pl.DeviceIdType.MESH)

    def _own(self, i, ins, outs, local_sems):
        gi, li, _ = self.flat[i]
        me = 4 * lax.axis_index("x") + 2 * lax.axis_index("y") + lax.axis_index("c")
        return pltpu.make_async_copy(ins[i], outs[gi].at[li, me], local_sems.at[i])

    def start(self, ins, outs, sems):
        send_sems, recv_sems, local_sems = sems
        for i in range(self.n):
            self._own(i, ins, outs, local_sems).start()
        for k in range(4):
            for i in range(self.n):
                self._copy(i, k, ins, outs, send_sems, recv_sems, False).start()

    def mid(self, ins, outs, sems):
        send_sems, recv_sems, _ = sems
        for k in range(1, 4):
            for i in range(self.n):
                self._copy(i, k, ins, outs, send_sems, recv_sems, True).wait_recv()
                self._copy(i, k + 3, ins, outs, send_sems, recv_sems, False).start()

    def wait(self, ins, outs, sems):
        send_sems, recv_sems, local_sems = sems
        for k in (0, 4, 5, 6):
            for i in range(self.n):
                self._copy(i, k, ins, outs, send_sems, recv_sems, True).wait_recv()
        for k in range(NDEV - 1):
            for i in range(self.n):
                self._copy(i, k, ins, outs, send_sems, recv_sems, False).wait_send()
        for i in range(self.n):
            self._own(i, ins, outs, local_sems).wait()


def _call(body, name, grid, in_specs, out_specs, out_shape, scratch, semantics, args, carry=None):
    n_in, n_out, n_scr = len(in_specs), len(out_specs), len(scratch)
    if carry is None:
        run = body
    else:
        semantics = ("arbitrary",) * len(grid)
        anyspec = pl.BlockSpec(memory_space=pl.ANY)
        in_specs = list(in_specs) + [anyspec] * carry.n
        out_specs = list(out_specs) + [anyspec] * len(carry.groups)
        out_shape = list(out_shape) + carry.out_shape
        scratch = list(scratch) + carry.scratch
        args = list(args) + carry.args

        def run(*refs):
            c_in, x_in = refs[:n_in], refs[n_in:n_in + carry.n]
            rest = refs[n_in + carry.n:]
            c_out, x_out = rest[:n_out], rest[n_out:n_out + len(carry.groups)]
            c_scr, sems = rest[n_out + len(carry.groups):len(rest) - 3], rest[len(rest) - 3:]
            step, total = 0, 1
            for d, extent in enumerate(grid):
                step = step * extent + pl.program_id(d)
                total *= extent

            @pl.when(step == 0)
            def _():
                carry.start(x_in, x_out, sems)

            body(*c_in, *c_out, *c_scr)

            @pl.when(step == (total * CARRY_MID_PERCENT) // 100)
            def _():
                carry.mid(x_in, x_out, sems)

            @pl.when(step == total - 1)
            def _():
                carry.wait(x_in, x_out, sems)

    res = pl.pallas_call(
        run, name=name, grid=grid, out_shape=list(out_shape), in_specs=list(in_specs), out_specs=list(out_specs),
        scratch_shapes=list(scratch),
        compiler_params=pltpu.CompilerParams(dimension_semantics=semantics, vmem_limit_bytes=VMEM_LIMIT,
                                             has_side_effects=carry is not None),
    )(*args)
    return list(res[:n_out]), list(res[n_out:])


def _exchange(ex, name):
    groups = ex.groups

    def body(*refs):
        ins, outs, sems = refs[:ex.n], refs[ex.n:ex.n + len(groups)], refs[ex.n + len(groups):]
        ex.start(ins, outs, sems)
        ex.mid(ins, outs, sems)
        ex.wait(ins, outs, sems)

    anyspec = pl.BlockSpec(memory_space=pl.ANY)
    return pl.pallas_call(
        body, name=name, out_shape=ex.out_shape, in_specs=[anyspec] * ex.n, out_specs=[anyspec] * len(groups),
        scratch_shapes=ex.scratch, compiler_params=pltpu.CompilerParams(has_side_effects=True),
    )(*ex.args)


def _matmul(a, b, mode, out_dtype, name, add=None, tm=1024, tn=1024, tk=1024, carry=None):
    if mode == "tn":
        (K, M), (K2, N) = a.shape, b.shape
    elif mode == "nt":
        (M, K), (N, K2) = a.shape, b.shape
    else:
        (M, K), (K2, N) = a.shape, b.shape
    assert K == K2, (a.shape, b.shape, mode)
    tm, tn, tk = _tile(M, tm), _tile(N, tn), _tile(K, tk)
    nk = K // tk
    a_spec = pl.BlockSpec((tk, tm), lambda i, j, k: (k, i)) if mode == "tn" else pl.BlockSpec((tm, tk), lambda i, j, k: (i, k))
    b_spec = pl.BlockSpec((tn, tk), lambda i, j, k: (j, k)) if mode == "nt" else pl.BlockSpec((tk, tn), lambda i, j, k: (k, j))
    dot = {"nn": _dot, "nt": _dot_nt, "tn": _dot_tn}[mode]
    has_add = add is not None

    def body(*refs):
        a_ref, b_ref = refs[0], refs[1]
        o_ref, acc = refs[-2], refs[-1]
        k = pl.program_id(2)

        @pl.when(k == 0)
        def _():
            acc[...] = jnp.zeros_like(acc)

        acc[...] += dot(a_ref[...].astype(BF16), b_ref[...].astype(BF16))

        @pl.when(k == nk - 1)
        def _():
            r = acc[...]
            if has_add:
                r = r + refs[2][...]
            o_ref[...] = r.astype(o_ref.dtype)

    in_specs = [a_spec, b_spec]
    args = [a, b]
    if has_add:
        in_specs.append(pl.BlockSpec((tm, tn), lambda i, j, k: (i, j)))
        args.append(add)
    (out,), moved = _call(body, name, (M // tm, N // tn, nk), in_specs, [pl.BlockSpec((tm, tn), lambda i, j, k: (i, j))],
                          [jax.ShapeDtypeStruct((M, N), out_dtype)], [pltpu.VMEM((tm, tn), F32)],
                          ("parallel", "parallel", "arbitrary"), args, carry)
    return out if carry is None else (out, moved)


def _row_specs(views, tile):
    return [pl.BlockSpec((tile, w), functools.partial(lambda i, cb: (i, cb), cb=cb)) for (_, w, cb) in views]


def _full_specs(arrs):
    return [pl.BlockSpec(p.shape, functools.partial(lambda i, nd: (0,) * nd, nd=p.ndim)) for p in arrs]


def _rowwise(fn, rows, aux, params, consts, outs, tile, name):
    S = rows[0][0].shape[0]
    nr, na, npar, nc = len(rows), len(aux), len(params), len(consts)

    def body(*refs):
        ins = [r[...].astype(F32) for r in refs[:nr + na]]
        small = [r[...] for r in refs[nr + na:nr + na + npar + nc]]
        res = fn(*ins, *small)
        for o_ref, r in zip(refs[nr + na + npar + nc:], res):
            o_ref[...] = r.astype(o_ref.dtype)

    return pl.pallas_call(
        body, name=name, grid=(S // tile,),
        out_shape=[jax.ShapeDtypeStruct((S, w), dt) for (w, dt) in outs],
        in_specs=_row_specs(rows + aux, tile) + _full_specs(params + consts),
        out_specs=[pl.BlockSpec((tile, w), lambda i: (i, 0)) for (w, _) in outs],
        compiler_params=pltpu.CompilerParams(dimension_semantics=("parallel",), vmem_limit_bytes=VMEM_LIMIT),
    )(*[v[0] for v in rows + aux], *params, *consts)


def _rowwise_vjp(fn, rows, aux, params, consts, cots, grad_dtypes, tile, name, primal=()):
    S = rows[0][0].shape[0]
    nr, na, npar, nc, nct, npr = len(rows), len(aux), len(params), len(consts), len(cots), len(primal)

    def body(*refs):
        n_in = nr + na + npar + nc + nct
        rv = [r[...].astype(F32) for r in refs[:nr]]
        av = [r[...].astype(F32) for r in refs[nr:nr + na]]
        pv = [r[...] for r in refs[nr + na:nr + na + npar]]
        cv = [r[...] for r in refs[nr + na + npar:nr + na + npar + nc]]
        ct = tuple(r[...].astype(F32) for r in refs[nr + na + npar + nc:n_in])
        res, vjp = jax.vjp(lambda *rp: tuple(fn(*rp[:nr], *av, *rp[nr:], *cv)), *rv, *pv)
        grads = vjp(ct)
        g_refs = refs[n_in:n_in + nr]
        p_refs = refs[n_in + nr:n_in + nr + npar]
        o_refs = refs[n_in + nr + npar:]
        for g_ref, g in zip(g_refs, grads[:nr]):
            g_ref[...] = g.astype(g_ref.dtype)

        @pl.when(pl.program_id(0) == 0)
        def _():
            for p_ref in p_refs:
                p_ref[...] = jnp.zeros_like(p_ref)

        for p_ref, g in zip(p_refs, grads[nr:]):
            p_ref[...] += g
        for o_ref, r in zip(o_refs, res[:npr]):
            o_ref[...] = r.astype(o_ref.dtype)

    out_shape = ([jax.ShapeDtypeStruct((S, w), dt) for (_, w, _), dt in zip(rows, grad_dtypes)]
                 + [jax.ShapeDtypeStruct(p.shape, F32) for p in params]
                 + [jax.ShapeDtypeStruct((S, w), dt) for (w, dt) in primal])
    out_specs = ([pl.BlockSpec((tile, w), lambda i: (i, 0)) for (_, w, _) in rows] + _full_specs(params)
                 + [pl.BlockSpec((tile, w), lambda i: (i, 0)) for (w, _) in primal])
    res = pl.pallas_call(
        body, name=name, grid=(S // tile,), out_shape=out_shape,
        in_specs=_row_specs(rows + aux, tile) + _full_specs(params + consts) + _row_specs(cots, tile),
        out_specs=out_specs,
        compiler_params=pltpu.CompilerParams(dimension_semantics=("arbitrary",), vmem_limit_bytes=VMEM_LIMIT),
    )(*[v[0] for v in rows + aux], *params, *consts, *[v[0] for v in cots])
    return res[:nr], res[nr:nr + npar], res[nr + npar:]


@jax.custom_vjp
def _mm(a, b):
    return _dot(a.astype(BF16), b.astype(BF16))


def _mm_fwd(a, b):
    return _mm(a, b), (a, b)


def _mm_bwd(res, ct):
    a, b = res
    ctb = ct.astype(BF16)
    return _dot_nt(ctb, b.astype(BF16)), _dot_tn(a.astype(BF16), ctb)


_mm.defvjp(_mm_fwd, _mm_bwd)


def _rms(x, g):
    return x * lax.rsqrt(jnp.mean(x * x, axis=-1, keepdims=True) + EPS) * g


def _f_pre(x, g):
    return (_rms(x, g),)


def _f_pre_res(x, g):
    return _rms(x, g), x


def _f_gate(y, z, g):
    return (_rms(y, g) * jax.nn.silu(z),)


def _f_gmlp(u, v, z, g_v, w_s, b_s, g_o):
    groups = w_s.shape[0]
    u, v = jax.nn.gelu(u), jax.nn.gelu(v)
    t_idx = lax.broadcasted_iota(jnp.int32, (LANE, LANE), 0)
    s_idx = lax.broadcasted_iota(jnp.int32, (LANE, LANE), 1)
    ys = []
    for g in range(groups):
        sl = slice(g * LANE, (g + 1) * LANE)
        vn = _rms(v[:, sl], g_v[:, sl])
        w = jnp.where(s_idx <= t_idx, w_s[g], 0.0)
        ys.append(u[:, sl] * (_mm(w, vn) + b_s[g]))
    return (_rms(jnp.concatenate(ys, axis=1), g_o) * jax.nn.silu(z),)


def _rope(x, cos2, sin2, rot):
    return x * cos2 + _mm(x, rot) * sin2


def _f_cpre(cq, ckv, kr, cos2, sin2, g_q, g_kv, rot):
    return _rms(cq, g_q), _rms(ckv, g_kv), _rope(kr, cos2, sin2, rot)


def _f_crope(q, kv, krr, cos2, sin2, rot):
    heads = q.shape[1] // (2 * LANE)
    qs, ks, vs = [], [], []
    for h in range(heads):
        lo, mid, hi = 2 * h * LANE, (2 * h + 1) * LANE, (2 * h + 2) * LANE
        qs += [q[:, lo:mid], _rope(q[:, mid:hi], cos2, sin2, rot)]
        ks += [kv[:, lo:mid], krr]
        vs += [kv[:, mid:hi]]
    return jnp.concatenate(qs, axis=1), jnp.concatenate(ks, axis=1), jnp.concatenate(vs, axis=1)


def _f_final(h, target, g):
    err = _rms(h, g) - target
    return (0.5 * jnp.mean(err * err, axis=-1, keepdims=True),)


def _rope_matrix():
    r = np.zeros((LANE, LANE), np.float32)
    half = ROPE // 2
    for i in range(half):
        r[i + half, i] = -1.0
        r[i, i + half] = 1.0
    return jnp.asarray(r)


def _head_spec(view, rows, n_rows_block):
    _, cb0, w = view
    if n_rows_block:
        return pl.BlockSpec((rows, w), functools.partial(lambda h, i, cb0: (i, cb0 + h), cb0=cb0))
    return pl.BlockSpec((rows, w), functools.partial(lambda h, i, cb0: (0, cb0 + h), cb0=cb0))


def _stat_spec(tq):
    return pl.BlockSpec((1, tq, 1), lambda h, i: (h, i, 0))


def _softplus(z):
    return jnp.maximum(z, 0.0) + jnp.log(1.0 + jnp.exp(-jnp.abs(z)))


def _cumsum_mm(x, m01):
    hi = x.astype(BF16)
    lo = (x - hi.astype(F32)).astype(BF16)
    return _dot(hi, m01) + _dot(lo, m01)


def _attn_call(body, name, heads, S, tq, ins, in_blocked, outs, out_blocked, scratch, stats_in=0, stats_out=0, carry=None):
    in_specs = [_head_spec(v, tq if blk else S, blk) for v, blk in zip(ins[:len(ins) - stats_in], in_blocked)]
    in_specs += [_stat_spec(tq)] * stats_in
    out_specs = [_head_spec((None, 0, w), tq if blk else S, blk) for (w, _), blk in zip(outs, out_blocked)]
    out_specs += [_stat_spec(tq)] * stats_out
    out_shape = [jax.ShapeDtypeStruct((S, heads * w), dt) for (w, dt) in outs]
    out_shape += [jax.ShapeDtypeStruct((heads, S, 1), F32)] * stats_out
    args = [v[0] for v in ins[:len(ins) - stats_in]] + list(ins[len(ins) - stats_in:])
    res, moved = _call(body, name, (heads, S // tq), in_specs, out_specs, out_shape, scratch, ("arbitrary", "arbitrary"),
                       args, carry)
    return res if carry is None else res + [moved]


def _softmax_fwd(q, k, v, heads, scale, name, tq, bk, carry=None):
    S, dv = q[0].shape[0], v[2]

    def body(q_ref, k_ref, v_ref, o_ref, lse_ref):
        qi = pl.program_id(1)
        qv = q_ref[...]
        row = qi * tq + lax.broadcasted_iota(jnp.int32, (tq, bk), 0)
        col0 = lax.broadcasted_iota(jnp.int32, (tq, bk), 1)

        def step(kb, carry):
            m, l, acc = carry
            sl = pl.ds(pl.multiple_of(kb * bk, bk), bk)
            s = _dot_nt(qv, k_ref[sl, :]) * scale
            s = jnp.where(kb * bk + col0 <= row, s, -1e30)
            m_new = jnp.maximum(m, jnp.max(s, axis=1, keepdims=True))
            p = jnp.exp(s - m_new)
            alpha = jnp.exp(m - m_new)
            l = alpha * l + jnp.sum(p, axis=1, keepdims=True)
            acc = alpha * acc + _dot(p.astype(BF16), v_ref[sl, :])
            return m_new, l, acc

        n_kb = (qi * tq + tq + bk - 1) // bk
        m, l, acc = lax.fori_loop(0, n_kb, step, (jnp.full((tq, 1), -1e30, F32), jnp.zeros((tq, 1), F32),
                                                  jnp.zeros((tq, dv), F32)))
        o_ref[...] = (acc / l).astype(o_ref.dtype)
        lse_ref[0] = m + jnp.log(l)

    return _attn_call(body, name, heads, S, tq, [q, k, v], [1, 0, 0], [(dv, BF16)], [1], [], stats_out=1, carry=carry)


def _softmax_bwd(q, k, v, o, do, lse, heads, scale, name, tq, bk, carry=None):
    S, dq_w, dv = q[0].shape[0], q[2], v[2]
    nq = S // tq

    bd = min(DIAG_BLOCK, tq)
    assert tq % bk == 0 and tq % bd == 0

    def body(q_ref, k_ref, v_ref, o_ref, do_ref, lse_ref, dq_ref, dk_ref, dv_ref, dk_acc, dv_acc, delta_scr, dq_scr):
        qi = pl.program_id(1)

        @pl.when(qi == 0)
        def _():
            dk_acc[...] = jnp.zeros_like(dk_acc)
            dv_acc[...] = jnp.zeros_like(dv_acc)

        delta_scr[...] = jnp.sum(do_ref[...].astype(F32) * o_ref[...].astype(F32), axis=1, keepdims=True)
        dq_scr[...] = jnp.zeros_like(dq_scr)

        def block(r0, sl, width, masked):
            qv, dov = q_ref[r0:, :], do_ref[r0:, :]
            ks, vs = k_ref[sl, :], v_ref[sl, :]
            p = jnp.exp(_dot_nt(qv, ks) * scale - lse_ref[0, r0:, :])
            if masked:
                shape = (tq - r0, width)
                p = jnp.where(lax.broadcasted_iota(jnp.int32, shape, 1) <= lax.broadcasted_iota(jnp.int32, shape, 0), p, 0.0)
            ds = (p * (_dot_nt(dov, vs) - delta_scr[r0:, :]) * scale).astype(BF16)
            dk_acc[sl, :] += _dot_tn(ds, qv)
            dv_acc[sl, :] += _dot_tn(p.astype(BF16), dov)
            dq_scr[r0:, :] += _dot(ds, ks)

        def step(kb, _):
            block(0, pl.ds(pl.multiple_of(kb * bk, bk), bk), bk, False)
            return 0

        lax.fori_loop(0, qi * (tq // bk), step, 0)
        for j in range(tq // bd):
            block(j * bd, pl.ds(pl.multiple_of(qi * tq + j * bd, bd), bd), bd, True)
        dq_ref[...] = dq_scr[...].astype(dq_ref.dtype)

        @pl.when(qi == nq - 1)
        def _():
            dk_ref[...] = dk_acc[...].astype(dk_ref.dtype)
            dv_ref[...] = dv_acc[...].astype(dv_ref.dtype)

    return _attn_call(body, name, heads, S, tq, [q, k, v, o, do, lse], [1, 0, 0, 1, 1],
                      [(dq_w, BF16), (dq_w, BF16), (dv, BF16)], [1, 0, 0],
                      [pltpu.VMEM((S, dq_w), F32), pltpu.VMEM((S, dv), F32), pltpu.VMEM((tq, 1), F32),
                       pltpu.VMEM((tq, dq_w), F32)], stats_in=1, carry=carry)


def _stick_fwd(q, k, v, heads, scale, name, tq, bk, carry=None):
    S, dv = q[0].shape[0], v[2]

    assert tq % bk == 0
    n_sub = tq // bk

    def body(q_ref, k_ref, v_ref, o_ref, tot_ref, c_scr, acc_scr):
        qi = pl.program_id(1)
        m_gt = (lax.broadcasted_iota(jnp.int32, (bk, bk), 0) > lax.broadcasted_iota(jnp.int32, (bk, bk), 1)).astype(BF16)

        def block(r0, sl, masked):
            rows = tq - r0
            z = _dot_nt(q_ref[r0:, :], k_ref[sl, :]) * scale
            sp = _softplus(z)
            lk = -sp
            if masked:
                mask = lax.broadcasted_iota(jnp.int32, (rows, bk), 1) < lax.broadcasted_iota(jnp.int32, (rows, bk), 0)
                lk = jnp.where(mask, lk, 0.0)
            after = _cumsum_mm(lk, m_gt) + c_scr[r0:, :]
            a = jnp.exp(z - sp + after)
            if masked:
                a = jnp.where(mask, a, 0.0)
            acc_scr[r0:, :] += _dot(a.astype(BF16), v_ref[sl, :])
            c_scr[r0:, :] += jnp.sum(lk, axis=1, keepdims=True)

        c_scr[...] = jnp.zeros_like(c_scr)
        acc_scr[...] = jnp.zeros_like(acc_scr)
        for j in reversed(range(n_sub)):
            block(j * bk, pl.ds(pl.multiple_of(qi * tq + j * bk, bk), bk), True)

        def step(it, _):
            block(0, pl.ds(pl.multiple_of((qi * n_sub - 1 - it) * bk, bk), bk), False)
            return 0

        lax.fori_loop(0, qi * n_sub, step, 0)
        o_ref[...] = acc_scr[...].astype(o_ref.dtype)
        tot_ref[0] = c_scr[...]

    return _attn_call(body, name, heads, S, tq, [q, k, v], [1, 0, 0], [(dv, BF16)], [1],
                      [pltpu.VMEM((tq, 1), F32), pltpu.VMEM((tq, dv), F32)], stats_out=1, carry=carry)


def _stick_bwd(q, k, v, do, tot, heads, scale, name, tq, bk, carry=None):
    S, dq_w, dv = q[0].shape[0], q[2], v[2]
    nq = S // tq

    assert tq % bk == 0
    n_sub = tq // bk

    def body(q_ref, k_ref, v_ref, do_ref, tot_ref, dq_ref, dk_ref, dv_ref, dk_acc, dv_acc, pc_scr, gc_scr, dq_scr):
        qi = pl.program_id(1)

        @pl.when(qi == 0)
        def _():
            dk_acc[...] = jnp.zeros_like(dk_acc)
            dv_acc[...] = jnp.zeros_like(dv_acc)

        j_idx = lax.broadcasted_iota(jnp.int32, (bk, bk), 0)
        s_idx = lax.broadcasted_iota(jnp.int32, (bk, bk), 1)
        m_le, m_lt = (j_idx <= s_idx).astype(BF16), (j_idx < s_idx).astype(BF16)

        def block(r0, sl, masked):
            rows = tq - r0
            qv, dov = q_ref[r0:, :], do_ref[r0:, :]
            ks, vs = k_ref[sl, :], v_ref[sl, :]
            z = _dot_nt(qv, ks) * scale
            sp = _softplus(z)
            lk = -sp
            if masked:
                mask = lax.broadcasted_iota(jnp.int32, (rows, bk), 1) < lax.broadcasted_iota(jnp.int32, (rows, bk), 0)
                lk = jnp.where(mask, lk, 0.0)
            after = tot_ref[0, r0:, :] - pc_scr[r0:, :] - _cumsum_mm(lk, m_le)
            log_beta = z - sp
            a = jnp.exp(log_beta + after)
            if masked:
                a = jnp.where(mask, a, 0.0)
            g = _dot_nt(dov, vs) * a
            cg = gc_scr[r0:, :] + _cumsum_mm(g, m_lt)
            dz = g * jnp.exp(-sp) - jnp.exp(log_beta) * cg
            if masked:
                dz = jnp.where(mask, dz, 0.0)
            dz = (dz * scale).astype(BF16)
            dk_acc[sl, :] += _dot_tn(dz, qv)
            dv_acc[sl, :] += _dot_tn(a.astype(BF16), dov)
            dq_scr[r0:, :] += _dot(dz, ks)
            pc_scr[r0:, :] += jnp.sum(lk, axis=1, keepdims=True)
            gc_scr[r0:, :] += jnp.sum(g, axis=1, keepdims=True)

        pc_scr[...] = jnp.zeros_like(pc_scr)
        gc_scr[...] = jnp.zeros_like(gc_scr)
        dq_scr[...] = jnp.zeros_like(dq_scr)

        def step(kb, _):
            block(0, pl.ds(pl.multiple_of(kb * bk, bk), bk), False)
            return 0

        lax.fori_loop(0, qi * n_sub, step, 0)
        for j in range(n_sub):
            block(j * bk, pl.ds(pl.multiple_of(qi * tq + j * bk, bk), bk), True)
        dq_ref[...] = dq_scr[...].astype(dq_ref.dtype)

        @pl.when(qi == nq - 1)
        def _():
            dk_ref[...] = dk_acc[...].astype(dk_ref.dtype)
            dv_ref[...] = dv_acc[...].astype(dv_ref.dtype)

    return _attn_call(body, name, heads, S, tq, [q, k, v, do, tot], [1, 0, 0, 1],
                      [(dq_w, BF16), (dq_w, BF16), (dv, BF16)], [1, 0, 0],
                      [pltpu.VMEM((S, dq_w), F32), pltpu.VMEM((S, dv), F32), pltpu.VMEM((tq, 1), F32),
                       pltpu.VMEM((tq, 1), F32), pltpu.VMEM((tq, dq_w), F32)], stats_in=1, carry=carry)


def _adamw(slots, w, m, v, layer, prev, name, col0=0):
    _, R, C = slots.shape
    L, full_c = w.shape[0], w.shape[2]
    item = slots.dtype.itemsize
    tc = _tile(C, 2048)
    tr = _tile(R, max(16, ADAM_TILE_BYTES // (item * tc)), mult=16)
    if tr == R and R * tc * item > ADAM_TILE_BYTES:
        tc = _tile(C, max(LANE, ADAM_TILE_BYTES // (item * R)))
    c1, c2 = 1.0 - ADAM_B1 ** ADAM_STEP, 1.0 - ADAM_B2 ** ADAM_STEP
    n_prev = 0 if prev is None else 4

    def body(s_ref, w_ref, m_ref, v_ref, *rest):
        g_out, d_out, m_out, v_out = rest[n_prev:]
        g = s_ref[0].astype(F32)
        for k in range(1, NDEV):
            g = g + s_ref[k].astype(F32)
        m_new = ADAM_B1 * m_ref[0] + (1.0 - ADAM_B1) * g
        v_new = ADAM_B2 * v_ref[0] + (1.0 - ADAM_B2) * (g * g)
        g_out[0] = g
        m_out[0] = m_new
        v_out[0] = v_new
        d_out[0] = -ADAM_LR * ((m_new / c1) / (jnp.sqrt(v_new / c2) + ADAM_EPS) + ADAM_WD * w_ref[0])

    assert col0 % tc == 0
    spec = pl.BlockSpec((1, tr, tc), lambda i, j: (layer, i, j + col0 // tc))
    in_specs = [pl.BlockSpec((NDEV, tr, tc), lambda i, j: (0, i, j)), spec, spec, spec]
    in_specs += [pl.BlockSpec(memory_space=pl.ANY)] * n_prev
    return pl.pallas_call(
        body, name=name, grid=(R // tr, C // tc), out_shape=[jax.ShapeDtypeStruct((L, R, full_c), F32)] * 4,
        in_specs=in_specs, out_specs=[spec] * 4, input_output_aliases={4 + i: i for i in range(n_prev)},
        compiler_params=pltpu.CompilerParams(dimension_semantics=("parallel", "parallel"), vmem_limit_bytes=VMEM_LIMIT),
    )(slots, w, m, v, *(prev or []))


class _Cfg:
    def __init__(self, S, D, groups, q_lora, kv_lora, c_heads, d_mix):
        self.S, self.D, self.G, self.Q, self.KV, self.Hc, self.DMIX = S, D, groups, q_lora, kv_lora, c_heads, d_mix
        self.A, self.C = groups * LANE, c_heads * LANE
        self.B = d_mix - self.A - self.C
        self.Hb = self.B // LANE
        A, B, C = self.A, self.B, self.C
        assert B % LANE == 0 and B % C == 0 and (B + C) % A == 0
        self.ref_segs = [("ua", A), ("va", A), ("za", A), ("qb", B), ("kb", B), ("vb", B), ("zb", B),
                         ("cq", q_lora), ("ckv", kv_lora), ("kr", ROPE), ("zc", C)]
        self.off, off = {}, 0
        for nm, w in [("ua", A), ("va", A), ("za", A), ("qb", B), ("kb", B), ("vb", B), ("zb", B), ("zc", C),
                      ("cq", q_lora), ("kr", LANE), ("ckv", kv_lora)]:
            off = -(-off // w) * w
            self.off[nm] = off
            off += w
        self.NP = -(-off // 512) * 512
        self.width = {"kr": LANE, **{nm: w for nm, w in self.ref_segs if nm != "kr"}}

    def tiles(self, kind, layer):
        tq, bk = ATTN_TILES[kind][layer % len(ATTN_TILES[kind])]
        return min(tq, self.S), min(bk, self.S)

    def view(self, arr, nm):
        w = self.width[nm]
        return (arr, w, self.off[nm] // w)

    def heads_view(self, arr, nm):
        return (arr, self.off[nm] // LANE, LANE)


def _pad_w_in(cfg, wt):
    pieces, start = {}, 0
    for nm, width in cfg.ref_segs:
        pieces[nm] = wt[start:start + width]
        start += width
    rows, pos = [], 0
    for nm, off in sorted(cfg.off.items(), key=lambda kv: kv[1]):
        if off > pos:
            rows.append(jnp.zeros((off - pos, wt.shape[1]), wt.dtype))
        rows.append(pieces[nm])
        pos = off + pieces[nm].shape[0]
    if cfg.NP > pos:
        rows.append(jnp.zeros((cfg.NP - pos, wt.shape[1]), wt.dtype))
    return jnp.concatenate(rows, axis=0)


def _unpad_w_in(cfg, wpt):
    return jnp.concatenate([wpt[cfg.off[nm]:cfg.off[nm] + width] for nm, width in cfg.ref_segs], axis=0)


def _to_slots_cols(w):
    R = w.shape[0]
    return w.reshape(R, NDEV, -1).transpose(1, 0, 2)


def _from_slots_cols(s):
    return s.transpose(1, 0, 2).reshape(s.shape[1], -1)


def _perm_rows_out(cfg, w):
    return jnp.concatenate([w[cfg.A:], w[:cfg.A]], axis=0)


def _unperm_rows_out(cfg, w):
    return jnp.concatenate([w[cfg.B + cfg.C:], w[:cfg.B + cfg.C]], axis=0)


def _layer_params(cfg, l, g_pre, a_g_v, a_w_s, a_b_s, c_g_q, c_g_kv, g_out):
    A, B = cfg.A, cfg.B
    return dict(g_pre=g_pre[l][None], g_v=a_g_v[l].reshape(1, A), w_s=a_w_s[l], b_s=a_b_s[l][:, :, None],
                g_q=c_g_q[l][None], g_kv=c_g_kv[l][None],
                g_oa=g_out[l][None, :A], g_ob=g_out[l][None, A:A + B], g_oc=g_out[l][None, A + B:])


def _layer_fwd(cfg, l, x, W, p, cos2, sin2, rot, carry_stick=None, carry_mla=None):
    S, D, A, B, C = cfg.S, cfg.D, cfg.A, cfg.B, cfg.C
    tag = f"l{l}"
    (h,) = _rowwise(_f_pre, [(x, D, 0)], [], [p["g_pre"]], [], [(D, BF16)], 256, f"pre_{tag}")
    proj = _matmul(h, W["in"], "nt", BF16, f"mm_in_{tag}")
    a_rows = [cfg.view(proj, "ua"), cfg.view(proj, "va"), cfg.view(proj, "za")]
    a_par = [p["g_v"], p["w_s"], p["b_s"], p["g_oa"]]
    (ya,) = _rowwise(_f_gmlp, a_rows, [], a_par, [], [(A, BF16)], LANE, f"gmlp_{tag}")
    qb, kb, vb = cfg.heads_view(proj, "qb"), cfg.heads_view(proj, "kb"), cfg.heads_view(proj, "vb")
    yb, tot, *moved_stick = _stick_fwd(qb, kb, vb, cfg.Hb, LANE ** -0.5, f"stick_fwd_{tag}", *cfg.tiles("stick_fwd", l),
                                       carry=carry_stick)
    (ybg,) = _rowwise(_f_gate, [(yb, B, 0), cfg.view(proj, "zb")], [], [p["g_ob"]], [], [(B, BF16)], 256, f"gate_b_{tag}")
    c_rows = [cfg.view(proj, "cq"), cfg.view(proj, "ckv"), cfg.view(proj, "kr")]
    trig = [(cos2, LANE, 0), (sin2, LANE, 0)]
    cqn, ckvn, krr = _rowwise(_f_cpre, c_rows, trig, [p["g_q"], p["g_kv"]], [rot],
                              [(cfg.Q, BF16), (cfg.KV, BF16), (LANE, BF16)], 256, f"cpre_{tag}")
    q_raw = _matmul(cqn, W["uq"], "nt", BF16, f"mm_uq_{tag}")
    kv = _matmul(ckvn, W["ukv"], "nn", BF16, f"mm_ukv_{tag}")
    r_rows = [(q_raw, 2 * C, 0), (kv, 2 * C, 0), (krr, LANE, 0)]
    q_rot, k_full, v_c = _rowwise(_f_crope, r_rows, trig, [], [rot], [(2 * C, BF16), (2 * C, BF16), (C, BF16)], 128,
                                  f"crope_{tag}")
    qc, kc, vc = (q_rot, 0, 2 * LANE), (k_full, 0, 2 * LANE), (v_c, 0, LANE)
    yc, lse, *moved_mla = _softmax_fwd(qc, kc, vc, cfg.Hc, (LANE + ROPE) ** -0.5, f"mla_fwd_{tag}", *cfg.tiles("mla_fwd", l),
                                       carry=carry_mla)
    (ycg,) = _rowwise(_f_gate, [(yc, C, 0), cfg.view(proj, "zc")], [], [p["g_oc"]], [], [(C, BF16)], 256, f"gate_c_{tag}")
    y = jnp.concatenate([ybg, ycg, ya], axis=1)
    out = _matmul(y, W["out"], "nn", F32, f"mm_out_{tag}", add=x)
    saved = dict(x=x, h=h, proj=proj, yb=yb, tot=tot, cqn=cqn, ckvn=ckvn, krr=krr, q_raw=q_raw, kv=kv,
                 q_rot=q_rot, k_full=k_full, v_c=v_c, yc=yc, lse=lse, y=y)
    return out, saved, (moved_stick[0] if moved_stick else []), (moved_mla[0] if moved_mla else [])


def _layer_bwd(cfg, l, dout, sv, W, p, cos2, sin2, rot, ext_stick, ext_mla, last):
    S, D, A, B, C = cfg.S, cfg.D, cfg.A, cfg.B, cfg.C
    tag = f"l{l}"
    proj = sv["proj"]
    dy = _matmul(dout, W["out"], "nt", BF16, f"mm_dy_{tag}")
    d_wout = _matmul(sv["y"], dout, "tn", BF16, f"mm_dwout_{tag}")
    wout_slots = _unperm_rows_out(cfg, d_wout).reshape(NDEV, cfg.DMIX // NDEV, D)
    (dyb, dzb), (dg_ob,), _ = _rowwise_vjp(_f_gate, [(sv["yb"], B, 0), cfg.view(proj, "zb")], [], [p["g_ob"]], [],
                                           [(dy, B, 0)], [BF16, BF16], 256, f"gate_b_bwd_{tag}")
    (dyc, dzc), (dg_oc,), _ = _rowwise_vjp(_f_gate, [(sv["yc"], C, 0), cfg.view(proj, "zc")], [], [p["g_oc"]], [],
                                           [(dy, C, B // C)], [BF16, BF16], 256, f"gate_c_bwd_{tag}")
    a_rows = [cfg.view(proj, "ua"), cfg.view(proj, "va"), cfg.view(proj, "za")]
    a_par = [p["g_v"], p["w_s"], p["b_s"], p["g_oa"]]
    (dua, dva, dza), (dg_v, dw_s, db_s, dg_oa), _ = _rowwise_vjp(
        _f_gmlp, a_rows, [], a_par, [], [(dy, A, (B + C) // A)], [BF16] * 3, LANE, f"gmlp_bwd_{tag}")
    qb, kb, vb = cfg.heads_view(proj, "qb"), cfg.heads_view(proj, "kb"), cfg.heads_view(proj, "vb")
    dqb, dkb, dvb, moved_stick = _stick_bwd(qb, kb, vb, (dyb, 0, LANE), sv["tot"], cfg.Hb, LANE ** -0.5,
                                            f"stick_bwd_{tag}", *cfg.tiles("stick_bwd", l),
                                            carry=_Exchange([[a] for a in (ext_stick or [wout_slots])], False))
    ext_got = [mv[0] for mv in moved_stick] if ext_stick else []
    ext_mla = ext_mla + ([wout_slots] if ext_stick else [])
    qc, kc, vc = (sv["q_rot"], 0, 2 * LANE), (sv["k_full"], 0, 2 * LANE), (sv["v_c"], 0, LANE)
    dq_rot, dk_full, dv_c, *moved_mla = _softmax_bwd(qc, kc, vc, (sv["yc"], 0, LANE), (dyc, 0, LANE), sv["lse"], cfg.Hc,
                                                     (LANE + ROPE) ** -0.5, f"mla_bwd_{tag}", *cfg.tiles("mla_bwd", l),
                                                     carry=_Exchange([[a] for a in ext_mla], False) if ext_mla else None)
    moved_mla = [mv[0] for mv in (moved_mla[0] if moved_mla else [])]
    got = dict(w_out=moved_mla.pop() if ext_stick else moved_stick[0][0])
    ext_got += moved_mla
    trig = [(cos2, LANE, 0), (sin2, LANE, 0)]
    r_rows = [(sv["q_raw"], 2 * C, 0), (sv["kv"], 2 * C, 0), (sv["krr"], LANE, 0)]
    (dq_raw, dkv, dkrr), _, _ = _rowwise_vjp(_f_crope, r_rows, trig, [], [rot],
                                             [(dq_rot, 2 * C, 0), (dk_full, 2 * C, 0), (dv_c, C, 0)], [BF16] * 3, 128,
                                             f"crope_bwd_{tag}")
    dcqn = _matmul(dq_raw, W["uq"], "nn", BF16, f"mm_dcq_{tag}")
    d_wuq = _matmul(dq_raw, sv["cqn"], "tn", BF16, f"mm_dwuq_{tag}")
    dckvn = _matmul(dkv, W["ukv"], "nt", BF16, f"mm_dckv_{tag}")
    d_wukv = _matmul(sv["ckvn"], dkv, "tn", BF16, f"mm_dwukv_{tag}")
    c_rows = [cfg.view(proj, "cq"), cfg.view(proj, "ckv"), cfg.view(proj, "kr")]
    (dcq, dckv, dkr), (dg_q, dg_kv), _ = _rowwise_vjp(
        _f_cpre, c_rows, trig, [p["g_q"], p["g_kv"]], [rot],
        [(dcqn, cfg.Q, 0), (dckvn, cfg.KV, 0), (dkrr, LANE, 0)], [BF16] * 3, 256, f"cpre_bwd_{tag}")
    parts = dict(ua=dua, va=dva, za=dza, qb=dqb, kb=dkb, vb=dvb, zb=dzb, zc=dzc, cq=dcq, kr=dkr, ckv=dckv)
    cols, pos = [], 0
    for nm, off in sorted(cfg.off.items(), key=lambda kv_: kv_[1]):
        if off > pos:
            cols.append(jnp.zeros((S, off - pos), BF16))
        cols.append(parts[nm])
        pos = off + parts[nm].shape[1]
    if cfg.NP > pos:
        cols.append(jnp.zeros((S, cfg.NP - pos), BF16))
    dproj = jnp.concatenate(cols, axis=1)
    to_send = dict(c_w_uq=d_wuq.reshape(cfg.Hc, 2 * LANE, cfg.Q)[:, :LANE + ROPE].reshape(NDEV, -1, cfg.Q),
                   c_w_ukv=_to_slots_cols(d_wukv))
    if last:
        half = D // 2
        d_win_a = _matmul(dproj, sv["h"][:, :half], "tn", BF16, f"mm_dwin_a_{tag}")
        d_win_b, moved_a = _matmul(dproj, sv["h"][:, half:], "tn", BF16, f"mm_dwin_b_{tag}",
                                   carry=_Exchange([[_unpad_w_in(cfg, d_win_a).reshape(NDEV, -1, half)]], False))
        dh, moved = _matmul(dproj, W["in"], "nn", BF16, f"mm_dh_{tag}",
                            carry=_Exchange([[_unpad_w_in(cfg, d_win_b).reshape(NDEV, -1, D - half)],
                                             [to_send["c_w_uq"]], [to_send["c_w_ukv"]]], False))
        got.update(w_in=(moved_a[0][0], moved[0][0]), c_w_uq=moved[1][0], c_w_ukv=moved[2][0])
        to_send = {}
    else:
        d_win = _matmul(dproj, sv["h"], "tn", BF16, f"mm_dwin_{tag}")
        to_send["w_in"] = _unpad_w_in(cfg, d_win).reshape(NDEV, -1, D)
        dh = _matmul(dproj, W["in"], "nn", BF16, f"mm_dh_{tag}")
    (dx,), (dg_pre,), _ = _rowwise_vjp(_f_pre_res, [(sv["x"], D, 0)], [], [p["g_pre"]], [],
                                       [(dh, D, 0), (dout, D, 0)], [F32], 128, f"pre_bwd_{tag}")
    small = dict(g_pre=dg_pre[0], a_g_v=dg_v.reshape(cfg.G, LANE), a_w_s=dw_s, a_b_s=db_s[:, :, 0], c_g_q=dg_q[0],
                 c_g_kv=dg_kv[0], g_out=jnp.concatenate([dg_oa[0], dg_ob[0], dg_oc[0]]))
    return dx, small, got, to_send, ext_got


def _pack_small(vals):
    pieces = []
    for nm in SMALL:
        piece = vals[nm].reshape(-1, LANE)
        pieces.append(jnp.pad(piece, ((0, -piece.shape[0] % 8), (0, 0))))
    packed = jnp.concatenate(pieces, axis=0)
    return jnp.pad(packed, ((0, -packed.shape[0] % SMALL_ROWS), (0, 0)))


def _unpack_small(packed, like):
    out, row = {}, 0
    for nm in SMALL:
        n = like[nm].size // LANE
        out[nm] = packed[row:row + n].reshape(like[nm].shape)
        row += n + (-n % 8)
    return out


def kernel(x, positions, g_pre, w_in, a_g_v, a_w_s, a_b_s, c_g_q, c_g_kv, c_w_uq, c_w_ukv, g_out, w_out, g_final, loss_target, m_g_pre, m_w_in, m_a_g_v, m_a_w_s, m_a_b_s, m_c_g_q, m_c_g_kv, m_c_w_uq, m_c_w_ukv, m_g_out, m_w_out, m_g_final, v_g_pre, v_w_in, v_a_g_v, v_a_w_s, v_a_b_s, v_c_g_q, v_c_g_kv, v_c_w_uq, v_c_w_ukv, v_g_out, v_w_out, v_g_final):
    depth, S, D = w_in.shape[0], x.shape[1], x.shape[2]
    cfg = _Cfg(S, D, a_g_v.shape[1], c_g_q.shape[1], c_g_kv.shape[1], c_w_ukv.shape[2] * NDEV // (2 * LANE), g_out.shape[1])
    weights = dict(g_pre=g_pre, w_in=w_in, a_g_v=a_g_v, a_w_s=a_w_s, a_b_s=a_b_s, c_g_q=c_g_q, c_g_kv=c_g_kv,
                   c_w_uq=c_w_uq, c_w_ukv=c_w_ukv, g_out=g_out, w_out=w_out, g_final=g_final)
    mom_m = dict(g_pre=m_g_pre, w_in=m_w_in, a_g_v=m_a_g_v, a_w_s=m_a_w_s, a_b_s=m_a_b_s, c_g_q=m_c_g_q, c_g_kv=m_c_g_kv,
                 c_w_uq=m_c_w_uq, c_w_ukv=m_c_w_ukv, g_out=m_g_out, w_out=m_w_out, g_final=m_g_final)
    mom_v = dict(g_pre=v_g_pre, w_in=v_w_in, a_g_v=v_a_g_v, a_w_s=v_a_w_s, a_b_s=v_a_b_s, c_g_q=v_c_g_q, c_g_kv=v_c_g_kv,
                 c_w_uq=v_c_w_uq, c_w_ukv=v_c_w_ukv, g_out=v_g_out, w_out=v_w_out, g_final=v_g_final)
    big_names = ("w_in", "c_w_uq", "c_w_ukv", "w_out")

    inv_freq = 1.0 / (ROPE_THETA ** (jnp.arange(0, ROPE, 2, dtype=F32) / ROPE))
    ang = positions[0].astype(F32)[:, None] * inv_freq
    zpad = jnp.zeros((S, LANE - ROPE), F32)
    cos2 = jnp.concatenate([jnp.cos(ang), jnp.cos(ang), zpad], axis=1)
    sin2 = jnp.concatenate([jnp.sin(ang), jnp.sin(ang), zpad], axis=1)
    rot = _rope_matrix()

    for tree in (weights, mom_m, mom_v):
        for nm in TRANSPOSED:
            tree[nm] = jnp.swapaxes(tree[nm], 1, 2)

    def shards(l, names):
        return [[weights[nm][l].astype(BF16)] for nm in names]

    def assemble(g_in, g_uq, g_ukv, g_wout):
        uq = jnp.pad(g_uq.reshape(cfg.Hc, LANE + ROPE, cfg.Q), ((0, 0), (0, LANE - ROPE), (0, 0))).reshape(2 * cfg.C, cfg.Q)
        return {"in": _pad_w_in(cfg, g_in.reshape(-1, D)), "uq": uq, "ukv": _from_slots_cols(g_ukv),
                "out": _perm_rows_out(cfg, g_wout.reshape(cfg.DMIX, D))}

    params = [_layer_params(cfg, l, g_pre, a_g_v, a_w_s, a_b_s, c_g_q, c_g_kv, g_out) for l in range(depth)]

    gathered = [g[0] for g in _exchange(_GatherTwoLevel(shards(0, big_names)), "gather_weights_l0")]
    hcur, saved, Ws = x[0], [], []
    for l in range(depth):
        Ws.append(assemble(*gathered))
        nxt = l + 1 < depth
        hcur, sv, got_in, got_rest = _layer_fwd(
            cfg, l, hcur, Ws[l], params[l], cos2, sin2, rot,
            carry_stick=_GatherTwoLevel(shards(l + 1, big_names[:1])) if nxt else None,
            carry_mla=_GatherTwoLevel(shards(l + 1, big_names[1:])) if nxt else None)
        saved.append(sv)
        gathered = [g[0] for g in got_in + got_rest]
    (dh,), (dg_final,), (loss_rows,) = _rowwise_vjp(
        _f_final, [(hcur, D, 0)], [(loss_target[0], D, 0)], [g_final[None]], [], [(jnp.ones((S, 1), F32), 1, 0)],
        [F32], 128, "final", primal=[(1, F32)])
    loss = lax.psum(jnp.sum(loss_rows), MESH_AXES)

    small_g, slots, pending = [None] * depth, [None] * depth, {}
    for l in reversed(range(depth)):
        ext_stick = [pending["w_in"]] if pending else []
        ext_mla = [pending["c_w_uq"], pending["c_w_ukv"]] if pending else []
        dh, small_g[l], slots[l], pending, ext_got = _layer_bwd(cfg, l, dh, saved[l], Ws[l], params[l], cos2, sin2, rot,
                                                                ext_stick, ext_mla, l == 0)
        if ext_got:
            slots[l + 1].update(w_in=ext_got[0], c_w_uq=ext_got[1], c_w_ukv=ext_got[2])
    grad_x = dh[None]
    small_grads = {nm: jnp.stack([small_g[l][nm] for l in range(depth)]) for nm in SMALL if nm != "g_final"}
    small_grads["g_final"] = dg_final[0]
    (small_slots,) = _exchange(_Exchange([[_pack_small(small_grads)]], True), "gather_small_grads")

    res = {}
    for nm in big_names:
        res[nm] = None
        for l in range(depth):
            parts = slots[l][nm] if isinstance(slots[l][nm], tuple) else (slots[l][nm],)
            col0 = 0
            for i, part in enumerate(parts):
                res[nm] = _adamw(part, weights[nm], mom_m[nm], mom_v[nm], l, res[nm], f"adamw_{nm}_l{l}_{i}", col0)
                col0 += part.shape[2]
        if nm in TRANSPOSED:
            res[nm] = [jnp.swapaxes(r, 1, 2) for r in res[nm]]
    packed = _adamw(small_slots[0], _pack_small(weights)[None], _pack_small(mom_m)[None], _pack_small(mom_v)[None], 0, None,
                    "adamw_small")
    small_res = [_unpack_small(r[0], weights) for r in packed]
    order = ("g_pre", "w_in", "a_g_v", "a_w_s", "a_b_s", "c_g_q", "c_g_kv", "c_w_uq", "c_w_ukv", "g_out", "w_out", "g_final")
    outs = [loss, grad_x]
    for kind in range(4):
        outs += [small_res[kind][nm] if nm in SMALL else res[nm][kind] for nm in order]
    return tuple(outs)
```

```python
import functools

import numpy as np
import jax
import jax.numpy as jnp
from jax import lax
from jax.experimental import pallas as pl
from jax.experimental.pallas import tpu as pltpu

NDEV = 8
MESH_AXES = ("x", "y", "c")
LANE = 128
ROPE = 64
EPS = 1e-6
ROPE_THETA = 10000.0
ADAM_LR, ADAM_B1, ADAM_B2, ADAM_EPS, ADAM_WD, ADAM_STEP = 0.001, 0.9, 0.999, 1e-08, 0.01, 10
VMEM_LIMIT = 48 * 1024 * 1024
ADAM_TILE_BYTES = 768 * 1024
CARRY_MID_PERCENT = 80
SMALL_ROWS = 256
ATTN_TILES = {"stick_fwd": [(1024, 256)], "stick_bwd": [(1024, 256)], "mla_fwd": [(512, 1024)], "mla_bwd": [(1024, 512)]}
DIAG_BLOCK = 256
F32, BF16 = jnp.float32, jnp.bfloat16
SMALL = ("g_pre", "a_g_v", "a_w_s", "a_b_s", "c_g_q", "c_g_kv", "g_out", "g_final")
TRANSPOSED = ("w_in", "c_w_uq")


def _tile(dim, cap, mult=LANE):
    if dim <= cap:
        return dim
    t = (cap // mult) * mult
    while t >= mult:
        if dim % t == 0:
            return t
        t -= mult
    return dim


def _dot_nt(a, b):
    return lax.dot_general(a, b, (((1,), (1,)), ((), ())), preferred_element_type=F32)


def _dot_tn(a, b):
    return lax.dot_general(a, b, (((0,), (0,)), ((), ())), preferred_element_type=F32)


def _dot(a, b):
    return jnp.dot(a, b, preferred_element_type=F32)


class _Exchange:
    def __init__(self, groups, gather):
        self.groups, self.gather = groups, gather
        self.flat = [(gi, li, a) for gi, grp in enumerate(groups) for li, a in enumerate(grp)]
        self.n = len(self.flat)
        self.args = [a for (_, _, a) in self.flat]
        self.out_shape = [jax.ShapeDtypeStruct((len(grp), NDEV) + tuple(grp[0].shape[-2:]), grp[0].dtype) for grp in groups]
        self.scratch = [pltpu.SemaphoreType.DMA((self.n, NDEV - 1)), pltpu.SemaphoreType.DMA((self.n, NDEV - 1)),
                        pltpu.SemaphoreType.DMA((self.n,))]

    def _copies(self, ins, outs, send_sems, recv_sems, local_sems, landings):
        x, y, c = lax.axis_index("x"), lax.axis_index("y"), lax.axis_index("c")
        me = 4 * x + 2 * y + c
        owns = [pltpu.make_async_copy(ins[i] if self.gather else ins[i].at[me], outs[gi].at[li, me], local_sems.at[i])
                for i, (gi, li, _) in enumerate(self.flat)]
        pairs = []
        for k in range(1, NDEV):
            px = 1 - x if k & 4 else x
            py = 1 - y if k & 2 else y
            pc = 1 - c if k & 1 else c
            peer = 4 * px + 2 * py + pc
            for i, (gi, li, _) in enumerate(self.flat):
                src = ins[i] if self.gather else ins[i].at[peer]
                sems = dict(send_sem=send_sems.at[i, k - 1], recv_sem=recv_sems.at[i, k - 1],
                            device_id=(px, py, pc), device_id_type=pl.DeviceIdType.MESH)
                out = pltpu.make_async_remote_copy(src_ref=src, dst_ref=outs[gi].at[li, me], **sems)
                landing = pltpu.make_async_remote_copy(src_ref=src, dst_ref=outs[gi].at[li, peer], **sems) if landings else None
                pairs.append((out, landing))
        return owns, pairs

    def start(self, ins, outs, sems):
        owns, pairs = self._copies(ins, outs, *sems, landings=False)
        for own in owns:
            own.start()
        for out, _ in pairs:
            out.start()

    def mid(self, ins, outs, sems):
        pass

    def wait(self, ins, outs, sems):
        owns, pairs = self._copies(ins, outs, *sems, landings=True)
        for out, landing in pairs:
            out.wait_send()
            landing.wait_recv()
        for own in owns:
            own.wait()


class _GatherTwoLevel(_Exchange):
    def __init__(self, groups):
        super().__init__(groups, True)

    def _copy(self, i, k, ins, outs, send_sems, recv_sems, landing):
        gi, li, _ = self.flat[i]
        x, y, c = lax.axis_index("x"), lax.axis_index("y"), lax.axis_index("c")
        chips = [(x, y), (1 - x, y), (x, 1 - y), (1 - x, 1 - y)]

        def slot(chip, core):
            return outs[gi].at[li, 4 * chip[0] + 2 * chip[1] + core]

        if k == 0:
            to, src, dst, lands = (x, y, 1 - c), ins[i], slot(chips[0], c), slot(chips[0], 1 - c)
        elif k <= 3:
            to, src, dst, lands = (*chips[k], c), ins[i], slot(chips[0], c), slot(chips[k], c)
        else:
            to, src, dst, lands = (x, y, 1 - c), slot(chips[k - 3], c), slot(chips[k - 3], c), slot(chips[k - 3], 1 - c)
        return pltpu.make_async_remote_copy(src_ref=src, dst_ref=lands if landing else dst, send_sem=send_sems.at[i, k],
                                            recv_sem=recv_sems.at[i, k], device_id=to, device_id_type=pl.DeviceIdType.MESH)

    def _own(self, i, ins, outs, local_sems):
        gi, li, _ = self.flat[i]
        me = 4 * lax.axis_index("x") + 2 * lax.axis_index("y") + lax.axis_index("c")
        return pltpu.make_async_copy(ins[i], outs[gi].at[li, me], local_sems.at[i])

    def start(self, ins, outs, sems):
        send_sems, recv_sems, local_sems = sems
        for i in range(self.n):
            self._own(i, ins, outs, local_sems).start()
        for k in range(4):
            for i in range(self.n):
                self._copy(i, k, ins, outs, send_sems, recv_sems, False).start()

    def mid(self, ins, outs, sems):
        send_sems, recv_sems, _ = sems
        for k in range(1, 4):
            for i in range(self.n):
                self._copy(i, k, ins, outs, send_sems, recv_sems, True).wait_recv()
                self._copy(i, k + 3, ins, outs, send_sems, recv_sems, False).start()

    def wait(self, ins, outs, sems):
        send_sems, recv_sems, local_sems = sems
        for k in (0, 4, 5, 6):
            for i in range(self.n):
                self._copy(i, k, ins, outs, send_sems, recv_sems, True).wait_recv()
        for k in range(NDEV - 1):
            for i in range(self.n):
                self._copy(i, k, ins, outs, send_sems, recv_sems, False).wait_send()
        for i in range(self.n):
            self._own(i, ins, outs, local_sems).wait()


def _call(body, name, grid, in_specs, out_specs, out_shape, scratch, semantics, args, carry=None):
    n_in, n_out, n_scr = len(in_specs), len(out_specs), len(scratch)
    if carry is None:
        run = body
    else:
        semantics = ("arbitrary",) * len(grid)
        anyspec = pl.BlockSpec(memory_space=pl.ANY)
        in_specs = list(in_specs) + [anyspec] * carry.n
        out_specs = list(out_specs) + [anyspec] * len(carry.groups)
        out_shape = list(out_shape) + carry.out_shape
        scratch = list(scratch) + carry.scratch
        args = list(args) + carry.args

        def run(*refs):
            c_in, x_in = refs[:n_in], refs[n_in:n_in + carry.n]
            rest = refs[n_in + carry.n:]
            c_out, x_out = rest[:n_out], rest[n_out:n_out + len(carry.groups)]
            c_scr, sems = rest[n_out + len(carry.groups):len(rest) - 3], rest[len(rest) - 3:]
            step, total = 0, 1
            for d, extent in enumerate(grid):
                step = step * extent + pl.program_id(d)
                total *= extent

            @pl.when(step == 0)
            def _():
                carry.start(x_in, x_out, sems)

            body(*c_in, *c_out, *c_scr)

            @pl.when(step == (total * CARRY_MID_PERCENT) // 100)
            def _():
                carry.mid(x_in, x_out, sems)

            @pl.when(step == total - 1)
            def _():
                carry.wait(x_in, x_out, sems)

    res = pl.pallas_call(
        run, name=name, grid=grid, out_shape=list(out_shape), in_specs=list(in_specs), out_specs=list(out_specs),
        scratch_shapes=list(scratch),
        compiler_params=pltpu.CompilerParams(dimension_semantics=semantics, vmem_limit_bytes=VMEM_LIMIT,
                                             has_side_effects=carry is not None),
    )(*args)
    return list(res[:n_out]), list(res[n_out:])


def _exchange(ex, name):
    groups = ex.groups

    def body(*refs):
        ins, outs, sems = refs[:ex.n], refs[ex.n:ex.n + len(groups)], refs[ex.n + len(groups):]
        ex.start(ins, outs, sems)
        ex.mid(ins, outs, sems)
        ex.wait(ins, outs, sems)

    anyspec = pl.BlockSpec(memory_space=pl.ANY)
    return pl.pallas_call(
        body, name=name, out_shape=ex.out_shape, in_specs=[anyspec] * ex.n, out_specs=[anyspec] * len(groups),
        scratch_shapes=ex.scratch, compiler_params=pltpu.CompilerParams(has_side_effects=True),
    )(*ex.args)


def _matmul(a, b, mode, out_dtype, name, add=None, tm=1024, tn=1024, tk=2048, carry=None):
    if mode == "tn":
        (K, M), (K2, N) = a.shape, b.shape
    elif mode == "nt":
        (M, K), (N, K2) = a.shape, b.shape
    else:
        (M, K), (K2, N) = a.shape, b.shape
    assert K == K2, (a.shape, b.shape, mode)
    tm, tn, tk = _tile(M, tm), _tile(N, tn), _tile(K, tk)
    nk = K // tk
    a_spec = pl.BlockSpec((tk, tm), lambda i, j, k: (k, i)) if mode == "tn" else pl.BlockSpec((tm, tk), lambda i, j, k: (i, k))
    b_spec = pl.BlockSpec((tn, tk), lambda i, j, k: (j, k)) if mode == "nt" else pl.BlockSpec((tk, tn), lambda i, j, k: (k, j))
    dot = {"nn": _dot, "nt": _dot_nt, "tn": _dot_tn}[mode]
    has_add = add is not None

    def body(*refs):
        a_ref, b_ref = refs[0], refs[1]
        o_ref, acc = refs[-2], refs[-1]
        k = pl.program_id(2)

        @pl.when(k == 0)
        def _():
            acc[...] = jnp.zeros_like(acc)

        acc[...] += dot(a_ref[...].astype(BF16), b_ref[...].astype(BF16))

        @pl.when(k == nk - 1)
        def _():
            r = acc[...]
            if has_add:
                r = r + refs[2][...]
            o_ref[...] = r.astype(o_ref.dtype)

    in_specs = [a_spec, b_spec]
    args = [a, b]
    if has_add:
        in_specs.append(pl.BlockSpec((tm, tn), lambda i, j, k: (i, j)))
        args.append(add)
    (out,), moved = _call(body, name, (M // tm, N // tn, nk), in_specs, [pl.BlockSpec((tm, tn), lambda i, j, k: (i, j))],
                          [jax.ShapeDtypeStruct((M, N), out_dtype)], [pltpu.VMEM((tm, tn), F32)],
                          ("parallel", "parallel", "arbitrary"), args, carry)
    return out if carry is None else (out, moved)


def _row_specs(views, tile):
    return [pl.BlockSpec((tile, w), functools.partial(lambda i, cb: (i, cb), cb=cb)) for (_, w, cb) in views]


def _full_specs(arrs):
    return [pl.BlockSpec(p.shape, functools.partial(lambda i, nd: (0,) * nd, nd=p.ndim)) for p in arrs]


def _rowwise(fn, rows, aux, params, consts, outs, tile, name):
    S = rows[0][0].shape[0]
    nr, na, npar, nc = len(rows), len(aux), len(params), len(consts)

    def body(*refs):
        ins = [r[...].astype(F32) for r in refs[:nr + na]]
        small = [r[...] for r in refs[nr + na:nr + na + npar + nc]]
        res = fn(*ins, *small)
        for o_ref, r in zip(refs[nr + na + npar + nc:], res):
            o_ref[...] = r.astype(o_ref.dtype)

    return pl.pallas_call(
        body, name=name, grid=(S // tile,),
        out_shape=[jax.ShapeDtypeStruct((S, w), dt) for (w, dt) in outs],
        in_specs=_row_specs(rows + aux, tile) + _full_specs(params + consts),
        out_specs=[pl.BlockSpec((tile, w), lambda i: (i, 0)) for (w, _) in outs],
        compiler_params=pltpu.CompilerParams(dimension_semantics=("parallel",), vmem_limit_bytes=VMEM_LIMIT),
    )(*[v[0] for v in rows + aux], *params, *consts)


def _rowwise_vjp(fn, rows, aux, params, consts, cots, grad_dtypes, tile, name, primal=()):
    S = rows[0][0].shape[0]
    nr, na, npar, nc, nct, npr = len(rows), len(aux), len(params), len(consts), len(cots), len(primal)

    def body(*refs):
        n_in = nr + na + npar + nc + nct
        rv = [r[...].astype(F32) for r in refs[:nr]]
        av = [r[...].astype(F32) for r in refs[nr:nr + na]]
        pv = [r[...] for r in refs[nr + na:nr + na + npar]]
        cv = [r[...] for r in refs[nr + na + npar:nr + na + npar + nc]]
        ct = tuple(r[...].astype(F32) for r in refs[nr + na + npar + nc:n_in])
        res, vjp = jax.vjp(lambda *rp: tuple(fn(*rp[:nr], *av, *rp[nr:], *cv)), *rv, *pv)
        grads = vjp(ct)
        g_refs = refs[n_in:n_in + nr]
        p_refs = refs[n_in + nr:n_in + nr + npar]
        o_refs = refs[n_in + nr + npar:]
        for g_ref, g in zip(g_refs, grads[:nr]):
            g_ref[...] = g.astype(g_ref.dtype)

        @pl.when(pl.program_id(0) == 0)
        def _():
            for p_ref in p_refs:
                p_ref[...] = jnp.zeros_like(p_ref)

        for p_ref, g in zip(p_refs, grads[nr:]):
            p_ref[...] += g
        for o_ref, r in zip(o_refs, res[:npr]):
            o_ref[...] = r.astype(o_ref.dtype)

    out_shape = ([jax.ShapeDtypeStruct((S, w), dt) for (_, w, _), dt in zip(rows, grad_dtypes)]
                 + [jax.ShapeDtypeStruct(p.shape, F32) for p in params]
                 + [jax.ShapeDtypeStruct((S, w), dt) for (w, dt) in primal])
    out_specs = ([pl.BlockSpec((tile, w), lambda i: (i, 0)) for (_, w, _) in rows] + _full_specs(params)
                 + [pl.BlockSpec((tile, w), lambda i: (i, 0)) for (w, _) in primal])
    res = pl.pallas_call(
        body, name=name, grid=(S // tile,), out_shape=out_shape,
        in_specs=_row_specs(rows + aux, tile) + _full_specs(params + consts) + _row_specs(cots, tile),
        out_specs=out_specs,
        compiler_params=pltpu.CompilerParams(dimension_semantics=("arbitrary",), vmem_limit_bytes=VMEM_LIMIT),
    )(*[v[0] for v in rows + aux], *params, *consts, *[v[0] for v in cots])
    return res[:nr], res[nr:nr + npar], res[nr + npar:]


@jax.custom_vjp
def _mm(a, b):
    return _dot(a.astype(BF16), b.astype(BF16))


def _mm_fwd(a, b):
    return _mm(a, b), (a, b)


def _mm_bwd(res, ct):
    a, b = res
    ctb = ct.astype(BF16)
    return _dot_nt(ctb, b.astype(BF16)), _dot_tn(a.astype(BF16), ctb)


_mm.defvjp(_mm_fwd, _mm_bwd)


def _rms(x, g):
    return x * lax.rsqrt(jnp.mean(x * x, axis=-1, keepdims=True) + EPS) * g


def _f_pre(x, g):
    return (_rms(x, g),)


def _f_pre_res(x, g):
    return _rms(x, g), x


def _f_gate(y, z, g):
    return (_rms(y, g) * jax.nn.silu(z),)


def _f_gmlp(u, v, z, g_v, w_s, b_s, g_o):
    groups = w_s.shape[0]
    u, v = jax.nn.gelu(u), jax.nn.gelu(v)
    t_idx = lax.broadcasted_iota(jnp.int32, (LANE, LANE), 0)
    s_idx = lax.broadcasted_iota(jnp.int32, (LANE, LANE), 1)
    ys = []
    for g in range(groups):
        sl = slice(g * LANE, (g + 1) * LANE)
        vn = _rms(v[:, sl], g_v[:, sl])
        w = jnp.where(s_idx <= t_idx, w_s[g], 0.0)
        ys.append(u[:, sl] * (_mm(w, vn) + b_s[g]))
    return (_rms(jnp.concatenate(ys, axis=1), g_o) * jax.nn.silu(z),)


def _rope(x, cos2, sin2, rot):
    return x * cos2 + _mm(x, rot) * sin2


def _f_cpre(cq, ckv, kr, cos2, sin2, g_q, g_kv, rot):
    return _rms(cq, g_q), _rms(ckv, g_kv), _rope(kr, cos2, sin2, rot)


def _f_crope(q, kv, krr, cos2, sin2, rot):
    heads = q.shape[1] // (2 * LANE)
    qs, ks, vs = [], [], []
    for h in range(heads):
        lo, mid, hi = 2 * h * LANE, (2 * h + 1) * LANE, (2 * h + 2) * LANE
        qs += [q[:, lo:mid], _rope(q[:, mid:hi], cos2, sin2, rot)]
        ks += [kv[:, lo:mid], krr]
        vs += [kv[:, mid:hi]]
    return jnp.concatenate(qs, axis=1), jnp.concatenate(ks, axis=1), jnp.concatenate(vs, axis=1)


def _f_final(h, target, g):
    err = _rms(h, g) - target
    return (0.5 * jnp.mean(err * err, axis=-1, keepdims=True),)


def _rope_matrix():
    r = np.zeros((LANE, LANE), np.float32)
    half = ROPE // 2
    for i in range(half):
        r[i + half, i] = -1.0
        r[i, i + half] = 1.0
    return jnp.asarray(r)


def _head_spec(view, rows, n_rows_block):
    _, cb0, w = view
    if n_rows_block:
        return pl.BlockSpec((rows, w), functools.partial(lambda h, i, cb0: (i, cb0 + h), cb0=cb0))
    return pl.BlockSpec((rows, w), functools.partial(lambda h, i, cb0: (0, cb0 + h), cb0=cb0))


def _stat_spec(tq):
    return pl.BlockSpec((1, tq, 1), lambda h, i: (h, i, 0))


def _softplus(z):
    return jnp.maximum(z, 0.0) + jnp.log(1.0 + jnp.exp(-jnp.abs(z)))


def _cumsum_mm(x, m01):
    hi = x.astype(BF16)
    lo = (x - hi.astype(F32)).astype(BF16)
    return _dot(hi, m01) + _dot(lo, m01)


def _attn_call(body, name, heads, S, tq, ins, in_blocked, outs, out_blocked, scratch, stats_in=0, stats_out=0, carry=None):
    in_specs = [_head_spec(v, tq if blk else S, blk) for v, blk in zip(ins[:len(ins) - stats_in], in_blocked)]
    in_specs += [_stat_spec(tq)] * stats_in
    out_specs = [_head_spec((None, 0, w), tq if blk else S, blk) for (w, _), blk in zip(outs, out_blocked)]
    out_specs += [_stat_spec(tq)] * stats_out
    out_shape = [jax.ShapeDtypeStruct((S, heads * w), dt) for (w, dt) in outs]
    out_shape += [jax.ShapeDtypeStruct((heads, S, 1), F32)] * stats_out
    args = [v[0] for v in ins[:len(ins) - stats_in]] + list(ins[len(ins) - stats_in:])
    res, moved = _call(body, name, (heads, S // tq), in_specs, out_specs, out_shape, scratch, ("arbitrary", "arbitrary"),
                       args, carry)
    return res if carry is None else res + [moved]


def _softmax_fwd(q, k, v, heads, scale, name, tq, bk, carry=None):
    S, dv = q[0].shape[0], v[2]

    def body(q_ref, k_ref, v_ref, o_ref, lse_ref):
        qi = pl.program_id(1)
        qv = q_ref[...]
        row = qi * tq + lax.broadcasted_iota(jnp.int32, (tq, bk), 0)
        col0 = lax.broadcasted_iota(jnp.int32, (tq, bk), 1)

        def step(kb, carry):
            m, l, acc = carry
            sl = pl.ds(pl.multiple_of(kb * bk, bk), bk)
            s = _dot_nt(qv, k_ref[sl, :]) * scale
            s = jnp.where(kb * bk + col0 <= row, s, -1e30)
            m_new = jnp.maximum(m, jnp.max(s, axis=1, keepdims=True))
            p = jnp.exp(s - m_new)
            alpha = jnp.exp(m - m_new)
            l = alpha * l + jnp.sum(p, axis=1, keepdims=True)
            acc = alpha * acc + _dot(p.astype(BF16), v_ref[sl, :])
            return m_new, l, acc

        n_kb = (qi * tq + tq + bk - 1) // bk
        m, l, acc = lax.fori_loop(0, n_kb, step, (jnp.full((tq, 1), -1e30, F32), jnp.zeros((tq, 1), F32),
                                                  jnp.zeros((tq, dv), F32)))
        o_ref[...] = (acc / l).astype(o_ref.dtype)
        lse_ref[0] = m + jnp.log(l)

    return _attn_call(body, name, heads, S, tq, [q, k, v], [1, 0, 0], [(dv, BF16)], [1], [], stats_out=1, carry=carry)


def _softmax_bwd(q, k, v, o, do, lse, heads, scale, name, tq, bk, carry=None):
    S, dq_w, dv = q[0].shape[0], q[2], v[2]
    nq = S // tq

    bd = min(DIAG_BLOCK, tq)
    assert tq % bk == 0 and tq % bd == 0

    def body(q_ref, k_ref, v_ref, o_ref, do_ref, lse_ref, dq_ref, dk_ref, dv_ref, dk_acc, dv_acc, delta_scr, dq_scr):
        qi = pl.program_id(1)

        @pl.when(qi == 0)
        def _():
            dk_acc[...] = jnp.zeros_like(dk_acc)
            dv_acc[...] = jnp.zeros_like(dv_acc)

        delta_scr[...] = jnp.sum(do_ref[...].astype(F32) * o_ref[...].astype(F32), axis=1, keepdims=True)
        dq_scr[...] = jnp.zeros_like(dq_scr)

        def block(r0, sl, width, masked):
            qv, dov = q_ref[r0:, :], do_ref[r0:, :]
            ks, vs = k_ref[sl, :], v_ref[sl, :]
            p = jnp.exp(_dot_nt(qv, ks) * scale - lse_ref[0, r0:, :])
            if masked:
                shape = (tq - r0, width)
                p = jnp.where(lax.broadcasted_iota(jnp.int32, shape, 1) <= lax.broadcasted_iota(jnp.int32, shape, 0), p, 0.0)
            ds = (p * (_dot_nt(dov, vs) - delta_scr[r0:, :]) * scale).astype(BF16)
            dk_acc[sl, :] += _dot_tn(ds, qv)
            dv_acc[sl, :] += _dot_tn(p.astype(BF16), dov)
            dq_scr[r0:, :] += _dot(ds, ks)

        def step(kb, _):
            block(0, pl.ds(pl.multiple_of(kb * bk, bk), bk), bk, False)
            return 0

        lax.fori_loop(0, qi * (tq // bk), step, 0)
        for j in range(tq // bd):
            block(j * bd, pl.ds(pl.multiple_of(qi * tq + j * bd, bd), bd), bd, True)
        dq_ref[...] = dq_scr[...].astype(dq_ref.dtype)

        @pl.when(qi == nq - 1)
        def _():
            dk_ref[...] = dk_acc[...].astype(dk_ref.dtype)
            dv_ref[...] = dv_acc[...].astype(dv_ref.dtype)

    return _attn_call(body, name, heads, S, tq, [q, k, v, o, do, lse], [1, 0, 0, 1, 1],
                      [(dq_w, BF16), (dq_w, BF16), (dv, BF16)], [1, 0, 0],
                      [pltpu.VMEM((S, dq_w), F32), pltpu.VMEM((S, dv), F32), pltpu.VMEM((tq, 1), F32),
                       pltpu.VMEM((tq, dq_w), F32)], stats_in=1, carry=carry)


def _stick_fwd(q, k, v, heads, scale, name, tq, bk, carry=None):
    S, dv = q[0].shape[0], v[2]

    assert tq % bk == 0
    n_sub = tq // bk

    def body(q_ref, k_ref, v_ref, o_ref, tot_ref, c_scr, acc_scr):
        qi = pl.program_id(1)
        m_gt = (lax.broadcasted_iota(jnp.int32, (bk, bk), 0) > lax.broadcasted_iota(jnp.int32, (bk, bk), 1)).astype(BF16)

        def block(r0, sl, masked):
            rows = tq - r0
            z = _dot_nt(q_ref[r0:, :], k_ref[sl, :]) * scale
            sp = _softplus(z)
            lk = -sp
            if masked:
                mask = lax.broadcasted_iota(jnp.int32, (rows, bk), 1) < lax.broadcasted_iota(jnp.int32, (rows, bk), 0)
                lk = jnp.where(mask, lk, 0.0)
            after = _cumsum_mm(lk, m_gt) + c_scr[r0:, :]
            a = jnp.exp(z - sp + after)
            if masked:
                a = jnp.where(mask, a, 0.0)
            acc_scr[r0:, :] += _dot(a.astype(BF16), v_ref[sl, :])
            c_scr[r0:, :] += jnp.sum(lk, axis=1, keepdims=True)

        c_scr[...] = jnp.zeros_like(c_scr)
        acc_scr[...] = jnp.zeros_like(acc_scr)
        for j in reversed(range(n_sub)):
            block(j * bk, pl.ds(pl.multiple_of(qi * tq + j * bk, bk), bk), True)

        def step(it, _):
            block(0, pl.ds(pl.multiple_of((qi * n_sub - 1 - it) * bk, bk), bk), False)
            return 0

        lax.fori_loop(0, qi * n_sub, step, 0)
        o_ref[...] = acc_scr[...].astype(o_ref.dtype)
        tot_ref[0] = c_scr[...]

    return _attn_call(body, name, heads, S, tq, [q, k, v], [1, 0, 0], [(dv, BF16)], [1],
                      [pltpu.VMEM((tq, 1), F32), pltpu.VMEM((tq, dv), F32)], stats_out=1, carry=carry)


def _stick_bwd(q, k, v, do, tot, heads, scale, name, tq, bk, carry=None):
    S, dq_w, dv = q[0].shape[0], q[2], v[2]
    nq = S // tq

    assert tq % bk == 0
    n_sub = tq // bk

    def body(q_ref, k_ref, v_ref, do_ref, tot_ref, dq_ref, dk_ref, dv_ref, dk_acc, dv_acc, pc_scr, gc_scr, dq_scr):
        qi = pl.program_id(1)

        @pl.when(qi == 0)
        def _():
            dk_acc[...] = jnp.zeros_like(dk_acc)
            dv_acc[...] = jnp.zeros_like(dv_acc)

        j_idx = lax.broadcasted_iota(jnp.int32, (bk, bk), 0)
        s_idx = lax.broadcasted_iota(jnp.int32, (bk, bk), 1)
        m_le, m_lt = (j_idx <= s_idx).astype(BF16), (j_idx < s_idx).astype(BF16)

        def block(r0, sl, masked):
            rows = tq - r0
            qv, dov = q_ref[r0:, :], do_ref[r0:, :]
            ks, vs = k_ref[sl, :], v_ref[sl, :]
            z = _dot_nt(qv, ks) * scale
            sp = _softplus(z)
            lk = -sp
            if masked:
                mask = lax.broadcasted_iota(jnp.int32, (rows, bk), 1) < lax.broadcasted_iota(jnp.int32, (rows, bk), 0)
                lk = jnp.where(mask, lk, 0.0)
            after = tot_ref[0, r0:, :] - pc_scr[r0:, :] - _cumsum_mm(lk, m_le)
            log_beta = z - sp
            a = jnp.exp(log_beta + after)
            if masked:
                a = jnp.where(mask, a, 0.0)
            g = _dot_nt(dov, vs) * a
            cg = gc_scr[r0:, :] + _cumsum_mm(g, m_lt)
            dz = g * jnp.exp(-sp) - jnp.exp(log_beta) * cg
            if masked:
                dz = jnp.where(mask, dz, 0.0)
            dz = (dz * scale).astype(BF16)
            dk_acc[sl, :] += _dot_tn(dz, qv)
            dv_acc[sl, :] += _dot_tn(a.astype(BF16), dov)
            dq_scr[r0:, :] += _dot(dz, ks)
            pc_scr[r0:, :] += jnp.sum(lk, axis=1, keepdims=True)
            gc_scr[r0:, :] += jnp.sum(g, axis=1, keepdims=True)

        pc_scr[...] = jnp.zeros_like(pc_scr)
        gc_scr[...] = jnp.zeros_like(gc_scr)
        dq_scr[...] = jnp.zeros_like(dq_scr)

        def step(kb, _):
            block(0, pl.ds(pl.multiple_of(kb * bk, bk), bk), False)
            return 0

        lax.fori_loop(0, qi * n_sub, step, 0)
        for j in range(n_sub):
            block(j * bk, pl.ds(pl.multiple_of(qi * tq + j * bk, bk), bk), True)
        dq_ref[...] = dq_scr[...].astype(dq_ref.dtype)

        @pl.when(qi == nq - 1)
        def _():
            dk_ref[...] = dk_acc[...].astype(dk_ref.dtype)
            dv_ref[...] = dv_acc[...].astype(dv_ref.dtype)

    return _attn_call(body, name, heads, S, tq, [q, k, v, do, tot], [1, 0, 0, 1],
                      [(dq_w, BF16), (dq_w, BF16), (dv, BF16)], [1, 0, 0],
                      [pltpu.VMEM((S, dq_w), F32), pltpu.VMEM((S, dv), F32), pltpu.VMEM((tq, 1), F32),
                       pltpu.VMEM((tq, 1), F32), pltpu.VMEM((tq, dq_w), F32)], stats_in=1, carry=carry)


def _adamw(slots, w, m, v, layer, prev, name, col0=0):
    _, R, C = slots.shape
    L, full_c = w.shape[0], w.shape[2]
    item = slots.dtype.itemsize
    tc = _tile(C, 2048)
    tr = _tile(R, max(16, ADAM_TILE_BYTES // (item * tc)), mult=16)
    if tr == R and R * tc * item > ADAM_TILE_BYTES:
        tc = _tile(C, max(LANE, ADAM_TILE_BYTES // (item * R)))
    c1, c2 = 1.0 - ADAM_B1 ** ADAM_STEP, 1.0 - ADAM_B2 ** ADAM_STEP
    n_prev = 0 if prev is None else 4

    def body(s_ref, w_ref, m_ref, v_ref, *rest):
        g_out, d_out, m_out, v_out = rest[n_prev:]
        g = s_ref[0].astype(F32)
        for k in range(1, NDEV):
            g = g + s_ref[k].astype(F32)
        m_new = ADAM_B1 * m_ref[0] + (1.0 - ADAM_B1) * g
        v_new = ADAM_B2 * v_ref[0] + (1.0 - ADAM_B2) * (g * g)
        g_out[0] = g
        m_out[0] = m_new
        v_out[0] = v_new
        d_out[0] = -ADAM_LR * ((m_new / c1) / (jnp.sqrt(v_new / c2) + ADAM_EPS) + ADAM_WD * w_ref[0])

    assert col0 % tc == 0
    spec = pl.BlockSpec((1, tr, tc), lambda i, j: (layer, i, j + col0 // tc))
    in_specs = [pl.BlockSpec((NDEV, tr, tc), lambda i, j: (0, i, j)), spec, spec, spec]
    in_specs += [pl.BlockSpec(memory_space=pl.ANY)] * n_prev
    return pl.pallas_call(
        body, name=name, grid=(R // tr, C // tc), out_shape=[jax.ShapeDtypeStruct((L, R, full_c), F32)] * 4,
        in_specs=in_specs, out_specs=[spec] * 4, input_output_aliases={4 + i: i for i in range(n_prev)},
        compiler_params=pltpu.CompilerParams(dimension_semantics=("parallel", "parallel"), vmem_limit_bytes=VMEM_LIMIT),
    )(slots, w, m, v, *(prev or []))


class _Cfg:
    def __init__(self, S, D, groups, q_lora, kv_lora, c_heads, d_mix):
        self.S, self.D, self.G, self.Q, self.KV, self.Hc, self.DMIX = S, D, groups, q_lora, kv_lora, c_heads, d_mix
        self.A, self.C = groups * LANE, c_heads * LANE
        self.B = d_mix - self.A - self.C
        self.Hb = self.B // LANE
        A, B, C = self.A, self.B, self.C
        assert B % LANE == 0 and B % C == 0 and (B + C) % A == 0
        self.ref_segs = [("ua", A), ("va", A), ("za", A), ("qb", B), ("kb", B), ("vb", B), ("zb", B),
                         ("cq", q_lora), ("ckv", kv_lora), ("kr", ROPE), ("zc", C)]
        self.off, off = {}, 0
        for nm, w in [("ua", A), ("va", A), ("za", A), ("qb", B), ("kb", B), ("vb", B), ("zb", B), ("zc", C),
                      ("cq", q_lora), ("kr", LANE), ("ckv", kv_lora)]:
            off = -(-off // w) * w
            self.off[nm] = off
            off += w
        self.NP = -(-off // 512) * 512
        self.width = {"kr": LANE, **{nm: w for nm, w in self.ref_segs if nm != "kr"}}

    def tiles(self, kind, layer):
        tq, bk = ATTN_TILES[kind][layer % len(ATTN_TILES[kind])]
        return min(tq, self.S), min(bk, self.S)

    def view(self, arr, nm):
        w = self.width[nm]
        return (arr, w, self.off[nm] // w)

    def heads_view(self, arr, nm):
        return (arr, self.off[nm] // LANE, LANE)


def _pad_w_in(cfg, wt):
    pieces, start = {}, 0
    for nm, width in cfg.ref_segs:
        pieces[nm] = wt[start:start + width]
        start += width
    rows, pos = [], 0
    for nm, off in sorted(cfg.off.items(), key=lambda kv: kv[1]):
        if off > pos:
            rows.append(jnp.zeros((off - pos, wt.shape[1]), wt.dtype))
        rows.append(pieces[nm])
        pos = off + pieces[nm].shape[0]
    if cfg.NP > pos:
        rows.append(jnp.zeros((cfg.NP - pos, wt.shape[1]), wt.dtype))
    return jnp.concatenate(rows, axis=0)


def _unpad_w_in(cfg, wpt):
    return jnp.concatenate([wpt[cfg.off[nm]:cfg.off[nm] + width] for nm, width in cfg.ref_segs], axis=0)


def _to_slots_cols(w):
    R = w.shape[0]
    return w.reshape(R, NDEV, -1).transpose(1, 0, 2)


def _from_slots_cols(s):
    return s.transpose(1, 0, 2).reshape(s.shape[1], -1)


def _perm_rows_out(cfg, w):
    return jnp.concatenate([w[cfg.A:], w[:cfg.A]], axis=0)


def _unperm_rows_out(cfg, w):
    return jnp.concatenate([w[cfg.B + cfg.C:], w[:cfg.B + cfg.C]], axis=0)


def _layer_params(cfg, l, g_pre, a_g_v, a_w_s, a_b_s, c_g_q, c_g_kv, g_out):
    A, B = cfg.A, cfg.B
    return dict(g_pre=g_pre[l][None], g_v=a_g_v[l].reshape(1, A), w_s=a_w_s[l], b_s=a_b_s[l][:, :, None],
                g_q=c_g_q[l][None], g_kv=c_g_kv[l][None],
                g_oa=g_out[l][None, :A], g_ob=g_out[l][None, A:A + B], g_oc=g_out[l][None, A + B:])


def _layer_fwd(cfg, l, x, W, p, cos2, sin2, rot, carry_in=None, carry_stick=None, carry_mla=None):
    S, D, A, B, C = cfg.S, cfg.D, cfg.A, cfg.B, cfg.C
    tag = f"l{l}"
    (h,) = _rowwise(_f_pre, [(x, D, 0)], [], [p["g_pre"]], [], [(D, BF16)], 256, f"pre_{tag}")
    if carry_in is None:
        proj = _matmul(h, W["in"], "nt", BF16, f"mm_in_{tag}")
    else:
        proj, moved_in = _matmul(h, W["in"], "nt", BF16, f"mm_in_{tag}", carry=carry_in[0])
        W.update(carry_in[1](*moved_in))
    a_rows = [cfg.view(proj, "ua"), cfg.view(proj, "va"), cfg.view(proj, "za")]
    a_par = [p["g_v"], p["w_s"], p["b_s"], p["g_oa"]]
    (ya,) = _rowwise(_f_gmlp, a_rows, [], a_par, [], [(A, BF16)], LANE, f"gmlp_{tag}")
    qb, kb, vb = cfg.heads_view(proj, "qb"), cfg.heads_view(proj, "kb"), cfg.heads_view(proj, "vb")
    yb, tot, *moved_stick = _stick_fwd(qb, kb, vb, cfg.Hb, LANE ** -0.5, f"stick_fwd_{tag}", *cfg.tiles("stick_fwd", l),
                                       carry=carry_stick)
    (ybg,) = _rowwise(_f_gate, [(yb, B, 0), cfg.view(proj, "zb")], [], [p["g_ob"]], [], [(B, BF16)], 256, f"gate_b_{tag}")
    c_rows = [cfg.view(proj, "cq"), cfg.view(proj, "ckv"), cfg.view(proj, "kr")]
    trig = [(cos2, LANE, 0), (sin2, LANE, 0)]
    cqn, ckvn, krr = _rowwise(_f_cpre, c_rows, trig, [p["g_q"], p["g_kv"]], [rot],
                              [(cfg.Q, BF16), (cfg.KV, BF16), (LANE, BF16)], 256, f"cpre_{tag}")
    q_raw = _matmul(cqn, W["uq"], "nt", BF16, f"mm_uq_{tag}")
    kv = _matmul(ckvn, W["ukv"], "nn", BF16, f"mm_ukv_{tag}")
    r_rows = [(q_raw, 2 * C, 0), (kv, 2 * C, 0), (krr, LANE, 0)]
    q_rot, k_full, v_c = _rowwise(_f_crope, r_rows, trig, [], [rot], [(2 * C, BF16), (2 * C, BF16), (C, BF16)], 128,
                                  f"crope_{tag}")
    qc, kc, vc = (q_rot, 0, 2 * LANE), (k_full, 0, 2 * LANE), (v_c, 0, LANE)
    yc, lse, *moved_mla = _softmax_fwd(qc, kc, vc, cfg.Hc, (LANE + ROPE) ** -0.5, f"mla_fwd_{tag}", *cfg.tiles("mla_fwd", l),
                                       carry=carry_mla)
    (ycg,) = _rowwise(_f_gate, [(yc, C, 0), cfg.view(proj, "zc")], [], [p["g_oc"]], [], [(C, BF16)], 256, f"gate_c_{tag}")
    y = jnp.concatenate([ybg, ycg, ya], axis=1)
    out = _matmul(y, W["out"], "nn", F32, f"mm_out_{tag}", add=x)
    saved = dict(x=x, h=h, proj=proj, yb=yb, tot=tot, cqn=cqn, ckvn=ckvn, krr=krr, q_raw=q_raw, kv=kv,
                 q_rot=q_rot, k_full=k_full, v_c=v_c, yc=yc, lse=lse, y=y)
    return out, saved, (moved_stick[0] if moved_stick else []), (moved_mla[0] if moved_mla else [])


def _layer_bwd(cfg, l, dout, sv, W, p, cos2, sin2, rot, ext_stick, ext_mla, last):
    S, D, A, B, C = cfg.S, cfg.D, cfg.A, cfg.B, cfg.C
    tag = f"l{l}"
    proj = sv["proj"]
    dy = _matmul(dout, W["out"], "nt", BF16, f"mm_dy_{tag}")
    d_wout = _matmul(sv["y"], dout, "tn", BF16, f"mm_dwout_{tag}")
    wout_slots = _unperm_rows_out(cfg, d_wout).reshape(NDEV, cfg.DMIX // NDEV, D)
    (dyb, dzb), (dg_ob,), _ = _rowwise_vjp(_f_gate, [(sv["yb"], B, 0), cfg.view(proj, "zb")], [], [p["g_ob"]], [],
                                           [(dy, B, 0)], [BF16, BF16], 256, f"gate_b_bwd_{tag}")
    (dyc, dzc), (dg_oc,), _ = _rowwise_vjp(_f_gate, [(sv["yc"], C, 0), cfg.view(proj, "zc")], [], [p["g_oc"]], [],
                                           [(dy, C, B // C)], [BF16, BF16], 256, f"gate_c_bwd_{tag}")
    a_rows = [cfg.view(proj, "ua"), cfg.view(proj, "va"), cfg.view(proj, "za")]
    a_par = [p["g_v"], p["w_s"], p["b_s"], p["g_oa"]]
    (dua, dva, dza), (dg_v, dw_s, db_s, dg_oa), _ = _rowwise_vjp(
        _f_gmlp, a_rows, [], a_par, [], [(dy, A, (B + C) // A)], [BF16] * 3, LANE, f"gmlp_bwd_{tag}")
    qb, kb, vb = cfg.heads_view(proj, "qb"), cfg.heads_view(proj, "kb"), cfg.heads_view(proj, "vb")
    dqb, dkb, dvb, moved_stick = _stick_bwd(qb, kb, vb, (dyb, 0, LANE), sv["tot"], cfg.Hb, LANE ** -0.5,
                                            f"stick_bwd_{tag}", *cfg.tiles("stick_bwd", l),
                                            carry=_Exchange([[a] for a in (ext_stick or [wout_slots])], False))
    ext_got = [mv[0] for mv in moved_stick] if ext_stick else []
    ext_mla = ext_mla + ([wout_slots] if ext_stick else [])
    qc, kc, vc = (sv["q_rot"], 0, 2 * LANE), (sv["k_full"], 0, 2 * LANE), (sv["v_c"], 0, LANE)
    dq_rot, dk_full, dv_c, *moved_mla = _softmax_bwd(qc, kc, vc, (sv["yc"], 0, LANE), (dyc, 0, LANE), sv["lse"], cfg.Hc,
                                                     (LANE + ROPE) ** -0.5, f"mla_bwd_{tag}", *cfg.tiles("mla_bwd", l),
                                                     carry=_Exchange([[a] for a in ext_mla], False) if ext_mla else None)
    moved_mla = [mv[0] for mv in (moved_mla[0] if moved_mla else [])]
    got = dict(w_out=moved_mla.pop() if ext_stick else moved_stick[0][0])
    ext_got += moved_mla
    trig = [(cos2, LANE, 0), (sin2, LANE, 0)]
    r_rows = [(sv["q_raw"], 2 * C, 0), (sv["kv"], 2 * C, 0), (sv["krr"], LANE, 0)]
    (dq_raw, dkv, dkrr), _, _ = _rowwise_vjp(_f_crope, r_rows, trig, [], [rot],
                                             [(dq_rot, 2 * C, 0), (dk_full, 2 * C, 0), (dv_c, C, 0)], [BF16] * 3, 128,
                                             f"crope_bwd_{tag}")
    dcqn = _matmul(dq_raw, W["uq"], "nn", BF16, f"mm_dcq_{tag}")
    d_wuq = _matmul(dq_raw, sv["cqn"], "tn", BF16, f"mm_dwuq_{tag}")
    dckvn = _matmul(dkv, W["ukv"], "nt", BF16, f"mm_dckv_{tag}")
    d_wukv = _matmul(sv["ckvn"], dkv, "tn", BF16, f"mm_dwukv_{tag}")
    c_rows = [cfg.view(proj, "cq"), cfg.view(proj, "ckv"), cfg.view(proj, "kr")]
    (dcq, dckv, dkr), (dg_q, dg_kv), _ = _rowwise_vjp(
        _f_cpre, c_rows, trig, [p["g_q"], p["g_kv"]], [rot],
        [(dcqn, cfg.Q, 0), (dckvn, cfg.KV, 0), (dkrr, LANE, 0)], [BF16] * 3, 256, f"cpre_bwd_{tag}")
    parts = dict(ua=dua, va=dva, za=dza, qb=dqb, kb=dkb, vb=dvb, zb=dzb, zc=dzc, cq=dcq, kr=dkr, ckv=dckv)
    cols, pos = [], 0
    for nm, off in sorted(cfg.off.items(), key=lambda kv_: kv_[1]):
        if off > pos:
            cols.append(jnp.zeros((S, off - pos), BF16))
        cols.append(parts[nm])
        pos = off + parts[nm].shape[1]
    if cfg.NP > pos:
        cols.append(jnp.zeros((S, cfg.NP - pos), BF16))
    dproj = jnp.concatenate(cols, axis=1)
    to_send = dict(c_w_uq=d_wuq.reshape(cfg.Hc, 2 * LANE, cfg.Q)[:, :LANE + ROPE].reshape(NDEV, -1, cfg.Q),
                   c_w_ukv=_to_slots_cols(d_wukv))
    if last:
        half = D // 2
        d_win_a = _matmul(dproj, sv["h"][:, :half], "tn", BF16, f"mm_dwin_a_{tag}")
        d_win_b, moved_a = _matmul(dproj, sv["h"][:, half:], "tn", BF16, f"mm_dwin_b_{tag}",
                                   carry=_Exchange([[_unpad_w_in(cfg, d_win_a).reshape(NDEV, -1, half)]], False))
        dh, moved = _matmul(dproj, W["in"], "nn", BF16, f"mm_dh_{tag}",
                            carry=_Exchange([[_unpad_w_in(cfg, d_win_b).reshape(NDEV, -1, D - half)],
                                             [to_send["c_w_uq"]], [to_send["c_w_ukv"]]], False))
        got.update(w_in=(moved_a[0][0], moved[0][0]), c_w_uq=moved[1][0], c_w_ukv=moved[2][0])
        to_send = {}
    else:
        d_win = _matmul(dproj, sv["h"], "tn", BF16, f"mm_dwin_{tag}")
        to_send["w_in"] = _unpad_w_in(cfg, d_win).reshape(NDEV, -1, D)
        dh = _matmul(dproj, W["in"], "nn", BF16, f"mm_dh_{tag}")
    (dx,), (dg_pre,), _ = _rowwise_vjp(_f_pre_res, [(sv["x"], D, 0)], [], [p["g_pre"]], [],
                                       [(dh, D, 0), (dout, D, 0)], [F32], 128, f"pre_bwd_{tag}")
    small = dict(g_pre=dg_pre[0], a_g_v=dg_v.reshape(cfg.G, LANE), a_w_s=dw_s, a_b_s=db_s[:, :, 0], c_g_q=dg_q[0],
                 c_g_kv=dg_kv[0], g_out=jnp.concatenate([dg_oa[0], dg_ob[0], dg_oc[0]]))
    return dx, small, got, to_send, ext_got


def _pack_small(vals):
    pieces = []
    for nm in SMALL:
        piece = vals[nm].reshape(-1, LANE)
        pieces.append(jnp.pad(piece, ((0, -piece.shape[0] % 8), (0, 0))))
    packed = jnp.concatenate(pieces, axis=0)
    return jnp.pad(packed, ((0, -packed.shape[0] % SMALL_ROWS), (0, 0)))


def _unpack_small(packed, like):
    out, row = {}, 0
    for nm in SMALL:
        n = like[nm].size // LANE
        out[nm] = packed[row:row + n].reshape(like[nm].shape)
        row += n + (-n % 8)
    return out


def kernel(x, positions, g_pre, w_in, a_g_v, a_w_s, a_b_s, c_g_q, c_g_kv, c_w_uq, c_w_ukv, g_out, w_out, g_final, loss_target, m_g_pre, m_w_in, m_a_g_v, m_a_w_s, m_a_b_s, m_c_g_q, m_c_g_kv, m_c_w_uq, m_c_w_ukv, m_g_out, m_w_out, m_g_final, v_g_pre, v_w_in, v_a_g_v, v_a_w_s, v_a_b_s, v_c_g_q, v_c_g_kv, v_c_w_uq, v_c_w_ukv, v_g_out, v_w_out, v_g_final):
    depth, S, D = w_in.shape[0], x.shape[1], x.shape[2]
    cfg = _Cfg(S, D, a_g_v.shape[1], c_g_q.shape[1], c_g_kv.shape[1], c_w_ukv.shape[2] * NDEV // (2 * LANE), g_out.shape[1])
    weights = dict(g_pre=g_pre, w_in=w_in, a_g_v=a_g_v, a_w_s=a_w_s, a_b_s=a_b_s, c_g_q=c_g_q, c_g_kv=c_g_kv,
                   c_w_uq=c_w_uq, c_w_ukv=c_w_ukv, g_out=g_out, w_out=w_out, g_final=g_final)
    mom_m = dict(g_pre=m_g_pre, w_in=m_w_in, a_g_v=m_a_g_v, a_w_s=m_a_w_s, a_b_s=m_a_b_s, c_g_q=m_c_g_q, c_g_kv=m_c_g_kv,
                 c_w_uq=m_c_w_uq, c_w_ukv=m_c_w_ukv, g_out=m_g_out, w_out=m_w_out, g_final=m_g_final)
    mom_v = dict(g_pre=v_g_pre, w_in=v_w_in, a_g_v=v_a_g_v, a_w_s=v_a_w_s, a_b_s=v_a_b_s, c_g_q=v_c_g_q, c_g_kv=v_c_g_kv,
                 c_w_uq=v_c_w_uq, c_w_ukv=v_c_w_ukv, g_out=v_g_out, w_out=v_w_out, g_final=v_g_final)
    big_names = ("w_in", "c_w_uq", "c_w_ukv", "w_out")

    inv_freq = 1.0 / (ROPE_THETA ** (jnp.arange(0, ROPE, 2, dtype=F32) / ROPE))
    ang = positions[0].astype(F32)[:, None] * inv_freq
    zpad = jnp.zeros((S, LANE - ROPE), F32)
    cos2 = jnp.concatenate([jnp.cos(ang), jnp.cos(ang), zpad], axis=1)
    sin2 = jnp.concatenate([jnp.sin(ang), jnp.sin(ang), zpad], axis=1)
    rot = _rope_matrix()

    for tree in (weights, mom_m, mom_v):
        for nm in TRANSPOSED:
            tree[nm] = jnp.swapaxes(tree[nm], 1, 2)

    def shards(l, names):
        return [[weights[nm][l].astype(BF16)] for nm in names]

    def assemble_rest(g_uq, g_ukv, g_wout):
        uq = jnp.pad(g_uq[0].reshape(cfg.Hc, LANE + ROPE, cfg.Q), ((0, 0), (0, LANE - ROPE), (0, 0)))
        return {"uq": uq.reshape(2 * cfg.C, cfg.Q), "ukv": _from_slots_cols(g_ukv[0]),
                "out": _perm_rows_out(cfg, g_wout[0].reshape(cfg.DMIX, D))}

    params = [_layer_params(cfg, l, g_pre, a_g_v, a_w_s, a_b_s, c_g_q, c_g_kv, g_out) for l in range(depth)]

    got_in = _exchange(_GatherTwoLevel(shards(0, big_names[:1])), "gather_w_in_l0")
    got_rest = None
    hcur, saved, Ws = x[0], [], []
    for l in range(depth):
        Ws.append({"in": _pad_w_in(cfg, got_in[0][0].reshape(-1, D))})
        if got_rest is not None:
            Ws[l].update(assemble_rest(*got_rest))
        nxt = l + 1 < depth
        hcur, sv, got_in, got_rest = _layer_fwd(
            cfg, l, hcur, Ws[l], params[l], cos2, sin2, rot,
            carry_in=None if got_rest is not None else (_GatherTwoLevel(shards(l, big_names[1:])), assemble_rest),
            carry_stick=_GatherTwoLevel(shards(l + 1, big_names[:1])) if nxt else None,
            carry_mla=_GatherTwoLevel(shards(l + 1, big_names[1:])) if nxt else None)
        saved.append(sv)
    (dh,), (dg_final,), (loss_rows,) = _rowwise_vjp(
        _f_final, [(hcur, D, 0)], [(loss_target[0], D, 0)], [g_final[None]], [], [(jnp.ones((S, 1), F32), 1, 0)],
        [F32], 128, "final", primal=[(1, F32)])
    loss = lax.psum(jnp.sum(loss_rows), MESH_AXES)

    small_g, slots, pending = [None] * depth, [None] * depth, {}
    for l in reversed(range(depth)):
        ext_stick = [pending["w_in"]] if pending else []
        ext_mla = [pending["c_w_uq"], pending["c_w_ukv"]] if pending else []
        dh, small_g[l], slots[l], pending, ext_got = _layer_bwd(cfg, l, dh, saved[l], Ws[l], params[l], cos2, sin2, rot,
                                                                ext_stick, ext_mla, l == 0)
        if ext_got:
            slots[l + 1].update(w_in=ext_got[0], c_w_uq=ext_got[1], c_w_ukv=ext_got[2])
    grad_x = dh[None]
    small_grads = {nm: jnp.stack([small_g[l][nm] for l in range(depth)]) for nm in SMALL if nm != "g_final"}
    small_grads["g_final"] = dg_final[0]
    (small_slots,) = _exchange(_Exchange([[_pack_small(small_grads)]], True), "gather_small_grads")

    res = {}
    for nm in big_names:
        res[nm] = None
        for l in range(depth):
            parts = slots[l][nm] if isinstance(slots[l][nm], tuple) else (slots[l][nm],)
            col0 = 0
            for i, part in enumerate(parts):
                res[nm] = _adamw(part, weights[nm], mom_m[nm], mom_v[nm], l, res[nm], f"adamw_{nm}_l{l}_{i}", col0)
                col0 += part.shape[2]
        if nm in TRANSPOSED:
            res[nm] = [jnp.swapaxes(r, 1, 2) for r in res[nm]]
    packed = _adamw(small_slots[0], _pack_small(weights)[None], _pack_small(mom_m)[None], _pack_small(mom_v)[None], 0, None,
                    "adamw_small")
    small_res = [_unpack_small(r[0], weights) for r in packed]
    order = ("g_pre", "w_in", "a_g_v", "a_w_s", "a_b_s", "c_g_q", "c_g_kv", "c_w_uq", "c_w_ukv", "g_out", "w_out", "g_final")
    outs = [loss, grad_x]
    for kind in range(4):
        outs += [small_res[kind][nm] if nm in SMALL else res[nm][kind] for nm in order]
    return tuple(outs)
```

```python
import functools

import numpy as np
import jax
import jax.numpy as jnp
from jax import lax
from jax.experimental import pallas as pl
from jax.experimental.pallas import tpu as pltpu

NDEV = 8
MESH_AXES = ("x", "y", "c")
LANE = 128
ROPE = 64
EPS = 1e-6
ROPE_THETA = 10000.0
ADAM_LR, ADAM_B1, ADAM_B2, ADAM_EPS, ADAM_WD, ADAM_STEP = 0.001, 0.9, 0.999, 1e-08, 0.01, 10
VMEM_LIMIT = 48 * 1024 * 1024
ADAM_TILE_BYTES = 768 * 1024
CARRY_MID_PERCENT = 80
SMALL_ROWS = 256
ATTN_TILES = {"stick_fwd": [(2048, 256)], "stick_bwd": [(2048, 256)], "mla_fwd": [(512, 1024)], "mla_bwd": [(2048, 512)]}
DIAG_BLOCK = 256
F32, BF16 = jnp.float32, jnp.bfloat16
SMALL = ("g_pre", "a_g_v", "a_w_s", "a_b_s", "c_g_q", "c_g_kv", "g_out", "g_final")
TRANSPOSED = ("w_in", "c_w_uq")


def _tile(dim, cap, mult=LANE):
    if dim <= cap:
        return dim
    t = (cap // mult) * mult
    while t >= mult:
        if dim % t == 0:
            return t
        t -= mult
    return dim


def _dot_nt(a, b):
    return lax.dot_general(a, b, (((1,), (1,)), ((), ())), preferred_element_type=F32)


def _dot_tn(a, b):
    return lax.dot_general(a, b, (((0,), (0,)), ((), ())), preferred_element_type=F32)


def _dot(a, b):
    return jnp.dot(a, b, preferred_element_type=F32)


class _Exchange:
    def __init__(self, groups, gather):
        self.groups, self.gather = groups, gather
        self.flat = [(gi, li, a) for gi, grp in enumerate(groups) for li, a in enumerate(grp)]
        self.n = len(self.flat)
        self.args = [a for (_, _, a) in self.flat]
        self.out_shape = [jax.ShapeDtypeStruct((len(grp), NDEV) + tuple(grp[0].shape[-2:]), grp[0].dtype) for grp in groups]
        self.scratch = [pltpu.SemaphoreType.DMA((self.n, NDEV - 1)), pltpu.SemaphoreType.DMA((self.n, NDEV - 1)),
                        pltpu.SemaphoreType.DMA((self.n,))]

    def _copies(self, ins, outs, send_sems, recv_sems, local_sems, landings):
        x, y, c = lax.axis_index("x"), lax.axis_index("y"), lax.axis_index("c")
        me = 4 * x + 2 * y + c
        owns = [pltpu.make_async_copy(ins[i] if self.gather else ins[i].at[me], outs[gi].at[li, me], local_sems.at[i])
                for i, (gi, li, _) in enumerate(self.flat)]
        pairs = []
        for k in range(1, NDEV):
            px = 1 - x if k & 4 else x
            py = 1 - y if k & 2 else y
            pc = 1 - c if k & 1 else c
            peer = 4 * px + 2 * py + pc
            for i, (gi, li, _) in enumerate(self.flat):
                src = ins[i] if self.gather else ins[i].at[peer]
                sems = dict(send_sem=send_sems.at[i, k - 1], recv_sem=recv_sems.at[i, k - 1],
                            device_id=(px, py, pc), device_id_type=pl.DeviceIdType.MESH)
                out = pltpu.make_async_remote_copy(src_ref=src, dst_ref=outs[gi].at[li, me], **sems)
                landing = pltpu.make_async_remote_copy(src_ref=src, dst_ref=outs[gi].at[li, peer], **sems) if landings else None
                pairs.append((out, landing))
        return owns, pairs

    def start(self, ins, outs, sems):
        owns, pairs = self._copies(ins, outs, *sems, landings=False)
        for own in owns:
            own.start()
        for out, _ in pairs:
            out.start()

    def mid(self, ins, outs, sems):
        pass

    def wait(self, ins, outs, sems):
        owns, pairs = self._copies(ins, outs, *sems, landings=True)
        for out, landing in pairs:
            out.wait_send()
            landing.wait_recv()
        for own in owns:
            own.wait()


class _GatherTwoLevel(_Exchange):
    def __init__(self, groups):
        super().__init__(groups, True)

    def _copy(self, i, k, ins, outs, send_sems, recv_sems, landing):
        gi, li, _ = self.flat[i]
        x, y, c = lax.axis_index("x"), lax.axis_index("y"), lax.axis_index("c")
        chips = [(x, y), (1 - x, y), (x, 1 - y), (1 - x, 1 - y)]

        def slot(chip, core):
            return outs[gi].at[li, 4 * chip[0] + 2 * chip[1] + core]

        if k == 0:
            to, src, dst, lands = (x, y, 1 - c), ins[i], slot(chips[0], c), slot(chips[0], 1 - c)
        elif k <= 3:
            to, src, dst, lands = (*chips[k], c), ins[i], slot(chips[0], c), slot(chips[k], c)
        else:
            to, src, dst, lands = (x, y, 1 - c), slot(chips[k - 3], c), slot(chips[k - 3], c), slot(chips[k - 3], 1 - c)
        return pltpu.make_async_remote_copy(src_ref=src, dst_ref=lands if landing else dst, send_sem=send_sems.at[i, k],
                                            recv_sem=recv_sems.at[i, k], device_id=to, device_id_type=pl.DeviceIdType.MESH)

    def _own(self, i, ins, outs, local_sems):
        gi, li, _ = self.flat[i]
        me = 4 * lax.axis_index("x") + 2 * lax.axis_index("y") + lax.axis_index("c")
        return pltpu.make_async_copy(ins[i], outs[gi].at[li, me], local_sems.at[i])

    def start(self, ins, outs, sems):
        send_sems, recv_sems, local_sems = sems
        for i in range(self.n):
            self._own(i, ins, outs, local_sems).start()
        for k in range(4):
            for i in range(self.n):
                self._copy(i, k, ins, outs, send_sems, recv_sems, False).start()

    def mid(self, ins, outs, sems):
        send_sems, recv_sems, _ = sems
        for k in range(1, 4):
            for i in range(self.n):
                self._copy(i, k, ins, outs, send_sems, recv_sems, True).wait_recv()
                self._copy(i, k + 3, ins, outs, send_sems, recv_sems, False).start()

    def wait(self, ins, outs, sems):
        send_sems, recv_sems, local_sems = sems
        for k in (0, 4, 5, 6):
            for i in range(self.n):
                self._copy(i, k, ins, outs, send_sems, recv_sems, True).wait_recv()
        for k in range(NDEV - 1):
            for i in range(self.n):
                self._copy(i, k, ins, outs, send_sems, recv_sems, False).wait_send()
        for i in range(self.n):
            self._own(i, ins, outs, local_sems).wait()


def _call(body, name, grid, in_specs, out_specs, out_shape, scratch, semantics, args, carry=None):
    n_in, n_out, n_scr = len(in_specs), len(out_specs), len(scratch)
    if carry is None:
        run = body
    else:
        semantics = ("arbitrary",) * len(grid)
        anyspec = pl.BlockSpec(memory_space=pl.ANY)
        in_specs = list(in_specs) + [anyspec] * carry.n
        out_specs = list(out_specs) + [anyspec] * len(carry.groups)
        out_shape = list(out_shape) + carry.out_shape
        scratch = list(scratch) + carry.scratch
        args = list(args) + carry.args

        def run(*refs):
            c_in, x_in = refs[:n_in], refs[n_in:n_in + carry.n]
            rest = refs[n_in + carry.n:]
            c_out, x_out = rest[:n_out], rest[n_out:n_out + len(carry.groups)]
            c_scr, sems = rest[n_out + len(carry.groups):len(rest) - 3], rest[len(rest) - 3:]
            step, total = 0, 1
            for d, extent in enumerate(grid):
                step = step * extent + pl.program_id(d)
                total *= extent

            @pl.when(step == 0)
            def _():
                carry.start(x_in, x_out, sems)

            body(*c_in, *c_out, *c_scr)

            @pl.when(step == (total * CARRY_MID_PERCENT) // 100)
            def _():
                carry.mid(x_in, x_out, sems)

            @pl.when(step == total - 1)
            def _():
                carry.wait(x_in, x_out, sems)

    res = pl.pallas_call(
        run, name=name, grid=grid, out_shape=list(out_shape), in_specs=list(in_specs), out_specs=list(out_specs),
        scratch_shapes=list(scratch),
        compiler_params=pltpu.CompilerParams(dimension_semantics=semantics, vmem_limit_bytes=VMEM_LIMIT,
                                             has_side_effects=carry is not None),
    )(*args)
    return list(res[:n_out]), list(res[n_out:])


def _exchange(ex, name):
    groups = ex.groups

    def body(*refs):
        ins, outs, sems = refs[:ex.n], refs[ex.n:ex.n + len(groups)], refs[ex.n + len(groups):]
        ex.start(ins, outs, sems)
        ex.mid(ins, outs, sems)
        ex.wait(ins, outs, sems)

    anyspec = pl.BlockSpec(memory_space=pl.ANY)
    return pl.pallas_call(
        body, name=name, out_shape=ex.out_shape, in_specs=[anyspec] * ex.n, out_specs=[anyspec] * len(groups),
        scratch_shapes=ex.scratch, compiler_params=pltpu.CompilerParams(has_side_effects=True),
    )(*ex.args)


def _matmul(a, b, mode, out_dtype, name, add=None, tm=1024, tn=1024, tk=2048, carry=None):
    if mode == "tn":
        (K, M), (K2, N) = a.shape, b.shape
    elif mode == "nt":
        (M, K), (N, K2) = a.shape, b.shape
    else:
        (M, K), (K2, N) = a.shape, b.shape
    assert K == K2, (a.shape, b.shape, mode)
    tm, tn, tk = _tile(M, tm), _tile(N, tn), _tile(K, tk)
    nk = K // tk
    a_spec = pl.BlockSpec((tk, tm), lambda i, j, k: (k, i)) if mode == "tn" else pl.BlockSpec((tm, tk), lambda i, j, k: (i, k))
    b_spec = pl.BlockSpec((tn, tk), lambda i, j, k: (j, k)) if mode == "nt" else pl.BlockSpec((tk, tn), lambda i, j, k: (k, j))
    dot = {"nn": _dot, "nt": _dot_nt, "tn": _dot_tn}[mode]
    has_add = add is not None

    def body(*refs):
        a_ref, b_ref = refs[0], refs[1]
        o_ref, acc = refs[-2], refs[-1]
        k = pl.program_id(2)

        @pl.when(k == 0)
        def _():
            acc[...] = jnp.zeros_like(acc)

        acc[...] += dot(a_ref[...].astype(BF16), b_ref[...].astype(BF16))

        @pl.when(k == nk - 1)
        def _():
            r = acc[...]
            if has_add:
                r = r + refs[2][...]
            o_ref[...] = r.astype(o_ref.dtype)

    in_specs = [a_spec, b_spec]
    args = [a, b]
    if has_add:
        in_specs.append(pl.BlockSpec((tm, tn), lambda i, j, k: (i, j)))
        args.append(add)
    (out,), moved = _call(body, name, (M // tm, N // tn, nk), in_specs, [pl.BlockSpec((tm, tn), lambda i, j, k: (i, j))],
                          [jax.ShapeDtypeStruct((M, N), out_dtype)], [pltpu.VMEM((tm, tn), F32)],
                          ("parallel", "parallel", "arbitrary"), args, carry)
    return out if carry is None else (out, moved)


def _row_specs(views, tile):
    return [pl.BlockSpec((tile, w), functools.partial(lambda i, cb: (i, cb), cb=cb)) for (_, w, cb) in views]


def _full_specs(arrs):
    return [pl.BlockSpec(p.shape, functools.partial(lambda i, nd: (0,) * nd, nd=p.ndim)) for p in arrs]


def _rowwise(fn, rows, aux, params, consts, outs, tile, name):
    S = rows[0][0].shape[0]
    nr, na, npar, nc = len(rows), len(aux), len(params), len(consts)

    def body(*refs):
        ins = [r[...].astype(F32) for r in refs[:nr + na]]
        small = [r[...] for r in refs[nr + na:nr + na + npar + nc]]
        res = fn(*ins, *small)
        for o_ref, r in zip(refs[nr + na + npar + nc:], res):
            o_ref[...] = r.astype(o_ref.dtype)

    return pl.pallas_call(
        body, name=name, grid=(S // tile,),
        out_shape=[jax.ShapeDtypeStruct((S, w), dt) for (w, dt) in outs],
        in_specs=_row_specs(rows + aux, tile) + _full_specs(params + consts),
        out_specs=[pl.BlockSpec((tile, w), lambda i: (i, 0)) for (w, _) in outs],
        compiler_params=pltpu.CompilerParams(dimension_semantics=("parallel",), vmem_limit_bytes=VMEM_LIMIT),
    )(*[v[0] for v in rows + aux], *params, *consts)


def _rowwise_vjp(fn, rows, aux, params, consts, cots, grad_dtypes, tile, name, primal=()):
    S = rows[0][0].shape[0]
    nr, na, npar, nc, nct, npr = len(rows), len(aux), len(params), len(consts), len(cots), len(primal)

    def body(*refs):
        n_in = nr + na + npar + nc + nct
        rv = [r[...].astype(F32) for r in refs[:nr]]
        av = [r[...].astype(F32) for r in refs[nr:nr + na]]
        pv = [r[...] for r in refs[nr + na:nr + na + npar]]
        cv = [r[...] for r in refs[nr + na + npar:nr + na + npar + nc]]
        ct = tuple(r[...].astype(F32) for r in refs[nr + na + npar + nc:n_in])
        res, vjp = jax.vjp(lambda *rp: tuple(fn(*rp[:nr], *av, *rp[nr:], *cv)), *rv, *pv)
        grads = vjp(ct)
        g_refs = refs[n_in:n_in + nr]
        p_refs = refs[n_in + nr:n_in + nr + npar]
        o_refs = refs[n_in + nr + npar:]
        for g_ref, g in zip(g_refs, grads[:nr]):
            g_ref[...] = g.astype(g_ref.dtype)

        @pl.when(pl.program_id(0) == 0)
        def _():
            for p_ref in p_refs:
                p_ref[...] = jnp.zeros_like(p_ref)

        for p_ref, g in zip(p_refs, grads[nr:]):
            p_ref[...] += g
        for o_ref, r in zip(o_refs, res[:npr]):
            o_ref[...] = r.astype(o_ref.dtype)

    out_shape = ([jax.ShapeDtypeStruct((S, w), dt) for (_, w, _), dt in zip(rows, grad_dtypes)]
                 + [jax.ShapeDtypeStruct(p.shape, F32) for p in params]
                 + [jax.ShapeDtypeStruct((S, w), dt) for (w, dt) in primal])
    out_specs = ([pl.BlockSpec((tile, w), lambda i: (i, 0)) for (_, w, _) in rows] + _full_specs(params)
                 + [pl.BlockSpec((tile, w), lambda i: (i, 0)) for (w, _) in primal])
    res = pl.pallas_call(
        body, name=name, grid=(S // tile,), out_shape=out_shape,
        in_specs=_row_specs(rows + aux, tile) + _full_specs(params + consts) + _row_specs(cots, tile),
        out_specs=out_specs,
        compiler_params=pltpu.CompilerParams(dimension_semantics=("arbitrary",), vmem_limit_bytes=VMEM_LIMIT),
    )(*[v[0] for v in rows + aux], *params, *consts, *[v[0] for v in cots])
    return res[:nr], res[nr:nr + npar], res[nr + npar:]


@jax.custom_vjp
def _mm(a, b):
    return _dot(a.astype(BF16), b.astype(BF16))


def _mm_fwd(a, b):
    return _mm(a, b), (a, b)


def _mm_bwd(res, ct):
    a, b = res
    ctb = ct.astype(BF16)
    return _dot_nt(ctb, b.astype(BF16)), _dot_tn(a.astype(BF16), ctb)


_mm.defvjp(_mm_fwd, _mm_bwd)


def _rms(x, g):
    return x * lax.rsqrt(jnp.mean(x * x, axis=-1, keepdims=True) + EPS) * g


def _f_pre(x, g):
    return (_rms(x, g),)


def _f_pre_res(x, g):
    return _rms(x, g), x


def _f_gate(y, z, g):
    return (_rms(y, g) * jax.nn.silu(z),)


def _f_gmlp(u, v, z, g_v, w_s, b_s, g_o):
    groups = w_s.shape[0]
    u, v = jax.nn.gelu(u), jax.nn.gelu(v)
    t_idx = lax.broadcasted_iota(jnp.int32, (LANE, LANE), 0)
    s_idx = lax.broadcasted_iota(jnp.int32, (LANE, LANE), 1)
    ys = []
    for g in range(groups):
        sl = slice(g * LANE, (g + 1) * LANE)
        vn = _rms(v[:, sl], g_v[:, sl])
        w = jnp.where(s_idx <= t_idx, w_s[g], 0.0)
        ys.append(u[:, sl] * (_mm(w, vn) + b_s[g]))
    return (_rms(jnp.concatenate(ys, axis=1), g_o) * jax.nn.silu(z),)


def _rope(x, cos2, sin2, rot):
    return x * cos2 + _mm(x, rot) * sin2


def _f_cpre(cq, ckv, kr, cos2, sin2, g_q, g_kv, rot):
    return _rms(cq, g_q), _rms(ckv, g_kv), _rope(kr, cos2, sin2, rot)


def _f_crope(q, kv, krr, cos2, sin2, rot):
    heads = q.shape[1] // (2 * LANE)
    qs, ks, vs = [], [], []
    for h in range(heads):
        lo, mid, hi = 2 * h * LANE, (2 * h + 1) * LANE, (2 * h + 2) * LANE
        qs += [q[:, lo:mid], _rope(q[:, mid:hi], cos2, sin2, rot)]
        ks += [kv[:, lo:mid], krr]
        vs += [kv[:, mid:hi]]
    return jnp.concatenate(qs, axis=1), jnp.concatenate(ks, axis=1), jnp.concatenate(vs, axis=1)


def _f_final(h, target, g):
    err = _rms(h, g) - target
    return (0.5 * jnp.mean(err * err, axis=-1, keepdims=True),)


def _rope_matrix():
    r = np.zeros((LANE, LANE), np.float32)
    half = ROPE // 2
    for i in range(half):
        r[i + half, i] = -1.0
        r[i, i + half] = 1.0
    return jnp.asarray(r)


def _head_spec(view, rows, n_rows_block):
    _, cb0, w = view
    if n_rows_block:
        return pl.BlockSpec((rows, w), functools.partial(lambda h, i, cb0: (i, cb0 + h), cb0=cb0))
    return pl.BlockSpec((rows, w), functools.partial(lambda h, i, cb0: (0, cb0 + h), cb0=cb0))


def _stat_spec(tq):
    return pl.BlockSpec((1, tq, 1), lambda h, i: (h, i, 0))


def _softplus(z):
    return jnp.maximum(z, 0.0) + jnp.log(1.0 + jnp.exp(-jnp.abs(z)))


def _cumsum_mm(x, m01):
    hi = x.astype(BF16)
    lo = (x - hi.astype(F32)).astype(BF16)
    return _dot(hi, m01) + _dot(lo, m01)


def _attn_call(body, name, heads, S, tq, ins, in_blocked, outs, out_blocked, scratch, stats_in=0, stats_out=0, carry=None):
    in_specs = [_head_spec(v, tq if blk else S, blk) for v, blk in zip(ins[:len(ins) - stats_in], in_blocked)]
    in_specs += [_stat_spec(tq)] * stats_in
    out_specs = [_head_spec((None, 0, w), tq if blk else S, blk) for (w, _), blk in zip(outs, out_blocked)]
    out_specs += [_stat_spec(tq)] * stats_out
    out_shape = [jax.ShapeDtypeStruct((S, heads * w), dt) for (w, dt) in outs]
    out_shape += [jax.ShapeDtypeStruct((heads, S, 1), F32)] * stats_out
    args = [v[0] for v in ins[:len(ins) - stats_in]] + list(ins[len(ins) - stats_in:])
    res, moved = _call(body, name, (heads, S // tq), in_specs, out_specs, out_shape, scratch, ("arbitrary", "arbitrary"),
                       args, carry)
    return res if carry is None else res + [moved]


def _softmax_fwd(q, k, v, heads, scale, name, tq, bk, carry=None):
    S, dv = q[0].shape[0], v[2]

    def body(q_ref, k_ref, v_ref, o_ref, lse_ref):
        qi = pl.program_id(1)
        qv = q_ref[...]
        row = qi * tq + lax.broadcasted_iota(jnp.int32, (tq, bk), 0)
        col0 = lax.broadcasted_iota(jnp.int32, (tq, bk), 1)

        def step(kb, carry):
            m, l, acc = carry
            sl = pl.ds(pl.multiple_of(kb * bk, bk), bk)
            s = _dot_nt(qv, k_ref[sl, :]) * scale
            s = jnp.where(kb * bk + col0 <= row, s, -1e30)
            m_new = jnp.maximum(m, jnp.max(s, axis=1, keepdims=True))
            p = jnp.exp(s - m_new)
            alpha = jnp.exp(m - m_new)
            l = alpha * l + jnp.sum(p, axis=1, keepdims=True)
            acc = alpha * acc + _dot(p.astype(BF16), v_ref[sl, :])
            return m_new, l, acc

        n_kb = (qi * tq + tq + bk - 1) // bk
        m, l, acc = lax.fori_loop(0, n_kb, step, (jnp.full((tq, 1), -1e30, F32), jnp.zeros((tq, 1), F32),
                                                  jnp.zeros((tq, dv), F32)))
        o_ref[...] = (acc / l).astype(o_ref.dtype)
        lse_ref[0] = m + jnp.log(l)

    return _attn_call(body, name, heads, S, tq, [q, k, v], [1, 0, 0], [(dv, BF16)], [1], [], stats_out=1, carry=carry)


def _softmax_bwd(q, k, v, o, do, lse, heads, scale, name, tq, bk, carry=None):
    S, dq_w, dv = q[0].shape[0], q[2], v[2]
    nq = S // tq

    bd = min(DIAG_BLOCK, tq)
    assert tq % bk == 0 and tq % bd == 0

    def body(q_ref, k_ref, v_ref, o_ref, do_ref, lse_ref, dq_ref, dk_ref, dv_ref, dk_acc, dv_acc, delta_scr, dq_scr):
        qi = pl.program_id(1)

        @pl.when(qi == 0)
        def _():
            dk_acc[...] = jnp.zeros_like(dk_acc)
            dv_acc[...] = jnp.zeros_like(dv_acc)

        delta_scr[...] = jnp.sum(do_ref[...].astype(F32) * o_ref[...].astype(F32), axis=1, keepdims=True)
        dq_scr[...] = jnp.zeros_like(dq_scr)

        def block(r0, sl, width, masked):
            qv, dov = q_ref[r0:, :], do_ref[r0:, :]
            ks, vs = k_ref[sl, :], v_ref[sl, :]
            p = jnp.exp(_dot_nt(qv, ks) * scale - lse_ref[0, r0:, :])
            if masked:
                shape = (tq - r0, width)
                p = jnp.where(lax.broadcasted_iota(jnp.int32, shape, 1) <= lax.broadcasted_iota(jnp.int32, shape, 0), p, 0.0)
            ds = (p * (_dot_nt(dov, vs) - delta_scr[r0:, :]) * scale).astype(BF16)
            dk_acc[sl, :] += _dot_tn(ds, qv)
            dv_acc[sl, :] += _dot_tn(p.astype(BF16), dov)
            dq_scr[r0:, :] += _dot(ds, ks)

        def step(kb, _):
            block(0, pl.ds(pl.multiple_of(kb * bk, bk), bk), bk, False)
            return 0

        lax.fori_loop(0, qi * (tq // bk), step, 0)
        for j in range(tq // bd):
            block(j * bd, pl.ds(pl.multiple_of(qi * tq + j * bd, bd), bd), bd, True)
        dq_ref[...] = dq_scr[...].astype(dq_ref.dtype)

        @pl.when(qi == nq - 1)
        def _():
            dk_ref[...] = dk_acc[...].astype(dk_ref.dtype)
            dv_ref[...] = dv_acc[...].astype(dv_ref.dtype)

    return _attn_call(body, name, heads, S, tq, [q, k, v, o, do, lse], [1, 0, 0, 1, 1],
                      [(dq_w, BF16), (dq_w, BF16), (dv, BF16)], [1, 0, 0],
                      [pltpu.VMEM((S, dq_w), F32), pltpu.VMEM((S, dv), F32), pltpu.VMEM((tq, 1), F32),
                       pltpu.VMEM((tq, dq_w), F32)], stats_in=1, carry=carry)


def _stick_fwd(q, k, v, heads, scale, name, tq, bk, carry=None):
    S, dv = q[0].shape[0], v[2]

    assert tq % bk == 0
    n_sub = tq // bk

    def body(q_ref, k_ref, v_ref, o_ref, tot_ref, c_scr, acc_scr):
        qi = pl.program_id(1)
        m_gt = (lax.broadcasted_iota(jnp.int32, (bk, bk), 0) > lax.broadcasted_iota(jnp.int32, (bk, bk), 1)).astype(BF16)

        def block(r0, sl, masked):
            rows = tq - r0
            z = _dot_nt(q_ref[r0:, :], k_ref[sl, :]) * scale
            sp = _softplus(z)
            lk = -sp
            if masked:
                mask = lax.broadcasted_iota(jnp.int32, (rows, bk), 1) < lax.broadcasted_iota(jnp.int32, (rows, bk), 0)
                lk = jnp.where(mask, lk, 0.0)
            after = _cumsum_mm(lk, m_gt) + c_scr[r0:, :]
            a = jnp.exp(z - sp + after)
            if masked:
                a = jnp.where(mask, a, 0.0)
            acc_scr[r0:, :] += _dot(a.astype(BF16), v_ref[sl, :])
            c_scr[r0:, :] += jnp.sum(lk, axis=1, keepdims=True)

        c_scr[...] = jnp.zeros_like(c_scr)
        acc_scr[...] = jnp.zeros_like(acc_scr)
        for j in reversed(range(n_sub)):
            block(j * bk, pl.ds(pl.multiple_of(qi * tq + j * bk, bk), bk), True)

        def step(it, _):
            block(0, pl.ds(pl.multiple_of((qi * n_sub - 1 - it) * bk, bk), bk), False)
            return 0

        lax.fori_loop(0, qi * n_sub, step, 0)
        o_ref[...] = acc_scr[...].astype(o_ref.dtype)
        tot_ref[0] = c_scr[...]

    return _attn_call(body, name, heads, S, tq, [q, k, v], [1, 0, 0], [(dv, BF16)], [1],
                      [pltpu.VMEM((tq, 1), F32), pltpu.VMEM((tq, dv), F32)], stats_out=1, carry=carry)


def _stick_bwd(q, k, v, do, tot, heads, scale, name, tq, bk, carry=None):
    S, dq_w, dv = q[0].shape[0], q[2], v[2]
    nq = S // tq

    assert tq % bk == 0
    n_sub = tq // bk

    def body(q_ref, k_ref, v_ref, do_ref, tot_ref, dq_ref, dk_ref, dv_ref, dk_acc, dv_acc, pc_scr, gc_scr, dq_scr):
        qi = pl.program_id(1)

        @pl.when(qi == 0)
        def _():
            dk_acc[...] = jnp.zeros_like(dk_acc)
            dv_acc[...] = jnp.zeros_like(dv_acc)

        j_idx = lax.broadcasted_iota(jnp.int32, (bk, bk), 0)
        s_idx = lax.broadcasted_iota(jnp.int32, (bk, bk), 1)
        m_le, m_lt = (j_idx <= s_idx).astype(BF16), (j_idx < s_idx).astype(BF16)

        def block(r0, sl, masked):
            rows = tq - r0
            qv, dov = q_ref[r0:, :], do_ref[r0:, :]
            ks, vs = k_ref[sl, :], v_ref[sl, :]
            z = _dot_nt(qv, ks) * scale
            sp = _softplus(z)
            lk = -sp
            if masked:
                mask = lax.broadcasted_iota(jnp.int32, (rows, bk), 1) < lax.broadcasted_iota(jnp.int32, (rows, bk), 0)
                lk = jnp.where(mask, lk, 0.0)
            after = tot_ref[0, r0:, :] - pc_scr[r0:, :] - _cumsum_mm(lk, m_le)
            log_beta = z - sp
            a = jnp.exp(log_beta + after)
            if masked:
                a = jnp.where(mask, a, 0.0)
            g = _dot_nt(dov, vs) * a
            cg = gc_scr[r0:, :] + _cumsum_mm(g, m_lt)
            dz = g * jnp.exp(-sp) - jnp.exp(log_beta) * cg
            if masked:
                dz = jnp.where(mask, dz, 0.0)
            dz = (dz * scale).astype(BF16)
            dk_acc[sl, :] += _dot_tn(dz, qv)
            dv_acc[sl, :] += _dot_tn(a.astype(BF16), dov)
            dq_scr[r0:, :] += _dot(dz, ks)
            pc_scr[r0:, :] += jnp.sum(lk, axis=1, keepdims=True)
            gc_scr[r0:, :] += jnp.sum(g, axis=1, keepdims=True)

        pc_scr[...] = jnp.zeros_like(pc_scr)
        gc_scr[...] = jnp.zeros_like(gc_scr)
        dq_scr[...] = jnp.zeros_like(dq_scr)

        def step(kb, _):
            block(0, pl.ds(pl.multiple_of(kb * bk, bk), bk), False)
            return 0

        lax.fori_loop(0, qi * n_sub, step, 0)
        for j in range(n_sub):
            block(j * bk, pl.ds(pl.multiple_of(qi * tq + j * bk, bk), bk), True)
        dq_ref[...] = dq_scr[...].astype(dq_ref.dtype)

        @pl.when(qi == nq - 1)
        def _():
            dk_ref[...] = dk_acc[...].astype(dk_ref.dtype)
            dv_ref[...] = dv_acc[...].astype(dv_ref.dtype)

    return _attn_call(body, name, heads, S, tq, [q, k, v, do, tot], [1, 0, 0, 1],
                      [(dq_w, BF16), (dq_w, BF16), (dv, BF16)], [1, 0, 0],
                      [pltpu.VMEM((S, dq_w), F32), pltpu.VMEM((S, dv), F32), pltpu.VMEM((tq, 1), F32),
                       pltpu.VMEM((tq, 1), F32), pltpu.VMEM((tq, dq_w), F32)], stats_in=1, carry=carry)


def _adamw(slots, w, m, v, layer, prev, name, col0=0):
    _, R, C = slots.shape
    L, full_c = w.shape[0], w.shape[2]
    item = slots.dtype.itemsize
    tc = _tile(C, 2048)
    tr = _tile(R, max(16, ADAM_TILE_BYTES // (item * tc)), mult=16)
    if tr == R and R * tc * item > ADAM_TILE_BYTES:
        tc = _tile(C, max(LANE, ADAM_TILE_BYTES // (item * R)))
    c1, c2 = 1.0 - ADAM_B1 ** ADAM_STEP, 1.0 - ADAM_B2 ** ADAM_STEP
    n_prev = 0 if prev is None else 4

    def body(s_ref, w_ref, m_ref, v_ref, *rest):
        g_out, d_out, m_out, v_out = rest[n_prev:]
        g = s_ref[0].astype(F32)
        for k in range(1, NDEV):
            g = g + s_ref[k].astype(F32)
        m_new = ADAM_B1 * m_ref[0] + (1.0 - ADAM_B1) * g
        v_new = ADAM_B2 * v_ref[0] + (1.0 - ADAM_B2) * (g * g)
        g_out[0] = g
        m_out[0] = m_new
        v_out[0] = v_new
        d_out[0] = -ADAM_LR * ((m_new / c1) / (jnp.sqrt(v_new / c2) + ADAM_EPS) + ADAM_WD * w_ref[0])

    assert col0 % tc == 0
    spec = pl.BlockSpec((1, tr, tc), lambda i, j: (layer, i, j + col0 // tc))
    in_specs = [pl.BlockSpec((NDEV, tr, tc), lambda i, j: (0, i, j)), spec, spec, spec]
    in_specs += [pl.BlockSpec(memory_space=pl.ANY)] * n_prev
    return pl.pallas_call(
        body, name=name, grid=(R // tr, C // tc), out_shape=[jax.ShapeDtypeStruct((L, R, full_c), F32)] * 4,
        in_specs=in_specs, out_specs=[spec] * 4, input_output_aliases={4 + i: i for i in range(n_prev)},
        compiler_params=pltpu.CompilerParams(dimension_semantics=("parallel", "parallel"), vmem_limit_bytes=VMEM_LIMIT),
    )(slots, w, m, v, *(prev or []))


class _Cfg:
    def __init__(self, S, D, groups, q_lora, kv_lora, c_heads, d_mix):
        self.S, self.D, self.G, self.Q, self.KV, self.Hc, self.DMIX = S, D, groups, q_lora, kv_lora, c_heads, d_mix
        self.A, self.C = groups * LANE, c_heads * LANE
        self.B = d_mix - self.A - self.C
        self.Hb = self.B // LANE
        A, B, C = self.A, self.B, self.C
        assert B % LANE == 0 and B % C == 0 and (B + C) % A == 0
        self.ref_segs = [("ua", A), ("va", A), ("za", A), ("qb", B), ("kb", B), ("vb", B), ("zb", B),
                         ("cq", q_lora), ("ckv", kv_lora), ("kr", ROPE), ("zc", C)]
        self.off, off = {}, 0
        for nm, w in [("ua", A), ("va", A), ("za", A), ("qb", B), ("kb", B), ("vb", B), ("zb", B), ("zc", C),
                      ("cq", q_lora), ("kr", LANE), ("ckv", kv_lora)]:
            off = -(-off // w) * w
            self.off[nm] = off
            off += w
        self.NP = -(-off // 512) * 512
        self.width = {"kr": LANE, **{nm: w for nm, w in self.ref_segs if nm != "kr"}}

    def tiles(self, kind, layer):
        tq, bk = ATTN_TILES[kind][layer % len(ATTN_TILES[kind])]
        return min(tq, self.S), min(bk, self.S)

    def view(self, arr, nm):
        w = self.width[nm]
        return (arr, w, self.off[nm] // w)

    def heads_view(self, arr, nm):
        return (arr, self.off[nm] // LANE, LANE)


def _pad_w_in(cfg, wt):
    pieces, start = {}, 0
    for nm, width in cfg.ref_segs:
        pieces[nm] = wt[start:start + width]
        start += width
    rows, pos = [], 0
    for nm, off in sorted(cfg.off.items(), key=lambda kv: kv[1]):
        if off > pos:
            rows.append(jnp.zeros((off - pos, wt.shape[1]), wt.dtype))
        rows.append(pieces[nm])
        pos = off + pieces[nm].shape[0]
    if cfg.NP > pos:
        rows.append(jnp.zeros((cfg.NP - pos, wt.shape[1]), wt.dtype))
    return jnp.concatenate(rows, axis=0)


def _unpad_w_in(cfg, wpt):
    return jnp.concatenate([wpt[cfg.off[nm]:cfg.off[nm] + width] for nm, width in cfg.ref_segs], axis=0)


def _to_slots_cols(w):
    R = w.shape[0]
    return w.reshape(R, NDEV, -1).transpose(1, 0, 2)


def _from_slots_cols(s):
    return s.transpose(1, 0, 2).reshape(s.shape[1], -1)


def _perm_rows_out(cfg, w):
    return jnp.concatenate([w[cfg.A:], w[:cfg.A]], axis=0)


def _unperm_rows_out(cfg, w):
    return jnp.concatenate([w[cfg.B + cfg.C:], w[:cfg.B + cfg.C]], axis=0)


def _layer_params(cfg, l, g_pre, a_g_v, a_w_s, a_b_s, c_g_q, c_g_kv, g_out):
    A, B = cfg.A, cfg.B
    return dict(g_pre=g_pre[l][None], g_v=a_g_v[l].reshape(1, A), w_s=a_w_s[l], b_s=a_b_s[l][:, :, None],
                g_q=c_g_q[l][None], g_kv=c_g_kv[l][None],
                g_oa=g_out[l][None, :A], g_ob=g_out[l][None, A:A + B], g_oc=g_out[l][None, A + B:])


def _layer_fwd(cfg, l, x, W, p, cos2, sin2, rot, carry_in=None, carry_stick=None, carry_mla=None):
    S, D, A, B, C = cfg.S, cfg.D, cfg.A, cfg.B, cfg.C
    tag = f"l{l}"
    (h,) = _rowwise(_f_pre, [(x, D, 0)], [], [p["g_pre"]], [], [(D, BF16)], 256, f"pre_{tag}")
    if carry_in is None:
        proj = _matmul(h, W["in"], "nt", BF16, f"mm_in_{tag}")
    else:
        proj, moved_in = _matmul(h, W["in"], "nt", BF16, f"mm_in_{tag}", carry=carry_in[0])
        W.update(carry_in[1](*moved_in))
    a_rows = [cfg.view(proj, "ua"), cfg.view(proj, "va"), cfg.view(proj, "za")]
    a_par = [p["g_v"], p["w_s"], p["b_s"], p["g_oa"]]
    (ya,) = _rowwise(_f_gmlp, a_rows, [], a_par, [], [(A, BF16)], LANE, f"gmlp_{tag}")
    qb, kb, vb = cfg.heads_view(proj, "qb"), cfg.heads_view(proj, "kb"), cfg.heads_view(proj, "vb")
    yb, tot, *moved_stick = _stick_fwd(qb, kb, vb, cfg.Hb, LANE ** -0.5, f"stick_fwd_{tag}", *cfg.tiles("stick_fwd", l),
                                       carry=carry_stick)
    (ybg,) = _rowwise(_f_gate, [(yb, B, 0), cfg.view(proj, "zb")], [], [p["g_ob"]], [], [(B, BF16)], 256, f"gate_b_{tag}")
    c_rows = [cfg.view(proj, "cq"), cfg.view(proj, "ckv"), cfg.view(proj, "kr")]
    trig = [(cos2, LANE, 0), (sin2, LANE, 0)]
    cqn, ckvn, krr = _rowwise(_f_cpre, c_rows, trig, [p["g_q"], p["g_kv"]], [rot],
                              [(cfg.Q, BF16), (cfg.KV, BF16), (LANE, BF16)], 256, f"cpre_{tag}")
    q_raw = _matmul(cqn, W["uq"], "nt", BF16, f"mm_uq_{tag}")
    kv = _matmul(ckvn, W["ukv"], "nn", BF16, f"mm_ukv_{tag}")
    r_rows = [(q_raw, 2 * C, 0), (kv, 2 * C, 0), (krr, LANE, 0)]
    q_rot, k_full, v_c = _rowwise(_f_crope, r_rows, trig, [], [rot], [(2 * C, BF16), (2 * C, BF16), (C, BF16)], 128,
                                  f"crope_{tag}")
    qc, kc, vc = (q_rot, 0, 2 * LANE), (k_full, 0, 2 * LANE), (v_c, 0, LANE)
    yc, lse, *moved_mla = _softmax_fwd(qc, kc, vc, cfg.Hc, (LANE + ROPE) ** -0.5, f"mla_fwd_{tag}", *cfg.tiles("mla_fwd", l),
                                       carry=carry_mla)
    (ycg,) = _rowwise(_f_gate, [(yc, C, 0), cfg.view(proj, "zc")], [], [p["g_oc"]], [], [(C, BF16)], 256, f"gate_c_{tag}")
    y = jnp.concatenate([ybg, ycg, ya], axis=1)
    out = _matmul(y, W["out"], "nn", F32, f"mm_out_{tag}", add=x)
    saved = dict(x=x, h=h, proj=proj, yb=yb, tot=tot, cqn=cqn, ckvn=ckvn, krr=krr, q_raw=q_raw, kv=kv,
                 q_rot=q_rot, k_full=k_full, v_c=v_c, yc=yc, lse=lse, y=y)
    return out, saved, (moved_stick[0] if moved_stick else []), (moved_mla[0] if moved_mla else [])


def _layer_bwd(cfg, l, dout, sv, W, p, cos2, sin2, rot, ext_stick, ext_mla, last):
    S, D, A, B, C = cfg.S, cfg.D, cfg.A, cfg.B, cfg.C
    tag = f"l{l}"
    proj = sv["proj"]
    dy = _matmul(dout, W["out"], "nt", BF16, f"mm_dy_{tag}")
    d_wout = _matmul(sv["y"], dout, "tn", BF16, f"mm_dwout_{tag}")
    wout_slots = _unperm_rows_out(cfg, d_wout).reshape(NDEV, cfg.DMIX // NDEV, D)
    (dyb, dzb), (dg_ob,), _ = _rowwise_vjp(_f_gate, [(sv["yb"], B, 0), cfg.view(proj, "zb")], [], [p["g_ob"]], [],
                                           [(dy, B, 0)], [BF16, BF16], 256, f"gate_b_bwd_{tag}")
    (dyc, dzc), (dg_oc,), _ = _rowwise_vjp(_f_gate, [(sv["yc"], C, 0), cfg.view(proj, "zc")], [], [p["g_oc"]], [],
                                           [(dy, C, B // C)], [BF16, BF16], 256, f"gate_c_bwd_{tag}")
    a_rows = [cfg.view(proj, "ua"), cfg.view(proj, "va"), cfg.view(proj, "za")]
    a_par = [p["g_v"], p["w_s"], p["b_s"], p["g_oa"]]
    (dua, dva, dza), (dg_v, dw_s, db_s, dg_oa), _ = _rowwise_vjp(
        _f_gmlp, a_rows, [], a_par, [], [(dy, A, (B + C) // A)], [BF16] * 3, LANE, f"gmlp_bwd_{tag}")
    qb, kb, vb = cfg.heads_view(proj, "qb"), cfg.heads_view(proj, "kb"), cfg.heads_view(proj, "vb")
    dqb, dkb, dvb, moved_stick = _stick_bwd(qb, kb, vb, (dyb, 0, LANE), sv["tot"], cfg.Hb, LANE ** -0.5,
                                            f"stick_bwd_{tag}", *cfg.tiles("stick_bwd", l),
                                            carry=_Exchange([[a] for a in (ext_stick or [wout_slots])], False))
    ext_got = [mv[0] for mv in moved_stick] if ext_stick else []
    ext_mla = ext_mla + ([wout_slots] if ext_stick else [])
    qc, kc, vc = (sv["q_rot"], 0, 2 * LANE), (sv["k_full"], 0, 2 * LANE), (sv["v_c"], 0, LANE)
    dq_rot, dk_full, dv_c, *moved_mla = _softmax_bwd(qc, kc, vc, (sv["yc"], 0, LANE), (dyc, 0, LANE), sv["lse"], cfg.Hc,
                                                     (LANE + ROPE) ** -0.5, f"mla_bwd_{tag}", *cfg.tiles("mla_bwd", l),
                                                     carry=_Exchange([[a] for a in ext_mla], False) if ext_mla else None)
    moved_mla = [mv[0] for mv in (moved_mla[0] if moved_mla else [])]
    got = dict(w_out=moved_mla.pop() if ext_stick else moved_stick[0][0])
    ext_got += moved_mla
    trig = [(cos2, LANE, 0), (sin2, LANE, 0)]
    r_rows = [(sv["q_raw"], 2 * C, 0), (sv["kv"], 2 * C, 0), (sv["krr"], LANE, 0)]
    (dq_raw, dkv, dkrr), _, _ = _rowwise_vjp(_f_crope, r_rows, trig, [], [rot],
                                             [(dq_rot, 2 * C, 0), (dk_full, 2 * C, 0), (dv_c, C, 0)], [BF16] * 3, 128,
                                             f"crope_bwd_{tag}")
    dcqn = _matmul(dq_raw, W["uq"], "nn", BF16, f"mm_dcq_{tag}")
    d_wuq = _matmul(dq_raw, sv["cqn"], "tn", BF16, f"mm_dwuq_{tag}")
    dckvn = _matmul(dkv, W["ukv"], "nt", BF16, f"mm_dckv_{tag}")
    d_wukv = _matmul(sv["ckvn"], dkv, "tn", BF16, f"mm_dwukv_{tag}")
    c_rows = [cfg.view(proj, "cq"), cfg.view(proj, "ckv"), cfg.view(proj, "kr")]
    (dcq, dckv, dkr), (dg_q, dg_kv), _ = _rowwise_vjp(
        _f_cpre, c_rows, trig, [p["g_q"], p["g_kv"]], [rot],
        [(dcqn, cfg.Q, 0), (dckvn, cfg.KV, 0), (dkrr, LANE, 0)], [BF16] * 3, 256, f"cpre_bwd_{tag}")
    parts = dict(ua=dua, va=dva, za=dza, qb=dqb, kb=dkb, vb=dvb, zb=dzb, zc=dzc, cq=dcq, kr=dkr, ckv=dckv)
    cols, pos = [], 0
    for nm, off in sorted(cfg.off.items(), key=lambda kv_: kv_[1]):
        if off > pos:
            cols.append(jnp.zeros((S, off - pos), BF16))
        cols.append(parts[nm])
        pos = off + parts[nm].shape[1]
    if cfg.NP > pos:
        cols.append(jnp.zeros((S, cfg.NP - pos), BF16))
    dproj = jnp.concatenate(cols, axis=1)
    to_send = dict(c_w_uq=d_wuq.reshape(cfg.Hc, 2 * LANE, cfg.Q)[:, :LANE + ROPE].reshape(NDEV, -1, cfg.Q),
                   c_w_ukv=_to_slots_cols(d_wukv))
    if last:
        half = D // 2
        d_win_a = _matmul(dproj, sv["h"][:, :half], "tn", BF16, f"mm_dwin_a_{tag}")
        d_win_b, moved_a = _matmul(dproj, sv["h"][:, half:], "tn", BF16, f"mm_dwin_b_{tag}",
                                   carry=_Exchange([[_unpad_w_in(cfg, d_win_a).reshape(NDEV, -1, half)]], False))
        dh, moved = _matmul(dproj, W["in"], "nn", BF16, f"mm_dh_{tag}",
                            carry=_Exchange([[_unpad_w_in(cfg, d_win_b).reshape(NDEV, -1, D - half)],
                                             [to_send["c_w_uq"]], [to_send["c_w_ukv"]]], False))
        got.update(w_in=(moved_a[0][0], moved[0][0]), c_w_uq=moved[1][0], c_w_ukv=moved[2][0])
        to_send = {}
    else:
        d_win = _matmul(dproj, sv["h"], "tn", BF16, f"mm_dwin_{tag}")
        to_send["w_in"] = _unpad_w_in(cfg, d_win).reshape(NDEV, -1, D)
        dh = _matmul(dproj, W["in"], "nn", BF16, f"mm_dh_{tag}")
    (dx,), (dg_pre,), _ = _rowwise_vjp(_f_pre_res, [(sv["x"], D, 0)], [], [p["g_pre"]], [],
                                       [(dh, D, 0), (dout, D, 0)], [F32], 128, f"pre_bwd_{tag}")
    small = dict(g_pre=dg_pre[0], a_g_v=dg_v.reshape(cfg.G, LANE), a_w_s=dw_s, a_b_s=db_s[:, :, 0], c_g_q=dg_q[0],
                 c_g_kv=dg_kv[0], g_out=jnp.concatenate([dg_oa[0], dg_ob[0], dg_oc[0]]))
    return dx, small, got, to_send, ext_got


def _pack_small(vals):
    pieces = []
    for nm in SMALL:
        piece = vals[nm].reshape(-1, LANE)
        pieces.append(jnp.pad(piece, ((0, -piece.shape[0] % 8), (0, 0))))
    packed = jnp.concatenate(pieces, axis=0)
    return jnp.pad(packed, ((0, -packed.shape[0] % SMALL_ROWS), (0, 0)))


def _unpack_small(packed, like):
    out, row = {}, 0
    for nm in SMALL:
        n = like[nm].size // LANE
        out[nm] = packed[row:row + n].reshape(like[nm].shape)
        row += n + (-n % 8)
    return out


def kernel(x, positions, g_pre, w_in, a_g_v, a_w_s, a_b_s, c_g_q, c_g_kv, c_w_uq, c_w_ukv, g_out, w_out, g_final, loss_target, m_g_pre, m_w_in, m_a_g_v, m_a_w_s, m_a_b_s, m_c_g_q, m_c_g_kv, m_c_w_uq, m_c_w_ukv, m_g_out, m_w_out, m_g_final, v_g_pre, v_w_in, v_a_g_v, v_a_w_s, v_a_b_s, v_c_g_q, v_c_g_kv, v_c_w_uq, v_c_w_ukv, v_g_out, v_w_out, v_g_final):
    depth, S, D = w_in.shape[0], x.shape[1], x.shape[2]
    cfg = _Cfg(S, D, a_g_v.shape[1], c_g_q.shape[1], c_g_kv.shape[1], c_w_ukv.shape[2] * NDEV // (2 * LANE), g_out.shape[1])
    weights = dict(g_pre=g_pre, w_in=w_in, a_g_v=a_g_v, a_w_s=a_w_s, a_b_s=a_b_s, c_g_q=c_g_q, c_g_kv=c_g_kv,
                   c_w_uq=c_w_uq, c_w_ukv=c_w_ukv, g_out=g_out, w_out=w_out, g_final=g_final)
    mom_m = dict(g_pre=m_g_pre, w_in=m_w_in, a_g_v=m_a_g_v, a_w_s=m_a_w_s, a_b_s=m_a_b_s, c_g_q=m_c_g_q, c_g_kv=m_c_g_kv,
                 c_w_uq=m_c_w_uq, c_w_ukv=m_c_w_ukv, g_out=m_g_out, w_out=m_w_out, g_final=m_g_final)
    mom_v = dict(g_pre=v_g_pre, w_in=v_w_in, a_g_v=v_a_g_v, a_w_s=v_a_w_s, a_b_s=v_a_b_s, c_g_q=v_c_g_q, c_g_kv=v_c_g_kv,
                 c_w_uq=v_c_w_uq, c_w_ukv=v_c_w_ukv, g_out=v_g_out, w_out=v_w_out, g_final=v_g_final)
    big_names = ("w_in", "c_w_uq", "c_w_ukv", "w_out")

    inv_freq = 1.0 / (ROPE_THETA ** (jnp.arange(0, ROPE, 2, dtype=F32) / ROPE))
    ang = positions[0].astype(F32)[:, None] * inv_freq
    zpad = jnp.zeros((S, LANE - ROPE), F32)
    cos2 = jnp.concatenate([jnp.cos(ang), jnp.cos(ang), zpad], axis=1)
    sin2 = jnp.concatenate([jnp.sin(ang), jnp.sin(ang), zpad], axis=1)
    rot = _rope_matrix()

    for tree in (weights, mom_m, mom_v):
        for nm in TRANSPOSED:
            tree[nm] = jnp.swapaxes(tree[nm], 1, 2)

    def shards(l, names):
        return [[weights[nm][l].astype(BF16)] for nm in names]

    def assemble_rest(g_uq, g_ukv, g_wout):
        uq = jnp.pad(g_uq[0].reshape(cfg.Hc, LANE + ROPE, cfg.Q), ((0, 0), (0, LANE - ROPE), (0, 0)))
        return {"uq": uq.reshape(2 * cfg.C, cfg.Q), "ukv": _from_slots_cols(g_ukv[0]),
                "out": _perm_rows_out(cfg, g_wout[0].reshape(cfg.DMIX, D))}

    params = [_layer_params(cfg, l, g_pre, a_g_v, a_w_s, a_b_s, c_g_q, c_g_kv, g_out) for l in range(depth)]

    got_in = _exchange(_GatherTwoLevel(shards(0, big_names[:1])), "gather_w_in_l0")
    got_rest = None
    hcur, saved, Ws = x[0], [], []
    for l in range(depth):
        Ws.append({"in": _pad_w_in(cfg, got_in[0][0].reshape(-1, D))})
        if got_rest is not None:
            Ws[l].update(assemble_rest(*got_rest))
        nxt = l + 1 < depth
        hcur, sv, got_in, got_rest = _layer_fwd(
            cfg, l, hcur, Ws[l], params[l], cos2, sin2, rot,
            carry_in=None if got_rest is not None else (_GatherTwoLevel(shards(l, big_names[1:])), assemble_rest),
            carry_stick=_GatherTwoLevel(shards(l + 1, big_names[:1])) if nxt else None,
            carry_mla=_GatherTwoLevel(shards(l + 1, big_names[1:])) if nxt else None)
        saved.append(sv)
    (dh,), (dg_final,), (loss_rows,) = _rowwise_vjp(
        _f_final, [(hcur, D, 0)], [(loss_target[0], D, 0)], [g_final[None]], [], [(jnp.ones((S, 1), F32), 1, 0)],
        [F32], 128, "final", primal=[(1, F32)])
    loss = lax.psum(jnp.sum(loss_rows), MESH_AXES)

    small_g, slots, pending = [None] * depth, [None] * depth, {}
    for l in reversed(range(depth)):
        ext_stick = [pending["w_in"]] if pending else []
        ext_mla = [pending["c_w_uq"], pending["c_w_ukv"]] if pending else []
        dh, small_g[l], slots[l], pending, ext_got = _layer_bwd(cfg, l, dh, saved[l], Ws[l], params[l], cos2, sin2, rot,
                                                                ext_stick, ext_mla, l == 0)
        if ext_got:
            slots[l + 1].update(w_in=ext_got[0], c_w_uq=ext_got[1], c_w_ukv=ext_got[2])
    grad_x = dh[None]
    small_grads = {nm: jnp.stack([small_g[l][nm] for l in range(depth)]) for nm in SMALL if nm != "g_final"}
    small_grads["g_final"] = dg_final[0]
    (small_slots,) = _exchange(_Exchange([[_pack_small(small_grads)]], True), "gather_small_grads")

    res = {}
    for nm in big_names:
        res[nm] = None
        for l in range(depth):
            parts = slots[l][nm] if isinstance(slots[l][nm], tuple) else (slots[l][nm],)
            col0 = 0
            for i, part in enumerate(parts):
                res[nm] = _adamw(part, weights[nm], mom_m[nm], mom_v[nm], l, res[nm], f"adamw_{nm}_l{l}_{i}", col0)
                col0 += part.shape[2]
        if nm in TRANSPOSED:
            res[nm] = [jnp.swapaxes(r, 1, 2) for r in res[nm]]
    packed = _adamw(small_slots[0], _pack_small(weights)[None], _pack_small(mom_m)[None], _pack_small(mom_v)[None], 0, None,
                    "adamw_small")
    small_res = [_unpack_small(r[0], weights) for r in packed]
    order = ("g_pre", "w_in", "a_g_v", "a_w_s", "a_b_s", "c_g_q", "c_g_kv", "c_w_uq", "c_w_ukv", "g_out", "w_out", "g_final")
    outs = [loss, grad_x]
    for kind in range(4):
        outs += [small_res[kind][nm] if nm in SMALL else res[nm][kind] for nm in order]
    return tuple(outs)
```

```python
import functools

import numpy as np
import jax
import jax.numpy as jnp
from jax import lax
from jax.experimental import pallas as pl
from jax.experimental.pallas import tpu as pltpu

NDEV = 8
MESH_AXES = ("x", "y", "c")
LANE = 128
ROPE = 64
EPS = 1e-6
ROPE_THETA = 10000.0
ADAM_LR, ADAM_B1, ADAM_B2, ADAM_EPS, ADAM_WD, ADAM_STEP = 0.001, 0.9, 0.999, 1e-08, 0.01, 10
VMEM_LIMIT = 48 * 1024 * 1024
ADAM_TILE_BYTES = 768 * 1024
W_IN_EARLY_ROWS = 512
CARRY_MID_PERCENT = 80
SMALL_ROWS = 256
ATTN_TILES = {"stick_fwd": [(2048, 256)], "stick_bwd": [(2048, 256)], "mla_fwd": [(512, 1024)], "mla_bwd": [(2048, 512)]}
DIAG_BLOCK = 256
F32, BF16 = jnp.float32, jnp.bfloat16
SMALL = ("g_pre", "a_g_v", "a_w_s", "a_b_s", "c_g_q", "c_g_kv", "g_out", "g_final")
TRANSPOSED = ("w_in", "c_w_uq")


def _tile(dim, cap, mult=LANE):
    if dim <= cap:
        return dim
    t = (cap // mult) * mult
    while t >= mult:
        if dim % t == 0:
            return t
        t -= mult
    return dim


def _dot_nt(a, b):
    return lax.dot_general(a, b, (((1,), (1,)), ((), ())), preferred_element_type=F32)


def _dot_tn(a, b):
    return lax.dot_general(a, b, (((0,), (0,)), ((), ())), preferred_element_type=F32)


def _dot(a, b):
    return jnp.dot(a, b, preferred_element_type=F32)


class _Exchange:
    def __init__(self, groups, gather):
        self.groups, self.gather = groups, gather
        self.flat = [(gi, li, a) for gi, grp in enumerate(groups) for li, a in enumerate(grp)]
        self.n = len(self.flat)
        self.args = [a for (_, _, a) in self.flat]
        self.out_shape = [jax.ShapeDtypeStruct((len(grp), NDEV) + tuple(grp[0].shape[-2:]), grp[0].dtype) for grp in groups]
        self.scratch = [pltpu.SemaphoreType.DMA((self.n, NDEV - 1)), pltpu.SemaphoreType.DMA((self.n, NDEV - 1)),
                        pltpu.SemaphoreType.DMA((self.n,))]

    def _copies(self, ins, outs, send_sems, recv_sems, local_sems, landings):
        x, y, c = lax.axis_index("x"), lax.axis_index("y"), lax.axis_index("c")
        me = 4 * x + 2 * y + c
        owns = [pltpu.make_async_copy(ins[i] if self.gather else ins[i].at[me], outs[gi].at[li, me], local_sems.at[i])
                for i, (gi, li, _) in enumerate(self.flat)]
        pairs = []
        for k in range(1, NDEV):
            px = 1 - x if k & 4 else x
            py = 1 - y if k & 2 else y
            pc = 1 - c if k & 1 else c
            peer = 4 * px + 2 * py + pc
            for i, (gi, li, _) in enumerate(self.flat):
                src = ins[i] if self.gather else ins[i].at[peer]
                sems = dict(send_sem=send_sems.at[i, k - 1], recv_sem=recv_sems.at[i, k - 1],
                            device_id=(px, py, pc), device_id_type=pl.DeviceIdType.MESH)
                out = pltpu.make_async_remote_copy(src_ref=src, dst_ref=outs[gi].at[li, me], **sems)
                landing = pltpu.make_async_remote_copy(src_ref=src, dst_ref=outs[gi].at[li, peer], **sems) if landings else None
                pairs.append((out, landing))
        return owns, pairs

    def start(self, ins, outs, sems):
        owns, pairs = self._copies(ins, outs, *sems, landings=False)
        for own in owns:
            own.start()
        for out, _ in pairs:
            out.start()

    def mid(self, ins, outs, sems):
        pass

    def wait(self, ins, outs, sems):
        owns, pairs = self._copies(ins, outs, *sems, landings=True)
        for out, landing in pairs:
            out.wait_send()
            landing.wait_recv()
        for own in owns:
            own.wait()


class _GatherTwoLevel(_Exchange):
    def __init__(self, groups):
        super().__init__(groups, True)

    def _copy(self, i, k, ins, outs, send_sems, recv_sems, landing):
        gi, li, _ = self.flat[i]
        x, y, c = lax.axis_index("x"), lax.axis_index("y"), lax.axis_index("c")
        chips = [(x, y), (1 - x, y), (x, 1 - y), (1 - x, 1 - y)]

        def slot(chip, core):
            return outs[gi].at[li, 4 * chip[0] + 2 * chip[1] + core]

        if k == 0:
            to, src, dst, lands = (x, y, 1 - c), ins[i], slot(chips[0], c), slot(chips[0], 1 - c)
        elif k <= 3:
            to, src, dst, lands = (*chips[k], c), ins[i], slot(chips[0], c), slot(chips[k], c)
        else:
            to, src, dst, lands = (x, y, 1 - c), slot(chips[k - 3], c), slot(chips[k - 3], c), slot(chips[k - 3], 1 - c)
        return pltpu.make_async_remote_copy(src_ref=src, dst_ref=lands if landing else dst, send_sem=send_sems.at[i, k],
                                            recv_sem=recv_sems.at[i, k], device_id=to, device_id_type=pl.DeviceIdType.MESH)

    def _own(self, i, ins, outs, local_sems):
        gi, li, _ = self.flat[i]
        me = 4 * lax.axis_index("x") + 2 * lax.axis_index("y") + lax.axis_index("c")
        return pltpu.make_async_copy(ins[i], outs[gi].at[li, me], local_sems.at[i])

    def start(self, ins, outs, sems):
        send_sems, recv_sems, local_sems = sems
        for i in range(self.n):
            self._own(i, ins, outs, local_sems).start()
        for k in range(4):
            for i in range(self.n):
                self._copy(i, k, ins, outs, send_sems, recv_sems, False).start()

    def mid(self, ins, outs, sems):
        send_sems, recv_sems, _ = sems
        for k in range(1, 4):
            for i in range(self.n):
                self._copy(i, k, ins, outs, send_sems, recv_sems, True).wait_recv()
                self._copy(i, k + 3, ins, outs, send_sems, recv_sems, False).start()

    def wait(self, ins, outs, sems):
        send_sems, recv_sems, local_sems = sems
        for k in (0, 4, 5, 6):
            for i in range(self.n):
                self._copy(i, k, ins, outs, send_sems, recv_sems, True).wait_recv()
        for k in range(NDEV - 1):
            for i in range(self.n):
                self._copy(i, k, ins, outs, send_sems, recv_sems, False).wait_send()
        for i in range(self.n):
            self._own(i, ins, outs, local_sems).wait()


def _call(body, name, grid, in_specs, out_specs, out_shape, scratch, semantics, args, carry=None, aliases=None):
    n_in, n_out, n_scr = len(in_specs), len(out_specs), len(scratch)
    if carry is None:
        run = body
    else:
        semantics = ("arbitrary",) * len(grid)
        anyspec = pl.BlockSpec(memory_space=pl.ANY)
        in_specs = list(in_specs) + [anyspec] * carry.n
        out_specs = list(out_specs) + [anyspec] * len(carry.groups)
        out_shape = list(out_shape) + carry.out_shape
        scratch = list(scratch) + carry.scratch
        args = list(args) + carry.args

        def run(*refs):
            c_in, x_in = refs[:n_in], refs[n_in:n_in + carry.n]
            rest = refs[n_in + carry.n:]
            c_out, x_out = rest[:n_out], rest[n_out:n_out + len(carry.groups)]
            c_scr, sems = rest[n_out + len(carry.groups):len(rest) - 3], rest[len(rest) - 3:]
            step, total = 0, 1
            for d, extent in enumerate(grid):
                step = step * extent + pl.program_id(d)
                total *= extent

            @pl.when(step == 0)
            def _():
                carry.start(x_in, x_out, sems)

            body(*c_in, *c_out, *c_scr)

            @pl.when(step == (total * CARRY_MID_PERCENT) // 100)
            def _():
                carry.mid(x_in, x_out, sems)

            @pl.when(step == total - 1)
            def _():
                carry.wait(x_in, x_out, sems)

    res = pl.pallas_call(
        run, name=name, grid=grid, out_shape=list(out_shape), in_specs=list(in_specs), out_specs=list(out_specs),
        scratch_shapes=list(scratch), input_output_aliases=aliases or {},
        compiler_params=pltpu.CompilerParams(dimension_semantics=semantics, vmem_limit_bytes=VMEM_LIMIT,
                                             has_side_effects=carry is not None),
    )(*args)
    return list(res[:n_out]), list(res[n_out:])


def _exchange(ex, name):
    groups = ex.groups

    def body(*refs):
        ins, outs, sems = refs[:ex.n], refs[ex.n:ex.n + len(groups)], refs[ex.n + len(groups):]
        ex.start(ins, outs, sems)
        ex.mid(ins, outs, sems)
        ex.wait(ins, outs, sems)

    anyspec = pl.BlockSpec(memory_space=pl.ANY)
    return pl.pallas_call(
        body, name=name, out_shape=ex.out_shape, in_specs=[anyspec] * ex.n, out_specs=[anyspec] * len(groups),
        scratch_shapes=ex.scratch, compiler_params=pltpu.CompilerParams(has_side_effects=True),
    )(*ex.args)


def _matmul(a, b, mode, out_dtype, name, add=None, tm=1024, tn=1024, tk=2048, carry=None):
    if mode == "tn":
        (K, M), (K2, N) = a.shape, b.shape
    elif mode == "nt":
        (M, K), (N, K2) = a.shape, b.shape
    else:
        (M, K), (K2, N) = a.shape, b.shape
    assert K == K2, (a.shape, b.shape, mode)
    tm, tn, tk = _tile(M, tm), _tile(N, tn), _tile(K, tk)
    nk = K // tk
    a_spec = pl.BlockSpec((tk, tm), lambda i, j, k: (k, i)) if mode == "tn" else pl.BlockSpec((tm, tk), lambda i, j, k: (i, k))
    b_spec = pl.BlockSpec((tn, tk), lambda i, j, k: (j, k)) if mode == "nt" else pl.BlockSpec((tk, tn), lambda i, j, k: (k, j))
    dot = {"nn": _dot, "nt": _dot_nt, "tn": _dot_tn}[mode]
    has_add = add is not None

    def body(*refs):
        a_ref, b_ref = refs[0], refs[1]
        o_ref, acc = refs[-2], refs[-1]
        k = pl.program_id(2)

        @pl.when(k == 0)
        def _():
            acc[...] = jnp.zeros_like(acc)

        acc[...] += dot(a_ref[...].astype(BF16), b_ref[...].astype(BF16))

        @pl.when(k == nk - 1)
        def _():
            r = acc[...]
            if has_add:
                r = r + refs[2][...]
            o_ref[...] = r.astype(o_ref.dtype)

    in_specs = [a_spec, b_spec]
    args = [a, b]
    if has_add:
        in_specs.append(pl.BlockSpec((tm, tn), lambda i, j, k: (i, j)))
        args.append(add)
    (out,), moved = _call(body, name, (M // tm, N // tn, nk), in_specs, [pl.BlockSpec((tm, tn), lambda i, j, k: (i, j))],
                          [jax.ShapeDtypeStruct((M, N), out_dtype)], [pltpu.VMEM((tm, tn), F32)],
                          ("parallel", "parallel", "arbitrary"), args, carry)
    return out if carry is None else (out, moved)


def _row_specs(views, tile):
    return [pl.BlockSpec((tile, w), functools.partial(lambda i, cb: (i, cb), cb=cb)) for (_, w, cb) in views]


def _full_specs(arrs):
    return [pl.BlockSpec(p.shape, functools.partial(lambda i, nd: (0,) * nd, nd=p.ndim)) for p in arrs]


def _rowwise(fn, rows, aux, params, consts, outs, tile, name):
    S = rows[0][0].shape[0]
    nr, na, npar, nc = len(rows), len(aux), len(params), len(consts)

    def body(*refs):
        ins = [r[...].astype(F32) for r in refs[:nr + na]]
        small = [r[...] for r in refs[nr + na:nr + na + npar + nc]]
        res = fn(*ins, *small)
        for o_ref, r in zip(refs[nr + na + npar + nc:], res):
            o_ref[...] = r.astype(o_ref.dtype)

    return pl.pallas_call(
        body, name=name, grid=(S // tile,),
        out_shape=[jax.ShapeDtypeStruct((S, w), dt) for (w, dt) in outs],
        in_specs=_row_specs(rows + aux, tile) + _full_specs(params + consts),
        out_specs=[pl.BlockSpec((tile, w), lambda i: (i, 0)) for (w, _) in outs],
        compiler_params=pltpu.CompilerParams(dimension_semantics=("parallel",), vmem_limit_bytes=VMEM_LIMIT),
    )(*[v[0] for v in rows + aux], *params, *consts)


def _rowwise_vjp(fn, rows, aux, params, consts, cots, grad_dtypes, tile, name, primal=()):
    S = rows[0][0].shape[0]
    nr, na, npar, nc, nct, npr = len(rows), len(aux), len(params), len(consts), len(cots), len(primal)

    def body(*refs):
        n_in = nr + na + npar + nc + nct
        rv = [r[...].astype(F32) for r in refs[:nr]]
        av = [r[...].astype(F32) for r in refs[nr:nr + na]]
        pv = [r[...] for r in refs[nr + na:nr + na + npar]]
        cv = [r[...] for r in refs[nr + na + npar:nr + na + npar + nc]]
        ct = tuple(r[...].astype(F32) for r in refs[nr + na + npar + nc:n_in])
        res, vjp = jax.vjp(lambda *rp: tuple(fn(*rp[:nr], *av, *rp[nr:], *cv)), *rv, *pv)
        grads = vjp(ct)
        g_refs = refs[n_in:n_in + nr]
        p_refs = refs[n_in + nr:n_in + nr + npar]
        o_refs = refs[n_in + nr + npar:]
        for g_ref, g in zip(g_refs, grads[:nr]):
            g_ref[...] = g.astype(g_ref.dtype)

        @pl.when(pl.program_id(0) == 0)
        def _():
            for p_ref in p_refs:
                p_ref[...] = jnp.zeros_like(p_ref)

        for p_ref, g in zip(p_refs, grads[nr:]):
            p_ref[...] += g
        for o_ref, r in zip(o_refs, res[:npr]):
            o_ref[...] = r.astype(o_ref.dtype)

    out_shape = ([jax.ShapeDtypeStruct((S, w), dt) for (_, w, _), dt in zip(rows, grad_dtypes)]
                 + [jax.ShapeDtypeStruct(p.shape, F32) for p in params]
                 + [jax.ShapeDtypeStruct((S, w), dt) for (w, dt) in primal])
    out_specs = ([pl.BlockSpec((tile, w), lambda i: (i, 0)) for (_, w, _) in rows] + _full_specs(params)
                 + [pl.BlockSpec((tile, w), lambda i: (i, 0)) for (w, _) in primal])
    res = pl.pallas_call(
        body, name=name, grid=(S // tile,), out_shape=out_shape,
        in_specs=_row_specs(rows + aux, tile) + _full_specs(params + consts) + _row_specs(cots, tile),
        out_specs=out_specs,
        compiler_params=pltpu.CompilerParams(dimension_semantics=("arbitrary",), vmem_limit_bytes=VMEM_LIMIT),
    )(*[v[0] for v in rows + aux], *params, *consts, *[v[0] for v in cots])
    return res[:nr], res[nr:nr + npar], res[nr + npar:]


@jax.custom_vjp
def _mm(a, b):
    return _dot(a.astype(BF16), b.astype(BF16))


def _mm_fwd(a, b):
    return _mm(a, b), (a, b)


def _mm_bwd(res, ct):
    a, b = res
    ctb = ct.astype(BF16)
    return _dot_nt(ctb, b.astype(BF16)), _dot_tn(a.astype(BF16), ctb)


_mm.defvjp(_mm_fwd, _mm_bwd)


def _rms(x, g):
    return x * lax.rsqrt(jnp.mean(x * x, axis=-1, keepdims=True) + EPS) * g


def _f_pre(x, g):
    return (_rms(x, g),)


def _f_pre_res(x, g):
    return _rms(x, g), x


def _f_gate(y, z, g):
    return (_rms(y, g) * jax.nn.silu(z),)


def _f_gmlp(u, v, z, g_v, w_s, b_s, g_o):
    groups = w_s.shape[0]
    u, v = jax.nn.gelu(u), jax.nn.gelu(v)
    t_idx = lax.broadcasted_iota(jnp.int32, (LANE, LANE), 0)
    s_idx = lax.broadcasted_iota(jnp.int32, (LANE, LANE), 1)
    ys = []
    for g in range(groups):
        sl = slice(g * LANE, (g + 1) * LANE)
        vn = _rms(v[:, sl], g_v[:, sl])
        w = jnp.where(s_idx <= t_idx, w_s[g], 0.0)
        ys.append(u[:, sl] * (_mm(w, vn) + b_s[g]))
    return (_rms(jnp.concatenate(ys, axis=1), g_o) * jax.nn.silu(z),)


def _rope(x, cos2, sin2, rot):
    return x * cos2 + _mm(x, rot) * sin2


def _f_cpre(cq, ckv, kr, cos2, sin2, g_q, g_kv, rot):
    return _rms(cq, g_q), _rms(ckv, g_kv), _rope(kr, cos2, sin2, rot)


def _f_crope(q, kv, krr, cos2, sin2, rot):
    heads = q.shape[1] // (2 * LANE)
    qs, ks, vs = [], [], []
    for h in range(heads):
        lo, mid, hi = 2 * h * LANE, (2 * h + 1) * LANE, (2 * h + 2) * LANE
        qs += [q[:, lo:mid], _rope(q[:, mid:hi], cos2, sin2, rot)]
        ks += [kv[:, lo:mid], krr]
        vs += [kv[:, mid:hi]]
    return jnp.concatenate(qs, axis=1), jnp.concatenate(ks, axis=1), jnp.concatenate(vs, axis=1)


def _f_final(h, target, g):
    err = _rms(h, g) - target
    return (0.5 * jnp.mean(err * err, axis=-1, keepdims=True),)


def _rope_matrix():
    r = np.zeros((LANE, LANE), np.float32)
    half = ROPE // 2
    for i in range(half):
        r[i + half, i] = -1.0
        r[i, i + half] = 1.0
    return jnp.asarray(r)


def _head_spec(view, rows, n_rows_block):
    _, cb0, w = view
    if n_rows_block:
        return pl.BlockSpec((rows, w), functools.partial(lambda h, i, cb0: (i, cb0 + h), cb0=cb0))
    return pl.BlockSpec((rows, w), functools.partial(lambda h, i, cb0: (0, cb0 + h), cb0=cb0))


def _stat_spec(tq):
    return pl.BlockSpec((1, tq, 1), lambda h, i: (h, i, 0))


def _softplus(z):
    return jnp.maximum(z, 0.0) + jnp.log(1.0 + jnp.exp(-jnp.abs(z)))


def _cumsum_mm(x, m01):
    hi = x.astype(BF16)
    lo = (x - hi.astype(F32)).astype(BF16)
    return _dot(hi, m01) + _dot(lo, m01)


def _attn_call(body, name, heads, S, tq, ins, in_blocked, outs, out_blocked, scratch, stats_in=0, stats_out=0, carry=None):
    in_specs = [_head_spec(v, tq if blk else S, blk) for v, blk in zip(ins[:len(ins) - stats_in], in_blocked)]
    in_specs += [_stat_spec(tq)] * stats_in
    out_specs = [_head_spec((None, 0, w), tq if blk else S, blk) for (w, _), blk in zip(outs, out_blocked)]
    out_specs += [_stat_spec(tq)] * stats_out
    out_shape = [jax.ShapeDtypeStruct((S, heads * w), dt) for (w, dt) in outs]
    out_shape += [jax.ShapeDtypeStruct((heads, S, 1), F32)] * stats_out
    args = [v[0] for v in ins[:len(ins) - stats_in]] + list(ins[len(ins) - stats_in:])
    res, moved = _call(body, name, (heads, S // tq), in_specs, out_specs, out_shape, scratch, ("arbitrary", "arbitrary"),
                       args, carry)
    return res if carry is None else res + [moved]


def _softmax_fwd(q, k, v, heads, scale, name, tq, bk, carry=None):
    S, dv = q[0].shape[0], v[2]

    def body(q_ref, k_ref, v_ref, o_ref, lse_ref):
        qi = pl.program_id(1)
        qv = q_ref[...]
        row = qi * tq + lax.broadcasted_iota(jnp.int32, (tq, bk), 0)
        col0 = lax.broadcasted_iota(jnp.int32, (tq, bk), 1)

        def step(kb, carry):
            m, l, acc = carry
            sl = pl.ds(pl.multiple_of(kb * bk, bk), bk)
            s = _dot_nt(qv, k_ref[sl, :]) * scale
            s = jnp.where(kb * bk + col0 <= row, s, -1e30)
            m_new = jnp.maximum(m, jnp.max(s, axis=1, keepdims=True))
            p = jnp.exp(s - m_new)
            alpha = jnp.exp(m - m_new)
            l = alpha * l + jnp.sum(p, axis=1, keepdims=True)
            acc = alpha * acc + _dot(p.astype(BF16), v_ref[sl, :])
            return m_new, l, acc

        n_kb = (qi * tq + tq + bk - 1) // bk
        m, l, acc = lax.fori_loop(0, n_kb, step, (jnp.full((tq, 1), -1e30, F32), jnp.zeros((tq, 1), F32),
                                                  jnp.zeros((tq, dv), F32)))
        o_ref[...] = (acc / l).astype(o_ref.dtype)
        lse_ref[0] = m + jnp.log(l)

    return _attn_call(body, name, heads, S, tq, [q, k, v], [1, 0, 0], [(dv, BF16)], [1], [], stats_out=1, carry=carry)


def _softmax_bwd(q, k, v, o, do, lse, heads, scale, name, tq, bk, carry=None):
    S, dq_w, dv = q[0].shape[0], q[2], v[2]
    nq = S // tq

    bd = min(DIAG_BLOCK, tq)
    assert tq % bk == 0 and tq % bd == 0

    def body(q_ref, k_ref, v_ref, o_ref, do_ref, lse_ref, dq_ref, dk_ref, dv_ref, dk_acc, dv_acc, delta_scr, dq_scr):
        qi = pl.program_id(1)

        @pl.when(qi == 0)
        def _():
            dk_acc[...] = jnp.zeros_like(dk_acc)
            dv_acc[...] = jnp.zeros_like(dv_acc)

        delta_scr[...] = jnp.sum(do_ref[...].astype(F32) * o_ref[...].astype(F32), axis=1, keepdims=True)
        dq_scr[...] = jnp.zeros_like(dq_scr)

        def block(r0, sl, width, masked):
            qv, dov = q_ref[r0:, :], do_ref[r0:, :]
            ks, vs = k_ref[sl, :], v_ref[sl, :]
            p = jnp.exp(_dot_nt(qv, ks) * scale - lse_ref[0, r0:, :])
            if masked:
                shape = (tq - r0, width)
                p = jnp.where(lax.broadcasted_iota(jnp.int32, shape, 1) <= lax.broadcasted_iota(jnp.int32, shape, 0), p, 0.0)
            ds = (p * (_dot_nt(dov, vs) - delta_scr[r0:, :]) * scale).astype(BF16)
            dk_acc[sl, :] += _dot_tn(ds, qv)
            dv_acc[sl, :] += _dot_tn(p.astype(BF16), dov)
            dq_scr[r0:, :] += _dot(ds, ks)

        def step(kb, _):
            block(0, pl.ds(pl.multiple_of(kb * bk, bk), bk), bk, False)
            return 0

        lax.fori_loop(0, qi * (tq // bk), step, 0)
        for j in range(tq // bd):
            block(j * bd, pl.ds(pl.multiple_of(qi * tq + j * bd, bd), bd), bd, True)
        dq_ref[...] = dq_scr[...].astype(dq_ref.dtype)

        @pl.when(qi == nq - 1)
        def _():
            dk_ref[...] = dk_acc[...].astype(dk_ref.dtype)
            dv_ref[...] = dv_acc[...].astype(dv_ref.dtype)

    return _attn_call(body, name, heads, S, tq, [q, k, v, o, do, lse], [1, 0, 0, 1, 1],
                      [(dq_w, BF16), (dq_w, BF16), (dv, BF16)], [1, 0, 0],
                      [pltpu.VMEM((S, dq_w), F32), pltpu.VMEM((S, dv), F32), pltpu.VMEM((tq, 1), F32),
                       pltpu.VMEM((tq, dq_w), F32)], stats_in=1, carry=carry)


def _stick_fwd(q, k, v, heads, scale, name, tq, bk, carry=None):
    S, dv = q[0].shape[0], v[2]

    assert tq % bk == 0
    n_sub = tq // bk

    def body(q_ref, k_ref, v_ref, o_ref, tot_ref, c_scr, acc_scr):
        qi = pl.program_id(1)
        m_gt = (lax.broadcasted_iota(jnp.int32, (bk, bk), 0) > lax.broadcasted_iota(jnp.int32, (bk, bk), 1)).astype(BF16)

        def block(r0, sl, masked):
            rows = tq - r0
            z = _dot_nt(q_ref[r0:, :], k_ref[sl, :]) * scale
            sp = _softplus(z)
            lk = -sp
            if masked:
                mask = lax.broadcasted_iota(jnp.int32, (rows, bk), 1) < lax.broadcasted_iota(jnp.int32, (rows, bk), 0)
                lk = jnp.where(mask, lk, 0.0)
            after = _cumsum_mm(lk, m_gt) + c_scr[r0:, :]
            a = jnp.exp(z - sp + after)
            if masked:
                a = jnp.where(mask, a, 0.0)
            acc_scr[r0:, :] += _dot(a.astype(BF16), v_ref[sl, :])
            c_scr[r0:, :] += jnp.sum(lk, axis=1, keepdims=True)

        c_scr[...] = jnp.zeros_like(c_scr)
        acc_scr[...] = jnp.zeros_like(acc_scr)
        for j in reversed(range(n_sub)):
            block(j * bk, pl.ds(pl.multiple_of(qi * tq + j * bk, bk), bk), True)

        def step(it, _):
            block(0, pl.ds(pl.multiple_of((qi * n_sub - 1 - it) * bk, bk), bk), False)
            return 0

        lax.fori_loop(0, qi * n_sub, step, 0)
        o_ref[...] = acc_scr[...].astype(o_ref.dtype)
        tot_ref[0] = c_scr[...]

    return _attn_call(body, name, heads, S, tq, [q, k, v], [1, 0, 0], [(dv, BF16)], [1],
                      [pltpu.VMEM((tq, 1), F32), pltpu.VMEM((tq, dv), F32)], stats_out=1, carry=carry)


def _stick_bwd(q, k, v, do, tot, heads, scale, name, tq, bk, carry=None):
    S, dq_w, dv = q[0].shape[0], q[2], v[2]
    nq = S // tq

    assert tq % bk == 0
    n_sub = tq // bk

    def body(q_ref, k_ref, v_ref, do_ref, tot_ref, dq_ref, dk_ref, dv_ref, dk_acc, dv_acc, pc_scr, gc_scr, dq_scr):
        qi = pl.program_id(1)

        @pl.when(qi == 0)
        def _():
            dk_acc[...] = jnp.zeros_like(dk_acc)
            dv_acc[...] = jnp.zeros_like(dv_acc)

        j_idx = lax.broadcasted_iota(jnp.int32, (bk, bk), 0)
        s_idx = lax.broadcasted_iota(jnp.int32, (bk, bk), 1)
        m_le, m_lt = (j_idx <= s_idx).astype(BF16), (j_idx < s_idx).astype(BF16)

        def block(r0, sl, masked):
            rows = tq - r0
            qv, dov = q_ref[r0:, :], do_ref[r0:, :]
            ks, vs = k_ref[sl, :], v_ref[sl, :]
            z = _dot_nt(qv, ks) * scale
            sp = _softplus(z)
            lk = -sp
            if masked:
                mask = lax.broadcasted_iota(jnp.int32, (rows, bk), 1) < lax.broadcasted_iota(jnp.int32, (rows, bk), 0)
                lk = jnp.where(mask, lk, 0.0)
            after = tot_ref[0, r0:, :] - pc_scr[r0:, :] - _cumsum_mm(lk, m_le)
            log_beta = z - sp
            a = jnp.exp(log_beta + after)
            if masked:
                a = jnp.where(mask, a, 0.0)
            g = _dot_nt(dov, vs) * a
            cg = gc_scr[r0:, :] + _cumsum_mm(g, m_lt)
            dz = g * jnp.exp(-sp) - jnp.exp(log_beta) * cg
            if masked:
                dz = jnp.where(mask, dz, 0.0)
            dz = (dz * scale).astype(BF16)
            dk_acc[sl, :] += _dot_tn(dz, qv)
            dv_acc[sl, :] += _dot_tn(a.astype(BF16), dov)
            dq_scr[r0:, :] += _dot(dz, ks)
            pc_scr[r0:, :] += jnp.sum(lk, axis=1, keepdims=True)
            gc_scr[r0:, :] += jnp.sum(g, axis=1, keepdims=True)

        pc_scr[...] = jnp.zeros_like(pc_scr)
        gc_scr[...] = jnp.zeros_like(gc_scr)
        dq_scr[...] = jnp.zeros_like(dq_scr)

        def step(kb, _):
            block(0, pl.ds(pl.multiple_of(kb * bk, bk), bk), False)
            return 0

        lax.fori_loop(0, qi * n_sub, step, 0)
        for j in range(n_sub):
            block(j * bk, pl.ds(pl.multiple_of(qi * tq + j * bk, bk), bk), True)
        dq_ref[...] = dq_scr[...].astype(dq_ref.dtype)

        @pl.when(qi == nq - 1)
        def _():
            dk_ref[...] = dk_acc[...].astype(dk_ref.dtype)
            dv_ref[...] = dv_acc[...].astype(dv_ref.dtype)

    return _attn_call(body, name, heads, S, tq, [q, k, v, do, tot], [1, 0, 0, 1],
                      [(dq_w, BF16), (dq_w, BF16), (dv, BF16)], [1, 0, 0],
                      [pltpu.VMEM((S, dq_w), F32), pltpu.VMEM((S, dv), F32), pltpu.VMEM((tq, 1), F32),
                       pltpu.VMEM((tq, 1), F32), pltpu.VMEM((tq, dq_w), F32)], stats_in=1, carry=carry)


def _adamw(slots, w, m, v, layer, prev, name, col0=0, carry=None):
    _, R, C = slots.shape
    L, full_c = w.shape[0], w.shape[2]
    item = slots.dtype.itemsize
    tc = _tile(C, 2048)
    tr = _tile(R, max(16, ADAM_TILE_BYTES // (item * tc)), mult=16)
    if tr == R and R * tc * item > ADAM_TILE_BYTES:
        tc = _tile(C, max(LANE, ADAM_TILE_BYTES // (item * R)))
    c1, c2 = 1.0 - ADAM_B1 ** ADAM_STEP, 1.0 - ADAM_B2 ** ADAM_STEP
    n_prev = 0 if prev is None else 4

    def body(s_ref, w_ref, m_ref, v_ref, *rest):
        g_out, d_out, m_out, v_out = rest[n_prev:]
        g = s_ref[0].astype(F32)
        for k in range(1, NDEV):
            g = g + s_ref[k].astype(F32)
        m_new = ADAM_B1 * m_ref[0] + (1.0 - ADAM_B1) * g
        v_new = ADAM_B2 * v_ref[0] + (1.0 - ADAM_B2) * (g * g)
        g_out[0] = g
        m_out[0] = m_new
        v_out[0] = v_new
        d_out[0] = -ADAM_LR * ((m_new / c1) / (jnp.sqrt(v_new / c2) + ADAM_EPS) + ADAM_WD * w_ref[0])

    assert col0 % tc == 0
    spec = pl.BlockSpec((1, tr, tc), lambda i, j: (layer, i, j + col0 // tc))
    in_specs = [pl.BlockSpec((NDEV, tr, tc), lambda i, j: (0, i, j)), spec, spec, spec]
    in_specs += [pl.BlockSpec(memory_space=pl.ANY)] * n_prev
    res, moved = _call(body, name, (R // tr, C // tc), in_specs, [spec] * 4, [jax.ShapeDtypeStruct((L, R, full_c), F32)] * 4,
                       [], ("parallel", "parallel"), [slots, w, m, v, *(prev or [])], carry,
                       aliases={4 + i: i for i in range(n_prev)})
    return res if carry is None else (res, moved)


class _Cfg:
    def __init__(self, S, D, groups, q_lora, kv_lora, c_heads, d_mix):
        self.S, self.D, self.G, self.Q, self.KV, self.Hc, self.DMIX = S, D, groups, q_lora, kv_lora, c_heads, d_mix
        self.A, self.C = groups * LANE, c_heads * LANE
        self.B = d_mix - self.A - self.C
        self.Hb = self.B // LANE
        A, B, C = self.A, self.B, self.C
        assert B % LANE == 0 and B % C == 0 and (B + C) % A == 0
        self.ref_segs = [("ua", A), ("va", A), ("za", A), ("qb", B), ("kb", B), ("vb", B), ("zb", B),
                         ("cq", q_lora), ("ckv", kv_lora), ("kr", ROPE), ("zc", C)]
        self.off, off = {}, 0
        for nm, w in [("ua", A), ("va", A), ("za", A), ("qb", B), ("kb", B), ("vb", B), ("zb", B), ("zc", C),
                      ("cq", q_lora), ("kr", LANE), ("ckv", kv_lora)]:
            off = -(-off // w) * w
            self.off[nm] = off
            off += w
        self.NP = -(-off // 512) * 512
        self.width = {"kr": LANE, **{nm: w for nm, w in self.ref_segs if nm != "kr"}}

    def tiles(self, kind, layer):
        tq, bk = ATTN_TILES[kind][layer % len(ATTN_TILES[kind])]
        return min(tq, self.S), min(bk, self.S)

    def view(self, arr, nm):
        w = self.width[nm]
        return (arr, w, self.off[nm] // w)

    def heads_view(self, arr, nm):
        return (arr, self.off[nm] // LANE, LANE)


def _gathered_rows(parts, a, b):
    per = sum(p.shape[1] for p in parts)
    out = []
    while a < b:
        k, r = divmod(a, per)
        i = 0
        while r >= parts[i].shape[1]:
            r -= parts[i].shape[1]
            i += 1
        n = min(b - a, parts[i].shape[1] - r)
        out.append(parts[i][k, r:r + n])
        a += n
    return out


def _pad_w_in(cfg, parts):
    width_d, dtype = parts[0].shape[2], parts[0].dtype
    start_of, start = {}, 0
    for nm, width in cfg.ref_segs:
        start_of[nm] = (start, width)
        start += width
    rows, pos = [], 0
    for nm, off in sorted(cfg.off.items(), key=lambda kv: kv[1]):
        if off > pos:
            rows.append(jnp.zeros((off - pos, width_d), dtype))
        rows += _gathered_rows(parts, start_of[nm][0], start_of[nm][0] + start_of[nm][1])
        pos = off + start_of[nm][1]
    if cfg.NP > pos:
        rows.append(jnp.zeros((cfg.NP - pos, width_d), dtype))
    return jnp.concatenate(rows, axis=0)


def _unpad_w_in(cfg, wpt):
    return jnp.concatenate([wpt[cfg.off[nm]:cfg.off[nm] + width] for nm, width in cfg.ref_segs], axis=0)


def _to_slots_cols(w):
    R = w.shape[0]
    return w.reshape(R, NDEV, -1).transpose(1, 0, 2)


def _from_slots_cols(s):
    return s.transpose(1, 0, 2).reshape(s.shape[1], -1)


def _perm_rows_out(cfg, w):
    return jnp.concatenate([w[cfg.A:], w[:cfg.A]], axis=0)


def _unperm_rows_out(cfg, w):
    return jnp.concatenate([w[cfg.B + cfg.C:], w[:cfg.B + cfg.C]], axis=0)


def _layer_params(cfg, l, g_pre, a_g_v, a_w_s, a_b_s, c_g_q, c_g_kv, g_out):
    A, B = cfg.A, cfg.B
    return dict(g_pre=g_pre[l][None], g_v=a_g_v[l].reshape(1, A), w_s=a_w_s[l], b_s=a_b_s[l][:, :, None],
                g_q=c_g_q[l][None], g_kv=c_g_kv[l][None],
                g_oa=g_out[l][None, :A], g_ob=g_out[l][None, A:A + B], g_oc=g_out[l][None, A + B:])


def _layer_fwd(cfg, l, x, W, p, cos2, sin2, rot, carry_in=None, carry_stick=None, carry_mla=None):
    S, D, A, B, C = cfg.S, cfg.D, cfg.A, cfg.B, cfg.C
    tag = f"l{l}"
    (h,) = _rowwise(_f_pre, [(x, D, 0)], [], [p["g_pre"]], [], [(D, BF16)], 256, f"pre_{tag}")
    if carry_in is None:
        proj = _matmul(h, W["in"], "nt", BF16, f"mm_in_{tag}")
    else:
        proj, moved_in = _matmul(h, W["in"], "nt", BF16, f"mm_in_{tag}", carry=carry_in[0])
        carry_in[1](moved_in)
    a_rows = [cfg.view(proj, "ua"), cfg.view(proj, "va"), cfg.view(proj, "za")]
    a_par = [p["g_v"], p["w_s"], p["b_s"], p["g_oa"]]
    (ya,) = _rowwise(_f_gmlp, a_rows, [], a_par, [], [(A, BF16)], LANE, f"gmlp_{tag}")
    qb, kb, vb = cfg.heads_view(proj, "qb"), cfg.heads_view(proj, "kb"), cfg.heads_view(proj, "vb")
    yb, tot, *moved_stick = _stick_fwd(qb, kb, vb, cfg.Hb, LANE ** -0.5, f"stick_fwd_{tag}", *cfg.tiles("stick_fwd", l),
                                       carry=carry_stick)
    (ybg,) = _rowwise(_f_gate, [(yb, B, 0), cfg.view(proj, "zb")], [], [p["g_ob"]], [], [(B, BF16)], 256, f"gate_b_{tag}")
    c_rows = [cfg.view(proj, "cq"), cfg.view(proj, "ckv"), cfg.view(proj, "kr")]
    trig = [(cos2, LANE, 0), (sin2, LANE, 0)]
    cqn, ckvn, krr = _rowwise(_f_cpre, c_rows, trig, [p["g_q"], p["g_kv"]], [rot],
                              [(cfg.Q, BF16), (cfg.KV, BF16), (LANE, BF16)], 256, f"cpre_{tag}")
    q_raw = _matmul(cqn, W["uq"], "nt", BF16, f"mm_uq_{tag}")
    kv = _matmul(ckvn, W["ukv"], "nn", BF16, f"mm_ukv_{tag}")
    r_rows = [(q_raw, 2 * C, 0), (kv, 2 * C, 0), (krr, LANE, 0)]
    q_rot, k_full, v_c = _rowwise(_f_crope, r_rows, trig, [], [rot], [(2 * C, BF16), (2 * C, BF16), (C, BF16)], 128,
                                  f"crope_{tag}")
    qc, kc, vc = (q_rot, 0, 2 * LANE), (k_full, 0, 2 * LANE), (v_c, 0, LANE)
    yc, lse, *moved_mla = _softmax_fwd(qc, kc, vc, cfg.Hc, (LANE + ROPE) ** -0.5, f"mla_fwd_{tag}", *cfg.tiles("mla_fwd", l),
                                       carry=carry_mla)
    (ycg,) = _rowwise(_f_gate, [(yc, C, 0), cfg.view(proj, "zc")], [], [p["g_oc"]], [], [(C, BF16)], 256, f"gate_c_{tag}")
    y = jnp.concatenate([ybg, ycg, ya], axis=1)
    out = _matmul(y, W["out"], "nn", F32, f"mm_out_{tag}", add=x)
    saved = dict(x=x, h=h, proj=proj, yb=yb, tot=tot, cqn=cqn, ckvn=ckvn, krr=krr, q_raw=q_raw, kv=kv,
                 q_rot=q_rot, k_full=k_full, v_c=v_c, yc=yc, lse=lse, y=y)
    return out, saved, (moved_stick[0] if moved_stick else []), (moved_mla[0] if moved_mla else [])


def _layer_bwd(cfg, l, dout, sv, W, p, cos2, sin2, rot, ext_stick, ext_mla, last):
    S, D, A, B, C = cfg.S, cfg.D, cfg.A, cfg.B, cfg.C
    tag = f"l{l}"
    proj = sv["proj"]
    dy = _matmul(dout, W["out"], "nt", BF16, f"mm_dy_{tag}")
    d_wout = _matmul(sv["y"], dout, "tn", BF16, f"mm_dwout_{tag}")
    wout_slots = _unperm_rows_out(cfg, d_wout).reshape(NDEV, cfg.DMIX // NDEV, D)
    (dyb, dzb), (dg_ob,), _ = _rowwise_vjp(_f_gate, [(sv["yb"], B, 0), cfg.view(proj, "zb")], [], [p["g_ob"]], [],
                                           [(dy, B, 0)], [BF16, BF16], 256, f"gate_b_bwd_{tag}")
    (dyc, dzc), (dg_oc,), _ = _rowwise_vjp(_f_gate, [(sv["yc"], C, 0), cfg.view(proj, "zc")], [], [p["g_oc"]], [],
                                           [(dy, C, B // C)], [BF16, BF16], 256, f"gate_c_bwd_{tag}")
    a_rows = [cfg.view(proj, "ua"), cfg.view(proj, "va"), cfg.view(proj, "za")]
    a_par = [p["g_v"], p["w_s"], p["b_s"], p["g_oa"]]
    (dua, dva, dza), (dg_v, dw_s, db_s, dg_oa), _ = _rowwise_vjp(
        _f_gmlp, a_rows, [], a_par, [], [(dy, A, (B + C) // A)], [BF16] * 3, LANE, f"gmlp_bwd_{tag}")
    qb, kb, vb = cfg.heads_view(proj, "qb"), cfg.heads_view(proj, "kb"), cfg.heads_view(proj, "vb")
    dqb, dkb, dvb, moved_stick = _stick_bwd(qb, kb, vb, (dyb, 0, LANE), sv["tot"], cfg.Hb, LANE ** -0.5,
                                            f"stick_bwd_{tag}", *cfg.tiles("stick_bwd", l),
                                            carry=_Exchange([[a] for a in ext_stick + [wout_slots]], False))
    got = dict(w_out=moved_stick[-1][0])
    ext_got = [mv[0] for mv in moved_stick[:-1]]
    qc, kc, vc = (sv["q_rot"], 0, 2 * LANE), (sv["k_full"], 0, 2 * LANE), (sv["v_c"], 0, LANE)
    dq_rot, dk_full, dv_c, *moved_mla = _softmax_bwd(qc, kc, vc, (sv["yc"], 0, LANE), (dyc, 0, LANE), sv["lse"], cfg.Hc,
                                                     (LANE + ROPE) ** -0.5, f"mla_bwd_{tag}", *cfg.tiles("mla_bwd", l),
                                                     carry=_Exchange([[a] for a in ext_mla], False) if ext_mla else None)
    ext_got += [mv[0] for mv in (moved_mla[0] if moved_mla else [])]
    trig = [(cos2, LANE, 0), (sin2, LANE, 0)]
    r_rows = [(sv["q_raw"], 2 * C, 0), (sv["kv"], 2 * C, 0), (sv["krr"], LANE, 0)]
    (dq_raw, dkv, dkrr), _, _ = _rowwise_vjp(_f_crope, r_rows, trig, [], [rot],
                                             [(dq_rot, 2 * C, 0), (dk_full, 2 * C, 0), (dv_c, C, 0)], [BF16] * 3, 128,
                                             f"crope_bwd_{tag}")
    dcqn = _matmul(dq_raw, W["uq"], "nn", BF16, f"mm_dcq_{tag}")
    d_wuq = _matmul(dq_raw, sv["cqn"], "tn", BF16, f"mm_dwuq_{tag}")
    dckvn = _matmul(dkv, W["ukv"], "nt", BF16, f"mm_dckv_{tag}")
    d_wukv = _matmul(sv["ckvn"], dkv, "tn", BF16, f"mm_dwukv_{tag}")
    c_rows = [cfg.view(proj, "cq"), cfg.view(proj, "ckv"), cfg.view(proj, "kr")]
    (dcq, dckv, dkr), (dg_q, dg_kv), _ = _rowwise_vjp(
        _f_cpre, c_rows, trig, [p["g_q"], p["g_kv"]], [rot],
        [(dcqn, cfg.Q, 0), (dckvn, cfg.KV, 0), (dkrr, LANE, 0)], [BF16] * 3, 256, f"cpre_bwd_{tag}")
    parts = dict(ua=dua, va=dva, za=dza, qb=dqb, kb=dkb, vb=dvb, zb=dzb, zc=dzc, cq=dcq, kr=dkr, ckv=dckv)
    cols, pos = [], 0
    for nm, off in sorted(cfg.off.items(), key=lambda kv_: kv_[1]):
        if off > pos:
            cols.append(jnp.zeros((S, off - pos), BF16))
        cols.append(parts[nm])
        pos = off + parts[nm].shape[1]
    if cfg.NP > pos:
        cols.append(jnp.zeros((S, cfg.NP - pos), BF16))
    dproj = jnp.concatenate(cols, axis=1)
    to_send = dict(c_w_uq=d_wuq.reshape(cfg.Hc, 2 * LANE, cfg.Q)[:, :LANE + ROPE].reshape(NDEV, -1, cfg.Q),
                   c_w_ukv=_to_slots_cols(d_wukv))
    half = D // 2
    d_win_a = _matmul(dproj, sv["h"][:, :half], "tn", BF16, f"mm_dwin_a_{tag}")
    slots_a = _unpad_w_in(cfg, d_win_a).reshape(NDEV, -1, half)
    if last:
        d_win_b, moved_a = _matmul(dproj, sv["h"][:, half:], "tn", BF16, f"mm_dwin_b_{tag}",
                                   carry=_Exchange([[slots_a]], False))
        dh, moved = _matmul(dproj, W["in"], "nn", BF16, f"mm_dh_{tag}",
                            carry=_Exchange([[_unpad_w_in(cfg, d_win_b).reshape(NDEV, -1, D - half)],
                                             [to_send["c_w_uq"]], [to_send["c_w_ukv"]]], False))
        got.update(w_in=(moved_a[0][0], moved[0][0]), c_w_uq=moved[1][0], c_w_ukv=moved[2][0])
        to_send = {}
    else:
        dh, moved_a = _matmul(dproj, W["in"], "nn", BF16, f"mm_dh_{tag}", carry=_Exchange([[slots_a]], False))
        d_win_b = _matmul(dproj, sv["h"][:, half:], "tn", BF16, f"mm_dwin_b_{tag}")
        got["w_in"] = (moved_a[0][0],)
        to_send["w_in"] = _unpad_w_in(cfg, d_win_b).reshape(NDEV, -1, D - half)
    (dx,), (dg_pre,), _ = _rowwise_vjp(_f_pre_res, [(sv["x"], D, 0)], [], [p["g_pre"]], [],
                                       [(dh, D, 0), (dout, D, 0)], [F32], 128, f"pre_bwd_{tag}")
    small = dict(g_pre=dg_pre[0], a_g_v=dg_v.reshape(cfg.G, LANE), a_w_s=dw_s, a_b_s=db_s[:, :, 0], c_g_q=dg_q[0],
                 c_g_kv=dg_kv[0], g_out=jnp.concatenate([dg_oa[0], dg_ob[0], dg_oc[0]]))
    return dx, small, got, to_send, ext_got


def _pack_small(vals):
    pieces = []
    for nm in SMALL:
        piece = vals[nm].reshape(-1, LANE)
        pieces.append(jnp.pad(piece, ((0, -piece.shape[0] % 8), (0, 0))))
    packed = jnp.concatenate(pieces, axis=0)
    return jnp.pad(packed, ((0, -packed.shape[0] % SMALL_ROWS), (0, 0)))


def _unpack_small(packed, like):
    out, row = {}, 0
    for nm in SMALL:
        n = like[nm].size // LANE
        out[nm] = packed[row:row + n].reshape(like[nm].shape)
        row += n + (-n % 8)
    return out


def kernel(x, positions, g_pre, w_in, a_g_v, a_w_s, a_b_s, c_g_q, c_g_kv, c_w_uq, c_w_ukv, g_out, w_out, g_final, loss_target, m_g_pre, m_w_in, m_a_g_v, m_a_w_s, m_a_b_s, m_c_g_q, m_c_g_kv, m_c_w_uq, m_c_w_ukv, m_g_out, m_w_out, m_g_final, v_g_pre, v_w_in, v_a_g_v, v_a_w_s, v_a_b_s, v_c_g_q, v_c_g_kv, v_c_w_uq, v_c_w_ukv, v_g_out, v_w_out, v_g_final):
    depth, S, D = w_in.shape[0], x.shape[1], x.shape[2]
    cfg = _Cfg(S, D, a_g_v.shape[1], c_g_q.shape[1], c_g_kv.shape[1], c_w_ukv.shape[2] * NDEV // (2 * LANE), g_out.shape[1])
    weights = dict(g_pre=g_pre, w_in=w_in, a_g_v=a_g_v, a_w_s=a_w_s, a_b_s=a_b_s, c_g_q=c_g_q, c_g_kv=c_g_kv,
                   c_w_uq=c_w_uq, c_w_ukv=c_w_ukv, g_out=g_out, w_out=w_out, g_final=g_final)
    mom_m = dict(g_pre=m_g_pre, w_in=m_w_in, a_g_v=m_a_g_v, a_w_s=m_a_w_s, a_b_s=m_a_b_s, c_g_q=m_c_g_q, c_g_kv=m_c_g_kv,
                 c_w_uq=m_c_w_uq, c_w_ukv=m_c_w_ukv, g_out=m_g_out, w_out=m_w_out, g_final=m_g_final)
    mom_v = dict(g_pre=v_g_pre, w_in=v_w_in, a_g_v=v_a_g_v, a_w_s=v_a_w_s, a_b_s=v_a_b_s, c_g_q=v_c_g_q, c_g_kv=v_c_g_kv,
                 c_w_uq=v_c_w_uq, c_w_ukv=v_c_w_ukv, g_out=v_g_out, w_out=v_w_out, g_final=v_g_final)
    big_names = ("w_in", "c_w_uq", "c_w_ukv", "w_out")

    inv_freq = 1.0 / (ROPE_THETA ** (jnp.arange(0, ROPE, 2, dtype=F32) / ROPE))
    ang = positions[0].astype(F32)[:, None] * inv_freq
    zpad = jnp.zeros((S, LANE - ROPE), F32)
    cos2 = jnp.concatenate([jnp.cos(ang), jnp.cos(ang), zpad], axis=1)
    sin2 = jnp.concatenate([jnp.sin(ang), jnp.sin(ang), zpad], axis=1)
    rot = _rope_matrix()

    for tree in (weights, mom_m, mom_v):
        for nm in TRANSPOSED:
            tree[nm] = jnp.swapaxes(tree[nm], 1, 2)

    def shards(l, names):
        return [[weights[nm][l].astype(BF16)] for nm in names]

    def assemble_rest(g_uq, g_ukv, g_wout):
        uq = jnp.pad(g_uq[0].reshape(cfg.Hc, LANE + ROPE, cfg.Q), ((0, 0), (0, LANE - ROPE), (0, 0)))
        return {"uq": uq.reshape(2 * cfg.C, cfg.Q), "ukv": _from_slots_cols(g_ukv[0]),
                "out": _perm_rows_out(cfg, g_wout[0].reshape(cfg.DMIX, D))}

    params = [_layer_params(cfg, l, g_pre, a_g_v, a_w_s, a_b_s, c_g_q, c_g_kv, g_out) for l in range(depth)]

    in_parts = [g[0] for g in _exchange(_GatherTwoLevel(shards(0, big_names[:1])), "gather_w_in_l0")]
    got_rest = None
    hcur, saved, Ws = x[0], [], []
    for l in range(depth):
        Ws.append({"in": _pad_w_in(cfg, in_parts)})
        if got_rest is not None:
            Ws[l].update(assemble_rest(*got_rest))
        nxt = l + 1 < depth
        early_rows = min(W_IN_EARLY_ROWS, weights["w_in"].shape[1] // 2)
        riding = ([] if got_rest is not None else shards(l, big_names[1:]))
        riding += [[weights["w_in"][l + 1][:early_rows].astype(BF16)]] if nxt else []
        in_parts = []

        def take(moved, l=l, rest_here=got_rest is None, nxt=nxt):
            if rest_here:
                Ws[l].update(assemble_rest(*moved[:3]))
            if nxt:
                in_parts.append(moved[-1][0])

        hcur, sv, got_late, got_rest = _layer_fwd(
            cfg, l, hcur, Ws[l], params[l], cos2, sin2, rot,
            carry_in=(_GatherTwoLevel(riding), take) if riding else None,
            carry_stick=_GatherTwoLevel([[weights["w_in"][l + 1][early_rows:].astype(BF16)]]) if nxt else None,
            carry_mla=_GatherTwoLevel(shards(l + 1, big_names[1:])) if nxt else None)
        in_parts += [g[0] for g in got_late]
        saved.append(sv)
    (dh,), (dg_final,), (loss_rows,) = _rowwise_vjp(
        _f_final, [(hcur, D, 0)], [(loss_target[0], D, 0)], [g_final[None]], [], [(jnp.ones((S, 1), F32), 1, 0)],
        [F32], 128, "final", primal=[(1, F32)])
    loss = lax.psum(jnp.sum(loss_rows), MESH_AXES)

    small_g, slots, pending = [None] * depth, [None] * depth, {}
    for l in reversed(range(depth)):
        ext_stick = [pending["w_in"]] if pending else []
        ext_mla = [pending["c_w_uq"], pending["c_w_ukv"]] if pending else []
        dh, small_g[l], slots[l], pending, ext_got = _layer_bwd(cfg, l, dh, saved[l], Ws[l], params[l], cos2, sin2, rot,
                                                                ext_stick, ext_mla, l == 0)
        if ext_got:
            slots[l + 1].update(w_in=slots[l + 1]["w_in"] + (ext_got[0],), c_w_uq=ext_got[1], c_w_ukv=ext_got[2])
    grad_x = dh[None]
    small_grads = {nm: jnp.stack([small_g[l][nm] for l in range(depth)]) for nm in SMALL if nm != "g_final"}
    small_grads["g_final"] = dg_final[0]
    small_gather = _Exchange([[_pack_small(small_grads)]], True)
    small_slots = None

    res = {}
    for nm in big_names:
        res[nm] = None
        for l in reversed(range(depth)):
            parts = slots[l][nm] if isinstance(slots[l][nm], tuple) else (slots[l][nm],)
            col0 = 0
            for i, part in enumerate(parts):
                out = _adamw(part, weights[nm], mom_m[nm], mom_v[nm], l, res[nm], f"adamw_{nm}_l{l}_{i}", col0,
                             carry=small_gather if small_slots is None else None)
                if small_slots is None:
                    res[nm], (small_slots,) = out
                else:
                    res[nm] = out
                col0 += part.shape[2]
        if nm in TRANSPOSED:
            res[nm] = [jnp.swapaxes(r, 1, 2) for r in res[nm]]
    packed = _adamw(small_slots[0], _pack_small(weights)[None], _pack_small(mom_m)[None], _pack_small(mom_v)[None], 0, None,
                    "adamw_small")
    small_res = [_unpack_small(r[0], weights) for r in packed]
    order = ("g_pre", "w_in", "a_g_v", "a_w_s", "a_b_s", "c_g_q", "c_g_kv", "c_w_uq", "c_w_ukv", "g_out", "w_out", "g_final")
    outs = [loss, grad_x]
    for kind in range(4):
        outs += [small_res[kind][nm] if nm in SMALL else res[nm][kind] for nm in order]
    return tuple(outs)
```

```python
import functools

import numpy as np
import jax
import jax.numpy as jnp
from jax import lax
from jax.experimental import pallas as pl
from jax.experimental.pallas import tpu as pltpu

NDEV = 8
MESH_AXES = ("x", "y", "c")
LANE = 128
ROPE = 64
EPS = 1e-6
ROPE_THETA = 10000.0
ADAM_LR, ADAM_B1, ADAM_B2, ADAM_EPS, ADAM_WD, ADAM_STEP = 0.001, 0.9, 0.999, 1e-08, 0.01, 10
VMEM_LIMIT = 48 * 1024 * 1024
ADAM_TILE_BYTES = 768 * 1024
W_IN_EARLY_ROWS = 512
CARRY_MID_PERCENT = 80
SMALL_ROWS = 256
ATTN_TILES = {"stick_fwd": [(2048, 256)], "stick_bwd": [(2048, 256)], "mla_fwd": [(512, 1024)], "mla_bwd": [(2048, 512)]}
DIAG_BLOCK = 256
F32, BF16 = jnp.float32, jnp.bfloat16
SMALL = ("g_pre", "a_g_v", "a_w_s", "a_b_s", "c_g_q", "c_g_kv", "g_out", "g_final")
TRANSPOSED = ("w_in", "c_w_uq")


def _tile(dim, cap, mult=LANE):
    if dim <= cap:
        return dim
    t = (cap // mult) * mult
    while t >= mult:
        if dim % t == 0:
            return t
        t -= mult
    return dim


def _dot_nt(a, b):
    return lax.dot_general(a, b, (((1,), (1,)), ((), ())), preferred_element_type=F32)


def _dot_tn(a, b):
    return lax.dot_general(a, b, (((0,), (0,)), ((), ())), preferred_element_type=F32)


def _dot(a, b):
    return jnp.dot(a, b, preferred_element_type=F32)


class _Exchange:
    def __init__(self, groups, gather):
        self.groups, self.gather = groups, gather
        self.flat = [(gi, li, a) for gi, grp in enumerate(groups) for li, a in enumerate(grp)]
        self.n = len(self.flat)
        self.args = [a for (_, _, a) in self.flat]
        self.out_shape = [jax.ShapeDtypeStruct((len(grp), NDEV) + tuple(grp[0].shape[-2:]), grp[0].dtype) for grp in groups]
        self.scratch = [pltpu.SemaphoreType.DMA((self.n, NDEV - 1)), pltpu.SemaphoreType.DMA((self.n, NDEV - 1)),
                        pltpu.SemaphoreType.DMA((self.n,))]

    def _copies(self, ins, outs, send_sems, recv_sems, local_sems, landings):
        x, y, c = lax.axis_index("x"), lax.axis_index("y"), lax.axis_index("c")
        me = 4 * x + 2 * y + c
        owns = [pltpu.make_async_copy(ins[i] if self.gather else ins[i].at[me], outs[gi].at[li, me], local_sems.at[i])
                for i, (gi, li, _) in enumerate(self.flat)]
        pairs = []
        for k in range(1, NDEV):
            px = 1 - x if k & 4 else x
            py = 1 - y if k & 2 else y
            pc = 1 - c if k & 1 else c
            peer = 4 * px + 2 * py + pc
            for i, (gi, li, _) in enumerate(self.flat):
                src = ins[i] if self.gather else ins[i].at[peer]
                sems = dict(send_sem=send_sems.at[i, k - 1], recv_sem=recv_sems.at[i, k - 1],
                            device_id=(px, py, pc), device_id_type=pl.DeviceIdType.MESH)
                out = pltpu.make_async_remote_copy(src_ref=src, dst_ref=outs[gi].at[li, me], **sems)
                landing = pltpu.make_async_remote_copy(src_ref=src, dst_ref=outs[gi].at[li, peer], **sems) if landings else None
                pairs.append((out, landing))
        return owns, pairs

    def start(self, ins, outs, sems):
        owns, pairs = self._copies(ins, outs, *sems, landings=False)
        for own in owns:
            own.start()
        for out, _ in pairs:
            out.start()

    def mid(self, ins, outs, sems):
        pass

    def wait(self, ins, outs, sems):
        owns, pairs = self._copies(ins, outs, *sems, landings=True)
        for out, landing in pairs:
            out.wait_send()
            landing.wait_recv()
        for own in owns:
            own.wait()


class _GatherTwoLevel(_Exchange):
    def __init__(self, groups):
        super().__init__(groups, True)

    def _copy(self, i, k, ins, outs, send_sems, recv_sems, landing):
        gi, li, _ = self.flat[i]
        x, y, c = lax.axis_index("x"), lax.axis_index("y"), lax.axis_index("c")
        chips = [(x, y), (1 - x, y), (x, 1 - y), (1 - x, 1 - y)]

        def slot(chip, core):
            return outs[gi].at[li, 4 * chip[0] + 2 * chip[1] + core]

        if k == 0:
            to, src, dst, lands = (x, y, 1 - c), ins[i], slot(chips[0], c), slot(chips[0], 1 - c)
        elif k <= 3:
            to, src, dst, lands = (*chips[k], c), ins[i], slot(chips[0], c), slot(chips[k], c)
        else:
            to, src, dst, lands = (x, y, 1 - c), slot(chips[k - 3], c), slot(chips[k - 3], c), slot(chips[k - 3], 1 - c)
        return pltpu.make_async_remote_copy(src_ref=src, dst_ref=lands if landing else dst, send_sem=send_sems.at[i, k],
                                            recv_sem=recv_sems.at[i, k], device_id=to, device_id_type=pl.DeviceIdType.MESH)

    def _own(self, i, ins, outs, local_sems):
        gi, li, _ = self.flat[i]
        me = 4 * lax.axis_index("x") + 2 * lax.axis_index("y") + lax.axis_index("c")
        return pltpu.make_async_copy(ins[i], outs[gi].at[li, me], local_sems.at[i])

    def start(self, ins, outs, sems):
        send_sems, recv_sems, local_sems = sems
        for i in range(self.n):
            self._own(i, ins, outs, local_sems).start()
        for k in range(4):
            for i in range(self.n):
                self._copy(i, k, ins, outs, send_sems, recv_sems, False).start()

    def mid(self, ins, outs, sems):
        send_sems, recv_sems, _ = sems
        for k in range(1, 4):
            for i in range(self.n):
                self._copy(i, k, ins, outs, send_sems, recv_sems, True).wait_recv()
                self._copy(i, k + 3, ins, outs, send_sems, recv_sems, False).start()

    def wait(self, ins, outs, sems):
        send_sems, recv_sems, local_sems = sems
        for k in (0, 4, 5, 6):
            for i in range(self.n):
                self._copy(i, k, ins, outs, send_sems, recv_sems, True).wait_recv()
        for k in range(NDEV - 1):
            for i in range(self.n):
                self._copy(i, k, ins, outs, send_sems, recv_sems, False).wait_send()
        for i in range(self.n):
            self._own(i, ins, outs, local_sems).wait()


def _call(body, name, grid, in_specs, out_specs, out_shape, scratch, semantics, args, carry=None, aliases=None):
    n_in, n_out, n_scr = len(in_specs), len(out_specs), len(scratch)
    if carry is None:
        run = body
    else:
        semantics = ("arbitrary",) * len(grid)
        anyspec = pl.BlockSpec(memory_space=pl.ANY)
        in_specs = list(in_specs) + [anyspec] * carry.n
        out_specs = list(out_specs) + [anyspec] * len(carry.groups)
        out_shape = list(out_shape) + carry.out_shape
        scratch = list(scratch) + carry.scratch
        args = list(args) + carry.args

        def run(*refs):
            c_in, x_in = refs[:n_in], refs[n_in:n_in + carry.n]
            rest = refs[n_in + carry.n:]
            c_out, x_out = rest[:n_out], rest[n_out:n_out + len(carry.groups)]
            c_scr, sems = rest[n_out + len(carry.groups):len(rest) - 3], rest[len(rest) - 3:]
            step, total = 0, 1
            for d, extent in enumerate(grid):
                step = step * extent + pl.program_id(d)
                total *= extent

            @pl.when(step == 0)
            def _():
                carry.start(x_in, x_out, sems)

            body(*c_in, *c_out, *c_scr)

            @pl.when(step == (total * CARRY_MID_PERCENT) // 100)
            def _():
                carry.mid(x_in, x_out, sems)

            @pl.when(step == total - 1)
            def _():
                carry.wait(x_in, x_out, sems)

    res = pl.pallas_call(
        run, name=name, grid=grid, out_shape=list(out_shape), in_specs=list(in_specs), out_specs=list(out_specs),
        scratch_shapes=list(scratch), input_output_aliases=aliases or {},
        compiler_params=pltpu.CompilerParams(dimension_semantics=semantics, vmem_limit_bytes=VMEM_LIMIT,
                                             has_side_effects=carry is not None),
    )(*args)
    return list(res[:n_out]), list(res[n_out:])


def _exchange(ex, name):
    groups = ex.groups

    def body(*refs):
        ins, outs, sems = refs[:ex.n], refs[ex.n:ex.n + len(groups)], refs[ex.n + len(groups):]
        ex.start(ins, outs, sems)
        ex.mid(ins, outs, sems)
        ex.wait(ins, outs, sems)

    anyspec = pl.BlockSpec(memory_space=pl.ANY)
    return pl.pallas_call(
        body, name=name, out_shape=ex.out_shape, in_specs=[anyspec] * ex.n, out_specs=[anyspec] * len(groups),
        scratch_shapes=ex.scratch, compiler_params=pltpu.CompilerParams(has_side_effects=True),
    )(*ex.args)


def _matmul(a, b, mode, out_dtype, name, add=None, tm=1024, tn=1024, tk=2048, carry=None, b_cols=None):
    if mode == "tn":
        (K, M), (K2, N) = a.shape, b.shape
    elif mode == "nt":
        (M, K), (N, K2) = a.shape, b.shape
    else:
        (M, K), (K2, N) = a.shape, b.shape
    assert K == K2, (a.shape, b.shape, mode)
    col0 = 0
    if b_cols is not None:
        assert mode != "nt"
        col0, N = b_cols
    tm, tn, tk = _tile(M, tm), _tile(N, tn), _tile(K, tk)
    assert col0 % tn == 0
    nk, jb = K // tk, col0 // tn
    a_spec = pl.BlockSpec((tk, tm), lambda i, j, k: (k, i)) if mode == "tn" else pl.BlockSpec((tm, tk), lambda i, j, k: (i, k))
    b_spec = pl.BlockSpec((tn, tk), lambda i, j, k: (j, k)) if mode == "nt" else pl.BlockSpec((tk, tn), lambda i, j, k: (k, j + jb))
    dot = {"nn": _dot, "nt": _dot_nt, "tn": _dot_tn}[mode]
    has_add = add is not None

    def body(*refs):
        a_ref, b_ref = refs[0], refs[1]
        o_ref, acc = refs[-2], refs[-1]
        k = pl.program_id(2)

        @pl.when(k == 0)
        def _():
            acc[...] = jnp.zeros_like(acc)

        acc[...] += dot(a_ref[...].astype(BF16), b_ref[...].astype(BF16))

        @pl.when(k == nk - 1)
        def _():
            r = acc[...]
            if has_add:
                r = r + refs[2][...]
            o_ref[...] = r.astype(o_ref.dtype)

    in_specs = [a_spec, b_spec]
    args = [a, b]
    if has_add:
        in_specs.append(pl.BlockSpec((tm, tn), lambda i, j, k: (i, j)))
        args.append(add)
    (out,), moved = _call(body, name, (M // tm, N // tn, nk), in_specs, [pl.BlockSpec((tm, tn), lambda i, j, k: (i, j))],
                          [jax.ShapeDtypeStruct((M, N), out_dtype)], [pltpu.VMEM((tm, tn), F32)],
                          ("parallel", "parallel", "arbitrary"), args, carry)
    return out if carry is None else (out, moved)


def _row_specs(views, tile):
    return [pl.BlockSpec((tile, w), functools.partial(lambda i, cb: (i, cb), cb=cb)) for (_, w, cb) in views]


def _full_specs(arrs):
    return [pl.BlockSpec(p.shape, functools.partial(lambda i, nd: (0,) * nd, nd=p.ndim)) for p in arrs]


def _rowwise(fn, rows, aux, params, consts, outs, tile, name):
    S = rows[0][0].shape[0]
    nr, na, npar, nc = len(rows), len(aux), len(params), len(consts)

    def body(*refs):
        ins = [r[...].astype(F32) for r in refs[:nr + na]]
        small = [r[...] for r in refs[nr + na:nr + na + npar + nc]]
        res = fn(*ins, *small)
        for o_ref, r in zip(refs[nr + na + npar + nc:], res):
            o_ref[...] = r.astype(o_ref.dtype)

    return pl.pallas_call(
        body, name=name, grid=(S // tile,),
        out_shape=[jax.ShapeDtypeStruct((S, w), dt) for (w, dt) in outs],
        in_specs=_row_specs(rows + aux, tile) + _full_specs(params + consts),
        out_specs=[pl.BlockSpec((tile, w), lambda i: (i, 0)) for (w, _) in outs],
        compiler_params=pltpu.CompilerParams(dimension_semantics=("parallel",), vmem_limit_bytes=VMEM_LIMIT),
    )(*[v[0] for v in rows + aux], *params, *consts)


def _rowwise_vjp(fn, rows, aux, params, consts, cots, grad_dtypes, tile, name, primal=()):
    S = rows[0][0].shape[0]
    nr, na, npar, nc, nct, npr = len(rows), len(aux), len(params), len(consts), len(cots), len(primal)

    def body(*refs):
        n_in = nr + na + npar + nc + nct
        rv = [r[...].astype(F32) for r in refs[:nr]]
        av = [r[...].astype(F32) for r in refs[nr:nr + na]]
        pv = [r[...] for r in refs[nr + na:nr + na + npar]]
        cv = [r[...] for r in refs[nr + na + npar:nr + na + npar + nc]]
        ct = tuple(r[...].astype(F32) for r in refs[nr + na + npar + nc:n_in])
        res, vjp = jax.vjp(lambda *rp: tuple(fn(*rp[:nr], *av, *rp[nr:], *cv)), *rv, *pv)
        grads = vjp(ct)
        g_refs = refs[n_in:n_in + nr]
        p_refs = refs[n_in + nr:n_in + nr + npar]
        o_refs = refs[n_in + nr + npar:]
        for g_ref, g in zip(g_refs, grads[:nr]):
            g_ref[...] = g.astype(g_ref.dtype)

        @pl.when(pl.program_id(0) == 0)
        def _():
            for p_ref in p_refs:
                p_ref[...] = jnp.zeros_like(p_ref)

        for p_ref, g in zip(p_refs, grads[nr:]):
            p_ref[...] += g
        for o_ref, r in zip(o_refs, res[:npr]):
            o_ref[...] = r.astype(o_ref.dtype)

    out_shape = ([jax.ShapeDtypeStruct((S, w), dt) for (_, w, _), dt in zip(rows, grad_dtypes)]
                 + [jax.ShapeDtypeStruct(p.shape, F32) for p in params]
                 + [jax.ShapeDtypeStruct((S, w), dt) for (w, dt) in primal])
    out_specs = ([pl.BlockSpec((tile, w), lambda i: (i, 0)) for (_, w, _) in rows] + _full_specs(params)
                 + [pl.BlockSpec((tile, w), lambda i: (i, 0)) for (w, _) in primal])
    res = pl.pallas_call(
        body, name=name, grid=(S // tile,), out_shape=out_shape,
        in_specs=_row_specs(rows + aux, tile) + _full_specs(params + consts) + _row_specs(cots, tile),
        out_specs=out_specs,
        compiler_params=pltpu.CompilerParams(dimension_semantics=("arbitrary",), vmem_limit_bytes=VMEM_LIMIT),
    )(*[v[0] for v in rows + aux], *params, *consts, *[v[0] for v in cots])
    return res[:nr], res[nr:nr + npar], res[nr + npar:]


@jax.custom_vjp
def _mm(a, b):
    return _dot(a.astype(BF16), b.astype(BF16))


def _mm_fwd(a, b):
    return _mm(a, b), (a, b)


def _mm_bwd(res, ct):
    a, b = res
    ctb = ct.astype(BF16)
    return _dot_nt(ctb, b.astype(BF16)), _dot_tn(a.astype(BF16), ctb)


_mm.defvjp(_mm_fwd, _mm_bwd)


def _rms(x, g):
    return x * lax.rsqrt(jnp.mean(x * x, axis=-1, keepdims=True) + EPS) * g


def _f_pre(x, g):
    return (_rms(x, g),)


def _f_pre_res(x, g):
    return _rms(x, g), x


def _f_gate(y, z, g):
    return (_rms(y, g) * jax.nn.silu(z),)


def _f_gmlp(u, v, z, g_v, w_s, b_s, g_o):
    groups = w_s.shape[0]
    u, v = jax.nn.gelu(u), jax.nn.gelu(v)
    t_idx = lax.broadcasted_iota(jnp.int32, (LANE, LANE), 0)
    s_idx = lax.broadcasted_iota(jnp.int32, (LANE, LANE), 1)
    ys = []
    for g in range(groups):
        sl = slice(g * LANE, (g + 1) * LANE)
        vn = _rms(v[:, sl], g_v[:, sl])
        w = jnp.where(s_idx <= t_idx, w_s[g], 0.0)
        ys.append(u[:, sl] * (_mm(w, vn) + b_s[g]))
    return (_rms(jnp.concatenate(ys, axis=1), g_o) * jax.nn.silu(z),)


def _rope(x, cos2, sin2, rot):
    return x * cos2 + _mm(x, rot) * sin2


def _f_cpre(cq, ckv, kr, cos2, sin2, g_q, g_kv, rot):
    return _rms(cq, g_q), _rms(ckv, g_kv), _rope(kr, cos2, sin2, rot)


def _f_crope(q, kv, krr, cos2, sin2, rot):
    heads = q.shape[1] // (2 * LANE)
    qs, ks, vs = [], [], []
    for h in range(heads):
        lo, mid, hi = 2 * h * LANE, (2 * h + 1) * LANE, (2 * h + 2) * LANE
        qs += [q[:, lo:mid], _rope(q[:, mid:hi], cos2, sin2, rot)]
        ks += [kv[:, lo:mid], krr]
        vs += [kv[:, mid:hi]]
    return jnp.concatenate(qs, axis=1), jnp.concatenate(ks, axis=1), jnp.concatenate(vs, axis=1)


def _f_final(h, target, g):
    err = _rms(h, g) - target
    return (0.5 * jnp.mean(err * err, axis=-1, keepdims=True),)


def _rope_matrix():
    r = np.zeros((LANE, LANE), np.float32)
    half = ROPE // 2
    for i in range(half):
        r[i + half, i] = -1.0
        r[i, i + half] = 1.0
    return jnp.asarray(r)


def _head_spec(view, rows, n_rows_block):
    _, cb0, w = view
    if n_rows_block:
        return pl.BlockSpec((rows, w), functools.partial(lambda h, i, cb0: (i, cb0 + h), cb0=cb0))
    return pl.BlockSpec((rows, w), functools.partial(lambda h, i, cb0: (0, cb0 + h), cb0=cb0))


def _stat_spec(tq):
    return pl.BlockSpec((1, tq, 1), lambda h, i: (h, i, 0))


def _softplus(z):
    return jnp.maximum(z, 0.0) + jnp.log(1.0 + jnp.exp(-jnp.abs(z)))


def _cumsum_mm(x, m01):
    hi = x.astype(BF16)
    lo = (x - hi.astype(F32)).astype(BF16)
    return _dot(hi, m01) + _dot(lo, m01)


def _attn_call(body, name, heads, S, tq, ins, in_blocked, outs, out_blocked, scratch, stats_in=0, stats_out=0, carry=None):
    in_specs = [_head_spec(v, tq if blk else S, blk) for v, blk in zip(ins[:len(ins) - stats_in], in_blocked)]
    in_specs += [_stat_spec(tq)] * stats_in
    out_specs = [_head_spec((None, 0, w), tq if blk else S, blk) for (w, _), blk in zip(outs, out_blocked)]
    out_specs += [_stat_spec(tq)] * stats_out
    out_shape = [jax.ShapeDtypeStruct((S, heads * w), dt) for (w, dt) in outs]
    out_shape += [jax.ShapeDtypeStruct((heads, S, 1), F32)] * stats_out
    args = [v[0] for v in ins[:len(ins) - stats_in]] + list(ins[len(ins) - stats_in:])
    res, moved = _call(body, name, (heads, S // tq), in_specs, out_specs, out_shape, scratch, ("arbitrary", "arbitrary"),
                       args, carry)
    return res if carry is None else res + [moved]


def _softmax_fwd(q, k, v, heads, scale, name, tq, bk, carry=None):
    S, dv = q[0].shape[0], v[2]

    def body(q_ref, k_ref, v_ref, o_ref, lse_ref):
        qi = pl.program_id(1)
        qv = q_ref[...]
        row = qi * tq + lax.broadcasted_iota(jnp.int32, (tq, bk), 0)
        col0 = lax.broadcasted_iota(jnp.int32, (tq, bk), 1)

        def step(kb, carry):
            m, l, acc = carry
            sl = pl.ds(pl.multiple_of(kb * bk, bk), bk)
            s = _dot_nt(qv, k_ref[sl, :]) * scale
            s = jnp.where(kb * bk + col0 <= row, s, -1e30)
            m_new = jnp.maximum(m, jnp.max(s, axis=1, keepdims=True))
            p = jnp.exp(s - m_new)
            alpha = jnp.exp(m - m_new)
            l = alpha * l + jnp.sum(p, axis=1, keepdims=True)
            acc = alpha * acc + _dot(p.astype(BF16), v_ref[sl, :])
            return m_new, l, acc

        n_kb = (qi * tq + tq + bk - 1) // bk
        m, l, acc = lax.fori_loop(0, n_kb, step, (jnp.full((tq, 1), -1e30, F32), jnp.zeros((tq, 1), F32),
                                                  jnp.zeros((tq, dv), F32)))
        o_ref[...] = (acc / l).astype(o_ref.dtype)
        lse_ref[0] = m + jnp.log(l)

    return _attn_call(body, name, heads, S, tq, [q, k, v], [1, 0, 0], [(dv, BF16)], [1], [], stats_out=1, carry=carry)


def _softmax_bwd(q, k, v, o, do, lse, heads, scale, name, tq, bk, carry=None):
    S, dq_w, dv = q[0].shape[0], q[2], v[2]
    nq = S // tq

    bd = min(DIAG_BLOCK, tq)
    assert tq % bk == 0 and tq % bd == 0

    def body(q_ref, k_ref, v_ref, o_ref, do_ref, lse_ref, dq_ref, dk_ref, dv_ref, dk_acc, dv_acc, delta_scr, dq_scr):
        qi = pl.program_id(1)

        @pl.when(qi == 0)
        def _():
            dk_acc[...] = jnp.zeros_like(dk_acc)
            dv_acc[...] = jnp.zeros_like(dv_acc)

        delta_scr[...] = jnp.sum(do_ref[...].astype(F32) * o_ref[...].astype(F32), axis=1, keepdims=True)
        dq_scr[...] = jnp.zeros_like(dq_scr)

        def block(r0, sl, width, masked):
            qv, dov = q_ref[r0:, :], do_ref[r0:, :]
            ks, vs = k_ref[sl, :], v_ref[sl, :]
            p = jnp.exp(_dot_nt(qv, ks) * scale - lse_ref[0, r0:, :])
            if masked:
                shape = (tq - r0, width)
                p = jnp.where(lax.broadcasted_iota(jnp.int32, shape, 1) <= lax.broadcasted_iota(jnp.int32, shape, 0), p, 0.0)
            ds = (p * (_dot_nt(dov, vs) - delta_scr[r0:, :]) * scale).astype(BF16)
            dk_acc[sl, :] += _dot_tn(ds, qv)
            dv_acc[sl, :] += _dot_tn(p.astype(BF16), dov)
            dq_scr[r0:, :] += _dot(ds, ks)

        def step(kb, _):
            block(0, pl.ds(pl.multiple_of(kb * bk, bk), bk), bk, False)
            return 0

        lax.fori_loop(0, qi * (tq // bk), step, 0)
        for j in range(tq // bd):
            block(j * bd, pl.ds(pl.multiple_of(qi * tq + j * bd, bd), bd), bd, True)
        dq_ref[...] = dq_scr[...].astype(dq_ref.dtype)

        @pl.when(qi == nq - 1)
        def _():
            dk_ref[...] = dk_acc[...].astype(dk_ref.dtype)
            dv_ref[...] = dv_acc[...].astype(dv_ref.dtype)

    return _attn_call(body, name, heads, S, tq, [q, k, v, o, do, lse], [1, 0, 0, 1, 1],
                      [(dq_w, BF16), (dq_w, BF16), (dv, BF16)], [1, 0, 0],
                      [pltpu.VMEM((S, dq_w), F32), pltpu.VMEM((S, dv), F32), pltpu.VMEM((tq, 1), F32),
                       pltpu.VMEM((tq, dq_w), F32)], stats_in=1, carry=carry)


def _stick_fwd(q, k, v, heads, scale, name, tq, bk, carry=None):
    S, dv = q[0].shape[0], v[2]

    assert tq % bk == 0
    n_sub = tq // bk

    def body(q_ref, k_ref, v_ref, o_ref, tot_ref, c_scr, acc_scr):
        qi = pl.program_id(1)
        m_gt = (lax.broadcasted_iota(jnp.int32, (bk, bk), 0) > lax.broadcasted_iota(jnp.int32, (bk, bk), 1)).astype(BF16)

        def block(r0, sl, masked):
            rows = tq - r0
            z = _dot_nt(q_ref[r0:, :], k_ref[sl, :]) * scale
            sp = _softplus(z)
            lk = -sp
            if masked:
                mask = lax.broadcasted_iota(jnp.int32, (rows, bk), 1) < lax.broadcasted_iota(jnp.int32, (rows, bk), 0)
                lk = jnp.where(mask, lk, 0.0)
            after = _cumsum_mm(lk, m_gt) + c_scr[r0:, :]
            a = jnp.exp(z - sp + after)
            if masked:
                a = jnp.where(mask, a, 0.0)
            acc_scr[r0:, :] += _dot(a.astype(BF16), v_ref[sl, :])
            c_scr[r0:, :] += jnp.sum(lk, axis=1, keepdims=True)

        c_scr[...] = jnp.zeros_like(c_scr)
        acc_scr[...] = jnp.zeros_like(acc_scr)
        for j in reversed(range(n_sub)):
            block(j * bk, pl.ds(pl.multiple_of(qi * tq + j * bk, bk), bk), True)

        def step(it, _):
            block(0, pl.ds(pl.multiple_of((qi * n_sub - 1 - it) * bk, bk), bk), False)
            return 0

        lax.fori_loop(0, qi * n_sub, step, 0)
        o_ref[...] = acc_scr[...].astype(o_ref.dtype)
        tot_ref[0] = c_scr[...]

    return _attn_call(body, name, heads, S, tq, [q, k, v], [1, 0, 0], [(dv, BF16)], [1],
                      [pltpu.VMEM((tq, 1), F32), pltpu.VMEM((tq, dv), F32)], stats_out=1, carry=carry)


def _stick_bwd(q, k, v, do, tot, heads, scale, name, tq, bk, carry=None):
    S, dq_w, dv = q[0].shape[0], q[2], v[2]
    nq = S // tq

    assert tq % bk == 0
    n_sub = tq // bk

    def body(q_ref, k_ref, v_ref, do_ref, tot_ref, dq_ref, dk_ref, dv_ref, dk_acc, dv_acc, pc_scr, gc_scr, dq_scr):
        qi = pl.program_id(1)

        @pl.when(qi == 0)
        def _():
            dk_acc[...] = jnp.zeros_like(dk_acc)
            dv_acc[...] = jnp.zeros_like(dv_acc)

        j_idx = lax.broadcasted_iota(jnp.int32, (bk, bk), 0)
        s_idx = lax.broadcasted_iota(jnp.int32, (bk, bk), 1)
        m_le, m_lt = (j_idx <= s_idx).astype(BF16), (j_idx < s_idx).astype(BF16)

        def block(r0, sl, masked):
            rows = tq - r0
            qv, dov = q_ref[r0:, :], do_ref[r0:, :]
            ks, vs = k_ref[sl, :], v_ref[sl, :]
            z = _dot_nt(qv, ks) * scale
            sp = _softplus(z)
            lk = -sp
            if masked:
                mask = lax.broadcasted_iota(jnp.int32, (rows, bk), 1) < lax.broadcasted_iota(jnp.int32, (rows, bk), 0)
                lk = jnp.where(mask, lk, 0.0)
            after = tot_ref[0, r0:, :] - pc_scr[r0:, :] - _cumsum_mm(lk, m_le)
            log_beta = z - sp
            a = jnp.exp(log_beta + after)
            if masked:
                a = jnp.where(mask, a, 0.0)
            g = _dot_nt(dov, vs) * a
            cg = gc_scr[r0:, :] + _cumsum_mm(g, m_lt)
            dz = g * jnp.exp(-sp) - jnp.exp(log_beta) * cg
            if masked:
                dz = jnp.where(mask, dz, 0.0)
            dz = (dz * scale).astype(BF16)
            dk_acc[sl, :] += _dot_tn(dz, qv)
            dv_acc[sl, :] += _dot_tn(a.astype(BF16), dov)
            dq_scr[r0:, :] += _dot(dz, ks)
            pc_scr[r0:, :] += jnp.sum(lk, axis=1, keepdims=True)
            gc_scr[r0:, :] += jnp.sum(g, axis=1, keepdims=True)

        pc_scr[...] = jnp.zeros_like(pc_scr)
        gc_scr[...] = jnp.zeros_like(gc_scr)
        dq_scr[...] = jnp.zeros_like(dq_scr)

        def step(kb, _):
            block(0, pl.ds(pl.multiple_of(kb * bk, bk), bk), False)
            return 0

        lax.fori_loop(0, qi * n_sub, step, 0)
        for j in range(n_sub):
            block(j * bk, pl.ds(pl.multiple_of(qi * tq + j * bk, bk), bk), True)
        dq_ref[...] = dq_scr[...].astype(dq_ref.dtype)

        @pl.when(qi == nq - 1)
        def _():
            dk_ref[...] = dk_acc[...].astype(dk_ref.dtype)
            dv_ref[...] = dv_acc[...].astype(dv_ref.dtype)

    return _attn_call(body, name, heads, S, tq, [q, k, v, do, tot], [1, 0, 0, 1],
                      [(dq_w, BF16), (dq_w, BF16), (dv, BF16)], [1, 0, 0],
                      [pltpu.VMEM((S, dq_w), F32), pltpu.VMEM((S, dv), F32), pltpu.VMEM((tq, 1), F32),
                       pltpu.VMEM((tq, 1), F32), pltpu.VMEM((tq, dq_w), F32)], stats_in=1, carry=carry)


def _adamw(slots, w, m, v, layer, prev, name, col0=0, carry=None):
    _, R, C = slots.shape
    L, full_c = w.shape[0], w.shape[2]
    item = slots.dtype.itemsize
    tc = _tile(C, 2048)
    tr = _tile(R, max(16, ADAM_TILE_BYTES // (item * tc)), mult=16)
    if tr == R and R * tc * item > ADAM_TILE_BYTES:
        tc = _tile(C, max(LANE, ADAM_TILE_BYTES // (item * R)))
    c1, c2 = 1.0 - ADAM_B1 ** ADAM_STEP, 1.0 - ADAM_B2 ** ADAM_STEP
    n_prev = 0 if prev is None else 4

    def body(s_ref, w_ref, m_ref, v_ref, *rest):
        g_out, d_out, m_out, v_out = rest[n_prev:]
        g = s_ref[0].astype(F32)
        for k in range(1, NDEV):
            g = g + s_ref[k].astype(F32)
        m_new = ADAM_B1 * m_ref[0] + (1.0 - ADAM_B1) * g
        v_new = ADAM_B2 * v_ref[0] + (1.0 - ADAM_B2) * (g * g)
        g_out[0] = g
        m_out[0] = m_new
        v_out[0] = v_new
        d_out[0] = -ADAM_LR * ((m_new / c1) / (jnp.sqrt(v_new / c2) + ADAM_EPS) + ADAM_WD * w_ref[0])

    assert col0 % tc == 0
    spec = pl.BlockSpec((1, tr, tc), lambda i, j: (layer, i, j + col0 // tc))
    in_specs = [pl.BlockSpec((NDEV, tr, tc), lambda i, j: (0, i, j)), spec, spec, spec]
    in_specs += [pl.BlockSpec(memory_space=pl.ANY)] * n_prev
    res, moved = _call(body, name, (R // tr, C // tc), in_specs, [spec] * 4, [jax.ShapeDtypeStruct((L, R, full_c), F32)] * 4,
                       [], ("parallel", "parallel"), [slots, w, m, v, *(prev or [])], carry,
                       aliases={4 + i: i for i in range(n_prev)})
    return res if carry is None else (res, moved)


class _Cfg:
    def __init__(self, S, D, groups, q_lora, kv_lora, c_heads, d_mix):
        self.S, self.D, self.G, self.Q, self.KV, self.Hc, self.DMIX = S, D, groups, q_lora, kv_lora, c_heads, d_mix
        self.A, self.C = groups * LANE, c_heads * LANE
        self.B = d_mix - self.A - self.C
        self.Hb = self.B // LANE
        A, B, C = self.A, self.B, self.C
        assert B % LANE == 0 and B % C == 0 and (B + C) % A == 0
        self.ref_segs = [("ua", A), ("va", A), ("za", A), ("qb", B), ("kb", B), ("vb", B), ("zb", B),
                         ("cq", q_lora), ("ckv", kv_lora), ("kr", ROPE), ("zc", C)]
        self.off, off = {}, 0
        for nm, w in [("ua", A), ("va", A), ("za", A), ("qb", B), ("kb", B), ("vb", B), ("zb", B), ("zc", C),
                      ("cq", q_lora), ("kr", LANE), ("ckv", kv_lora)]:
            off = -(-off // w) * w
            self.off[nm] = off
            off += w
        self.NP = -(-off // 512) * 512
        self.width = {"kr": LANE, **{nm: w for nm, w in self.ref_segs if nm != "kr"}}

    def tiles(self, kind, layer):
        tq, bk = ATTN_TILES[kind][layer % len(ATTN_TILES[kind])]
        return min(tq, self.S), min(bk, self.S)

    def view(self, arr, nm):
        w = self.width[nm]
        return (arr, w, self.off[nm] // w)

    def heads_view(self, arr, nm):
        return (arr, self.off[nm] // LANE, LANE)


def _gathered_rows(parts, a, b):
    per = sum(p.shape[1] for p in parts)
    out = []
    while a < b:
        k, r = divmod(a, per)
        i = 0
        while r >= parts[i].shape[1]:
            r -= parts[i].shape[1]
            i += 1
        n = min(b - a, parts[i].shape[1] - r)
        out.append(parts[i][k, r:r + n])
        a += n
    return out


def _pad_w_in(cfg, parts):
    width_d, dtype = parts[0].shape[2], parts[0].dtype
    start_of, start = {}, 0
    for nm, width in cfg.ref_segs:
        start_of[nm] = (start, width)
        start += width
    rows, pos = [], 0
    for nm, off in sorted(cfg.off.items(), key=lambda kv: kv[1]):
        if off > pos:
            rows.append(jnp.zeros((off - pos, width_d), dtype))
        rows += _gathered_rows(parts, start_of[nm][0], start_of[nm][0] + start_of[nm][1])
        pos = off + start_of[nm][1]
    if cfg.NP > pos:
        rows.append(jnp.zeros((cfg.NP - pos, width_d), dtype))
    return jnp.concatenate(rows, axis=0)


def _unpad_w_in(cfg, wpt):
    return jnp.concatenate([wpt[cfg.off[nm]:cfg.off[nm] + width] for nm, width in cfg.ref_segs], axis=0)


def _to_slots_cols(w):
    R = w.shape[0]
    return w.reshape(R, NDEV, -1).transpose(1, 0, 2)


def _from_slots_cols(s):
    return s.transpose(1, 0, 2).reshape(s.shape[1], -1)


def _perm_rows_out(cfg, w):
    return jnp.concatenate([w[cfg.A:], w[:cfg.A]], axis=0)


def _unperm_rows_out(cfg, w):
    return jnp.concatenate([w[cfg.B + cfg.C:], w[:cfg.B + cfg.C]], axis=0)


def _layer_params(cfg, l, g_pre, a_g_v, a_w_s, a_b_s, c_g_q, c_g_kv, g_out):
    A, B = cfg.A, cfg.B
    return dict(g_pre=g_pre[l][None], g_v=a_g_v[l].reshape(1, A), w_s=a_w_s[l], b_s=a_b_s[l][:, :, None],
                g_q=c_g_q[l][None], g_kv=c_g_kv[l][None],
                g_oa=g_out[l][None, :A], g_ob=g_out[l][None, A:A + B], g_oc=g_out[l][None, A + B:])


def _layer_fwd(cfg, l, x, W, p, cos2, sin2, rot, carry_in=None, carry_stick=None, carry_mla=None):
    S, D, A, B, C = cfg.S, cfg.D, cfg.A, cfg.B, cfg.C
    tag = f"l{l}"
    (h,) = _rowwise(_f_pre, [(x, D, 0)], [], [p["g_pre"]], [], [(D, BF16)], 256, f"pre_{tag}")
    if carry_in is None:
        proj = _matmul(h, W["in"], "nt", BF16, f"mm_in_{tag}")
    else:
        proj, moved_in = _matmul(h, W["in"], "nt", BF16, f"mm_in_{tag}", carry=carry_in[0])
        carry_in[1](moved_in)
    a_rows = [cfg.view(proj, "ua"), cfg.view(proj, "va"), cfg.view(proj, "za")]
    a_par = [p["g_v"], p["w_s"], p["b_s"], p["g_oa"]]
    (ya,) = _rowwise(_f_gmlp, a_rows, [], a_par, [], [(A, BF16)], LANE, f"gmlp_{tag}")
    qb, kb, vb = cfg.heads_view(proj, "qb"), cfg.heads_view(proj, "kb"), cfg.heads_view(proj, "vb")
    yb, tot, *moved_stick = _stick_fwd(qb, kb, vb, cfg.Hb, LANE ** -0.5, f"stick_fwd_{tag}", *cfg.tiles("stick_fwd", l),
                                       carry=carry_stick)
    (ybg,) = _rowwise(_f_gate, [(yb, B, 0), cfg.view(proj, "zb")], [], [p["g_ob"]], [], [(B, BF16)], 256, f"gate_b_{tag}")
    c_rows = [cfg.view(proj, "cq"), cfg.view(proj, "ckv"), cfg.view(proj, "kr")]
    trig = [(cos2, LANE, 0), (sin2, LANE, 0)]
    cqn, ckvn, krr = _rowwise(_f_cpre, c_rows, trig, [p["g_q"], p["g_kv"]], [rot],
                              [(cfg.Q, BF16), (cfg.KV, BF16), (LANE, BF16)], 256, f"cpre_{tag}")
    q_raw = _matmul(cqn, W["uq"], "nt", BF16, f"mm_uq_{tag}")
    kv = _matmul(ckvn, W["ukv"], "nn", BF16, f"mm_ukv_{tag}")
    r_rows = [(q_raw, 2 * C, 0), (kv, 2 * C, 0), (krr, LANE, 0)]
    q_rot, k_full, v_c = _rowwise(_f_crope, r_rows, trig, [], [rot], [(2 * C, BF16), (2 * C, BF16), (C, BF16)], 128,
                                  f"crope_{tag}")
    qc, kc, vc = (q_rot, 0, 2 * LANE), (k_full, 0, 2 * LANE), (v_c, 0, LANE)
    yc, lse, *moved_mla = _softmax_fwd(qc, kc, vc, cfg.Hc, (LANE + ROPE) ** -0.5, f"mla_fwd_{tag}", *cfg.tiles("mla_fwd", l),
                                       carry=carry_mla)
    (ycg,) = _rowwise(_f_gate, [(yc, C, 0), cfg.view(proj, "zc")], [], [p["g_oc"]], [], [(C, BF16)], 256, f"gate_c_{tag}")
    y = jnp.concatenate([ybg, ycg, ya], axis=1)
    out = _matmul(y, W["out"], "nn", F32, f"mm_out_{tag}", add=x)
    saved = dict(x=x, h=h, proj=proj, yb=yb, tot=tot, cqn=cqn, ckvn=ckvn, krr=krr, q_raw=q_raw, kv=kv,
                 q_rot=q_rot, k_full=k_full, v_c=v_c, yc=yc, lse=lse, y=y)
    return out, saved, (moved_stick[0] if moved_stick else []), (moved_mla[0] if moved_mla else [])


def _layer_bwd(cfg, l, dout, sv, W, p, cos2, sin2, rot, ext_stick, ext_mla, last):
    S, D, A, B, C = cfg.S, cfg.D, cfg.A, cfg.B, cfg.C
    tag = f"l{l}"
    proj = sv["proj"]
    dy = _matmul(dout, W["out"], "nt", BF16, f"mm_dy_{tag}")
    d_wout = _matmul(sv["y"], dout, "tn", BF16, f"mm_dwout_{tag}")
    wout_slots = _unperm_rows_out(cfg, d_wout).reshape(NDEV, cfg.DMIX // NDEV, D)
    (dyb, dzb), (dg_ob,), _ = _rowwise_vjp(_f_gate, [(sv["yb"], B, 0), cfg.view(proj, "zb")], [], [p["g_ob"]], [],
                                           [(dy, B, 0)], [BF16, BF16], 256, f"gate_b_bwd_{tag}")
    (dyc, dzc), (dg_oc,), _ = _rowwise_vjp(_f_gate, [(sv["yc"], C, 0), cfg.view(proj, "zc")], [], [p["g_oc"]], [],
                                           [(dy, C, B // C)], [BF16, BF16], 256, f"gate_c_bwd_{tag}")
    a_rows = [cfg.view(proj, "ua"), cfg.view(proj, "va"), cfg.view(proj, "za")]
    a_par = [p["g_v"], p["w_s"], p["b_s"], p["g_oa"]]
    (dua, dva, dza), (dg_v, dw_s, db_s, dg_oa), _ = _rowwise_vjp(
        _f_gmlp, a_rows, [], a_par, [], [(dy, A, (B + C) // A)], [BF16] * 3, LANE, f"gmlp_bwd_{tag}")
    qb, kb, vb = cfg.heads_view(proj, "qb"), cfg.heads_view(proj, "kb"), cfg.heads_view(proj, "vb")
    dqb, dkb, dvb, moved_stick = _stick_bwd(qb, kb, vb, (dyb, 0, LANE), sv["tot"], cfg.Hb, LANE ** -0.5,
                                            f"stick_bwd_{tag}", *cfg.tiles("stick_bwd", l),
                                            carry=_Exchange([[a] for a in ext_stick + [wout_slots]], False))
    got = dict(w_out=moved_stick[-1][0])
    ext_got = [mv[0] for mv in moved_stick[:-1]]
    qc, kc, vc = (sv["q_rot"], 0, 2 * LANE), (sv["k_full"], 0, 2 * LANE), (sv["v_c"], 0, LANE)
    dq_rot, dk_full, dv_c, *moved_mla = _softmax_bwd(qc, kc, vc, (sv["yc"], 0, LANE), (dyc, 0, LANE), sv["lse"], cfg.Hc,
                                                     (LANE + ROPE) ** -0.5, f"mla_bwd_{tag}", *cfg.tiles("mla_bwd", l),
                                                     carry=_Exchange([[a] for a in ext_mla], False) if ext_mla else None)
    ext_got += [mv[0] for mv in (moved_mla[0] if moved_mla else [])]
    trig = [(cos2, LANE, 0), (sin2, LANE, 0)]
    r_rows = [(sv["q_raw"], 2 * C, 0), (sv["kv"], 2 * C, 0), (sv["krr"], LANE, 0)]
    (dq_raw, dkv, dkrr), _, _ = _rowwise_vjp(_f_crope, r_rows, trig, [], [rot],
                                             [(dq_rot, 2 * C, 0), (dk_full, 2 * C, 0), (dv_c, C, 0)], [BF16] * 3, 128,
                                             f"crope_bwd_{tag}")
    dcqn = _matmul(dq_raw, W["uq"], "nn", BF16, f"mm_dcq_{tag}")
    d_wuq = _matmul(dq_raw, sv["cqn"], "tn", BF16, f"mm_dwuq_{tag}")
    dckvn = _matmul(dkv, W["ukv"], "nt", BF16, f"mm_dckv_{tag}")
    d_wukv = _matmul(sv["ckvn"], dkv, "tn", BF16, f"mm_dwukv_{tag}")
    c_rows = [cfg.view(proj, "cq"), cfg.view(proj, "ckv"), cfg.view(proj, "kr")]
    (dcq, dckv, dkr), (dg_q, dg_kv), _ = _rowwise_vjp(
        _f_cpre, c_rows, trig, [p["g_q"], p["g_kv"]], [rot],
        [(dcqn, cfg.Q, 0), (dckvn, cfg.KV, 0), (dkrr, LANE, 0)], [BF16] * 3, 256, f"cpre_bwd_{tag}")
    parts = dict(ua=dua, va=dva, za=dza, qb=dqb, kb=dkb, vb=dvb, zb=dzb, zc=dzc, cq=dcq, kr=dkr, ckv=dckv)
    cols, pos = [], 0
    for nm, off in sorted(cfg.off.items(), key=lambda kv_: kv_[1]):
        if off > pos:
            cols.append(jnp.zeros((S, off - pos), BF16))
        cols.append(parts[nm])
        pos = off + parts[nm].shape[1]
    if cfg.NP > pos:
        cols.append(jnp.zeros((S, cfg.NP - pos), BF16))
    dproj = jnp.concatenate(cols, axis=1)
    to_send = dict(c_w_uq=d_wuq.reshape(cfg.Hc, 2 * LANE, cfg.Q)[:, :LANE + ROPE].reshape(NDEV, -1, cfg.Q),
                   c_w_ukv=_to_slots_cols(d_wukv))
    half = D // 2
    d_win_a = _matmul(dproj, sv["h"], "tn", BF16, f"mm_dwin_a_{tag}", b_cols=(0, half))
    slots_a = _unpad_w_in(cfg, d_win_a).reshape(NDEV, -1, half)
    if last:
        d_win_b, moved_a = _matmul(dproj, sv["h"], "tn", BF16, f"mm_dwin_b_{tag}", b_cols=(half, D - half),
                                   carry=_Exchange([[slots_a]], False))
        dh, moved = _matmul(dproj, W["in"], "nn", BF16, f"mm_dh_{tag}",
                            carry=_Exchange([[_unpad_w_in(cfg, d_win_b).reshape(NDEV, -1, D - half)],
                                             [to_send["c_w_uq"]], [to_send["c_w_ukv"]]], False))
        got.update(w_in=(moved_a[0][0], moved[0][0]), c_w_uq=moved[1][0], c_w_ukv=moved[2][0])
        to_send = {}
    else:
        dh, moved_a = _matmul(dproj, W["in"], "nn", BF16, f"mm_dh_{tag}", carry=_Exchange([[slots_a]], False))
        d_win_b = _matmul(dproj, sv["h"], "tn", BF16, f"mm_dwin_b_{tag}", b_cols=(half, D - half))
        got["w_in"] = (moved_a[0][0],)
        to_send["w_in"] = _unpad_w_in(cfg, d_win_b).reshape(NDEV, -1, D - half)
    (dx,), (dg_pre,), _ = _rowwise_vjp(_f_pre_res, [(sv["x"], D, 0)], [], [p["g_pre"]], [],
                                       [(dh, D, 0), (dout, D, 0)], [F32], 128, f"pre_bwd_{tag}")
    small = dict(g_pre=dg_pre[0], a_g_v=dg_v.reshape(cfg.G, LANE), a_w_s=dw_s, a_b_s=db_s[:, :, 0], c_g_q=dg_q[0],
                 c_g_kv=dg_kv[0], g_out=jnp.concatenate([dg_oa[0], dg_ob[0], dg_oc[0]]))
    return dx, small, got, to_send, ext_got


def _pack_small(vals):
    pieces = []
    for nm in SMALL:
        piece = vals[nm].reshape(-1, LANE)
        pieces.append(jnp.pad(piece, ((0, -piece.shape[0] % 8), (0, 0))))
    packed = jnp.concatenate(pieces, axis=0)
    return jnp.pad(packed, ((0, -packed.shape[0] % SMALL_ROWS), (0, 0)))


def _unpack_small(packed, like):
    out, row = {}, 0
    for nm in SMALL:
        n = like[nm].size // LANE
        out[nm] = packed[row:row + n].reshape(like[nm].shape)
        row += n + (-n % 8)
    return out


def kernel(x, positions, g_pre, w_in, a_g_v, a_w_s, a_b_s, c_g_q, c_g_kv, c_w_uq, c_w_ukv, g_out, w_out, g_final, loss_target, m_g_pre, m_w_in, m_a_g_v, m_a_w_s, m_a_b_s, m_c_g_q, m_c_g_kv, m_c_w_uq, m_c_w_ukv, m_g_out, m_w_out, m_g_final, v_g_pre, v_w_in, v_a_g_v, v_a_w_s, v_a_b_s, v_c_g_q, v_c_g_kv, v_c_w_uq, v_c_w_ukv, v_g_out, v_w_out, v_g_final):
    depth, S, D = w_in.shape[0], x.shape[1], x.shape[2]
    cfg = _Cfg(S, D, a_g_v.shape[1], c_g_q.shape[1], c_g_kv.shape[1], c_w_ukv.shape[2] * NDEV // (2 * LANE), g_out.shape[1])
    weights = dict(g_pre=g_pre, w_in=w_in, a_g_v=a_g_v, a_w_s=a_w_s, a_b_s=a_b_s, c_g_q=c_g_q, c_g_kv=c_g_kv,
                   c_w_uq=c_w_uq, c_w_ukv=c_w_ukv, g_out=g_out, w_out=w_out, g_final=g_final)
    mom_m = dict(g_pre=m_g_pre, w_in=m_w_in, a_g_v=m_a_g_v, a_w_s=m_a_w_s, a_b_s=m_a_b_s, c_g_q=m_c_g_q, c_g_kv=m_c_g_kv,
                 c_w_uq=m_c_w_uq, c_w_ukv=m_c_w_ukv, g_out=m_g_out, w_out=m_w_out, g_final=m_g_final)
    mom_v = dict(g_pre=v_g_pre, w_in=v_w_in, a_g_v=v_a_g_v, a_w_s=v_a_w_s, a_b_s=v_a_b_s, c_g_q=v_c_g_q, c_g_kv=v_c_g_kv,
                 c_w_uq=v_c_w_uq, c_w_ukv=v_c_w_ukv, g_out=v_g_out, w_out=v_w_out, g_final=v_g_final)
    big_names = ("w_in", "c_w_uq", "c_w_ukv", "w_out")

    inv_freq = 1.0 / (ROPE_THETA ** (jnp.arange(0, ROPE, 2, dtype=F32) / ROPE))
    ang = positions[0].astype(F32)[:, None] * inv_freq
    zpad = jnp.zeros((S, LANE - ROPE), F32)
    cos2 = jnp.concatenate([jnp.cos(ang), jnp.cos(ang), zpad], axis=1)
    sin2 = jnp.concatenate([jnp.sin(ang), jnp.sin(ang), zpad], axis=1)
    rot = _rope_matrix()

    for tree in (weights, mom_m, mom_v):
        for nm in TRANSPOSED:
            tree[nm] = jnp.swapaxes(tree[nm], 1, 2)

    def shards(l, names):
        return [[weights[nm][l].astype(BF16)] for nm in names]

    def assemble_rest(g_uq, g_ukv, g_wout):
        uq = jnp.pad(g_uq[0].reshape(cfg.Hc, LANE + ROPE, cfg.Q), ((0, 0), (0, LANE - ROPE), (0, 0)))
        return {"uq": uq.reshape(2 * cfg.C, cfg.Q), "ukv": _from_slots_cols(g_ukv[0]),
                "out": _perm_rows_out(cfg, g_wout[0].reshape(cfg.DMIX, D))}

    params = [_layer_params(cfg, l, g_pre, a_g_v, a_w_s, a_b_s, c_g_q, c_g_kv, g_out) for l in range(depth)]

    in_parts = [g[0] for g in _exchange(_GatherTwoLevel(shards(0, big_names[:1])), "gather_w_in_l0")]
    got_rest = None
    hcur, saved, Ws = x[0], [], []
    for l in range(depth):
        Ws.append({"in": _pad_w_in(cfg, in_parts)})
        if got_rest is not None:
            Ws[l].update(assemble_rest(*got_rest))
        nxt = l + 1 < depth
        early_rows = min(W_IN_EARLY_ROWS, weights["w_in"].shape[1] // 2)
        riding = ([] if got_rest is not None else shards(l, big_names[1:]))
        riding += [[weights["w_in"][l + 1][:early_rows].astype(BF16)]] if nxt else []
        in_parts = []

        def take(moved, l=l, rest_here=got_rest is None, nxt=nxt):
            if rest_here:
                Ws[l].update(assemble_rest(*moved[:3]))
            if nxt:
                in_parts.append(moved[-1][0])

        hcur, sv, got_late, got_rest = _layer_fwd(
            cfg, l, hcur, Ws[l], params[l], cos2, sin2, rot,
            carry_in=(_GatherTwoLevel(riding), take) if riding else None,
            carry_stick=_GatherTwoLevel([[weights["w_in"][l + 1][early_rows:].astype(BF16)]]) if nxt else None,
            carry_mla=_GatherTwoLevel(shards(l + 1, big_names[1:])) if nxt else None)
        in_parts += [g[0] for g in got_late]
        saved.append(sv)
    (dh,), (dg_final,), (loss_rows,) = _rowwise_vjp(
        _f_final, [(hcur, D, 0)], [(loss_target[0], D, 0)], [g_final[None]], [], [(jnp.ones((S, 1), F32), 1, 0)],
        [F32], 128, "final", primal=[(1, F32)])
    loss = lax.psum(jnp.sum(loss_rows), MESH_AXES)

    small_g, slots, pending = [None] * depth, [None] * depth, {}
    for l in reversed(range(depth)):
        ext_stick = [pending["w_in"]] if pending else []
        ext_mla = [pending["c_w_uq"], pending["c_w_ukv"]] if pending else []
        dh, small_g[l], slots[l], pending, ext_got = _layer_bwd(cfg, l, dh, saved[l], Ws[l], params[l], cos2, sin2, rot,
                                                                ext_stick, ext_mla, l == 0)
        if ext_got:
            slots[l + 1].update(w_in=slots[l + 1]["w_in"] + (ext_got[0],), c_w_uq=ext_got[1], c_w_ukv=ext_got[2])
    grad_x = dh[None]
    small_grads = {nm: jnp.stack([small_g[l][nm] for l in range(depth)]) for nm in SMALL if nm != "g_final"}
    small_grads["g_final"] = dg_final[0]
    small_gather = _Exchange([[_pack_small(small_grads)]], True)
    small_slots = None

    res = {}
    for nm in big_names:
        res[nm] = None
        for l in reversed(range(depth)):
            parts = slots[l][nm] if isinstance(slots[l][nm], tuple) else (slots[l][nm],)
            col0 = 0
            for i, part in enumerate(parts):
                out = _adamw(part, weights[nm], mom_m[nm], mom_v[nm], l, res[nm], f"adamw_{nm}_l{l}_{i}", col0,
                             carry=small_gather if small_slots is None else None)
                if small_slots is None:
                    res[nm], (small_slots,) = out
                else:
                    res[nm] = out
                col0 += part.shape[2]
        if nm in TRANSPOSED:
            res[nm] = [jnp.swapaxes(r, 1, 2) for r in res[nm]]
    packed = _adamw(small_slots[0], _pack_small(weights)[None], _pack_small(mom_m)[None], _pack_small(mom_v)[None], 0, None,
                    "adamw_small")
    small_res = [_unpack_small(r[0], weights) for r in packed]
    order = ("g_pre", "w_in", "a_g_v", "a_w_s", "a_b_s", "c_g_q", "c_g_kv", "c_w_uq", "c_w_ukv", "g_out", "w_out", "g_final")
    outs = [loss, grad_x]
    for kind in range(4):
        outs += [small_res[kind][nm] if nm in SMALL else res[nm][kind] for nm in order]
    return tuple(outs)
```

```python
import functools

import numpy as np
import jax
import jax.numpy as jnp
from jax import lax
from jax.experimental import pallas as pl
from jax.experimental.pallas import tpu as pltpu

NDEV = 8
MESH_AXES = ("x", "y", "c")
LANE = 128
ROPE = 64
EPS = 1e-6
ROPE_THETA = 10000.0
ADAM_LR, ADAM_B1, ADAM_B2, ADAM_EPS, ADAM_WD, ADAM_STEP = 0.001, 0.9, 0.999, 1e-08, 0.01, 10
VMEM_LIMIT = 48 * 1024 * 1024
ADAM_TILE_BYTES = 768 * 1024
W_IN_EARLY_ROWS = 512
CARRY_MID_PERCENT = 80
SMALL_ROWS = 256
ATTN_TILES = {"stick_fwd": [(2048, 256)], "stick_bwd": [(2048, 256)], "mla_fwd": [(512, 1024)], "mla_bwd": [(2048, 512)]}
DIAG_BLOCK = 256
F32, BF16 = jnp.float32, jnp.bfloat16
SMALL = ("g_pre", "a_g_v", "a_w_s", "a_b_s", "c_g_q", "c_g_kv", "g_out", "g_final")
TRANSPOSED = ("w_in", "c_w_uq")


def _tile(dim, cap, mult=LANE):
    if dim <= cap:
        return dim
    t = (cap // mult) * mult
    while t >= mult:
        if dim % t == 0:
            return t
        t -= mult
    return dim


def _dot_nt(a, b):
    return lax.dot_general(a, b, (((1,), (1,)), ((), ())), preferred_element_type=F32)


def _dot_tn(a, b):
    return lax.dot_general(a, b, (((0,), (0,)), ((), ())), preferred_element_type=F32)


def _dot(a, b):
    return jnp.dot(a, b, preferred_element_type=F32)


class _Exchange:
    def __init__(self, groups, gather):
        self.groups, self.gather = groups, gather
        self.flat = [(gi, li, a) for gi, grp in enumerate(groups) for li, a in enumerate(grp)]
        self.n = len(self.flat)
        self.args = [a for (_, _, a) in self.flat]
        self.out_shape = [jax.ShapeDtypeStruct((len(grp), NDEV) + tuple(grp[0].shape[-2:]), grp[0].dtype) for grp in groups]
        self.scratch = [pltpu.SemaphoreType.DMA((self.n, NDEV - 1)), pltpu.SemaphoreType.DMA((self.n, NDEV - 1)),
                        pltpu.SemaphoreType.DMA((self.n,))]

    def _copies(self, ins, outs, send_sems, recv_sems, local_sems, landings):
        x, y, c = lax.axis_index("x"), lax.axis_index("y"), lax.axis_index("c")
        me = 4 * x + 2 * y + c
        owns = [pltpu.make_async_copy(ins[i] if self.gather else ins[i].at[me], outs[gi].at[li, me], local_sems.at[i])
                for i, (gi, li, _) in enumerate(self.flat)]
        pairs = []
        for k in range(1, NDEV):
            px = 1 - x if k & 4 else x
            py = 1 - y if k & 2 else y
            pc = 1 - c if k & 1 else c
            peer = 4 * px + 2 * py + pc
            for i, (gi, li, _) in enumerate(self.flat):
                src = ins[i] if self.gather else ins[i].at[peer]
                sems = dict(send_sem=send_sems.at[i, k - 1], recv_sem=recv_sems.at[i, k - 1],
                            device_id=(px, py, pc), device_id_type=pl.DeviceIdType.MESH)
                out = pltpu.make_async_remote_copy(src_ref=src, dst_ref=outs[gi].at[li, me], **sems)
                landing = pltpu.make_async_remote_copy(src_ref=src, dst_ref=outs[gi].at[li, peer], **sems) if landings else None
                pairs.append((out, landing))
        return owns, pairs

    def start(self, ins, outs, sems):
        owns, pairs = self._copies(ins, outs, *sems, landings=False)
        for own in owns:
            own.start()
        for out, _ in pairs:
            out.start()

    def mid(self, ins, outs, sems):
        pass

    def wait(self, ins, outs, sems):
        owns, pairs = self._copies(ins, outs, *sems, landings=True)
        for out, landing in pairs:
            out.wait_send()
            landing.wait_recv()
        for own in owns:
            own.wait()


class _GatherTwoLevel(_Exchange):
    def __init__(self, groups):
        super().__init__(groups, True)

    def _copy(self, i, k, ins, outs, send_sems, recv_sems, landing):
        gi, li, _ = self.flat[i]
        x, y, c = lax.axis_index("x"), lax.axis_index("y"), lax.axis_index("c")
        chips = [(x, y), (1 - x, y), (x, 1 - y), (1 - x, 1 - y)]

        def slot(chip, core):
            return outs[gi].at[li, 4 * chip[0] + 2 * chip[1] + core]

        if k == 0:
            to, src, dst, lands = (x, y, 1 - c), ins[i], slot(chips[0], c), slot(chips[0], 1 - c)
        elif k <= 3:
            to, src, dst, lands = (*chips[k], c), ins[i], slot(chips[0], c), slot(chips[k], c)
        else:
            to, src, dst, lands = (x, y, 1 - c), slot(chips[k - 3], c), slot(chips[k - 3], c), slot(chips[k - 3], 1 - c)
        return pltpu.make_async_remote_copy(src_ref=src, dst_ref=lands if landing else dst, send_sem=send_sems.at[i, k],
                                            recv_sem=recv_sems.at[i, k], device_id=to, device_id_type=pl.DeviceIdType.MESH)

    def _own(self, i, ins, outs, local_sems):
        gi, li, _ = self.flat[i]
        me = 4 * lax.axis_index("x") + 2 * lax.axis_index("y") + lax.axis_index("c")
        return pltpu.make_async_copy(ins[i], outs[gi].at[li, me], local_sems.at[i])

    def start(self, ins, outs, sems):
        send_sems, recv_sems, local_sems = sems
        for i in range(self.n):
            self._own(i, ins, outs, local_sems).start()
        for k in range(4):
            for i in range(self.n):
                self._copy(i, k, ins, outs, send_sems, recv_sems, False).start()

    def mid(self, ins, outs, sems):
        send_sems, recv_sems, _ = sems
        for k in range(1, 4):
            for i in range(self.n):
                self._copy(i, k, ins, outs, send_sems, recv_sems, True).wait_recv()
                self._copy(i, k + 3, ins, outs, send_sems, recv_sems, False).start()

    def wait(self, ins, outs, sems):
        send_sems, recv_sems, local_sems = sems
        for k in (0, 4, 5, 6):
            for i in range(self.n):
                self._copy(i, k, ins, outs, send_sems, recv_sems, True).wait_recv()
        for k in range(NDEV - 1):
            for i in range(self.n):
                self._copy(i, k, ins, outs, send_sems, recv_sems, False).wait_send()
        for i in range(self.n):
            self._own(i, ins, outs, local_sems).wait()


def _call(body, name, grid, in_specs, out_specs, out_shape, scratch, semantics, args, carry=None, aliases=None):
    n_in, n_out, n_scr = len(in_specs), len(out_specs), len(scratch)
    if carry is None:
        run = body
    else:
        semantics = ("arbitrary",) * len(grid)
        anyspec = pl.BlockSpec(memory_space=pl.ANY)
        in_specs = list(in_specs) + [anyspec] * carry.n
        out_specs = list(out_specs) + [anyspec] * len(carry.groups)
        out_shape = list(out_shape) + carry.out_shape
        scratch = list(scratch) + carry.scratch
        args = list(args) + carry.args

        def run(*refs):
            c_in, x_in = refs[:n_in], refs[n_in:n_in + carry.n]
            rest = refs[n_in + carry.n:]
            c_out, x_out = rest[:n_out], rest[n_out:n_out + len(carry.groups)]
            c_scr, sems = rest[n_out + len(carry.groups):len(rest) - 3], rest[len(rest) - 3:]
            step, total = 0, 1
            for d, extent in enumerate(grid):
                step = step * extent + pl.program_id(d)
                total *= extent

            @pl.when(step == 0)
            def _():
                carry.start(x_in, x_out, sems)

            body(*c_in, *c_out, *c_scr)

            @pl.when(step == (total * CARRY_MID_PERCENT) // 100)
            def _():
                carry.mid(x_in, x_out, sems)

            @pl.when(step == total - 1)
            def _():
                carry.wait(x_in, x_out, sems)

    res = pl.pallas_call(
        run, name=name, grid=grid, out_shape=list(out_shape), in_specs=list(in_specs), out_specs=list(out_specs),
        scratch_shapes=list(scratch), input_output_aliases=aliases or {},
        compiler_params=pltpu.CompilerParams(dimension_semantics=semantics, vmem_limit_bytes=VMEM_LIMIT,
                                             has_side_effects=carry is not None),
    )(*args)
    return list(res[:n_out]), list(res[n_out:])


def _exchange(ex, name):
    groups = ex.groups

    def body(*refs):
        ins, outs, sems = refs[:ex.n], refs[ex.n:ex.n + len(groups)], refs[ex.n + len(groups):]
        ex.start(ins, outs, sems)
        ex.mid(ins, outs, sems)
        ex.wait(ins, outs, sems)

    anyspec = pl.BlockSpec(memory_space=pl.ANY)
    return pl.pallas_call(
        body, name=name, out_shape=ex.out_shape, in_specs=[anyspec] * ex.n, out_specs=[anyspec] * len(groups),
        scratch_shapes=ex.scratch, compiler_params=pltpu.CompilerParams(has_side_effects=True),
    )(*ex.args)


def _matmul(a, b, mode, out_dtype, name, add=None, tm=1024, tn=1024, tk=2048, carry=None, b_cols=None):
    if mode == "tn":
        (K, M), (K2, N) = a.shape, b.shape
    elif mode == "nt":
        (M, K), (N, K2) = a.shape, b.shape
    else:
        (M, K), (K2, N) = a.shape, b.shape
    assert K == K2, (a.shape, b.shape, mode)
    col0 = 0
    if b_cols is not None:
        assert mode != "nt"
        col0, N = b_cols
    tm, tn, tk = _tile(M, tm), _tile(N, tn), _tile(K, tk)
    assert col0 % tn == 0
    nk, jb = K // tk, col0 // tn
    a_spec = pl.BlockSpec((tk, tm), lambda i, j, k: (k, i)) if mode == "tn" else pl.BlockSpec((tm, tk), lambda i, j, k: (i, k))
    b_spec = pl.BlockSpec((tn, tk), lambda i, j, k: (j, k)) if mode == "nt" else pl.BlockSpec((tk, tn), lambda i, j, k: (k, j + jb))
    dot = {"nn": _dot, "nt": _dot_nt, "tn": _dot_tn}[mode]
    has_add = add is not None

    def body(*refs):
        a_ref, b_ref = refs[0], refs[1]
        o_ref, acc = refs[-2], refs[-1]
        k = pl.program_id(2)

        @pl.when(k == 0)
        def _():
            acc[...] = jnp.zeros_like(acc)

        acc[...] += dot(a_ref[...].astype(BF16), b_ref[...].astype(BF16))

        @pl.when(k == nk - 1)
        def _():
            r = acc[...]
            if has_add:
                r = r + refs[2][...]
            o_ref[...] = r.astype(o_ref.dtype)

    in_specs = [a_spec, b_spec]
    args = [a, b]
    if has_add:
        in_specs.append(pl.BlockSpec((tm, tn), lambda i, j, k: (i, j)))
        args.append(add)
    (out,), moved = _call(body, name, (M // tm, N // tn, nk), in_specs, [pl.BlockSpec((tm, tn), lambda i, j, k: (i, j))],
                          [jax.ShapeDtypeStruct((M, N), out_dtype)], [pltpu.VMEM((tm, tn), F32)],
                          ("parallel", "parallel", "arbitrary"), args, carry)
    return out if carry is None else (out, moved)


def _row_specs(views, tile):
    return [pl.BlockSpec((tile, w), functools.partial(lambda i, cb: (i, cb), cb=cb)) for (_, w, cb) in views]


def _full_specs(arrs):
    return [pl.BlockSpec(p.shape, functools.partial(lambda i, nd: (0,) * nd, nd=p.ndim)) for p in arrs]


def _rowwise(fn, rows, aux, params, consts, outs, tile, name):
    S = rows[0][0].shape[0]
    nr, na, npar, nc = len(rows), len(aux), len(params), len(consts)

    def body(*refs):
        ins = [r[...].astype(F32) for r in refs[:nr + na]]
        small = [r[...] for r in refs[nr + na:nr + na + npar + nc]]
        res = fn(*ins, *small)
        for o_ref, r in zip(refs[nr + na + npar + nc:], res):
            o_ref[...] = r.astype(o_ref.dtype)

    return pl.pallas_call(
        body, name=name, grid=(S // tile,),
        out_shape=[jax.ShapeDtypeStruct((S, w), dt) for (w, dt) in outs],
        in_specs=_row_specs(rows + aux, tile) + _full_specs(params + consts),
        out_specs=[pl.BlockSpec((tile, w), lambda i: (i, 0)) for (w, _) in outs],
        compiler_params=pltpu.CompilerParams(dimension_semantics=("parallel",), vmem_limit_bytes=VMEM_LIMIT),
    )(*[v[0] for v in rows + aux], *params, *consts)


def _rowwise_vjp(fn, rows, aux, params, consts, cots, grad_dtypes, tile, name, primal=()):
    S = rows[0][0].shape[0]
    nr, na, npar, nc, nct, npr = len(rows), len(aux), len(params), len(consts), len(cots), len(primal)

    def body(*refs):
        n_in = nr + na + npar + nc + nct
        rv = [r[...].astype(F32) for r in refs[:nr]]
        av = [r[...].astype(F32) for r in refs[nr:nr + na]]
        pv = [r[...] for r in refs[nr + na:nr + na + npar]]
        cv = [r[...] for r in refs[nr + na + npar:nr + na + npar + nc]]
        ct = tuple(r[...].astype(F32) for r in refs[nr + na + npar + nc:n_in])
        res, vjp = jax.vjp(lambda *rp: tuple(fn(*rp[:nr], *av, *rp[nr:], *cv)), *rv, *pv)
        grads = vjp(ct)
        g_refs = refs[n_in:n_in + nr]
        p_refs = refs[n_in + nr:n_in + nr + npar]
        o_refs = refs[n_in + nr + npar:]
        for g_ref, g in zip(g_refs, grads[:nr]):
            g_ref[...] = g.astype(g_ref.dtype)

        @pl.when(pl.program_id(0) == 0)
        def _():
            for p_ref in p_refs:
                p_ref[...] = jnp.zeros_like(p_ref)

        for p_ref, g in zip(p_refs, grads[nr:]):
            p_ref[...] += g
        for o_ref, r in zip(o_refs, res[:npr]):
            o_ref[...] = r.astype(o_ref.dtype)

    out_shape = ([jax.ShapeDtypeStruct((S, w), dt) for (_, w, _), dt in zip(rows, grad_dtypes)]
                 + [jax.ShapeDtypeStruct(p.shape, F32) for p in params]
                 + [jax.ShapeDtypeStruct((S, w), dt) for (w, dt) in primal])
    out_specs = ([pl.BlockSpec((tile, w), lambda i: (i, 0)) for (_, w, _) in rows] + _full_specs(params)
                 + [pl.BlockSpec((tile, w), lambda i: (i, 0)) for (w, _) in primal])
    res = pl.pallas_call(
        body, name=name, grid=(S // tile,), out_shape=out_shape,
        in_specs=_row_specs(rows + aux, tile) + _full_specs(params + consts) + _row_specs(cots, tile),
        out_specs=out_specs,
        compiler_params=pltpu.CompilerParams(dimension_semantics=("arbitrary",), vmem_limit_bytes=VMEM_LIMIT),
    )(*[v[0] for v in rows + aux], *params, *consts, *[v[0] for v in cots])
    return res[:nr], res[nr:nr + npar], res[nr + npar:]


@jax.custom_vjp
def _mm(a, b):
    return _dot(a.astype(BF16), b.astype(BF16))


def _mm_fwd(a, b):
    return _mm(a, b), (a, b)


def _mm_bwd(res, ct):
    a, b = res
    ctb = ct.astype(BF16)
    return _dot_nt(ctb, b.astype(BF16)), _dot_tn(a.astype(BF16), ctb)


_mm.defvjp(_mm_fwd, _mm_bwd)


def _rms(x, g):
    return x * lax.rsqrt(jnp.mean(x * x, axis=-1, keepdims=True) + EPS) * g


def _f_pre(x, g):
    return (_rms(x, g),)


def _f_pre_res(x, g):
    return _rms(x, g), x


def _f_gate(y, z, g):
    return (_rms(y, g) * jax.nn.silu(z),)


def _f_gmlp(u, v, z, g_v, w_s, b_s, g_o):
    groups = w_s.shape[0]
    u, v = jax.nn.gelu(u), jax.nn.gelu(v)
    t_idx = lax.broadcasted_iota(jnp.int32, (LANE, LANE), 0)
    s_idx = lax.broadcasted_iota(jnp.int32, (LANE, LANE), 1)
    ys = []
    for g in range(groups):
        sl = slice(g * LANE, (g + 1) * LANE)
        vn = _rms(v[:, sl], g_v[:, sl])
        w = jnp.where(s_idx <= t_idx, w_s[g], 0.0)
        ys.append(u[:, sl] * (_mm(w, vn) + b_s[g]))
    return (_rms(jnp.concatenate(ys, axis=1), g_o) * jax.nn.silu(z),)


def _rope(x, cos2, sin2, rot):
    return x * cos2 + _mm(x, rot) * sin2


def _f_cpre(cq, ckv, kr, cos2, sin2, g_q, g_kv, rot):
    return _rms(cq, g_q), _rms(ckv, g_kv), _rope(kr, cos2, sin2, rot)


def _f_crope(q, kv, krr, cos2, sin2, rot):
    heads = q.shape[1] // (2 * LANE)
    qs, ks, vs = [], [], []
    for h in range(heads):
        lo, mid, hi = 2 * h * LANE, (2 * h + 1) * LANE, (2 * h + 2) * LANE
        qs += [q[:, lo:mid], _rope(q[:, mid:hi], cos2, sin2, rot)]
        ks += [kv[:, lo:mid], krr]
        vs += [kv[:, mid:hi]]
    return jnp.concatenate(qs, axis=1), jnp.concatenate(ks, axis=1), jnp.concatenate(vs, axis=1)


def _f_final(h, target, g):
    err = _rms(h, g) - target
    return (0.5 * jnp.mean(err * err, axis=-1, keepdims=True),)


def _rope_matrix():
    r = np.zeros((LANE, LANE), np.float32)
    half = ROPE // 2
    for i in range(half):
        r[i + half, i] = -1.0
        r[i, i + half] = 1.0
    return jnp.asarray(r)


def _head_spec(view, rows, n_rows_block):
    _, cb0, w = view
    if n_rows_block:
        return pl.BlockSpec((rows, w), functools.partial(lambda h, i, cb0: (i, cb0 + h), cb0=cb0))
    return pl.BlockSpec((rows, w), functools.partial(lambda h, i, cb0: (0, cb0 + h), cb0=cb0))


def _stat_spec(tq):
    return pl.BlockSpec((1, tq, 1), lambda h, i: (h, i, 0))


def _softplus(z):
    return jnp.maximum(z, 0.0) + jnp.log(1.0 + jnp.exp(-jnp.abs(z)))


def _cumsum_mm(x, m01):
    hi = x.astype(BF16)
    lo = (x - hi.astype(F32)).astype(BF16)
    return _dot(hi, m01) + _dot(lo, m01)


def _attn_call(body, name, heads, S, tq, ins, in_blocked, outs, out_blocked, scratch, stats_in=0, stats_out=0, carry=None):
    in_specs = [_head_spec(v, tq if blk else S, blk) for v, blk in zip(ins[:len(ins) - stats_in], in_blocked)]
    in_specs += [_stat_spec(tq)] * stats_in
    out_specs = [_head_spec((None, 0, w), tq if blk else S, blk) for (w, _), blk in zip(outs, out_blocked)]
    out_specs += [_stat_spec(tq)] * stats_out
    out_shape = [jax.ShapeDtypeStruct((S, heads * w), dt) for (w, dt) in outs]
    out_shape += [jax.ShapeDtypeStruct((heads, S, 1), F32)] * stats_out
    args = [v[0] for v in ins[:len(ins) - stats_in]] + list(ins[len(ins) - stats_in:])
    res, moved = _call(body, name, (heads, S // tq), in_specs, out_specs, out_shape, scratch, ("arbitrary", "arbitrary"),
                       args, carry)
    return res if carry is None else res + [moved]


def _softmax_fwd(q, k, v, heads, scale, name, tq, bk, carry=None):
    S, dv = q[0].shape[0], v[2]

    def body(q_ref, k_ref, v_ref, o_ref, lse_ref):
        qi = pl.program_id(1)
        qv = q_ref[...]
        row = qi * tq + lax.broadcasted_iota(jnp.int32, (tq, bk), 0)
        col0 = lax.broadcasted_iota(jnp.int32, (tq, bk), 1)

        def step(kb, carry):
            m, l, acc = carry
            sl = pl.ds(pl.multiple_of(kb * bk, bk), bk)
            s = _dot_nt(qv, k_ref[sl, :]) * scale
            s = jnp.where(kb * bk + col0 <= row, s, -1e30)
            m_new = jnp.maximum(m, jnp.max(s, axis=1, keepdims=True))
            p = jnp.exp(s - m_new)
            alpha = jnp.exp(m - m_new)
            l = alpha * l + jnp.sum(p, axis=1, keepdims=True)
            acc = alpha * acc + _dot(p.astype(BF16), v_ref[sl, :])
            return m_new, l, acc

        n_kb = (qi * tq + tq + bk - 1) // bk
        m, l, acc = lax.fori_loop(0, n_kb, step, (jnp.full((tq, 1), -1e30, F32), jnp.zeros((tq, 1), F32),
                                                  jnp.zeros((tq, dv), F32)))
        o_ref[...] = (acc / l).astype(o_ref.dtype)
        lse_ref[0] = m + jnp.log(l)

    return _attn_call(body, name, heads, S, tq, [q, k, v], [1, 0, 0], [(dv, BF16)], [1], [], stats_out=1, carry=carry)


def _softmax_bwd(q, k, v, o, do, lse, heads, scale, name, tq, bk, carry=None):
    S, dq_w, dv = q[0].shape[0], q[2], v[2]
    nq = S // tq

    bd = min(DIAG_BLOCK, tq)
    assert tq % bk == 0 and tq % bd == 0

    def body(q_ref, k_ref, v_ref, o_ref, do_ref, lse_ref, dq_ref, dk_ref, dv_ref, dk_acc, dv_acc, delta_scr, dq_scr):
        qi = pl.program_id(1)

        @pl.when(qi == 0)
        def _():
            dk_acc[...] = jnp.zeros_like(dk_acc)
            dv_acc[...] = jnp.zeros_like(dv_acc)

        delta_scr[...] = jnp.sum(do_ref[...].astype(F32) * o_ref[...].astype(F32), axis=1, keepdims=True)
        dq_scr[...] = jnp.zeros_like(dq_scr)

        def block(r0, sl, width, masked):
            qv, dov = q_ref[r0:, :], do_ref[r0:, :]
            ks, vs = k_ref[sl, :], v_ref[sl, :]
            p = jnp.exp(_dot_nt(qv, ks) * scale - lse_ref[0, r0:, :])
            if masked:
                shape = (tq - r0, width)
                p = jnp.where(lax.broadcasted_iota(jnp.int32, shape, 1) <= lax.broadcasted_iota(jnp.int32, shape, 0), p, 0.0)
            ds = (p * (_dot_nt(dov, vs) - delta_scr[r0:, :]) * scale).astype(BF16)
            dk_acc[sl, :] += _dot_tn(ds, qv)
            dv_acc[sl, :] += _dot_tn(p.astype(BF16), dov)
            dq_scr[r0:, :] += _dot(ds, ks)

        def step(kb, _):
            block(0, pl.ds(pl.multiple_of(kb * bk, bk), bk), bk, False)
            return 0

        lax.fori_loop(0, qi * (tq // bk), step, 0)
        for j in range(tq // bd):
            block(j * bd, pl.ds(pl.multiple_of(qi * tq + j * bd, bd), bd), bd, True)
        dq_ref[...] = dq_scr[...].astype(dq_ref.dtype)

        @pl.when(qi == nq - 1)
        def _():
            dk_ref[...] = dk_acc[...].astype(dk_ref.dtype)
            dv_ref[...] = dv_acc[...].astype(dv_ref.dtype)

    return _attn_call(body, name, heads, S, tq, [q, k, v, o, do, lse], [1, 0, 0, 1, 1],
                      [(dq_w, BF16), (dq_w, BF16), (dv, BF16)], [1, 0, 0],
                      [pltpu.VMEM((S, dq_w), F32), pltpu.VMEM((S, dv), F32), pltpu.VMEM((tq, 1), F32),
                       pltpu.VMEM((tq, dq_w), F32)], stats_in=1, carry=carry)


def _stick_fwd(q, k, v, heads, scale, name, tq, bk, carry=None):
    S, dv = q[0].shape[0], v[2]

    assert tq % bk == 0
    n_sub = tq // bk

    def body(q_ref, k_ref, v_ref, o_ref, tot_ref, c_scr, acc_scr):
        qi = pl.program_id(1)
        m_gt = (lax.broadcasted_iota(jnp.int32, (bk, bk), 0) > lax.broadcasted_iota(jnp.int32, (bk, bk), 1)).astype(BF16)

        def block(r0, sl, masked):
            rows = tq - r0
            z = _dot_nt(q_ref[r0:, :], k_ref[sl, :]) * scale
            sp = _softplus(z)
            lk = -sp
            if masked:
                mask = lax.broadcasted_iota(jnp.int32, (rows, bk), 1) < lax.broadcasted_iota(jnp.int32, (rows, bk), 0)
                lk = jnp.where(mask, lk, 0.0)
            after = _cumsum_mm(lk, m_gt) + c_scr[r0:, :]
            a = jnp.exp(z - sp + after)
            if masked:
                a = jnp.where(mask, a, 0.0)
            acc_scr[r0:, :] += _dot(a.astype(BF16), v_ref[sl, :])
            c_scr[r0:, :] += jnp.sum(lk, axis=1, keepdims=True)

        c_scr[...] = jnp.zeros_like(c_scr)
        acc_scr[...] = jnp.zeros_like(acc_scr)
        for j in reversed(range(n_sub)):
            block(j * bk, pl.ds(pl.multiple_of(qi * tq + j * bk, bk), bk), True)

        def step(it, _):
            block(0, pl.ds(pl.multiple_of((qi * n_sub - 1 - it) * bk, bk), bk), False)
            return 0

        lax.fori_loop(0, qi * n_sub, step, 0)
        o_ref[...] = acc_scr[...].astype(o_ref.dtype)
        tot_ref[0] = c_scr[...]

    return _attn_call(body, name, heads, S, tq, [q, k, v], [1, 0, 0], [(dv, BF16)], [1],
                      [pltpu.VMEM((tq, 1), F32), pltpu.VMEM((tq, dv), F32)], stats_out=1, carry=carry)


def _stick_bwd(q, k, v, do, tot, heads, scale, name, tq, bk, carry=None):
    S, dq_w, dv = q[0].shape[0], q[2], v[2]
    nq = S // tq

    assert tq % bk == 0
    n_sub = tq // bk

    def body(q_ref, k_ref, v_ref, do_ref, tot_ref, dq_ref, dk_ref, dv_ref, dk_acc, dv_acc, pc_scr, gc_scr, dq_scr):
        qi = pl.program_id(1)

        @pl.when(qi == 0)
        def _():
            dk_acc[...] = jnp.zeros_like(dk_acc)
            dv_acc[...] = jnp.zeros_like(dv_acc)

        j_idx = lax.broadcasted_iota(jnp.int32, (bk, bk), 0)
        s_idx = lax.broadcasted_iota(jnp.int32, (bk, bk), 1)
        m_le, m_lt = (j_idx <= s_idx).astype(BF16), (j_idx < s_idx).astype(BF16)

        def block(r0, sl, masked):
            rows = tq - r0
            qv, dov = q_ref[r0:, :], do_ref[r0:, :]
            ks, vs = k_ref[sl, :], v_ref[sl, :]
            z = _dot_nt(qv, ks) * scale
            sp = _softplus(z)
            lk = -sp
            if masked:
                mask = lax.broadcasted_iota(jnp.int32, (rows, bk), 1) < lax.broadcasted_iota(jnp.int32, (rows, bk), 0)
                lk = jnp.where(mask, lk, 0.0)
            after = tot_ref[0, r0:, :] - pc_scr[r0:, :] - _cumsum_mm(lk, m_le)
            log_beta = z - sp
            a = jnp.exp(log_beta + after)
            if masked:
                a = jnp.where(mask, a, 0.0)
            g = _dot_nt(dov, vs) * a
            cg = gc_scr[r0:, :] + _cumsum_mm(g, m_lt)
            dz = g * jnp.exp(-sp) - jnp.exp(log_beta) * cg
            if masked:
                dz = jnp.where(mask, dz, 0.0)
            dz = (dz * scale).astype(BF16)
            dk_acc[sl, :] += _dot_tn(dz, qv)
            dv_acc[sl, :] += _dot_tn(a.astype(BF16), dov)
            dq_scr[r0:, :] += _dot(dz, ks)
            pc_scr[r0:, :] += jnp.sum(lk, axis=1, keepdims=True)
            gc_scr[r0:, :] += jnp.sum(g, axis=1, keepdims=True)

        pc_scr[...] = jnp.zeros_like(pc_scr)
        gc_scr[...] = jnp.zeros_like(gc_scr)
        dq_scr[...] = jnp.zeros_like(dq_scr)

        def step(kb, _):
            block(0, pl.ds(pl.multiple_of(kb * bk, bk), bk), False)
            return 0

        lax.fori_loop(0, qi * n_sub, step, 0)
        for j in range(n_sub):
            block(j * bk, pl.ds(pl.multiple_of(qi * tq + j * bk, bk), bk), True)
        dq_ref[...] = dq_scr[...].astype(dq_ref.dtype)

        @pl.when(qi == nq - 1)
        def _():
            dk_ref[...] = dk_acc[...].astype(dk_ref.dtype)
            dv_ref[...] = dv_acc[...].astype(dv_ref.dtype)

    return _attn_call(body, name, heads, S, tq, [q, k, v, do, tot], [1, 0, 0, 1],
                      [(dq_w, BF16), (dq_w, BF16), (dv, BF16)], [1, 0, 0],
                      [pltpu.VMEM((S, dq_w), F32), pltpu.VMEM((S, dv), F32), pltpu.VMEM((tq, 1), F32),
                       pltpu.VMEM((tq, 1), F32), pltpu.VMEM((tq, dq_w), F32)], stats_in=1, carry=carry)


def _adamw(slots, w, m, v, layer, prev, name, col0=0, carry=None):
    _, R, C = slots.shape
    L, full_c = w.shape[0], w.shape[2]
    item = slots.dtype.itemsize
    tc = _tile(C, 2048)
    tr = _tile(R, max(16, ADAM_TILE_BYTES // (item * tc)), mult=16)
    if tr == R and R * tc * item > ADAM_TILE_BYTES:
        tc = _tile(C, max(LANE, ADAM_TILE_BYTES // (item * R)))
    c1, c2 = 1.0 - ADAM_B1 ** ADAM_STEP, 1.0 - ADAM_B2 ** ADAM_STEP
    n_prev = 0 if prev is None else 4

    def body(s_ref, w_ref, m_ref, v_ref, *rest):
        g_out, d_out, m_out, v_out = rest[n_prev:]
        g = s_ref[0].astype(F32)
        for k in range(1, NDEV):
            g = g + s_ref[k].astype(F32)
        m_new = ADAM_B1 * m_ref[0] + (1.0 - ADAM_B1) * g
        v_new = ADAM_B2 * v_ref[0] + (1.0 - ADAM_B2) * (g * g)
        g_out[0] = g
        m_out[0] = m_new
        v_out[0] = v_new
        d_out[0] = -ADAM_LR * ((m_new / c1) / (jnp.sqrt(v_new / c2) + ADAM_EPS) + ADAM_WD * w_ref[0])

    assert col0 % tc == 0
    spec = pl.BlockSpec((1, tr, tc), lambda i, j: (layer, i, j + col0 // tc))
    in_specs = [pl.BlockSpec((NDEV, tr, tc), lambda i, j: (0, i, j)), spec, spec, spec]
    in_specs += [pl.BlockSpec(memory_space=pl.ANY)] * n_prev
    res, moved = _call(body, name, (R // tr, C // tc), in_specs, [spec] * 4, [jax.ShapeDtypeStruct((L, R, full_c), F32)] * 4,
                       [], ("parallel", "parallel"), [slots, w, m, v, *(prev or [])], carry,
                       aliases={4 + i: i for i in range(n_prev)})
    return res if carry is None else (res, moved)


class _Cfg:
    def __init__(self, S, D, groups, q_lora, kv_lora, c_heads, d_mix):
        self.S, self.D, self.G, self.Q, self.KV, self.Hc, self.DMIX = S, D, groups, q_lora, kv_lora, c_heads, d_mix
        self.A, self.C = groups * LANE, c_heads * LANE
        self.B = d_mix - self.A - self.C
        self.Hb = self.B // LANE
        A, B, C = self.A, self.B, self.C
        assert B % LANE == 0 and B % C == 0 and (B + C) % A == 0
        self.ref_segs = [("ua", A), ("va", A), ("za", A), ("qb", B), ("kb", B), ("vb", B), ("zb", B),
                         ("cq", q_lora), ("ckv", kv_lora), ("kr", ROPE), ("zc", C)]
        self.off, off = {}, 0
        for nm, w in [("ua", A), ("va", A), ("za", A), ("qb", B), ("kb", B), ("vb", B), ("zb", B), ("zc", C),
                      ("cq", q_lora), ("kr", LANE), ("ckv", kv_lora)]:
            off = -(-off // w) * w
            self.off[nm] = off
            off += w
        self.NP = -(-off // 512) * 512
        self.width = {"kr": LANE, **{nm: w for nm, w in self.ref_segs if nm != "kr"}}

    def tiles(self, kind, layer):
        tq, bk = ATTN_TILES[kind][layer % len(ATTN_TILES[kind])]
        return min(tq, self.S), min(bk, self.S)

    def view(self, arr, nm):
        w = self.width[nm]
        return (arr, w, self.off[nm] // w)

    def heads_view(self, arr, nm):
        return (arr, self.off[nm] // LANE, LANE)


def _gathered_rows(parts, a, b):
    per = sum(p.shape[1] for p in parts)
    out = []
    while a < b:
        k, r = divmod(a, per)
        i = 0
        while r >= parts[i].shape[1]:
            r -= parts[i].shape[1]
            i += 1
        n = min(b - a, parts[i].shape[1] - r)
        out.append(parts[i][k, r:r + n])
        a += n
    return out


def _pad_w_in(cfg, parts):
    width_d, dtype = parts[0].shape[2], parts[0].dtype
    start_of, start = {}, 0
    for nm, width in cfg.ref_segs:
        start_of[nm] = (start, width)
        start += width
    rows, pos = [], 0
    for nm, off in sorted(cfg.off.items(), key=lambda kv: kv[1]):
        if off > pos:
            rows.append(jnp.zeros((off - pos, width_d), dtype))
        rows += _gathered_rows(parts, start_of[nm][0], start_of[nm][0] + start_of[nm][1])
        pos = off + start_of[nm][1]
    if cfg.NP > pos:
        rows.append(jnp.zeros((cfg.NP - pos, width_d), dtype))
    return jnp.concatenate(rows, axis=0)


def _unpad_w_in(cfg, wpt):
    return jnp.concatenate([wpt[cfg.off[nm]:cfg.off[nm] + width] for nm, width in cfg.ref_segs], axis=0)


def _to_slots_cols(w):
    R = w.shape[0]
    return w.reshape(R, NDEV, -1).transpose(1, 0, 2)


def _from_slots_cols(s):
    return s.transpose(1, 0, 2).reshape(s.shape[1], -1)


def _perm_rows_out(cfg, w):
    return jnp.concatenate([w[cfg.A:], w[:cfg.A]], axis=0)


def _unperm_rows_out(cfg, w):
    return jnp.concatenate([w[cfg.B + cfg.C:], w[:cfg.B + cfg.C]], axis=0)


def _layer_params(cfg, l, g_pre, a_g_v, a_w_s, a_b_s, c_g_q, c_g_kv, g_out):
    A, B = cfg.A, cfg.B
    return dict(g_pre=g_pre[l][None], g_v=a_g_v[l].reshape(1, A), w_s=a_w_s[l], b_s=a_b_s[l][:, :, None],
                g_q=c_g_q[l][None], g_kv=c_g_kv[l][None],
                g_oa=g_out[l][None, :A], g_ob=g_out[l][None, A:A + B], g_oc=g_out[l][None, A + B:])


def _layer_fwd(cfg, l, x, W, p, cos2, sin2, rot, carry_in=None, carry_stick=None, carry_mla=None):
    S, D, A, B, C = cfg.S, cfg.D, cfg.A, cfg.B, cfg.C
    tag = f"l{l}"
    (h,) = _rowwise(_f_pre, [(x, D, 0)], [], [p["g_pre"]], [], [(D, BF16)], 256, f"pre_{tag}")
    if carry_in is None:
        proj = _matmul(h, W["in"], "nt", BF16, f"mm_in_{tag}")
    else:
        proj, moved_in = _matmul(h, W["in"], "nt", BF16, f"mm_in_{tag}", carry=carry_in[0])
        carry_in[1](moved_in)
    a_rows = [cfg.view(proj, "ua"), cfg.view(proj, "va"), cfg.view(proj, "za")]
    a_par = [p["g_v"], p["w_s"], p["b_s"], p["g_oa"]]
    (ya,) = _rowwise(_f_gmlp, a_rows, [], a_par, [], [(A, BF16)], LANE, f"gmlp_{tag}")
    qb, kb, vb = cfg.heads_view(proj, "qb"), cfg.heads_view(proj, "kb"), cfg.heads_view(proj, "vb")
    yb, tot, *moved_stick = _stick_fwd(qb, kb, vb, cfg.Hb, LANE ** -0.5, f"stick_fwd_{tag}", *cfg.tiles("stick_fwd", l),
                                       carry=carry_stick)
    (ybg,) = _rowwise(_f_gate, [(yb, B, 0), cfg.view(proj, "zb")], [], [p["g_ob"]], [], [(B, BF16)], 256, f"gate_b_{tag}")
    c_rows = [cfg.view(proj, "cq"), cfg.view(proj, "ckv"), cfg.view(proj, "kr")]
    trig = [(cos2, LANE, 0), (sin2, LANE, 0)]
    cqn, ckvn, krr = _rowwise(_f_cpre, c_rows, trig, [p["g_q"], p["g_kv"]], [rot],
                              [(cfg.Q, BF16), (cfg.KV, BF16), (LANE, BF16)], 256, f"cpre_{tag}")
    q_raw = _matmul(cqn, W["uq"], "nt", BF16, f"mm_uq_{tag}")
    kv = _matmul(ckvn, W["ukv"], "nn", BF16, f"mm_ukv_{tag}")
    r_rows = [(q_raw, 2 * C, 0), (kv, 2 * C, 0), (krr, LANE, 0)]
    q_rot, k_full, v_c = _rowwise(_f_crope, r_rows, trig, [], [rot], [(2 * C, BF16), (2 * C, BF16), (C, BF16)], 128,
                                  f"crope_{tag}")
    qc, kc, vc = (q_rot, 0, 2 * LANE), (k_full, 0, 2 * LANE), (v_c, 0, LANE)
    yc, lse, *moved_mla = _softmax_fwd(qc, kc, vc, cfg.Hc, (LANE + ROPE) ** -0.5, f"mla_fwd_{tag}", *cfg.tiles("mla_fwd", l),
                                       carry=carry_mla)
    (ycg,) = _rowwise(_f_gate, [(yc, C, 0), cfg.view(proj, "zc")], [], [p["g_oc"]], [], [(C, BF16)], 256, f"gate_c_{tag}")
    y = jnp.concatenate([ybg, ycg, ya], axis=1)
    out = _matmul(y, W["out"], "nn", F32, f"mm_out_{tag}", add=x)
    saved = dict(x=x, h=h, proj=proj, yb=yb, tot=tot, cqn=cqn, ckvn=ckvn, krr=krr, q_raw=q_raw, kv=kv,
                 q_rot=q_rot, k_full=k_full, v_c=v_c, yc=yc, lse=lse, y=y)
    return out, saved, (moved_stick[0] if moved_stick else []), (moved_mla[0] if moved_mla else [])


def _layer_bwd(cfg, l, dout, sv, W, p, cos2, sin2, rot, ext_stick, ext_mla, last):
    S, D, A, B, C = cfg.S, cfg.D, cfg.A, cfg.B, cfg.C
    tag = f"l{l}"
    proj = sv["proj"]
    dy = _matmul(dout, W["out"], "nt", BF16, f"mm_dy_{tag}")
    d_wout = _matmul(sv["y"], dout, "tn", BF16, f"mm_dwout_{tag}")
    wout_slots = _unperm_rows_out(cfg, d_wout).reshape(NDEV, cfg.DMIX // NDEV, D)
    (dyb, dzb), (dg_ob,), _ = _rowwise_vjp(_f_gate, [(sv["yb"], B, 0), cfg.view(proj, "zb")], [], [p["g_ob"]], [],
                                           [(dy, B, 0)], [BF16, BF16], 256, f"gate_b_bwd_{tag}")
    (dyc, dzc), (dg_oc,), _ = _rowwise_vjp(_f_gate, [(sv["yc"], C, 0), cfg.view(proj, "zc")], [], [p["g_oc"]], [],
                                           [(dy, C, B // C)], [BF16, BF16], 256, f"gate_c_bwd_{tag}")
    a_rows = [cfg.view(proj, "ua"), cfg.view(proj, "va"), cfg.view(proj, "za")]
    a_par = [p["g_v"], p["w_s"], p["b_s"], p["g_oa"]]
    (dua, dva, dza), (dg_v, dw_s, db_s, dg_oa), _ = _rowwise_vjp(
        _f_gmlp, a_rows, [], a_par, [], [(dy, A, (B + C) // A)], [BF16] * 3, LANE, f"gmlp_bwd_{tag}")
    qb, kb, vb = cfg.heads_view(proj, "qb"), cfg.heads_view(proj, "kb"), cfg.heads_view(proj, "vb")
    dqb, dkb, dvb, moved_stick = _stick_bwd(qb, kb, vb, (dyb, 0, LANE), sv["tot"], cfg.Hb, LANE ** -0.5,
                                            f"stick_bwd_{tag}", *cfg.tiles("stick_bwd", l),
                                            carry=_Exchange([[a] for a in (ext_stick or [wout_slots])], False))
    ext_got = [mv[0] for mv in moved_stick] if ext_stick else []
    ext_mla = ext_mla + ([wout_slots] if ext_stick else [])
    qc, kc, vc = (sv["q_rot"], 0, 2 * LANE), (sv["k_full"], 0, 2 * LANE), (sv["v_c"], 0, LANE)
    dq_rot, dk_full, dv_c, *moved_mla = _softmax_bwd(qc, kc, vc, (sv["yc"], 0, LANE), (dyc, 0, LANE), sv["lse"], cfg.Hc,
                                                     (LANE + ROPE) ** -0.5, f"mla_bwd_{tag}", *cfg.tiles("mla_bwd", l),
                                                     carry=_Exchange([[a] for a in ext_mla], False) if ext_mla else None)
    moved_mla = [mv[0] for mv in (moved_mla[0] if moved_mla else [])]
    got = dict(w_out=moved_mla.pop() if ext_stick else moved_stick[0][0])
    ext_got += moved_mla
    trig = [(cos2, LANE, 0), (sin2, LANE, 0)]
    r_rows = [(sv["q_raw"], 2 * C, 0), (sv["kv"], 2 * C, 0), (sv["krr"], LANE, 0)]
    (dq_raw, dkv, dkrr), _, _ = _rowwise_vjp(_f_crope, r_rows, trig, [], [rot],
                                             [(dq_rot, 2 * C, 0), (dk_full, 2 * C, 0), (dv_c, C, 0)], [BF16] * 3, 128,
                                             f"crope_bwd_{tag}")
    dcqn = _matmul(dq_raw, W["uq"], "nn", BF16, f"mm_dcq_{tag}")
    d_wuq = _matmul(dq_raw, sv["cqn"], "tn", BF16, f"mm_dwuq_{tag}")
    dckvn = _matmul(dkv, W["ukv"], "nt", BF16, f"mm_dckv_{tag}")
    d_wukv = _matmul(sv["ckvn"], dkv, "tn", BF16, f"mm_dwukv_{tag}")
    c_rows = [cfg.view(proj, "cq"), cfg.view(proj, "ckv"), cfg.view(proj, "kr")]
    (dcq, dckv, dkr), (dg_q, dg_kv), _ = _rowwise_vjp(
        _f_cpre, c_rows, trig, [p["g_q"], p["g_kv"]], [rot],
        [(dcqn, cfg.Q, 0), (dckvn, cfg.KV, 0), (dkrr, LANE, 0)], [BF16] * 3, 256, f"cpre_bwd_{tag}")
    parts = dict(ua=dua, va=dva, za=dza, qb=dqb, kb=dkb, vb=dvb, zb=dzb, zc=dzc, cq=dcq, kr=dkr, ckv=dckv)
    cols, pos = [], 0
    for nm, off in sorted(cfg.off.items(), key=lambda kv_: kv_[1]):
        if off > pos:
            cols.append(jnp.zeros((S, off - pos), BF16))
        cols.append(parts[nm])
        pos = off + parts[nm].shape[1]
    if cfg.NP > pos:
        cols.append(jnp.zeros((S, cfg.NP - pos), BF16))
    dproj = jnp.concatenate(cols, axis=1)
    to_send = dict(c_w_uq=d_wuq.reshape(cfg.Hc, 2 * LANE, cfg.Q)[:, :LANE + ROPE].reshape(NDEV, -1, cfg.Q),
                   c_w_ukv=_to_slots_cols(d_wukv))
    half = D // 2 if last or (D // 4) % LANE else D // 4
    d_win_a = _matmul(dproj, sv["h"], "tn", BF16, f"mm_dwin_a_{tag}", b_cols=(0, half))
    slots_a = _unpad_w_in(cfg, d_win_a).reshape(NDEV, -1, half)
    if last:
        d_win_b, moved_a = _matmul(dproj, sv["h"], "tn", BF16, f"mm_dwin_b_{tag}", b_cols=(half, D - half),
                                   carry=_Exchange([[slots_a]], False))
        dh, moved = _matmul(dproj, W["in"], "nn", BF16, f"mm_dh_{tag}",
                            carry=_Exchange([[_unpad_w_in(cfg, d_win_b).reshape(NDEV, -1, D - half)],
                                             [to_send["c_w_uq"]], [to_send["c_w_ukv"]]], False))
        got.update(w_in=(moved_a[0][0], moved[0][0]), c_w_uq=moved[1][0], c_w_ukv=moved[2][0])
        to_send = {}
    else:
        dh, moved_a = _matmul(dproj, W["in"], "nn", BF16, f"mm_dh_{tag}", carry=_Exchange([[slots_a]], False))
        d_win_b = _matmul(dproj, sv["h"], "tn", BF16, f"mm_dwin_b_{tag}", b_cols=(half, D - half))
        got["w_in"] = (moved_a[0][0],)
        to_send["w_in"] = _unpad_w_in(cfg, d_win_b).reshape(NDEV, -1, D - half)
    (dx,), (dg_pre,), _ = _rowwise_vjp(_f_pre_res, [(sv["x"], D, 0)], [], [p["g_pre"]], [],
                                       [(dh, D, 0), (dout, D, 0)], [F32], 128, f"pre_bwd_{tag}")
    small = dict(g_pre=dg_pre[0], a_g_v=dg_v.reshape(cfg.G, LANE), a_w_s=dw_s, a_b_s=db_s[:, :, 0], c_g_q=dg_q[0],
                 c_g_kv=dg_kv[0], g_out=jnp.concatenate([dg_oa[0], dg_ob[0], dg_oc[0]]))
    return dx, small, got, to_send, ext_got


def _pack_small(vals):
    pieces = []
    for nm in SMALL:
        piece = vals[nm].reshape(-1, LANE)
        pieces.append(jnp.pad(piece, ((0, -piece.shape[0] % 8), (0, 0))))
    packed = jnp.concatenate(pieces, axis=0)
    return jnp.pad(packed, ((0, -packed.shape[0] % SMALL_ROWS), (0, 0)))


def _unpack_small(packed, like):
    out, row = {}, 0
    for nm in SMALL:
        n = like[nm].size // LANE
        out[nm] = packed[row:row + n].reshape(like[nm].shape)
        row += n + (-n % 8)
    return out


def kernel(x, positions, g_pre, w_in, a_g_v, a_w_s, a_b_s, c_g_q, c_g_kv, c_w_uq, c_w_ukv, g_out, w_out, g_final, loss_target, m_g_pre, m_w_in, m_a_g_v, m_a_w_s, m_a_b_s, m_c_g_q, m_c_g_kv, m_c_w_uq, m_c_w_ukv, m_g_out, m_w_out, m_g_final, v_g_pre, v_w_in, v_a_g_v, v_a_w_s, v_a_b_s, v_c_g_q, v_c_g_kv, v_c_w_uq, v_c_w_ukv, v_g_out, v_w_out, v_g_final):
    depth, S, D = w_in.shape[0], x.shape[1], x.shape[2]
    cfg = _Cfg(S, D, a_g_v.shape[1], c_g_q.shape[1], c_g_kv.shape[1], c_w_ukv.shape[2] * NDEV // (2 * LANE), g_out.shape[1])
    weights = dict(g_pre=g_pre, w_in=w_in, a_g_v=a_g_v, a_w_s=a_w_s, a_b_s=a_b_s, c_g_q=c_g_q, c_g_kv=c_g_kv,
                   c_w_uq=c_w_uq, c_w_ukv=c_w_ukv, g_out=g_out, w_out=w_out, g_final=g_final)
    mom_m = dict(g_pre=m_g_pre, w_in=m_w_in, a_g_v=m_a_g_v, a_w_s=m_a_w_s, a_b_s=m_a_b_s, c_g_q=m_c_g_q, c_g_kv=m_c_g_kv,
                 c_w_uq=m_c_w_uq, c_w_ukv=m_c_w_ukv, g_out=m_g_out, w_out=m_w_out, g_final=m_g_final)
    mom_v = dict(g_pre=v_g_pre, w_in=v_w_in, a_g_v=v_a_g_v, a_w_s=v_a_w_s, a_b_s=v_a_b_s, c_g_q=v_c_g_q, c_g_kv=v_c_g_kv,
                 c_w_uq=v_c_w_uq, c_w_ukv=v_c_w_ukv, g_out=v_g_out, w_out=v_w_out, g_final=v_g_final)
    big_names = ("w_in", "c_w_uq", "c_w_ukv", "w_out")

    inv_freq = 1.0 / (ROPE_THETA ** (jnp.arange(0, ROPE, 2, dtype=F32) / ROPE))
    ang = positions[0].astype(F32)[:, None] * inv_freq
    zpad = jnp.zeros((S, LANE - ROPE), F32)
    cos2 = jnp.concatenate([jnp.cos(ang), jnp.cos(ang), zpad], axis=1)
    sin2 = jnp.concatenate([jnp.sin(ang), jnp.sin(ang), zpad], axis=1)
    rot = _rope_matrix()

    for tree in (weights, mom_m, mom_v):
        for nm in TRANSPOSED:
            tree[nm] = jnp.swapaxes(tree[nm], 1, 2)

    def shards(l, names):
        return [[weights[nm][l].astype(BF16)] for nm in names]

    def assemble_rest(g_uq, g_ukv, g_wout):
        uq = jnp.pad(g_uq[0].reshape(cfg.Hc, LANE + ROPE, cfg.Q), ((0, 0), (0, LANE - ROPE), (0, 0)))
        return {"uq": uq.reshape(2 * cfg.C, cfg.Q), "ukv": _from_slots_cols(g_ukv[0]),
                "out": _perm_rows_out(cfg, g_wout[0].reshape(cfg.DMIX, D))}

    params = [_layer_params(cfg, l, g_pre, a_g_v, a_w_s, a_b_s, c_g_q, c_g_kv, g_out) for l in range(depth)]

    in_parts = [g[0] for g in _exchange(_GatherTwoLevel(shards(0, big_names[:1])), "gather_w_in_l0")]
    got_rest = None
    hcur, saved, Ws = x[0], [], []
    for l in range(depth):
        Ws.append({"in": _pad_w_in(cfg, in_parts)})
        if got_rest is not None:
            Ws[l].update(assemble_rest(*got_rest))
        nxt = l + 1 < depth
        early_rows = min(W_IN_EARLY_ROWS, weights["w_in"].shape[1] // 2)
        riding = ([] if got_rest is not None else shards(l, big_names[1:]))
        riding += [[weights["w_in"][l + 1][:early_rows].astype(BF16)]] if nxt else []
        in_parts = []

        def take(moved, l=l, rest_here=got_rest is None, nxt=nxt):
            if rest_here:
                Ws[l].update(assemble_rest(*moved[:3]))
            if nxt:
                in_parts.append(moved[-1][0])

        hcur, sv, got_late, got_rest = _layer_fwd(
            cfg, l, hcur, Ws[l], params[l], cos2, sin2, rot,
            carry_in=(_GatherTwoLevel(riding), take) if riding else None,
            carry_stick=_GatherTwoLevel([[weights["w_in"][l + 1][early_rows:].astype(BF16)]]) if nxt else None,
            carry_mla=_GatherTwoLevel(shards(l + 1, big_names[1:])) if nxt else None)
        in_parts += [g[0] for g in got_late]
        saved.append(sv)
    (dh,), (dg_final,), (loss_rows,) = _rowwise_vjp(
        _f_final, [(hcur, D, 0)], [(loss_target[0], D, 0)], [g_final[None]], [], [(jnp.ones((S, 1), F32), 1, 0)],
        [F32], 128, "final", primal=[(1, F32)])
    loss = lax.psum(jnp.sum(loss_rows), MESH_AXES)

    small_g, slots, pending = [None] * depth, [None] * depth, {}
    for l in reversed(range(depth)):
        ext_stick = [pending["w_in"]] if pending else []
        ext_mla = [pending["c_w_uq"], pending["c_w_ukv"]] if pending else []
        dh, small_g[l], slots[l], pending, ext_got = _layer_bwd(cfg, l, dh, saved[l], Ws[l], params[l], cos2, sin2, rot,
                                                                ext_stick, ext_mla, l == 0)
        if ext_got:
            slots[l + 1].update(w_in=slots[l + 1]["w_in"] + (ext_got[0],), c_w_uq=ext_got[1], c_w_ukv=ext_got[2])
    grad_x = dh[None]
    small_grads = {nm: jnp.stack([small_g[l][nm] for l in range(depth)]) for nm in SMALL if nm != "g_final"}
    small_grads["g_final"] = dg_final[0]
    (small_slots,) = _exchange(_Exchange([[_pack_small(small_grads)]], True), "gather_small_grads")

    res = {}
    for nm in big_names:
        res[nm] = None
        for l in reversed(range(depth)):
            parts = slots[l][nm] if isinstance(slots[l][nm], tuple) else (slots[l][nm],)
            col0 = 0
            for i, part in enumerate(parts):
                res[nm] = _adamw(part, weights[nm], mom_m[nm], mom_v[nm], l, res[nm], f"adamw_{nm}_l{l}_{i}", col0)
                col0 += part.shape[2]
        if nm in TRANSPOSED:
            res[nm] = [jnp.swapaxes(r, 1, 2) for r in res[nm]]
    packed = _adamw(small_slots[0], _pack_small(weights)[None], _pack_small(mom_m)[None], _pack_small(mom_v)[None], 0, None,
                    "adamw_small")
    small_res = [_unpack_small(r[0], weights) for r in packed]
    order = ("g_pre", "w_in", "a_g_v", "a_w_s", "a_b_s", "c_g_q", "c_g_kv", "c_w_uq", "c_w_ukv", "g_out", "w_out", "g_final")
    outs = [loss, grad_x]
    for kind in range(4):
        outs += [small_res[kind][nm] if nm in SMALL else res[nm][kind] for nm in order]
    return tuple(outs)
```

```python
import functools
import math

import numpy as np
import jax
import jax.numpy as jnp
from jax import lax
from jax.experimental import pallas as pl
from jax.experimental.pallas import tpu as pltpu

NDEV = 8
MESH_AXES = ("x", "y", "c")
LANE = 128
ROPE = 64
EPS = 1e-6
ROPE_THETA = 10000.0
ADAM_LR, ADAM_B1, ADAM_B2, ADAM_EPS, ADAM_WD, ADAM_STEP = 0.001, 0.9, 0.999, 1e-08, 0.01, 10
VMEM_LIMIT = 48 * 1024 * 1024
ADAM_TILE_BYTES = 768 * 1024
LAST_W_IN_RANGES = (2, 3, 3)
W_IN_EARLY_ROWS = 384
CARRY_MID_PERCENT = 80
SMALL_ROWS = 256
ATTN_TILES = {"stick_fwd": [(2048, 256)], "stick_bwd": [(2048, 256)], "mla_fwd": [(512, 1024)], "mla_bwd": [(2048, 512)]}
DIAG_BLOCK = 256
F32, BF16 = jnp.float32, jnp.bfloat16
SMALL = ("g_pre", "a_g_v", "a_w_s", "a_b_s", "c_g_q", "c_g_kv", "g_out", "g_final")
TRANSPOSED = ("w_in", "c_w_uq")


def _tile(dim, cap, mult=LANE):
    if dim <= cap:
        return dim
    t = (cap // mult) * mult
    while t >= mult:
        if dim % t == 0:
            return t
        t -= mult
    return dim


def _dot_nt(a, b):
    return lax.dot_general(a, b, (((1,), (1,)), ((), ())), preferred_element_type=F32)


def _dot_tn(a, b):
    return lax.dot_general(a, b, (((0,), (0,)), ((), ())), preferred_element_type=F32)


def _dot(a, b):
    return jnp.dot(a, b, preferred_element_type=F32)


class _Exchange:
    def __init__(self, groups, gather):
        self.groups, self.gather = groups, gather
        self.flat = [(gi, li, a) for gi, grp in enumerate(groups) for li, a in enumerate(grp)]
        self.n = len(self.flat)
        self.args = [a for (_, _, a) in self.flat]
        self.out_shape = [jax.ShapeDtypeStruct((len(grp), NDEV) + tuple(grp[0].shape[-2:]), grp[0].dtype) for grp in groups]
        self.scratch = [pltpu.SemaphoreType.DMA((self.n, NDEV - 1)), pltpu.SemaphoreType.DMA((self.n, NDEV - 1)),
                        pltpu.SemaphoreType.DMA((self.n,))]

    def _copies(self, ins, outs, send_sems, recv_sems, local_sems, landings):
        x, y, c = lax.axis_index("x"), lax.axis_index("y"), lax.axis_index("c")
        me = 4 * x + 2 * y + c
        owns = [pltpu.make_async_copy(ins[i] if self.gather else ins[i].at[me], outs[gi].at[li, me], local_sems.at[i])
                for i, (gi, li, _) in enumerate(self.flat)]
        pairs = []
        for k in range(1, NDEV):
            px = 1 - x if k & 4 else x
            py = 1 - y if k & 2 else y
            pc = 1 - c if k & 1 else c
            peer = 4 * px + 2 * py + pc
            for i, (gi, li, _) in enumerate(self.flat):
                src = ins[i] if self.gather else ins[i].at[peer]
                sems = dict(send_sem=send_sems.at[i, k - 1], recv_sem=recv_sems.at[i, k - 1],
                            device_id=(px, py, pc), device_id_type=pl.DeviceIdType.MESH)
                out = pltpu.make_async_remote_copy(src_ref=src, dst_ref=outs[gi].at[li, me], **sems)
                landing = pltpu.make_async_remote_copy(src_ref=src, dst_ref=outs[gi].at[li, peer], **sems) if landings else None
                pairs.append((out, landing))
        return owns, pairs

    def start(self, ins, outs, sems):
        owns, pairs = self._copies(ins, outs, *sems, landings=False)
        for own in owns:
            own.start()
        for out, _ in pairs:
            out.start()

    def mid(self, ins, outs, sems):
        pass

    def wait(self, ins, outs, sems):
        owns, pairs = self._copies(ins, outs, *sems, landings=True)
        for out, landing in pairs:
            out.wait_send()
            landing.wait_recv()
        for own in owns:
            own.wait()


class _GatherTwoLevel(_Exchange):
    def __init__(self, groups):
        super().__init__(groups, True)

    def _copy(self, i, k, ins, outs, send_sems, recv_sems, landing):
        gi, li, _ = self.flat[i]
        x, y, c = lax.axis_index("x"), lax.axis_index("y"), lax.axis_index("c")
        chips = [(x, y), (1 - x, y), (x, 1 - y), (1 - x, 1 - y)]

        def slot(chip, core):
            return outs[gi].at[li, 4 * chip[0] + 2 * chip[1] + core]

        if k == 0:
            to, src, dst, lands = (x, y, 1 - c), ins[i], slot(chips[0], c), slot(chips[0], 1 - c)
        elif k <= 3:
            to, src, dst, lands = (*chips[k], c), ins[i], slot(chips[0], c), slot(chips[k], c)
        else:
            to, src, dst, lands = (x, y, 1 - c), slot(chips[k - 3], c), slot(chips[k - 3], c), slot(chips[k - 3], 1 - c)
        return pltpu.make_async_remote_copy(src_ref=src, dst_ref=lands if landing else dst, send_sem=send_sems.at[i, k],
                                            recv_sem=recv_sems.at[i, k], device_id=to, device_id_type=pl.DeviceIdType.MESH)

    def _own(self, i, ins, outs, local_sems):
        gi, li, _ = self.flat[i]
        me = 4 * lax.axis_index("x") + 2 * lax.axis_index("y") + lax.axis_index("c")
        return pltpu.make_async_copy(ins[i], outs[gi].at[li, me], local_sems.at[i])

    def start(self, ins, outs, sems):
        send_sems, recv_sems, local_sems = sems
        for i in range(self.n):
            self._own(i, ins, outs, local_sems).start()
        for k in range(4):
            for i in range(self.n):
                self._copy(i, k, ins, outs, send_sems, recv_sems, False).start()

    def mid(self, ins, outs, sems):
        send_sems, recv_sems, _ = sems
        for k in range(1, 4):
            for i in range(self.n):
                self._copy(i, k, ins, outs, send_sems, recv_sems, True).wait_recv()
                self._copy(i, k + 3, ins, outs, send_sems, recv_sems, False).start()

    def wait(self, ins, outs, sems):
        send_sems, recv_sems, local_sems = sems
        for k in (0, 4, 5, 6):
            for i in range(self.n):
                self._copy(i, k, ins, outs, send_sems, recv_sems, True).wait_recv()
        for k in range(NDEV - 1):
            for i in range(self.n):
                self._copy(i, k, ins, outs, send_sems, recv_sems, False).wait_send()
        for i in range(self.n):
            self._own(i, ins, outs, local_sems).wait()


def _call(body, name, grid, in_specs, out_specs, out_shape, scratch, semantics, args, carry=None, aliases=None):
    n_in, n_out, n_scr = len(in_specs), len(out_specs), len(scratch)
    if carry is None:
        run = body
    else:
        semantics = ("arbitrary",) * len(grid)
        anyspec = pl.BlockSpec(memory_space=pl.ANY)
        in_specs = list(in_specs) + [anyspec] * carry.n
        out_specs = list(out_specs) + [anyspec] * len(carry.groups)
        out_shape = list(out_shape) + carry.out_shape
        scratch = list(scratch) + carry.scratch
        args = list(args) + carry.args

        def run(*refs):
            c_in, x_in = refs[:n_in], refs[n_in:n_in + carry.n]
            rest = refs[n_in + carry.n:]
            c_out, x_out = rest[:n_out], rest[n_out:n_out + len(carry.groups)]
            c_scr, sems = rest[n_out + len(carry.groups):len(rest) - 3], rest[len(rest) - 3:]
            step, total = 0, 1
            for d, extent in enumerate(grid):
                step = step * extent + pl.program_id(d)
                total *= extent

            @pl.when(step == 0)
            def _():
                carry.start(x_in, x_out, sems)

            body(*c_in, *c_out, *c_scr)

            @pl.when(step == (total * CARRY_MID_PERCENT) // 100)
            def _():
                carry.mid(x_in, x_out, sems)

            @pl.when(step == total - 1)
            def _():
                carry.wait(x_in, x_out, sems)

    res = pl.pallas_call(
        run, name=name, grid=grid, out_shape=list(out_shape), in_specs=list(in_specs), out_specs=list(out_specs),
        scratch_shapes=list(scratch), input_output_aliases=aliases or {},
        compiler_params=pltpu.CompilerParams(dimension_semantics=semantics, vmem_limit_bytes=VMEM_LIMIT,
                                             has_side_effects=carry is not None),
    )(*args)
    return list(res[:n_out]), list(res[n_out:])


def _exchange(ex, name):
    groups = ex.groups

    def body(*refs):
        ins, outs, sems = refs[:ex.n], refs[ex.n:ex.n + len(groups)], refs[ex.n + len(groups):]
        ex.start(ins, outs, sems)
        ex.mid(ins, outs, sems)
        ex.wait(ins, outs, sems)

    anyspec = pl.BlockSpec(memory_space=pl.ANY)
    return pl.pallas_call(
        body, name=name, out_shape=ex.out_shape, in_specs=[anyspec] * ex.n, out_specs=[anyspec] * len(groups),
        scratch_shapes=ex.scratch, compiler_params=pltpu.CompilerParams(has_side_effects=True),
    )(*ex.args)


def _matmul(a, b, mode, out_dtype, name, add=None, tm=1024, tn=1024, tk=2048, carry=None, b_cols=None):
    if mode == "tn":
        (K, M), (K2, N) = a.shape, b.shape
    elif mode == "nt":
        (M, K), (N, K2) = a.shape, b.shape
    else:
        (M, K), (K2, N) = a.shape, b.shape
    assert K == K2, (a.shape, b.shape, mode)
    col0 = 0
    if b_cols is not None:
        assert mode != "nt"
        col0, N = b_cols
        tn = min(tn, math.gcd(col0, N))
    tm, tn, tk = _tile(M, tm), _tile(N, tn), _tile(K, tk)
    assert col0 % tn == 0
    nk, jb = K // tk, col0 // tn
    a_spec = pl.BlockSpec((tk, tm), lambda i, j, k: (k, i)) if mode == "tn" else pl.BlockSpec((tm, tk), lambda i, j, k: (i, k))
    b_spec = pl.BlockSpec((tn, tk), lambda i, j, k: (j, k)) if mode == "nt" else pl.BlockSpec((tk, tn), lambda i, j, k: (k, j + jb))
    dot = {"nn": _dot, "nt": _dot_nt, "tn": _dot_tn}[mode]
    has_add = add is not None

    def body(*refs):
        a_ref, b_ref = refs[0], refs[1]
        o_ref, acc = refs[-2], refs[-1]
        k = pl.program_id(2)

        @pl.when(k == 0)
        def _():
            acc[...] = jnp.zeros_like(acc)

        acc[...] += dot(a_ref[...].astype(BF16), b_ref[...].astype(BF16))

        @pl.when(k == nk - 1)
        def _():
            r = acc[...]
            if has_add:
                r = r + refs[2][...]
            o_ref[...] = r.astype(o_ref.dtype)

    in_specs = [a_spec, b_spec]
    args = [a, b]
    if has_add:
        in_specs.append(pl.BlockSpec((tm, tn), lambda i, j, k: (i, j)))
        args.append(add)
    (out,), moved = _call(body, name, (M // tm, N // tn, nk), in_specs, [pl.BlockSpec((tm, tn), lambda i, j, k: (i, j))],
                          [jax.ShapeDtypeStruct((M, N), out_dtype)], [pltpu.VMEM((tm, tn), F32)],
                          ("parallel", "parallel", "arbitrary"), args, carry)
    return out if carry is None else (out, moved)


def _row_specs(views, tile):
    return [pl.BlockSpec((tile, w), functools.partial(lambda i, cb: (i, cb), cb=cb)) for (_, w, cb) in views]


def _full_specs(arrs):
    return [pl.BlockSpec(p.shape, functools.partial(lambda i, nd: (0,) * nd, nd=p.ndim)) for p in arrs]


def _rowwise(fn, rows, aux, params, consts, outs, tile, name):
    S = rows[0][0].shape[0]
    nr, na, npar, nc = len(rows), len(aux), len(params), len(consts)

    def body(*refs):
        ins = [r[...].astype(F32) for r in refs[:nr + na]]
        small = [r[...] for r in refs[nr + na:nr + na + npar + nc]]
        res = fn(*ins, *small)
        for o_ref, r in zip(refs[nr + na + npar + nc:], res):
            o_ref[...] = r.astype(o_ref.dtype)

    return pl.pallas_call(
        body, name=name, grid=(S // tile,),
        out_shape=[jax.ShapeDtypeStruct((S, w), dt) for (w, dt) in outs],
        in_specs=_row_specs(rows + aux, tile) + _full_specs(params + consts),
        out_specs=[pl.BlockSpec((tile, w), lambda i: (i, 0)) for (w, _) in outs],
        compiler_params=pltpu.CompilerParams(dimension_semantics=("parallel",), vmem_limit_bytes=VMEM_LIMIT),
    )(*[v[0] for v in rows + aux], *params, *consts)


def _rowwise_vjp(fn, rows, aux, params, consts, cots, grad_dtypes, tile, name, primal=()):
    S = rows[0][0].shape[0]
    nr, na, npar, nc, nct, npr = len(rows), len(aux), len(params), len(consts), len(cots), len(primal)

    def body(*refs):
        n_in = nr + na + npar + nc + nct
        rv = [r[...].astype(F32) for r in refs[:nr]]
        av = [r[...].astype(F32) for r in refs[nr:nr + na]]
        pv = [r[...] for r in refs[nr + na:nr + na + npar]]
        cv = [r[...] for r in refs[nr + na + npar:nr + na + npar + nc]]
        ct = tuple(r[...].astype(F32) for r in refs[nr + na + npar + nc:n_in])
        res, vjp = jax.vjp(lambda *rp: tuple(fn(*rp[:nr], *av, *rp[nr:], *cv)), *rv, *pv)
        grads = vjp(ct)
        g_refs = refs[n_in:n_in + nr]
        p_refs = refs[n_in + nr:n_in + nr + npar]
        o_refs = refs[n_in + nr + npar:]
        for g_ref, g in zip(g_refs, grads[:nr]):
            g_ref[...] = g.astype(g_ref.dtype)

        @pl.when(pl.program_id(0) == 0)
        def _():
            for p_ref in p_refs:
                p_ref[...] = jnp.zeros_like(p_ref)

        for p_ref, g in zip(p_refs, grads[nr:]):
            p_ref[...] += g
        for o_ref, r in zip(o_refs, res[:npr]):
            o_ref[...] = r.astype(o_ref.dtype)

    out_shape = ([jax.ShapeDtypeStruct((S, w), dt) for (_, w, _), dt in zip(rows, grad_dtypes)]
                 + [jax.ShapeDtypeStruct(p.shape, F32) for p in params]
                 + [jax.ShapeDtypeStruct((S, w), dt) for (w, dt) in primal])
    out_specs = ([pl.BlockSpec((tile, w), lambda i: (i, 0)) for (_, w, _) in rows] + _full_specs(params)
                 + [pl.BlockSpec((tile, w), lambda i: (i, 0)) for (w, _) in primal])
    res = pl.pallas_call(
        body, name=name, grid=(S // tile,), out_shape=out_shape,
        in_specs=_row_specs(rows + aux, tile) + _full_specs(params + consts) + _row_specs(cots, tile),
        out_specs=out_specs,
        compiler_params=pltpu.CompilerParams(dimension_semantics=("arbitrary",), vmem_limit_bytes=VMEM_LIMIT),
    )(*[v[0] for v in rows + aux], *params, *consts, *[v[0] for v in cots])
    return res[:nr], res[nr:nr + npar], res[nr + npar:]


@jax.custom_vjp
def _mm(a, b):
    return _dot(a.astype(BF16), b.astype(BF16))


def _mm_fwd(a, b):
    return _mm(a, b), (a, b)


def _mm_bwd(res, ct):
    a, b = res
    ctb = ct.astype(BF16)
    return _dot_nt(ctb, b.astype(BF16)), _dot_tn(a.astype(BF16), ctb)


_mm.defvjp(_mm_fwd, _mm_bwd)


def _rms(x, g):
    return x * lax.rsqrt(jnp.mean(x * x, axis=-1, keepdims=True) + EPS) * g


def _f_pre(x, g):
    return (_rms(x, g),)


def _f_pre_res(x, g):
    return _rms(x, g), x


def _f_gate(y, z, g):
    return (_rms(y, g) * jax.nn.silu(z),)


def _f_gmlp(u, v, z, g_v, w_s, b_s, g_o):
    groups = w_s.shape[0]
    u, v = jax.nn.gelu(u), jax.nn.gelu(v)
    t_idx = lax.broadcasted_iota(jnp.int32, (LANE, LANE), 0)
    s_idx = lax.broadcasted_iota(jnp.int32, (LANE, LANE), 1)
    ys = []
    for g in range(groups):
        sl = slice(g * LANE, (g + 1) * LANE)
        vn = _rms(v[:, sl], g_v[:, sl])
        w = jnp.where(s_idx <= t_idx, w_s[g], 0.0)
        ys.append(u[:, sl] * (_mm(w, vn) + b_s[g]))
    return (_rms(jnp.concatenate(ys, axis=1), g_o) * jax.nn.silu(z),)


def _rope(x, cos2, sin2, rot):
    return x * cos2 + _mm(x, rot) * sin2


def _f_cpre(cq, ckv, kr, cos2, sin2, g_q, g_kv, rot):
    return _rms(cq, g_q), _rms(ckv, g_kv), _rope(kr, cos2, sin2, rot)


def _f_crope(q, kv, krr, cos2, sin2, rot):
    heads = q.shape[1] // (2 * LANE)
    qs, ks, vs = [], [], []
    for h in range(heads):
        lo, mid, hi = 2 * h * LANE, (2 * h + 1) * LANE, (2 * h + 2) * LANE
        qs += [q[:, lo:mid], _rope(q[:, mid:hi], cos2, sin2, rot)]
        ks += [kv[:, lo:mid], krr]
        vs += [kv[:, mid:hi]]
    return jnp.concatenate(qs, axis=1), jnp.concatenate(ks, axis=1), jnp.concatenate(vs, axis=1)


def _f_final(h, target, g):
    err = _rms(h, g) - target
    return (0.5 * jnp.mean(err * err, axis=-1, keepdims=True),)


def _rope_matrix():
    r = np.zeros((LANE, LANE), np.float32)
    half = ROPE // 2
    for i in range(half):
        r[i + half, i] = -1.0
        r[i, i + half] = 1.0
    return jnp.asarray(r)


def _head_spec(view, rows, n_rows_block):
    _, cb0, w = view
    if n_rows_block:
        return pl.BlockSpec((rows, w), functools.partial(lambda h, i, cb0: (i, cb0 + h), cb0=cb0))
    return pl.BlockSpec((rows, w), functools.partial(lambda h, i, cb0: (0, cb0 + h), cb0=cb0))


def _stat_spec(tq):
    return pl.BlockSpec((1, tq, 1), lambda h, i: (h, i, 0))


def _softplus(z):
    return jnp.maximum(z, 0.0) + jnp.log(1.0 + jnp.exp(-jnp.abs(z)))


def _cumsum_mm(x, m01):
    hi = x.astype(BF16)
    lo = (x - hi.astype(F32)).astype(BF16)
    return _dot(hi, m01) + _dot(lo, m01)


def _attn_call(body, name, heads, S, tq, ins, in_blocked, outs, out_blocked, scratch, stats_in=0, stats_out=0, carry=None):
    in_specs = [_head_spec(v, tq if blk else S, blk) for v, blk in zip(ins[:len(ins) - stats_in], in_blocked)]
    in_specs += [_stat_spec(tq)] * stats_in
    out_specs = [_head_spec((None, 0, w), tq if blk else S, blk) for (w, _), blk in zip(outs, out_blocked)]
    out_specs += [_stat_spec(tq)] * stats_out
    out_shape = [jax.ShapeDtypeStruct((S, heads * w), dt) for (w, dt) in outs]
    out_shape += [jax.ShapeDtypeStruct((heads, S, 1), F32)] * stats_out
    args = [v[0] for v in ins[:len(ins) - stats_in]] + list(ins[len(ins) - stats_in:])
    res, moved = _call(body, name, (heads, S // tq), in_specs, out_specs, out_shape, scratch, ("arbitrary", "arbitrary"),
                       args, carry)
    return res if carry is None else res + [moved]


def _softmax_fwd(q, k, v, heads, scale, name, tq, bk, carry=None):
    S, dv = q[0].shape[0], v[2]

    def body(q_ref, k_ref, v_ref, o_ref, lse_ref):
        qi = pl.program_id(1)
        qv = q_ref[...]
        row = qi * tq + lax.broadcasted_iota(jnp.int32, (tq, bk), 0)
        col0 = lax.broadcasted_iota(jnp.int32, (tq, bk), 1)

        def step(kb, carry):
            m, l, acc = carry
            sl = pl.ds(pl.multiple_of(kb * bk, bk), bk)
            s = _dot_nt(qv, k_ref[sl, :]) * scale
            s = jnp.where(kb * bk + col0 <= row, s, -1e30)
            m_new = jnp.maximum(m, jnp.max(s, axis=1, keepdims=True))
            p = jnp.exp(s - m_new)
            alpha = jnp.exp(m - m_new)
            l = alpha * l + jnp.sum(p, axis=1, keepdims=True)
            acc = alpha * acc + _dot(p.astype(BF16), v_ref[sl, :])
            return m_new, l, acc

        n_kb = (qi * tq + tq + bk - 1) // bk
        m, l, acc = lax.fori_loop(0, n_kb, step, (jnp.full((tq, 1), -1e30, F32), jnp.zeros((tq, 1), F32),
                                                  jnp.zeros((tq, dv), F32)))
        o_ref[...] = (acc / l).astype(o_ref.dtype)
        lse_ref[0] = m + jnp.log(l)

    return _attn_call(body, name, heads, S, tq, [q, k, v], [1, 0, 0], [(dv, BF16)], [1], [], stats_out=1, carry=carry)


def _softmax_bwd(q, k, v, o, do, lse, heads, scale, name, tq, bk, carry=None):
    S, dq_w, dv = q[0].shape[0], q[2], v[2]
    nq = S // tq

    bd = min(DIAG_BLOCK, tq)
    assert tq % bk == 0 and tq % bd == 0

    def body(q_ref, k_ref, v_ref, o_ref, do_ref, lse_ref, dq_ref, dk_ref, dv_ref, dk_acc, dv_acc, delta_scr, dq_scr):
        qi = pl.program_id(1)

        @pl.when(qi == 0)
        def _():
            dk_acc[...] = jnp.zeros_like(dk_acc)
            dv_acc[...] = jnp.zeros_like(dv_acc)

        delta_scr[...] = jnp.sum(do_ref[...].astype(F32) * o_ref[...].astype(F32), axis=1, keepdims=True)
        dq_scr[...] = jnp.zeros_like(dq_scr)

        def block(r0, sl, width, masked):
            qv, dov = q_ref[r0:, :], do_ref[r0:, :]
            ks, vs = k_ref[sl, :], v_ref[sl, :]
            p = jnp.exp(_dot_nt(qv, ks) * scale - lse_ref[0, r0:, :])
            if masked:
                shape = (tq - r0, width)
                p = jnp.where(lax.broadcasted_iota(jnp.int32, shape, 1) <= lax.broadcasted_iota(jnp.int32, shape, 0), p, 0.0)
            ds = (p * (_dot_nt(dov, vs) - delta_scr[r0:, :]) * scale).astype(BF16)
            dk_acc[sl, :] += _dot_tn(ds, qv)
            dv_acc[sl, :] += _dot_tn(p.astype(BF16), dov)
            dq_scr[r0:, :] += _dot(ds, ks)

        def step(kb, _):
            block(0, pl.ds(pl.multiple_of(kb * bk, bk), bk), bk, False)
            return 0

        lax.fori_loop(0, qi * (tq // bk), step, 0)
        for j in range(tq // bd):
            block(j * bd, pl.ds(pl.multiple_of(qi * tq + j * bd, bd), bd), bd, True)
        dq_ref[...] = dq_scr[...].astype(dq_ref.dtype)

        @pl.when(qi == nq - 1)
        def _():
            dk_ref[...] = dk_acc[...].astype(dk_ref.dtype)
            dv_ref[...] = dv_acc[...].astype(dv_ref.dtype)

    return _attn_call(body, name, heads, S, tq, [q, k, v, o, do, lse], [1, 0, 0, 1, 1],
                      [(dq_w, BF16), (dq_w, BF16), (dv, BF16)], [1, 0, 0],
                      [pltpu.VMEM((S, dq_w), F32), pltpu.VMEM((S, dv), F32), pltpu.VMEM((tq, 1), F32),
                       pltpu.VMEM((tq, dq_w), F32)], stats_in=1, carry=carry)


def _stick_fwd(q, k, v, heads, scale, name, tq, bk, carry=None):
    S, dv = q[0].shape[0], v[2]

    assert tq % bk == 0
    n_sub = tq // bk

    def body(q_ref, k_ref, v_ref, o_ref, tot_ref, c_scr, acc_scr):
        qi = pl.program_id(1)
        m_gt = (lax.broadcasted_iota(jnp.int32, (bk, bk), 0) > lax.broadcasted_iota(jnp.int32, (bk, bk), 1)).astype(BF16)

        def block(r0, sl, masked):
            rows = tq - r0
            z = _dot_nt(q_ref[r0:, :], k_ref[sl, :]) * scale
            sp = _softplus(z)
            lk = -sp
            if masked:
                mask = lax.broadcasted_iota(jnp.int32, (rows, bk), 1) < lax.broadcasted_iota(jnp.int32, (rows, bk), 0)
                lk = jnp.where(mask, lk, 0.0)
            after = _cumsum_mm(lk, m_gt) + c_scr[r0:, :]
            a = jnp.exp(z - sp + after)
            if masked:
                a = jnp.where(mask, a, 0.0)
            acc_scr[r0:, :] += _dot(a.astype(BF16), v_ref[sl, :])
            c_scr[r0:, :] += jnp.sum(lk, axis=1, keepdims=True)

        c_scr[...] = jnp.zeros_like(c_scr)
        acc_scr[...] = jnp.zeros_like(acc_scr)
        for j in reversed(range(n_sub)):
            block(j * bk, pl.ds(pl.multiple_of(qi * tq + j * bk, bk), bk), True)

        def step(it, _):
            block(0, pl.ds(pl.multiple_of((qi * n_sub - 1 - it) * bk, bk), bk), False)
            return 0

        lax.fori_loop(0, qi * n_sub, step, 0)
        o_ref[...] = acc_scr[...].astype(o_ref.dtype)
        tot_ref[0] = c_scr[...]

    return _attn_call(body, name, heads, S, tq, [q, k, v], [1, 0, 0], [(dv, BF16)], [1],
                      [pltpu.VMEM((tq, 1), F32), pltpu.VMEM((tq, dv), F32)], stats_out=1, carry=carry)


def _stick_bwd(q, k, v, do, tot, heads, scale, name, tq, bk, carry=None):
    S, dq_w, dv = q[0].shape[0], q[2], v[2]
    nq = S // tq

    assert tq % bk == 0
    n_sub = tq // bk

    def body(q_ref, k_ref, v_ref, do_ref, tot_ref, dq_ref, dk_ref, dv_ref, dk_acc, dv_acc, pc_scr, gc_scr, dq_scr):
        qi = pl.program_id(1)

        @pl.when(qi == 0)
        def _():
            dk_acc[...] = jnp.zeros_like(dk_acc)
            dv_acc[...] = jnp.zeros_like(dv_acc)

        j_idx = lax.broadcasted_iota(jnp.int32, (bk, bk), 0)
        s_idx = lax.broadcasted_iota(jnp.int32, (bk, bk), 1)
        m_le, m_lt = (j_idx <= s_idx).astype(BF16), (j_idx < s_idx).astype(BF16)

        def block(r0, sl, masked):
            rows = tq - r0
            qv, dov = q_ref[r0:, :], do_ref[r0:, :]
            ks, vs = k_ref[sl, :], v_ref[sl, :]
            z = _dot_nt(qv, ks) * scale
            sp = _softplus(z)
            lk = -sp
            if masked:
                mask = lax.broadcasted_iota(jnp.int32, (rows, bk), 1) < lax.broadcasted_iota(jnp.int32, (rows, bk), 0)
                lk = jnp.where(mask, lk, 0.0)
            after = tot_ref[0, r0:, :] - pc_scr[r0:, :] - _cumsum_mm(lk, m_le)
            log_beta = z - sp
            a = jnp.exp(log_beta + after)
            if masked:
                a = jnp.where(mask, a, 0.0)
            g = _dot_nt(dov, vs) * a
            cg = gc_scr[r0:, :] + _cumsum_mm(g, m_lt)
            dz = g * jnp.exp(-sp) - jnp.exp(log_beta) * cg
            if masked:
                dz = jnp.where(mask, dz, 0.0)
            dz = (dz * scale).astype(BF16)
            dk_acc[sl, :] += _dot_tn(dz, qv)
            dv_acc[sl, :] += _dot_tn(a.astype(BF16), dov)
            dq_scr[r0:, :] += _dot(dz, ks)
            pc_scr[r0:, :] += jnp.sum(lk, axis=1, keepdims=True)
            gc_scr[r0:, :] += jnp.sum(g, axis=1, keepdims=True)

        pc_scr[...] = jnp.zeros_like(pc_scr)
        gc_scr[...] = jnp.zeros_like(gc_scr)
        dq_scr[...] = jnp.zeros_like(dq_scr)

        def step(kb, _):
            block(0, pl.ds(pl.multiple_of(kb * bk, bk), bk), False)
            return 0

        lax.fori_loop(0, qi * n_sub, step, 0)
        for j in range(n_sub):
            block(j * bk, pl.ds(pl.multiple_of(qi * tq + j * bk, bk), bk), True)
        dq_ref[...] = dq_scr[...].astype(dq_ref.dtype)

        @pl.when(qi == nq - 1)
        def _():
            dk_ref[...] = dk_acc[...].astype(dk_ref.dtype)
            dv_ref[...] = dv_acc[...].astype(dv_ref.dtype)

    return _attn_call(body, name, heads, S, tq, [q, k, v, do, tot], [1, 0, 0, 1],
                      [(dq_w, BF16), (dq_w, BF16), (dv, BF16)], [1, 0, 0],
                      [pltpu.VMEM((S, dq_w), F32), pltpu.VMEM((S, dv), F32), pltpu.VMEM((tq, 1), F32),
                       pltpu.VMEM((tq, 1), F32), pltpu.VMEM((tq, dq_w), F32)], stats_in=1, carry=carry)


def _adamw(slots, w, m, v, layer, prev, name, col0=0, carry=None):
    _, R, C = slots.shape
    L, full_c = w.shape[0], w.shape[2]
    item = slots.dtype.itemsize
    tc = _tile(C, 2048)
    tr = _tile(R, max(16, ADAM_TILE_BYTES // (item * tc)), mult=16)
    if tr == R and R * tc * item > ADAM_TILE_BYTES:
        tc = _tile(C, max(LANE, ADAM_TILE_BYTES // (item * R)))
    c1, c2 = 1.0 - ADAM_B1 ** ADAM_STEP, 1.0 - ADAM_B2 ** ADAM_STEP
    n_prev = 0 if prev is None else 4

    def body(s_ref, w_ref, m_ref, v_ref, *rest):
        g_out, d_out, m_out, v_out = rest[n_prev:]
        g = s_ref[0].astype(F32)
        for k in range(1, NDEV):
            g = g + s_ref[k].astype(F32)
        m_new = ADAM_B1 * m_ref[0] + (1.0 - ADAM_B1) * g
        v_new = ADAM_B2 * v_ref[0] + (1.0 - ADAM_B2) * (g * g)
        g_out[0] = g
        m_out[0] = m_new
        v_out[0] = v_new
        d_out[0] = -ADAM_LR * ((m_new / c1) / (jnp.sqrt(v_new / c2) + ADAM_EPS) + ADAM_WD * w_ref[0])

    assert col0 % tc == 0
    spec = pl.BlockSpec((1, tr, tc), lambda i, j: (layer, i, j + col0 // tc))
    in_specs = [pl.BlockSpec((NDEV, tr, tc), lambda i, j: (0, i, j)), spec, spec, spec]
    in_specs += [pl.BlockSpec(memory_space=pl.ANY)] * n_prev
    res, moved = _call(body, name, (R // tr, C // tc), in_specs, [spec] * 4, [jax.ShapeDtypeStruct((L, R, full_c), F32)] * 4,
                       [], ("parallel", "parallel"), [slots, w, m, v, *(prev or [])], carry,
                       aliases={4 + i: i for i in range(n_prev)})
    return res if carry is None else (res, moved)


class _Cfg:
    def __init__(self, S, D, groups, q_lora, kv_lora, c_heads, d_mix):
        self.S, self.D, self.G, self.Q, self.KV, self.Hc, self.DMIX = S, D, groups, q_lora, kv_lora, c_heads, d_mix
        self.A, self.C = groups * LANE, c_heads * LANE
        self.B = d_mix - self.A - self.C
        self.Hb = self.B // LANE
        A, B, C = self.A, self.B, self.C
        assert B % LANE == 0 and B % C == 0 and (B + C) % A == 0
        self.ref_segs = [("ua", A), ("va", A), ("za", A), ("qb", B), ("kb", B), ("vb", B), ("zb", B),
                         ("cq", q_lora), ("ckv", kv_lora), ("kr", ROPE), ("zc", C)]
        self.off, off = {}, 0
        for nm, w in [("ua", A), ("va", A), ("za", A), ("qb", B), ("kb", B), ("vb", B), ("zb", B), ("zc", C),
                      ("cq", q_lora), ("kr", LANE), ("ckv", kv_lora)]:
            off = -(-off // w) * w
            self.off[nm] = off
            off += w
        self.NP = -(-off // 512) * 512
        self.width = {"kr": LANE, **{nm: w for nm, w in self.ref_segs if nm != "kr"}}

    def tiles(self, kind, layer):
        tq, bk = ATTN_TILES[kind][layer % len(ATTN_TILES[kind])]
        return min(tq, self.S), min(bk, self.S)

    def view(self, arr, nm):
        w = self.width[nm]
        return (arr, w, self.off[nm] // w)

    def heads_view(self, arr, nm):
        return (arr, self.off[nm] // LANE, LANE)


def _gathered_rows(parts, a, b):
    per = sum(p.shape[1] for p in parts)
    out = []
    while a < b:
        k, r = divmod(a, per)
        i = 0
        while r >= parts[i].shape[1]:
            r -= parts[i].shape[1]
            i += 1
        n = min(b - a, parts[i].shape[1] - r)
        out.append(parts[i][k, r:r + n])
        a += n
    return out


def _pad_w_in(cfg, parts):
    width_d, dtype = parts[0].shape[2], parts[0].dtype
    start_of, start = {}, 0
    for nm, width in cfg.ref_segs:
        start_of[nm] = (start, width)
        start += width
    rows, pos = [], 0
    for nm, off in sorted(cfg.off.items(), key=lambda kv: kv[1]):
        if off > pos:
            rows.append(jnp.zeros((off - pos, width_d), dtype))
        rows += _gathered_rows(parts, start_of[nm][0], start_of[nm][0] + start_of[nm][1])
        pos = off + start_of[nm][1]
    if cfg.NP > pos:
        rows.append(jnp.zeros((cfg.NP - pos, width_d), dtype))
    return jnp.concatenate(rows, axis=0)


def _unpad_w_in(cfg, wpt):
    return jnp.concatenate([wpt[cfg.off[nm]:cfg.off[nm] + width] for nm, width in cfg.ref_segs], axis=0)


def _to_slots_cols(w):
    R = w.shape[0]
    return w.reshape(R, NDEV, -1).transpose(1, 0, 2)


def _from_slots_cols(s):
    return s.transpose(1, 0, 2).reshape(s.shape[1], -1)


def _perm_rows_out(cfg, w):
    return jnp.concatenate([w[cfg.A:], w[:cfg.A]], axis=0)


def _unperm_rows_out(cfg, w):
    return jnp.concatenate([w[cfg.B + cfg.C:], w[:cfg.B + cfg.C]], axis=0)


def _layer_params(cfg, l, g_pre, a_g_v, a_w_s, a_b_s, c_g_q, c_g_kv, g_out):
    A, B = cfg.A, cfg.B
    return dict(g_pre=g_pre[l][None], g_v=a_g_v[l].reshape(1, A), w_s=a_w_s[l], b_s=a_b_s[l][:, :, None],
                g_q=c_g_q[l][None], g_kv=c_g_kv[l][None],
                g_oa=g_out[l][None, :A], g_ob=g_out[l][None, A:A + B], g_oc=g_out[l][None, A + B:])


def _layer_fwd(cfg, l, x, W, p, cos2, sin2, rot, carry_in=None, carry_stick=None, carry_mla=None):
    S, D, A, B, C = cfg.S, cfg.D, cfg.A, cfg.B, cfg.C
    tag = f"l{l}"
    (h,) = _rowwise(_f_pre, [(x, D, 0)], [], [p["g_pre"]], [], [(D, BF16)], 256, f"pre_{tag}")
    if carry_in is None:
        proj = _matmul(h, W["in"], "nt", BF16, f"mm_in_{tag}")
    else:
        proj, moved_in = _matmul(h, W["in"], "nt", BF16, f"mm_in_{tag}", carry=carry_in[0])
        carry_in[1](moved_in)
    a_rows = [cfg.view(proj, "ua"), cfg.view(proj, "va"), cfg.view(proj, "za")]
    a_par = [p["g_v"], p["w_s"], p["b_s"], p["g_oa"]]
    (ya,) = _rowwise(_f_gmlp, a_rows, [], a_par, [], [(A, BF16)], LANE, f"gmlp_{tag}")
    qb, kb, vb = cfg.heads_view(proj, "qb"), cfg.heads_view(proj, "kb"), cfg.heads_view(proj, "vb")
    yb, tot, *moved_stick = _stick_fwd(qb, kb, vb, cfg.Hb, LANE ** -0.5, f"stick_fwd_{tag}", *cfg.tiles("stick_fwd", l),
                                       carry=carry_stick)
    (ybg,) = _rowwise(_f_gate, [(yb, B, 0), cfg.view(proj, "zb")], [], [p["g_ob"]], [], [(B, BF16)], 256, f"gate_b_{tag}")
    c_rows = [cfg.view(proj, "cq"), cfg.view(proj, "ckv"), cfg.view(proj, "kr")]
    trig = [(cos2, LANE, 0), (sin2, LANE, 0)]
    cqn, ckvn, krr = _rowwise(_f_cpre, c_rows, trig, [p["g_q"], p["g_kv"]], [rot],
                              [(cfg.Q, BF16), (cfg.KV, BF16), (LANE, BF16)], 256, f"cpre_{tag}")
    q_raw = _matmul(cqn, W["uq"], "nt", BF16, f"mm_uq_{tag}")
    kv = _matmul(ckvn, W["ukv"], "nn", BF16, f"mm_ukv_{tag}")
    r_rows = [(q_raw, 2 * C, 0), (kv, 2 * C, 0), (krr, LANE, 0)]
    q_rot, k_full, v_c = _rowwise(_f_crope, r_rows, trig, [], [rot], [(2 * C, BF16), (2 * C, BF16), (C, BF16)], 128,
                                  f"crope_{tag}")
    qc, kc, vc = (q_rot, 0, 2 * LANE), (k_full, 0, 2 * LANE), (v_c, 0, LANE)
    yc, lse, *moved_mla = _softmax_fwd(qc, kc, vc, cfg.Hc, (LANE + ROPE) ** -0.5, f"mla_fwd_{tag}", *cfg.tiles("mla_fwd", l),
                                       carry=carry_mla)
    (ycg,) = _rowwise(_f_gate, [(yc, C, 0), cfg.view(proj, "zc")], [], [p["g_oc"]], [], [(C, BF16)], 256, f"gate_c_{tag}")
    y = jnp.concatenate([ybg, ycg, ya], axis=1)
    out = _matmul(y, W["out"], "nn", F32, f"mm_out_{tag}", add=x)
    saved = dict(x=x, h=h, proj=proj, yb=yb, tot=tot, cqn=cqn, ckvn=ckvn, krr=krr, q_raw=q_raw, kv=kv,
                 q_rot=q_rot, k_full=k_full, v_c=v_c, yc=yc, lse=lse, y=y)
    return out, saved, (moved_stick[0] if moved_stick else []), (moved_mla[0] if moved_mla else [])


def _layer_bwd(cfg, l, dout, sv, W, p, cos2, sin2, rot, ext_stick, ext_mla, last):
    S, D, A, B, C = cfg.S, cfg.D, cfg.A, cfg.B, cfg.C
    tag = f"l{l}"
    proj = sv["proj"]
    dy = _matmul(dout, W["out"], "nt", BF16, f"mm_dy_{tag}")
    d_wout = _matmul(sv["y"], dout, "tn", BF16, f"mm_dwout_{tag}")
    wout_slots = _unperm_rows_out(cfg, d_wout).reshape(NDEV, cfg.DMIX // NDEV, D)
    (dyb, dzb), (dg_ob,), _ = _rowwise_vjp(_f_gate, [(sv["yb"], B, 0), cfg.view(proj, "zb")], [], [p["g_ob"]], [],
                                           [(dy, B, 0)], [BF16, BF16], 256, f"gate_b_bwd_{tag}")
    (dyc, dzc), (dg_oc,), _ = _rowwise_vjp(_f_gate, [(sv["yc"], C, 0), cfg.view(proj, "zc")], [], [p["g_oc"]], [],
                                           [(dy, C, B // C)], [BF16, BF16], 256, f"gate_c_bwd_{tag}")
    a_rows = [cfg.view(proj, "ua"), cfg.view(proj, "va"), cfg.view(proj, "za")]
    a_par = [p["g_v"], p["w_s"], p["b_s"], p["g_oa"]]
    (dua, dva, dza), (dg_v, dw_s, db_s, dg_oa), _ = _rowwise_vjp(
        _f_gmlp, a_rows, [], a_par, [], [(dy, A, (B + C) // A)], [BF16] * 3, LANE, f"gmlp_bwd_{tag}")
    qb, kb, vb = cfg.heads_view(proj, "qb"), cfg.heads_view(proj, "kb"), cfg.heads_view(proj, "vb")
    dqb, dkb, dvb, moved_stick = _stick_bwd(qb, kb, vb, (dyb, 0, LANE), sv["tot"], cfg.Hb, LANE ** -0.5,
                                            f"stick_bwd_{tag}", *cfg.tiles("stick_bwd", l),
                                            carry=_Exchange([[a] for a in (ext_stick or [wout_slots])], False))
    ext_got = [mv[0] for mv in moved_stick] if ext_stick else []
    ext_mla = ext_mla + ([wout_slots] if ext_stick else [])
    qc, kc, vc = (sv["q_rot"], 0, 2 * LANE), (sv["k_full"], 0, 2 * LANE), (sv["v_c"], 0, LANE)
    dq_rot, dk_full, dv_c, *moved_mla = _softmax_bwd(qc, kc, vc, (sv["yc"], 0, LANE), (dyc, 0, LANE), sv["lse"], cfg.Hc,
                                                     (LANE + ROPE) ** -0.5, f"mla_bwd_{tag}", *cfg.tiles("mla_bwd", l),
                                                     carry=_Exchange([[a] for a in ext_mla], False) if ext_mla else None)
    moved_mla = [mv[0] for mv in (moved_mla[0] if moved_mla else [])]
    got = dict(w_out=moved_mla.pop() if ext_stick else moved_stick[0][0])
    ext_got += moved_mla
    trig = [(cos2, LANE, 0), (sin2, LANE, 0)]
    r_rows = [(sv["q_raw"], 2 * C, 0), (sv["kv"], 2 * C, 0), (sv["krr"], LANE, 0)]
    (dq_raw, dkv, dkrr), _, _ = _rowwise_vjp(_f_crope, r_rows, trig, [], [rot],
                                             [(dq_rot, 2 * C, 0), (dk_full, 2 * C, 0), (dv_c, C, 0)], [BF16] * 3, 128,
                                             f"crope_bwd_{tag}")
    dcqn = _matmul(dq_raw, W["uq"], "nn", BF16, f"mm_dcq_{tag}")
    d_wuq = _matmul(dq_raw, sv["cqn"], "tn", BF16, f"mm_dwuq_{tag}")
    dckvn = _matmul(dkv, W["ukv"], "nt", BF16, f"mm_dckv_{tag}")
    d_wukv = _matmul(sv["ckvn"], dkv, "tn", BF16, f"mm_dwukv_{tag}")
    c_rows = [cfg.view(proj, "cq"), cfg.view(proj, "ckv"), cfg.view(proj, "kr")]
    (dcq, dckv, dkr), (dg_q, dg_kv), _ = _rowwise_vjp(
        _f_cpre, c_rows, trig, [p["g_q"], p["g_kv"]], [rot],
        [(dcqn, cfg.Q, 0), (dckvn, cfg.KV, 0), (dkrr, LANE, 0)], [BF16] * 3, 256, f"cpre_bwd_{tag}")
    parts = dict(ua=dua, va=dva, za=dza, qb=dqb, kb=dkb, vb=dvb, zb=dzb, zc=dzc, cq=dcq, kr=dkr, ckv=dckv)
    cols, pos = [], 0
    for nm, off in sorted(cfg.off.items(), key=lambda kv_: kv_[1]):
        if off > pos:
            cols.append(jnp.zeros((S, off - pos), BF16))
        cols.append(parts[nm])
        pos = off + parts[nm].shape[1]
    if cfg.NP > pos:
        cols.append(jnp.zeros((S, cfg.NP - pos), BF16))
    dproj = jnp.concatenate(cols, axis=1)
    to_send = dict(c_w_uq=d_wuq.reshape(cfg.Hc, 2 * LANE, cfg.Q)[:, :LANE + ROPE].reshape(NDEV, -1, cfg.Q),
                   c_w_ukv=_to_slots_cols(d_wukv))
    def d_win(c0, width, name, carry=None):
        res = _matmul(dproj, sv["h"], "tn", BF16, name, b_cols=(c0, width), carry=carry)
        wt, moved = res if carry is not None else (res, None)
        return _unpad_w_in(cfg, wt).reshape(NDEV, -1, width), moved

    if last:
        ranges = LAST_W_IN_RANGES if D % (sum(LAST_W_IN_RANGES) * LANE) == 0 else (1, 1)
        unit = D // sum(ranges)
        arrived, c0, sending = [], 0, None
        for i, r in enumerate(ranges):
            carry = _Exchange([[sending]], False) if sending is not None else None
            sending, moved = d_win(c0, r * unit, f"mm_dwin_{i}_{tag}", carry)
            if moved is not None:
                arrived.append(moved[0][0])
            c0 += r * unit
        dh, moved = _matmul(dproj, W["in"], "nn", BF16, f"mm_dh_{tag}",
                            carry=_Exchange([[sending], [to_send["c_w_uq"]], [to_send["c_w_ukv"]]], False))
        got.update(w_in=tuple(arrived) + (moved[0][0],), c_w_uq=moved[1][0], c_w_ukv=moved[2][0])
        to_send = {}
    else:
        first = D // 2 if (D // 4) % LANE else D // 4
        slots_a, _ = d_win(0, first, f"mm_dwin_a_{tag}")
        dh, moved_a = _matmul(dproj, W["in"], "nn", BF16, f"mm_dh_{tag}", carry=_Exchange([[slots_a]], False))
        got["w_in"] = (moved_a[0][0],)
        to_send["w_in"], _ = d_win(first, D - first, f"mm_dwin_b_{tag}")
    (dx,), (dg_pre,), _ = _rowwise_vjp(_f_pre_res, [(sv["x"], D, 0)], [], [p["g_pre"]], [],
                                       [(dh, D, 0), (dout, D, 0)], [F32], 128, f"pre_bwd_{tag}")
    small = dict(g_pre=dg_pre[0], a_g_v=dg_v.reshape(cfg.G, LANE), a_w_s=dw_s, a_b_s=db_s[:, :, 0], c_g_q=dg_q[0],
                 c_g_kv=dg_kv[0], g_out=jnp.concatenate([dg_oa[0], dg_ob[0], dg_oc[0]]))
    return dx, small, got, to_send, ext_got


def _pack_small(vals):
    pieces = []
    for nm in SMALL:
        piece = vals[nm].reshape(-1, LANE)
        pieces.append(jnp.pad(piece, ((0, -piece.shape[0] % 8), (0, 0))))
    packed = jnp.concatenate(pieces, axis=0)
    return jnp.pad(packed, ((0, -packed.shape[0] % SMALL_ROWS), (0, 0)))


def _unpack_small(packed, like):
    out, row = {}, 0
    for nm in SMALL:
        n = like[nm].size // LANE
        out[nm] = packed[row:row + n].reshape(like[nm].shape)
        row += n + (-n % 8)
    return out


def kernel(x, positions, g_pre, w_in, a_g_v, a_w_s, a_b_s, c_g_q, c_g_kv, c_w_uq, c_w_ukv, g_out, w_out, g_final, loss_target, m_g_pre, m_w_in, m_a_g_v, m_a_w_s, m_a_b_s, m_c_g_q, m_c_g_kv, m_c_w_uq, m_c_w_ukv, m_g_out, m_w_out, m_g_final, v_g_pre, v_w_in, v_a_g_v, v_a_w_s, v_a_b_s, v_c_g_q, v_c_g_kv, v_c_w_uq, v_c_w_ukv, v_g_out, v_w_out, v_g_final):
    depth, S, D = w_in.shape[0], x.shape[1], x.shape[2]
    cfg = _Cfg(S, D, a_g_v.shape[1], c_g_q.shape[1], c_g_kv.shape[1], c_w_ukv.shape[2] * NDEV // (2 * LANE), g_out.shape[1])
    weights = dict(g_pre=g_pre, w_in=w_in, a_g_v=a_g_v, a_w_s=a_w_s, a_b_s=a_b_s, c_g_q=c_g_q, c_g_kv=c_g_kv,
                   c_w_uq=c_w_uq, c_w_ukv=c_w_ukv, g_out=g_out, w_out=w_out, g_final=g_final)
    mom_m = dict(g_pre=m_g_pre, w_in=m_w_in, a_g_v=m_a_g_v, a_w_s=m_a_w_s, a_b_s=m_a_b_s, c_g_q=m_c_g_q, c_g_kv=m_c_g_kv,
                 c_w_uq=m_c_w_uq, c_w_ukv=m_c_w_ukv, g_out=m_g_out, w_out=m_w_out, g_final=m_g_final)
    mom_v = dict(g_pre=v_g_pre, w_in=v_w_in, a_g_v=v_a_g_v, a_w_s=v_a_w_s, a_b_s=v_a_b_s, c_g_q=v_c_g_q, c_g_kv=v_c_g_kv,
                 c_w_uq=v_c_w_uq, c_w_ukv=v_c_w_ukv, g_out=v_g_out, w_out=v_w_out, g_final=v_g_final)
    big_names = ("w_in", "c_w_uq", "c_w_ukv", "w_out")

    inv_freq = 1.0 / (ROPE_THETA ** (jnp.arange(0, ROPE, 2, dtype=F32) / ROPE))
    ang = positions[0].astype(F32)[:, None] * inv_freq
    zpad = jnp.zeros((S, LANE - ROPE), F32)
    cos2 = jnp.concatenate([jnp.cos(ang), jnp.cos(ang), zpad], axis=1)
    sin2 = jnp.concatenate([jnp.sin(ang), jnp.sin(ang), zpad], axis=1)
    rot = _rope_matrix()

    for tree in (weights, mom_m, mom_v):
        for nm in TRANSPOSED:
            tree[nm] = jnp.swapaxes(tree[nm], 1, 2)

    def shards(l, names):
        return [[weights[nm][l].astype(BF16)] for nm in names]

    def assemble_rest(g_uq, g_ukv, g_wout):
        uq = jnp.pad(g_uq[0].reshape(cfg.Hc, LANE + ROPE, cfg.Q), ((0, 0), (0, LANE - ROPE), (0, 0)))
        return {"uq": uq.reshape(2 * cfg.C, cfg.Q), "ukv": _from_slots_cols(g_ukv[0]),
                "out": _perm_rows_out(cfg, g_wout[0].reshape(cfg.DMIX, D))}

    params = [_layer_params(cfg, l, g_pre, a_g_v, a_w_s, a_b_s, c_g_q, c_g_kv, g_out) for l in range(depth)]

    in_parts = [g[0] for g in _exchange(_GatherTwoLevel(shards(0, big_names[:1])), "gather_w_in_l0")]
    got_rest = None
    hcur, saved, Ws = x[0], [], []
    for l in range(depth):
        Ws.append({"in": _pad_w_in(cfg, in_parts)})
        if got_rest is not None:
            Ws[l].update(assemble_rest(*got_rest))
        nxt = l + 1 < depth
        early_rows = min(W_IN_EARLY_ROWS, weights["w_in"].shape[1] // 2)
        riding = ([] if got_rest is not None else shards(l, big_names[1:]))
        riding += [[weights["w_in"][l + 1][:early_rows].astype(BF16)]] if nxt else []
        in_parts = []

        def take(moved, l=l, rest_here=got_rest is None, nxt=nxt):
            if rest_here:
                Ws[l].update(assemble_rest(*moved[:3]))
            if nxt:
                in_parts.append(moved[-1][0])

        hcur, sv, got_late, got_rest = _layer_fwd(
            cfg, l, hcur, Ws[l], params[l], cos2, sin2, rot,
            carry_in=(_GatherTwoLevel(riding), take) if riding else None,
            carry_stick=_GatherTwoLevel([[weights["w_in"][l + 1][early_rows:].astype(BF16)]]) if nxt else None,
            carry_mla=_GatherTwoLevel(shards(l + 1, big_names[1:])) if nxt else None)
        in_parts += [g[0] for g in got_late]
        saved.append(sv)
    (dh,), (dg_final,), (loss_rows,) = _rowwise_vjp(
        _f_final, [(hcur, D, 0)], [(loss_target[0], D, 0)], [g_final[None]], [], [(jnp.ones((S, 1), F32), 1, 0)],
        [F32], 128, "final", primal=[(1, F32)])
    loss = lax.psum(jnp.sum(loss_rows), MESH_AXES)

    small_g, slots, pending = [None] * depth, [None] * depth, {}
    for l in reversed(range(depth)):
        ext_stick = [pending["w_in"]] if pending else []
        ext_mla = [pending["c_w_uq"], pending["c_w_ukv"]] if pending else []
        dh, small_g[l], slots[l], pending, ext_got = _layer_bwd(cfg, l, dh, saved[l], Ws[l], params[l], cos2, sin2, rot,
                                                                ext_stick, ext_mla, l == 0)
        if ext_got:
            slots[l + 1].update(w_in=slots[l + 1]["w_in"] + (ext_got[0],), c_w_uq=ext_got[1], c_w_ukv=ext_got[2])
    grad_x = dh[None]
    small_grads = {nm: jnp.stack([small_g[l][nm] for l in range(depth)]) for nm in SMALL if nm != "g_final"}
    small_grads["g_final"] = dg_final[0]
    (small_slots,) = _exchange(_GatherTwoLevel([[_pack_small(small_grads)]]), "gather_small_grads")

    res = {}
    for nm in big_names:
        res[nm] = None
        for l in reversed(range(depth)):
            parts = slots[l][nm] if isinstance(slots[l][nm], tuple) else (slots[l][nm],)
            col0 = 0
            for i, part in enumerate(parts):
                res[nm] = _adamw(part, weights[nm], mom_m[nm], mom_v[nm], l, res[nm], f"adamw_{nm}_l{l}_{i}", col0)
                col0 += part.shape[2]
        if nm in TRANSPOSED:
            res[nm] = [jnp.swapaxes(r, 1, 2) for r in res[nm]]
    packed = _adamw(small_slots[0], _pack_small(weights)[None], _pack_small(mom_m)[None], _pack_small(mom_v)[None], 0, None,
                    "adamw_small")
    small_res = [_unpack_small(r[0], weights) for r in packed]
    order = ("g_pre", "w_in", "a_g_v", "a_w_s", "a_b_s", "c_g_q", "c_g_kv", "c_w_uq", "c_w_ukv", "g_out", "w_out", "g_final")
    outs = [loss, grad_x]
    for kind in range(4):
        outs += [small_res[kind][nm] if nm in SMALL else res[nm][kind] for nm in order]
    return tuple(outs)
```

```python
import functools
import math

import numpy as np
import jax
import jax.numpy as jnp
from jax import lax
from jax.experimental import pallas as pl
from jax.experimental.pallas import tpu as pltpu

NDEV = 8
MESH_AXES = ("x", "y", "c")
LANE = 128
ROPE = 64
EPS = 1e-6
ROPE_THETA = 10000.0
ADAM_LR, ADAM_B1, ADAM_B2, ADAM_EPS, ADAM_WD, ADAM_STEP = 0.001, 0.9, 0.999, 1e-08, 0.01, 10
VMEM_LIMIT = 48 * 1024 * 1024
ADAM_TILE_BYTES = 768 * 1024
LAST_W_IN_RANGES = (1, 2, 2, 3)
W_IN_EARLY_ROWS = 512
CARRY_MID_PERCENT = 88
SMALL_ROWS = 256
ATTN_TILES = {"stick_fwd": [(2048, 256)], "stick_bwd": [(2048, 256)], "mla_fwd": [(512, 1024)], "mla_bwd": [(2048, 512)]}
DIAG_BLOCK = 256
F32, BF16 = jnp.float32, jnp.bfloat16
SMALL = ("g_pre", "a_g_v", "a_w_s", "a_b_s", "c_g_q", "c_g_kv", "g_out", "g_final")
TRANSPOSED = ("w_in", "c_w_uq")


def _tile(dim, cap, mult=LANE):
    if dim <= cap:
        return dim
    t = (cap // mult) * mult
    while t >= mult:
        if dim % t == 0:
            return t
        t -= mult
    return dim


def _dot_nt(a, b):
    return lax.dot_general(a, b, (((1,), (1,)), ((), ())), preferred_element_type=F32)


def _dot_tn(a, b):
    return lax.dot_general(a, b, (((0,), (0,)), ((), ())), preferred_element_type=F32)


def _dot(a, b):
    return jnp.dot(a, b, preferred_element_type=F32)


class _Exchange:
    def __init__(self, groups, gather):
        self.groups, self.gather = groups, gather
        self.flat = [(gi, li, a) for gi, grp in enumerate(groups) for li, a in enumerate(grp)]
        self.n = len(self.flat)
        self.args = [a for (_, _, a) in self.flat]
        self.out_shape = [jax.ShapeDtypeStruct((len(grp), NDEV) + tuple(grp[0].shape[-2:]), grp[0].dtype) for grp in groups]
        self.scratch = [pltpu.SemaphoreType.DMA((self.n, NDEV - 1)), pltpu.SemaphoreType.DMA((self.n, NDEV - 1)),
                        pltpu.SemaphoreType.DMA((self.n,))]

    def _copies(self, ins, outs, send_sems, recv_sems, local_sems, landings):
        x, y, c = lax.axis_index("x"), lax.axis_index("y"), lax.axis_index("c")
        me = 4 * x + 2 * y + c
        owns = [pltpu.make_async_copy(ins[i] if self.gather else ins[i].at[me], outs[gi].at[li, me], local_sems.at[i])
                for i, (gi, li, _) in enumerate(self.flat)]
        pairs = []
        for k in range(1, NDEV):
            px = 1 - x if k & 4 else x
            py = 1 - y if k & 2 else y
            pc = 1 - c if k & 1 else c
            peer = 4 * px + 2 * py + pc
            for i, (gi, li, _) in enumerate(self.flat):
                src = ins[i] if self.gather else ins[i].at[peer]
                sems = dict(send_sem=send_sems.at[i, k - 1], recv_sem=recv_sems.at[i, k - 1],
                            device_id=(px, py, pc), device_id_type=pl.DeviceIdType.MESH)
                out = pltpu.make_async_remote_copy(src_ref=src, dst_ref=outs[gi].at[li, me], **sems)
                landing = pltpu.make_async_remote_copy(src_ref=src, dst_ref=outs[gi].at[li, peer], **sems) if landings else None
                pairs.append((out, landing))
        return owns, pairs

    def start(self, ins, outs, sems):
        owns, pairs = self._copies(ins, outs, *sems, landings=False)
        for own in owns:
            own.start()
        for out, _ in pairs:
            out.start()

    def mid(self, ins, outs, sems):
        pass

    def wait(self, ins, outs, sems):
        owns, pairs = self._copies(ins, outs, *sems, landings=True)
        for out, landing in pairs:
            out.wait_send()
            landing.wait_recv()
        for own in owns:
            own.wait()


class _GatherTwoLevel(_Exchange):
    def __init__(self, groups):
        super().__init__(groups, True)

    def _copy(self, i, k, ins, outs, send_sems, recv_sems, landing):
        gi, li, _ = self.flat[i]
        x, y, c = lax.axis_index("x"), lax.axis_index("y"), lax.axis_index("c")
        chips = [(x, y), (1 - x, y), (x, 1 - y), (1 - x, 1 - y)]

        def slot(chip, core):
            return outs[gi].at[li, 4 * chip[0] + 2 * chip[1] + core]

        if k == 0:
            to, src, dst, lands = (x, y, 1 - c), ins[i], slot(chips[0], c), slot(chips[0], 1 - c)
        elif k <= 3:
            to, src, dst, lands = (*chips[k], c), ins[i], slot(chips[0], c), slot(chips[k], c)
        else:
            to, src, dst, lands = (x, y, 1 - c), slot(chips[k - 3], c), slot(chips[k - 3], c), slot(chips[k - 3], 1 - c)
        return pltpu.make_async_remote_copy(src_ref=src, dst_ref=lands if landing else dst, send_sem=send_sems.at[i, k],
                                            recv_sem=recv_sems.at[i, k], device_id=to, device_id_type=pl.DeviceIdType.MESH)

    def _own(self, i, ins, outs, local_sems):
        gi, li, _ = self.flat[i]
        me = 4 * lax.axis_index("x") + 2 * lax.axis_index("y") + lax.axis_index("c")
        return pltpu.make_async_copy(ins[i], outs[gi].at[li, me], local_sems.at[i])

    def start(self, ins, outs, sems):
        send_sems, recv_sems, local_sems = sems
        for i in range(self.n):
            self._own(i, ins, outs, local_sems).start()
        for k in range(4):
            for i in range(self.n):
                self._copy(i, k, ins, outs, send_sems, recv_sems, False).start()

    def mid(self, ins, outs, sems):
        send_sems, recv_sems, _ = sems
        for k in range(1, 4):
            for i in range(self.n):
                self._copy(i, k, ins, outs, send_sems, recv_sems, True).wait_recv()
                self._copy(i, k + 3, ins, outs, send_sems, recv_sems, False).start()

    def wait(self, ins, outs, sems):
        send_sems, recv_sems, local_sems = sems
        for k in (0, 4, 5, 6):
            for i in range(self.n):
                self._copy(i, k, ins, outs, send_sems, recv_sems, True).wait_recv()
        for k in range(NDEV - 1):
            for i in range(self.n):
                self._copy(i, k, ins, outs, send_sems, recv_sems, False).wait_send()
        for i in range(self.n):
            self._own(i, ins, outs, local_sems).wait()


def _call(body, name, grid, in_specs, out_specs, out_shape, scratch, semantics, args, carry=None, aliases=None):
    n_in, n_out, n_scr = len(in_specs), len(out_specs), len(scratch)
    if carry is None:
        run = body
    else:
        semantics = ("arbitrary",) * len(grid)
        anyspec = pl.BlockSpec(memory_space=pl.ANY)
        in_specs = list(in_specs) + [anyspec] * carry.n
        out_specs = list(out_specs) + [anyspec] * len(carry.groups)
        out_shape = list(out_shape) + carry.out_shape
        scratch = list(scratch) + carry.scratch
        args = list(args) + carry.args

        def run(*refs):
            c_in, x_in = refs[:n_in], refs[n_in:n_in + carry.n]
            rest = refs[n_in + carry.n:]
            c_out, x_out = rest[:n_out], rest[n_out:n_out + len(carry.groups)]
            c_scr, sems = rest[n_out + len(carry.groups):len(rest) - 3], rest[len(rest) - 3:]
            step, total = 0, 1
            for d, extent in enumerate(grid):
                step = step * extent + pl.program_id(d)
                total *= extent

            @pl.when(step == 0)
            def _():
                carry.start(x_in, x_out, sems)

            body(*c_in, *c_out, *c_scr)

            @pl.when(step == (total * CARRY_MID_PERCENT) // 100)
            def _():
                carry.mid(x_in, x_out, sems)

            @pl.when(step == total - 1)
            def _():
                carry.wait(x_in, x_out, sems)

    res = pl.pallas_call(
        run, name=name, grid=grid, out_shape=list(out_shape), in_specs=list(in_specs), out_specs=list(out_specs),
        scratch_shapes=list(scratch), input_output_aliases=aliases or {},
        compiler_params=pltpu.CompilerParams(dimension_semantics=semantics, vmem_limit_bytes=VMEM_LIMIT,
                                             has_side_effects=carry is not None),
    )(*args)
    return list(res[:n_out]), list(res[n_out:])


def _exchange(ex, name):
    groups = ex.groups

    def body(*refs):
        ins, outs, sems = refs[:ex.n], refs[ex.n:ex.n + len(groups)], refs[ex.n + len(groups):]
        ex.start(ins, outs, sems)
        ex.mid(ins, outs, sems)
        ex.wait(ins, outs, sems)

    anyspec = pl.BlockSpec(memory_space=pl.ANY)
    return pl.pallas_call(
        body, name=name, out_shape=ex.out_shape, in_specs=[anyspec] * ex.n, out_specs=[anyspec] * len(groups),
        scratch_shapes=ex.scratch, compiler_params=pltpu.CompilerParams(has_side_effects=True),
    )(*ex.args)


def _matmul(a, b, mode, out_dtype, name, add=None, tm=1024, tn=1024, tk=2048, carry=None, b_cols=None):
    if mode == "tn":
        (K, M), (K2, N) = a.shape, b.shape
    elif mode == "nt":
        (M, K), (N, K2) = a.shape, b.shape
    else:
        (M, K), (K2, N) = a.shape, b.shape
    assert K == K2, (a.shape, b.shape, mode)
    col0 = 0
    if b_cols is not None:
        assert mode != "nt"
        col0, N = b_cols
        tn = min(tn, math.gcd(col0, N))
    tm, tn, tk = _tile(M, tm), _tile(N, tn), _tile(K, tk)
    assert col0 % tn == 0
    nk, jb = K // tk, col0 // tn
    a_spec = pl.BlockSpec((tk, tm), lambda i, j, k: (k, i)) if mode == "tn" else pl.BlockSpec((tm, tk), lambda i, j, k: (i, k))
    b_spec = pl.BlockSpec((tn, tk), lambda i, j, k: (j, k)) if mode == "nt" else pl.BlockSpec((tk, tn), lambda i, j, k: (k, j + jb))
    dot = {"nn": _dot, "nt": _dot_nt, "tn": _dot_tn}[mode]
    has_add = add is not None

    def body(*refs):
        a_ref, b_ref = refs[0], refs[1]
        o_ref, acc = refs[-2], refs[-1]
        k = pl.program_id(2)

        @pl.when(k == 0)
        def _():
            acc[...] = jnp.zeros_like(acc)

        acc[...] += dot(a_ref[...].astype(BF16), b_ref[...].astype(BF16))

        @pl.when(k == nk - 1)
        def _():
            r = acc[...]
            if has_add:
                r = r + refs[2][...]
            o_ref[...] = r.astype(o_ref.dtype)

    in_specs = [a_spec, b_spec]
    args = [a, b]
    if has_add:
        in_specs.append(pl.BlockSpec((tm, tn), lambda i, j, k: (i, j)))
        args.append(add)
    (out,), moved = _call(body, name, (M // tm, N // tn, nk), in_specs, [pl.BlockSpec((tm, tn), lambda i, j, k: (i, j))],
                          [jax.ShapeDtypeStruct((M, N), out_dtype)], [pltpu.VMEM((tm, tn), F32)],
                          ("parallel", "parallel", "arbitrary"), args, carry)
    return out if carry is None else (out, moved)


def _row_specs(views, tile):
    return [pl.BlockSpec((tile, w), functools.partial(lambda i, cb: (i, cb), cb=cb)) for (_, w, cb) in views]


def _full_specs(arrs):
    return [pl.BlockSpec(p.shape, functools.partial(lambda i, nd: (0,) * nd, nd=p.ndim)) for p in arrs]


def _rowwise(fn, rows, aux, params, consts, outs, tile, name):
    S = rows[0][0].shape[0]
    nr, na, npar, nc = len(rows), len(aux), len(params), len(consts)

    def body(*refs):
        ins = [r[...].astype(F32) for r in refs[:nr + na]]
        small = [r[...] for r in refs[nr + na:nr + na + npar + nc]]
        res = fn(*ins, *small)
        for o_ref, r in zip(refs[nr + na + npar + nc:], res):
            o_ref[...] = r.astype(o_ref.dtype)

    return pl.pallas_call(
        body, name=name, grid=(S // tile,),
        out_shape=[jax.ShapeDtypeStruct((S, w), dt) for (w, dt) in outs],
        in_specs=_row_specs(rows + aux, tile) + _full_specs(params + consts),
        out_specs=[pl.BlockSpec((tile, w), lambda i: (i, 0)) for (w, _) in outs],
        compiler_params=pltpu.CompilerParams(dimension_semantics=("parallel",), vmem_limit_bytes=VMEM_LIMIT),
    )(*[v[0] for v in rows + aux], *params, *consts)


def _rowwise_vjp(fn, rows, aux, params, consts, cots, grad_dtypes, tile, name, primal=()):
    S = rows[0][0].shape[0]
    nr, na, npar, nc, nct, npr = len(rows), len(aux), len(params), len(consts), len(cots), len(primal)

    def body(*refs):
        n_in = nr + na + npar + nc + nct
        rv = [r[...].astype(F32) for r in refs[:nr]]
        av = [r[...].astype(F32) for r in refs[nr:nr + na]]
        pv = [r[...] for r in refs[nr + na:nr + na + npar]]
        cv = [r[...] for r in refs[nr + na + npar:nr + na + npar + nc]]
        ct = tuple(r[...].astype(F32) for r in refs[nr + na + npar + nc:n_in])
        res, vjp = jax.vjp(lambda *rp: tuple(fn(*rp[:nr], *av, *rp[nr:], *cv)), *rv, *pv)
        grads = vjp(ct)
        g_refs = refs[n_in:n_in + nr]
        p_refs = refs[n_in + nr:n_in + nr + npar]
        o_refs = refs[n_in + nr + npar:]
        for g_ref, g in zip(g_refs, grads[:nr]):
            g_ref[...] = g.astype(g_ref.dtype)

        @pl.when(pl.program_id(0) == 0)
        def _():
            for p_ref in p_refs:
                p_ref[...] = jnp.zeros_like(p_ref)

        for p_ref, g in zip(p_refs, grads[nr:]):
            p_ref[...] += g
        for o_ref, r in zip(o_refs, res[:npr]):
            o_ref[...] = r.astype(o_ref.dtype)

    out_shape = ([jax.ShapeDtypeStruct((S, w), dt) for (_, w, _), dt in zip(rows, grad_dtypes)]
                 + [jax.ShapeDtypeStruct(p.shape, F32) for p in params]
                 + [jax.ShapeDtypeStruct((S, w), dt) for (w, dt) in primal])
    out_specs = ([pl.BlockSpec((tile, w), lambda i: (i, 0)) for (_, w, _) in rows] + _full_specs(params)
                 + [pl.BlockSpec((tile, w), lambda i: (i, 0)) for (w, _) in primal])
    res = pl.pallas_call(
        body, name=name, grid=(S // tile,), out_shape=out_shape,
        in_specs=_row_specs(rows + aux, tile) + _full_specs(params + consts) + _row_specs(cots, tile),
        out_specs=out_specs,
        compiler_params=pltpu.CompilerParams(dimension_semantics=("arbitrary",), vmem_limit_bytes=VMEM_LIMIT),
    )(*[v[0] for v in rows + aux], *params, *consts, *[v[0] for v in cots])
    return res[:nr], res[nr:nr + npar], res[nr + npar:]


@jax.custom_vjp
def _mm(a, b):
    return _dot(a.astype(BF16), b.astype(BF16))


def _mm_fwd(a, b):
    return _mm(a, b), (a, b)


def _mm_bwd(res, ct):
    a, b = res
    ctb = ct.astype(BF16)
    return _dot_nt(ctb, b.astype(BF16)), _dot_tn(a.astype(BF16), ctb)


_mm.defvjp(_mm_fwd, _mm_bwd)


def _rms(x, g):
    return x * lax.rsqrt(jnp.mean(x * x, axis=-1, keepdims=True) + EPS) * g


def _f_pre(x, g):
    return (_rms(x, g),)


def _f_pre_res(x, g):
    return _rms(x, g), x


def _f_gate(y, z, g):
    return (_rms(y, g) * jax.nn.silu(z),)


def _f_gmlp(u, v, z, g_v, w_s, b_s, g_o):
    groups = w_s.shape[0]
    u, v = jax.nn.gelu(u), jax.nn.gelu(v)
    t_idx = lax.broadcasted_iota(jnp.int32, (LANE, LANE), 0)
    s_idx = lax.broadcasted_iota(jnp.int32, (LANE, LANE), 1)
    ys = []
    for g in range(groups):
        sl = slice(g * LANE, (g + 1) * LANE)
        vn = _rms(v[:, sl], g_v[:, sl])
        w = jnp.where(s_idx <= t_idx, w_s[g], 0.0)
        ys.append(u[:, sl] * (_mm(w, vn) + b_s[g]))
    return (_rms(jnp.concatenate(ys, axis=1), g_o) * jax.nn.silu(z),)


def _rope(x, cos2, sin2, rot):
    return x * cos2 + _mm(x, rot) * sin2


def _f_cpre(cq, ckv, kr, cos2, sin2, g_q, g_kv, rot):
    return _rms(cq, g_q), _rms(ckv, g_kv), _rope(kr, cos2, sin2, rot)


def _f_crope(q, kv, krr, cos2, sin2, rot):
    heads = q.shape[1] // (2 * LANE)
    qs, ks, vs = [], [], []
    for h in range(heads):
        lo, mid, hi = 2 * h * LANE, (2 * h + 1) * LANE, (2 * h + 2) * LANE
        qs += [q[:, lo:mid], _rope(q[:, mid:hi], cos2, sin2, rot)]
        ks += [kv[:, lo:mid], krr]
        vs += [kv[:, mid:hi]]
    return jnp.concatenate(qs, axis=1), jnp.concatenate(ks, axis=1), jnp.concatenate(vs, axis=1)


def _f_final(h, target, g):
    err = _rms(h, g) - target
    return (0.5 * jnp.mean(err * err, axis=-1, keepdims=True),)


def _rope_matrix():
    r = np.zeros((LANE, LANE), np.float32)
    half = ROPE // 2
    for i in range(half):
        r[i + half, i] = -1.0
        r[i, i + half] = 1.0
    return jnp.asarray(r)


def _head_spec(view, rows, n_rows_block):
    _, cb0, w = view
    if n_rows_block:
        return pl.BlockSpec((rows, w), functools.partial(lambda h, i, cb0: (i, cb0 + h), cb0=cb0))
    return pl.BlockSpec((rows, w), functools.partial(lambda h, i, cb0: (0, cb0 + h), cb0=cb0))


def _stat_spec(tq):
    return pl.BlockSpec((1, tq, 1), lambda h, i: (h, i, 0))


def _softplus(z):
    return jnp.maximum(z, 0.0) + jnp.log(1.0 + jnp.exp(-jnp.abs(z)))


def _cumsum_mm(x, m01):
    hi = x.astype(BF16)
    lo = (x - hi.astype(F32)).astype(BF16)
    return _dot(hi, m01) + _dot(lo, m01)


def _attn_call(body, name, heads, S, tq, ins, in_blocked, outs, out_blocked, scratch, stats_in=0, stats_out=0, carry=None):
    in_specs = [_head_spec(v, tq if blk else S, blk) for v, blk in zip(ins[:len(ins) - stats_in], in_blocked)]
    in_specs += [_stat_spec(tq)] * stats_in
    out_specs = [_head_spec((None, 0, w), tq if blk else S, blk) for (w, _), blk in zip(outs, out_blocked)]
    out_specs += [_stat_spec(tq)] * stats_out
    out_shape = [jax.ShapeDtypeStruct((S, heads * w), dt) for (w, dt) in outs]
    out_shape += [jax.ShapeDtypeStruct((heads, S, 1), F32)] * stats_out
    args = [v[0] for v in ins[:len(ins) - stats_in]] + list(ins[len(ins) - stats_in:])
    res, moved = _call(body, name, (heads, S // tq), in_specs, out_specs, out_shape, scratch, ("arbitrary", "arbitrary"),
                       args, carry)
    return res if carry is None else res + [moved]


def _softmax_fwd(q, k, v, heads, scale, name, tq, bk, carry=None):
    S, dv = q[0].shape[0], v[2]

    def body(q_ref, k_ref, v_ref, o_ref, lse_ref):
        qi = pl.program_id(1)
        qv = q_ref[...]
        row = qi * tq + lax.broadcasted_iota(jnp.int32, (tq, bk), 0)
        col0 = lax.broadcasted_iota(jnp.int32, (tq, bk), 1)

        def step(kb, carry):
            m, l, acc = carry
            sl = pl.ds(pl.multiple_of(kb * bk, bk), bk)
            s = _dot_nt(qv, k_ref[sl, :]) * scale
            s = jnp.where(kb * bk + col0 <= row, s, -1e30)
            m_new = jnp.maximum(m, jnp.max(s, axis=1, keepdims=True))
            p = jnp.exp(s - m_new)
            alpha = jnp.exp(m - m_new)
            l = alpha * l + jnp.sum(p, axis=1, keepdims=True)
            acc = alpha * acc + _dot(p.astype(BF16), v_ref[sl, :])
            return m_new, l, acc

        n_kb = (qi * tq + tq + bk - 1) // bk
        m, l, acc = lax.fori_loop(0, n_kb, step, (jnp.full((tq, 1), -1e30, F32), jnp.zeros((tq, 1), F32),
                                                  jnp.zeros((tq, dv), F32)))
        o_ref[...] = (acc / l).astype(o_ref.dtype)
        lse_ref[0] = m + jnp.log(l)

    return _attn_call(body, name, heads, S, tq, [q, k, v], [1, 0, 0], [(dv, BF16)], [1], [], stats_out=1, carry=carry)


def _softmax_bwd(q, k, v, o, do, lse, heads, scale, name, tq, bk, carry=None):
    S, dq_w, dv = q[0].shape[0], q[2], v[2]
    nq = S // tq

    bd = min(DIAG_BLOCK, tq)
    assert tq % bk == 0 and tq % bd == 0

    def body(q_ref, k_ref, v_ref, o_ref, do_ref, lse_ref, dq_ref, dk_ref, dv_ref, dk_acc, dv_acc, delta_scr, dq_scr):
        qi = pl.program_id(1)

        @pl.when(qi == 0)
        def _():
            dk_acc[...] = jnp.zeros_like(dk_acc)
            dv_acc[...] = jnp.zeros_like(dv_acc)

        delta_scr[...] = jnp.sum(do_ref[...].astype(F32) * o_ref[...].astype(F32), axis=1, keepdims=True)
        dq_scr[...] = jnp.zeros_like(dq_scr)

        def block(r0, sl, width, masked):
            qv, dov = q_ref[r0:, :], do_ref[r0:, :]
            ks, vs = k_ref[sl, :], v_ref[sl, :]
            p = jnp.exp(_dot_nt(qv, ks) * scale - lse_ref[0, r0:, :])
            if masked:
                shape = (tq - r0, width)
                p = jnp.where(lax.broadcasted_iota(jnp.int32, shape, 1) <= lax.broadcasted_iota(jnp.int32, shape, 0), p, 0.0)
            ds = (p * (_dot_nt(dov, vs) - delta_scr[r0:, :]) * scale).astype(BF16)
            dk_acc[sl, :] += _dot_tn(ds, qv)
            dv_acc[sl, :] += _dot_tn(p.astype(BF16), dov)
            dq_scr[r0:, :] += _dot(ds, ks)

        def step(kb, _):
            block(0, pl.ds(pl.multiple_of(kb * bk, bk), bk), bk, False)
            return 0

        lax.fori_loop(0, qi * (tq // bk), step, 0)
        for j in range(tq // bd):
            block(j * bd, pl.ds(pl.multiple_of(qi * tq + j * bd, bd), bd), bd, True)
        dq_ref[...] = dq_scr[...].astype(dq_ref.dtype)

        @pl.when(qi == nq - 1)
        def _():
            dk_ref[...] = dk_acc[...].astype(dk_ref.dtype)
            dv_ref[...] = dv_acc[...].astype(dv_ref.dtype)

    return _attn_call(body, name, heads, S, tq, [q, k, v, o, do, lse], [1, 0, 0, 1, 1],
                      [(dq_w, BF16), (dq_w, BF16), (dv, BF16)], [1, 0, 0],
                      [pltpu.VMEM((S, dq_w), F32), pltpu.VMEM((S, dv), F32), pltpu.VMEM((tq, 1), F32),
                       pltpu.VMEM((tq, dq_w), F32)], stats_in=1, carry=carry)


def _stick_fwd(q, k, v, heads, scale, name, tq, bk, carry=None):
    S, dv = q[0].shape[0], v[2]

    assert tq % bk == 0
    n_sub = tq // bk

    def body(q_ref, k_ref, v_ref, o_ref, tot_ref, c_scr, acc_scr):
        qi = pl.program_id(1)
        m_gt = (lax.broadcasted_iota(jnp.int32, (bk, bk), 0) > lax.broadcasted_iota(jnp.int32, (bk, bk), 1)).astype(BF16)

        def block(r0, sl, masked):
            rows = tq - r0
            z = _dot_nt(q_ref[r0:, :], k_ref[sl, :]) * scale
            sp = _softplus(z)
            lk = -sp
            if masked:
                mask = lax.broadcasted_iota(jnp.int32, (rows, bk), 1) < lax.broadcasted_iota(jnp.int32, (rows, bk), 0)
                lk = jnp.where(mask, lk, 0.0)
            after = _cumsum_mm(lk, m_gt) + c_scr[r0:, :]
            a = jnp.exp(z - sp + after)
            if masked:
                a = jnp.where(mask, a, 0.0)
            acc_scr[r0:, :] += _dot(a.astype(BF16), v_ref[sl, :])
            c_scr[r0:, :] += jnp.sum(lk, axis=1, keepdims=True)

        c_scr[...] = jnp.zeros_like(c_scr)
        acc_scr[...] = jnp.zeros_like(acc_scr)
        for j in reversed(range(n_sub)):
            block(j * bk, pl.ds(pl.multiple_of(qi * tq + j * bk, bk), bk), True)

        def step(it, _):
            block(0, pl.ds(pl.multiple_of((qi * n_sub - 1 - it) * bk, bk), bk), False)
            return 0

        lax.fori_loop(0, qi * n_sub, step, 0)
        o_ref[...] = acc_scr[...].astype(o_ref.dtype)
        tot_ref[0] = c_scr[...]

    return _attn_call(body, name, heads, S, tq, [q, k, v], [1, 0, 0], [(dv, BF16)], [1],
                      [pltpu.VMEM((tq, 1), F32), pltpu.VMEM((tq, dv), F32)], stats_out=1, carry=carry)


def _stick_bwd(q, k, v, do, tot, heads, scale, name, tq, bk, carry=None):
    S, dq_w, dv = q[0].shape[0], q[2], v[2]
    nq = S // tq

    assert tq % bk == 0
    n_sub = tq // bk

    def body(q_ref, k_ref, v_ref, do_ref, tot_ref, dq_ref, dk_ref, dv_ref, dk_acc, dv_acc, pc_scr, gc_scr, dq_scr):
        qi = pl.program_id(1)

        @pl.when(qi == 0)
        def _():
            dk_acc[...] = jnp.zeros_like(dk_acc)
            dv_acc[...] = jnp.zeros_like(dv_acc)

        j_idx = lax.broadcasted_iota(jnp.int32, (bk, bk), 0)
        s_idx = lax.broadcasted_iota(jnp.int32, (bk, bk), 1)
        m_le, m_lt = (j_idx <= s_idx).astype(BF16), (j_idx < s_idx).astype(BF16)

        def block(r0, sl, masked):
            rows = tq - r0
            qv, dov = q_ref[r0:, :], do_ref[r0:, :]
            ks, vs = k_ref[sl, :], v_ref[sl, :]
            z = _dot_nt(qv, ks) * scale
            sp = _softplus(z)
            lk = -sp
            if masked:
                mask = lax.broadcasted_iota(jnp.int32, (rows, bk), 1) < lax.broadcasted_iota(jnp.int32, (rows, bk), 0)
                lk = jnp.where(mask, lk, 0.0)
            after = tot_ref[0, r0:, :] - pc_scr[r0:, :] - _cumsum_mm(lk, m_le)
            log_beta = z - sp
            a = jnp.exp(log_beta + after)
            if masked:
                a = jnp.where(mask, a, 0.0)
            g = _dot_nt(dov, vs) * a
            cg = gc_scr[r0:, :] + _cumsum_mm(g, m_lt)
            dz = g * jnp.exp(-sp) - jnp.exp(log_beta) * cg
            if masked:
                dz = jnp.where(mask, dz, 0.0)
            dz = (dz * scale).astype(BF16)
            dk_acc[sl, :] += _dot_tn(dz, qv)
            dv_acc[sl, :] += _dot_tn(a.astype(BF16), dov)
            dq_scr[r0:, :] += _dot(dz, ks)
            pc_scr[r0:, :] += jnp.sum(lk, axis=1, keepdims=True)
            gc_scr[r0:, :] += jnp.sum(g, axis=1, keepdims=True)

        pc_scr[...] = jnp.zeros_like(pc_scr)
        gc_scr[...] = jnp.zeros_like(gc_scr)
        dq_scr[...] = jnp.zeros_like(dq_scr)

        def step(kb, _):
            block(0, pl.ds(pl.multiple_of(kb * bk, bk), bk), False)
            return 0

        lax.fori_loop(0, qi * n_sub, step, 0)
        for j in range(n_sub):
            block(j * bk, pl.ds(pl.multiple_of(qi * tq + j * bk, bk), bk), True)
        dq_ref[...] = dq_scr[...].astype(dq_ref.dtype)

        @pl.when(qi == nq - 1)
        def _():
            dk_ref[...] = dk_acc[...].astype(dk_ref.dtype)
            dv_ref[...] = dv_acc[...].astype(dv_ref.dtype)

    return _attn_call(body, name, heads, S, tq, [q, k, v, do, tot], [1, 0, 0, 1],
                      [(dq_w, BF16), (dq_w, BF16), (dv, BF16)], [1, 0, 0],
                      [pltpu.VMEM((S, dq_w), F32), pltpu.VMEM((S, dv), F32), pltpu.VMEM((tq, 1), F32),
                       pltpu.VMEM((tq, 1), F32), pltpu.VMEM((tq, dq_w), F32)], stats_in=1, carry=carry)


def _adamw(slots, w, m, v, layer, prev, name, col0=0, carry=None):
    _, R, C = slots.shape
    L, full_c = w.shape[0], w.shape[2]
    item = slots.dtype.itemsize
    tc = _tile(C, 2048)
    tr = _tile(R, max(16, ADAM_TILE_BYTES // (item * tc)), mult=16)
    if tr == R and R * tc * item > ADAM_TILE_BYTES:
        tc = _tile(C, max(LANE, ADAM_TILE_BYTES // (item * R)))
    c1, c2 = 1.0 - ADAM_B1 ** ADAM_STEP, 1.0 - ADAM_B2 ** ADAM_STEP
    n_prev = 0 if prev is None else 4

    def body(s_ref, w_ref, m_ref, v_ref, *rest):
        g_out, d_out, m_out, v_out = rest[n_prev:]
        g = s_ref[0].astype(F32)
        for k in range(1, NDEV):
            g = g + s_ref[k].astype(F32)
        m_new = ADAM_B1 * m_ref[0] + (1.0 - ADAM_B1) * g
        v_new = ADAM_B2 * v_ref[0] + (1.0 - ADAM_B2) * (g * g)
        g_out[0] = g
        m_out[0] = m_new
        v_out[0] = v_new
        d_out[0] = -ADAM_LR * ((m_new / c1) / (jnp.sqrt(v_new / c2) + ADAM_EPS) + ADAM_WD * w_ref[0])

    assert col0 % tc == 0
    spec = pl.BlockSpec((1, tr, tc), lambda i, j: (layer, i, j + col0 // tc))
    in_specs = [pl.BlockSpec((NDEV, tr, tc), lambda i, j: (0, i, j)), spec, spec, spec]
    in_specs += [pl.BlockSpec(memory_space=pl.ANY)] * n_prev
    res, moved = _call(body, name, (R // tr, C // tc), in_specs, [spec] * 4, [jax.ShapeDtypeStruct((L, R, full_c), F32)] * 4,
                       [], ("parallel", "parallel"), [slots, w, m, v, *(prev or [])], carry,
                       aliases={4 + i: i for i in range(n_prev)})
    return res if carry is None else (res, moved)


class _Cfg:
    def __init__(self, S, D, groups, q_lora, kv_lora, c_heads, d_mix):
        self.S, self.D, self.G, self.Q, self.KV, self.Hc, self.DMIX = S, D, groups, q_lora, kv_lora, c_heads, d_mix
        self.A, self.C = groups * LANE, c_heads * LANE
        self.B = d_mix - self.A - self.C
        self.Hb = self.B // LANE
        A, B, C = self.A, self.B, self.C
        assert B % LANE == 0 and B % C == 0 and (B + C) % A == 0
        self.ref_segs = [("ua", A), ("va", A), ("za", A), ("qb", B), ("kb", B), ("vb", B), ("zb", B),
                         ("cq", q_lora), ("ckv", kv_lora), ("kr", ROPE), ("zc", C)]
        self.off, off = {}, 0
        for nm, w in [("ua", A), ("va", A), ("za", A), ("qb", B), ("kb", B), ("vb", B), ("zb", B), ("zc", C),
                      ("cq", q_lora), ("kr", LANE), ("ckv", kv_lora)]:
            off = -(-off // w) * w
            self.off[nm] = off
            off += w
        self.NP = -(-off // 512) * 512
        self.width = {"kr": LANE, **{nm: w for nm, w in self.ref_segs if nm != "kr"}}

    def tiles(self, kind, layer):
        tq, bk = ATTN_TILES[kind][layer % len(ATTN_TILES[kind])]
        return min(tq, self.S), min(bk, self.S)

    def view(self, arr, nm):
        w = self.width[nm]
        return (arr, w, self.off[nm] // w)

    def heads_view(self, arr, nm):
        return (arr, self.off[nm] // LANE, LANE)


def _gathered_rows(parts, a, b):
    per = sum(p.shape[1] for p in parts)
    out = []
    while a < b:
        k, r = divmod(a, per)
        i = 0
        while r >= parts[i].shape[1]:
            r -= parts[i].shape[1]
            i += 1
        n = min(b - a, parts[i].shape[1] - r)
        out.append(parts[i][k, r:r + n])
        a += n
    return out


def _pad_w_in(cfg, parts):
    width_d, dtype = parts[0].shape[2], parts[0].dtype
    start_of, start = {}, 0
    for nm, width in cfg.ref_segs:
        start_of[nm] = (start, width)
        start += width
    rows, pos = [], 0
    for nm, off in sorted(cfg.off.items(), key=lambda kv: kv[1]):
        if off > pos:
            rows.append(jnp.zeros((off - pos, width_d), dtype))
        rows += _gathered_rows(parts, start_of[nm][0], start_of[nm][0] + start_of[nm][1])
        pos = off + start_of[nm][1]
    if cfg.NP > pos:
        rows.append(jnp.zeros((cfg.NP - pos, width_d), dtype))
    return jnp.concatenate(rows, axis=0)


def _unpad_w_in(cfg, wpt):
    return jnp.concatenate([wpt[cfg.off[nm]:cfg.off[nm] + width] for nm, width in cfg.ref_segs], axis=0)


def _to_slots_cols(w):
    R = w.shape[0]
    return w.reshape(R, NDEV, -1).transpose(1, 0, 2)


def _from_slots_cols(s):
    return s.transpose(1, 0, 2).reshape(s.shape[1], -1)


def _perm_rows_out(cfg, w):
    return jnp.concatenate([w[cfg.A:], w[:cfg.A]], axis=0)


def _unperm_rows_out(cfg, w):
    return jnp.concatenate([w[cfg.B + cfg.C:], w[:cfg.B + cfg.C]], axis=0)


def _layer_params(cfg, l, g_pre, a_g_v, a_w_s, a_b_s, c_g_q, c_g_kv, g_out):
    A, B = cfg.A, cfg.B
    return dict(g_pre=g_pre[l][None], g_v=a_g_v[l].reshape(1, A), w_s=a_w_s[l], b_s=a_b_s[l][:, :, None],
                g_q=c_g_q[l][None], g_kv=c_g_kv[l][None],
                g_oa=g_out[l][None, :A], g_ob=g_out[l][None, A:A + B], g_oc=g_out[l][None, A + B:])


def _layer_fwd(cfg, l, x, W, p, cos2, sin2, rot, carry_in=None, carry_stick=None, carry_mla=None):
    S, D, A, B, C = cfg.S, cfg.D, cfg.A, cfg.B, cfg.C
    tag = f"l{l}"
    (h,) = _rowwise(_f_pre, [(x, D, 0)], [], [p["g_pre"]], [], [(D, BF16)], 256, f"pre_{tag}")
    if carry_in is None:
        proj = _matmul(h, W["in"], "nt", BF16, f"mm_in_{tag}")
    else:
        proj, moved_in = _matmul(h, W["in"], "nt", BF16, f"mm_in_{tag}", carry=carry_in[0])
        carry_in[1](moved_in)
    a_rows = [cfg.view(proj, "ua"), cfg.view(proj, "va"), cfg.view(proj, "za")]
    a_par = [p["g_v"], p["w_s"], p["b_s"], p["g_oa"]]
    (ya,) = _rowwise(_f_gmlp, a_rows, [], a_par, [], [(A, BF16)], LANE, f"gmlp_{tag}")
    qb, kb, vb = cfg.heads_view(proj, "qb"), cfg.heads_view(proj, "kb"), cfg.heads_view(proj, "vb")
    yb, tot, *moved_stick = _stick_fwd(qb, kb, vb, cfg.Hb, LANE ** -0.5, f"stick_fwd_{tag}", *cfg.tiles("stick_fwd", l),
                                       carry=carry_stick)
    (ybg,) = _rowwise(_f_gate, [(yb, B, 0), cfg.view(proj, "zb")], [], [p["g_ob"]], [], [(B, BF16)], 256, f"gate_b_{tag}")
    c_rows = [cfg.view(proj, "cq"), cfg.view(proj, "ckv"), cfg.view(proj, "kr")]
    trig = [(cos2, LANE, 0), (sin2, LANE, 0)]
    cqn, ckvn, krr = _rowwise(_f_cpre, c_rows, trig, [p["g_q"], p["g_kv"]], [rot],
                              [(cfg.Q, BF16), (cfg.KV, BF16), (LANE, BF16)], 256, f"cpre_{tag}")
    q_raw = _matmul(cqn, W["uq"], "nt", BF16, f"mm_uq_{tag}")
    kv = _matmul(ckvn, W["ukv"], "nn", BF16, f"mm_ukv_{tag}")
    r_rows = [(q_raw, 2 * C, 0), (kv, 2 * C, 0), (krr, LANE, 0)]
    q_rot, k_full, v_c = _rowwise(_f_crope, r_rows, trig, [], [rot], [(2 * C, BF16), (2 * C, BF16), (C, BF16)], 128,
                                  f"crope_{tag}")
    qc, kc, vc = (q_rot, 0, 2 * LANE), (k_full, 0, 2 * LANE), (v_c, 0, LANE)
    yc, lse, *moved_mla = _softmax_fwd(qc, kc, vc, cfg.Hc, (LANE + ROPE) ** -0.5, f"mla_fwd_{tag}", *cfg.tiles("mla_fwd", l),
                                       carry=carry_mla)
    (ycg,) = _rowwise(_f_gate, [(yc, C, 0), cfg.view(proj, "zc")], [], [p["g_oc"]], [], [(C, BF16)], 256, f"gate_c_{tag}")
    y = jnp.concatenate([ybg, ycg, ya], axis=1)
    out = _matmul(y, W["out"], "nn", F32, f"mm_out_{tag}", add=x)
    saved = dict(x=x, h=h, proj=proj, yb=yb, tot=tot, cqn=cqn, ckvn=ckvn, krr=krr, q_raw=q_raw, kv=kv,
                 q_rot=q_rot, k_full=k_full, v_c=v_c, yc=yc, lse=lse, y=y)
    return out, saved, (moved_stick[0] if moved_stick else []), (moved_mla[0] if moved_mla else [])


def _layer_bwd(cfg, l, dout, sv, W, p, cos2, sin2, rot, ext_stick, ext_mla, last):
    S, D, A, B, C = cfg.S, cfg.D, cfg.A, cfg.B, cfg.C
    tag = f"l{l}"
    proj = sv["proj"]
    dy = _matmul(dout, W["out"], "nt", BF16, f"mm_dy_{tag}")
    d_wout = _matmul(sv["y"], dout, "tn", BF16, f"mm_dwout_{tag}")
    wout_slots = _unperm_rows_out(cfg, d_wout).reshape(NDEV, cfg.DMIX // NDEV, D)
    (dyb, dzb), (dg_ob,), _ = _rowwise_vjp(_f_gate, [(sv["yb"], B, 0), cfg.view(proj, "zb")], [], [p["g_ob"]], [],
                                           [(dy, B, 0)], [BF16, BF16], 256, f"gate_b_bwd_{tag}")
    (dyc, dzc), (dg_oc,), _ = _rowwise_vjp(_f_gate, [(sv["yc"], C, 0), cfg.view(proj, "zc")], [], [p["g_oc"]], [],
                                           [(dy, C, B // C)], [BF16, BF16], 256, f"gate_c_bwd_{tag}")
    a_rows = [cfg.view(proj, "ua"), cfg.view(proj, "va"), cfg.view(proj, "za")]
    a_par = [p["g_v"], p["w_s"], p["b_s"], p["g_oa"]]
    (dua, dva, dza), (dg_v, dw_s, db_s, dg_oa), _ = _rowwise_vjp(
        _f_gmlp, a_rows, [], a_par, [], [(dy, A, (B + C) // A)], [BF16] * 3, LANE, f"gmlp_bwd_{tag}")
    qb, kb, vb = cfg.heads_view(proj, "qb"), cfg.heads_view(proj, "kb"), cfg.heads_view(proj, "vb")
    dqb, dkb, dvb, moved_stick = _stick_bwd(qb, kb, vb, (dyb, 0, LANE), sv["tot"], cfg.Hb, LANE ** -0.5,
                                            f"stick_bwd_{tag}", *cfg.tiles("stick_bwd", l),
                                            carry=_Exchange([[a] for a in (ext_stick or [wout_slots])], False))
    ext_got = [mv[0] for mv in moved_stick] if ext_stick else []
    ext_mla = ext_mla + ([wout_slots] if ext_stick else [])
    qc, kc, vc = (sv["q_rot"], 0, 2 * LANE), (sv["k_full"], 0, 2 * LANE), (sv["v_c"], 0, LANE)
    dq_rot, dk_full, dv_c, *moved_mla = _softmax_bwd(qc, kc, vc, (sv["yc"], 0, LANE), (dyc, 0, LANE), sv["lse"], cfg.Hc,
                                                     (LANE + ROPE) ** -0.5, f"mla_bwd_{tag}", *cfg.tiles("mla_bwd", l),
                                                     carry=_Exchange([[a] for a in ext_mla], False) if ext_mla else None)
    moved_mla = [mv[0] for mv in (moved_mla[0] if moved_mla else [])]
    got = dict(w_out=moved_mla.pop() if ext_stick else moved_stick[0][0])
    ext_got += moved_mla
    trig = [(cos2, LANE, 0), (sin2, LANE, 0)]
    r_rows = [(sv["q_raw"], 2 * C, 0), (sv["kv"], 2 * C, 0), (sv["krr"], LANE, 0)]
    (dq_raw, dkv, dkrr), _, _ = _rowwise_vjp(_f_crope, r_rows, trig, [], [rot],
                                             [(dq_rot, 2 * C, 0), (dk_full, 2 * C, 0), (dv_c, C, 0)], [BF16] * 3, 128,
                                             f"crope_bwd_{tag}")
    dcqn = _matmul(dq_raw, W["uq"], "nn", BF16, f"mm_dcq_{tag}")
    d_wuq = _matmul(dq_raw, sv["cqn"], "tn", BF16, f"mm_dwuq_{tag}")
    dckvn = _matmul(dkv, W["ukv"], "nt", BF16, f"mm_dckv_{tag}")
    d_wukv = _matmul(sv["ckvn"], dkv, "tn", BF16, f"mm_dwukv_{tag}")
    c_rows = [cfg.view(proj, "cq"), cfg.view(proj, "ckv"), cfg.view(proj, "kr")]
    (dcq, dckv, dkr), (dg_q, dg_kv), _ = _rowwise_vjp(
        _f_cpre, c_rows, trig, [p["g_q"], p["g_kv"]], [rot],
        [(dcqn, cfg.Q, 0), (dckvn, cfg.KV, 0), (dkrr, LANE, 0)], [BF16] * 3, 256, f"cpre_bwd_{tag}")
    parts = dict(ua=dua, va=dva, za=dza, qb=dqb, kb=dkb, vb=dvb, zb=dzb, zc=dzc, cq=dcq, kr=dkr, ckv=dckv)
    cols, pos = [], 0
    for nm, off in sorted(cfg.off.items(), key=lambda kv_: kv_[1]):
        if off > pos:
            cols.append(jnp.zeros((S, off - pos), BF16))
        cols.append(parts[nm])
        pos = off + parts[nm].shape[1]
    if cfg.NP > pos:
        cols.append(jnp.zeros((S, cfg.NP - pos), BF16))
    dproj = jnp.concatenate(cols, axis=1)
    to_send = dict(c_w_uq=d_wuq.reshape(cfg.Hc, 2 * LANE, cfg.Q)[:, :LANE + ROPE].reshape(NDEV, -1, cfg.Q),
                   c_w_ukv=_to_slots_cols(d_wukv))
    def d_win(c0, width, name, carry=None):
        res = _matmul(dproj, sv["h"], "tn", BF16, name, b_cols=(c0, width), carry=carry)
        wt, moved = res if carry is not None else (res, None)
        return _unpad_w_in(cfg, wt).reshape(NDEV, -1, width), moved

    if last:
        ranges = LAST_W_IN_RANGES if D % (sum(LAST_W_IN_RANGES) * LANE) == 0 else (1, 1)
        unit = D // sum(ranges)
        arrived, c0, sending = [], 0, None
        for i, r in enumerate(ranges):
            carry = _Exchange([[sending]], False) if sending is not None else None
            sending, moved = d_win(c0, r * unit, f"mm_dwin_{i}_{tag}", carry)
            if moved is not None:
                arrived.append(moved[0][0])
            c0 += r * unit
        dh, moved = _matmul(dproj, W["in"], "nn", BF16, f"mm_dh_{tag}",
                            carry=_Exchange([[sending], [to_send["c_w_uq"]], [to_send["c_w_ukv"]]], False))
        got.update(w_in=tuple(arrived) + (moved[0][0],), c_w_uq=moved[1][0], c_w_ukv=moved[2][0])
        to_send = {}
    else:
        first = D // 2 if (D // 4) % LANE else D // 4
        slots_a, _ = d_win(0, first, f"mm_dwin_a_{tag}")
        dh, moved_a = _matmul(dproj, W["in"], "nn", BF16, f"mm_dh_{tag}", carry=_Exchange([[slots_a]], False))
        got["w_in"] = (moved_a[0][0],)
        to_send["w_in"], _ = d_win(first, D - first, f"mm_dwin_b_{tag}")
    (dx,), (dg_pre,), _ = _rowwise_vjp(_f_pre_res, [(sv["x"], D, 0)], [], [p["g_pre"]], [],
                                       [(dh, D, 0), (dout, D, 0)], [F32], 128, f"pre_bwd_{tag}")
    small = dict(g_pre=dg_pre[0], a_g_v=dg_v.reshape(cfg.G, LANE), a_w_s=dw_s, a_b_s=db_s[:, :, 0], c_g_q=dg_q[0],
                 c_g_kv=dg_kv[0], g_out=jnp.concatenate([dg_oa[0], dg_ob[0], dg_oc[0]]))
    return dx, small, got, to_send, ext_got


def _pack_small(vals):
    pieces = []
    for nm in SMALL:
        piece = vals[nm].reshape(-1, LANE)
        pieces.append(jnp.pad(piece, ((0, -piece.shape[0] % 8), (0, 0))))
    packed = jnp.concatenate(pieces, axis=0)
    return jnp.pad(packed, ((0, -packed.shape[0] % SMALL_ROWS), (0, 0)))


def _unpack_small(packed, like):
    out, row = {}, 0
    for nm in SMALL:
        n = like[nm].size // LANE
        out[nm] = packed[row:row + n].reshape(like[nm].shape)
        row += n + (-n % 8)
    return out


def kernel(x, positions, g_pre, w_in, a_g_v, a_w_s, a_b_s, c_g_q, c_g_kv, c_w_uq, c_w_ukv, g_out, w_out, g_final, loss_target, m_g_pre, m_w_in, m_a_g_v, m_a_w_s, m_a_b_s, m_c_g_q, m_c_g_kv, m_c_w_uq, m_c_w_ukv, m_g_out, m_w_out, m_g_final, v_g_pre, v_w_in, v_a_g_v, v_a_w_s, v_a_b_s, v_c_g_q, v_c_g_kv, v_c_w_uq, v_c_w_ukv, v_g_out, v_w_out, v_g_final):
    depth, S, D = w_in.shape[0], x.shape[1], x.shape[2]
    cfg = _Cfg(S, D, a_g_v.shape[1], c_g_q.shape[1], c_g_kv.shape[1], c_w_ukv.shape[2] * NDEV // (2 * LANE), g_out.shape[1])
    weights = dict(g_pre=g_pre, w_in=w_in, a_g_v=a_g_v, a_w_s=a_w_s, a_b_s=a_b_s, c_g_q=c_g_q, c_g_kv=c_g_kv,
                   c_w_uq=c_w_uq, c_w_ukv=c_w_ukv, g_out=g_out, w_out=w_out, g_final=g_final)
    mom_m = dict(g_pre=m_g_pre, w_in=m_w_in, a_g_v=m_a_g_v, a_w_s=m_a_w_s, a_b_s=m_a_b_s, c_g_q=m_c_g_q, c_g_kv=m_c_g_kv,
                 c_w_uq=m_c_w_uq, c_w_ukv=m_c_w_ukv, g_out=m_g_out, w_out=m_w_out, g_final=m_g_final)
    mom_v = dict(g_pre=v_g_pre, w_in=v_w_in, a_g_v=v_a_g_v, a_w_s=v_a_w_s, a_b_s=v_a_b_s, c_g_q=v_c_g_q, c_g_kv=v_c_g_kv,
                 c_w_uq=v_c_w_uq, c_w_ukv=v_c_w_ukv, g_out=v_g_out, w_out=v_w_out, g_final=v_g_final)
    big_names = ("w_in", "c_w_uq", "c_w_ukv", "w_out")

    inv_freq = 1.0 / (ROPE_THETA ** (jnp.arange(0, ROPE, 2, dtype=F32) / ROPE))
    ang = positions[0].astype(F32)[:, None] * inv_freq
    zpad = jnp.zeros((S, LANE - ROPE), F32)
    cos2 = jnp.concatenate([jnp.cos(ang), jnp.cos(ang), zpad], axis=1)
    sin2 = jnp.concatenate([jnp.sin(ang), jnp.sin(ang), zpad], axis=1)
    rot = _rope_matrix()

    for tree in (weights, mom_m, mom_v):
        for nm in TRANSPOSED:
            tree[nm] = jnp.swapaxes(tree[nm], 1, 2)

    def shards(l, names):
        return [[weights[nm][l].astype(BF16)] for nm in names]

    def assemble_rest(g_uq, g_ukv, g_wout):
        uq = jnp.pad(g_uq[0].reshape(cfg.Hc, LANE + ROPE, cfg.Q), ((0, 0), (0, LANE - ROPE), (0, 0)))
        return {"uq": uq.reshape(2 * cfg.C, cfg.Q), "ukv": _from_slots_cols(g_ukv[0]),
                "out": _perm_rows_out(cfg, g_wout[0].reshape(cfg.DMIX, D))}

    params = [_layer_params(cfg, l, g_pre, a_g_v, a_w_s, a_b_s, c_g_q, c_g_kv, g_out) for l in range(depth)]

    in_parts = [g[0] for g in _exchange(_GatherTwoLevel(shards(0, big_names[:1])), "gather_w_in_l0")]
    got_rest = None
    hcur, saved, Ws = x[0], [], []
    for l in range(depth):
        Ws.append({"in": _pad_w_in(cfg, in_parts)})
        if got_rest is not None:
            Ws[l].update(assemble_rest(*got_rest))
        nxt = l + 1 < depth
        early_rows = min(W_IN_EARLY_ROWS, weights["w_in"].shape[1] // 2)
        riding = ([] if got_rest is not None else shards(l, big_names[1:]))
        riding += [[weights["w_in"][l + 1][:early_rows].astype(BF16)]] if nxt else []
        in_parts = []

        def take(moved, l=l, rest_here=got_rest is None, nxt=nxt):
            if rest_here:
                Ws[l].update(assemble_rest(*moved[:3]))
            if nxt:
                in_parts.append(moved[-1][0])

        hcur, sv, got_late, got_rest = _layer_fwd(
            cfg, l, hcur, Ws[l], params[l], cos2, sin2, rot,
            carry_in=(_GatherTwoLevel(riding), take) if riding else None,
            carry_stick=_GatherTwoLevel([[weights["w_in"][l + 1][early_rows:].astype(BF16)]]) if nxt else None,
            carry_mla=_GatherTwoLevel(shards(l + 1, big_names[1:])) if nxt else None)
        in_parts += [g[0] for g in got_late]
        saved.append(sv)
    (dh,), (dg_final,), (loss_rows,) = _rowwise_vjp(
        _f_final, [(hcur, D, 0)], [(loss_target[0], D, 0)], [g_final[None]], [], [(jnp.ones((S, 1), F32), 1, 0)],
        [F32], 128, "final", primal=[(1, F32)])
    loss = lax.psum(jnp.sum(loss_rows), MESH_AXES)

    small_g, slots, pending = [None] * depth, [None] * depth, {}
    for l in reversed(range(depth)):
        ext_stick = [pending["w_in"]] if pending else []
        ext_mla = [pending["c_w_uq"], pending["c_w_ukv"]] if pending else []
        dh, small_g[l], slots[l], pending, ext_got = _layer_bwd(cfg, l, dh, saved[l], Ws[l], params[l], cos2, sin2, rot,
                                                                ext_stick, ext_mla, l == 0)
        if ext_got:
            slots[l + 1].update(w_in=slots[l + 1]["w_in"] + (ext_got[0],), c_w_uq=ext_got[1], c_w_ukv=ext_got[2])
    grad_x = dh[None]
    small_grads = {nm: jnp.stack([small_g[l][nm] for l in range(depth)]) for nm in SMALL if nm != "g_final"}
    small_grads["g_final"] = dg_final[0]
    (small_slots,) = _exchange(_GatherTwoLevel([[_pack_small(small_grads)]]), "gather_small_grads")

    res = {}
    for nm in big_names:
        res[nm] = None
        for l in reversed(range(depth)):
            parts = slots[l][nm] if isinstance(slots[l][nm], tuple) else (slots[l][nm],)
            col0 = 0
            for i, part in enumerate(parts):
                res[nm] = _adamw(part, weights[nm], mom_m[nm], mom_v[nm], l, res[nm], f"adamw_{nm}_l{l}_{i}", col0)
                col0 += part.shape[2]
        if nm in TRANSPOSED:
            res[nm] = [jnp.swapaxes(r, 1, 2) for r in res[nm]]
    packed = _adamw(small_slots[0], _pack_small(weights)[None], _pack_small(mom_m)[None], _pack_small(mom_v)[None], 0, None,
                    "adamw_small")
    small_res = [_unpack_small(r[0], weights) for r in packed]
    order = ("g_pre", "w_in", "a_g_v", "a_w_s", "a_b_s", "c_g_q", "c_g_kv", "c_w_uq", "c_w_ukv", "g_out", "w_out", "g_final")
    outs = [loss, grad_x]
    for kind in range(4):
        outs += [small_res[kind][nm] if nm in SMALL else res[nm][kind] for nm in order]
    return tuple(outs)
```

```python
import functools
import math

import numpy as np
import jax
import jax.numpy as jnp
from jax import lax
from jax.experimental import pallas as pl
from jax.experimental.pallas import tpu as pltpu

NDEV = 8
MESH_AXES = ("x", "y", "c")
LANE = 128
ROPE = 64
EPS = 1e-6
ROPE_THETA = 10000.0
ADAM_LR, ADAM_B1, ADAM_B2, ADAM_EPS, ADAM_WD, ADAM_STEP = 0.001, 0.9, 0.999, 1e-08, 0.01, 10
VMEM_LIMIT = 48 * 1024 * 1024
ADAM_TILE_BYTES = 768 * 1024
LAST_W_IN_RANGES = (1, 2, 2, 3)
W_IN_EARLY_ROWS = 512
CARRY_MID_PERCENT = 88
SMALL_ROWS = 256
ATTN_TILES = {"stick_fwd": [(2048, 256)], "stick_bwd": [(2048, 256)], "mla_fwd": [(512, 1024)], "mla_bwd": [(2048, 512)]}
DIAG_BLOCK = 256
F32, BF16 = jnp.float32, jnp.bfloat16
SMALL = ("g_pre", "a_g_v", "a_w_s", "a_b_s", "c_g_q", "c_g_kv", "g_out", "g_final")
TRANSPOSED = ("w_in", "c_w_uq")


def _tile(dim, cap, mult=LANE):
    if dim <= cap:
        return dim
    t = (cap // mult) * mult
    while t >= mult:
        if dim % t == 0:
            return t
        t -= mult
    return dim


def _dot_nt(a, b):
    return lax.dot_general(a, b, (((1,), (1,)), ((), ())), preferred_element_type=F32)


def _dot_tn(a, b):
    return lax.dot_general(a, b, (((0,), (0,)), ((), ())), preferred_element_type=F32)


def _dot(a, b):
    return jnp.dot(a, b, preferred_element_type=F32)


class _Exchange:
    def __init__(self, groups, gather, rots=None):
        self.groups, self.gather = groups, gather
        self.rots = rots or [0] * len(groups)
        assert not (gather and any(self.rots)) or type(self) is not _Exchange
        self.flat = [(gi, li, a) for gi, grp in enumerate(groups) for li, a in enumerate(grp)]
        self.n = len(self.flat)
        self.args = [a for (_, _, a) in self.flat]
        self.out_shape = [jax.ShapeDtypeStruct((len(grp), NDEV) + tuple(grp[0].shape[-2:]), grp[0].dtype) for grp in groups]
        self.scratch = [pltpu.SemaphoreType.DMA((self.n, NDEV - 1)), pltpu.SemaphoreType.DMA((self.n, NDEV - 1)),
                        pltpu.SemaphoreType.DMA((self.n,))]

    def _copies(self, ins, outs, send_sems, recv_sems, local_sems, landings):
        x, y, c = lax.axis_index("x"), lax.axis_index("y"), lax.axis_index("c")
        me = 4 * x + 2 * y + c
        def part_for(dev, gi):
            return (dev + NDEV - self.rots[gi]) % NDEV

        owns = [pltpu.make_async_copy(ins[i] if self.gather else ins[i].at[part_for(me, gi)], outs[gi].at[li, me],
                                      local_sems.at[i]) for i, (gi, li, _) in enumerate(self.flat)]
        pairs = []
        for k in range(1, NDEV):
            px = 1 - x if k & 4 else x
            py = 1 - y if k & 2 else y
            pc = 1 - c if k & 1 else c
            peer = 4 * px + 2 * py + pc
            for i, (gi, li, _) in enumerate(self.flat):
                src = ins[i] if self.gather else ins[i].at[part_for(peer, gi)]
                sems = dict(send_sem=send_sems.at[i, k - 1], recv_sem=recv_sems.at[i, k - 1],
                            device_id=(px, py, pc), device_id_type=pl.DeviceIdType.MESH)
                out = pltpu.make_async_remote_copy(src_ref=src, dst_ref=outs[gi].at[li, me], **sems)
                landing = pltpu.make_async_remote_copy(src_ref=src, dst_ref=outs[gi].at[li, peer], **sems) if landings else None
                pairs.append((out, landing))
        return owns, pairs

    def start(self, ins, outs, sems):
        owns, pairs = self._copies(ins, outs, *sems, landings=False)
        for own in owns:
            own.start()
        for out, _ in pairs:
            out.start()

    def mid(self, ins, outs, sems):
        pass

    def wait(self, ins, outs, sems):
        owns, pairs = self._copies(ins, outs, *sems, landings=True)
        for out, landing in pairs:
            out.wait_send()
            landing.wait_recv()
        for own in owns:
            own.wait()


class _GatherTwoLevel(_Exchange):
    def __init__(self, groups, rots=None):
        super().__init__(groups, True, rots)

    def _copy(self, i, k, ins, outs, send_sems, recv_sems, landing):
        gi, li, _ = self.flat[i]
        x, y, c = lax.axis_index("x"), lax.axis_index("y"), lax.axis_index("c")
        chips = [(x, y), (1 - x, y), (x, 1 - y), (1 - x, 1 - y)]

        def slot(chip, core):
            return outs[gi].at[li, (4 * chip[0] + 2 * chip[1] + core + NDEV - self.rots[gi]) % NDEV]

        if k == 0:
            to, src, dst, lands = (x, y, 1 - c), ins[i], slot(chips[0], c), slot(chips[0], 1 - c)
        elif k <= 3:
            to, src, dst, lands = (*chips[k], c), ins[i], slot(chips[0], c), slot(chips[k], c)
        else:
            to, src, dst, lands = (x, y, 1 - c), slot(chips[k - 3], c), slot(chips[k - 3], c), slot(chips[k - 3], 1 - c)
        return pltpu.make_async_remote_copy(src_ref=src, dst_ref=lands if landing else dst, send_sem=send_sems.at[i, k],
                                            recv_sem=recv_sems.at[i, k], device_id=to, device_id_type=pl.DeviceIdType.MESH)

    def _own(self, i, ins, outs, local_sems):
        gi, li, _ = self.flat[i]
        me = 4 * lax.axis_index("x") + 2 * lax.axis_index("y") + lax.axis_index("c")
        return pltpu.make_async_copy(ins[i], outs[gi].at[li, (me + NDEV - self.rots[gi]) % NDEV], local_sems.at[i])

    def start(self, ins, outs, sems):
        send_sems, recv_sems, local_sems = sems
        for i in range(self.n):
            self._own(i, ins, outs, local_sems).start()
        for k in range(4):
            for i in range(self.n):
                self._copy(i, k, ins, outs, send_sems, recv_sems, False).start()

    def mid(self, ins, outs, sems):
        send_sems, recv_sems, _ = sems
        for k in range(1, 4):
            for i in range(self.n):
                self._copy(i, k, ins, outs, send_sems, recv_sems, True).wait_recv()
                self._copy(i, k + 3, ins, outs, send_sems, recv_sems, False).start()

    def wait(self, ins, outs, sems):
        send_sems, recv_sems, local_sems = sems
        for k in (0, 4, 5, 6):
            for i in range(self.n):
                self._copy(i, k, ins, outs, send_sems, recv_sems, True).wait_recv()
        for k in range(NDEV - 1):
            for i in range(self.n):
                self._copy(i, k, ins, outs, send_sems, recv_sems, False).wait_send()
        for i in range(self.n):
            self._own(i, ins, outs, local_sems).wait()


def _call(body, name, grid, in_specs, out_specs, out_shape, scratch, semantics, args, carry=None, aliases=None):
    n_in, n_out, n_scr = len(in_specs), len(out_specs), len(scratch)
    if carry is None:
        run = body
    else:
        semantics = ("arbitrary",) * len(grid)
        anyspec = pl.BlockSpec(memory_space=pl.ANY)
        in_specs = list(in_specs) + [anyspec] * carry.n
        out_specs = list(out_specs) + [anyspec] * len(carry.groups)
        out_shape = list(out_shape) + carry.out_shape
        scratch = list(scratch) + carry.scratch
        args = list(args) + carry.args

        def run(*refs):
            c_in, x_in = refs[:n_in], refs[n_in:n_in + carry.n]
            rest = refs[n_in + carry.n:]
            c_out, x_out = rest[:n_out], rest[n_out:n_out + len(carry.groups)]
            c_scr, sems = rest[n_out + len(carry.groups):len(rest) - 3], rest[len(rest) - 3:]
            step, total = 0, 1
            for d, extent in enumerate(grid):
                step = step * extent + pl.program_id(d)
                total *= extent

            @pl.when(step == 0)
            def _():
                carry.start(x_in, x_out, sems)

            body(*c_in, *c_out, *c_scr)

            @pl.when(step == (total * CARRY_MID_PERCENT) // 100)
            def _():
                carry.mid(x_in, x_out, sems)

            @pl.when(step == total - 1)
            def _():
                carry.wait(x_in, x_out, sems)

    res = pl.pallas_call(
        run, name=name, grid=grid, out_shape=list(out_shape), in_specs=list(in_specs), out_specs=list(out_specs),
        scratch_shapes=list(scratch), input_output_aliases=aliases or {},
        compiler_params=pltpu.CompilerParams(dimension_semantics=semantics, vmem_limit_bytes=VMEM_LIMIT,
                                             has_side_effects=carry is not None),
    )(*args)
    return list(res[:n_out]), list(res[n_out:])


def _exchange(ex, name):
    groups = ex.groups

    def body(*refs):
        ins, outs, sems = refs[:ex.n], refs[ex.n:ex.n + len(groups)], refs[ex.n + len(groups):]
        ex.start(ins, outs, sems)
        ex.mid(ins, outs, sems)
        ex.wait(ins, outs, sems)

    anyspec = pl.BlockSpec(memory_space=pl.ANY)
    return pl.pallas_call(
        body, name=name, out_shape=ex.out_shape, in_specs=[anyspec] * ex.n, out_specs=[anyspec] * len(groups),
        scratch_shapes=ex.scratch, compiler_params=pltpu.CompilerParams(has_side_effects=True),
    )(*ex.args)


def _matmul(a, b, mode, out_dtype, name, add=None, tm=1024, tn=1024, tk=2048, carry=None, b_cols=None):
    if mode == "tn":
        (K, M), (K2, N) = a.shape, b.shape
    elif mode == "nt":
        (M, K), (N, K2) = a.shape, b.shape
    else:
        (M, K), (K2, N) = a.shape, b.shape
    assert K == K2, (a.shape, b.shape, mode)
    col0 = 0
    if b_cols is not None:
        assert mode != "nt"
        col0, N = b_cols
        tn = min(tn, math.gcd(col0, N))
    tm, tn, tk = _tile(M, tm), _tile(N, tn), _tile(K, tk)
    assert col0 % tn == 0
    nk, jb = K // tk, col0 // tn
    a_spec = pl.BlockSpec((tk, tm), lambda i, j, k: (k, i)) if mode == "tn" else pl.BlockSpec((tm, tk), lambda i, j, k: (i, k))
    b_spec = pl.BlockSpec((tn, tk), lambda i, j, k: (j, k)) if mode == "nt" else pl.BlockSpec((tk, tn), lambda i, j, k: (k, j + jb))
    dot = {"nn": _dot, "nt": _dot_nt, "tn": _dot_tn}[mode]
    has_add = add is not None

    def body(*refs):
        a_ref, b_ref = refs[0], refs[1]
        o_ref, acc = refs[-2], refs[-1]
        k = pl.program_id(2)

        @pl.when(k == 0)
        def _():
            acc[...] = jnp.zeros_like(acc)

        acc[...] += dot(a_ref[...].astype(BF16), b_ref[...].astype(BF16))

        @pl.when(k == nk - 1)
        def _():
            r = acc[...]
            if has_add:
                r = r + refs[2][...]
            o_ref[...] = r.astype(o_ref.dtype)

    in_specs = [a_spec, b_spec]
    args = [a, b]
    if has_add:
        in_specs.append(pl.BlockSpec((tm, tn), lambda i, j, k: (i, j)))
        args.append(add)
    (out,), moved = _call(body, name, (M // tm, N // tn, nk), in_specs, [pl.BlockSpec((tm, tn), lambda i, j, k: (i, j))],
                          [jax.ShapeDtypeStruct((M, N), out_dtype)], [pltpu.VMEM((tm, tn), F32)],
                          ("parallel", "parallel", "arbitrary"), args, carry)
    return out if carry is None else (out, moved)


def _row_specs(views, tile):
    return [pl.BlockSpec((tile, w), functools.partial(lambda i, cb: (i, cb), cb=cb)) for (_, w, cb) in views]


def _full_specs(arrs):
    return [pl.BlockSpec(p.shape, functools.partial(lambda i, nd: (0,) * nd, nd=p.ndim)) for p in arrs]


def _rowwise(fn, rows, aux, params, consts, outs, tile, name):
    S = rows[0][0].shape[0]
    nr, na, npar, nc = len(rows), len(aux), len(params), len(consts)

    def body(*refs):
        ins = [r[...].astype(F32) for r in refs[:nr + na]]
        small = [r[...] for r in refs[nr + na:nr + na + npar + nc]]
        res = fn(*ins, *small)
        for o_ref, r in zip(refs[nr + na + npar + nc:], res):
            o_ref[...] = r.astype(o_ref.dtype)

    return pl.pallas_call(
        body, name=name, grid=(S // tile,),
        out_shape=[jax.ShapeDtypeStruct((S, w), dt) for (w, dt) in outs],
        in_specs=_row_specs(rows + aux, tile) + _full_specs(params + consts),
        out_specs=[pl.BlockSpec((tile, w), lambda i: (i, 0)) for (w, _) in outs],
        compiler_params=pltpu.CompilerParams(dimension_semantics=("parallel",), vmem_limit_bytes=VMEM_LIMIT),
    )(*[v[0] for v in rows + aux], *params, *consts)


def _rowwise_vjp(fn, rows, aux, params, consts, cots, grad_dtypes, tile, name, primal=()):
    S = rows[0][0].shape[0]
    nr, na, npar, nc, nct, npr = len(rows), len(aux), len(params), len(consts), len(cots), len(primal)

    def body(*refs):
        n_in = nr + na + npar + nc + nct
        rv = [r[...].astype(F32) for r in refs[:nr]]
        av = [r[...].astype(F32) for r in refs[nr:nr + na]]
        pv = [r[...] for r in refs[nr + na:nr + na + npar]]
        cv = [r[...] for r in refs[nr + na + npar:nr + na + npar + nc]]
        ct = tuple(r[...].astype(F32) for r in refs[nr + na + npar + nc:n_in])
        res, vjp = jax.vjp(lambda *rp: tuple(fn(*rp[:nr], *av, *rp[nr:], *cv)), *rv, *pv)
        grads = vjp(ct)
        g_refs = refs[n_in:n_in + nr]
        p_refs = refs[n_in + nr:n_in + nr + npar]
        o_refs = refs[n_in + nr + npar:]
        for g_ref, g in zip(g_refs, grads[:nr]):
            g_ref[...] = g.astype(g_ref.dtype)

        @pl.when(pl.program_id(0) == 0)
        def _():
            for p_ref in p_refs:
                p_ref[...] = jnp.zeros_like(p_ref)

        for p_ref, g in zip(p_refs, grads[nr:]):
            p_ref[...] += g
        for o_ref, r in zip(o_refs, res[:npr]):
            o_ref[...] = r.astype(o_ref.dtype)

    out_shape = ([jax.ShapeDtypeStruct((S, w), dt) for (_, w, _), dt in zip(rows, grad_dtypes)]
                 + [jax.ShapeDtypeStruct(p.shape, F32) for p in params]
                 + [jax.ShapeDtypeStruct((S, w), dt) for (w, dt) in primal])
    out_specs = ([pl.BlockSpec((tile, w), lambda i: (i, 0)) for (_, w, _) in rows] + _full_specs(params)
                 + [pl.BlockSpec((tile, w), lambda i: (i, 0)) for (w, _) in primal])
    res = pl.pallas_call(
        body, name=name, grid=(S // tile,), out_shape=out_shape,
        in_specs=_row_specs(rows + aux, tile) + _full_specs(params + consts) + _row_specs(cots, tile),
        out_specs=out_specs,
        compiler_params=pltpu.CompilerParams(dimension_semantics=("arbitrary",), vmem_limit_bytes=VMEM_LIMIT),
    )(*[v[0] for v in rows + aux], *params, *consts, *[v[0] for v in cots])
    return res[:nr], res[nr:nr + npar], res[nr + npar:]


@jax.custom_vjp
def _mm(a, b):
    return _dot(a.astype(BF16), b.astype(BF16))


def _mm_fwd(a, b):
    return _mm(a, b), (a, b)


def _mm_bwd(res, ct):
    a, b = res
    ctb = ct.astype(BF16)
    return _dot_nt(ctb, b.astype(BF16)), _dot_tn(a.astype(BF16), ctb)


_mm.defvjp(_mm_fwd, _mm_bwd)


def _rms(x, g):
    return x * lax.rsqrt(jnp.mean(x * x, axis=-1, keepdims=True) + EPS) * g


def _f_pre(x, g):
    return (_rms(x, g),)


def _f_pre_res(x, g):
    return _rms(x, g), x


def _f_gate(y, z, g):
    return (_rms(y, g) * jax.nn.silu(z),)


def _f_gmlp(u, v, z, g_v, w_s, b_s, g_o):
    groups = w_s.shape[0]
    u, v = jax.nn.gelu(u), jax.nn.gelu(v)
    t_idx = lax.broadcasted_iota(jnp.int32, (LANE, LANE), 0)
    s_idx = lax.broadcasted_iota(jnp.int32, (LANE, LANE), 1)
    ys = []
    for g in range(groups):
        sl = slice(g * LANE, (g + 1) * LANE)
        vn = _rms(v[:, sl], g_v[:, sl])
        w = jnp.where(s_idx <= t_idx, w_s[g], 0.0)
        ys.append(u[:, sl] * (_mm(w, vn) + b_s[g]))
    return (_rms(jnp.concatenate(ys, axis=1), g_o) * jax.nn.silu(z),)


def _rope(x, cos2, sin2, rot):
    return x * cos2 + _mm(x, rot) * sin2


def _f_cpre(cq, ckv, kr, cos2, sin2, g_q, g_kv, rot):
    return _rms(cq, g_q), _rms(ckv, g_kv), _rope(kr, cos2, sin2, rot)


def _f_crope(q, kv, krr, cos2, sin2, rot):
    heads = q.shape[1] // (2 * LANE)
    qs, ks, vs = [], [], []
    for h in range(heads):
        lo, mid, hi = 2 * h * LANE, (2 * h + 1) * LANE, (2 * h + 2) * LANE
        qs += [q[:, lo:mid], _rope(q[:, mid:hi], cos2, sin2, rot)]
        ks += [kv[:, lo:mid], krr]
        vs += [kv[:, mid:hi]]
    return jnp.concatenate(qs, axis=1), jnp.concatenate(ks, axis=1), jnp.concatenate(vs, axis=1)


def _f_final(h, target, g):
    err = _rms(h, g) - target
    return (0.5 * jnp.mean(err * err, axis=-1, keepdims=True),)


def _rope_matrix():
    r = np.zeros((LANE, LANE), np.float32)
    half = ROPE // 2
    for i in range(half):
        r[i + half, i] = -1.0
        r[i, i + half] = 1.0
    return jnp.asarray(r)


def _head_spec(view, rows, n_rows_block):
    _, cb0, w = view
    if n_rows_block:
        return pl.BlockSpec((rows, w), functools.partial(lambda h, i, cb0: (i, cb0 + h), cb0=cb0))
    return pl.BlockSpec((rows, w), functools.partial(lambda h, i, cb0: (0, cb0 + h), cb0=cb0))


def _stat_spec(tq):
    return pl.BlockSpec((1, tq, 1), lambda h, i: (h, i, 0))


def _softplus(z):
    return jnp.maximum(z, 0.0) + jnp.log(1.0 + jnp.exp(-jnp.abs(z)))


def _cumsum_mm(x, m01):
    hi = x.astype(BF16)
    lo = (x - hi.astype(F32)).astype(BF16)
    return _dot(hi, m01) + _dot(lo, m01)


def _attn_call(body, name, heads, S, tq, ins, in_blocked, outs, out_blocked, scratch, stats_in=0, stats_out=0, carry=None):
    in_specs = [_head_spec(v, tq if blk else S, blk) for v, blk in zip(ins[:len(ins) - stats_in], in_blocked)]
    in_specs += [_stat_spec(tq)] * stats_in
    out_specs = [_head_spec((None, 0, w), tq if blk else S, blk) for (w, _), blk in zip(outs, out_blocked)]
    out_specs += [_stat_spec(tq)] * stats_out
    out_shape = [jax.ShapeDtypeStruct((S, heads * w), dt) for (w, dt) in outs]
    out_shape += [jax.ShapeDtypeStruct((heads, S, 1), F32)] * stats_out
    args = [v[0] for v in ins[:len(ins) - stats_in]] + list(ins[len(ins) - stats_in:])
    res, moved = _call(body, name, (heads, S // tq), in_specs, out_specs, out_shape, scratch, ("arbitrary", "arbitrary"),
                       args, carry)
    return res if carry is None else res + [moved]


def _softmax_fwd(q, k, v, heads, scale, name, tq, bk, carry=None):
    S, dv = q[0].shape[0], v[2]

    def body(q_ref, k_ref, v_ref, o_ref, lse_ref):
        qi = pl.program_id(1)
        qv = q_ref[...]
        row = qi * tq + lax.broadcasted_iota(jnp.int32, (tq, bk), 0)
        col0 = lax.broadcasted_iota(jnp.int32, (tq, bk), 1)

        def step(kb, carry):
            m, l, acc = carry
            sl = pl.ds(pl.multiple_of(kb * bk, bk), bk)
            s = _dot_nt(qv, k_ref[sl, :]) * scale
            s = jnp.where(kb * bk + col0 <= row, s, -1e30)
            m_new = jnp.maximum(m, jnp.max(s, axis=1, keepdims=True))
            p = jnp.exp(s - m_new)
            alpha = jnp.exp(m - m_new)
            l = alpha * l + jnp.sum(p, axis=1, keepdims=True)
            acc = alpha * acc + _dot(p.astype(BF16), v_ref[sl, :])
            return m_new, l, acc

        n_kb = (qi * tq + tq + bk - 1) // bk
        m, l, acc = lax.fori_loop(0, n_kb, step, (jnp.full((tq, 1), -1e30, F32), jnp.zeros((tq, 1), F32),
                                                  jnp.zeros((tq, dv), F32)))
        o_ref[...] = (acc / l).astype(o_ref.dtype)
        lse_ref[0] = m + jnp.log(l)

    return _attn_call(body, name, heads, S, tq, [q, k, v], [1, 0, 0], [(dv, BF16)], [1], [], stats_out=1, carry=carry)


def _softmax_bwd(q, k, v, o, do, lse, heads, scale, name, tq, bk, carry=None):
    S, dq_w, dv = q[0].shape[0], q[2], v[2]
    nq = S // tq

    bd = min(DIAG_BLOCK, tq)
    assert tq % bk == 0 and tq % bd == 0

    def body(q_ref, k_ref, v_ref, o_ref, do_ref, lse_ref, dq_ref, dk_ref, dv_ref, dk_acc, dv_acc, delta_scr, dq_scr):
        qi = pl.program_id(1)

        @pl.when(qi == 0)
        def _():
            dk_acc[...] = jnp.zeros_like(dk_acc)
            dv_acc[...] = jnp.zeros_like(dv_acc)

        delta_scr[...] = jnp.sum(do_ref[...].astype(F32) * o_ref[...].astype(F32), axis=1, keepdims=True)
        dq_scr[...] = jnp.zeros_like(dq_scr)

        def block(r0, sl, width, masked):
            qv, dov = q_ref[r0:, :], do_ref[r0:, :]
            ks, vs = k_ref[sl, :], v_ref[sl, :]
            p = jnp.exp(_dot_nt(qv, ks) * scale - lse_ref[0, r0:, :])
            if masked:
                shape = (tq - r0, width)
                p = jnp.where(lax.broadcasted_iota(jnp.int32, shape, 1) <= lax.broadcasted_iota(jnp.int32, shape, 0), p, 0.0)
            ds = (p * (_dot_nt(dov, vs) - delta_scr[r0:, :]) * scale).astype(BF16)
            dk_acc[sl, :] += _dot_tn(ds, qv)
            dv_acc[sl, :] += _dot_tn(p.astype(BF16), dov)
            dq_scr[r0:, :] += _dot(ds, ks)

        def step(kb, _):
            block(0, pl.ds(pl.multiple_of(kb * bk, bk), bk), bk, False)
            return 0

        lax.fori_loop(0, qi * (tq // bk), step, 0)
        for j in range(tq // bd):
            block(j * bd, pl.ds(pl.multiple_of(qi * tq + j * bd, bd), bd), bd, True)
        dq_ref[...] = dq_scr[...].astype(dq_ref.dtype)

        @pl.when(qi == nq - 1)
        def _():
            dk_ref[...] = dk_acc[...].astype(dk_ref.dtype)
            dv_ref[...] = dv_acc[...].astype(dv_ref.dtype)

    return _attn_call(body, name, heads, S, tq, [q, k, v, o, do, lse], [1, 0, 0, 1, 1],
                      [(dq_w, BF16), (dq_w, BF16), (dv, BF16)], [1, 0, 0],
                      [pltpu.VMEM((S, dq_w), F32), pltpu.VMEM((S, dv), F32), pltpu.VMEM((tq, 1), F32),
                       pltpu.VMEM((tq, dq_w), F32)], stats_in=1, carry=carry)


def _stick_fwd(q, k, v, heads, scale, name, tq, bk, carry=None):
    S, dv = q[0].shape[0], v[2]

    assert tq % bk == 0
    n_sub = tq // bk

    def body(q_ref, k_ref, v_ref, o_ref, tot_ref, c_scr, acc_scr):
        qi = pl.program_id(1)
        m_gt = (lax.broadcasted_iota(jnp.int32, (bk, bk), 0) > lax.broadcasted_iota(jnp.int32, (bk, bk), 1)).astype(BF16)

        def block(r0, sl, masked):
            rows = tq - r0
            z = _dot_nt(q_ref[r0:, :], k_ref[sl, :]) * scale
            sp = _softplus(z)
            lk = -sp
            if masked:
                mask = lax.broadcasted_iota(jnp.int32, (rows, bk), 1) < lax.broadcasted_iota(jnp.int32, (rows, bk), 0)
                lk = jnp.where(mask, lk, 0.0)
            after = _cumsum_mm(lk, m_gt) + c_scr[r0:, :]
            a = jnp.exp(z - sp + after)
            if masked:
                a = jnp.where(mask, a, 0.0)
            acc_scr[r0:, :] += _dot(a.astype(BF16), v_ref[sl, :])
            c_scr[r0:, :] += jnp.sum(lk, axis=1, keepdims=True)

        c_scr[...] = jnp.zeros_like(c_scr)
        acc_scr[...] = jnp.zeros_like(acc_scr)
        for j in reversed(range(n_sub)):
            block(j * bk, pl.ds(pl.multiple_of(qi * tq + j * bk, bk), bk), True)

        def step(it, _):
            block(0, pl.ds(pl.multiple_of((qi * n_sub - 1 - it) * bk, bk), bk), False)
            return 0

        lax.fori_loop(0, qi * n_sub, step, 0)
        o_ref[...] = acc_scr[...].astype(o_ref.dtype)
        tot_ref[0] = c_scr[...]

    return _attn_call(body, name, heads, S, tq, [q, k, v], [1, 0, 0], [(dv, BF16)], [1],
                      [pltpu.VMEM((tq, 1), F32), pltpu.VMEM((tq, dv), F32)], stats_out=1, carry=carry)


def _stick_bwd(q, k, v, do, tot, heads, scale, name, tq, bk, carry=None):
    S, dq_w, dv = q[0].shape[0], q[2], v[2]
    nq = S // tq

    assert tq % bk == 0
    n_sub = tq // bk

    def body(q_ref, k_ref, v_ref, do_ref, tot_ref, dq_ref, dk_ref, dv_ref, dk_acc, dv_acc, pc_scr, gc_scr, dq_scr):
        qi = pl.program_id(1)

        @pl.when(qi == 0)
        def _():
            dk_acc[...] = jnp.zeros_like(dk_acc)
            dv_acc[...] = jnp.zeros_like(dv_acc)

        j_idx = lax.broadcasted_iota(jnp.int32, (bk, bk), 0)
        s_idx = lax.broadcasted_iota(jnp.int32, (bk, bk), 1)
        m_le, m_lt = (j_idx <= s_idx).astype(BF16), (j_idx < s_idx).astype(BF16)

        def block(r0, sl, masked):
            rows = tq - r0
            qv, dov = q_ref[r0:, :], do_ref[r0:, :]
            ks, vs = k_ref[sl, :], v_ref[sl, :]
            z = _dot_nt(qv, ks) * scale
            sp = _softplus(z)
            lk = -sp
            if masked:
                mask = lax.broadcasted_iota(jnp.int32, (rows, bk), 1) < lax.broadcasted_iota(jnp.int32, (rows, bk), 0)
                lk = jnp.where(mask, lk, 0.0)
            after = tot_ref[0, r0:, :] - pc_scr[r0:, :] - _cumsum_mm(lk, m_le)
            log_beta = z - sp
            a = jnp.exp(log_beta + after)
            if masked:
                a = jnp.where(mask, a, 0.0)
            g = _dot_nt(dov, vs) * a
            cg = gc_scr[r0:, :] + _cumsum_mm(g, m_lt)
            dz = g * jnp.exp(-sp) - jnp.exp(log_beta) * cg
            if masked:
                dz = jnp.where(mask, dz, 0.0)
            dz = (dz * scale).astype(BF16)
            dk_acc[sl, :] += _dot_tn(dz, qv)
            dv_acc[sl, :] += _dot_tn(a.astype(BF16), dov)
            dq_scr[r0:, :] += _dot(dz, ks)
            pc_scr[r0:, :] += jnp.sum(lk, axis=1, keepdims=True)
            gc_scr[r0:, :] += jnp.sum(g, axis=1, keepdims=True)

        pc_scr[...] = jnp.zeros_like(pc_scr)
        gc_scr[...] = jnp.zeros_like(gc_scr)
        dq_scr[...] = jnp.zeros_like(dq_scr)

        def step(kb, _):
            block(0, pl.ds(pl.multiple_of(kb * bk, bk), bk), False)
            return 0

        lax.fori_loop(0, qi * n_sub, step, 0)
        for j in range(n_sub):
            block(j * bk, pl.ds(pl.multiple_of(qi * tq + j * bk, bk), bk), True)
        dq_ref[...] = dq_scr[...].astype(dq_ref.dtype)

        @pl.when(qi == nq - 1)
        def _():
            dk_ref[...] = dk_acc[...].astype(dk_ref.dtype)
            dv_ref[...] = dv_acc[...].astype(dv_ref.dtype)

    return _attn_call(body, name, heads, S, tq, [q, k, v, do, tot], [1, 0, 0, 1],
                      [(dq_w, BF16), (dq_w, BF16), (dv, BF16)], [1, 0, 0],
                      [pltpu.VMEM((S, dq_w), F32), pltpu.VMEM((S, dv), F32), pltpu.VMEM((tq, 1), F32),
                       pltpu.VMEM((tq, 1), F32), pltpu.VMEM((tq, dq_w), F32)], stats_in=1, carry=carry)


def _adamw(slots, w, m, v, layer, prev, name, col0=0, carry=None):
    _, R, C = slots.shape
    L, full_c = w.shape[0], w.shape[2]
    item = slots.dtype.itemsize
    tc = _tile(C, 2048)
    tr = _tile(R, max(16, ADAM_TILE_BYTES // (item * tc)), mult=16)
    if tr == R and R * tc * item > ADAM_TILE_BYTES:
        tc = _tile(C, max(LANE, ADAM_TILE_BYTES // (item * R)))
    c1, c2 = 1.0 - ADAM_B1 ** ADAM_STEP, 1.0 - ADAM_B2 ** ADAM_STEP
    n_prev = 0 if prev is None else 4

    def body(s_ref, w_ref, m_ref, v_ref, *rest):
        g_out, d_out, m_out, v_out = rest[n_prev:]
        g = s_ref[0].astype(F32)
        for k in range(1, NDEV):
            g = g + s_ref[k].astype(F32)
        m_new = ADAM_B1 * m_ref[0] + (1.0 - ADAM_B1) * g
        v_new = ADAM_B2 * v_ref[0] + (1.0 - ADAM_B2) * (g * g)
        g_out[0] = g
        m_out[0] = m_new
        v_out[0] = v_new
        d_out[0] = -ADAM_LR * ((m_new / c1) / (jnp.sqrt(v_new / c2) + ADAM_EPS) + ADAM_WD * w_ref[0])

    assert col0 % tc == 0
    spec = pl.BlockSpec((1, tr, tc), lambda i, j: (layer, i, j + col0 // tc))
    in_specs = [pl.BlockSpec((NDEV, tr, tc), lambda i, j: (0, i, j)), spec, spec, spec]
    in_specs += [pl.BlockSpec(memory_space=pl.ANY)] * n_prev
    res, moved = _call(body, name, (R // tr, C // tc), in_specs, [spec] * 4, [jax.ShapeDtypeStruct((L, R, full_c), F32)] * 4,
                       [], ("parallel", "parallel"), [slots, w, m, v, *(prev or [])], carry,
                       aliases={4 + i: i for i in range(n_prev)})
    return res if carry is None else (res, moved)


class _Cfg:
    def __init__(self, S, D, groups, q_lora, kv_lora, c_heads, d_mix):
        self.S, self.D, self.G, self.Q, self.KV, self.Hc, self.DMIX = S, D, groups, q_lora, kv_lora, c_heads, d_mix
        self.A, self.C = groups * LANE, c_heads * LANE
        self.B = d_mix - self.A - self.C
        self.Hb = self.B // LANE
        A, B, C = self.A, self.B, self.C
        assert B % LANE == 0 and B % C == 0 and (B + C) % A == 0
        self.rot_out = A // (d_mix // NDEV) if A % (d_mix // NDEV) == 0 else None
        self.ref_segs = [("ua", A), ("va", A), ("za", A), ("qb", B), ("kb", B), ("vb", B), ("zb", B),
                         ("cq", q_lora), ("ckv", kv_lora), ("kr", ROPE), ("zc", C)]
        self.off, off = {}, 0
        for nm, w in [("ua", A), ("va", A), ("za", A), ("qb", B), ("kb", B), ("vb", B), ("zb", B), ("zc", C),
                      ("cq", q_lora), ("kr", LANE), ("ckv", kv_lora)]:
            off = -(-off // w) * w
            self.off[nm] = off
            off += w
        self.NP = -(-off // 512) * 512
        self.width = {"kr": LANE, **{nm: w for nm, w in self.ref_segs if nm != "kr"}}

    def tiles(self, kind, layer):
        tq, bk = ATTN_TILES[kind][layer % len(ATTN_TILES[kind])]
        return min(tq, self.S), min(bk, self.S)

    def view(self, arr, nm):
        w = self.width[nm]
        return (arr, w, self.off[nm] // w)

    def heads_view(self, arr, nm):
        return (arr, self.off[nm] // LANE, LANE)


def _gathered_rows(parts, a, b):
    per = sum(p.shape[1] for p in parts)
    out = []
    while a < b:
        k, r = divmod(a, per)
        i = 0
        while r >= parts[i].shape[1]:
            r -= parts[i].shape[1]
            i += 1
        n = min(b - a, parts[i].shape[1] - r)
        out.append(parts[i][k, r:r + n])
        a += n
    return out


def _pad_w_in(cfg, parts):
    width_d, dtype = parts[0].shape[2], parts[0].dtype
    start_of, start = {}, 0
    for nm, width in cfg.ref_segs:
        start_of[nm] = (start, width)
        start += width
    rows, pos = [], 0
    for nm, off in sorted(cfg.off.items(), key=lambda kv: kv[1]):
        if off > pos:
            rows.append(jnp.zeros((off - pos, width_d), dtype))
        rows += _gathered_rows(parts, start_of[nm][0], start_of[nm][0] + start_of[nm][1])
        pos = off + start_of[nm][1]
    if cfg.NP > pos:
        rows.append(jnp.zeros((cfg.NP - pos, width_d), dtype))
    return jnp.concatenate(rows, axis=0)


def _unpad_w_in(cfg, wpt):
    return jnp.concatenate([wpt[cfg.off[nm]:cfg.off[nm] + width] for nm, width in cfg.ref_segs], axis=0)


def _to_slots_cols(w):
    R = w.shape[0]
    return w.reshape(R, NDEV, -1).transpose(1, 0, 2)


def _from_slots_cols(s):
    return s.transpose(1, 0, 2).reshape(s.shape[1], -1)


def _perm_rows_out(cfg, w):
    return jnp.concatenate([w[cfg.A:], w[:cfg.A]], axis=0)


def _unperm_rows_out(cfg, w):
    return jnp.concatenate([w[cfg.B + cfg.C:], w[:cfg.B + cfg.C]], axis=0)


def _layer_params(cfg, l, g_pre, a_g_v, a_w_s, a_b_s, c_g_q, c_g_kv, g_out):
    A, B = cfg.A, cfg.B
    return dict(g_pre=g_pre[l][None], g_v=a_g_v[l].reshape(1, A), w_s=a_w_s[l], b_s=a_b_s[l][:, :, None],
                g_q=c_g_q[l][None], g_kv=c_g_kv[l][None],
                g_oa=g_out[l][None, :A], g_ob=g_out[l][None, A:A + B], g_oc=g_out[l][None, A + B:])


def _layer_fwd(cfg, l, x, W, p, cos2, sin2, rot, carry_in=None, carry_stick=None, carry_mla=None):
    S, D, A, B, C = cfg.S, cfg.D, cfg.A, cfg.B, cfg.C
    tag = f"l{l}"
    (h,) = _rowwise(_f_pre, [(x, D, 0)], [], [p["g_pre"]], [], [(D, BF16)], 256, f"pre_{tag}")
    if carry_in is None:
        proj = _matmul(h, W["in"], "nt", BF16, f"mm_in_{tag}")
    else:
        proj, moved_in = _matmul(h, W["in"], "nt", BF16, f"mm_in_{tag}", carry=carry_in[0])
        carry_in[1](moved_in)
    a_rows = [cfg.view(proj, "ua"), cfg.view(proj, "va"), cfg.view(proj, "za")]
    a_par = [p["g_v"], p["w_s"], p["b_s"], p["g_oa"]]
    (ya,) = _rowwise(_f_gmlp, a_rows, [], a_par, [], [(A, BF16)], LANE, f"gmlp_{tag}")
    qb, kb, vb = cfg.heads_view(proj, "qb"), cfg.heads_view(proj, "kb"), cfg.heads_view(proj, "vb")
    yb, tot, *moved_stick = _stick_fwd(qb, kb, vb, cfg.Hb, LANE ** -0.5, f"stick_fwd_{tag}", *cfg.tiles("stick_fwd", l),
                                       carry=carry_stick)
    (ybg,) = _rowwise(_f_gate, [(yb, B, 0), cfg.view(proj, "zb")], [], [p["g_ob"]], [], [(B, BF16)], 256, f"gate_b_{tag}")
    c_rows = [cfg.view(proj, "cq"), cfg.view(proj, "ckv"), cfg.view(proj, "kr")]
    trig = [(cos2, LANE, 0), (sin2, LANE, 0)]
    cqn, ckvn, krr = _rowwise(_f_cpre, c_rows, trig, [p["g_q"], p["g_kv"]], [rot],
                              [(cfg.Q, BF16), (cfg.KV, BF16), (LANE, BF16)], 256, f"cpre_{tag}")
    q_raw = _matmul(cqn, W["uq"], "nt", BF16, f"mm_uq_{tag}")
    kv = _matmul(ckvn, W["ukv"], "nn", BF16, f"mm_ukv_{tag}")
    r_rows = [(q_raw, 2 * C, 0), (kv, 2 * C, 0), (krr, LANE, 0)]
    q_rot, k_full, v_c = _rowwise(_f_crope, r_rows, trig, [], [rot], [(2 * C, BF16), (2 * C, BF16), (C, BF16)], 128,
                                  f"crope_{tag}")
    qc, kc, vc = (q_rot, 0, 2 * LANE), (k_full, 0, 2 * LANE), (v_c, 0, LANE)
    yc, lse, *moved_mla = _softmax_fwd(qc, kc, vc, cfg.Hc, (LANE + ROPE) ** -0.5, f"mla_fwd_{tag}", *cfg.tiles("mla_fwd", l),
                                       carry=carry_mla)
    (ycg,) = _rowwise(_f_gate, [(yc, C, 0), cfg.view(proj, "zc")], [], [p["g_oc"]], [], [(C, BF16)], 256, f"gate_c_{tag}")
    y = jnp.concatenate([ybg, ycg, ya], axis=1)
    out = _matmul(y, W["out"], "nn", F32, f"mm_out_{tag}", add=x)
    saved = dict(x=x, h=h, proj=proj, yb=yb, tot=tot, cqn=cqn, ckvn=ckvn, krr=krr, q_raw=q_raw, kv=kv,
                 q_rot=q_rot, k_full=k_full, v_c=v_c, yc=yc, lse=lse, y=y)
    return out, saved, (moved_stick[0] if moved_stick else []), (moved_mla[0] if moved_mla else [])


def _layer_bwd(cfg, l, dout, sv, W, p, cos2, sin2, rot, ext_stick, ext_mla, last):
    S, D, A, B, C = cfg.S, cfg.D, cfg.A, cfg.B, cfg.C
    tag = f"l{l}"
    proj = sv["proj"]
    dy = _matmul(dout, W["out"], "nt", BF16, f"mm_dy_{tag}")
    d_wout = _matmul(sv["y"], dout, "tn", BF16, f"mm_dwout_{tag}")
    out_rot = cfg.rot_out or 0
    wout_slots = (d_wout if cfg.rot_out else _unperm_rows_out(cfg, d_wout)).reshape(NDEV, cfg.DMIX // NDEV, D)
    (dyb, dzb), (dg_ob,), _ = _rowwise_vjp(_f_gate, [(sv["yb"], B, 0), cfg.view(proj, "zb")], [], [p["g_ob"]], [],
                                           [(dy, B, 0)], [BF16, BF16], 256, f"gate_b_bwd_{tag}")
    (dyc, dzc), (dg_oc,), _ = _rowwise_vjp(_f_gate, [(sv["yc"], C, 0), cfg.view(proj, "zc")], [], [p["g_oc"]], [],
                                           [(dy, C, B // C)], [BF16, BF16], 256, f"gate_c_bwd_{tag}")
    a_rows = [cfg.view(proj, "ua"), cfg.view(proj, "va"), cfg.view(proj, "za")]
    a_par = [p["g_v"], p["w_s"], p["b_s"], p["g_oa"]]
    (dua, dva, dza), (dg_v, dw_s, db_s, dg_oa), _ = _rowwise_vjp(
        _f_gmlp, a_rows, [], a_par, [], [(dy, A, (B + C) // A)], [BF16] * 3, LANE, f"gmlp_bwd_{tag}")
    qb, kb, vb = cfg.heads_view(proj, "qb"), cfg.heads_view(proj, "kb"), cfg.heads_view(proj, "vb")
    dqb, dkb, dvb, moved_stick = _stick_bwd(qb, kb, vb, (dyb, 0, LANE), sv["tot"], cfg.Hb, LANE ** -0.5,
                                            f"stick_bwd_{tag}", *cfg.tiles("stick_bwd", l),
                                            carry=_Exchange([[a] for a in (ext_stick or [wout_slots])], False,
                                                            None if ext_stick else [out_rot]))
    ext_got = [mv[0] for mv in moved_stick] if ext_stick else []
    mla_rots = [0] * len(ext_mla) + ([out_rot] if ext_stick else [])
    ext_mla = ext_mla + ([wout_slots] if ext_stick else [])
    qc, kc, vc = (sv["q_rot"], 0, 2 * LANE), (sv["k_full"], 0, 2 * LANE), (sv["v_c"], 0, LANE)
    dq_rot, dk_full, dv_c, *moved_mla = _softmax_bwd(qc, kc, vc, (sv["yc"], 0, LANE), (dyc, 0, LANE), sv["lse"], cfg.Hc,
                                                     (LANE + ROPE) ** -0.5, f"mla_bwd_{tag}", *cfg.tiles("mla_bwd", l),
                                                     carry=_Exchange([[a] for a in ext_mla], False, mla_rots) if ext_mla else None)
    moved_mla = [mv[0] for mv in (moved_mla[0] if moved_mla else [])]
    got = dict(w_out=moved_mla.pop() if ext_stick else moved_stick[0][0])
    ext_got += moved_mla
    trig = [(cos2, LANE, 0), (sin2, LANE, 0)]
    r_rows = [(sv["q_raw"], 2 * C, 0), (sv["kv"], 2 * C, 0), (sv["krr"], LANE, 0)]
    (dq_raw, dkv, dkrr), _, _ = _rowwise_vjp(_f_crope, r_rows, trig, [], [rot],
                                             [(dq_rot, 2 * C, 0), (dk_full, 2 * C, 0), (dv_c, C, 0)], [BF16] * 3, 128,
                                             f"crope_bwd_{tag}")
    dcqn = _matmul(dq_raw, W["uq"], "nn", BF16, f"mm_dcq_{tag}")
    d_wuq = _matmul(dq_raw, sv["cqn"], "tn", BF16, f"mm_dwuq_{tag}")
    dckvn = _matmul(dkv, W["ukv"], "nt", BF16, f"mm_dckv_{tag}")
    d_wukv = _matmul(sv["ckvn"], dkv, "tn", BF16, f"mm_dwukv_{tag}")
    c_rows = [cfg.view(proj, "cq"), cfg.view(proj, "ckv"), cfg.view(proj, "kr")]
    (dcq, dckv, dkr), (dg_q, dg_kv), _ = _rowwise_vjp(
        _f_cpre, c_rows, trig, [p["g_q"], p["g_kv"]], [rot],
        [(dcqn, cfg.Q, 0), (dckvn, cfg.KV, 0), (dkrr, LANE, 0)], [BF16] * 3, 256, f"cpre_bwd_{tag}")
    parts = dict(ua=dua, va=dva, za=dza, qb=dqb, kb=dkb, vb=dvb, zb=dzb, zc=dzc, cq=dcq, kr=dkr, ckv=dckv)
    cols, pos = [], 0
    for nm, off in sorted(cfg.off.items(), key=lambda kv_: kv_[1]):
        if off > pos:
            cols.append(jnp.zeros((S, off - pos), BF16))
        cols.append(parts[nm])
        pos = off + parts[nm].shape[1]
    if cfg.NP > pos:
        cols.append(jnp.zeros((S, cfg.NP - pos), BF16))
    dproj = jnp.concatenate(cols, axis=1)
    to_send = dict(c_w_uq=d_wuq.reshape(cfg.Hc, 2 * LANE, cfg.Q)[:, :LANE + ROPE].reshape(NDEV, -1, cfg.Q),
                   c_w_ukv=_to_slots_cols(d_wukv))
    def d_win(c0, width, name, carry=None):
        res = _matmul(dproj, sv["h"], "tn", BF16, name, b_cols=(c0, width), carry=carry)
        wt, moved = res if carry is not None else (res, None)
        return _unpad_w_in(cfg, wt).reshape(NDEV, -1, width), moved

    if last:
        ranges = LAST_W_IN_RANGES if D % (sum(LAST_W_IN_RANGES) * LANE) == 0 else (1, 1)
        unit = D // sum(ranges)
        arrived, c0, sending = [], 0, None
        for i, r in enumerate(ranges):
            carry = _Exchange([[sending]], False) if sending is not None else None
            sending, moved = d_win(c0, r * unit, f"mm_dwin_{i}_{tag}", carry)
            if moved is not None:
                arrived.append(moved[0][0])
            c0 += r * unit
        dh, moved = _matmul(dproj, W["in"], "nn", BF16, f"mm_dh_{tag}",
                            carry=_Exchange([[sending], [to_send["c_w_uq"]], [to_send["c_w_ukv"]]], False))
        got.update(w_in=tuple(arrived) + (moved[0][0],), c_w_uq=moved[1][0], c_w_ukv=moved[2][0])
        to_send = {}
    else:
        first = D // 2 if (D // 4) % LANE else D // 4
        slots_a, _ = d_win(0, first, f"mm_dwin_a_{tag}")
        dh, moved_a = _matmul(dproj, W["in"], "nn", BF16, f"mm_dh_{tag}", carry=_Exchange([[slots_a]], False))
        got["w_in"] = (moved_a[0][0],)
        to_send["w_in"], _ = d_win(first, D - first, f"mm_dwin_b_{tag}")
    (dx,), (dg_pre,), _ = _rowwise_vjp(_f_pre_res, [(sv["x"], D, 0)], [], [p["g_pre"]], [],
                                       [(dh, D, 0), (dout, D, 0)], [F32], 128, f"pre_bwd_{tag}")
    small = dict(g_pre=dg_pre[0], a_g_v=dg_v.reshape(cfg.G, LANE), a_w_s=dw_s, a_b_s=db_s[:, :, 0], c_g_q=dg_q[0],
                 c_g_kv=dg_kv[0], g_out=jnp.concatenate([dg_oa[0], dg_ob[0], dg_oc[0]]))
    return dx, small, got, to_send, ext_got


def _pack_small(vals):
    pieces = []
    for nm in SMALL:
        piece = vals[nm].reshape(-1, LANE)
        pieces.append(jnp.pad(piece, ((0, -piece.shape[0] % 8), (0, 0))))
    packed = jnp.concatenate(pieces, axis=0)
    return jnp.pad(packed, ((0, -packed.shape[0] % SMALL_ROWS), (0, 0)))


def _unpack_small(packed, like):
    out, row = {}, 0
    for nm in SMALL:
        n = like[nm].size // LANE
        out[nm] = packed[row:row + n].reshape(like[nm].shape)
        row += n + (-n % 8)
    return out


def kernel(x, positions, g_pre, w_in, a_g_v, a_w_s, a_b_s, c_g_q, c_g_kv, c_w_uq, c_w_ukv, g_out, w_out, g_final, loss_target, m_g_pre, m_w_in, m_a_g_v, m_a_w_s, m_a_b_s, m_c_g_q, m_c_g_kv, m_c_w_uq, m_c_w_ukv, m_g_out, m_w_out, m_g_final, v_g_pre, v_w_in, v_a_g_v, v_a_w_s, v_a_b_s, v_c_g_q, v_c_g_kv, v_c_w_uq, v_c_w_ukv, v_g_out, v_w_out, v_g_final):
    depth, S, D = w_in.shape[0], x.shape[1], x.shape[2]
    cfg = _Cfg(S, D, a_g_v.shape[1], c_g_q.shape[1], c_g_kv.shape[1], c_w_ukv.shape[2] * NDEV // (2 * LANE), g_out.shape[1])
    weights = dict(g_pre=g_pre, w_in=w_in, a_g_v=a_g_v, a_w_s=a_w_s, a_b_s=a_b_s, c_g_q=c_g_q, c_g_kv=c_g_kv,
                   c_w_uq=c_w_uq, c_w_ukv=c_w_ukv, g_out=g_out, w_out=w_out, g_final=g_final)
    mom_m = dict(g_pre=m_g_pre, w_in=m_w_in, a_g_v=m_a_g_v, a_w_s=m_a_w_s, a_b_s=m_a_b_s, c_g_q=m_c_g_q, c_g_kv=m_c_g_kv,
                 c_w_uq=m_c_w_uq, c_w_ukv=m_c_w_ukv, g_out=m_g_out, w_out=m_w_out, g_final=m_g_final)
    mom_v = dict(g_pre=v_g_pre, w_in=v_w_in, a_g_v=v_a_g_v, a_w_s=v_a_w_s, a_b_s=v_a_b_s, c_g_q=v_c_g_q, c_g_kv=v_c_g_kv,
                 c_w_uq=v_c_w_uq, c_w_ukv=v_c_w_ukv, g_out=v_g_out, w_out=v_w_out, g_final=v_g_final)
    big_names = ("w_in", "c_w_uq", "c_w_ukv", "w_out")

    inv_freq = 1.0 / (ROPE_THETA ** (jnp.arange(0, ROPE, 2, dtype=F32) / ROPE))
    ang = positions[0].astype(F32)[:, None] * inv_freq
    zpad = jnp.zeros((S, LANE - ROPE), F32)
    cos2 = jnp.concatenate([jnp.cos(ang), jnp.cos(ang), zpad], axis=1)
    sin2 = jnp.concatenate([jnp.sin(ang), jnp.sin(ang), zpad], axis=1)
    rot = _rope_matrix()

    for tree in (weights, mom_m, mom_v):
        for nm in TRANSPOSED:
            tree[nm] = jnp.swapaxes(tree[nm], 1, 2)

    def shards(l, names):
        return [[weights[nm][l].astype(BF16)] for nm in names]

    def assemble_rest(g_uq, g_ukv, g_wout):
        uq = jnp.pad(g_uq[0].reshape(cfg.Hc, LANE + ROPE, cfg.Q), ((0, 0), (0, LANE - ROPE), (0, 0)))
        return {"uq": uq.reshape(2 * cfg.C, cfg.Q), "ukv": _from_slots_cols(g_ukv[0]),
                "out": g_wout[0].reshape(cfg.DMIX, D) if cfg.rot_out else _perm_rows_out(cfg, g_wout[0].reshape(cfg.DMIX, D))}

    params = [_layer_params(cfg, l, g_pre, a_g_v, a_w_s, a_b_s, c_g_q, c_g_kv, g_out) for l in range(depth)]

    in_parts = [g[0] for g in _exchange(_GatherTwoLevel(shards(0, big_names[:1])), "gather_w_in_l0")]
    got_rest = None
    hcur, saved, Ws = x[0], [], []
    for l in range(depth):
        Ws.append({"in": _pad_w_in(cfg, in_parts)})
        if got_rest is not None:
            Ws[l].update(assemble_rest(*got_rest))
        nxt = l + 1 < depth
        early_rows = min(W_IN_EARLY_ROWS, weights["w_in"].shape[1] // 2)
        rest_rots = [0, 0, cfg.rot_out or 0]
        riding = ([] if got_rest is not None else shards(l, big_names[1:]))
        riding_rots = ([] if got_rest is not None else rest_rots) + ([0] if nxt else [])
        riding += [[weights["w_in"][l + 1][:early_rows].astype(BF16)]] if nxt else []
        in_parts = []

        def take(moved, l=l, rest_here=got_rest is None, nxt=nxt):
            if rest_here:
                Ws[l].update(assemble_rest(*moved[:3]))
            if nxt:
                in_parts.append(moved[-1][0])

        hcur, sv, got_late, got_rest = _layer_fwd(
            cfg, l, hcur, Ws[l], params[l], cos2, sin2, rot,
            carry_in=(_GatherTwoLevel(riding, riding_rots), take) if riding else None,
            carry_stick=_GatherTwoLevel([[weights["w_in"][l + 1][early_rows:].astype(BF16)]]) if nxt else None,
            carry_mla=_GatherTwoLevel(shards(l + 1, big_names[1:]), rest_rots) if nxt else None)
        in_parts += [g[0] for g in got_late]
        saved.append(sv)
    (dh,), (dg_final,), (loss_rows,) = _rowwise_vjp(
        _f_final, [(hcur, D, 0)], [(loss_target[0], D, 0)], [g_final[None]], [], [(jnp.ones((S, 1), F32), 1, 0)],
        [F32], 128, "final", primal=[(1, F32)])
    loss = lax.psum(jnp.sum(loss_rows), MESH_AXES)

    small_g, slots, pending = [None] * depth, [None] * depth, {}
    for l in reversed(range(depth)):
        ext_stick = [pending["w_in"]] if pending else []
        ext_mla = [pending["c_w_uq"], pending["c_w_ukv"]] if pending else []
        dh, small_g[l], slots[l], pending, ext_got = _layer_bwd(cfg, l, dh, saved[l], Ws[l], params[l], cos2, sin2, rot,
                                                                ext_stick, ext_mla, l == 0)
        if ext_got:
            slots[l + 1].update(w_in=slots[l + 1]["w_in"] + (ext_got[0],), c_w_uq=ext_got[1], c_w_ukv=ext_got[2])
    grad_x = dh[None]
    small_grads = {nm: jnp.stack([small_g[l][nm] for l in range(depth)]) for nm in SMALL if nm != "g_final"}
    small_grads["g_final"] = dg_final[0]
    (small_slots,) = _exchange(_GatherTwoLevel([[_pack_small(small_grads)]]), "gather_small_grads")

    res = {}
    for nm in big_names:
        res[nm] = None
        for l in reversed(range(depth)):
            parts = slots[l][nm] if isinstance(slots[l][nm], tuple) else (slots[l][nm],)
            col0 = 0
            for i, part in enumerate(parts):
                res[nm] = _adamw(part, weights[nm], mom_m[nm], mom_v[nm], l, res[nm], f"adamw_{nm}_l{l}_{i}", col0)
                col0 += part.shape[2]
        if nm in TRANSPOSED:
            res[nm] = [jnp.swapaxes(r, 1, 2) for r in res[nm]]
    packed = _adamw(small_slots[0], _pack_small(weights)[None], _pack_small(mom_m)[None], _pack_small(mom_v)[None], 0, None,
                    "adamw_small")
    small_res = [_unpack_small(r[0], weights) for r in packed]
    order = ("g_pre", "w_in", "a_g_v", "a_w_s", "a_b_s", "c_g_q", "c_g_kv", "c_w_uq", "c_w_ukv", "g_out", "w_out", "g_final")
    outs = [loss, grad_x]
    for kind in range(4):
        outs += [small_res[kind][nm] if nm in SMALL else res[nm][kind] for nm in order]
    return tuple(outs)
```

```python
import functools
import math

import numpy as np
import jax
import jax.numpy as jnp
from jax import lax
from jax.experimental import pallas as pl
from jax.experimental.pallas import tpu as pltpu

NDEV = 8
MESH_AXES = ("x", "y", "c")
LANE = 128
ROPE = 64
EPS = 1e-6
ROPE_THETA = 10000.0
ADAM_LR, ADAM_B1, ADAM_B2, ADAM_EPS, ADAM_WD, ADAM_STEP = 0.001, 0.9, 0.999, 1e-08, 0.01, 10
VMEM_LIMIT = 48 * 1024 * 1024
ADAM_TILE_BYTES = 768 * 1024
LAST_W_IN_RANGES = (1, 2, 2, 3)
W_IN_EARLY_ROWS = 512
CARRY_MID_PERCENT = 88
SMALL_ROWS = 256
ATTN_TILES = {"stick_fwd": [(2048, 256)], "stick_bwd": [(2048, 256)], "mla_fwd": [(512, 1024)], "mla_bwd": [(2048, 512)]}
DIAG_BLOCK = 256
F32, BF16 = jnp.float32, jnp.bfloat16
SMALL = ("g_pre", "a_g_v", "a_w_s", "a_b_s", "c_g_q", "c_g_kv", "g_out", "g_final")
TRANSPOSED = ("w_in", "c_w_uq")


def _tile(dim, cap, mult=LANE):
    if dim <= cap:
        return dim
    t = (cap // mult) * mult
    while t >= mult:
        if dim % t == 0:
            return t
        t -= mult
    return dim


def _dot_nt(a, b):
    return lax.dot_general(a, b, (((1,), (1,)), ((), ())), preferred_element_type=F32)


def _dot_tn(a, b):
    return lax.dot_general(a, b, (((0,), (0,)), ((), ())), preferred_element_type=F32)


def _dot(a, b):
    return jnp.dot(a, b, preferred_element_type=F32)


class _Exchange:
    def __init__(self, groups, gather, rots=None):
        self.groups, self.gather = groups, gather
        self.rots = rots or [0] * len(groups)
        assert not (gather and any(self.rots)) or type(self) is not _Exchange
        self.flat = [(gi, li, a) for gi, grp in enumerate(groups) for li, a in enumerate(grp)]
        self.n = len(self.flat)
        self.args = [a for (_, _, a) in self.flat]
        self.out_shape = [jax.ShapeDtypeStruct((len(grp), NDEV) + tuple(grp[0].shape[-2:]), grp[0].dtype) for grp in groups]
        self.scratch = [pltpu.SemaphoreType.DMA((self.n, NDEV - 1)), pltpu.SemaphoreType.DMA((self.n, NDEV - 1)),
                        pltpu.SemaphoreType.DMA((self.n,))]

    def _copies(self, ins, outs, send_sems, recv_sems, local_sems, landings):
        x, y, c = lax.axis_index("x"), lax.axis_index("y"), lax.axis_index("c")
        me = 4 * x + 2 * y + c
        def part_for(dev, gi):
            return (dev + NDEV - self.rots[gi]) % NDEV

        owns = [pltpu.make_async_copy(ins[i] if self.gather else ins[i].at[part_for(me, gi)], outs[gi].at[li, me],
                                      local_sems.at[i]) for i, (gi, li, _) in enumerate(self.flat)]
        pairs = []
        for k in range(1, NDEV):
            px = 1 - x if k & 4 else x
            py = 1 - y if k & 2 else y
            pc = 1 - c if k & 1 else c
            peer = 4 * px + 2 * py + pc
            for i, (gi, li, _) in enumerate(self.flat):
                src = ins[i] if self.gather else ins[i].at[part_for(peer, gi)]
                sems = dict(send_sem=send_sems.at[i, k - 1], recv_sem=recv_sems.at[i, k - 1],
                            device_id=(px, py, pc), device_id_type=pl.DeviceIdType.MESH)
                out = pltpu.make_async_remote_copy(src_ref=src, dst_ref=outs[gi].at[li, me], **sems)
                landing = pltpu.make_async_remote_copy(src_ref=src, dst_ref=outs[gi].at[li, peer], **sems) if landings else None
                pairs.append((out, landing))
        return owns, pairs

    def start(self, ins, outs, sems):
        owns, pairs = self._copies(ins, outs, *sems, landings=False)
        for own in owns:
            own.start()
        for out, _ in pairs:
            out.start()

    def mid(self, ins, outs, sems):
        pass

    def wait(self, ins, outs, sems):
        owns, pairs = self._copies(ins, outs, *sems, landings=True)
        for out, landing in pairs:
            out.wait_send()
            landing.wait_recv()
        for own in owns:
            own.wait()


class _GatherTwoLevel(_Exchange):
    def __init__(self, groups, rots=None):
        super().__init__(groups, True, rots)

    def _copy(self, i, k, ins, outs, send_sems, recv_sems, landing):
        gi, li, _ = self.flat[i]
        x, y, c = lax.axis_index("x"), lax.axis_index("y"), lax.axis_index("c")
        chips = [(x, y), (1 - x, y), (x, 1 - y), (1 - x, 1 - y)]

        def slot(chip, core):
            return outs[gi].at[li, (4 * chip[0] + 2 * chip[1] + core + NDEV - self.rots[gi]) % NDEV]

        if k == 0:
            to, src, dst, lands = (x, y, 1 - c), ins[i], slot(chips[0], c), slot(chips[0], 1 - c)
        elif k <= 3:
            to, src, dst, lands = (*chips[k], c), ins[i], slot(chips[0], c), slot(chips[k], c)
        else:
            to, src, dst, lands = (x, y, 1 - c), slot(chips[k - 3], c), slot(chips[k - 3], c), slot(chips[k - 3], 1 - c)
        return pltpu.make_async_remote_copy(src_ref=src, dst_ref=lands if landing else dst, send_sem=send_sems.at[i, k],
                                            recv_sem=recv_sems.at[i, k], device_id=to, device_id_type=pl.DeviceIdType.MESH)

    def _own(self, i, ins, outs, local_sems):
        gi, li, _ = self.flat[i]
        me = 4 * lax.axis_index("x") + 2 * lax.axis_index("y") + lax.axis_index("c")
        return pltpu.make_async_copy(ins[i], outs[gi].at[li, (me + NDEV - self.rots[gi]) % NDEV], local_sems.at[i])

    def start(self, ins, outs, sems):
        send_sems, recv_sems, local_sems = sems
        for i in range(self.n):
            self._own(i, ins, outs, local_sems).start()
        for k in range(4):
            for i in range(self.n):
                self._copy(i, k, ins, outs, send_sems, recv_sems, False).start()

    def mid(self, ins, outs, sems):
        send_sems, recv_sems, _ = sems
        for k in range(1, 4):
            for i in range(self.n):
                self._copy(i, k, ins, outs, send_sems, recv_sems, True).wait_recv()
                self._copy(i, k + 3, ins, outs, send_sems, recv_sems, False).start()

    def wait(self, ins, outs, sems):
        send_sems, recv_sems, local_sems = sems
        for k in (0, 4, 5, 6):
            for i in range(self.n):
                self._copy(i, k, ins, outs, send_sems, recv_sems, True).wait_recv()
        for k in range(NDEV - 1):
            for i in range(self.n):
                self._copy(i, k, ins, outs, send_sems, recv_sems, False).wait_send()
        for i in range(self.n):
            self._own(i, ins, outs, local_sems).wait()


def _call(body, name, grid, in_specs, out_specs, out_shape, scratch, semantics, args, carry=None, aliases=None):
    n_in, n_out, n_scr = len(in_specs), len(out_specs), len(scratch)
    if carry is None:
        run = body
    else:
        semantics = ("arbitrary",) * len(grid)
        anyspec = pl.BlockSpec(memory_space=pl.ANY)
        in_specs = list(in_specs) + [anyspec] * carry.n
        out_specs = list(out_specs) + [anyspec] * len(carry.groups)
        out_shape = list(out_shape) + carry.out_shape
        scratch = list(scratch) + carry.scratch
        args = list(args) + carry.args

        def run(*refs):
            c_in, x_in = refs[:n_in], refs[n_in:n_in + carry.n]
            rest = refs[n_in + carry.n:]
            c_out, x_out = rest[:n_out], rest[n_out:n_out + len(carry.groups)]
            c_scr, sems = rest[n_out + len(carry.groups):len(rest) - 3], rest[len(rest) - 3:]
            step, total = 0, 1
            for d, extent in enumerate(grid):
                step = step * extent + pl.program_id(d)
                total *= extent

            @pl.when(step == 0)
            def _():
                carry.start(x_in, x_out, sems)

            body(*c_in, *c_out, *c_scr)

            @pl.when(step == (total * CARRY_MID_PERCENT) // 100)
            def _():
                carry.mid(x_in, x_out, sems)

            @pl.when(step == total - 1)
            def _():
                carry.wait(x_in, x_out, sems)

    res = pl.pallas_call(
        run, name=name, grid=grid, out_shape=list(out_shape), in_specs=list(in_specs), out_specs=list(out_specs),
        scratch_shapes=list(scratch), input_output_aliases=aliases or {},
        compiler_params=pltpu.CompilerParams(dimension_semantics=semantics, vmem_limit_bytes=VMEM_LIMIT,
                                             has_side_effects=carry is not None),
    )(*args)
    return list(res[:n_out]), list(res[n_out:])


def _exchange(ex, name):
    groups = ex.groups

    def body(*refs):
        ins, outs, sems = refs[:ex.n], refs[ex.n:ex.n + len(groups)], refs[ex.n + len(groups):]
        ex.start(ins, outs, sems)
        ex.mid(ins, outs, sems)
        ex.wait(ins, outs, sems)

    anyspec = pl.BlockSpec(memory_space=pl.ANY)
    return pl.pallas_call(
        body, name=name, out_shape=ex.out_shape, in_specs=[anyspec] * ex.n, out_specs=[anyspec] * len(groups),
        scratch_shapes=ex.scratch, compiler_params=pltpu.CompilerParams(has_side_effects=True),
    )(*ex.args)


def _matmul(a, b, mode, out_dtype, name, add=None, tm=1024, tn=1024, tk=2048, carry=None, b_cols=None):
    if mode == "tn":
        (K, M), (K2, N) = a.shape, b.shape
    elif mode == "nt":
        (M, K), (N, K2) = a.shape, b.shape
    else:
        (M, K), (K2, N) = a.shape, b.shape
    assert K == K2, (a.shape, b.shape, mode)
    col0 = 0
    if b_cols is not None:
        assert mode != "nt"
        col0, N = b_cols
        tn = min(tn, math.gcd(col0, N))
    tm, tn, tk = _tile(M, tm), _tile(N, tn), _tile(K, tk)
    assert col0 % tn == 0
    nk, jb = K // tk, col0 // tn
    a_spec = pl.BlockSpec((tk, tm), lambda i, j, k: (k, i)) if mode == "tn" else pl.BlockSpec((tm, tk), lambda i, j, k: (i, k))
    b_spec = pl.BlockSpec((tn, tk), lambda i, j, k: (j, k)) if mode == "nt" else pl.BlockSpec((tk, tn), lambda i, j, k: (k, j + jb))
    dot = {"nn": _dot, "nt": _dot_nt, "tn": _dot_tn}[mode]
    has_add = add is not None

    def body(*refs):
        a_ref, b_ref = refs[0], refs[1]
        o_ref, acc = refs[-2], refs[-1]
        k = pl.program_id(2)

        @pl.when(k == 0)
        def _():
            acc[...] = jnp.zeros_like(acc)

        acc[...] += dot(a_ref[...].astype(BF16), b_ref[...].astype(BF16))

        @pl.when(k == nk - 1)
        def _():
            r = acc[...]
            if has_add:
                r = r + refs[2][...]
            o_ref[...] = r.astype(o_ref.dtype)

    in_specs = [a_spec, b_spec]
    args = [a, b]
    if has_add:
        in_specs.append(pl.BlockSpec((tm, tn), lambda i, j, k: (i, j)))
        args.append(add)
    (out,), moved = _call(body, name, (M // tm, N // tn, nk), in_specs, [pl.BlockSpec((tm, tn), lambda i, j, k: (i, j))],
                          [jax.ShapeDtypeStruct((M, N), out_dtype)], [pltpu.VMEM((tm, tn), F32)],
                          ("parallel", "parallel", "arbitrary"), args, carry)
    return out if carry is None else (out, moved)


def _row_specs(views, tile):
    return [pl.BlockSpec((tile, w), functools.partial(lambda i, cb: (i, cb), cb=cb)) for (_, w, cb) in views]


def _full_specs(arrs):
    return [pl.BlockSpec(p.shape, functools.partial(lambda i, nd: (0,) * nd, nd=p.ndim)) for p in arrs]


def _rowwise(fn, rows, aux, params, consts, outs, tile, name):
    S = rows[0][0].shape[0]
    nr, na, npar, nc = len(rows), len(aux), len(params), len(consts)

    def body(*refs):
        ins = [r[...].astype(F32) for r in refs[:nr + na]]
        small = [r[...] for r in refs[nr + na:nr + na + npar + nc]]
        res = fn(*ins, *small)
        for o_ref, r in zip(refs[nr + na + npar + nc:], res):
            o_ref[...] = r.astype(o_ref.dtype)

    return pl.pallas_call(
        body, name=name, grid=(S // tile,),
        out_shape=[jax.ShapeDtypeStruct((S, w), dt) for (w, dt) in outs],
        in_specs=_row_specs(rows + aux, tile) + _full_specs(params + consts),
        out_specs=[pl.BlockSpec((tile, w), lambda i: (i, 0)) for (w, _) in outs],
        compiler_params=pltpu.CompilerParams(dimension_semantics=("parallel",), vmem_limit_bytes=VMEM_LIMIT),
    )(*[v[0] for v in rows + aux], *params, *consts)


def _rowwise_vjp(fn, rows, aux, params, consts, cots, grad_dtypes, tile, name, primal=()):
    S = rows[0][0].shape[0]
    nr, na, npar, nc, nct, npr = len(rows), len(aux), len(params), len(consts), len(cots), len(primal)

    def body(*refs):
        n_in = nr + na + npar + nc + nct
        rv = [r[...].astype(F32) for r in refs[:nr]]
        av = [r[...].astype(F32) for r in refs[nr:nr + na]]
        pv = [r[...] for r in refs[nr + na:nr + na + npar]]
        cv = [r[...] for r in refs[nr + na + npar:nr + na + npar + nc]]
        ct = tuple(r[...].astype(F32) for r in refs[nr + na + npar + nc:n_in])
        res, vjp = jax.vjp(lambda *rp: tuple(fn(*rp[:nr], *av, *rp[nr:], *cv)), *rv, *pv)
        grads = vjp(ct)
        g_refs = refs[n_in:n_in + nr]
        p_refs = refs[n_in + nr:n_in + nr + npar]
        o_refs = refs[n_in + nr + npar:]
        for g_ref, g in zip(g_refs, grads[:nr]):
            g_ref[...] = g.astype(g_ref.dtype)

        @pl.when(pl.program_id(0) == 0)
        def _():
            for p_ref in p_refs:
                p_ref[...] = jnp.zeros_like(p_ref)

        for p_ref, g in zip(p_refs, grads[nr:]):
            p_ref[...] += g
        for o_ref, r in zip(o_refs, res[:npr]):
            o_ref[...] = r.astype(o_ref.dtype)

    out_shape = ([jax.ShapeDtypeStruct((S, w), dt) for (_, w, _), dt in zip(rows, grad_dtypes)]
                 + [jax.ShapeDtypeStruct(p.shape, F32) for p in params]
                 + [jax.ShapeDtypeStruct((S, w), dt) for (w, dt) in primal])
    out_specs = ([pl.BlockSpec((tile, w), lambda i: (i, 0)) for (_, w, _) in rows] + _full_specs(params)
                 + [pl.BlockSpec((tile, w), lambda i: (i, 0)) for (w, _) in primal])
    res = pl.pallas_call(
        body, name=name, grid=(S // tile,), out_shape=out_shape,
        in_specs=_row_specs(rows + aux, tile) + _full_specs(params + consts) + _row_specs(cots, tile),
        out_specs=out_specs,
        compiler_params=pltpu.CompilerParams(dimension_semantics=("arbitrary",), vmem_limit_bytes=VMEM_LIMIT),
    )(*[v[0] for v in rows + aux], *params, *consts, *[v[0] for v in cots])
    return res[:nr], res[nr:nr + npar], res[nr + npar:]


@jax.custom_vjp
def _mm(a, b):
    return _dot(a.astype(BF16), b.astype(BF16))


def _mm_fwd(a, b):
    return _mm(a, b), (a, b)


def _mm_bwd(res, ct):
    a, b = res
    ctb = ct.astype(BF16)
    return _dot_nt(ctb, b.astype(BF16)), _dot_tn(a.astype(BF16), ctb)


_mm.defvjp(_mm_fwd, _mm_bwd)


def _rms(x, g):
    return x * lax.rsqrt(jnp.mean(x * x, axis=-1, keepdims=True) + EPS) * g


def _f_pre(x, g):
    return (_rms(x, g),)


def _f_pre_res(x, g):
    return _rms(x, g), x


def _f_gate(y, z, g):
    return (_rms(y, g) * jax.nn.silu(z),)


def _f_gmlp(u, v, z, g_v, w_s, b_s, g_o):
    groups = w_s.shape[0]
    u, v = jax.nn.gelu(u), jax.nn.gelu(v)
    t_idx = lax.broadcasted_iota(jnp.int32, (LANE, LANE), 0)
    s_idx = lax.broadcasted_iota(jnp.int32, (LANE, LANE), 1)
    ys = []
    for g in range(groups):
        sl = slice(g * LANE, (g + 1) * LANE)
        vn = _rms(v[:, sl], g_v[:, sl])
        w = jnp.where(s_idx <= t_idx, w_s[g], 0.0)
        ys.append(u[:, sl] * (_mm(w, vn) + b_s[g]))
    return (_rms(jnp.concatenate(ys, axis=1), g_o) * jax.nn.silu(z),)


def _rope(x, cos2, sin2, rot):
    return x * cos2 + _mm(x, rot) * sin2


def _f_cpre(cq, ckv, kr, cos2, sin2, g_q, g_kv, rot):
    return _rms(cq, g_q), _rms(ckv, g_kv), _rope(kr, cos2, sin2, rot)


def _f_crope(q, kv, krr, cos2, sin2, rot):
    heads = q.shape[1] // (2 * LANE)
    qs, ks, vs = [], [], []
    for h in range(heads):
        lo, mid, hi = 2 * h * LANE, (2 * h + 1) * LANE, (2 * h + 2) * LANE
        qs += [q[:, lo:mid], _rope(q[:, mid:hi], cos2, sin2, rot)]
        ks += [kv[:, lo:mid], krr]
        vs += [kv[:, mid:hi]]
    return jnp.concatenate(qs, axis=1), jnp.concatenate(ks, axis=1), jnp.concatenate(vs, axis=1)


def _f_final(h, target, g):
    err = _rms(h, g) - target
    return (0.5 * jnp.mean(err * err, axis=-1, keepdims=True),)


def _rope_matrix():
    r = np.zeros((LANE, LANE), np.float32)
    half = ROPE // 2
    for i in range(half):
        r[i + half, i] = -1.0
        r[i, i + half] = 1.0
    return jnp.asarray(r)


def _head_spec(view, rows, n_rows_block):
    _, cb0, w = view
    if n_rows_block:
        return pl.BlockSpec((rows, w), functools.partial(lambda h, i, cb0: (i, cb0 + h), cb0=cb0))
    return pl.BlockSpec((rows, w), functools.partial(lambda h, i, cb0: (0, cb0 + h), cb0=cb0))


def _stat_spec(tq):
    return pl.BlockSpec((1, tq, 1), lambda h, i: (h, i, 0))


def _softplus(z):
    return jnp.maximum(z, 0.0) + jnp.log(1.0 + jnp.exp(-jnp.abs(z)))


def _cumsum_mm(x, m01):
    hi = x.astype(BF16)
    lo = (x - hi.astype(F32)).astype(BF16)
    return _dot(hi, m01) + _dot(lo, m01)


def _attn_call(body, name, heads, S, tq, ins, in_blocked, outs, out_blocked, scratch, stats_in=0, stats_out=0, carry=None):
    in_specs = [_head_spec(v, tq if blk else S, blk) for v, blk in zip(ins[:len(ins) - stats_in], in_blocked)]
    in_specs += [_stat_spec(tq)] * stats_in
    out_specs = [_head_spec((None, 0, w), tq if blk else S, blk) for (w, _), blk in zip(outs, out_blocked)]
    out_specs += [_stat_spec(tq)] * stats_out
    out_shape = [jax.ShapeDtypeStruct((S, heads * w), dt) for (w, dt) in outs]
    out_shape += [jax.ShapeDtypeStruct((heads, S, 1), F32)] * stats_out
    args = [v[0] for v in ins[:len(ins) - stats_in]] + list(ins[len(ins) - stats_in:])
    res, moved = _call(body, name, (heads, S // tq), in_specs, out_specs, out_shape, scratch, ("arbitrary", "arbitrary"),
                       args, carry)
    return res if carry is None else res + [moved]


def _softmax_fwd(q, k, v, heads, scale, name, tq, bk, carry=None):
    S, dv = q[0].shape[0], v[2]

    def body(q_ref, k_ref, v_ref, o_ref, lse_ref):
        qi = pl.program_id(1)
        qv = q_ref[...]
        row = qi * tq + lax.broadcasted_iota(jnp.int32, (tq, bk), 0)
        col0 = lax.broadcasted_iota(jnp.int32, (tq, bk), 1)

        def step(kb, carry):
            m, l, acc = carry
            sl = pl.ds(pl.multiple_of(kb * bk, bk), bk)
            s = _dot_nt(qv, k_ref[sl, :]) * scale
            s = jnp.where(kb * bk + col0 <= row, s, -1e30)
            m_new = jnp.maximum(m, jnp.max(s, axis=1, keepdims=True))
            p = jnp.exp(s - m_new)
            alpha = jnp.exp(m - m_new)
            l = alpha * l + jnp.sum(p, axis=1, keepdims=True)
            acc = alpha * acc + _dot(p.astype(BF16), v_ref[sl, :])
            return m_new, l, acc

        n_kb = (qi * tq + tq + bk - 1) // bk
        m, l, acc = lax.fori_loop(0, n_kb, step, (jnp.full((tq, 1), -1e30, F32), jnp.zeros((tq, 1), F32),
                                                  jnp.zeros((tq, dv), F32)))
        o_ref[...] = (acc / l).astype(o_ref.dtype)
        lse_ref[0] = m + jnp.log(l)

    return _attn_call(body, name, heads, S, tq, [q, k, v], [1, 0, 0], [(dv, BF16)], [1], [], stats_out=1, carry=carry)


def _softmax_bwd(q, k, v, o, do, lse, heads, scale, name, tq, bk, carry=None):
    S, dq_w, dv = q[0].shape[0], q[2], v[2]
    nq = S // tq

    bd = min(DIAG_BLOCK, tq)
    assert tq % bk == 0 and tq % bd == 0

    def body(q_ref, k_ref, v_ref, o_ref, do_ref, lse_ref, dq_ref, dk_ref, dv_ref, dk_acc, dv_acc, delta_scr, dq_scr):
        qi = pl.program_id(1)

        @pl.when(qi == 0)
        def _():
            dk_acc[...] = jnp.zeros_like(dk_acc)
            dv_acc[...] = jnp.zeros_like(dv_acc)

        delta_scr[...] = jnp.sum(do_ref[...].astype(F32) * o_ref[...].astype(F32), axis=1, keepdims=True)
        dq_scr[...] = jnp.zeros_like(dq_scr)

        def block(r0, sl, width, masked):
            qv, dov = q_ref[r0:, :], do_ref[r0:, :]
            ks, vs = k_ref[sl, :], v_ref[sl, :]
            p = jnp.exp(_dot_nt(qv, ks) * scale - lse_ref[0, r0:, :])
            if masked:
                shape = (tq - r0, width)
                p = jnp.where(lax.broadcasted_iota(jnp.int32, shape, 1) <= lax.broadcasted_iota(jnp.int32, shape, 0), p, 0.0)
            ds = (p * (_dot_nt(dov, vs) - delta_scr[r0:, :]) * scale).astype(BF16)
            dk_acc[sl, :] += _dot_tn(ds, qv)
            dv_acc[sl, :] += _dot_tn(p.astype(BF16), dov)
            dq_scr[r0:, :] += _dot(ds, ks)

        def step(kb, _):
            block(0, pl.ds(pl.multiple_of(kb * bk, bk), bk), bk, False)
            return 0

        lax.fori_loop(0, qi * (tq // bk), step, 0)
        for j in range(tq // bd):
            block(j * bd, pl.ds(pl.multiple_of(qi * tq + j * bd, bd), bd), bd, True)
        dq_ref[...] = dq_scr[...].astype(dq_ref.dtype)

        @pl.when(qi == nq - 1)
        def _():
            dk_ref[...] = dk_acc[...].astype(dk_ref.dtype)
            dv_ref[...] = dv_acc[...].astype(dv_ref.dtype)

    return _attn_call(body, name, heads, S, tq, [q, k, v, o, do, lse], [1, 0, 0, 1, 1],
                      [(dq_w, BF16), (dq_w, BF16), (dv, BF16)], [1, 0, 0],
                      [pltpu.VMEM((S, dq_w), F32), pltpu.VMEM((S, dv), F32), pltpu.VMEM((tq, 1), F32),
                       pltpu.VMEM((tq, dq_w), F32)], stats_in=1, carry=carry)


def _stick_fwd(q, k, v, heads, scale, name, tq, bk, carry=None):
    S, dv = q[0].shape[0], v[2]

    assert tq % bk == 0
    n_sub = tq // bk

    def body(q_ref, k_ref, v_ref, o_ref, tot_ref, c_scr, acc_scr):
        qi = pl.program_id(1)
        m_gt = (lax.broadcasted_iota(jnp.int32, (bk, bk), 0) > lax.broadcasted_iota(jnp.int32, (bk, bk), 1)).astype(BF16)

        def block(r0, sl, masked):
            rows = tq - r0
            z = _dot_nt(q_ref[r0:, :], k_ref[sl, :]) * scale
            sp = _softplus(z)
            lk = -sp
            if masked:
                mask = lax.broadcasted_iota(jnp.int32, (rows, bk), 1) < lax.broadcasted_iota(jnp.int32, (rows, bk), 0)
                lk = jnp.where(mask, lk, 0.0)
            after = _cumsum_mm(lk, m_gt) + c_scr[r0:, :]
            a = jnp.exp(z - sp + after)
            if masked:
                a = jnp.where(mask, a, 0.0)
            acc_scr[r0:, :] += _dot(a.astype(BF16), v_ref[sl, :])
            c_scr[r0:, :] += jnp.sum(lk, axis=1, keepdims=True)

        c_scr[...] = jnp.zeros_like(c_scr)
        acc_scr[...] = jnp.zeros_like(acc_scr)
        for j in reversed(range(n_sub)):
            block(j * bk, pl.ds(pl.multiple_of(qi * tq + j * bk, bk), bk), True)

        def step(it, _):
            block(0, pl.ds(pl.multiple_of((qi * n_sub - 1 - it) * bk, bk), bk), False)
            return 0

        lax.fori_loop(0, qi * n_sub, step, 0)
        o_ref[...] = acc_scr[...].astype(o_ref.dtype)
        tot_ref[0] = c_scr[...]

    return _attn_call(body, name, heads, S, tq, [q, k, v], [1, 0, 0], [(dv, BF16)], [1],
                      [pltpu.VMEM((tq, 1), F32), pltpu.VMEM((tq, dv), F32)], stats_out=1, carry=carry)


def _stick_bwd(q, k, v, do, tot, heads, scale, name, tq, bk, carry=None):
    S, dq_w, dv = q[0].shape[0], q[2], v[2]
    nq = S // tq

    assert tq % bk == 0
    n_sub = tq // bk

    def body(q_ref, k_ref, v_ref, do_ref, tot_ref, dq_ref, dk_ref, dv_ref, dk_acc, dv_acc, pc_scr, gc_scr, dq_scr):
        qi = pl.program_id(1)

        @pl.when(qi == 0)
        def _():
            dk_acc[...] = jnp.zeros_like(dk_acc)
            dv_acc[...] = jnp.zeros_like(dv_acc)

        j_idx = lax.broadcasted_iota(jnp.int32, (bk, bk), 0)
        s_idx = lax.broadcasted_iota(jnp.int32, (bk, bk), 1)
        m_le, m_lt = (j_idx <= s_idx).astype(BF16), (j_idx < s_idx).astype(BF16)

        def block(r0, sl, masked):
            rows = tq - r0
            qv, dov = q_ref[r0:, :], do_ref[r0:, :]
            ks, vs = k_ref[sl, :], v_ref[sl, :]
            z = _dot_nt(qv, ks) * scale
            sp = _softplus(z)
            lk = -sp
            if masked:
                mask = lax.broadcasted_iota(jnp.int32, (rows, bk), 1) < lax.broadcasted_iota(jnp.int32, (rows, bk), 0)
                lk = jnp.where(mask, lk, 0.0)
            after = tot_ref[0, r0:, :] - pc_scr[r0:, :] - _cumsum_mm(lk, m_le)
            log_beta = z - sp
            a = jnp.exp(log_beta + after)
            if masked:
                a = jnp.where(mask, a, 0.0)
            g = _dot_nt(dov, vs) * a
            cg = gc_scr[r0:, :] + _cumsum_mm(g, m_lt)
            dz = g * jnp.exp(-sp) - jnp.exp(log_beta) * cg
            if masked:
                dz = jnp.where(mask, dz, 0.0)
            dz = (dz * scale).astype(BF16)
            dk_acc[sl, :] += _dot_tn(dz, qv)
            dv_acc[sl, :] += _dot_tn(a.astype(BF16), dov)
            dq_scr[r0:, :] += _dot(dz, ks)
            pc_scr[r0:, :] += jnp.sum(lk, axis=1, keepdims=True)
            gc_scr[r0:, :] += jnp.sum(g, axis=1, keepdims=True)

        pc_scr[...] = jnp.zeros_like(pc_scr)
        gc_scr[...] = jnp.zeros_like(gc_scr)
        dq_scr[...] = jnp.zeros_like(dq_scr)

        def step(kb, _):
            block(0, pl.ds(pl.multiple_of(kb * bk, bk), bk), False)
            return 0

        lax.fori_loop(0, qi * n_sub, step, 0)
        for j in range(n_sub):
            block(j * bk, pl.ds(pl.multiple_of(qi * tq + j * bk, bk), bk), True)
        dq_ref[...] = dq_scr[...].astype(dq_ref.dtype)

        @pl.when(qi == nq - 1)
        def _():
            dk_ref[...] = dk_acc[...].astype(dk_ref.dtype)
            dv_ref[...] = dv_acc[...].astype(dv_ref.dtype)

    return _attn_call(body, name, heads, S, tq, [q, k, v, do, tot], [1, 0, 0, 1],
                      [(dq_w, BF16), (dq_w, BF16), (dv, BF16)], [1, 0, 0],
                      [pltpu.VMEM((S, dq_w), F32), pltpu.VMEM((S, dv), F32), pltpu.VMEM((tq, 1), F32),
                       pltpu.VMEM((tq, 1), F32), pltpu.VMEM((tq, dq_w), F32)], stats_in=1, carry=carry)


def _adamw(slots, w, m, v, layer, prev, name, col0=0, carry=None):
    _, R, C = slots.shape
    L, full_c = w.shape[0], w.shape[2]
    item = slots.dtype.itemsize
    tc = _tile(C, 2048)
    tr = _tile(R, max(16, ADAM_TILE_BYTES // (item * tc)), mult=16)
    if tr == R and R * tc * item > ADAM_TILE_BYTES:
        tc = _tile(C, max(LANE, ADAM_TILE_BYTES // (item * R)))
    c1, c2 = 1.0 - ADAM_B1 ** ADAM_STEP, 1.0 - ADAM_B2 ** ADAM_STEP
    n_prev = 0 if prev is None else 4

    def body(s_ref, w_ref, m_ref, v_ref, *rest):
        g_out, d_out, m_out, v_out = rest[n_prev:]
        g = s_ref[0].astype(F32)
        for k in range(1, NDEV):
            g = g + s_ref[k].astype(F32)
        m_new = ADAM_B1 * m_ref[0] + (1.0 - ADAM_B1) * g
        v_new = ADAM_B2 * v_ref[0] + (1.0 - ADAM_B2) * (g * g)
        g_out[0] = g
        m_out[0] = m_new
        v_out[0] = v_new
        d_out[0] = -ADAM_LR * ((m_new / c1) / (jnp.sqrt(v_new / c2) + ADAM_EPS) + ADAM_WD * w_ref[0])

    assert col0 % tc == 0
    spec = pl.BlockSpec((1, tr, tc), lambda i, j: (layer, i, j + col0 // tc))
    in_specs = [pl.BlockSpec((NDEV, tr, tc), lambda i, j: (0, i, j)), spec, spec, spec]
    in_specs += [pl.BlockSpec(memory_space=pl.ANY)] * n_prev
    res, moved = _call(body, name, (R // tr, C // tc), in_specs, [spec] * 4, [jax.ShapeDtypeStruct((L, R, full_c), F32)] * 4,
                       [], ("parallel", "parallel"), [slots, w, m, v, *(prev or [])], carry,
                       aliases={4 + i: i for i in range(n_prev)})
    return res if carry is None else (res, moved)


class _Cfg:
    def __init__(self, S, D, groups, q_lora, kv_lora, c_heads, d_mix):
        self.S, self.D, self.G, self.Q, self.KV, self.Hc, self.DMIX = S, D, groups, q_lora, kv_lora, c_heads, d_mix
        self.A, self.C = groups * LANE, c_heads * LANE
        self.B = d_mix - self.A - self.C
        self.Hb = self.B // LANE
        A, B, C = self.A, self.B, self.C
        assert B % LANE == 0 and B % C == 0 and (B + C) % A == 0
        self.rot_out = A // (d_mix // NDEV) if A % (d_mix // NDEV) == 0 else None
        self.ref_segs = [("ua", A), ("va", A), ("za", A), ("qb", B), ("kb", B), ("vb", B), ("zb", B),
                         ("cq", q_lora), ("ckv", kv_lora), ("kr", ROPE), ("zc", C)]
        self.off, off = {}, 0
        for nm, w in [("ua", A), ("va", A), ("za", A), ("qb", B), ("kb", B), ("vb", B), ("zb", B), ("zc", C),
                      ("cq", q_lora), ("kr", LANE), ("ckv", kv_lora)]:
            off = -(-off // w) * w
            self.off[nm] = off
            off += w
        self.NP = -(-off // 512) * 512
        self.width = {"kr": LANE, **{nm: w for nm, w in self.ref_segs if nm != "kr"}}

    def tiles(self, kind, layer):
        tq, bk = ATTN_TILES[kind][layer % len(ATTN_TILES[kind])]
        return min(tq, self.S), min(bk, self.S)

    def view(self, arr, nm):
        w = self.width[nm]
        return (arr, w, self.off[nm] // w)

    def heads_view(self, arr, nm):
        return (arr, self.off[nm] // LANE, LANE)


def _gathered_rows(parts, a, b):
    per = sum(p.shape[1] for p in parts)
    out = []
    while a < b:
        k, r = divmod(a, per)
        i = 0
        while r >= parts[i].shape[1]:
            r -= parts[i].shape[1]
            i += 1
        n = min(b - a, parts[i].shape[1] - r)
        out.append(parts[i][k, r:r + n])
        a += n
    return out


def _pad_w_in(cfg, parts):
    width_d, dtype = parts[0].shape[2], parts[0].dtype
    start_of, start = {}, 0
    for nm, width in cfg.ref_segs:
        start_of[nm] = (start, width)
        start += width
    rows, pos = [], 0
    for nm, off in sorted(cfg.off.items(), key=lambda kv: kv[1]):
        if off > pos:
            rows.append(jnp.zeros((off - pos, width_d), dtype))
        rows += _gathered_rows(parts, start_of[nm][0], start_of[nm][0] + start_of[nm][1])
        pos = off + start_of[nm][1]
    if cfg.NP > pos:
        rows.append(jnp.zeros((cfg.NP - pos, width_d), dtype))
    return jnp.concatenate(rows, axis=0)


def _unpad_w_in(cfg, wpt):
    return jnp.concatenate([wpt[cfg.off[nm]:cfg.off[nm] + width] for nm, width in cfg.ref_segs], axis=0)


def _to_slots_cols(w):
    R = w.shape[0]
    return w.reshape(R, NDEV, -1).transpose(1, 0, 2)


def _from_slots_cols(s):
    return s.transpose(1, 0, 2).reshape(s.shape[1], -1)


def _perm_rows_out(cfg, w):
    return jnp.concatenate([w[cfg.A:], w[:cfg.A]], axis=0)


def _unperm_rows_out(cfg, w):
    return jnp.concatenate([w[cfg.B + cfg.C:], w[:cfg.B + cfg.C]], axis=0)


def _layer_params(cfg, l, g_pre, a_g_v, a_w_s, a_b_s, c_g_q, c_g_kv, g_out):
    A, B = cfg.A, cfg.B
    return dict(g_pre=g_pre[l][None], g_v=a_g_v[l].reshape(1, A), w_s=a_w_s[l], b_s=a_b_s[l][:, :, None],
                g_q=c_g_q[l][None], g_kv=c_g_kv[l][None],
                g_oa=g_out[l][None, :A], g_ob=g_out[l][None, A:A + B], g_oc=g_out[l][None, A + B:])


def _layer_fwd(cfg, l, x, W, p, cos2, sin2, rot, carry_in=None, carry_stick=None, carry_mla=None):
    S, D, A, B, C = cfg.S, cfg.D, cfg.A, cfg.B, cfg.C
    tag = f"l{l}"
    (h,) = _rowwise(_f_pre, [(x, D, 0)], [], [p["g_pre"]], [], [(D, BF16)], 256, f"pre_{tag}")
    if carry_in is None:
        proj = _matmul(h, W["in"], "nt", BF16, f"mm_in_{tag}")
    else:
        proj, moved_in = _matmul(h, W["in"], "nt", BF16, f"mm_in_{tag}", carry=carry_in[0])
        carry_in[1](moved_in)
    a_rows = [cfg.view(proj, "ua"), cfg.view(proj, "va"), cfg.view(proj, "za")]
    a_par = [p["g_v"], p["w_s"], p["b_s"], p["g_oa"]]
    (ya,) = _rowwise(_f_gmlp, a_rows, [], a_par, [], [(A, BF16)], LANE, f"gmlp_{tag}")
    qb, kb, vb = cfg.heads_view(proj, "qb"), cfg.heads_view(proj, "kb"), cfg.heads_view(proj, "vb")
    yb, tot, *moved_stick = _stick_fwd(qb, kb, vb, cfg.Hb, LANE ** -0.5, f"stick_fwd_{tag}", *cfg.tiles("stick_fwd", l),
                                       carry=carry_stick)
    (ybg,) = _rowwise(_f_gate, [(yb, B, 0), cfg.view(proj, "zb")], [], [p["g_ob"]], [], [(B, BF16)], 256, f"gate_b_{tag}")
    c_rows = [cfg.view(proj, "cq"), cfg.view(proj, "ckv"), cfg.view(proj, "kr")]
    trig = [(cos2, LANE, 0), (sin2, LANE, 0)]
    cqn, ckvn, krr = _rowwise(_f_cpre, c_rows, trig, [p["g_q"], p["g_kv"]], [rot],
                              [(cfg.Q, BF16), (cfg.KV, BF16), (LANE, BF16)], 256, f"cpre_{tag}")
    q_raw = _matmul(cqn, W["uq"], "nt", BF16, f"mm_uq_{tag}")
    kv = _matmul(ckvn, W["ukv"], "nn", BF16, f"mm_ukv_{tag}")
    r_rows = [(q_raw, 2 * C, 0), (kv, 2 * C, 0), (krr, LANE, 0)]
    q_rot, k_full, v_c = _rowwise(_f_crope, r_rows, trig, [], [rot], [(2 * C, BF16), (2 * C, BF16), (C, BF16)], 128,
                                  f"crope_{tag}")
    qc, kc, vc = (q_rot, 0, 2 * LANE), (k_full, 0, 2 * LANE), (v_c, 0, LANE)
    yc, lse, *moved_mla = _softmax_fwd(qc, kc, vc, cfg.Hc, (LANE + ROPE) ** -0.5, f"mla_fwd_{tag}", *cfg.tiles("mla_fwd", l),
                                       carry=carry_mla)
    (ycg,) = _rowwise(_f_gate, [(yc, C, 0), cfg.view(proj, "zc")], [], [p["g_oc"]], [], [(C, BF16)], 256, f"gate_c_{tag}")
    y = jnp.concatenate([ybg, ycg, ya], axis=1)
    out = _matmul(y, W["out"], "nn", F32, f"mm_out_{tag}", add=x)
    saved = dict(x=x, h=h, proj=proj, yb=yb, tot=tot, cqn=cqn, ckvn=ckvn, krr=krr, q_raw=q_raw, kv=kv,
                 q_rot=q_rot, k_full=k_full, v_c=v_c, yc=yc, lse=lse, y=y)
    return out, saved, (moved_stick[0] if moved_stick else []), (moved_mla[0] if moved_mla else [])


def _layer_bwd(cfg, l, dout, sv, W, p, cos2, sin2, rot, ext_stick, ext_mla, last):
    S, D, A, B, C = cfg.S, cfg.D, cfg.A, cfg.B, cfg.C
    tag = f"l{l}"
    proj = sv["proj"]
    dy = _matmul(dout, W["out"], "nt", BF16, f"mm_dy_{tag}")
    d_wout = _matmul(sv["y"], dout, "tn", BF16, f"mm_dwout_{tag}")
    out_rot = cfg.rot_out or 0
    wout_slots = (d_wout if cfg.rot_out else _unperm_rows_out(cfg, d_wout)).reshape(NDEV, cfg.DMIX // NDEV, D)
    (dyb, dzb), (dg_ob,), _ = _rowwise_vjp(_f_gate, [(sv["yb"], B, 0), cfg.view(proj, "zb")], [], [p["g_ob"]], [],
                                           [(dy, B, 0)], [BF16, BF16], 512, f"gate_b_bwd_{tag}")
    (dyc, dzc), (dg_oc,), _ = _rowwise_vjp(_f_gate, [(sv["yc"], C, 0), cfg.view(proj, "zc")], [], [p["g_oc"]], [],
                                           [(dy, C, B // C)], [BF16, BF16], 512, f"gate_c_bwd_{tag}")
    a_rows = [cfg.view(proj, "ua"), cfg.view(proj, "va"), cfg.view(proj, "za")]
    a_par = [p["g_v"], p["w_s"], p["b_s"], p["g_oa"]]
    (dua, dva, dza), (dg_v, dw_s, db_s, dg_oa), _ = _rowwise_vjp(
        _f_gmlp, a_rows, [], a_par, [], [(dy, A, (B + C) // A)], [BF16] * 3, LANE, f"gmlp_bwd_{tag}")
    qb, kb, vb = cfg.heads_view(proj, "qb"), cfg.heads_view(proj, "kb"), cfg.heads_view(proj, "vb")
    dqb, dkb, dvb, moved_stick = _stick_bwd(qb, kb, vb, (dyb, 0, LANE), sv["tot"], cfg.Hb, LANE ** -0.5,
                                            f"stick_bwd_{tag}", *cfg.tiles("stick_bwd", l),
                                            carry=_Exchange([[a] for a in (ext_stick or [wout_slots])], False,
                                                            None if ext_stick else [out_rot]))
    ext_got = [mv[0] for mv in moved_stick] if ext_stick else []
    mla_rots = [0] * len(ext_mla) + ([out_rot] if ext_stick else [])
    ext_mla = ext_mla + ([wout_slots] if ext_stick else [])
    qc, kc, vc = (sv["q_rot"], 0, 2 * LANE), (sv["k_full"], 0, 2 * LANE), (sv["v_c"], 0, LANE)
    dq_rot, dk_full, dv_c, *moved_mla = _softmax_bwd(qc, kc, vc, (sv["yc"], 0, LANE), (dyc, 0, LANE), sv["lse"], cfg.Hc,
                                                     (LANE + ROPE) ** -0.5, f"mla_bwd_{tag}", *cfg.tiles("mla_bwd", l),
                                                     carry=_Exchange([[a] for a in ext_mla], False, mla_rots) if ext_mla else None)
    moved_mla = [mv[0] for mv in (moved_mla[0] if moved_mla else [])]
    got = dict(w_out=moved_mla.pop() if ext_stick else moved_stick[0][0])
    ext_got += moved_mla
    trig = [(cos2, LANE, 0), (sin2, LANE, 0)]
    r_rows = [(sv["q_raw"], 2 * C, 0), (sv["kv"], 2 * C, 0), (sv["krr"], LANE, 0)]
    (dq_raw, dkv, dkrr), _, _ = _rowwise_vjp(_f_crope, r_rows, trig, [], [rot],
                                             [(dq_rot, 2 * C, 0), (dk_full, 2 * C, 0), (dv_c, C, 0)], [BF16] * 3, 128,
                                             f"crope_bwd_{tag}")
    dcqn = _matmul(dq_raw, W["uq"], "nn", BF16, f"mm_dcq_{tag}")
    d_wuq = _matmul(dq_raw, sv["cqn"], "tn", BF16, f"mm_dwuq_{tag}")
    dckvn = _matmul(dkv, W["ukv"], "nt", BF16, f"mm_dckv_{tag}")
    d_wukv = _matmul(sv["ckvn"], dkv, "tn", BF16, f"mm_dwukv_{tag}")
    c_rows = [cfg.view(proj, "cq"), cfg.view(proj, "ckv"), cfg.view(proj, "kr")]
    (dcq, dckv, dkr), (dg_q, dg_kv), _ = _rowwise_vjp(
        _f_cpre, c_rows, trig, [p["g_q"], p["g_kv"]], [rot],
        [(dcqn, cfg.Q, 0), (dckvn, cfg.KV, 0), (dkrr, LANE, 0)], [BF16] * 3, 256, f"cpre_bwd_{tag}")
    parts = dict(ua=dua, va=dva, za=dza, qb=dqb, kb=dkb, vb=dvb, zb=dzb, zc=dzc, cq=dcq, kr=dkr, ckv=dckv)
    cols, pos = [], 0
    for nm, off in sorted(cfg.off.items(), key=lambda kv_: kv_[1]):
        if off > pos:
            cols.append(jnp.zeros((S, off - pos), BF16))
        cols.append(parts[nm])
        pos = off + parts[nm].shape[1]
    if cfg.NP > pos:
        cols.append(jnp.zeros((S, cfg.NP - pos), BF16))
    dproj = jnp.concatenate(cols, axis=1)
    to_send = dict(c_w_uq=d_wuq.reshape(cfg.Hc, 2 * LANE, cfg.Q)[:, :LANE + ROPE].reshape(NDEV, -1, cfg.Q),
                   c_w_ukv=_to_slots_cols(d_wukv))
    def d_win(c0, width, name, carry=None):
        res = _matmul(dproj, sv["h"], "tn", BF16, name, b_cols=(c0, width), carry=carry)
        wt, moved = res if carry is not None else (res, None)
        return _unpad_w_in(cfg, wt).reshape(NDEV, -1, width), moved

    if last:
        ranges = LAST_W_IN_RANGES if D % (sum(LAST_W_IN_RANGES) * LANE) == 0 else (1, 1)
        unit = D // sum(ranges)
        arrived, c0, sending = [], 0, None
        for i, r in enumerate(ranges):
            carry = _Exchange([[sending]], False) if sending is not None else None
            sending, moved = d_win(c0, r * unit, f"mm_dwin_{i}_{tag}", carry)
            if moved is not None:
                arrived.append(moved[0][0])
            c0 += r * unit
        dh, moved = _matmul(dproj, W["in"], "nn", BF16, f"mm_dh_{tag}",
                            carry=_Exchange([[sending], [to_send["c_w_uq"]], [to_send["c_w_ukv"]]], False))
        got.update(w_in=tuple(arrived) + (moved[0][0],), c_w_uq=moved[1][0], c_w_ukv=moved[2][0])
        to_send = {}
    else:
        first = D // 2 if (D // 4) % LANE else D // 4
        slots_a, _ = d_win(0, first, f"mm_dwin_a_{tag}")
        dh, moved_a = _matmul(dproj, W["in"], "nn", BF16, f"mm_dh_{tag}", carry=_Exchange([[slots_a]], False))
        got["w_in"] = (moved_a[0][0],)
        to_send["w_in"], _ = d_win(first, D - first, f"mm_dwin_b_{tag}")
    (dx,), (dg_pre,), _ = _rowwise_vjp(_f_pre_res, [(sv["x"], D, 0)], [], [p["g_pre"]], [],
                                       [(dh, D, 0), (dout, D, 0)], [F32], 256, f"pre_bwd_{tag}")
    small = dict(g_pre=dg_pre[0], a_g_v=dg_v.reshape(cfg.G, LANE), a_w_s=dw_s, a_b_s=db_s[:, :, 0], c_g_q=dg_q[0],
                 c_g_kv=dg_kv[0], g_out=jnp.concatenate([dg_oa[0], dg_ob[0], dg_oc[0]]))
    return dx, small, got, to_send, ext_got


def _pack_small(vals):
    pieces = []
    for nm in SMALL:
        piece = vals[nm].reshape(-1, LANE)
        pieces.append(jnp.pad(piece, ((0, -piece.shape[0] % 8), (0, 0))))
    packed = jnp.concatenate(pieces, axis=0)
    return jnp.pad(packed, ((0, -packed.shape[0] % SMALL_ROWS), (0, 0)))


def _unpack_small(packed, like):
    out, row = {}, 0
    for nm in SMALL:
        n = like[nm].size // LANE
        out[nm] = packed[row:row + n].reshape(like[nm].shape)
        row += n + (-n % 8)
    return out


def kernel(x, positions, g_pre, w_in, a_g_v, a_w_s, a_b_s, c_g_q, c_g_kv, c_w_uq, c_w_ukv, g_out, w_out, g_final, loss_target, m_g_pre, m_w_in, m_a_g_v, m_a_w_s, m_a_b_s, m_c_g_q, m_c_g_kv, m_c_w_uq, m_c_w_ukv, m_g_out, m_w_out, m_g_final, v_g_pre, v_w_in, v_a_g_v, v_a_w_s, v_a_b_s, v_c_g_q, v_c_g_kv, v_c_w_uq, v_c_w_ukv, v_g_out, v_w_out, v_g_final):
    depth, S, D = w_in.shape[0], x.shape[1], x.shape[2]
    cfg = _Cfg(S, D, a_g_v.shape[1], c_g_q.shape[1], c_g_kv.shape[1], c_w_ukv.shape[2] * NDEV // (2 * LANE), g_out.shape[1])
    weights = dict(g_pre=g_pre, w_in=w_in, a_g_v=a_g_v, a_w_s=a_w_s, a_b_s=a_b_s, c_g_q=c_g_q, c_g_kv=c_g_kv,
                   c_w_uq=c_w_uq, c_w_ukv=c_w_ukv, g_out=g_out, w_out=w_out, g_final=g_final)
    mom_m = dict(g_pre=m_g_pre, w_in=m_w_in, a_g_v=m_a_g_v, a_w_s=m_a_w_s, a_b_s=m_a_b_s, c_g_q=m_c_g_q, c_g_kv=m_c_g_kv,
                 c_w_uq=m_c_w_uq, c_w_ukv=m_c_w_ukv, g_out=m_g_out, w_out=m_w_out, g_final=m_g_final)
    mom_v = dict(g_pre=v_g_pre, w_in=v_w_in, a_g_v=v_a_g_v, a_w_s=v_a_w_s, a_b_s=v_a_b_s, c_g_q=v_c_g_q, c_g_kv=v_c_g_kv,
                 c_w_uq=v_c_w_uq, c_w_ukv=v_c_w_ukv, g_out=v_g_out, w_out=v_w_out, g_final=v_g_final)
    big_names = ("w_in", "c_w_uq", "c_w_ukv", "w_out")

    inv_freq = 1.0 / (ROPE_THETA ** (jnp.arange(0, ROPE, 2, dtype=F32) / ROPE))
    ang = positions[0].astype(F32)[:, None] * inv_freq
    zpad = jnp.zeros((S, LANE - ROPE), F32)
    cos2 = jnp.concatenate([jnp.cos(ang), jnp.cos(ang), zpad], axis=1)
    sin2 = jnp.concatenate([jnp.sin(ang), jnp.sin(ang), zpad], axis=1)
    rot = _rope_matrix()

    for tree in (weights, mom_m, mom_v):
        for nm in TRANSPOSED:
            tree[nm] = jnp.swapaxes(tree[nm], 1, 2)

    def shards(l, names):
        return [[weights[nm][l].astype(BF16)] for nm in names]

    def assemble_rest(g_uq, g_ukv, g_wout):
        uq = jnp.pad(g_uq[0].reshape(cfg.Hc, LANE + ROPE, cfg.Q), ((0, 0), (0, LANE - ROPE), (0, 0)))
        return {"uq": uq.reshape(2 * cfg.C, cfg.Q), "ukv": _from_slots_cols(g_ukv[0]),
                "out": g_wout[0].reshape(cfg.DMIX, D) if cfg.rot_out else _perm_rows_out(cfg, g_wout[0].reshape(cfg.DMIX, D))}

    params = [_layer_params(cfg, l, g_pre, a_g_v, a_w_s, a_b_s, c_g_q, c_g_kv, g_out) for l in range(depth)]

    in_parts = [g[0] for g in _exchange(_GatherTwoLevel(shards(0, big_names[:1])), "gather_w_in_l0")]
    got_rest = None
    hcur, saved, Ws = x[0], [], []
    for l in range(depth):
        Ws.append({"in": _pad_w_in(cfg, in_parts)})
        if got_rest is not None:
            Ws[l].update(assemble_rest(*got_rest))
        nxt = l + 1 < depth
        early_rows = min(W_IN_EARLY_ROWS, weights["w_in"].shape[1] // 2)
        rest_rots = [0, 0, cfg.rot_out or 0]
        riding = ([] if got_rest is not None else shards(l, big_names[1:]))
        riding_rots = ([] if got_rest is not None else rest_rots) + ([0] if nxt else [])
        riding += [[weights["w_in"][l + 1][:early_rows].astype(BF16)]] if nxt else []
        in_parts = []

        def take(moved, l=l, rest_here=got_rest is None, nxt=nxt):
            if rest_here:
                Ws[l].update(assemble_rest(*moved[:3]))
            if nxt:
                in_parts.append(moved[-1][0])

        hcur, sv, got_late, got_rest = _layer_fwd(
            cfg, l, hcur, Ws[l], params[l], cos2, sin2, rot,
            carry_in=(_GatherTwoLevel(riding, riding_rots), take) if riding else None,
            carry_stick=_GatherTwoLevel([[weights["w_in"][l + 1][early_rows:].astype(BF16)]]) if nxt else None,
            carry_mla=_GatherTwoLevel(shards(l + 1, big_names[1:]), rest_rots) if nxt else None)
        in_parts += [g[0] for g in got_late]
        saved.append(sv)
    (dh,), (dg_final,), (loss_rows,) = _rowwise_vjp(
        _f_final, [(hcur, D, 0)], [(loss_target[0], D, 0)], [g_final[None]], [], [(jnp.ones((S, 1), F32), 1, 0)],
        [F32], 256, "final", primal=[(1, F32)])
    loss = lax.psum(jnp.sum(loss_rows), MESH_AXES)

    small_g, slots, pending = [None] * depth, [None] * depth, {}
    for l in reversed(range(depth)):
        ext_stick = [pending["w_in"]] if pending else []
        ext_mla = [pending["c_w_uq"], pending["c_w_ukv"]] if pending else []
        dh, small_g[l], slots[l], pending, ext_got = _layer_bwd(cfg, l, dh, saved[l], Ws[l], params[l], cos2, sin2, rot,
                                                                ext_stick, ext_mla, l == 0)
        if ext_got:
            slots[l + 1].update(w_in=slots[l + 1]["w_in"] + (ext_got[0],), c_w_uq=ext_got[1], c_w_ukv=ext_got[2])
    grad_x = dh[None]
    small_grads = {nm: jnp.stack([small_g[l][nm] for l in range(depth)]) for nm in SMALL if nm != "g_final"}
    small_grads["g_final"] = dg_final[0]
    (small_slots,) = _exchange(_GatherTwoLevel([[_pack_small(small_grads)]]), "gather_small_grads")

    res = {}
    for nm in big_names:
        res[nm] = None
        for l in reversed(range(depth)):
            parts = slots[l][nm] if isinstance(slots[l][nm], tuple) else (slots[l][nm],)
            col0 = 0
            for i, part in enumerate(parts):
                res[nm] = _adamw(part, weights[nm], mom_m[nm], mom_v[nm], l, res[nm], f"adamw_{nm}_l{l}_{i}", col0)
                col0 += part.shape[2]
        if nm in TRANSPOSED:
            res[nm] = [jnp.swapaxes(r, 1, 2) for r in res[nm]]
    packed = _adamw(small_slots[0], _pack_small(weights)[None], _pack_small(mom_m)[None], _pack_small(mom_v)[None], 0, None,
                    "adamw_small")
    small_res = [_unpack_small(r[0], weights) for r in packed]
    order = ("g_pre", "w_in", "a_g_v", "a_w_s", "a_b_s", "c_g_q", "c_g_kv", "c_w_uq", "c_w_ukv", "g_out", "w_out", "g_final")
    outs = [loss, grad_x]
    for kind in range(4):
        outs += [small_res[kind][nm] if nm in SMALL else res[nm][kind] for nm in order]
    return tuple(outs)
```
